```python
import math
import jax, jax.numpy as jnp
from jax import lax
import numpy as np

D_MODEL = 1024
BATCH = 2
SEQ = 16384
DEPTH = 4

MEM_LEN = 256
N_MIXERS = 2
MIX_WIDTH = 768
MEM_HEADS = 4
MEM_HEAD_DIM = 64
MEM_WIDTH = MEM_HEADS * MEM_HEAD_DIM
MIX_TOTAL = MIX_WIDTH + MEM_WIDTH
HGRN_EXPAND = 128
HGRN_HEADS = MIX_WIDTH // HGRN_EXPAND
HGRN_VDIM = MIX_WIDTH // HGRN_HEADS
HGRN_CHUNK = 64
MOBA_HEADS = 12
MOBA_HEAD_DIM = MIX_WIDTH // MOBA_HEADS
MOBA_BLOCK = 256
MOBA_TOPK = 3
MOBA_QBLOCK = 32
N_EXPERTS = 32
TOP_K = 4
D_FF = 1024
SWIGLU_ALPHA = 1.702
SWIGLU_LIMIT = 7.0
MOE_ROW_BLOCK = 512
DEEPNORM_ALPHA = (2 * DEPTH) ** 0.25
DEEPNORM_BETA = (8 * DEPTH) ** -0.25
LN_EPS = 1e-5
RMS_EPS = 1e-6
N_HGRN_LAYERS = (DEPTH + 1) // 2
N_MOBA_LAYERS = DEPTH // 2
HGRN_COLS = 4 * MIX_WIDTH + MEM_WIDTH
MOBA_COLS = 3 * MIX_WIDTH + MEM_WIDTH

kernel_name = 'hybrid_hgrn2_moba_memxattn_moe_deepnorm'


def _alibi_slope_list(n):
    def pow2(m):
        start = 2.0 ** (-(2.0 ** -(math.log2(m) - 3)))
        return [start ** (i + 1) for i in range(m)]
    if math.log2(n).is_integer():
        return pow2(n)
    c = 2 ** math.floor(math.log2(n))
    return pow2(c) + _alibi_slope_list(2 * c)[0::2][:n - c]


def alibi_slopes(n):
    return jnp.asarray(_alibi_slope_list(n), dtype=jnp.float32)


def layer_norm(x, g, b):
    xf = x.astype(jnp.float32)
    mu = jnp.mean(xf, axis=-1, keepdims=True)
    var = jnp.mean(jnp.square(xf - mu), axis=-1, keepdims=True)
    return ((xf - mu) * lax.rsqrt(var + LN_EPS) * g.astype(jnp.float32) + b.astype(jnp.float32)).astype(x.dtype)


def hgrn2_mixer(h, lb, norm_g):
    B, S, _ = h.shape
    dt = h.dtype
    q, f, i, g = jnp.split(h, 4, axis=-1)
    q = jax.nn.silu(q.astype(jnp.float32))
    forget = lb + (1.0 - lb) * jax.nn.sigmoid(f.astype(jnp.float32))
    k = 1.0 - forget
    logf = jnp.log(forget)
    nc = S // HGRN_CHUNK

    def to_chunks(t, d):
        return t.reshape(B, nc, HGRN_CHUNK, HGRN_HEADS, d).transpose(1, 0, 3, 2, 4)

    qs = to_chunks(q, HGRN_EXPAND)
    ks = to_chunks(k, HGRN_EXPAND)
    gs = to_chunks(logf, HGRN_EXPAND)
    vs = to_chunks(i.astype(jnp.float32), HGRN_VDIM)
    causal = jnp.tril(jnp.ones((HGRN_CHUNK, HGRN_CHUNK), dtype=bool))

    def step(state, inp):
        qc, kc, vc, gc = inp
        G = jnp.cumsum(gc, axis=2)
        diff = jnp.where(causal[:, :, None], G[:, :, :, None, :] - G[:, :, None, :, :], -jnp.inf)
        A = jnp.einsum('bhtd,bhsd,bhtsd->bhts', qc, kc, jnp.exp(diff))
        o = (jnp.einsum('bhts,bhsv->bhtv', A, vc)
             + jnp.einsum('bhtd,bhdv->bhtv', qc * jnp.exp(G), state))
        G_end = G[:, :, -1:, :]
        state = (jnp.exp(G_end[:, :, 0, :, None]) * state
                 + jnp.einsum('bhsd,bhsv->bhdv', kc * jnp.exp(G_end - G), vc))
        return state, o

    state0 = jnp.zeros((B, HGRN_HEADS, HGRN_EXPAND, HGRN_VDIM), jnp.float32)
    _, o = lax.scan(step, state0, (qs, ks, vs, gs))
    o = o.transpose(1, 0, 3, 2, 4).reshape(B, S, HGRN_HEADS, HGRN_VDIM)
    o = o * lax.rsqrt(jnp.mean(jnp.square(o), axis=-1, keepdims=True) + RMS_EPS) * norm_g.astype(jnp.float32)
    o = o.reshape(B, S, MIX_WIDTH) * jax.nn.sigmoid(g.astype(jnp.float32))
    return o.astype(dt)


def moba_attention(h):
    B, S, _ = h.shape
    H, dh, BLK = MOBA_HEADS, MOBA_HEAD_DIM, MOBA_BLOCK
    q, k, v = jnp.split(h, 3, axis=-1)
    q, k, v = (t.reshape(B, S, H, dh).transpose(0, 2, 1, 3) for t in (q, k, v))
    nb = -(-S // BLK)
    pad = nb * BLK - S
    k = jnp.pad(k, ((0, 0), (0, 0), (0, pad), (0, 0)))
    v = jnp.pad(v, ((0, 0), (0, 0), (0, pad), (0, 0)))
    k_blocks = k.reshape(B, H, nb, BLK, dh)
    v_blocks = v.reshape(B, H, nb, BLK, dh)
    k_mean = jnp.mean(k_blocks.astype(jnp.float32), axis=3)
    n_sel = min(MOBA_TOPK, nb)
    slopes = alibi_slopes(H)
    scale = dh ** -0.5
    b_idx = jnp.arange(B)[:, None, None, None]
    h_idx = jnp.arange(H)[None, :, None, None]
    blk_pos = jnp.arange(BLK)
    blk_ids = jnp.arange(nb)

    def one_query_block(c):
        t0 = c * MOBA_QBLOCK
        qc = lax.dynamic_slice_in_dim(q, t0, MOBA_QBLOCK, axis=2)
        t = t0 + jnp.arange(MOBA_QBLOCK)
        cur = t0 // BLK
        gate = jnp.einsum('bhqd,bhnd->bhqn', qc.astype(jnp.float32), k_mean)
        gate = jnp.where(blk_ids < cur, gate, -jnp.inf)
        _, sel = lax.top_k(gate, n_sel)
        valid = sel < cur
        k_sel = k_blocks[b_idx, h_idx, sel]
        v_sel = v_blocks[b_idx, h_idx, sel]
        s_sel = sel[..., None] * BLK + blk_pos
        lg_sel = (jnp.einsum('bhqd,bhqnkd->bhqnk', qc, k_sel).astype(jnp.float32) * scale
                  - slopes[:, None, None, None] * (t[:, None, None] - s_sel).astype(jnp.float32))
        lg_sel = jnp.where(valid[..., None], lg_sel, -jnp.inf)
        k_own = lax.dynamic_slice_in_dim(k, cur * BLK, BLK, axis=2)
        v_own = lax.dynamic_slice_in_dim(v, cur * BLK, BLK, axis=2)
        s_own = cur * BLK + blk_pos
        lg_own = (jnp.einsum('bhqd,bhkd->bhqk', qc, k_own).astype(jnp.float32) * scale
                  - slopes[:, None, None] * (t[:, None] - s_own[None, :]).astype(jnp.float32))
        lg_own = jnp.where(s_own[None, :] <= t[:, None], lg_own, -jnp.inf)
        logits = jnp.concatenate([lg_sel.reshape(B, H, MOBA_QBLOCK, n_sel * BLK), lg_own], axis=-1)
        p = jax.nn.softmax(logits, axis=-1).astype(v.dtype)
        p_sel = p[..., :n_sel * BLK].reshape(B, H, MOBA_QBLOCK, n_sel, BLK)
        p_own = p[..., n_sel * BLK:]
        return (jnp.einsum('bhqnk,bhqnkd->bhqd', p_sel, v_sel)
                + jnp.einsum('bhqk,bhkd->bhqd', p_own, v_own))

    out = lax.map(one_query_block, jnp.arange(S // MOBA_QBLOCK))
    return out.transpose(1, 0, 3, 2, 4).reshape(B, S, MIX_WIDTH)


def memory_attention(q_mem, mem, w_kv):
    B, S, _ = q_mem.shape
    q = q_mem.reshape(B, S, MEM_HEADS, MEM_HEAD_DIM)
    kv = mem @ w_kv
    km, vm = jnp.split(kv, 2, axis=-1)
    km = km.reshape(B, -1, MEM_HEADS, MEM_HEAD_DIM)
    vm = vm.reshape(B, -1, MEM_HEADS, MEM_HEAD_DIM)
    s = jnp.einsum('bshd,bmhd->bhsm', q, km).astype(jnp.float32) * (MEM_HEAD_DIM ** -0.5)
    p = jax.nn.softmax(s, axis=-1).astype(vm.dtype)
    return jnp.einsum('bhsm,bmhd->bshd', p, vm).reshape(B, S, MEM_WIDTH)


def moe_ffn(x, w_router, b_router, w_gate, b_gate, w_up, b_up, w_down, b_down):
    B, S, D = x.shape
    T = B * S
    TK = T * TOP_K
    xt = x.reshape(T, D)
    logits = (xt @ w_router + b_router).astype(jnp.float32)
    top_val, top_idx = lax.top_k(logits, TOP_K)
    gates = jax.nn.softmax(top_val, axis=-1)
    flat_e = top_idx.reshape(-1)
    order = jnp.argsort(flat_e)
    sorted_e = flat_e[order]
    sorted_tok = (order // TOP_K).astype(jnp.int32)
    sorted_w = gates.reshape(-1)[order]
    counts = jnp.bincount(flat_e, length=N_EXPERTS)
    starts = jnp.cumsum(counts) - counts
    padded = (counts + MOE_ROW_BLOCK - 1) // MOE_ROW_BLOCK * MOE_ROW_BLOCK
    padded_ends = jnp.cumsum(padded)
    padded_starts = padded_ends - padded
    dest = padded_starts[sorted_e] + jnp.arange(TK) - starts[sorted_e]
    n_blocks = -(-TK // MOE_ROW_BLOCK) + N_EXPERTS
    n_rows = n_blocks * MOE_ROW_BLOCK
    row_tok = jnp.zeros((n_rows,), jnp.int32).at[dest].set(sorted_tok)
    row_w = jnp.zeros((n_rows,), x.dtype).at[dest].set(sorted_w.astype(x.dtype))
    block_e = jnp.minimum(
        jnp.searchsorted(padded_ends, jnp.arange(n_blocks) * MOE_ROW_BLOCK, side='right'),
        N_EXPERTS - 1)

    def expert_block(args):
        tok, w, e = args
        xb = xt[tok]
        gate = jnp.minimum(xb @ w_gate[e] + b_gate[e], SWIGLU_LIMIT)
        up = jnp.clip(xb @ w_up[e] + b_up[e], -SWIGLU_LIMIT, SWIGLU_LIMIT)
        hid = gate * jax.nn.sigmoid(SWIGLU_ALPHA * gate) * (up + 1.0)
        return (hid @ w_down[e] + b_down[e]) * w[:, None]

    y_rows = lax.map(expert_block, (row_tok.reshape(n_blocks, MOE_ROW_BLOCK),
                                    row_w.reshape(n_blocks, MOE_ROW_BLOCK), block_e))
    y = jax.ops.segment_sum(y_rows.reshape(n_rows, D), row_tok, num_segments=T)
    return y.reshape(B, S, D).astype(x.dtype)


def setup_inputs(seed: int = 0) -> dict:
    key = jax.random.key(seed)
    ks = jax.random.split(key, 22)
    nrm = jax.random.normal
    f32 = jnp.float32
    d = D_MODEL
    return {
        'x': nrm(ks[0], (BATCH, SEQ, d), f32),
        'mem': nrm(ks[1], (BATCH, MEM_LEN, d), f32),
        'w_in_hgrn': nrm(ks[2], (N_HGRN_LAYERS, d, HGRN_COLS), f32) * d ** -0.5,
        'hgrn_lb_logits': nrm(ks[3], (N_HGRN_LAYERS, MIX_WIDTH), f32) * 0.5,
        'hgrn_norm_g': 1.0 + 0.02 * nrm(ks[4], (N_HGRN_LAYERS, HGRN_VDIM), f32),
        'w_in_moba': nrm(ks[5], (N_MOBA_LAYERS, d, MOBA_COLS), f32) * d ** -0.5,
        'w_mem_kv': nrm(ks[6], (DEPTH, d, 2 * MEM_WIDTH), f32) * d ** -0.5,
        'w_o': nrm(ks[7], (DEPTH, MIX_TOTAL, d), f32) * (MIX_TOTAL ** -0.5 * DEEPNORM_BETA),
        'ln_mix_g': 1.0 + 0.02 * nrm(ks[8], (DEPTH, d), f32),
        'ln_mix_b': 0.02 * nrm(ks[9], (DEPTH, d), f32),
        'w_router': nrm(ks[10], (DEPTH, d, N_EXPERTS), f32) * d ** -0.5,
        'b_router': 0.01 * nrm(ks[11], (DEPTH, N_EXPERTS), f32),
        'w_gate': nrm(ks[12], (DEPTH, N_EXPERTS, d, D_FF), f32) * d ** -0.5,
        'b_gate': 0.01 * nrm(ks[13], (DEPTH, N_EXPERTS, D_FF), f32),
        'w_up': nrm(ks[14], (DEPTH, N_EXPERTS, d, D_FF), f32) * d ** -0.5,
        'b_up': 0.01 * nrm(ks[15], (DEPTH, N_EXPERTS, D_FF), f32),
        'w_down': nrm(ks[16], (DEPTH, N_EXPERTS, D_FF, d), f32) * (D_FF ** -0.5 * DEEPNORM_BETA),
        'b_down': 0.01 * nrm(ks[17], (DEPTH, N_EXPERTS, d), f32),
        'ln_ffn_g': 1.0 + 0.02 * nrm(ks[18], (DEPTH, d), f32),
        'ln_ffn_b': 0.02 * nrm(ks[19], (DEPTH, d), f32),
    }


def reference(x, mem, w_in_hgrn, hgrn_lb_logits, hgrn_norm_g, w_in_moba, w_mem_kv, w_o,
              ln_mix_g, ln_mix_b, w_router, b_router, w_gate, b_gate, w_up, b_up,
              w_down, b_down, ln_ffn_g, ln_ffn_b):
    p_lb = jax.nn.softmax(hgrn_lb_logits.astype(jnp.float32), axis=0)
    lower_bounds = jnp.cumsum(p_lb, axis=0) - p_lb[0]
    for layer in range(DEPTH):
        j = layer // N_MIXERS
        if layer % N_MIXERS == 0:
            h = x @ w_in_hgrn[j]
            mix = hgrn2_mixer(h[..., :4 * MIX_WIDTH], lower_bounds[j], hgrn_norm_g[j])
        else:
            h = x @ w_in_moba[j]
            mix = moba_attention(h[..., :3 * MIX_WIDTH])
        mem_out = memory_attention(h[..., -MEM_WIDTH:], mem, w_mem_kv[layer])
        y = jnp.concatenate([mix, mem_out.astype(mix.dtype)], axis=-1) @ w_o[layer]
        x = layer_norm(DEEPNORM_ALPHA * x + y, ln_mix_g[layer], ln_mix_b[layer])
        f = moe_ffn(x, w_router[layer], b_router[layer], w_gate[layer], b_gate[layer],
                    w_up[layer], b_up[layer], w_down[layer], b_down[layer])
        x = layer_norm(DEEPNORM_ALPHA * x + f, ln_ffn_g[layer], ln_ffn_b[layer])
    return x
```

```python
import functools
import math

import jax
import jax.numpy as jnp
from jax import lax
from jax.experimental import pallas as pl
from jax.experimental.pallas import tpu as pltpu

MIX_WIDTH = 768
MEM_HEADS = 4
MEM_HEAD_DIM = 64
MEM_WIDTH = MEM_HEADS * MEM_HEAD_DIM
HGRN_HEADS = 6
HGRN_DK = 128
MOBA_HEADS = 12
MOBA_HEAD_DIM = 64
MOBA_BLOCK = 256
MOBA_TOPK = 3
N_EXPERTS = 32
TOP_K = 4
SWIGLU_ALPHA = 1.702
SWIGLU_LIMIT = 7.0
LN_EPS = 1e-5
RMS_EPS = 1e-6

LANES = 128
SUBLANES = 8
VMEM_LIMIT_BYTES = 56 * 1024 * 1024

INPROJ_ROWS = 512
HGRN_CHUNK = 64
HGRN_ROWS = 512
POST_ROWS = 256
MOE_ROWS = 256
COMBINE_ROWS = 128

BF16 = jnp.bfloat16
F32 = jnp.float32
NEG_BIG = -1e30

_NT = (((1,), (1,)), ((), ()))
_TN = (((0,), (0,)), ((), ()))


def _alibi_slope_list(n):
    def pow2(m):
        start = 2.0 ** (-(2.0 ** -(math.log2(m) - 3)))
        return [start ** (i + 1) for i in range(m)]
    if math.log2(n).is_integer():
        return pow2(n)
    c = 2 ** math.floor(math.log2(n))
    return pow2(c) + _alibi_slope_list(2 * c)[0::2][:n - c]


def _sigmoid(x):
    return 1.0 / (1.0 + jnp.exp(-x))


def _params(*sem):
    return pltpu.CompilerParams(dimension_semantics=sem, vmem_limit_bytes=VMEM_LIMIT_BYTES)


def _inproj_kernel(x_ref, w_ref, h_ref, *maybe_hb_ref):
    h = jnp.dot(x_ref[...].astype(BF16), w_ref[...], preferred_element_type=F32)
    h_ref[...] = h
    for hb_ref in maybe_hb_ref:
        hb_ref[...] = h.astype(BF16)


def _inproj(x2, w_bf16, emit_bf16):
    T, D = x2.shape
    N = w_bf16.shape[1]
    tm = INPROJ_ROWS
    out_shape = [jax.ShapeDtypeStruct((T, N), F32)]
    out_specs = [pl.BlockSpec((tm, N), lambda i: (i, 0))]
    if emit_bf16:
        out_shape.append(jax.ShapeDtypeStruct((T, N), BF16))
        out_specs.append(pl.BlockSpec((tm, N), lambda i: (i, 0)))
    return pl.pallas_call(
        _inproj_kernel,
        grid=(T // tm,),
        in_specs=[pl.BlockSpec((tm, D), lambda i: (i, 0)),
                  pl.BlockSpec((D, N), lambda i: (0, 0))],
        out_specs=out_specs,
        out_shape=out_shape,
        compiler_params=_params("parallel"),
        name="inproj",
    )(x2, w_bf16)


def _cumsum_rows(x, row):
    n = x.shape[0]
    sh = 1
    while sh < n:
        x = x + jnp.where(row >= sh, pltpu.roll(x, sh, 0), 0.0)
        sh *= 2
    return x


def _bcast_row(a, group, r):
    n = a.shape[0]
    a3 = a.reshape(n // group, group, LANES)
    return jnp.broadcast_to(a3[:, r:r + 1, :], a3.shape).reshape(n, LANES)


def _hgrn_chunk(qr, fr, v, gr, lb, ng, e_sum, st_t):
    C = qr.shape[0]
    row = lax.broadcasted_iota(jnp.int32, (C, LANES), 0)
    rr = lax.broadcasted_iota(jnp.int32, (C, C), 0)
    cc = lax.broadcasted_iota(jnp.int32, (C, C), 1)

    q = qr * _sigmoid(qr)
    forget = lb + (1.0 - lb) * _sigmoid(fr)
    k = 1.0 - forget
    G = _cumsum_rows(jnp.log(forget), row)

    parts = []
    for s in range(SUBLANES):
        Gs = _bcast_row(G, SUBLANES, s)
        ks = _bcast_row(k, SUBLANES, s)
        parts.append((q * ks * jnp.exp(jnp.minimum(G - Gs, 0.0))).astype(BF16))
    a_diag = jnp.dot(jnp.concatenate(parts, axis=1), e_sum, preferred_element_type=F32)
    A = jnp.where(((rr >> 3) == (cc >> 3)) & (cc <= rr), a_diag, 0.0)

    m = SUBLANES
    while m < C:
        lg = int(math.log2(m))
        Gr = _bcast_row(G, 2 * m, m - 1)
        second = ((row >> lg) & 1) == 1
        qm = jnp.where(second, q * jnp.exp(jnp.minimum(G - Gr, 0.0)), 0.0)
        km = jnp.where(second, 0.0, k * jnp.exp(jnp.minimum(Gr - G, 0.0)))
        am = lax.dot_general(qm.astype(BF16), km.astype(BF16), _NT, preferred_element_type=F32)
        A = A + jnp.where((rr >> (lg + 1)) == (cc >> (lg + 1)), am, 0.0)
        m *= 2

    vb = v.astype(BF16)
    o = jnp.dot(A.astype(BF16), vb, preferred_element_type=F32)
    o = o + lax.dot_general((q * jnp.exp(G)).astype(BF16), st_t.astype(BF16), _NT,
                            preferred_element_type=F32)
    g_end = G[C - 1:C, :]
    kd = (k * jnp.exp(g_end - G)).astype(BF16)
    st_new = st_t * jnp.exp(g_end) + lax.dot_general(vb, kd, _TN, preferred_element_type=F32)

    ms = jnp.mean(o * o, axis=-1, keepdims=True)
    out = o * lax.rsqrt(ms + RMS_EPS) * ng * _sigmoid(gr)
    return out, st_new


def _hgrn_kernel(q_ref, f_ref, i_ref, g_ref, lb_ref, ng_ref, e_ref, o_ref, st_ref, *, chunk):
    @pl.when(pl.program_id(2) == 0)
    def _():
        st_ref[...] = jnp.zeros_like(st_ref)

    lb = lb_ref[0]
    ng = ng_ref[...]
    e_sum = e_ref[...]
    n_chunks = q_ref.shape[1] // chunk
    for c in range(n_chunks):
        sl = pl.ds(c * chunk, chunk)
        out, st_new = _hgrn_chunk(q_ref[0, sl, :], f_ref[0, sl, :], i_ref[0, sl, :],
                                  g_ref[0, sl, :], lb, ng, e_sum, st_ref[...])
        st_ref[...] = st_new
        o_ref[0, sl, :] = out


def _hgrn_mixer(h3, lb, norm_g):
    B, S, _ = h3.shape
    ts = min(HGRN_ROWS, S)
    C = HGRN_CHUNK
    H = HGRN_HEADS
    e_sum = (jnp.arange(SUBLANES * LANES)[:, None] // LANES == jnp.arange(C)[None, :] % SUBLANES
             ).astype(BF16)
    col = lambda off: pl.BlockSpec((1, ts, LANES), lambda b, h, s, off=off: (b, s, off + h))
    return pl.pallas_call(
        functools.partial(_hgrn_kernel, chunk=C),
        grid=(B, H, S // ts),
        in_specs=[col(0), col(H), col(2 * H), col(3 * H),
                  pl.BlockSpec((1, 1, LANES), lambda b, h, s: (h, 0, 0)),
                  pl.BlockSpec((1, LANES), lambda b, h, s: (0, 0)),
                  pl.BlockSpec((SUBLANES * LANES, C), lambda b, h, s: (0, 0))],
        out_specs=pl.BlockSpec((1, ts, LANES), lambda b, h, s: (b, s, h)),
        out_shape=jax.ShapeDtypeStruct((B, S, MIX_WIDTH), F32),
        scratch_shapes=[pltpu.VMEM((HGRN_DK, HGRN_DK), F32)],
        compiler_params=_params("parallel", "parallel", "arbitrary"),
        name="hgrn",
    )(h3, h3, h3, h3, lb.reshape(H, 1, LANES), norm_g.reshape(1, LANES), e_sum)


def _moba_kernel(sl_ref, q_ref, kf_ref, kb_ref, vb_ref, o_ref, kmean_ref):
    BLK = MOBA_BLOCK
    p = pl.program_id(1)
    i = pl.program_id(2)

    @pl.when(i == 0)
    def _():
        kmean_ref[...] = jnp.zeros_like(kmean_ref)

    qf = q_ref[0]
    own = pl.ds(pl.multiple_of(i * BLK, BLK), BLK)
    k_own = kb_ref[0, own, :]
    v_own = vb_ref[0, own, :]

    lane = lax.broadcasted_iota(jnp.int32, (BLK, LANES), 1)
    rr = lax.broadcasted_iota(jnp.int32, (BLK, BLK), 0)
    cc = lax.broadcasted_iota(jnp.int32, (BLK, BLK), 1)
    rel = (rr - cc).astype(F32)
    scale = MOBA_HEAD_DIM ** -0.5

    outs = []
    for hh in range(2):
        head = (lane < MOBA_HEAD_DIM) if hh == 0 else (lane >= MOBA_HEAD_DIM)
        slope = sl_ref[2 * p + hh]
        qh = jnp.where(head, qf, 0.0)

        gate = lax.dot_general(qh, kmean_ref[...], _NT, precision=lax.Precision.HIGHEST,
                               preferred_element_type=F32)
        past = lane < i
        g = jnp.where(past, gate, -jnp.inf)
        sel = jnp.zeros((BLK, LANES), jnp.bool_)
        for _ in range(MOBA_TOPK):
            mx = jnp.max(g, axis=1, keepdims=True)
            idx = jnp.min(jnp.where(g == mx, lane, LANES), axis=1, keepdims=True)
            pick = lane == idx
            sel = sel | pick
            g = jnp.where(pick, -jnp.inf, g)
        sel_bias = jnp.where(sel & past, 0.0, NEG_BIG)
        q_aug = jnp.concatenate([(qh * scale).astype(BF16), sel_bias.astype(BF16)], axis=1)

        bias = -slope * rel
        s = lax.dot_general((qh * scale).astype(BF16), k_own, _NT, preferred_element_type=F32)
        s = s + jnp.where(cc <= rr, bias, -jnp.inf)
        m0 = jnp.max(s, axis=1, keepdims=True)
        p0 = jnp.exp(s - m0)
        l0 = jnp.sum(p0, axis=1, keepdims=True)
        acc0 = jnp.dot(p0.astype(BF16), v_own, preferred_element_type=F32)

        def body(n, carry):
            m, l, acc = carry
            blk = pl.ds(pl.multiple_of(n * BLK, BLK), BLK)
            onehot = jnp.where(lane == n, 1.0, 0.0).astype(BF16)
            k_aug = jnp.concatenate([kb_ref[0, blk, :], onehot], axis=1)
            sn = lax.dot_general(q_aug, k_aug, _NT, preferred_element_type=F32)
            sn = sn + (bias - slope * ((i - n) * BLK).astype(F32))
            m_new = jnp.maximum(m, jnp.max(sn, axis=1, keepdims=True))
            a = jnp.exp(m - m_new)
            pn = jnp.exp(sn - m_new)
            l = a * l + jnp.sum(pn, axis=1, keepdims=True)
            acc = a * acc + jnp.dot(pn.astype(BF16), vb_ref[0, blk, :],
                                    preferred_element_type=F32)
            return m_new, l, acc

        _, l, acc = lax.fori_loop(0, i, body, (m0, l0, acc0))
        outs.append(acc / l)

    o_ref[0] = jnp.where(lane < MOBA_HEAD_DIM, outs[0], outs[1])
    kmean_ref[pl.ds(i, 1), :] = jnp.mean(kf_ref[0], axis=0, keepdims=True)


def _moba_mixer(h3, hb3):
    B, S, _ = h3.shape
    BLK = MOBA_BLOCK
    NP = MOBA_HEADS // 2
    slopes = jnp.asarray(_alibi_slope_list(MOBA_HEADS), F32)
    grid_spec = pltpu.PrefetchScalarGridSpec(
        num_scalar_prefetch=1,
        grid=(B, NP, S // BLK),
        in_specs=[pl.BlockSpec((1, BLK, LANES), lambda b, p, i, sl: (b, i, p)),
                  pl.BlockSpec((1, BLK, LANES), lambda b, p, i, sl: (b, i, NP + p)),
                  pl.BlockSpec((1, S, LANES), lambda b, p, i, sl: (b, 0, NP + p)),
                  pl.BlockSpec((1, S, LANES), lambda b, p, i, sl: (b, 0, 2 * NP + p))],
        out_specs=pl.BlockSpec((1, BLK, LANES), lambda b, p, i, sl: (b, i, p)),
        scratch_shapes=[pltpu.VMEM((LANES, LANES), F32)],
    )
    return pl.pallas_call(
        _moba_kernel,
        grid_spec=grid_spec,
        out_shape=jax.ShapeDtypeStruct((B, S, MIX_WIDTH), F32),
        compiler_params=_params("parallel", "parallel", "arbitrary"),
        name="moba",
    )(slopes, h3, h3, hb3, hb3)


def _memkv_kernel(mem_ref, w_ref, kv_ref):
    kv_ref[0] = jnp.dot(mem_ref[0].astype(BF16), w_ref[...].astype(BF16),
                        preferred_element_type=F32).astype(BF16)


def _memkv(mem, w_kv):
    B, M, D = mem.shape
    N = w_kv.shape[1]
    return pl.pallas_call(
        _memkv_kernel,
        grid=(B,),
        in_specs=[pl.BlockSpec((1, M, D), lambda b: (b, 0, 0)),
                  pl.BlockSpec((D, N), lambda b: (0, 0))],
        out_specs=pl.BlockSpec((1, M, N), lambda b: (b, 0, 0)),
        out_shape=jax.ShapeDtypeStruct((B, M, N), BF16),
        compiler_params=_params("parallel"),
        name="memkv",
    )(mem, w_kv)


def _layer_norm(z, g, b):
    mu = jnp.mean(z, axis=-1, keepdims=True)
    zc = z - mu
    var = jnp.mean(zc * zc, axis=-1, keepdims=True)
    return zc * lax.rsqrt(var + LN_EPS) * g + b


def _post_kernel(x_ref, mix_ref, mq_ref, kv_ref, wo_ref, g_ref, b_ref, wr_ref, br_ref,
                 x1_ref, idx_ref, gate_ref, *, alpha):
    tm = x_ref.shape[0]
    mq = mq_ref[...]
    kv = kv_ref[0]
    km = kv[:, :MEM_WIDTH]
    vm = kv[:, MEM_WIDTH:]
    lane = lax.broadcasted_iota(jnp.int32, (tm, MEM_WIDTH), 1)
    scale = MEM_HEAD_DIM ** -0.5
    mo = jnp.zeros((tm, MEM_WIDTH), F32)
    for hd in range(MEM_HEADS):
        head = (lane >> 6) == hd
        qh = jnp.where(head, mq * scale, 0.0).astype(BF16)
        s = lax.dot_general(qh, km, _NT, preferred_element_type=F32)
        m = jnp.max(s, axis=1, keepdims=True)
        p = jnp.exp(s - m)
        l = jnp.sum(p, axis=1, keepdims=True)
        oh = jnp.dot(p.astype(BF16), vm, preferred_element_type=F32) / l
        mo = jnp.where(head, oh, mo)

    y = jnp.dot(mix_ref[...].astype(BF16), wo_ref[:MIX_WIDTH, :], preferred_element_type=F32)
    y = y + jnp.dot(mo.astype(BF16), wo_ref[MIX_WIDTH:, :], preferred_element_type=F32)
    x1 = _layer_norm(alpha * x_ref[...] + y, g_ref[...], b_ref[...])
    x1_ref[...] = x1

    logits = jnp.dot(x1, wr_ref[...], precision=lax.Precision.HIGHEST,
                     preferred_element_type=F32) + br_ref[...]
    lane_e = lax.broadcasted_iota(jnp.int32, (tm, LANES), 1)
    g = jnp.where(lane_e < N_EXPERTS, logits, -jnp.inf)
    idx_out = jnp.zeros((tm, LANES), jnp.int32)
    val_out = jnp.full((tm, LANES), -jnp.inf, F32)
    for kk in range(TOP_K):
        mx = jnp.max(g, axis=1, keepdims=True)
        idx = jnp.min(jnp.where(g == mx, lane_e, LANES), axis=1, keepdims=True)
        idx_out = jnp.where(lane_e == kk, idx, idx_out)
        val_out = jnp.where(lane_e == kk, mx, val_out)
        g = jnp.where(lane_e == idx, -jnp.inf, g)
    vmax = jnp.max(val_out, axis=1, keepdims=True)
    ev = jnp.exp(val_out - vmax)
    idx_ref[...] = idx_out
    gate_ref[...] = ev / jnp.sum(ev, axis=1, keepdims=True)


def _post_mixer(x2, mix2, h2, kv, w_o_bf16, ln_g, ln_b, w_router, b_router, seq_len, alpha):
    T, D = x2.shape
    tm = POST_ROWS
    N = h2.shape[1]
    M = kv.shape[1]
    mq_col = (N - MEM_WIDTH) // MEM_WIDTH
    tiles_per_seq = seq_len // tm
    wr = jnp.zeros((D, LANES), F32).at[:, :N_EXPERTS].set(w_router)
    br = jnp.zeros((1, LANES), F32).at[0, :N_EXPERTS].set(b_router)
    row = lambda n: pl.BlockSpec((tm, n), lambda i: (i, 0))
    full = lambda a, b: pl.BlockSpec((a, b), lambda i: (0, 0))
    return pl.pallas_call(
        functools.partial(_post_kernel, alpha=alpha),
        grid=(T // tm,),
        in_specs=[row(D), row(MIX_WIDTH),
                  pl.BlockSpec((tm, MEM_WIDTH), lambda i: (i, mq_col)),
                  pl.BlockSpec((1, M, 2 * MEM_WIDTH), lambda i: (i // tiles_per_seq, 0, 0)),
                  full(D, D), full(1, D), full(1, D), full(D, LANES), full(1, LANES)],
        out_specs=[row(D), row(LANES), row(LANES)],
        out_shape=[jax.ShapeDtypeStruct((T, D), F32),
                   jax.ShapeDtypeStruct((T, LANES), jnp.int32),
                   jax.ShapeDtypeStruct((T, LANES), F32)],
        compiler_params=_params("parallel"),
        name="post_mixer",
    )(x2, mix2, h2, kv, w_o_bf16, ln_g.reshape(1, D), ln_b.reshape(1, D), wr, br)


def _route(top_idx, n_tokens):
    rb = MOE_ROWS
    tk = n_tokens * TOP_K
    flat_e = top_idx.reshape(-1)
    onehot = (flat_e[:, None] == jnp.arange(N_EXPERTS, dtype=jnp.int32)[None, :]).astype(jnp.int32)
    csum = jnp.cumsum(onehot, axis=0)
    rank = jnp.sum(onehot * csum, axis=1) - 1
    counts = csum[-1]
    padded = (counts + rb - 1) // rb * rb
    pend = jnp.cumsum(padded)
    pstart = pend - padded
    dest = (pstart[flat_e] + rank).astype(jnp.int32)
    n_blocks = tk // rb + N_EXPERTS
    row_tok = jnp.zeros((n_blocks * rb,), jnp.int32).at[dest].set(
        jnp.arange(tk, dtype=jnp.int32) // TOP_K)
    block_e = jnp.minimum(
        jnp.searchsorted(pend, jnp.arange(n_blocks, dtype=jnp.int32) * rb, side='right'),
        N_EXPERTS - 1).astype(jnp.int32)
    n_used = (pend[-1] // rb).astype(jnp.int32).reshape(1)
    return dest, row_tok.reshape(n_blocks, 1, rb), block_e, n_used


def _expert_kernel(be_ref, nu_ref, tok_ref, x_hbm, wg_ref, bg_ref, wu_ref, bu_ref, wd_ref, bd_ref,
                   y_ref, xbuf, wgb, wub, wdb, sem):
    i = pl.program_id(0)
    rb = xbuf.shape[0]
    used = i < nu_ref[0]

    def row_copy(r):
        return pltpu.make_async_copy(x_hbm.at[pl.ds(tok_ref[0, 0, r], 1)],
                                     xbuf.at[pl.ds(r, 1)], sem)

    @pl.when(used)
    def _():
        def issue(r, c):
            row_copy(r).start()
            return c
        lax.fori_loop(0, rb, issue, 0)

    prev = be_ref[jnp.maximum(i - 1, 0)]

    @pl.when((i == 0) | (be_ref[i] != prev))
    def _():
        wgb[...] = wg_ref[0, 0].astype(BF16)
        wub[...] = wu_ref[0, 0].astype(BF16)
        wdb[...] = wd_ref[0, 0].astype(BF16)

    @pl.when(used)
    def _():
        def wait(r, c):
            row_copy(r).wait()
            return c
        lax.fori_loop(0, rb, wait, 0)
        xb = xbuf[...].astype(BF16)
        gate = jnp.dot(xb, wgb[...], preferred_element_type=F32) + bg_ref[0, 0]
        gate = jnp.minimum(gate, SWIGLU_LIMIT)
        up = jnp.dot(xb, wub[...], preferred_element_type=F32) + bu_ref[0, 0]
        up = jnp.clip(up, -SWIGLU_LIMIT, SWIGLU_LIMIT)
        hid = gate * _sigmoid(SWIGLU_ALPHA * gate) * (up + 1.0)
        y_ref[...] = jnp.dot(hid.astype(BF16), wdb[...], preferred_element_type=F32) + bd_ref[0, 0]

    @pl.when(jnp.logical_not(used))
    def _():
        y_ref[...] = jnp.zeros_like(y_ref)


def _experts(x1, row_tok, block_e, n_used, layer, w_gate, b_gate, w_up, b_up, w_down, b_down):
    T, D = x1.shape
    rb = MOE_ROWS
    n_blocks = row_tok.shape[0]
    E, _, F = w_gate.shape[1:]
    wspec = lambda a, b: pl.BlockSpec((1, 1, a, b), lambda i, be, nu: (layer, be[i], 0, 0))
    grid_spec = pltpu.PrefetchScalarGridSpec(
        num_scalar_prefetch=2,
        grid=(n_blocks,),
        in_specs=[pl.BlockSpec((1, 1, rb), lambda i, be, nu: (i, 0, 0),
                               memory_space=pltpu.SMEM),
                  pl.BlockSpec(memory_space=pl.ANY),
                  wspec(D, F), wspec(1, F), wspec(D, F), wspec(1, F), wspec(F, D), wspec(1, D)],
        out_specs=pl.BlockSpec((rb, D), lambda i, be, nu: (i, 0)),
        scratch_shapes=[pltpu.VMEM((rb, D), F32),
                        pltpu.VMEM((D, F), BF16), pltpu.VMEM((D, F), BF16),
                        pltpu.VMEM((F, D), BF16),
                        pltpu.SemaphoreType.DMA(())],
    )
    depth = w_gate.shape[0]
    return pl.pallas_call(
        _expert_kernel,
        grid_spec=grid_spec,
        out_shape=jax.ShapeDtypeStruct((n_blocks * rb, D), F32),
        compiler_params=_params("arbitrary"),
        name="experts",
    )(block_e, n_used, row_tok, x1, w_gate, b_gate.reshape(depth, E, 1, F),
      w_up, b_up.reshape(depth, E, 1, F), w_down, b_down.reshape(depth, E, 1, D))


def _combine_kernel(pos_ref, x1_ref, gate_ref, y_hbm, g_ref, b_ref, o_ref, buf, sem, *, alpha):
    tm = x1_ref.shape[0]
    n = TOP_K * tm

    def row_copy(r):
        return pltpu.make_async_copy(y_hbm.at[pl.ds(pos_ref[0, 0, r], 1)],
                                     buf.at[pl.ds(r, 1)], sem)

    def issue(r, c):
        row_copy(r).start()
        return c
    lax.fori_loop(0, n, issue, 0)

    def wait(r, c):
        row_copy(r).wait()
        return c
    lax.fori_loop(0, n, wait, 0)

    gates = gate_ref[...]
    f = jnp.zeros(x1_ref.shape, F32)
    for kk in range(TOP_K):
        f = f + gates[:, kk:kk + 1] * buf[kk * tm:(kk + 1) * tm, :]
    o_ref[...] = _layer_norm(alpha * x1_ref[...] + f, g_ref[...], b_ref[...])


def _combine(x1, gates, dest, y_rows, ln_g, ln_b, alpha):
    T, D = x1.shape
    tm = COMBINE_ROWS
    nt = T // tm
    pos = dest.reshape(nt, tm, TOP_K).transpose(0, 2, 1).reshape(nt, 1, TOP_K * tm)
    row = lambda n: pl.BlockSpec((tm, n), lambda i: (i, 0))
    full = lambda a, b: pl.BlockSpec((a, b), lambda i: (0, 0))
    return pl.pallas_call(
        functools.partial(_combine_kernel, alpha=alpha),
        grid=(nt,),
        in_specs=[pl.BlockSpec((1, 1, TOP_K * tm), lambda i: (i, 0, 0), memory_space=pltpu.SMEM),
                  row(D), row(LANES), pl.BlockSpec(memory_space=pl.ANY), full(1, D), full(1, D)],
        out_specs=row(D),
        out_shape=jax.ShapeDtypeStruct((T, D), F32),
        scratch_shapes=[pltpu.VMEM((TOP_K * tm, D), F32), pltpu.SemaphoreType.DMA(())],
        compiler_params=_params("arbitrary"),
        name="combine",
    )(pos, x1, gates, y_rows, ln_g.reshape(1, D), ln_b.reshape(1, D))


def kernel(x, mem, w_in_hgrn, hgrn_lb_logits, hgrn_norm_g, w_in_moba, w_mem_kv, w_o,
           ln_mix_g, ln_mix_b, w_router, b_router, w_gate, b_gate, w_up, b_up,
           w_down, b_down, ln_ffn_g, ln_ffn_b):
    B, S, D = x.shape
    T = B * S
    depth = w_o.shape[0]
    alpha = (2 * depth) ** 0.25

    p_lb = jax.nn.softmax(hgrn_lb_logits.astype(F32), axis=0)
    lower_bounds = jnp.cumsum(p_lb, axis=0) - p_lb[0]

    x2 = x.reshape(T, D)
    for layer in range(depth):
        j = layer // 2
        if layer % 2 == 0:
            (h2,) = _inproj(x2, w_in_hgrn[j].astype(BF16), emit_bf16=False)
            mix = _hgrn_mixer(h2.reshape(B, S, -1), lower_bounds[j], hgrn_norm_g[j])
        else:
            h2, hb2 = _inproj(x2, w_in_moba[j].astype(BF16), emit_bf16=True)
            mix = _moba_mixer(h2.reshape(B, S, -1), hb2.reshape(B, S, -1))
        kv = _memkv(mem, w_mem_kv[layer])
        x1, top_idx, gates = _post_mixer(
            x2, mix.reshape(T, MIX_WIDTH), h2, kv, w_o[layer].astype(BF16),
            ln_mix_g[layer], ln_mix_b[layer], w_router[layer], b_router[layer], S, alpha)
        dest, row_tok, block_e, n_used = _route(top_idx[:, :TOP_K], T)
        y_rows = _experts(x1, row_tok, block_e, n_used, layer,
                          w_gate, b_gate, w_up, b_up, w_down, b_down)
        x2 = _combine(x1, gates, dest, y_rows, ln_ffn_g[layer], ln_ffn_b[layer], alpha)
    return x2.reshape(B, S, D)
```

```python
import functools
import math

import jax
import jax.numpy as jnp
from jax import lax
from jax.experimental import pallas as pl
from jax.experimental.pallas import tpu as pltpu
from jax.experimental.pallas import tpu_sc as plsc

MIX_WIDTH = 768
MEM_HEADS = 4
MEM_HEAD_DIM = 64
MEM_WIDTH = MEM_HEADS * MEM_HEAD_DIM
HGRN_HEADS = 6
HGRN_DK = 128
MOBA_HEADS = 12
MOBA_HEAD_DIM = 64
MOBA_BLOCK = 256
MOBA_TOPK = 3
N_EXPERTS = 32
TOP_K = 4
SWIGLU_ALPHA = 1.702
SWIGLU_LIMIT = 7.0
LN_EPS = 1e-5
RMS_EPS = 1e-6

LANES = 128
SUBLANES = 8
VMEM_LIMIT_BYTES = 56 * 1024 * 1024

INPROJ_ROWS = 512
HGRN_CHUNK = 64
HGRN_ROWS = 512
POST_ROWS = 256
MOE_ROWS = 256
COMBINE_ROWS = 256
SC_WINDOW = 128
SC_IDX_ROWS = 8

BF16 = jnp.bfloat16
F32 = jnp.float32
NEG_BIG = -1e30

_NT = (((1,), (1,)), ((), ()))
_TN = (((0,), (0,)), ((), ()))


def _alibi_slope_list(n):
    def pow2(m):
        start = 2.0 ** (-(2.0 ** -(math.log2(m) - 3)))
        return [start ** (i + 1) for i in range(m)]
    if math.log2(n).is_integer():
        return pow2(n)
    c = 2 ** math.floor(math.log2(n))
    return pow2(c) + _alibi_slope_list(2 * c)[0::2][:n - c]


def _sigmoid(x):
    return 1.0 / (1.0 + jnp.exp(-x))


def _params(*sem):
    return pltpu.CompilerParams(dimension_semantics=sem, vmem_limit_bytes=VMEM_LIMIT_BYTES)


def _inproj_kernel(x_ref, w_ref, h_ref, *maybe_hb_ref):
    h = jnp.dot(x_ref[...].astype(BF16), w_ref[...], preferred_element_type=F32)
    h_ref[...] = h
    for hb_ref in maybe_hb_ref:
        hb_ref[...] = h.astype(BF16)


def _inproj(x2, w_bf16, emit_bf16):
    T, D = x2.shape
    N = w_bf16.shape[1]
    tm = INPROJ_ROWS
    out_shape = [jax.ShapeDtypeStruct((T, N), F32)]
    out_specs = [pl.BlockSpec((tm, N), lambda i: (i, 0))]
    if emit_bf16:
        out_shape.append(jax.ShapeDtypeStruct((T, N), BF16))
        out_specs.append(pl.BlockSpec((tm, N), lambda i: (i, 0)))
    return pl.pallas_call(
        _inproj_kernel,
        grid=(T // tm,),
        in_specs=[pl.BlockSpec((tm, D), lambda i: (i, 0)),
                  pl.BlockSpec((D, N), lambda i: (0, 0))],
        out_specs=out_specs,
        out_shape=out_shape,
        compiler_params=_params("parallel"),
        name="inproj",
    )(x2, w_bf16)


def _cumsum_rows(x, row):
    n = x.shape[0]
    sh = 1
    while sh < n:
        x = x + jnp.where(row >= sh, pltpu.roll(x, sh, 0), 0.0)
        sh *= 2
    return x


def _bcast_row(a, group, r):
    n = a.shape[0]
    a3 = a.reshape(n // group, group, LANES)
    return jnp.broadcast_to(a3[:, r:r + 1, :], a3.shape).reshape(n, LANES)


def _hgrn_chunk(qr, fr, v, gr, lb, ng, e_sum, st_t):
    C = qr.shape[0]
    row = lax.broadcasted_iota(jnp.int32, (C, LANES), 0)
    rr = lax.broadcasted_iota(jnp.int32, (C, C), 0)
    cc = lax.broadcasted_iota(jnp.int32, (C, C), 1)

    q = qr * _sigmoid(qr)
    forget = lb + (1.0 - lb) * _sigmoid(fr)
    k = 1.0 - forget
    G = _cumsum_rows(jnp.log(forget), row)

    parts = []
    for s in range(SUBLANES):
        Gs = _bcast_row(G, SUBLANES, s)
        ks = _bcast_row(k, SUBLANES, s)
        parts.append((q * ks * jnp.exp(jnp.minimum(G - Gs, 0.0))).astype(BF16))
    a_diag = jnp.dot(jnp.concatenate(parts, axis=1), e_sum, preferred_element_type=F32)
    A = jnp.where(((rr >> 3) == (cc >> 3)) & (cc <= rr), a_diag, 0.0)

    m = SUBLANES
    while m < C:
        lg = int(math.log2(m))
        Gr = _bcast_row(G, 2 * m, m - 1)
        second = ((row >> lg) & 1) == 1
        qm = jnp.where(second, q * jnp.exp(jnp.minimum(G - Gr, 0.0)), 0.0)
        km = jnp.where(second, 0.0, k * jnp.exp(jnp.minimum(Gr - G, 0.0)))
        am = lax.dot_general(qm.astype(BF16), km.astype(BF16), _NT, preferred_element_type=F32)
        A = A + jnp.where((rr >> (lg + 1)) == (cc >> (lg + 1)), am, 0.0)
        m *= 2

    vb = v.astype(BF16)
    o = jnp.dot(A.astype(BF16), vb, preferred_element_type=F32)
    o = o + lax.dot_general((q * jnp.exp(G)).astype(BF16), st_t.astype(BF16), _NT,
                            preferred_element_type=F32)
    g_end = G[C - 1:C, :]
    kd = (k * jnp.exp(g_end - G)).astype(BF16)
    st_new = st_t * jnp.exp(g_end) + lax.dot_general(vb, kd, _TN, preferred_element_type=F32)

    ms = jnp.mean(o * o, axis=-1, keepdims=True)
    out = o * lax.rsqrt(ms + RMS_EPS) * ng * _sigmoid(gr)
    return out, st_new


def _hgrn_kernel(q_ref, f_ref, i_ref, g_ref, lb_ref, ng_ref, e_ref, o_ref, st_ref, *, chunk):
    @pl.when(pl.program_id(2) == 0)
    def _():
        st_ref[...] = jnp.zeros_like(st_ref)

    lb = lb_ref[0]
    ng = ng_ref[...]
    e_sum = e_ref[...]
    n_chunks = q_ref.shape[1] // chunk
    for c in range(n_chunks):
        sl = pl.ds(c * chunk, chunk)
        out, st_new = _hgrn_chunk(q_ref[0, sl, :], f_ref[0, sl, :], i_ref[0, sl, :],
                                  g_ref[0, sl, :], lb, ng, e_sum, st_ref[...])
        st_ref[...] = st_new
        o_ref[0, sl, :] = out


def _hgrn_mixer(h3, lb, norm_g):
    B, S, _ = h3.shape
    ts = min(HGRN_ROWS, S)
    C = HGRN_CHUNK
    H = HGRN_HEADS
    e_sum = (jnp.arange(SUBLANES * LANES)[:, None] // LANES == jnp.arange(C)[None, :] % SUBLANES
             ).astype(BF16)
    col = lambda off: pl.BlockSpec((1, ts, LANES), lambda b, h, s, off=off: (b, s, off + h))
    return pl.pallas_call(
        functools.partial(_hgrn_kernel, chunk=C),
        grid=(B, H, S // ts),
        in_specs=[col(0), col(H), col(2 * H), col(3 * H),
                  pl.BlockSpec((1, 1, LANES), lambda b, h, s: (h, 0, 0)),
                  pl.BlockSpec((1, LANES), lambda b, h, s: (0, 0)),
                  pl.BlockSpec((SUBLANES * LANES, C), lambda b, h, s: (0, 0))],
        out_specs=pl.BlockSpec((1, ts, LANES), lambda b, h, s: (b, s, h)),
        out_shape=jax.ShapeDtypeStruct((B, S, MIX_WIDTH), F32),
        scratch_shapes=[pltpu.VMEM((HGRN_DK, HGRN_DK), F32)],
        compiler_params=_params("parallel", "parallel", "arbitrary"),
        name="hgrn",
    )(h3, h3, h3, h3, lb.reshape(H, 1, LANES), norm_g.reshape(1, LANES), e_sum)


def _moba_kernel(sl_ref, q_ref, kf_ref, kb_ref, vb_ref, o_ref, kmean_ref):
    BLK = MOBA_BLOCK
    p = pl.program_id(1)
    i = pl.program_id(2)

    @pl.when(i == 0)
    def _():
        kmean_ref[...] = jnp.zeros_like(kmean_ref)

    qf = q_ref[0]
    own = pl.ds(pl.multiple_of(i * BLK, BLK), BLK)
    k_own = kb_ref[0, own, :]
    v_own = vb_ref[0, own, :]

    lane = lax.broadcasted_iota(jnp.int32, (BLK, LANES), 1)
    rr = lax.broadcasted_iota(jnp.int32, (BLK, BLK), 0)
    cc = lax.broadcasted_iota(jnp.int32, (BLK, BLK), 1)
    rel = (rr - cc).astype(F32)
    scale = MOBA_HEAD_DIM ** -0.5

    outs = []
    for hh in range(2):
        head = (lane < MOBA_HEAD_DIM) if hh == 0 else (lane >= MOBA_HEAD_DIM)
        slope = sl_ref[2 * p + hh]
        qh = jnp.where(head, qf, 0.0)

        gate = lax.dot_general(qh, kmean_ref[...], _NT, precision=lax.Precision.HIGHEST,
                               preferred_element_type=F32)
        past = lane < i
        g = jnp.where(past, gate, -jnp.inf)
        sel = jnp.zeros((BLK, LANES), jnp.bool_)
        for _ in range(MOBA_TOPK):
            mx = jnp.max(g, axis=1, keepdims=True)
            idx = jnp.min(jnp.where(g == mx, lane, LANES), axis=1, keepdims=True)
            pick = lane == idx
            sel = sel | pick
            g = jnp.where(pick, -jnp.inf, g)
        sel_bias = jnp.where(sel & past, 0.0, NEG_BIG)
        q_aug = jnp.concatenate([(qh * scale).astype(BF16), sel_bias.astype(BF16)], axis=1)

        bias = -slope * rel
        s = lax.dot_general((qh * scale).astype(BF16), k_own, _NT, preferred_element_type=F32)
        s = s + jnp.where(cc <= rr, bias, -jnp.inf)
        m0 = jnp.max(s, axis=1, keepdims=True)
        p0 = jnp.exp(s - m0)
        l0 = jnp.sum(p0, axis=1, keepdims=True)
        acc0 = jnp.dot(p0.astype(BF16), v_own, preferred_element_type=F32)

        def body(n, carry):
            m, l, acc = carry
            blk = pl.ds(pl.multiple_of(n * BLK, BLK), BLK)
            onehot = jnp.where(lane == n, 1.0, 0.0).astype(BF16)
            k_aug = jnp.concatenate([kb_ref[0, blk, :], onehot], axis=1)
            sn = lax.dot_general(q_aug, k_aug, _NT, preferred_element_type=F32)
            sn = sn + (bias - slope * ((i - n) * BLK).astype(F32))
            m_new = jnp.maximum(m, jnp.max(sn, axis=1, keepdims=True))
            a = jnp.exp(m - m_new)
            pn = jnp.exp(sn - m_new)
            l = a * l + jnp.sum(pn, axis=1, keepdims=True)
            acc = a * acc + jnp.dot(pn.astype(BF16), vb_ref[0, blk, :],
                                    preferred_element_type=F32)
            return m_new, l, acc

        _, l, acc = lax.fori_loop(0, i, body, (m0, l0, acc0))
        outs.append(acc / l)

    o_ref[0] = jnp.where(lane < MOBA_HEAD_DIM, outs[0], outs[1])
    kmean_ref[pl.ds(i, 1), :] = jnp.mean(kf_ref[0], axis=0, keepdims=True)


def _moba_mixer(h3, hb3):
    B, S, _ = h3.shape
    BLK = MOBA_BLOCK
    NP = MOBA_HEADS // 2
    slopes = jnp.asarray(_alibi_slope_list(MOBA_HEADS), F32)
    grid_spec = pltpu.PrefetchScalarGridSpec(
        num_scalar_prefetch=1,
        grid=(B, NP, S // BLK),
        in_specs=[pl.BlockSpec((1, BLK, LANES), lambda b, p, i, sl: (b, i, p)),
                  pl.BlockSpec((1, BLK, LANES), lambda b, p, i, sl: (b, i, NP + p)),
                  pl.BlockSpec((1, S, LANES), lambda b, p, i, sl: (b, 0, NP + p)),
                  pl.BlockSpec((1, S, LANES), lambda b, p, i, sl: (b, 0, 2 * NP + p))],
        out_specs=pl.BlockSpec((1, BLK, LANES), lambda b, p, i, sl: (b, i, p)),
        scratch_shapes=[pltpu.VMEM((LANES, LANES), F32)],
    )
    return pl.pallas_call(
        _moba_kernel,
        grid_spec=grid_spec,
        out_shape=jax.ShapeDtypeStruct((B, S, MIX_WIDTH), F32),
        compiler_params=_params("parallel", "parallel", "arbitrary"),
        name="moba",
    )(slopes, h3, h3, hb3, hb3)


def _memkv_kernel(mem_ref, w_ref, kv_ref):
    kv_ref[0] = jnp.dot(mem_ref[0].astype(BF16), w_ref[...].astype(BF16),
                        preferred_element_type=F32).astype(BF16)


def _memkv(mem, w_kv):
    B, M, D = mem.shape
    N = w_kv.shape[1]
    return pl.pallas_call(
        _memkv_kernel,
        grid=(B,),
        in_specs=[pl.BlockSpec((1, M, D), lambda b: (b, 0, 0)),
                  pl.BlockSpec((D, N), lambda b: (0, 0))],
        out_specs=pl.BlockSpec((1, M, N), lambda b: (b, 0, 0)),
        out_shape=jax.ShapeDtypeStruct((B, M, N), BF16),
        compiler_params=_params("parallel"),
        name="memkv",
    )(mem, w_kv)


def _layer_norm(z, g, b):
    mu = jnp.mean(z, axis=-1, keepdims=True)
    zc = z - mu
    var = jnp.mean(zc * zc, axis=-1, keepdims=True)
    return zc * lax.rsqrt(var + LN_EPS) * g + b


def _post_kernel(x_ref, mix_ref, mq_ref, kv_ref, wo_ref, g_ref, b_ref, wr_ref, br_ref,
                 x1_ref, idx_ref, gate_ref, *, alpha):
    tm = x_ref.shape[0]
    mq = mq_ref[...]
    kv = kv_ref[0]
    km = kv[:, :MEM_WIDTH]
    vm = kv[:, MEM_WIDTH:]
    lane = lax.broadcasted_iota(jnp.int32, (tm, MEM_WIDTH), 1)
    scale = MEM_HEAD_DIM ** -0.5
    mo = jnp.zeros((tm, MEM_WIDTH), F32)
    for hd in range(MEM_HEADS):
        head = (lane >> 6) == hd
        qh = jnp.where(head, mq * scale, 0.0).astype(BF16)
        s = lax.dot_general(qh, km, _NT, preferred_element_type=F32)
        m = jnp.max(s, axis=1, keepdims=True)
        p = jnp.exp(s - m)
        l = jnp.sum(p, axis=1, keepdims=True)
        oh = jnp.dot(p.astype(BF16), vm, preferred_element_type=F32) / l
        mo = jnp.where(head, oh, mo)

    y = jnp.dot(mix_ref[...].astype(BF16), wo_ref[:MIX_WIDTH, :], preferred_element_type=F32)
    y = y + jnp.dot(mo.astype(BF16), wo_ref[MIX_WIDTH:, :], preferred_element_type=F32)
    x1 = _layer_norm(alpha * x_ref[...] + y, g_ref[...], b_ref[...])
    x1_ref[...] = x1

    logits = jnp.dot(x1, wr_ref[...], precision=lax.Precision.HIGHEST,
                     preferred_element_type=F32) + br_ref[...]
    lane_e = lax.broadcasted_iota(jnp.int32, (tm, LANES), 1)
    g = jnp.where(lane_e < N_EXPERTS, logits, -jnp.inf)
    idx_out = jnp.zeros((tm, LANES), jnp.int32)
    val_out = jnp.full((tm, LANES), -jnp.inf, F32)
    for kk in range(TOP_K):
        mx = jnp.max(g, axis=1, keepdims=True)
        idx = jnp.min(jnp.where(g == mx, lane_e, LANES), axis=1, keepdims=True)
        idx_out = jnp.where(lane_e == kk, idx, idx_out)
        val_out = jnp.where(lane_e == kk, mx, val_out)
        g = jnp.where(lane_e == idx, -jnp.inf, g)
    vmax = jnp.max(val_out, axis=1, keepdims=True)
    ev = jnp.exp(val_out - vmax)
    idx_ref[...] = idx_out
    gate_ref[...] = ev / jnp.sum(ev, axis=1, keepdims=True)


def _post_mixer(x2, mix2, h2, kv, w_o_bf16, ln_g, ln_b, w_router, b_router, seq_len, alpha):
    T, D = x2.shape
    tm = POST_ROWS
    N = h2.shape[1]
    M = kv.shape[1]
    mq_col = (N - MEM_WIDTH) // MEM_WIDTH
    tiles_per_seq = seq_len // tm
    wr = jnp.zeros((D, LANES), F32).at[:, :N_EXPERTS].set(w_router)
    br = jnp.zeros((1, LANES), F32).at[0, :N_EXPERTS].set(b_router)
    row = lambda n: pl.BlockSpec((tm, n), lambda i: (i, 0))
    full = lambda a, b: pl.BlockSpec((a, b), lambda i: (0, 0))
    return pl.pallas_call(
        functools.partial(_post_kernel, alpha=alpha),
        grid=(T // tm,),
        in_specs=[row(D), row(MIX_WIDTH),
                  pl.BlockSpec((tm, MEM_WIDTH), lambda i: (i, mq_col)),
                  pl.BlockSpec((1, M, 2 * MEM_WIDTH), lambda i: (i // tiles_per_seq, 0, 0)),
                  full(D, D), full(1, D), full(1, D), full(D, LANES), full(1, LANES)],
        out_specs=[row(D), row(LANES), row(LANES)],
        out_shape=[jax.ShapeDtypeStruct((T, D), F32),
                   jax.ShapeDtypeStruct((T, LANES), jnp.int32),
                   jax.ShapeDtypeStruct((T, LANES), F32)],
        compiler_params=_params("parallel"),
        name="post_mixer",
    )(x2, mix2, h2, kv, w_o_bf16, ln_g.reshape(1, D), ln_b.reshape(1, D), wr, br)


def _sc_mesh():
    return plsc.VectorSubcoreMesh(core_axis_name="core", subcore_axis_name="subcore")


def _sc_scatter_rows(rows, idx, n_rep, n_out):
    R, W = rows.shape

    @functools.partial(pl.kernel, out_type=jax.ShapeDtypeStruct((n_out, W), rows.dtype),
                       mesh=_sc_mesh(), scratch_types=[])
    def scatter(x_hbm, i_hbm, o_hbm):
        def body(x_vmem, i_vmem):
            for r in range(n_rep):
                pltpu.sync_copy(x_vmem, o_hbm.at[i_vmem.at[r]])

        pltpu.emit_pipeline(
            body, grid=(R // SC_WINDOW,),
            in_specs=[pl.BlockSpec((SC_WINDOW, W), lambda i: (i, 0)),
                      pl.BlockSpec((SC_IDX_ROWS, SC_WINDOW), lambda i: (0, i))],
            out_specs=[], core_axis_name=("core", "subcore"),
            dimension_semantics=(pltpu.PARALLEL,))(x_hbm, i_hbm)

    return scatter(rows, idx)


def _sc_gather_rows(table, idx):
    n = idx.shape[0]
    W = table.shape[1]

    @functools.partial(pl.kernel, out_type=jax.ShapeDtypeStruct((n, W), table.dtype),
                       mesh=_sc_mesh(), scratch_types=[])
    def gather(t_hbm, i_hbm, o_hbm):
        def body(i_vmem, o_vmem):
            pltpu.sync_copy(t_hbm.at[i_vmem.at[0]], o_vmem)

        pltpu.emit_pipeline(
            body, grid=(n // SC_WINDOW,),
            in_specs=[pl.BlockSpec((1, SC_WINDOW), lambda i: (0, i))],
            out_specs=[pl.BlockSpec((SC_WINDOW, W), lambda i: (i, 0))],
            core_axis_name=("core", "subcore"),
            dimension_semantics=(pltpu.PARALLEL,))(i_hbm, o_hbm)

    return gather(table, idx.reshape(1, n))


def _route(top_idx, n_tokens):
    rb = MOE_ROWS
    tk = n_tokens * TOP_K
    flat_e = top_idx.reshape(-1)
    onehot = (flat_e[:, None] == jnp.arange(N_EXPERTS, dtype=jnp.int32)[None, :]).astype(jnp.int32)
    csum = jnp.cumsum(onehot, axis=0)
    rank = jnp.sum(onehot * csum, axis=1) - 1
    counts = csum[-1]
    padded = (counts + rb - 1) // rb * rb
    pend = jnp.cumsum(padded)
    pstart = pend - padded
    dest = (pstart[flat_e] + rank).astype(jnp.int32).reshape(n_tokens, TOP_K)
    n_blocks = tk // rb + N_EXPERTS
    block_e = jnp.minimum(
        jnp.searchsorted(pend, jnp.arange(n_blocks, dtype=jnp.int32) * rb, side='right'),
        N_EXPERTS - 1).astype(jnp.int32)
    n_used = (pend[-1] // rb).astype(jnp.int32).reshape(1)
    return dest, block_e, n_used


def _dispatch(x1, dest, n_rows):
    T, D = x1.shape
    sub = D // LANES
    c = jnp.arange(sub, dtype=jnp.int32)
    idx = (dest.T[:, :, None] * sub + c[None, None, :]).reshape(TOP_K, T * sub)
    idx = jnp.concatenate([idx, jnp.zeros((SC_IDX_ROWS - TOP_K, T * sub), jnp.int32)], axis=0)
    xs = _sc_scatter_rows(x1.reshape(T * sub, LANES), idx, TOP_K, n_rows * sub)
    return xs.reshape(n_rows, D)


def _expert_kernel(be_ref, nu_ref, x_ref, wg_ref, bg_ref, wu_ref, bu_ref, wd_ref, bd_ref,
                   y_ref, wgb, wub, wdb):
    i = pl.program_id(0)
    prev = be_ref[jnp.maximum(i - 1, 0)]

    @pl.when((i == 0) | (be_ref[i] != prev))
    def _():
        wgb[...] = wg_ref[0, 0].astype(BF16)
        wub[...] = wu_ref[0, 0].astype(BF16)
        wdb[...] = wd_ref[0, 0].astype(BF16)

    @pl.when(i < nu_ref[0])
    def _():
        xb = x_ref[...].astype(BF16)
        gate = jnp.dot(xb, wgb[...], preferred_element_type=F32) + bg_ref[0, 0]
        gate = jnp.minimum(gate, SWIGLU_LIMIT)
        up = jnp.dot(xb, wub[...], preferred_element_type=F32) + bu_ref[0, 0]
        up = jnp.clip(up, -SWIGLU_LIMIT, SWIGLU_LIMIT)
        hid = gate * _sigmoid(SWIGLU_ALPHA * gate) * (up + 1.0)
        y_ref[...] = jnp.dot(hid.astype(BF16), wdb[...], preferred_element_type=F32) + bd_ref[0, 0]


def _experts(xs, block_e, n_used, layer, w_gate, b_gate, w_up, b_up, w_down, b_down):
    D = xs.shape[1]
    rb = MOE_ROWS
    n_blocks = block_e.shape[0]
    E, _, F = w_gate.shape[1:]
    wspec = lambda a, b: pl.BlockSpec((1, 1, a, b), lambda i, be, nu: (layer, be[i], 0, 0))
    live = lambda i, be, nu: (jnp.where(i < nu[0], i, n_blocks), 0)
    grid_spec = pltpu.PrefetchScalarGridSpec(
        num_scalar_prefetch=2,
        grid=(n_blocks,),
        in_specs=[pl.BlockSpec((rb, D), live),
                  wspec(D, F), wspec(1, F), wspec(D, F), wspec(1, F), wspec(F, D), wspec(1, D)],
        out_specs=pl.BlockSpec((rb, D), live),
        scratch_shapes=[pltpu.VMEM((D, F), BF16), pltpu.VMEM((D, F), BF16),
                        pltpu.VMEM((F, D), BF16)],
    )
    depth = w_gate.shape[0]
    return pl.pallas_call(
        _expert_kernel,
        grid_spec=grid_spec,
        out_shape=jax.ShapeDtypeStruct(xs.shape, F32),
        compiler_params=_params("arbitrary"),
        name="experts",
    )(block_e, n_used, xs, w_gate, b_gate.reshape(depth, E, 1, F),
      w_up, b_up.reshape(depth, E, 1, F), w_down, b_down.reshape(depth, E, 1, D))


def _combine_kernel(x1_ref, gate_ref, y_ref, g_ref, b_ref, o_ref, *, alpha):
    D = x1_ref.shape[1]
    gates = gate_ref[...]
    f = jnp.zeros(x1_ref.shape, F32)
    for kk in range(TOP_K):
        f = f + gates[:, kk:kk + 1] * y_ref[:, kk * D:(kk + 1) * D]
    o_ref[...] = _layer_norm(alpha * x1_ref[...] + f, g_ref[...], b_ref[...])


def _combine(x1, gates, dest, y_rows, ln_g, ln_b, alpha):
    T, D = x1.shape
    sub = D // LANES
    tm = COMBINE_ROWS
    c = jnp.arange(sub, dtype=jnp.int32)
    idx = (dest[:, :, None] * sub + c[None, None, :]).reshape(T * TOP_K * sub)
    yg = _sc_gather_rows(y_rows.reshape(-1, LANES), idx).reshape(T, TOP_K * D)
    row = lambda n: pl.BlockSpec((tm, n), lambda i: (i, 0))
    full = lambda a, b: pl.BlockSpec((a, b), lambda i: (0, 0))
    return pl.pallas_call(
        functools.partial(_combine_kernel, alpha=alpha),
        grid=(T // tm,),
        in_specs=[row(D), row(LANES), row(TOP_K * D), full(1, D), full(1, D)],
        out_specs=row(D),
        out_shape=jax.ShapeDtypeStruct((T, D), F32),
        compiler_params=_params("parallel"),
        name="combine",
    )(x1, gates, yg, ln_g.reshape(1, D), ln_b.reshape(1, D))


def kernel(x, mem, w_in_hgrn, hgrn_lb_logits, hgrn_norm_g, w_in_moba, w_mem_kv, w_o,
           ln_mix_g, ln_mix_b, w_router, b_router, w_gate, b_gate, w_up, b_up,
           w_down, b_down, ln_ffn_g, ln_ffn_b):
    B, S, D = x.shape
    T = B * S
    depth = w_o.shape[0]
    alpha = (2 * depth) ** 0.25

    p_lb = jax.nn.softmax(hgrn_lb_logits.astype(F32), axis=0)
    lower_bounds = jnp.cumsum(p_lb, axis=0) - p_lb[0]

    x2 = x.reshape(T, D)
    for layer in range(depth):
        j = layer // 2
        if layer % 2 == 0:
            (h2,) = _inproj(x2, w_in_hgrn[j].astype(BF16), emit_bf16=False)
            mix = _hgrn_mixer(h2.reshape(B, S, -1), lower_bounds[j], hgrn_norm_g[j])
        else:
            h2, hb2 = _inproj(x2, w_in_moba[j].astype(BF16), emit_bf16=True)
            mix = _moba_mixer(h2.reshape(B, S, -1), hb2.reshape(B, S, -1))
        kv = _memkv(mem, w_mem_kv[layer])
        x1, top_idx, gates = _post_mixer(
            x2, mix.reshape(T, MIX_WIDTH), h2, kv, w_o[layer].astype(BF16),
            ln_mix_g[layer], ln_mix_b[layer], w_router[layer], b_router[layer], S, alpha)
        dest, block_e, n_used = _route(top_idx[:, :TOP_K], T)
        xs = _dispatch(x1, dest, (block_e.shape[0] + 1) * MOE_ROWS)
        y_rows = _experts(xs, block_e, n_used, layer, w_gate, b_gate, w_up, b_up, w_down, b_down)
        x2 = _combine(x1, gates, dest, y_rows, ln_ffn_g[layer], ln_ffn_b[layer], alpha)
    return x2.reshape(B, S, D)
```

```python
import functools
import math

import jax
import jax.numpy as jnp
from jax import lax
from jax.experimental import pallas as pl
from jax.experimental.pallas import tpu as pltpu
from jax.experimental.pallas import tpu_sc as plsc

MIX_WIDTH = 768
MEM_HEADS = 4
MEM_HEAD_DIM = 64
MEM_WIDTH = MEM_HEADS * MEM_HEAD_DIM
HGRN_HEADS = 6
HGRN_DK = 128
MOBA_HEADS = 12
MOBA_HEAD_DIM = 64
MOBA_BLOCK = 256
MOBA_TOPK = 3
N_EXPERTS = 32
TOP_K = 4
SWIGLU_ALPHA = 1.702
SWIGLU_LIMIT = 7.0
LN_EPS = 1e-5
RMS_EPS = 1e-6

LANES = 128
SUBLANES = 8
VMEM_LIMIT_BYTES = 56 * 1024 * 1024

INPROJ_ROWS = 512
HGRN_CHUNK = 64
HGRN_ROWS = 512
POST_ROWS = 256
MOE_ROWS = 256
COMBINE_ROWS = 256
SC_WINDOW = 128
SC_IDX_ROWS = 8

BF16 = jnp.bfloat16
F32 = jnp.float32
NEG_BIG = -1e30

_NT = (((1,), (1,)), ((), ()))
_TN = (((0,), (0,)), ((), ()))


def _alibi_slope_list(n):
    def pow2(m):
        start = 2.0 ** (-(2.0 ** -(math.log2(m) - 3)))
        return [start ** (i + 1) for i in range(m)]
    if math.log2(n).is_integer():
        return pow2(n)
    c = 2 ** math.floor(math.log2(n))
    return pow2(c) + _alibi_slope_list(2 * c)[0::2][:n - c]


def _sigmoid(x):
    return 1.0 / (1.0 + jnp.exp(-x))


def _params(*sem):
    return pltpu.CompilerParams(dimension_semantics=sem, vmem_limit_bytes=VMEM_LIMIT_BYTES)


def _store_subrows(ref, value, first=0, stride=None):
    sub = value.shape[1] // LANES
    stride = stride or sub
    for c in range(sub):
        ref[pl.ds(first + c, value.shape[0], stride=stride), :] = value[:, c * LANES:(c + 1) * LANES]


def _load_subrows(ref, rows, sub, first=0, stride=None):
    stride = stride or sub
    return jnp.concatenate(
        [ref[pl.ds(first + c, rows, stride=stride), :] for c in range(sub)], axis=1)


def _inproj_kernel(x_ref, w_ref, h_ref, *maybe_hb_ref):
    h = jnp.dot(x_ref[...].astype(BF16), w_ref[...], preferred_element_type=F32)
    h_ref[...] = h
    for hb_ref in maybe_hb_ref:
        hb_ref[...] = h.astype(BF16)


def _inproj(x2, w_bf16, emit_bf16):
    T, D = x2.shape
    N = w_bf16.shape[1]
    tm = INPROJ_ROWS
    out_shape = [jax.ShapeDtypeStruct((T, N), F32)]
    out_specs = [pl.BlockSpec((tm, N), lambda i: (i, 0))]
    if emit_bf16:
        out_shape.append(jax.ShapeDtypeStruct((T, N), BF16))
        out_specs.append(pl.BlockSpec((tm, N), lambda i: (i, 0)))
    return pl.pallas_call(
        _inproj_kernel,
        grid=(T // tm,),
        in_specs=[pl.BlockSpec((tm, D), lambda i: (i, 0)),
                  pl.BlockSpec((D, N), lambda i: (0, 0))],
        out_specs=out_specs,
        out_shape=out_shape,
        compiler_params=_params("parallel"),
        name="inproj",
    )(x2, w_bf16)


def _cumsum_rows(x, row):
    n = x.shape[0]
    sh = 1
    while sh < n:
        x = x + jnp.where(row >= sh, pltpu.roll(x, sh, 0), 0.0)
        sh *= 2
    return x


def _bcast_row(a, group, r):
    n = a.shape[0]
    a3 = a.reshape(n // group, group, LANES)
    return jnp.broadcast_to(a3[:, r:r + 1, :], a3.shape).reshape(n, LANES)


def _hgrn_chunk(qr, fr, v, gr, lb, ng, e_sum, st_t):
    C = qr.shape[0]
    row = lax.broadcasted_iota(jnp.int32, (C, LANES), 0)
    rr = lax.broadcasted_iota(jnp.int32, (C, C), 0)
    cc = lax.broadcasted_iota(jnp.int32, (C, C), 1)

    q = qr * _sigmoid(qr)
    forget = lb + (1.0 - lb) * _sigmoid(fr)
    k = 1.0 - forget
    G = _cumsum_rows(jnp.log(forget), row)

    parts = []
    for s in range(SUBLANES):
        Gs = _bcast_row(G, SUBLANES, s)
        ks = _bcast_row(k, SUBLANES, s)
        parts.append((q * ks * jnp.exp(jnp.minimum(G - Gs, 0.0))).astype(BF16))
    a_diag = jnp.dot(jnp.concatenate(parts, axis=1), e_sum, preferred_element_type=F32)
    A = jnp.where(((rr >> 3) == (cc >> 3)) & (cc <= rr), a_diag, 0.0)

    m = SUBLANES
    while m < C:
        lg = int(math.log2(m))
        Gr = _bcast_row(G, 2 * m, m - 1)
        second = ((row >> lg) & 1) == 1
        qm = jnp.where(second, q * jnp.exp(jnp.minimum(G - Gr, 0.0)), 0.0)
        km = jnp.where(second, 0.0, k * jnp.exp(jnp.minimum(Gr - G, 0.0)))
        am = lax.dot_general(qm.astype(BF16), km.astype(BF16), _NT, preferred_element_type=F32)
        A = A + jnp.where((rr >> (lg + 1)) == (cc >> (lg + 1)), am, 0.0)
        m *= 2

    vb = v.astype(BF16)
    o = jnp.dot(A.astype(BF16), vb, preferred_element_type=F32)
    o = o + lax.dot_general((q * jnp.exp(G)).astype(BF16), st_t.astype(BF16), _NT,
                            preferred_element_type=F32)
    g_end = G[C - 1:C, :]
    kd = (k * jnp.exp(g_end - G)).astype(BF16)
    st_new = st_t * jnp.exp(g_end) + lax.dot_general(vb, kd, _TN, preferred_element_type=F32)

    ms = jnp.mean(o * o, axis=-1, keepdims=True)
    out = o * lax.rsqrt(ms + RMS_EPS) * ng * _sigmoid(gr)
    return out, st_new


def _hgrn_kernel(q_ref, f_ref, i_ref, g_ref, lb_ref, ng_ref, e_ref, o_ref, st_ref, *, chunk):
    @pl.when(pl.program_id(2) == 0)
    def _():
        st_ref[...] = jnp.zeros_like(st_ref)

    lb = lb_ref[0]
    ng = ng_ref[...]
    e_sum = e_ref[...]
    n_chunks = q_ref.shape[1] // chunk
    for c in range(n_chunks):
        sl = pl.ds(c * chunk, chunk)
        out, st_new = _hgrn_chunk(q_ref[0, sl, :], f_ref[0, sl, :], i_ref[0, sl, :],
                                  g_ref[0, sl, :], lb, ng, e_sum, st_ref[...])
        st_ref[...] = st_new
        o_ref[0, sl, :] = out


def _hgrn_mixer(h3, lb, norm_g):
    B, S, _ = h3.shape
    ts = min(HGRN_ROWS, S)
    C = HGRN_CHUNK
    H = HGRN_HEADS
    e_sum = (jnp.arange(SUBLANES * LANES)[:, None] // LANES == jnp.arange(C)[None, :] % SUBLANES
             ).astype(BF16)
    col = lambda off: pl.BlockSpec((1, ts, LANES), lambda b, h, s, off=off: (b, s, off + h))
    return pl.pallas_call(
        functools.partial(_hgrn_kernel, chunk=C),
        grid=(B, H, S // ts),
        in_specs=[col(0), col(H), col(2 * H), col(3 * H),
                  pl.BlockSpec((1, 1, LANES), lambda b, h, s: (h, 0, 0)),
                  pl.BlockSpec((1, LANES), lambda b, h, s: (0, 0)),
                  pl.BlockSpec((SUBLANES * LANES, C), lambda b, h, s: (0, 0))],
        out_specs=pl.BlockSpec((1, ts, LANES), lambda b, h, s: (b, s, h)),
        out_shape=jax.ShapeDtypeStruct((B, S, MIX_WIDTH), F32),
        scratch_shapes=[pltpu.VMEM((HGRN_DK, HGRN_DK), F32)],
        compiler_params=_params("parallel", "parallel", "arbitrary"),
        name="hgrn",
    )(h3, h3, h3, h3, lb.reshape(H, 1, LANES), norm_g.reshape(1, LANES), e_sum)


def _moba_kernel(sl_ref, q_ref, kf_ref, kb_ref, vb_ref, o_ref, kmean_ref):
    BLK = MOBA_BLOCK
    p = pl.program_id(1)
    i = pl.program_id(2)

    @pl.when(i == 0)
    def _():
        kmean_ref[...] = jnp.zeros_like(kmean_ref)

    qf = q_ref[0]
    own = pl.ds(pl.multiple_of(i * BLK, BLK), BLK)
    k_own = kb_ref[0, own, :]
    v_own = vb_ref[0, own, :]

    lane = lax.broadcasted_iota(jnp.int32, (BLK, LANES), 1)
    rr = lax.broadcasted_iota(jnp.int32, (BLK, BLK), 0)
    cc = lax.broadcasted_iota(jnp.int32, (BLK, BLK), 1)
    rel = (rr - cc).astype(F32)
    scale = MOBA_HEAD_DIM ** -0.5

    outs = []
    for hh in range(2):
        head = (lane < MOBA_HEAD_DIM) if hh == 0 else (lane >= MOBA_HEAD_DIM)
        slope = sl_ref[2 * p + hh]
        qh = jnp.where(head, qf, 0.0)

        gate = lax.dot_general(qh, kmean_ref[...], _NT, precision=lax.Precision.HIGHEST,
                               preferred_element_type=F32)
        past = lane < i
        g = jnp.where(past, gate, -jnp.inf)
        sel = jnp.zeros((BLK, LANES), jnp.bool_)
        for _ in range(MOBA_TOPK):
            mx = jnp.max(g, axis=1, keepdims=True)
            idx = jnp.min(jnp.where(g == mx, lane, LANES), axis=1, keepdims=True)
            pick = lane == idx
            sel = sel | pick
            g = jnp.where(pick, -jnp.inf, g)
        sel_bias = jnp.where(sel & past, 0.0, NEG_BIG)
        q_aug = jnp.concatenate([(qh * scale).astype(BF16), sel_bias.astype(BF16)], axis=1)

        bias = -slope * rel
        s = lax.dot_general((qh * scale).astype(BF16), k_own, _NT, preferred_element_type=F32)
        s = s + jnp.where(cc <= rr, bias, -jnp.inf)
        m0 = jnp.max(s, axis=1, keepdims=True)
        p0 = jnp.exp(s - m0)
        l0 = jnp.sum(p0, axis=1, keepdims=True)
        acc0 = jnp.dot(p0.astype(BF16), v_own, preferred_element_type=F32)

        def body(n, carry):
            m, l, acc = carry
            blk = pl.ds(pl.multiple_of(n * BLK, BLK), BLK)
            onehot = jnp.where(lane == n, 1.0, 0.0).astype(BF16)
            k_aug = jnp.concatenate([kb_ref[0, blk, :], onehot], axis=1)
            sn = lax.dot_general(q_aug, k_aug, _NT, preferred_element_type=F32)
            sn = sn + (bias - slope * ((i - n) * BLK).astype(F32))
            m_new = jnp.maximum(m, jnp.max(sn, axis=1, keepdims=True))
            a = jnp.exp(m - m_new)
            pn = jnp.exp(sn - m_new)
            l = a * l + jnp.sum(pn, axis=1, keepdims=True)
            acc = a * acc + jnp.dot(pn.astype(BF16), vb_ref[0, blk, :],
                                    preferred_element_type=F32)
            return m_new, l, acc

        _, l, acc = lax.fori_loop(0, i, body, (m0, l0, acc0))
        outs.append(acc / l)

    o_ref[0] = jnp.where(lane < MOBA_HEAD_DIM, outs[0], outs[1])
    kmean_ref[pl.ds(i, 1), :] = jnp.mean(kf_ref[0], axis=0, keepdims=True)


def _moba_mixer(h3, hb3):
    B, S, _ = h3.shape
    BLK = MOBA_BLOCK
    NP = MOBA_HEADS // 2
    slopes = jnp.asarray(_alibi_slope_list(MOBA_HEADS), F32)
    grid_spec = pltpu.PrefetchScalarGridSpec(
        num_scalar_prefetch=1,
        grid=(B, NP, S // BLK),
        in_specs=[pl.BlockSpec((1, BLK, LANES), lambda b, p, i, sl: (b, i, p)),
                  pl.BlockSpec((1, BLK, LANES), lambda b, p, i, sl: (b, i, NP + p)),
                  pl.BlockSpec((1, S, LANES), lambda b, p, i, sl: (b, 0, NP + p)),
                  pl.BlockSpec((1, S, LANES), lambda b, p, i, sl: (b, 0, 2 * NP + p))],
        out_specs=pl.BlockSpec((1, BLK, LANES), lambda b, p, i, sl: (b, i, p)),
        scratch_shapes=[pltpu.VMEM((LANES, LANES), F32)],
    )
    return pl.pallas_call(
        _moba_kernel,
        grid_spec=grid_spec,
        out_shape=jax.ShapeDtypeStruct((B, S, MIX_WIDTH), F32),
        compiler_params=_params("parallel", "parallel", "arbitrary"),
        name="moba",
    )(slopes, h3, h3, hb3, hb3)


def _memkv_kernel(mem_ref, w_ref, kv_ref):
    kv_ref[0] = jnp.dot(mem_ref[0].astype(BF16), w_ref[...].astype(BF16),
                        preferred_element_type=F32).astype(BF16)


def _memkv(mem, w_kv):
    B, M, D = mem.shape
    N = w_kv.shape[1]
    return pl.pallas_call(
        _memkv_kernel,
        grid=(B,),
        in_specs=[pl.BlockSpec((1, M, D), lambda b: (b, 0, 0)),
                  pl.BlockSpec((D, N), lambda b: (0, 0))],
        out_specs=pl.BlockSpec((1, M, N), lambda b: (b, 0, 0)),
        out_shape=jax.ShapeDtypeStruct((B, M, N), BF16),
        compiler_params=_params("parallel"),
        name="memkv",
    )(mem, w_kv)


def _layer_norm(z, g, b):
    mu = jnp.mean(z, axis=-1, keepdims=True)
    zc = z - mu
    var = jnp.mean(zc * zc, axis=-1, keepdims=True)
    return zc * lax.rsqrt(var + LN_EPS) * g + b


def _post_kernel(x_ref, mix_ref, mq_ref, kv_ref, wo_ref, g_ref, b_ref, wr_ref, br_ref,
                 x1_ref, x1s_ref, idx_ref, gate_ref, *, alpha):
    tm = x_ref.shape[0]
    mq = mq_ref[...]
    kv = kv_ref[0]
    km = kv[:, :MEM_WIDTH]
    vm = kv[:, MEM_WIDTH:]
    lane = lax.broadcasted_iota(jnp.int32, (tm, MEM_WIDTH), 1)
    scale = MEM_HEAD_DIM ** -0.5
    mo = jnp.zeros((tm, MEM_WIDTH), F32)
    for hd in range(MEM_HEADS):
        head = (lane >> 6) == hd
        qh = jnp.where(head, mq * scale, 0.0).astype(BF16)
        s = lax.dot_general(qh, km, _NT, preferred_element_type=F32)
        m = jnp.max(s, axis=1, keepdims=True)
        p = jnp.exp(s - m)
        l = jnp.sum(p, axis=1, keepdims=True)
        oh = jnp.dot(p.astype(BF16), vm, preferred_element_type=F32) / l
        mo = jnp.where(head, oh, mo)

    y = jnp.dot(mix_ref[...].astype(BF16), wo_ref[:MIX_WIDTH, :], preferred_element_type=F32)
    y = y + jnp.dot(mo.astype(BF16), wo_ref[MIX_WIDTH:, :], preferred_element_type=F32)
    x1 = _layer_norm(alpha * x_ref[...] + y, g_ref[...], b_ref[...])
    x1_ref[...] = x1
    _store_subrows(x1s_ref, x1)

    logits = jnp.dot(x1, wr_ref[...], precision=lax.Precision.HIGHEST,
                     preferred_element_type=F32) + br_ref[...]
    lane_e = lax.broadcasted_iota(jnp.int32, (tm, LANES), 1)
    g = jnp.where(lane_e < N_EXPERTS, logits, -jnp.inf)
    idx_out = jnp.zeros((tm, LANES), jnp.int32)
    val_out = jnp.full((tm, LANES), -jnp.inf, F32)
    for kk in range(TOP_K):
        mx = jnp.max(g, axis=1, keepdims=True)
        idx = jnp.min(jnp.where(g == mx, lane_e, LANES), axis=1, keepdims=True)
        idx_out = jnp.where(lane_e == kk, idx, idx_out)
        val_out = jnp.where(lane_e == kk, mx, val_out)
        g = jnp.where(lane_e == idx, -jnp.inf, g)
    vmax = jnp.max(val_out, axis=1, keepdims=True)
    ev = jnp.exp(val_out - vmax)
    idx_ref[...] = idx_out
    gate_ref[...] = ev / jnp.sum(ev, axis=1, keepdims=True)


def _post_mixer(x2, mix2, h2, kv, w_o_bf16, ln_g, ln_b, w_router, b_router, seq_len, alpha):
    T, D = x2.shape
    tm = POST_ROWS
    N = h2.shape[1]
    M = kv.shape[1]
    mq_col = (N - MEM_WIDTH) // MEM_WIDTH
    tiles_per_seq = seq_len // tm
    wr = jnp.zeros((D, LANES), F32).at[:, :N_EXPERTS].set(w_router)
    br = jnp.zeros((1, LANES), F32).at[0, :N_EXPERTS].set(b_router)
    row = lambda n: pl.BlockSpec((tm, n), lambda i: (i, 0))
    full = lambda a, b: pl.BlockSpec((a, b), lambda i: (0, 0))
    return pl.pallas_call(
        functools.partial(_post_kernel, alpha=alpha),
        grid=(T // tm,),
        in_specs=[row(D), row(MIX_WIDTH),
                  pl.BlockSpec((tm, MEM_WIDTH), lambda i: (i, mq_col)),
                  pl.BlockSpec((1, M, 2 * MEM_WIDTH), lambda i: (i // tiles_per_seq, 0, 0)),
                  full(D, D), full(1, D), full(1, D), full(D, LANES), full(1, LANES)],
        out_specs=[row(D), pl.BlockSpec((tm * (D // LANES), LANES), lambda i: (i, 0)),
                   row(LANES), row(LANES)],
        out_shape=[jax.ShapeDtypeStruct((T, D), F32),
                   jax.ShapeDtypeStruct((T * (D // LANES), LANES), F32),
                   jax.ShapeDtypeStruct((T, LANES), jnp.int32),
                   jax.ShapeDtypeStruct((T, LANES), F32)],
        compiler_params=_params("parallel"),
        name="post_mixer",
    )(x2, mix2, h2, kv, w_o_bf16, ln_g.reshape(1, D), ln_b.reshape(1, D), wr, br)


def _sc_mesh():
    return plsc.VectorSubcoreMesh(core_axis_name="core", subcore_axis_name="subcore")


def _sc_scatter_rows(rows, idx, n_rep, n_out):
    R, W = rows.shape

    @functools.partial(pl.kernel, out_type=jax.ShapeDtypeStruct((n_out, W), rows.dtype),
                       mesh=_sc_mesh(), scratch_types=[])
    def scatter(x_hbm, i_hbm, o_hbm):
        def body(x_vmem, i_vmem):
            for r in range(n_rep):
                pltpu.sync_copy(x_vmem, o_hbm.at[i_vmem.at[r]])

        pltpu.emit_pipeline(
            body, grid=(R // SC_WINDOW,),
            in_specs=[pl.BlockSpec((SC_WINDOW, W), lambda i: (i, 0)),
                      pl.BlockSpec((SC_IDX_ROWS, SC_WINDOW), lambda i: (0, i))],
            out_specs=[], core_axis_name=("core", "subcore"),
            dimension_semantics=(pltpu.PARALLEL,), trace_scopes=False)(x_hbm, i_hbm)

    return scatter(rows, idx)


def _sc_gather_rows(table, idx):
    n = idx.shape[0]
    W = table.shape[1]

    @functools.partial(pl.kernel, out_type=jax.ShapeDtypeStruct((n, W), table.dtype),
                       mesh=_sc_mesh(), scratch_types=[])
    def gather(t_hbm, i_hbm, o_hbm):
        def body(i_vmem, o_vmem):
            pltpu.sync_copy(t_hbm.at[i_vmem.at[0]], o_vmem)

        pltpu.emit_pipeline(
            body, grid=(n // SC_WINDOW,),
            in_specs=[pl.BlockSpec((1, SC_WINDOW), lambda i: (0, i))],
            out_specs=[pl.BlockSpec((SC_WINDOW, W), lambda i: (i, 0))],
            core_axis_name=("core", "subcore"),
            dimension_semantics=(pltpu.PARALLEL,), trace_scopes=False)(i_hbm, o_hbm)

    return gather(table, idx.reshape(1, n))


def _route(top_idx, n_tokens):
    rb = MOE_ROWS
    tk = n_tokens * TOP_K
    flat_e = top_idx.reshape(-1)
    onehot = (flat_e[:, None] == jnp.arange(N_EXPERTS, dtype=jnp.int32)[None, :]).astype(jnp.int32)
    csum = jnp.cumsum(onehot, axis=0)
    rank = jnp.sum(onehot * csum, axis=1) - 1
    counts = csum[-1]
    padded = (counts + rb - 1) // rb * rb
    pend = jnp.cumsum(padded)
    pstart = pend - padded
    dest = (pstart[flat_e] + rank).astype(jnp.int32).reshape(n_tokens, TOP_K)
    n_blocks = tk // rb + N_EXPERTS
    block_e = jnp.minimum(
        jnp.searchsorted(pend, jnp.arange(n_blocks, dtype=jnp.int32) * rb, side='right'),
        N_EXPERTS - 1).astype(jnp.int32)
    n_used = (pend[-1] // rb).astype(jnp.int32).reshape(1)
    return dest, block_e, n_used


def _dispatch(x1s, dest, n_rows, sub):
    T = dest.shape[0]
    c = jnp.arange(sub, dtype=jnp.int32)
    idx = (dest.T[:, :, None] * sub + c[None, None, :]).reshape(TOP_K, T * sub)
    idx = jnp.concatenate([idx, jnp.zeros((SC_IDX_ROWS - TOP_K, T * sub), jnp.int32)], axis=0)
    return _sc_scatter_rows(x1s, idx, TOP_K, n_rows * sub)


def _expert_kernel(be_ref, nu_ref, x_ref, wg_ref, bg_ref, wu_ref, bu_ref, wd_ref, bd_ref,
                   y_ref, wgb, wub, wdb):
    i = pl.program_id(0)
    prev = be_ref[jnp.maximum(i - 1, 0)]

    @pl.when((i == 0) | (be_ref[i] != prev))
    def _():
        wgb[...] = wg_ref[0, 0].astype(BF16)
        wub[...] = wu_ref[0, 0].astype(BF16)
        wdb[...] = wd_ref[0, 0].astype(BF16)

    @pl.when(i < nu_ref[0])
    def _():
        sub = wgb.shape[0] // LANES
        xb = _load_subrows(x_ref, x_ref.shape[0] // sub, sub).astype(BF16)
        gate = jnp.dot(xb, wgb[...], preferred_element_type=F32) + bg_ref[0, 0]
        gate = jnp.minimum(gate, SWIGLU_LIMIT)
        up = jnp.dot(xb, wub[...], preferred_element_type=F32) + bu_ref[0, 0]
        up = jnp.clip(up, -SWIGLU_LIMIT, SWIGLU_LIMIT)
        hid = gate * _sigmoid(SWIGLU_ALPHA * gate) * (up + 1.0)
        y = jnp.dot(hid.astype(BF16), wdb[...], preferred_element_type=F32) + bd_ref[0, 0]
        _store_subrows(y_ref, y)


def _experts(xs, block_e, n_used, layer, w_gate, b_gate, w_up, b_up, w_down, b_down):
    rb = MOE_ROWS
    n_blocks = block_e.shape[0]
    E, D, F = w_gate.shape[1:]
    sub = D // LANES
    wspec = lambda a, b: pl.BlockSpec((1, 1, a, b), lambda i, be, nu: (layer, be[i], 0, 0))
    live = lambda i, be, nu: (jnp.where(i < nu[0], i, n_blocks), 0)
    grid_spec = pltpu.PrefetchScalarGridSpec(
        num_scalar_prefetch=2,
        grid=(n_blocks,),
        in_specs=[pl.BlockSpec((rb * sub, LANES), live),
                  wspec(D, F), wspec(1, F), wspec(D, F), wspec(1, F), wspec(F, D), wspec(1, D)],
        out_specs=pl.BlockSpec((rb * sub, LANES), live),
        scratch_shapes=[pltpu.VMEM((D, F), BF16), pltpu.VMEM((D, F), BF16),
                        pltpu.VMEM((F, D), BF16)],
    )
    depth = w_gate.shape[0]
    return pl.pallas_call(
        _expert_kernel,
        grid_spec=grid_spec,
        out_shape=jax.ShapeDtypeStruct(xs.shape, F32),
        compiler_params=_params("arbitrary"),
        name="experts",
    )(block_e, n_used, xs, w_gate, b_gate.reshape(depth, E, 1, F),
      w_up, b_up.reshape(depth, E, 1, F), w_down, b_down.reshape(depth, E, 1, D))


def _combine_kernel(x1_ref, gate_ref, y_ref, g_ref, b_ref, o_ref, *, alpha):
    tm, D = x1_ref.shape
    sub = D // LANES
    gates = gate_ref[...]
    f = jnp.zeros(x1_ref.shape, F32)
    for kk in range(TOP_K):
        f = f + gates[:, kk:kk + 1] * _load_subrows(y_ref, tm, sub, kk * sub, TOP_K * sub)
    o_ref[...] = _layer_norm(alpha * x1_ref[...] + f, g_ref[...], b_ref[...])


def _combine(x1, gates, dest, y_rows, ln_g, ln_b, alpha):
    T, D = x1.shape
    sub = D // LANES
    tm = COMBINE_ROWS
    c = jnp.arange(sub, dtype=jnp.int32)
    idx = (dest[:, :, None] * sub + c[None, None, :]).reshape(T * TOP_K * sub)
    yg = _sc_gather_rows(y_rows, idx)
    row = lambda n: pl.BlockSpec((tm, n), lambda i: (i, 0))
    full = lambda a, b: pl.BlockSpec((a, b), lambda i: (0, 0))
    return pl.pallas_call(
        functools.partial(_combine_kernel, alpha=alpha),
        grid=(T // tm,),
        in_specs=[row(D), row(LANES), pl.BlockSpec((tm * TOP_K * sub, LANES), lambda i: (i, 0)),
                  full(1, D), full(1, D)],
        out_specs=row(D),
        out_shape=jax.ShapeDtypeStruct((T, D), F32),
        compiler_params=_params("parallel"),
        name="combine",
    )(x1, gates, yg, ln_g.reshape(1, D), ln_b.reshape(1, D))


def kernel(x, mem, w_in_hgrn, hgrn_lb_logits, hgrn_norm_g, w_in_moba, w_mem_kv, w_o,
           ln_mix_g, ln_mix_b, w_router, b_router, w_gate, b_gate, w_up, b_up,
           w_down, b_down, ln_ffn_g, ln_ffn_b):
    B, S, D = x.shape
    T = B * S
    depth = w_o.shape[0]
    alpha = (2 * depth) ** 0.25

    p_lb = jax.nn.softmax(hgrn_lb_logits.astype(F32), axis=0)
    lower_bounds = jnp.cumsum(p_lb, axis=0) - p_lb[0]

    x2 = x.reshape(T, D)
    for layer in range(depth):
        j = layer // 2
        if layer % 2 == 0:
            (h2,) = _inproj(x2, w_in_hgrn[j].astype(BF16), emit_bf16=False)
            mix = _hgrn_mixer(h2.reshape(B, S, -1), lower_bounds[j], hgrn_norm_g[j])
        else:
            h2, hb2 = _inproj(x2, w_in_moba[j].astype(BF16), emit_bf16=True)
            mix = _moba_mixer(h2.reshape(B, S, -1), hb2.reshape(B, S, -1))
        kv = _memkv(mem, w_mem_kv[layer])
        x1, x1s, top_idx, gates = _post_mixer(
            x2, mix.reshape(T, MIX_WIDTH), h2, kv, w_o[layer].astype(BF16),
            ln_mix_g[layer], ln_mix_b[layer], w_router[layer], b_router[layer], S, alpha)
        dest, block_e, n_used = _route(top_idx[:, :TOP_K], T)
        xs = _dispatch(x1s, dest, (block_e.shape[0] + 1) * MOE_ROWS, D // LANES)
        y_rows = _experts(xs, block_e, n_used, layer, w_gate, b_gate, w_up, b_up, w_down, b_down)
        x2 = _combine(x1, gates, dest, y_rows, ln_ffn_g[layer], ln_ffn_b[layer], alpha)
    return x2.reshape(B, S, D)
```

```python
import functools
import math

import jax
import jax.numpy as jnp
from jax import lax
from jax.experimental import pallas as pl
from jax.experimental.pallas import tpu as pltpu
from jax.experimental.pallas import tpu_sc as plsc

MIX_WIDTH = 768
MEM_HEADS = 4
MEM_HEAD_DIM = 64
MEM_WIDTH = MEM_HEADS * MEM_HEAD_DIM
HGRN_HEADS = 6
HGRN_DK = 128
MOBA_HEADS = 12
MOBA_HEAD_DIM = 64
MOBA_BLOCK = 256
MOBA_TOPK = 3
N_EXPERTS = 32
TOP_K = 4
SWIGLU_ALPHA = 1.702
SWIGLU_LIMIT = 7.0
LN_EPS = 1e-5
RMS_EPS = 1e-6

LANES = 128
SUBLANES = 8
VMEM_LIMIT_BYTES = 56 * 1024 * 1024

INPROJ_ROWS = 512
HGRN_CHUNK = 64
HGRN_ROWS = 512
POST_ROWS = 256
MOBA_TILE = 256
MOE_ROWS = 256
COMBINE_ROWS = 256
SC_WINDOW = 128
SC_IDX_ROWS = 8

BF16 = jnp.bfloat16
F32 = jnp.float32
NEG_BIG = -1e30

_NT = (((1,), (1,)), ((), ()))
_TN = (((0,), (0,)), ((), ()))


def _alibi_slope_list(n):
    def pow2(m):
        start = 2.0 ** (-(2.0 ** -(math.log2(m) - 3)))
        return [start ** (i + 1) for i in range(m)]
    if math.log2(n).is_integer():
        return pow2(n)
    c = 2 ** math.floor(math.log2(n))
    return pow2(c) + _alibi_slope_list(2 * c)[0::2][:n - c]


def _sigmoid(x):
    return 1.0 / (1.0 + jnp.exp(-x))


def _params(*sem):
    return pltpu.CompilerParams(dimension_semantics=sem, vmem_limit_bytes=VMEM_LIMIT_BYTES)


def _store_subrows(ref, value, first=0, stride=None):
    sub = value.shape[1] // LANES
    stride = stride or sub
    for c in range(sub):
        ref[pl.ds(first + c, value.shape[0], stride=stride), :] = value[:, c * LANES:(c + 1) * LANES]


def _load_subrows(ref, rows, sub, first=0, stride=None):
    stride = stride or sub
    return jnp.concatenate(
        [ref[pl.ds(first + c, rows, stride=stride), :] for c in range(sub)], axis=1)


def _inproj_kernel(x_ref, w_ref, h_ref, *moba_refs):
    h = jnp.dot(x_ref[...].astype(BF16), w_ref[...], preferred_element_type=F32)
    h_ref[...] = h
    if moba_refs:
        hb_ref, q6_ref = moba_refs
        hb_ref[...] = h.astype(BF16)
        for p in range(q6_ref.shape[0]):
            q6_ref[p] = h[:, p * LANES:(p + 1) * LANES]


def _inproj(x2, w_bf16, for_moba):
    T, D = x2.shape
    N = w_bf16.shape[1]
    tm = INPROJ_ROWS
    out_shape = [jax.ShapeDtypeStruct((T, N), F32)]
    out_specs = [pl.BlockSpec((tm, N), lambda i: (i, 0))]
    if for_moba:
        NP = MOBA_HEADS // 2
        out_shape += [jax.ShapeDtypeStruct((T, N), BF16), jax.ShapeDtypeStruct((NP, T, LANES), F32)]
        out_specs += [pl.BlockSpec((tm, N), lambda i: (i, 0)),
                      pl.BlockSpec((NP, tm, LANES), lambda i: (0, i, 0))]
    return pl.pallas_call(
        _inproj_kernel,
        grid=(T // tm,),
        in_specs=[pl.BlockSpec((tm, D), lambda i: (i, 0)),
                  pl.BlockSpec((D, N), lambda i: (0, 0))],
        out_specs=out_specs,
        out_shape=out_shape,
        compiler_params=_params("parallel"),
        name="inproj",
    )(x2, w_bf16)


def _cumsum_rows(x, row):
    n = x.shape[0]
    sh = 1
    while sh < n:
        x = x + jnp.where(row >= sh, pltpu.roll(x, sh, 0), 0.0)
        sh *= 2
    return x


def _bcast_row(a, group, r):
    n = a.shape[0]
    a3 = a.reshape(n // group, group, LANES)
    return jnp.broadcast_to(a3[:, r:r + 1, :], a3.shape).reshape(n, LANES)


def _hgrn_chunk(qr, fr, v, gr, lb, ng, e_sum, st_t):
    C = qr.shape[0]
    row = lax.broadcasted_iota(jnp.int32, (C, LANES), 0)
    rr = lax.broadcasted_iota(jnp.int32, (C, C), 0)
    cc = lax.broadcasted_iota(jnp.int32, (C, C), 1)

    q = qr * _sigmoid(qr)
    forget = lb + (1.0 - lb) * _sigmoid(fr)
    k = 1.0 - forget
    G = _cumsum_rows(jnp.log(forget), row)

    parts = []
    for s in range(SUBLANES):
        Gs = _bcast_row(G, SUBLANES, s)
        ks = _bcast_row(k, SUBLANES, s)
        parts.append((q * ks * jnp.exp(jnp.minimum(G - Gs, 0.0))).astype(BF16))
    a_diag = jnp.dot(jnp.concatenate(parts, axis=1), e_sum, preferred_element_type=F32)
    A = jnp.where(((rr >> 3) == (cc >> 3)) & (cc <= rr), a_diag, 0.0)

    m = SUBLANES
    while m < C:
        lg = int(math.log2(m))
        Gr = _bcast_row(G, 2 * m, m - 1)
        second = ((row >> lg) & 1) == 1
        qm = jnp.where(second, q * jnp.exp(jnp.minimum(G - Gr, 0.0)), 0.0)
        km = jnp.where(second, 0.0, k * jnp.exp(jnp.minimum(Gr - G, 0.0)))
        am = lax.dot_general(qm.astype(BF16), km.astype(BF16), _NT, preferred_element_type=F32)
        A = A + jnp.where((rr >> (lg + 1)) == (cc >> (lg + 1)), am, 0.0)
        m *= 2

    vb = v.astype(BF16)
    o = jnp.dot(A.astype(BF16), vb, preferred_element_type=F32)
    o = o + lax.dot_general((q * jnp.exp(G)).astype(BF16), st_t.astype(BF16), _NT,
                            preferred_element_type=F32)
    g_end = G[C - 1:C, :]
    kd = (k * jnp.exp(g_end - G)).astype(BF16)
    st_new = st_t * jnp.exp(g_end) + lax.dot_general(vb, kd, _TN, preferred_element_type=F32)

    ms = jnp.mean(o * o, axis=-1, keepdims=True)
    out = o * lax.rsqrt(ms + RMS_EPS) * ng * _sigmoid(gr)
    return out, st_new


def _hgrn_kernel(q_ref, f_ref, i_ref, g_ref, lb_ref, ng_ref, e_ref, o_ref, st_ref, *, chunk):
    @pl.when(pl.program_id(2) == 0)
    def _():
        st_ref[...] = jnp.zeros_like(st_ref)

    lb = lb_ref[0]
    ng = ng_ref[...]
    e_sum = e_ref[...]
    n_chunks = q_ref.shape[1] // chunk
    for c in range(n_chunks):
        sl = pl.ds(c * chunk, chunk)
        out, st_new = _hgrn_chunk(q_ref[0, sl, :], f_ref[0, sl, :], i_ref[0, sl, :],
                                  g_ref[0, sl, :], lb, ng, e_sum, st_ref[...])
        st_ref[...] = st_new
        o_ref[0, sl, :] = out


def _hgrn_mixer(h3, lb, norm_g):
    B, S, _ = h3.shape
    ts = min(HGRN_ROWS, S)
    C = HGRN_CHUNK
    H = HGRN_HEADS
    e_sum = (jnp.arange(SUBLANES * LANES)[:, None] // LANES == jnp.arange(C)[None, :] % SUBLANES
             ).astype(BF16)
    col = lambda off: pl.BlockSpec((1, ts, LANES), lambda b, h, s, off=off: (b, s, off + h))
    return pl.pallas_call(
        functools.partial(_hgrn_kernel, chunk=C),
        grid=(B, H, S // ts),
        in_specs=[col(0), col(H), col(2 * H), col(3 * H),
                  pl.BlockSpec((1, 1, LANES), lambda b, h, s: (h, 0, 0)),
                  pl.BlockSpec((1, LANES), lambda b, h, s: (0, 0)),
                  pl.BlockSpec((SUBLANES * LANES, C), lambda b, h, s: (0, 0))],
        out_specs=pl.BlockSpec((1, ts, LANES), lambda b, h, s: (b, s, h)),
        out_shape=jax.ShapeDtypeStruct((B, S, MIX_WIDTH), F32),
        scratch_shapes=[pltpu.VMEM((HGRN_DK, HGRN_DK), F32)],
        compiler_params=_params("parallel", "parallel", "arbitrary"),
        name="hgrn",
    )(h3, h3, h3, h3, lb.reshape(H, 1, LANES), norm_g.reshape(1, LANES), e_sum)


def _moba_kernel(sl_ref, q_ref, kf_ref, kb_ref, vb_ref, o_ref, kmean_ref):
    BLK = MOBA_BLOCK
    p = pl.program_id(1)
    i = pl.program_id(2)

    @pl.when(i == 0)
    def _():
        kmean_ref[...] = jnp.zeros_like(kmean_ref)

    qf = q_ref[0]
    own = pl.ds(pl.multiple_of(i * BLK, BLK), BLK)
    k_own = kb_ref[0, own, :]
    v_own = vb_ref[0, own, :]

    lane = lax.broadcasted_iota(jnp.int32, (BLK, LANES), 1)
    rr = lax.broadcasted_iota(jnp.int32, (BLK, BLK), 0)
    cc = lax.broadcasted_iota(jnp.int32, (BLK, BLK), 1)
    rel = (rr - cc).astype(F32)
    scale = MOBA_HEAD_DIM ** -0.5

    outs = []
    for hh in range(2):
        head = (lane < MOBA_HEAD_DIM) if hh == 0 else (lane >= MOBA_HEAD_DIM)
        slope = sl_ref[2 * p + hh]
        qh = jnp.where(head, qf, 0.0)

        gate = lax.dot_general(qh, kmean_ref[...], _NT, precision=lax.Precision.HIGHEST,
                               preferred_element_type=F32)
        past = lane < i
        g = jnp.where(past, gate, -jnp.inf)
        sel = jnp.zeros((BLK, LANES), jnp.bool_)
        for _ in range(MOBA_TOPK):
            mx = jnp.max(g, axis=1, keepdims=True)
            idx = jnp.min(jnp.where(g == mx, lane, LANES), axis=1, keepdims=True)
            pick = lane == idx
            sel = sel | pick
            g = jnp.where(pick, -jnp.inf, g)
        sel_bias = jnp.where(sel & past, 0.0, NEG_BIG)
        q_aug = jnp.concatenate([(qh * scale).astype(BF16), sel_bias.astype(BF16)], axis=1)

        bias = -slope * rel
        s = lax.dot_general((qh * scale).astype(BF16), k_own, _NT, preferred_element_type=F32)
        s = s + jnp.where(cc <= rr, bias, -jnp.inf)
        m0 = jnp.max(s, axis=1, keepdims=True)
        p0 = jnp.exp(s - m0)
        l0 = jnp.sum(p0, axis=1, keepdims=True)
        acc0 = jnp.dot(p0.astype(BF16), v_own, preferred_element_type=F32)

        def body(n, carry):
            m, l, acc = carry
            blk = pl.ds(pl.multiple_of(n * BLK, BLK), BLK)
            onehot = jnp.where(lane == n, 1.0, 0.0).astype(BF16)
            k_aug = jnp.concatenate([kb_ref[0, blk, :], onehot], axis=1)
            sn = lax.dot_general(q_aug, k_aug, _NT, preferred_element_type=F32)
            sn = sn + (bias - slope * ((i - n) * BLK).astype(F32))
            m_new = jnp.maximum(m, jnp.max(sn, axis=1, keepdims=True))
            a = jnp.exp(m - m_new)
            pn = jnp.exp(sn - m_new)
            l = a * l + jnp.sum(pn, axis=1, keepdims=True)
            acc = a * acc + jnp.dot(pn.astype(BF16), vb_ref[0, blk, :],
                                    preferred_element_type=F32)
            return m_new, l, acc

        _, l, acc = lax.fori_loop(0, i, body, (m0, l0, acc0))
        outs.append(acc / l)

    o_ref[0] = jnp.where(lane < MOBA_HEAD_DIM, outs[0], outs[1])
    kmean_ref[pl.ds(i, 1), :] = jnp.mean(kf_ref[0], axis=0, keepdims=True)


def _moba_mixer(h3, hb3):
    B, S, _ = h3.shape
    BLK = MOBA_BLOCK
    NP = MOBA_HEADS // 2
    slopes = jnp.asarray(_alibi_slope_list(MOBA_HEADS), F32)
    grid_spec = pltpu.PrefetchScalarGridSpec(
        num_scalar_prefetch=1,
        grid=(B, NP, S // BLK),
        in_specs=[pl.BlockSpec((1, BLK, LANES), lambda b, p, i, sl: (b, i, p)),
                  pl.BlockSpec((1, BLK, LANES), lambda b, p, i, sl: (b, i, NP + p)),
                  pl.BlockSpec((1, S, LANES), lambda b, p, i, sl: (b, 0, NP + p)),
                  pl.BlockSpec((1, S, LANES), lambda b, p, i, sl: (b, 0, 2 * NP + p))],
        out_specs=pl.BlockSpec((1, BLK, LANES), lambda b, p, i, sl: (b, i, p)),
        scratch_shapes=[pltpu.VMEM((LANES, LANES), F32)],
    )
    return pl.pallas_call(
        _moba_kernel,
        grid_spec=grid_spec,
        out_shape=jax.ShapeDtypeStruct((B, S, MIX_WIDTH), F32),
        compiler_params=_params("parallel", "parallel", "arbitrary"),
        name="moba",
    )(slopes, h3, h3, hb3, hb3)


def _moba_top3(qh, kmean, i, lane):
    gate = lax.dot_general(qh, kmean, _NT, precision=lax.Precision.HIGHEST,
                           preferred_element_type=F32)
    g = jnp.where(lane < i, gate, -jnp.inf)
    picks = []
    for _ in range(MOBA_TOPK):
        mx = jnp.max(g, axis=1, keepdims=True)
        idx = jnp.min(jnp.where(g == mx, lane, LANES), axis=1, keepdims=True)
        picks.append((idx, (mx > -jnp.inf) & (idx < i)))
        g = jnp.where(lane == idx, -jnp.inf, g)
    return picks


def _moba_select_kernel(q_ref, kf_ref, sr_ref, cnt_ref, kmean_ref):
    BLK = MOBA_BLOCK
    i = pl.program_id(2)

    @pl.when(i == 0)
    def _():
        kmean_ref[...] = jnp.zeros_like(kmean_ref)

    qf = q_ref[0]
    lane = lax.broadcasted_iota(jnp.int32, (BLK, LANES), 1)
    rr = lax.broadcasted_iota(jnp.int32, (BLK, BLK), 0)
    cc = lax.broadcasted_iota(jnp.int32, (BLK, BLK), 1)
    before = jnp.where(cc < rr, 1.0, 0.0).astype(BF16)
    out = jnp.zeros((BLK, LANES), jnp.int32)
    counts = []
    for hh in range(2):
        head = (lane >> 6) == hh
        picks = _moba_top3(jnp.where(head, qf, 0.0), kmean_ref[...], i, lane)
        chosen = jnp.zeros((BLK, LANES), F32)
        for idx, valid in picks:
            chosen = chosen + jnp.where((lane == idx) & valid, 1.0, 0.0)
        earlier = jnp.dot(before, chosen.astype(BF16), preferred_element_type=F32)
        for j, (idx, valid) in enumerate(picks):
            rep = hh * MOBA_TOPK + j
            rank = jnp.sum(jnp.where(lane == idx, earlier, 0.0), axis=1, keepdims=True)
            out = jnp.where(lane == rep, jnp.where(valid, idx, -1), out)
            out = jnp.where(lane == SUBLANES + rep, rank.astype(jnp.int32), out)
        total = earlier[BLK - 1:BLK, :] + chosen[BLK - 1:BLK, :]
        counts.append(jnp.broadcast_to(total, (SUBLANES, LANES)))
    sr_ref[0, 0] = out
    cnt_ref[0, 0] = jnp.concatenate(counts, axis=0)
    kmean_ref[pl.ds(i, 1), :] = jnp.mean(kf_ref[0], axis=0, keepdims=True)


def _moba_dest_kernel(sr_ref, tab_ref, idx_ref, *, trash):
    BLK = MOBA_BLOCK
    blk = sr_ref[0, 0]
    tab = tab_ref[0, 0]
    lane = lax.broadcasted_iota(jnp.int32, (BLK, LANES), 1)
    d = jnp.full((BLK, LANES), float(trash), F32)
    for rep in range(2 * MOBA_TOPK):
        hh = rep // MOBA_TOPK
        sel = blk[:, rep:rep + 1]
        rank = blk[:, SUBLANES + rep:SUBLANES + rep + 1]
        start = jnp.sum(jnp.where(lane == sel, tab[hh:hh + 1, :], 0.0), axis=1, keepdims=True)
        dest = jnp.where(sel >= 0, start + rank.astype(F32), float(trash))
        d = jnp.where(lane == rep, dest, d)
    idx_ref[...] = d.T[:SC_IDX_ROWS, :].astype(jnp.int32)


def _moba_tile_kernel(tb_ref, tp_ref, th_ref, tn_ref, nu_ref, sl_ref, q_ref, k_ref, v_ref, o_ref):
    t = pl.program_id(0)

    @pl.when(t < nu_ref[0])
    def _():
        tq = q_ref.shape[0]
        hh = th_ref[t]
        slope = sl_ref[2 * tp_ref[t] + hh]
        lane = lax.broadcasted_iota(jnp.int32, (tq, LANES), 1)
        head = (lane >> 6) == hh
        q = jnp.where(head, q_ref[...] * (MOBA_HEAD_DIM ** -0.5), 0.0).astype(BF16)
        s = lax.dot_general(q, k_ref[0], _NT, preferred_element_type=F32)
        kpos = lax.broadcasted_iota(jnp.int32, s.shape, 1).astype(F32)
        s = s + slope * kpos
        m = jnp.max(s, axis=1, keepdims=True)
        p = jnp.exp(s - m)
        l = jnp.sum(p, axis=1, keepdims=True)
        o = jnp.dot(p.astype(BF16), v_ref[0], preferred_element_type=F32)
        half = lane & (MOBA_HEAD_DIM - 1)
        stats = jnp.where(half == 0, m, jnp.where(half == 1, l, 0.0))
        o_ref[...] = jnp.where(head, o, stats)


def _moba_merge_kernel(sl_ref, q_ref, k_ref, v_ref, sr_ref, pg_ref, o_ref):
    BLK = MOBA_BLOCK
    p = pl.program_id(1)
    i = pl.program_id(2)
    qf = q_ref[0]
    k_own = k_ref[0]
    v_own = v_ref[0]
    blk = sr_ref[0, 0]
    lane = lax.broadcasted_iota(jnp.int32, (BLK, LANES), 1)
    rr = lax.broadcasted_iota(jnp.int32, (BLK, BLK), 0)
    cc = lax.broadcasted_iota(jnp.int32, (BLK, BLK), 1)
    rel = (rr - cc).astype(F32)
    row = lax.broadcasted_iota(jnp.int32, (BLK, 1), 0)
    outs = []
    for hh in range(2):
        head = (lane >> 6) == hh
        slope = sl_ref[2 * p + hh]
        qh = jnp.where(head, qf * (MOBA_HEAD_DIM ** -0.5), 0.0).astype(BF16)
        s = lax.dot_general(qh, k_own, _NT, preferred_element_type=F32)
        s = s + jnp.where(cc <= rr, -slope * rel, -jnp.inf)
        m = jnp.max(s, axis=1, keepdims=True)
        pr = jnp.exp(s - m)
        ms = [m]
        ls = [jnp.sum(pr, axis=1, keepdims=True)]
        accs = [jnp.dot(pr.astype(BF16), v_own, preferred_element_type=F32)]
        stat_lane = MOBA_HEAD_DIM * (1 - hh)
        for j in range(MOBA_TOPK):
            rep = hh * MOBA_TOPK + j
            part = pg_ref[rep, 0]
            sel = blk[:, rep:rep + 1]
            valid = sel >= 0
            shift = -slope * ((i - sel) * BLK + row).astype(F32)
            ms.append(jnp.where(valid, part[:, stat_lane:stat_lane + 1] + shift, -jnp.inf))
            ls.append(jnp.where(valid, part[:, stat_lane + 1:stat_lane + 2], 0.0))
            accs.append(jnp.where(valid, part, 0.0))
        top = functools.reduce(jnp.maximum, ms)
        ws = [jnp.exp(mk - top) for mk in ms]
        den = sum(w * lk for w, lk in zip(ws, ls))
        num = sum(w * ak for w, ak in zip(ws, accs))
        outs.append(num / den)
    o_ref[0] = jnp.where(lane < MOBA_HEAD_DIM, outs[0], outs[1])


def _moba_sparse_mixer(h3, hb3, q6):
    B, S, _ = h3.shape
    T = B * S
    BLK = MOBA_BLOCK
    TQ = MOBA_TILE
    NB = S // BLK
    NP = MOBA_HEADS // 2
    n_rep = 2 * MOBA_TOPK
    slopes = jnp.asarray(_alibi_slope_list(MOBA_HEADS), F32)
    step = lambda *rest: pl.BlockSpec((1, 1) + rest, lambda b, p, i: (b, p, i) + (0,) * (len(rest) - 1))

    selrank, counts = pl.pallas_call(
        _moba_select_kernel,
        grid=(B, NP, NB),
        in_specs=[pl.BlockSpec((1, BLK, LANES), lambda b, p, i: (b, i, p)),
                  pl.BlockSpec((1, BLK, LANES), lambda b, p, i: (b, i, NP + p))],
        out_specs=[step(BLK, LANES), step(2 * SUBLANES, LANES)],
        out_shape=[jax.ShapeDtypeStruct((B, NP, S, LANES), jnp.int32),
                   jax.ShapeDtypeStruct((B, NP, NB * 2 * SUBLANES, LANES), F32)],
        scratch_shapes=[pltpu.VMEM((LANES, LANES), F32)],
        compiler_params=_params("parallel", "parallel", "arbitrary"),
        name="moba_select",
    )(h3, h3)

    cnt = counts.reshape(B, NP, NB, 2, SUBLANES, LANES)[:, :, :, :, 0, :NB].astype(jnp.int32)
    base = jnp.cumsum(cnt, axis=2) - cnt
    total = jnp.sum(cnt, axis=2)
    padded = (total + TQ - 1) // TQ * TQ
    pend = jnp.cumsum(padded.reshape(-1))
    seg_start = (pend - padded.reshape(-1)).reshape(B, NP, 1, 2, NB)
    table = jnp.zeros((B, NP, NB, SUBLANES, LANES), F32).at[:, :, :, :2, :NB].set(
        (seg_start + base).astype(F32)).reshape(B, NP, NB * SUBLANES, LANES)
    n_seg = B * NP * 2 * NB
    max_tiles = (T * NP * n_rep) // TQ + n_seg
    tile_seg = jnp.minimum(
        jnp.searchsorted(pend, jnp.arange(max_tiles, dtype=jnp.int32) * TQ, side='right'),
        n_seg - 1).astype(jnp.int32)
    tile_n = tile_seg % NB
    tile_h = (tile_seg // NB) % 2
    tile_p = (tile_seg // (2 * NB)) % NP
    tile_b = tile_seg // (2 * NB * NP)
    n_used = (pend[-1] // TQ).astype(jnp.int32).reshape(1)
    trash = max_tiles * TQ

    idx = pl.pallas_call(
        functools.partial(_moba_dest_kernel, trash=trash),
        grid=(B, NP, NB),
        in_specs=[step(BLK, LANES), step(SUBLANES, LANES)],
        out_specs=pl.BlockSpec((SC_IDX_ROWS, BLK), lambda b, p, i: (0, p * (T // BLK) + b * NB + i)),
        out_shape=jax.ShapeDtypeStruct((SC_IDX_ROWS, NP * T), jnp.int32),
        compiler_params=_params("parallel", "parallel", "parallel"),
        name="moba_dest",
    )(selrank, table)

    n_rows = (max_tiles + 1) * TQ
    qs = _sc_scatter_rows(q6.reshape(NP * T, LANES), idx, n_rep, n_rows)

    live = lambda t, tb, tp, th, tn, nu, sl: (jnp.where(t < nu[0], t, max_tiles), 0)
    kv_blk = lambda off: pl.BlockSpec(
        (1, BLK, LANES), lambda t, tb, tp, th, tn, nu, sl, off=off: (tb[t], tn[t], off + tp[t]))
    part = pl.pallas_call(
        _moba_tile_kernel,
        grid_spec=pltpu.PrefetchScalarGridSpec(
            num_scalar_prefetch=6,
            grid=(max_tiles,),
            in_specs=[pl.BlockSpec((TQ, LANES), live), kv_blk(NP), kv_blk(2 * NP)],
            out_specs=pl.BlockSpec((TQ, LANES), live)),
        out_shape=jax.ShapeDtypeStruct((n_rows, LANES), F32),
        compiler_params=_params("arbitrary"),
        name="moba_tiles",
    )(tile_b, tile_p, tile_h, tile_n, n_used, slopes, qs, hb3, hb3)

    pg = _sc_gather_rows(part, idx[:n_rep].reshape(-1)).reshape(n_rep, NP, T, LANES)

    own = lambda off: pl.BlockSpec((1, BLK, LANES), lambda b, p, i, sl, off=off: (b, i, off + p))
    return pl.pallas_call(
        _moba_merge_kernel,
        grid_spec=pltpu.PrefetchScalarGridSpec(
            num_scalar_prefetch=1,
            grid=(B, NP, NB),
            in_specs=[own(0), own(NP), own(2 * NP),
                      pl.BlockSpec((1, 1, BLK, LANES), lambda b, p, i, sl: (b, p, i, 0)),
                      pl.BlockSpec((n_rep, 1, BLK, LANES), lambda b, p, i, sl: (0, p, b * NB + i, 0))],
            out_specs=pl.BlockSpec((1, BLK, LANES), lambda b, p, i, sl: (b, i, p))),
        out_shape=jax.ShapeDtypeStruct((B, S, MIX_WIDTH), F32),
        compiler_params=_params("parallel", "parallel", "parallel"),
        name="moba_merge",
    )(slopes, h3, hb3, hb3, selrank, pg)


def _memkv_kernel(mem_ref, w_ref, kv_ref):
    kv_ref[0] = jnp.dot(mem_ref[0].astype(BF16), w_ref[...].astype(BF16),
                        preferred_element_type=F32).astype(BF16)


def _memkv(mem, w_kv):
    B, M, D = mem.shape
    N = w_kv.shape[1]
    return pl.pallas_call(
        _memkv_kernel,
        grid=(B,),
        in_specs=[pl.BlockSpec((1, M, D), lambda b: (b, 0, 0)),
                  pl.BlockSpec((D, N), lambda b: (0, 0))],
        out_specs=pl.BlockSpec((1, M, N), lambda b: (b, 0, 0)),
        out_shape=jax.ShapeDtypeStruct((B, M, N), BF16),
        compiler_params=_params("parallel"),
        name="memkv",
    )(mem, w_kv)


def _layer_norm(z, g, b):
    mu = jnp.mean(z, axis=-1, keepdims=True)
    zc = z - mu
    var = jnp.mean(zc * zc, axis=-1, keepdims=True)
    return zc * lax.rsqrt(var + LN_EPS) * g + b


def _post_kernel(x_ref, mix_ref, mq_ref, kv_ref, wo_ref, g_ref, b_ref, wr_ref, br_ref,
                 x1_ref, x1s_ref, idx_ref, gate_ref, *, alpha):
    tm = x_ref.shape[0]
    mq = mq_ref[...]
    kv = kv_ref[0]
    km = kv[:, :MEM_WIDTH]
    vm = kv[:, MEM_WIDTH:]
    lane = lax.broadcasted_iota(jnp.int32, (tm, MEM_WIDTH), 1)
    scale = MEM_HEAD_DIM ** -0.5
    mo = jnp.zeros((tm, MEM_WIDTH), F32)
    for hd in range(MEM_HEADS):
        head = (lane >> 6) == hd
        qh = jnp.where(head, mq * scale, 0.0).astype(BF16)
        s = lax.dot_general(qh, km, _NT, preferred_element_type=F32)
        m = jnp.max(s, axis=1, keepdims=True)
        p = jnp.exp(s - m)
        l = jnp.sum(p, axis=1, keepdims=True)
        oh = jnp.dot(p.astype(BF16), vm, preferred_element_type=F32) / l
        mo = jnp.where(head, oh, mo)

    y = jnp.dot(mix_ref[...].astype(BF16), wo_ref[:MIX_WIDTH, :], preferred_element_type=F32)
    y = y + jnp.dot(mo.astype(BF16), wo_ref[MIX_WIDTH:, :], preferred_element_type=F32)
    x1 = _layer_norm(alpha * x_ref[...] + y, g_ref[...], b_ref[...])
    x1_ref[...] = x1
    _store_subrows(x1s_ref, x1)

    logits = jnp.dot(x1, wr_ref[...], precision=lax.Precision.HIGHEST,
                     preferred_element_type=F32) + br_ref[...]
    lane_e = lax.broadcasted_iota(jnp.int32, (tm, LANES), 1)
    g = jnp.where(lane_e < N_EXPERTS, logits, -jnp.inf)
    idx_out = jnp.zeros((tm, LANES), jnp.int32)
    val_out = jnp.full((tm, LANES), -jnp.inf, F32)
    for kk in range(TOP_K):
        mx = jnp.max(g, axis=1, keepdims=True)
        idx = jnp.min(jnp.where(g == mx, lane_e, LANES), axis=1, keepdims=True)
        idx_out = jnp.where(lane_e == kk, idx, idx_out)
        val_out = jnp.where(lane_e == kk, mx, val_out)
        g = jnp.where(lane_e == idx, -jnp.inf, g)
    vmax = jnp.max(val_out, axis=1, keepdims=True)
    ev = jnp.exp(val_out - vmax)
    idx_ref[...] = idx_out
    gate_ref[...] = ev / jnp.sum(ev, axis=1, keepdims=True)


def _post_mixer(x2, mix2, h2, kv, w_o_bf16, ln_g, ln_b, w_router, b_router, seq_len, alpha):
    T, D = x2.shape
    tm = POST_ROWS
    N = h2.shape[1]
    M = kv.shape[1]
    mq_col = (N - MEM_WIDTH) // MEM_WIDTH
    tiles_per_seq = seq_len // tm
    wr = jnp.zeros((D, LANES), F32).at[:, :N_EXPERTS].set(w_router)
    br = jnp.zeros((1, LANES), F32).at[0, :N_EXPERTS].set(b_router)
    row = lambda n: pl.BlockSpec((tm, n), lambda i: (i, 0))
    full = lambda a, b: pl.BlockSpec((a, b), lambda i: (0, 0))
    return pl.pallas_call(
        functools.partial(_post_kernel, alpha=alpha),
        grid=(T // tm,),
        in_specs=[row(D), row(MIX_WIDTH),
                  pl.BlockSpec((tm, MEM_WIDTH), lambda i: (i, mq_col)),
                  pl.BlockSpec((1, M, 2 * MEM_WIDTH), lambda i: (i // tiles_per_seq, 0, 0)),
                  full(D, D), full(1, D), full(1, D), full(D, LANES), full(1, LANES)],
        out_specs=[row(D), pl.BlockSpec((tm * (D // LANES), LANES), lambda i: (i, 0)),
                   row(LANES), row(LANES)],
        out_shape=[jax.ShapeDtypeStruct((T, D), F32),
                   jax.ShapeDtypeStruct((T * (D // LANES), LANES), F32),
                   jax.ShapeDtypeStruct((T, LANES), jnp.int32),
                   jax.ShapeDtypeStruct((T, LANES), F32)],
        compiler_params=_params("parallel"),
        name="post_mixer",
    )(x2, mix2, h2, kv, w_o_bf16, ln_g.reshape(1, D), ln_b.reshape(1, D), wr, br)


def _sc_mesh():
    return plsc.VectorSubcoreMesh(core_axis_name="core", subcore_axis_name="subcore")


def _sc_scatter_rows(rows, idx, n_rep, n_out):
    R, W = rows.shape

    @functools.partial(pl.kernel, out_type=jax.ShapeDtypeStruct((n_out, W), rows.dtype),
                       mesh=_sc_mesh(), scratch_types=[])
    def scatter(x_hbm, i_hbm, o_hbm):
        def body(x_vmem, i_vmem):
            for r in range(n_rep):
                pltpu.sync_copy(x_vmem, o_hbm.at[i_vmem.at[r]])

        pltpu.emit_pipeline(
            body, grid=(R // SC_WINDOW,),
            in_specs=[pl.BlockSpec((SC_WINDOW, W), lambda i: (i, 0)),
                      pl.BlockSpec((SC_IDX_ROWS, SC_WINDOW), lambda i: (0, i))],
            out_specs=[], core_axis_name=("core", "subcore"),
            dimension_semantics=(pltpu.PARALLEL,), trace_scopes=False)(x_hbm, i_hbm)

    return scatter(rows, idx)


def _sc_gather_rows(table, idx):
    n = idx.shape[0]
    W = table.shape[1]

    @functools.partial(pl.kernel, out_type=jax.ShapeDtypeStruct((n, W), table.dtype),
                       mesh=_sc_mesh(), scratch_types=[])
    def gather(t_hbm, i_hbm, o_hbm):
        def body(i_vmem, o_vmem):
            pltpu.sync_copy(t_hbm.at[i_vmem.at[0]], o_vmem)

        pltpu.emit_pipeline(
            body, grid=(n // SC_WINDOW,),
            in_specs=[pl.BlockSpec((1, SC_WINDOW), lambda i: (0, i))],
            out_specs=[pl.BlockSpec((SC_WINDOW, W), lambda i: (i, 0))],
            core_axis_name=("core", "subcore"),
            dimension_semantics=(pltpu.PARALLEL,), trace_scopes=False)(i_hbm, o_hbm)

    return gather(table, idx.reshape(1, n))


def _route(top_idx, n_tokens):
    rb = MOE_ROWS
    tk = n_tokens * TOP_K
    flat_e = top_idx.reshape(-1)
    onehot = (flat_e[:, None] == jnp.arange(N_EXPERTS, dtype=jnp.int32)[None, :]).astype(jnp.int32)
    csum = jnp.cumsum(onehot, axis=0)
    rank = jnp.sum(onehot * csum, axis=1) - 1
    counts = csum[-1]
    padded = (counts + rb - 1) // rb * rb
    pend = jnp.cumsum(padded)
    pstart = pend - padded
    dest = (pstart[flat_e] + rank).astype(jnp.int32).reshape(n_tokens, TOP_K)
    n_blocks = tk // rb + N_EXPERTS
    block_e = jnp.minimum(
        jnp.searchsorted(pend, jnp.arange(n_blocks, dtype=jnp.int32) * rb, side='right'),
        N_EXPERTS - 1).astype(jnp.int32)
    n_used = (pend[-1] // rb).astype(jnp.int32).reshape(1)
    return dest, block_e, n_used


def _dispatch(x1s, dest, n_rows, sub):
    T = dest.shape[0]
    c = jnp.arange(sub, dtype=jnp.int32)
    idx = (dest.T[:, :, None] * sub + c[None, None, :]).reshape(TOP_K, T * sub)
    idx = jnp.concatenate([idx, jnp.zeros((SC_IDX_ROWS - TOP_K, T * sub), jnp.int32)], axis=0)
    return _sc_scatter_rows(x1s, idx, TOP_K, n_rows * sub)


def _expert_kernel(be_ref, nu_ref, x_ref, wg_ref, bg_ref, wu_ref, bu_ref, wd_ref, bd_ref,
                   y_ref, wgb, wub, wdb):
    i = pl.program_id(0)
    prev = be_ref[jnp.maximum(i - 1, 0)]

    @pl.when((i == 0) | (be_ref[i] != prev))
    def _():
        wgb[...] = wg_ref[0, 0].astype(BF16)
        wub[...] = wu_ref[0, 0].astype(BF16)
        wdb[...] = wd_ref[0, 0].astype(BF16)

    @pl.when(i < nu_ref[0])
    def _():
        sub = wgb.shape[0] // LANES
        xb = _load_subrows(x_ref, x_ref.shape[0] // sub, sub).astype(BF16)
        gate = jnp.dot(xb, wgb[...], preferred_element_type=F32) + bg_ref[0, 0]
        gate = jnp.minimum(gate, SWIGLU_LIMIT)
        up = jnp.dot(xb, wub[...], preferred_element_type=F32) + bu_ref[0, 0]
        up = jnp.clip(up, -SWIGLU_LIMIT, SWIGLU_LIMIT)
        hid = gate * _sigmoid(SWIGLU_ALPHA * gate) * (up + 1.0)
        y = jnp.dot(hid.astype(BF16), wdb[...], preferred_element_type=F32) + bd_ref[0, 0]
        _store_subrows(y_ref, y)


def _experts(xs, block_e, n_used, layer, w_gate, b_gate, w_up, b_up, w_down, b_down):
    rb = MOE_ROWS
    n_blocks = block_e.shape[0]
    E, D, F = w_gate.shape[1:]
    sub = D // LANES
    wspec = lambda a, b: pl.BlockSpec((1, 1, a, b), lambda i, be, nu: (layer, be[i], 0, 0))
    live = lambda i, be, nu: (jnp.where(i < nu[0], i, n_blocks), 0)
    grid_spec = pltpu.PrefetchScalarGridSpec(
        num_scalar_prefetch=2,
        grid=(n_blocks,),
        in_specs=[pl.BlockSpec((rb * sub, LANES), live),
                  wspec(D, F), wspec(1, F), wspec(D, F), wspec(1, F), wspec(F, D), wspec(1, D)],
        out_specs=pl.BlockSpec((rb * sub, LANES), live),
        scratch_shapes=[pltpu.VMEM((D, F), BF16), pltpu.VMEM((D, F), BF16),
                        pltpu.VMEM((F, D), BF16)],
    )
    depth = w_gate.shape[0]
    return pl.pallas_call(
        _expert_kernel,
        grid_spec=grid_spec,
        out_shape=jax.ShapeDtypeStruct(xs.shape, F32),
        compiler_params=_params("arbitrary"),
        name="experts",
    )(block_e, n_used, xs, w_gate, b_gate.reshape(depth, E, 1, F),
      w_up, b_up.reshape(depth, E, 1, F), w_down, b_down.reshape(depth, E, 1, D))


def _combine_kernel(x1_ref, gate_ref, y_ref, g_ref, b_ref, o_ref, *, alpha):
    tm, D = x1_ref.shape
    sub = D // LANES
    gates = gate_ref[...]
    f = jnp.zeros(x1_ref.shape, F32)
    for kk in range(TOP_K):
        f = f + gates[:, kk:kk + 1] * _load_subrows(y_ref, tm, sub, kk * sub, TOP_K * sub)
    o_ref[...] = _layer_norm(alpha * x1_ref[...] + f, g_ref[...], b_ref[...])


def _combine(x1, gates, dest, y_rows, ln_g, ln_b, alpha):
    T, D = x1.shape
    sub = D // LANES
    tm = COMBINE_ROWS
    c = jnp.arange(sub, dtype=jnp.int32)
    idx = (dest[:, :, None] * sub + c[None, None, :]).reshape(T * TOP_K * sub)
    yg = _sc_gather_rows(y_rows, idx)
    row = lambda n: pl.BlockSpec((tm, n), lambda i: (i, 0))
    full = lambda a, b: pl.BlockSpec((a, b), lambda i: (0, 0))
    return pl.pallas_call(
        functools.partial(_combine_kernel, alpha=alpha),
        grid=(T // tm,),
        in_specs=[row(D), row(LANES), pl.BlockSpec((tm * TOP_K * sub, LANES), lambda i: (i, 0)),
                  full(1, D), full(1, D)],
        out_specs=row(D),
        out_shape=jax.ShapeDtypeStruct((T, D), F32),
        compiler_params=_params("parallel"),
        name="combine",
    )(x1, gates, yg, ln_g.reshape(1, D), ln_b.reshape(1, D))


def kernel(x, mem, w_in_hgrn, hgrn_lb_logits, hgrn_norm_g, w_in_moba, w_mem_kv, w_o,
           ln_mix_g, ln_mix_b, w_router, b_router, w_gate, b_gate, w_up, b_up,
           w_down, b_down, ln_ffn_g, ln_ffn_b):
    B, S, D = x.shape
    T = B * S
    depth = w_o.shape[0]
    alpha = (2 * depth) ** 0.25

    p_lb = jax.nn.softmax(hgrn_lb_logits.astype(F32), axis=0)
    lower_bounds = jnp.cumsum(p_lb, axis=0) - p_lb[0]

    x2 = x.reshape(T, D)
    for layer in range(depth):
        j = layer // 2
        if layer % 2 == 0:
            (h2,) = _inproj(x2, w_in_hgrn[j].astype(BF16), for_moba=False)
            mix = _hgrn_mixer(h2.reshape(B, S, -1), lower_bounds[j], hgrn_norm_g[j])
        else:
            h2, hb2, q6 = _inproj(x2, w_in_moba[j].astype(BF16), for_moba=True)
            mix = _moba_sparse_mixer(h2.reshape(B, S, -1), hb2.reshape(B, S, -1), q6)
        kv = _memkv(mem, w_mem_kv[layer])
        x1, x1s, top_idx, gates = _post_mixer(
            x2, mix.reshape(T, MIX_WIDTH), h2, kv, w_o[layer].astype(BF16),
            ln_mix_g[layer], ln_mix_b[layer], w_router[layer], b_router[layer], S, alpha)
        dest, block_e, n_used = _route(top_idx[:, :TOP_K], T)
        xs = _dispatch(x1s, dest, (block_e.shape[0] + 1) * MOE_ROWS, D // LANES)
        y_rows = _experts(xs, block_e, n_used, layer, w_gate, b_gate, w_up, b_up, w_down, b_down)
        x2 = _combine(x1, gates, dest, y_rows, ln_ffn_g[layer], ln_ffn_b[layer], alpha)
    return x2.reshape(B, S, D)
```

```python
import functools
import math

import jax
import jax.numpy as jnp
from jax import lax
from jax.experimental import pallas as pl
from jax.experimental.pallas import tpu as pltpu
from jax.experimental.pallas import tpu_sc as plsc

MIX_WIDTH = 768
MEM_HEADS = 4
MEM_HEAD_DIM = 64
MEM_WIDTH = MEM_HEADS * MEM_HEAD_DIM
HGRN_HEADS = 6
HGRN_DK = 128
MOBA_HEADS = 12
MOBA_HEAD_DIM = 64
MOBA_BLOCK = 256
MOBA_TOPK = 3
N_EXPERTS = 32
TOP_K = 4
SWIGLU_ALPHA = 1.702
SWIGLU_LIMIT = 7.0
LN_EPS = 1e-5
RMS_EPS = 1e-6

LANES = 128
SUBLANES = 8
VMEM_LIMIT_BYTES = 56 * 1024 * 1024

INPROJ_ROWS = 512
HGRN_CHUNK = 64
HGRN_ROWS = 512
POST_ROWS = 256
MOBA_TILE = 256
MOBA_TILE_GROUP = 4
MOBA_PLACE_GROUP = 8
MOE_ROWS = 256
COMBINE_ROWS = 256
SC_WINDOW = 128
SC_IDX_ROWS = 8

BF16 = jnp.bfloat16
F32 = jnp.float32
NEG_BIG = -1e30

_NT = (((1,), (1,)), ((), ()))
_TN = (((0,), (0,)), ((), ()))


def _alibi_slope_list(n):
    def pow2(m):
        start = 2.0 ** (-(2.0 ** -(math.log2(m) - 3)))
        return [start ** (i + 1) for i in range(m)]
    if math.log2(n).is_integer():
        return pow2(n)
    c = 2 ** math.floor(math.log2(n))
    return pow2(c) + _alibi_slope_list(2 * c)[0::2][:n - c]


def _sigmoid(x):
    return 1.0 / (1.0 + jnp.exp(-x))


def _params(*sem):
    return pltpu.CompilerParams(dimension_semantics=sem, vmem_limit_bytes=VMEM_LIMIT_BYTES)


def _store_subrows(ref, value, first=0, stride=None):
    sub = value.shape[1] // LANES
    stride = stride or sub
    for c in range(sub):
        ref[pl.ds(first + c, value.shape[0], stride=stride), :] = value[:, c * LANES:(c + 1) * LANES]


def _load_subrows(ref, rows, sub, first=0, stride=None):
    stride = stride or sub
    return jnp.concatenate(
        [ref[pl.ds(first + c, rows, stride=stride), :] for c in range(sub)], axis=1)


def _inproj_kernel(x_ref, w_ref, h_ref, *moba_refs):
    h = jnp.dot(x_ref[...].astype(BF16), w_ref[...], preferred_element_type=F32)
    h_ref[...] = h
    if moba_refs:
        hb_ref, q6_ref = moba_refs
        hb_ref[...] = h.astype(BF16)
        for p in range(q6_ref.shape[0]):
            q6_ref[p] = h[:, p * LANES:(p + 1) * LANES]


def _inproj(x2, w_bf16, for_moba):
    T, D = x2.shape
    N = w_bf16.shape[1]
    tm = INPROJ_ROWS
    out_shape = [jax.ShapeDtypeStruct((T, N), F32)]
    out_specs = [pl.BlockSpec((tm, N), lambda i: (i, 0))]
    if for_moba:
        NP = MOBA_HEADS // 2
        out_shape += [jax.ShapeDtypeStruct((T, N), BF16), jax.ShapeDtypeStruct((NP, T, LANES), F32)]
        out_specs += [pl.BlockSpec((tm, N), lambda i: (i, 0)),
                      pl.BlockSpec((NP, tm, LANES), lambda i: (0, i, 0))]
    return pl.pallas_call(
        _inproj_kernel,
        grid=(T // tm,),
        in_specs=[pl.BlockSpec((tm, D), lambda i: (i, 0)),
                  pl.BlockSpec((D, N), lambda i: (0, 0))],
        out_specs=out_specs,
        out_shape=out_shape,
        compiler_params=_params("parallel"),
        name="inproj",
    )(x2, w_bf16)


def _cumsum_rows(x, row):
    n = x.shape[0]
    sh = 1
    while sh < n:
        x = x + jnp.where(row >= sh, pltpu.roll(x, sh, 0), 0.0)
        sh *= 2
    return x


def _bcast_row(a, group, r):
    n = a.shape[0]
    a3 = a.reshape(n // group, group, LANES)
    return jnp.broadcast_to(a3[:, r:r + 1, :], a3.shape).reshape(n, LANES)


def _hgrn_chunk(qr, fr, v, gr, lb, ng, e_sum, st_t):
    C = qr.shape[0]
    row = lax.broadcasted_iota(jnp.int32, (C, LANES), 0)
    rr = lax.broadcasted_iota(jnp.int32, (C, C), 0)
    cc = lax.broadcasted_iota(jnp.int32, (C, C), 1)

    q = qr * _sigmoid(qr)
    forget = lb + (1.0 - lb) * _sigmoid(fr)
    k = 1.0 - forget
    G = _cumsum_rows(jnp.log(forget), row)

    parts = []
    for s in range(SUBLANES):
        Gs = _bcast_row(G, SUBLANES, s)
        ks = _bcast_row(k, SUBLANES, s)
        parts.append((q * ks * jnp.exp(jnp.minimum(G - Gs, 0.0))).astype(BF16))
    a_diag = jnp.dot(jnp.concatenate(parts, axis=1), e_sum, preferred_element_type=F32)
    A = jnp.where(((rr >> 3) == (cc >> 3)) & (cc <= rr), a_diag, 0.0)

    m = SUBLANES
    while m < C:
        lg = int(math.log2(m))
        Gr = _bcast_row(G, 2 * m, m - 1)
        second = ((row >> lg) & 1) == 1
        qm = jnp.where(second, q * jnp.exp(jnp.minimum(G - Gr, 0.0)), 0.0)
        km = jnp.where(second, 0.0, k * jnp.exp(jnp.minimum(Gr - G, 0.0)))
        am = lax.dot_general(qm.astype(BF16), km.astype(BF16), _NT, preferred_element_type=F32)
        A = A + jnp.where((rr >> (lg + 1)) == (cc >> (lg + 1)), am, 0.0)
        m *= 2

    vb = v.astype(BF16)
    o = jnp.dot(A.astype(BF16), vb, preferred_element_type=F32)
    o = o + lax.dot_general((q * jnp.exp(G)).astype(BF16), st_t.astype(BF16), _NT,
                            preferred_element_type=F32)
    g_end = G[C - 1:C, :]
    kd = (k * jnp.exp(g_end - G)).astype(BF16)
    st_new = st_t * jnp.exp(g_end) + lax.dot_general(vb, kd, _TN, preferred_element_type=F32)

    ms = jnp.mean(o * o, axis=-1, keepdims=True)
    out = o * lax.rsqrt(ms + RMS_EPS) * ng * _sigmoid(gr)
    return out, st_new


def _hgrn_kernel(q_ref, f_ref, i_ref, g_ref, lb_ref, ng_ref, e_ref, o_ref, st_ref, *, chunk):
    @pl.when(pl.program_id(2) == 0)
    def _():
        st_ref[...] = jnp.zeros_like(st_ref)

    lb = lb_ref[0]
    ng = ng_ref[...]
    e_sum = e_ref[...]
    n_chunks = q_ref.shape[1] // chunk
    for c in range(n_chunks):
        sl = pl.ds(c * chunk, chunk)
        out, st_new = _hgrn_chunk(q_ref[0, sl, :], f_ref[0, sl, :], i_ref[0, sl, :],
                                  g_ref[0, sl, :], lb, ng, e_sum, st_ref[...])
        st_ref[...] = st_new
        o_ref[0, sl, :] = out


def _hgrn_mixer(h3, lb, norm_g):
    B, S, _ = h3.shape
    ts = min(HGRN_ROWS, S)
    C = HGRN_CHUNK
    H = HGRN_HEADS
    e_sum = (jnp.arange(SUBLANES * LANES)[:, None] // LANES == jnp.arange(C)[None, :] % SUBLANES
             ).astype(BF16)
    col = lambda off: pl.BlockSpec((1, ts, LANES), lambda b, h, s, off=off: (b, s, off + h))
    return pl.pallas_call(
        functools.partial(_hgrn_kernel, chunk=C),
        grid=(B, H, S // ts),
        in_specs=[col(0), col(H), col(2 * H), col(3 * H),
                  pl.BlockSpec((1, 1, LANES), lambda b, h, s: (h, 0, 0)),
                  pl.BlockSpec((1, LANES), lambda b, h, s: (0, 0)),
                  pl.BlockSpec((SUBLANES * LANES, C), lambda b, h, s: (0, 0))],
        out_specs=pl.BlockSpec((1, ts, LANES), lambda b, h, s: (b, s, h)),
        out_shape=jax.ShapeDtypeStruct((B, S, MIX_WIDTH), F32),
        scratch_shapes=[pltpu.VMEM((HGRN_DK, HGRN_DK), F32)],
        compiler_params=_params("parallel", "parallel", "arbitrary"),
        name="hgrn",
    )(h3, h3, h3, h3, lb.reshape(H, 1, LANES), norm_g.reshape(1, LANES), e_sum)


def _moba_kernel(sl_ref, q_ref, kf_ref, kb_ref, vb_ref, o_ref, kmean_ref):
    BLK = MOBA_BLOCK
    p = pl.program_id(1)
    i = pl.program_id(2)

    @pl.when(i == 0)
    def _():
        kmean_ref[...] = jnp.zeros_like(kmean_ref)

    qf = q_ref[0]
    own = pl.ds(pl.multiple_of(i * BLK, BLK), BLK)
    k_own = kb_ref[0, own, :]
    v_own = vb_ref[0, own, :]

    lane = lax.broadcasted_iota(jnp.int32, (BLK, LANES), 1)
    rr = lax.broadcasted_iota(jnp.int32, (BLK, BLK), 0)
    cc = lax.broadcasted_iota(jnp.int32, (BLK, BLK), 1)
    rel = (rr - cc).astype(F32)
    scale = MOBA_HEAD_DIM ** -0.5

    outs = []
    for hh in range(2):
        head = (lane < MOBA_HEAD_DIM) if hh == 0 else (lane >= MOBA_HEAD_DIM)
        slope = sl_ref[2 * p + hh]
        qh = jnp.where(head, qf, 0.0)

        gate = lax.dot_general(qh, kmean_ref[...], _NT, precision=lax.Precision.HIGHEST,
                               preferred_element_type=F32)
        past = lane < i
        g = jnp.where(past, gate, -jnp.inf)
        sel = jnp.zeros((BLK, LANES), jnp.bool_)
        for _ in range(MOBA_TOPK):
            mx = jnp.max(g, axis=1, keepdims=True)
            idx = jnp.min(jnp.where(g == mx, lane, LANES), axis=1, keepdims=True)
            pick = lane == idx
            sel = sel | pick
            g = jnp.where(pick, -jnp.inf, g)
        sel_bias = jnp.where(sel & past, 0.0, NEG_BIG)
        q_aug = jnp.concatenate([(qh * scale).astype(BF16), sel_bias.astype(BF16)], axis=1)

        bias = -slope * rel
        s = lax.dot_general((qh * scale).astype(BF16), k_own, _NT, preferred_element_type=F32)
        s = s + jnp.where(cc <= rr, bias, -jnp.inf)
        m0 = jnp.max(s, axis=1, keepdims=True)
        p0 = jnp.exp(s - m0)
        l0 = jnp.sum(p0, axis=1, keepdims=True)
        acc0 = jnp.dot(p0.astype(BF16), v_own, preferred_element_type=F32)

        def body(n, carry):
            m, l, acc = carry
            blk = pl.ds(pl.multiple_of(n * BLK, BLK), BLK)
            onehot = jnp.where(lane == n, 1.0, 0.0).astype(BF16)
            k_aug = jnp.concatenate([kb_ref[0, blk, :], onehot], axis=1)
            sn = lax.dot_general(q_aug, k_aug, _NT, preferred_element_type=F32)
            sn = sn + (bias - slope * ((i - n) * BLK).astype(F32))
            m_new = jnp.maximum(m, jnp.max(sn, axis=1, keepdims=True))
            a = jnp.exp(m - m_new)
            pn = jnp.exp(sn - m_new)
            l = a * l + jnp.sum(pn, axis=1, keepdims=True)
            acc = a * acc + jnp.dot(pn.astype(BF16), vb_ref[0, blk, :],
                                    preferred_element_type=F32)
            return m_new, l, acc

        _, l, acc = lax.fori_loop(0, i, body, (m0, l0, acc0))
        outs.append(acc / l)

    o_ref[0] = jnp.where(lane < MOBA_HEAD_DIM, outs[0], outs[1])
    kmean_ref[pl.ds(i, 1), :] = jnp.mean(kf_ref[0], axis=0, keepdims=True)


def _moba_mixer(h3, hb3):
    B, S, _ = h3.shape
    BLK = MOBA_BLOCK
    NP = MOBA_HEADS // 2
    slopes = jnp.asarray(_alibi_slope_list(MOBA_HEADS), F32)
    grid_spec = pltpu.PrefetchScalarGridSpec(
        num_scalar_prefetch=1,
        grid=(B, NP, S // BLK),
        in_specs=[pl.BlockSpec((1, BLK, LANES), lambda b, p, i, sl: (b, i, p)),
                  pl.BlockSpec((1, BLK, LANES), lambda b, p, i, sl: (b, i, NP + p)),
                  pl.BlockSpec((1, S, LANES), lambda b, p, i, sl: (b, 0, NP + p)),
                  pl.BlockSpec((1, S, LANES), lambda b, p, i, sl: (b, 0, 2 * NP + p))],
        out_specs=pl.BlockSpec((1, BLK, LANES), lambda b, p, i, sl: (b, i, p)),
        scratch_shapes=[pltpu.VMEM((LANES, LANES), F32)],
    )
    return pl.pallas_call(
        _moba_kernel,
        grid_spec=grid_spec,
        out_shape=jax.ShapeDtypeStruct((B, S, MIX_WIDTH), F32),
        compiler_params=_params("parallel", "parallel", "arbitrary"),
        name="moba",
    )(slopes, h3, h3, hb3, hb3)


def _moba_top3(qh, kmean, i, lane):
    gate = lax.dot_general(qh, kmean, _NT, precision=lax.Precision.HIGHEST,
                           preferred_element_type=F32)
    g = jnp.where(lane < i, gate, -jnp.inf)
    picks = []
    for _ in range(MOBA_TOPK):
        mx = jnp.max(g, axis=1, keepdims=True)
        idx = jnp.min(jnp.where(g == mx, lane, LANES), axis=1, keepdims=True)
        picks.append((idx, (mx > -jnp.inf) & (idx < i)))
        g = jnp.where(lane == idx, -jnp.inf, g)
    return picks


def _moba_select_kernel(q_ref, kf_ref, sr_ref, cnt_ref, kmean_ref):
    BLK = MOBA_BLOCK
    i = pl.program_id(2)

    @pl.when(i == 0)
    def _():
        kmean_ref[...] = jnp.zeros_like(kmean_ref)

    qf = q_ref[0]
    lane = lax.broadcasted_iota(jnp.int32, (BLK, LANES), 1)
    rr = lax.broadcasted_iota(jnp.int32, (BLK, BLK), 0)
    cc = lax.broadcasted_iota(jnp.int32, (BLK, BLK), 1)
    before = jnp.where(cc < rr, 1.0, 0.0).astype(BF16)
    out = jnp.zeros((BLK, LANES), jnp.int32)
    counts = []
    for hh in range(2):
        head = (lane >> 6) == hh
        picks = _moba_top3(jnp.where(head, qf, 0.0), kmean_ref[...], i, lane)
        chosen = jnp.zeros((BLK, LANES), F32)
        for idx, valid in picks:
            chosen = chosen + jnp.where((lane == idx) & valid, 1.0, 0.0)
        earlier = jnp.dot(before, chosen.astype(BF16), preferred_element_type=F32)
        for j, (idx, valid) in enumerate(picks):
            rep = hh * MOBA_TOPK + j
            rank = jnp.sum(jnp.where(lane == idx, earlier, 0.0), axis=1, keepdims=True)
            out = jnp.where(lane == rep, jnp.where(valid, idx, -1), out)
            out = jnp.where(lane == SUBLANES + rep, rank.astype(jnp.int32), out)
        total = earlier[BLK - 1:BLK, :] + chosen[BLK - 1:BLK, :]
        counts.append(jnp.broadcast_to(total, (SUBLANES, LANES)))
    sr_ref[0, 0] = out
    cnt_ref[0, 0] = jnp.concatenate(counts, axis=0)
    kmean_ref[pl.ds(i, 1), :] = jnp.mean(kf_ref[0], axis=0, keepdims=True)


def _moba_dest_kernel(sr_ref, tab_ref, idx_ref, *, trash):
    BLK = MOBA_BLOCK
    blk = sr_ref[0, 0]
    tab = tab_ref[0, 0]
    lane = lax.broadcasted_iota(jnp.int32, (BLK, LANES), 1)
    d = jnp.full((BLK, LANES), float(trash), F32)
    for rep in range(2 * MOBA_TOPK):
        hh = rep // MOBA_TOPK
        sel = blk[:, rep:rep + 1]
        rank = blk[:, SUBLANES + rep:SUBLANES + rep + 1]
        start = jnp.sum(jnp.where(lane == sel, tab[hh:hh + 1, :], 0.0), axis=1, keepdims=True)
        dest = jnp.where(sel >= 0, start + rank.astype(F32), float(trash))
        d = jnp.where(lane == rep, dest, d)
    idx_ref[...] = d.T[:SC_IDX_ROWS, :].astype(jnp.int32)


def _moba_tile_kernel(tb_ref, tp_ref, th_ref, tn_ref, nu_ref, sl_ref, q_ref, k_ref, v_ref, o_ref):
    t = pl.program_id(0)

    @pl.when(t < nu_ref[0])
    def _():
        tq = q_ref.shape[0]
        hh = th_ref[t]
        slope = sl_ref[2 * tp_ref[t] + hh]
        lane = lax.broadcasted_iota(jnp.int32, (tq, LANES), 1)
        head = (lane >> 6) == hh
        q = jnp.where(head, q_ref[...] * (MOBA_HEAD_DIM ** -0.5), 0.0).astype(BF16)
        s = lax.dot_general(q, k_ref[0], _NT, preferred_element_type=F32)
        kpos = lax.broadcasted_iota(jnp.int32, s.shape, 1).astype(F32)
        s = s + slope * kpos
        m = jnp.max(s, axis=1, keepdims=True)
        p = jnp.exp(s - m)
        l = jnp.sum(p, axis=1, keepdims=True)
        o = jnp.dot(p.astype(BF16), v_ref[0], preferred_element_type=F32)
        half = lane & (MOBA_HEAD_DIM - 1)
        stats = jnp.where(half == 0, m, jnp.where(half == 1, l, 0.0))
        o_ref[...] = jnp.where(head, o, stats)


def _moba_merge_kernel(sl_ref, q_ref, k_ref, v_ref, sr_ref, pg_ref, o_ref):
    BLK = MOBA_BLOCK
    p = pl.program_id(1)
    i = pl.program_id(2)
    qf = q_ref[0]
    k_own = k_ref[0]
    v_own = v_ref[0]
    blk = sr_ref[0, 0]
    lane = lax.broadcasted_iota(jnp.int32, (BLK, LANES), 1)
    rr = lax.broadcasted_iota(jnp.int32, (BLK, BLK), 0)
    cc = lax.broadcasted_iota(jnp.int32, (BLK, BLK), 1)
    rel = (rr - cc).astype(F32)
    row = lax.broadcasted_iota(jnp.int32, (BLK, 1), 0)
    outs = []
    for hh in range(2):
        head = (lane >> 6) == hh
        slope = sl_ref[2 * p + hh]
        qh = jnp.where(head, qf * (MOBA_HEAD_DIM ** -0.5), 0.0).astype(BF16)
        s = lax.dot_general(qh, k_own, _NT, preferred_element_type=F32)
        s = s + jnp.where(cc <= rr, -slope * rel, -jnp.inf)
        m = jnp.max(s, axis=1, keepdims=True)
        pr = jnp.exp(s - m)
        ms = [m]
        ls = [jnp.sum(pr, axis=1, keepdims=True)]
        accs = [jnp.dot(pr.astype(BF16), v_own, preferred_element_type=F32)]
        stat_lane = MOBA_HEAD_DIM * (1 - hh)
        for j in range(MOBA_TOPK):
            rep = hh * MOBA_TOPK + j
            part = pg_ref[rep, 0]
            sel = blk[:, rep:rep + 1]
            valid = sel >= 0
            shift = -slope * ((i - sel) * BLK + row).astype(F32)
            ms.append(jnp.where(valid, part[:, stat_lane:stat_lane + 1] + shift, -jnp.inf))
            ls.append(jnp.where(valid, part[:, stat_lane + 1:stat_lane + 2], 0.0))
            accs.append(jnp.where(valid, part, 0.0))
        top = functools.reduce(jnp.maximum, ms)
        ws = [jnp.exp(mk - top) for mk in ms]
        den = sum(w * lk for w, lk in zip(ws, ls))
        num = sum(w * ak for w, ak in zip(ws, accs))
        outs.append(num / den)
    o_ref[0] = jnp.where(lane < MOBA_HEAD_DIM, outs[0], outs[1])


def _moba_sparse_mixer(h3, hb3, q6):
    B, S, _ = h3.shape
    T = B * S
    BLK = MOBA_BLOCK
    TQ = MOBA_TILE
    NB = S // BLK
    NP = MOBA_HEADS // 2
    n_rep = 2 * MOBA_TOPK
    slopes = jnp.asarray(_alibi_slope_list(MOBA_HEADS), F32)
    step = lambda *rest: pl.BlockSpec((1, 1) + rest, lambda b, p, i: (b, p, i) + (0,) * (len(rest) - 1))

    selrank, counts = pl.pallas_call(
        _moba_select_kernel,
        grid=(B, NP, NB),
        in_specs=[pl.BlockSpec((1, BLK, LANES), lambda b, p, i: (b, i, p)),
                  pl.BlockSpec((1, BLK, LANES), lambda b, p, i: (b, i, NP + p))],
        out_specs=[step(BLK, LANES), step(2 * SUBLANES, LANES)],
        out_shape=[jax.ShapeDtypeStruct((B, NP, S, LANES), jnp.int32),
                   jax.ShapeDtypeStruct((B, NP, NB * 2 * SUBLANES, LANES), F32)],
        scratch_shapes=[pltpu.VMEM((LANES, LANES), F32)],
        compiler_params=_params("parallel", "parallel", "arbitrary"),
        name="moba_select",
    )(h3, h3)

    cnt = counts.reshape(B, NP, NB, 2, SUBLANES, LANES)[:, :, :, :, 0, :NB].astype(jnp.int32)
    base = jnp.cumsum(cnt, axis=2) - cnt
    total = jnp.sum(cnt, axis=2)
    padded = (total + TQ - 1) // TQ * TQ
    pend = jnp.cumsum(padded.reshape(-1))
    seg_start = (pend - padded.reshape(-1)).reshape(B, NP, 1, 2, NB)
    table = jnp.zeros((B, NP, NB, SUBLANES, LANES), F32).at[:, :, :, :2, :NB].set(
        (seg_start + base).astype(F32)).reshape(B, NP, NB * SUBLANES, LANES)
    n_seg = B * NP * 2 * NB
    max_tiles = (T * NP * n_rep) // TQ + n_seg
    tile_seg = jnp.minimum(
        jnp.searchsorted(pend, jnp.arange(max_tiles, dtype=jnp.int32) * TQ, side='right'),
        n_seg - 1).astype(jnp.int32)
    tile_n = tile_seg % NB
    tile_h = (tile_seg // NB) % 2
    tile_p = (tile_seg // (2 * NB)) % NP
    tile_b = tile_seg // (2 * NB * NP)
    n_used = (pend[-1] // TQ).astype(jnp.int32).reshape(1)
    trash = max_tiles * TQ

    idx = pl.pallas_call(
        functools.partial(_moba_dest_kernel, trash=trash),
        grid=(B, NP, NB),
        in_specs=[step(BLK, LANES), step(SUBLANES, LANES)],
        out_specs=pl.BlockSpec((SC_IDX_ROWS, BLK), lambda b, p, i: (0, p * (T // BLK) + b * NB + i)),
        out_shape=jax.ShapeDtypeStruct((SC_IDX_ROWS, NP * T), jnp.int32),
        compiler_params=_params("parallel", "parallel", "parallel"),
        name="moba_dest",
    )(selrank, table)

    n_rows = (max_tiles + 1) * TQ
    qs = _sc_scatter_rows(q6.reshape(NP * T, LANES), idx, n_rep, n_rows)

    live = lambda t, tb, tp, th, tn, nu, sl: (jnp.where(t < nu[0], t, max_tiles), 0)
    kv_blk = lambda off: pl.BlockSpec(
        (1, BLK, LANES), lambda t, tb, tp, th, tn, nu, sl, off=off: (tb[t], tn[t], off + tp[t]))
    part = pl.pallas_call(
        _moba_tile_kernel,
        grid_spec=pltpu.PrefetchScalarGridSpec(
            num_scalar_prefetch=6,
            grid=(max_tiles,),
            in_specs=[pl.BlockSpec((TQ, LANES), live), kv_blk(NP), kv_blk(2 * NP)],
            out_specs=pl.BlockSpec((TQ, LANES), live)),
        out_shape=jax.ShapeDtypeStruct((n_rows, LANES), F32),
        compiler_params=_params("arbitrary"),
        name="moba_tiles",
    )(tile_b, tile_p, tile_h, tile_n, n_used, slopes, qs, hb3, hb3)

    pg = _sc_gather_rows(part, idx[:n_rep].reshape(-1)).reshape(n_rep, NP, T, LANES)

    own = lambda off: pl.BlockSpec((1, BLK, LANES), lambda b, p, i, sl, off=off: (b, i, off + p))
    return pl.pallas_call(
        _moba_merge_kernel,
        grid_spec=pltpu.PrefetchScalarGridSpec(
            num_scalar_prefetch=1,
            grid=(B, NP, NB),
            in_specs=[own(0), own(NP), own(2 * NP),
                      pl.BlockSpec((1, 1, BLK, LANES), lambda b, p, i, sl: (b, p, i, 0)),
                      pl.BlockSpec((n_rep, 1, BLK, LANES), lambda b, p, i, sl: (0, p, b * NB + i, 0))],
            out_specs=pl.BlockSpec((1, BLK, LANES), lambda b, p, i, sl: (b, i, p))),
        out_shape=jax.ShapeDtypeStruct((B, S, MIX_WIDTH), F32),
        compiler_params=_params("parallel", "parallel", "parallel"),
        name="moba_merge",
    )(slopes, h3, hb3, hb3, selrank, pg)


def _moba_pick_kernel(q_ref, kf_ref, sr_ref, cnt_ref, kmean_ref):
    BLK = MOBA_BLOCK
    nbp = kmean_ref.shape[0]
    i = pl.program_id(2)

    @pl.when(i == 0)
    def _():
        kmean_ref[...] = jnp.zeros_like(kmean_ref)

    qf = q_ref[0]
    lane = lax.broadcasted_iota(jnp.int32, (BLK, LANES), 1)
    nrow = lax.broadcasted_iota(jnp.int32, (nbp, BLK), 0)
    orow = lax.broadcasted_iota(jnp.int32, (2 * SUBLANES, BLK), 0)
    lane_c = lax.broadcasted_iota(jnp.int32, (nbp, LANES), 1)
    qa = lax.broadcasted_iota(jnp.int32, (BLK, BLK), 0)
    qb = lax.broadcasted_iota(jnp.int32, (BLK, BLK), 1)
    before = jnp.where(qa < qb, 1.0, 0.0).astype(BF16)
    out = jnp.zeros((2 * SUBLANES, BLK), jnp.int32)
    cnt = jnp.zeros((nbp, LANES), F32)
    for hh in range(2):
        qh = jnp.where((lane >> 6) == hh, qf, 0.0)
        gate = lax.dot_general(kmean_ref[...], qh, _NT, precision=lax.Precision.HIGHEST,
                               preferred_element_type=F32)
        g = jnp.where(nrow < i, gate, -jnp.inf)
        picks = []
        for _ in range(MOBA_TOPK):
            mx = jnp.max(g, axis=0, keepdims=True)
            idx = jnp.min(jnp.where(g == mx, nrow, nbp), axis=0, keepdims=True)
            picks.append((idx, (mx > -jnp.inf) & (idx < i)))
            g = jnp.where(nrow == idx, -jnp.inf, g)
        chosen = jnp.zeros((nbp, BLK), F32)
        for idx, valid in picks:
            chosen = chosen + jnp.where((nrow == idx) & valid, 1.0, 0.0)
        earlier = jnp.dot(chosen.astype(BF16), before, preferred_element_type=F32)
        for j, (idx, valid) in enumerate(picks):
            rep = hh * MOBA_TOPK + j
            rank = jnp.sum(jnp.where(nrow == idx, earlier, 0.0), axis=0, keepdims=True)
            out = jnp.where(orow == rep, jnp.where(valid, idx, -1), out)
            out = jnp.where(orow == SUBLANES + rep, rank.astype(jnp.int32), out)
        cnt = jnp.where(lane_c == hh, jnp.sum(chosen, axis=1, keepdims=True), cnt)
    sr_ref[0, 0] = out
    cnt_ref[0, 0] = cnt
    kmean_ref[pl.ds(i, 1), :] = jnp.mean(kf_ref[0], axis=0, keepdims=True)


def _moba_place_kernel(sr_ref, tab_ref, idx_ref, *, group, spare):
    BLK = MOBA_BLOCK
    n_rep = 2 * MOBA_TOPK
    b = pl.program_id(0)
    p = pl.program_id(1)
    ig = pl.program_id(2)
    nrow = lax.broadcasted_iota(jnp.int32, (LANES, BLK), 0)
    orow = lax.broadcasted_iota(jnp.int32, (SC_IDX_ROWS, BLK), 0)
    qpos = lax.broadcasted_iota(jnp.int32, (1, BLK), 1)
    for g in range(group):
        blk = sr_ref[0, 0, :, g * BLK:(g + 1) * BLK]
        tab_t = tab_ref[0, 0, g * SUBLANES:(g + 1) * SUBLANES, :].T
        early = jnp.minimum(ig * group + g, MOBA_TOPK - 1)
        out = jnp.zeros((SC_IDX_ROWS, BLK), jnp.int32)
        for rep in range(n_rep):
            hh = rep // MOBA_TOPK
            sel = blk[rep:rep + 1, :]
            rank = blk[SUBLANES + rep:SUBLANES + rep + 1, :]
            start = jnp.sum(jnp.where(nrow == sel, tab_t[:, hh:hh + 1], 0.0), axis=0, keepdims=True)
            unused = spare + (((b * pl.num_programs(1) + p) * MOBA_TOPK + early) * n_rep + rep) * BLK
            dest = jnp.where(sel >= 0, start.astype(jnp.int32) + rank, unused + qpos)
            out = jnp.where(orow == rep, dest, out)
        idx_ref[:, g * BLK:(g + 1) * BLK] = out


def _moba_group_kernel(tb_ref, tp_ref, th_ref, tn_ref, nu_ref, sl_ref, q_ref, *refs, group):
    k_refs, v_refs, o_ref = refs[:group], refs[group:2 * group], refs[2 * group]
    t = pl.program_id(0)
    tq = q_ref.shape[0] // group

    @pl.when(t * group < nu_ref[0])
    def _():
        lane = lax.broadcasted_iota(jnp.int32, (tq, LANES), 1)
        kpos = lax.broadcasted_iota(jnp.int32, (tq, MOBA_BLOCK), 1)
        for g in range(group):
            tt = t * group + g
            hh = th_ref[tt]
            slope = sl_ref[2 * tp_ref[tt] + hh]
            head = (lane >> 6) == hh
            rows = pl.ds(g * tq, tq)
            q = jnp.where(head, q_ref[rows, :] * (MOBA_HEAD_DIM ** -0.5), 0.0).astype(BF16)
            s = lax.dot_general(q, k_refs[g][0], _NT, preferred_element_type=F32)
            s = s + slope * (kpos + tn_ref[tt] * MOBA_BLOCK).astype(F32)
            m = jnp.max(s, axis=1, keepdims=True)
            pr = jnp.exp(s - m)
            l = jnp.sum(pr, axis=1, keepdims=True)
            o = jnp.dot(pr.astype(BF16), v_refs[g][0], preferred_element_type=F32) / l
            o_ref[rows, :] = jnp.where(head, o, m + jnp.log(l))


def _moba_join_kernel(sl_ref, q_ref, k_ref, v_ref, pg_ref, o_ref):
    BLK = MOBA_BLOCK
    p = pl.program_id(1)
    i = pl.program_id(2)
    qf = q_ref[0]
    k_own = k_ref[0]
    v_own = v_ref[0]
    lane = lax.broadcasted_iota(jnp.int32, (BLK, LANES), 1)
    rr = lax.broadcasted_iota(jnp.int32, (BLK, BLK), 0)
    cc = lax.broadcasted_iota(jnp.int32, (BLK, BLK), 1)
    key_pos = (cc + i * BLK).astype(F32)
    outs = []
    for hh in range(2):
        head = (lane >> 6) == hh
        slope = sl_ref[2 * p + hh]
        qh = jnp.where(head, qf * (MOBA_HEAD_DIM ** -0.5), 0.0).astype(BF16)
        s = lax.dot_general(qh, k_own, _NT, preferred_element_type=F32)
        s = jnp.where(cc <= rr, s + slope * key_pos, -jnp.inf)
        m = jnp.max(s, axis=1, keepdims=True)
        pr = jnp.exp(s - m)
        l = jnp.sum(pr, axis=1, keepdims=True)
        lses = [m + jnp.log(l)]
        vals = [jnp.dot(pr.astype(BF16), v_own, preferred_element_type=F32) / l]
        for j in range(MOBA_TOPK):
            part = pg_ref[hh * MOBA_TOPK + j, 0]
            has_block = j < i
            lses.append(jnp.where(has_block, pltpu.roll(part, MOBA_HEAD_DIM, 1), -jnp.inf))
            vals.append(jnp.where(has_block, part, 0.0))
        top = functools.reduce(jnp.maximum, lses)
        ws = [jnp.exp(x - top) for x in lses]
        outs.append(sum(w * v for w, v in zip(ws, vals)) / sum(ws))
    o_ref[0] = jnp.where(lane < MOBA_HEAD_DIM, outs[0], outs[1])


def _moba_mixer_v2(h3, hb3, q6):
    B, S, _ = h3.shape
    T = B * S
    BLK = MOBA_BLOCK
    TQ = MOBA_TILE
    G = MOBA_TILE_GROUP
    NB = S // BLK
    GI = math.gcd(NB, MOBA_PLACE_GROUP)
    nbp = -(-NB // SUBLANES) * SUBLANES
    NP = MOBA_HEADS // 2
    n_rep = 2 * MOBA_TOPK
    slopes = jnp.asarray(_alibi_slope_list(MOBA_HEADS), F32)

    selrank, counts = pl.pallas_call(
        _moba_pick_kernel,
        grid=(B, NP, NB),
        in_specs=[pl.BlockSpec((1, BLK, LANES), lambda b, p, i: (b, i, p)),
                  pl.BlockSpec((1, BLK, LANES), lambda b, p, i: (b, i, NP + p))],
        out_specs=[pl.BlockSpec((1, 1, 2 * SUBLANES, BLK), lambda b, p, i: (b, p, 0, i)),
                   pl.BlockSpec((1, 1, nbp, LANES), lambda b, p, i: (b, p, i, 0))],
        out_shape=[jax.ShapeDtypeStruct((B, NP, 2 * SUBLANES, S), jnp.int32),
                   jax.ShapeDtypeStruct((B, NP, NB * nbp, LANES), F32)],
        scratch_shapes=[pltpu.VMEM((nbp, LANES), F32)],
        compiler_params=_params("parallel", "parallel", "arbitrary"),
        name="moba_pick",
    )(h3, h3)

    cnt = counts.reshape(B, NP, NB, nbp, LANES)[:, :, :, :NB, :2].astype(jnp.int32)
    cnt = cnt.transpose(0, 1, 2, 4, 3)
    base = jnp.cumsum(cnt, axis=2) - cnt
    total = jnp.sum(cnt, axis=2)
    padded = (total + TQ - 1) // TQ * TQ
    pend = jnp.cumsum(padded.reshape(-1))
    seg_start = (pend - padded.reshape(-1)).reshape(B, NP, 1, 2, NB)
    table = jnp.zeros((B, NP, NB, SUBLANES, LANES), F32).at[:, :, :, :2, :NB].set(
        (seg_start + base).astype(F32)).reshape(B, NP, NB * SUBLANES, LANES)
    n_seg = B * NP * 2 * NB
    max_tiles = -(-((T * NP * n_rep) // TQ + n_seg) // G) * G
    first_row = jnp.arange(max_tiles, dtype=jnp.int32) * TQ
    tile_seg = jnp.minimum(jnp.sum(pend[None, :] <= first_row[:, None], axis=1), n_seg - 1)
    tile_n = (tile_seg % NB).astype(jnp.int32)
    tile_h = ((tile_seg // NB) % 2).astype(jnp.int32)
    tile_p = ((tile_seg // (2 * NB)) % NP).astype(jnp.int32)
    tile_b = (tile_seg // (2 * NB * NP)).astype(jnp.int32)
    n_used = (pend[-1] // TQ).astype(jnp.int32).reshape(1)
    spare = max_tiles * TQ
    n_rows = spare + max(B * NP * MOBA_TOPK * n_rep * BLK, G * TQ)

    idx = pl.pallas_call(
        functools.partial(_moba_place_kernel, group=GI, spare=spare),
        grid=(B, NP, NB // GI),
        in_specs=[pl.BlockSpec((1, 1, 2 * SUBLANES, GI * BLK), lambda b, p, i: (b, p, 0, i)),
                  pl.BlockSpec((1, 1, GI * SUBLANES, LANES), lambda b, p, i: (b, p, i, 0))],
        out_specs=pl.BlockSpec((SC_IDX_ROWS, GI * BLK),
                               lambda b, p, i: (0, p * (T // (GI * BLK)) + b * (NB // GI) + i)),
        out_shape=jax.ShapeDtypeStruct((SC_IDX_ROWS, NP * T), jnp.int32),
        compiler_params=_params("parallel", "parallel", "parallel"),
        name="moba_place",
    )(selrank, table)

    qs = _sc_scatter_rows(q6.reshape(NP * T, LANES), idx, n_rep, n_rows)

    live = lambda t, tb, tp, th, tn, nu, sl: (jnp.where(t * G < nu[0], t, max_tiles // G), 0)
    kv_blk = lambda off, g: pl.BlockSpec(
        (1, BLK, LANES),
        lambda t, tb, tp, th, tn, nu, sl: (tb[t * G + g], tn[t * G + g], off + tp[t * G + g]))
    part = pl.pallas_call(
        functools.partial(_moba_group_kernel, group=G),
        grid_spec=pltpu.PrefetchScalarGridSpec(
            num_scalar_prefetch=6,
            grid=(max_tiles // G,),
            in_specs=([pl.BlockSpec((G * TQ, LANES), live)]
                      + [kv_blk(NP, g) for g in range(G)] + [kv_blk(2 * NP, g) for g in range(G)]),
            out_specs=pl.BlockSpec((G * TQ, LANES), live)),
        out_shape=jax.ShapeDtypeStruct((n_rows, LANES), F32),
        compiler_params=_params("arbitrary"),
        name="moba_tiles",
    )(tile_b, tile_p, tile_h, tile_n, n_used, slopes, qs, *([hb3] * (2 * G)))

    pg = _sc_gather_rows(part, idx[:n_rep].reshape(-1)).reshape(n_rep, NP, T, LANES)

    own = lambda off: pl.BlockSpec((1, BLK, LANES), lambda b, p, i, sl, off=off: (b, i, off + p))
    return pl.pallas_call(
        _moba_join_kernel,
        grid_spec=pltpu.PrefetchScalarGridSpec(
            num_scalar_prefetch=1,
            grid=(B, NP, NB),
            in_specs=[own(0), own(NP), own(2 * NP),
                      pl.BlockSpec((n_rep, 1, BLK, LANES), lambda b, p, i, sl: (0, p, b * NB + i, 0))],
            out_specs=pl.BlockSpec((1, BLK, LANES), lambda b, p, i, sl: (b, i, p))),
        out_shape=jax.ShapeDtypeStruct((B, S, MIX_WIDTH), F32),
        compiler_params=_params("parallel", "parallel", "parallel"),
        name="moba_join",
    )(slopes, h3, hb3, hb3, pg)


def _memkv_kernel(mem_ref, w_ref, kv_ref):
    kv_ref[0] = jnp.dot(mem_ref[0].astype(BF16), w_ref[...].astype(BF16),
                        preferred_element_type=F32).astype(BF16)


def _memkv(mem, w_kv):
    B, M, D = mem.shape
    N = w_kv.shape[1]
    return pl.pallas_call(
        _memkv_kernel,
        grid=(B,),
        in_specs=[pl.BlockSpec((1, M, D), lambda b: (b, 0, 0)),
                  pl.BlockSpec((D, N), lambda b: (0, 0))],
        out_specs=pl.BlockSpec((1, M, N), lambda b: (b, 0, 0)),
        out_shape=jax.ShapeDtypeStruct((B, M, N), BF16),
        compiler_params=_params("parallel"),
        name="memkv",
    )(mem, w_kv)


def _layer_norm(z, g, b):
    mu = jnp.mean(z, axis=-1, keepdims=True)
    zc = z - mu
    var = jnp.mean(zc * zc, axis=-1, keepdims=True)
    return zc * lax.rsqrt(var + LN_EPS) * g + b


def _post_kernel(x_ref, mix_ref, mq_ref, kv_ref, wo_ref, g_ref, b_ref, wr_ref, br_ref,
                 x1_ref, x1s_ref, idx_ref, gate_ref, *, alpha):
    tm = x_ref.shape[0]
    mq = mq_ref[...]
    kv = kv_ref[0]
    km = kv[:, :MEM_WIDTH]
    vm = kv[:, MEM_WIDTH:]
    lane = lax.broadcasted_iota(jnp.int32, (tm, MEM_WIDTH), 1)
    scale = MEM_HEAD_DIM ** -0.5
    mo = jnp.zeros((tm, MEM_WIDTH), F32)
    for hd in range(MEM_HEADS):
        head = (lane >> 6) == hd
        qh = jnp.where(head, mq * scale, 0.0).astype(BF16)
        s = lax.dot_general(qh, km, _NT, preferred_element_type=F32)
        m = jnp.max(s, axis=1, keepdims=True)
        p = jnp.exp(s - m)
        l = jnp.sum(p, axis=1, keepdims=True)
        oh = jnp.dot(p.astype(BF16), vm, preferred_element_type=F32) / l
        mo = jnp.where(head, oh, mo)

    y = jnp.dot(mix_ref[...].astype(BF16), wo_ref[:MIX_WIDTH, :], preferred_element_type=F32)
    y = y + jnp.dot(mo.astype(BF16), wo_ref[MIX_WIDTH:, :], preferred_element_type=F32)
    x1 = _layer_norm(alpha * x_ref[...] + y, g_ref[...], b_ref[...])
    x1_ref[...] = x1
    _store_subrows(x1s_ref, x1)

    logits = jnp.dot(x1, wr_ref[...], precision=lax.Precision.HIGHEST,
                     preferred_element_type=F32) + br_ref[...]
    lane_e = lax.broadcasted_iota(jnp.int32, (tm, LANES), 1)
    g = jnp.where(lane_e < N_EXPERTS, logits, -jnp.inf)
    idx_out = jnp.zeros((tm, LANES), jnp.int32)
    val_out = jnp.full((tm, LANES), -jnp.inf, F32)
    for kk in range(TOP_K):
        mx = jnp.max(g, axis=1, keepdims=True)
        idx = jnp.min(jnp.where(g == mx, lane_e, LANES), axis=1, keepdims=True)
        idx_out = jnp.where(lane_e == kk, idx, idx_out)
        val_out = jnp.where(lane_e == kk, mx, val_out)
        g = jnp.where(lane_e == idx, -jnp.inf, g)
    vmax = jnp.max(val_out, axis=1, keepdims=True)
    ev = jnp.exp(val_out - vmax)
    idx_ref[...] = idx_out
    gate_ref[...] = ev / jnp.sum(ev, axis=1, keepdims=True)


def _post_mixer(x2, mix2, h2, kv, w_o_bf16, ln_g, ln_b, w_router, b_router, seq_len, alpha):
    T, D = x2.shape
    tm = POST_ROWS
    N = h2.shape[1]
    M = kv.shape[1]
    mq_col = (N - MEM_WIDTH) // MEM_WIDTH
    tiles_per_seq = seq_len // tm
    wr = jnp.zeros((D, LANES), F32).at[:, :N_EXPERTS].set(w_router)
    br = jnp.zeros((1, LANES), F32).at[0, :N_EXPERTS].set(b_router)
    row = lambda n: pl.BlockSpec((tm, n), lambda i: (i, 0))
    full = lambda a, b: pl.BlockSpec((a, b), lambda i: (0, 0))
    return pl.pallas_call(
        functools.partial(_post_kernel, alpha=alpha),
        grid=(T // tm,),
        in_specs=[row(D), row(MIX_WIDTH),
                  pl.BlockSpec((tm, MEM_WIDTH), lambda i: (i, mq_col)),
                  pl.BlockSpec((1, M, 2 * MEM_WIDTH), lambda i: (i // tiles_per_seq, 0, 0)),
                  full(D, D), full(1, D), full(1, D), full(D, LANES), full(1, LANES)],
        out_specs=[row(D), pl.BlockSpec((tm * (D // LANES), LANES), lambda i: (i, 0)),
                   row(LANES), row(LANES)],
        out_shape=[jax.ShapeDtypeStruct((T, D), F32),
                   jax.ShapeDtypeStruct((T * (D // LANES), LANES), F32),
                   jax.ShapeDtypeStruct((T, LANES), jnp.int32),
                   jax.ShapeDtypeStruct((T, LANES), F32)],
        compiler_params=_params("parallel"),
        name="post_mixer",
    )(x2, mix2, h2, kv, w_o_bf16, ln_g.reshape(1, D), ln_b.reshape(1, D), wr, br)


def _sc_mesh():
    return plsc.VectorSubcoreMesh(core_axis_name="core", subcore_axis_name="subcore")


def _sc_scatter_rows(rows, idx, n_rep, n_out):
    R, W = rows.shape

    @functools.partial(pl.kernel, out_type=jax.ShapeDtypeStruct((n_out, W), rows.dtype),
                       mesh=_sc_mesh(), scratch_types=[])
    def scatter(x_hbm, i_hbm, o_hbm):
        def body(x_vmem, i_vmem):
            for r in range(n_rep):
                pltpu.sync_copy(x_vmem, o_hbm.at[i_vmem.at[r]])

        pltpu.emit_pipeline(
            body, grid=(R // SC_WINDOW,),
            in_specs=[pl.BlockSpec((SC_WINDOW, W), lambda i: (i, 0)),
                      pl.BlockSpec((SC_IDX_ROWS, SC_WINDOW), lambda i: (0, i))],
            out_specs=[], core_axis_name=("core", "subcore"),
            dimension_semantics=(pltpu.PARALLEL,), trace_scopes=False)(x_hbm, i_hbm)

    return scatter(rows, idx)


def _sc_gather_rows(table, idx):
    n = idx.shape[0]
    W = table.shape[1]

    @functools.partial(pl.kernel, out_type=jax.ShapeDtypeStruct((n, W), table.dtype),
                       mesh=_sc_mesh(), scratch_types=[])
    def gather(t_hbm, i_hbm, o_hbm):
        def body(i_vmem, o_vmem):
            pltpu.sync_copy(t_hbm.at[i_vmem.at[0]], o_vmem)

        pltpu.emit_pipeline(
            body, grid=(n // SC_WINDOW,),
            in_specs=[pl.BlockSpec((1, SC_WINDOW), lambda i: (0, i))],
            out_specs=[pl.BlockSpec((SC_WINDOW, W), lambda i: (i, 0))],
            core_axis_name=("core", "subcore"),
            dimension_semantics=(pltpu.PARALLEL,), trace_scopes=False)(i_hbm, o_hbm)

    return gather(table, idx.reshape(1, n))


def _route(top_idx, n_tokens):
    rb = MOE_ROWS
    tk = n_tokens * TOP_K
    flat_e = top_idx.reshape(-1)
    onehot = (flat_e[:, None] == jnp.arange(N_EXPERTS, dtype=jnp.int32)[None, :]).astype(jnp.int32)
    csum = jnp.cumsum(onehot, axis=0)
    rank = jnp.sum(onehot * csum, axis=1) - 1
    counts = csum[-1]
    padded = (counts + rb - 1) // rb * rb
    pend = jnp.cumsum(padded)
    pstart = pend - padded
    dest = (pstart[flat_e] + rank).astype(jnp.int32).reshape(n_tokens, TOP_K)
    n_blocks = tk // rb + N_EXPERTS
    first_row = jnp.arange(n_blocks, dtype=jnp.int32) * rb
    block_e = jnp.minimum(jnp.sum(pend[None, :] <= first_row[:, None], axis=1),
                          N_EXPERTS - 1).astype(jnp.int32)
    n_used = (pend[-1] // rb).astype(jnp.int32).reshape(1)
    return dest, block_e, n_used


def _dispatch(x1s, dest, n_rows, sub):
    T = dest.shape[0]
    c = jnp.arange(sub, dtype=jnp.int32)
    idx = (dest.T[:, :, None] * sub + c[None, None, :]).reshape(TOP_K, T * sub)
    idx = jnp.concatenate([idx, jnp.zeros((SC_IDX_ROWS - TOP_K, T * sub), jnp.int32)], axis=0)
    return _sc_scatter_rows(x1s, idx, TOP_K, n_rows * sub)


def _expert_kernel(be_ref, nu_ref, x_ref, wg_ref, bg_ref, wu_ref, bu_ref, wd_ref, bd_ref,
                   y_ref, wgb, wub, wdb):
    i = pl.program_id(0)
    prev = be_ref[jnp.maximum(i - 1, 0)]

    @pl.when((i == 0) | (be_ref[i] != prev))
    def _():
        wgb[...] = wg_ref[0, 0].astype(BF16)
        wub[...] = wu_ref[0, 0].astype(BF16)
        wdb[...] = wd_ref[0, 0].astype(BF16)

    @pl.when(i < nu_ref[0])
    def _():
        sub = wgb.shape[0] // LANES
        xb = _load_subrows(x_ref, x_ref.shape[0] // sub, sub).astype(BF16)
        gate = jnp.dot(xb, wgb[...], preferred_element_type=F32) + bg_ref[0, 0]
        gate = jnp.minimum(gate, SWIGLU_LIMIT)
        up = jnp.dot(xb, wub[...], preferred_element_type=F32) + bu_ref[0, 0]
        up = jnp.clip(up, -SWIGLU_LIMIT, SWIGLU_LIMIT)
        hid = gate * _sigmoid(SWIGLU_ALPHA * gate) * (up + 1.0)
        y = jnp.dot(hid.astype(BF16), wdb[...], preferred_element_type=F32) + bd_ref[0, 0]
        _store_subrows(y_ref, y)


def _experts(xs, block_e, n_used, layer, w_gate, b_gate, w_up, b_up, w_down, b_down):
    rb = MOE_ROWS
    n_blocks = block_e.shape[0]
    E, D, F = w_gate.shape[1:]
    sub = D // LANES
    wspec = lambda a, b: pl.BlockSpec((1, 1, a, b), lambda i, be, nu: (layer, be[i], 0, 0))
    live = lambda i, be, nu: (jnp.where(i < nu[0], i, n_blocks), 0)
    grid_spec = pltpu.PrefetchScalarGridSpec(
        num_scalar_prefetch=2,
        grid=(n_blocks,),
        in_specs=[pl.BlockSpec((rb * sub, LANES), live),
                  wspec(D, F), wspec(1, F), wspec(D, F), wspec(1, F), wspec(F, D), wspec(1, D)],
        out_specs=pl.BlockSpec((rb * sub, LANES), live),
        scratch_shapes=[pltpu.VMEM((D, F), BF16), pltpu.VMEM((D, F), BF16),
                        pltpu.VMEM((F, D), BF16)],
    )
    depth = w_gate.shape[0]
    return pl.pallas_call(
        _expert_kernel,
        grid_spec=grid_spec,
        out_shape=jax.ShapeDtypeStruct(xs.shape, F32),
        compiler_params=_params("arbitrary"),
        name="experts",
    )(block_e, n_used, xs, w_gate, b_gate.reshape(depth, E, 1, F),
      w_up, b_up.reshape(depth, E, 1, F), w_down, b_down.reshape(depth, E, 1, D))


def _combine_kernel(x1_ref, gate_ref, y_ref, g_ref, b_ref, o_ref, *, alpha):
    tm, D = x1_ref.shape
    sub = D // LANES
    gates = gate_ref[...]
    f = jnp.zeros(x1_ref.shape, F32)
    for kk in range(TOP_K):
        f = f + gates[:, kk:kk + 1] * _load_subrows(y_ref, tm, sub, kk * sub, TOP_K * sub)
    o_ref[...] = _layer_norm(alpha * x1_ref[...] + f, g_ref[...], b_ref[...])


def _combine(x1, gates, dest, y_rows, ln_g, ln_b, alpha):
    T, D = x1.shape
    sub = D // LANES
    tm = COMBINE_ROWS
    c = jnp.arange(sub, dtype=jnp.int32)
    idx = (dest[:, :, None] * sub + c[None, None, :]).reshape(T * TOP_K * sub)
    yg = _sc_gather_rows(y_rows, idx)
    row = lambda n: pl.BlockSpec((tm, n), lambda i: (i, 0))
    full = lambda a, b: pl.BlockSpec((a, b), lambda i: (0, 0))
    return pl.pallas_call(
        functools.partial(_combine_kernel, alpha=alpha),
        grid=(T // tm,),
        in_specs=[row(D), row(LANES), pl.BlockSpec((tm * TOP_K * sub, LANES), lambda i: (i, 0)),
                  full(1, D), full(1, D)],
        out_specs=row(D),
        out_shape=jax.ShapeDtypeStruct((T, D), F32),
        compiler_params=_params("parallel"),
        name="combine",
    )(x1, gates, yg, ln_g.reshape(1, D), ln_b.reshape(1, D))


def kernel(x, mem, w_in_hgrn, hgrn_lb_logits, hgrn_norm_g, w_in_moba, w_mem_kv, w_o,
           ln_mix_g, ln_mix_b, w_router, b_router, w_gate, b_gate, w_up, b_up,
           w_down, b_down, ln_ffn_g, ln_ffn_b):
    B, S, D = x.shape
    T = B * S
    depth = w_o.shape[0]
    alpha = (2 * depth) ** 0.25

    p_lb = jax.nn.softmax(hgrn_lb_logits.astype(F32), axis=0)
    lower_bounds = jnp.cumsum(p_lb, axis=0) - p_lb[0]

    x2 = x.reshape(T, D)
    for layer in range(depth):
        j = layer // 2
        if layer % 2 == 0:
            (h2,) = _inproj(x2, w_in_hgrn[j].astype(BF16), for_moba=False)
            mix = _hgrn_mixer(h2.reshape(B, S, -1), lower_bounds[j], hgrn_norm_g[j])
        else:
            h2, hb2, q6 = _inproj(x2, w_in_moba[j].astype(BF16), for_moba=True)
            mix = _moba_mixer_v2(h2.reshape(B, S, -1), hb2.reshape(B, S, -1), q6)
        kv = _memkv(mem, w_mem_kv[layer])
        x1, x1s, top_idx, gates = _post_mixer(
            x2, mix.reshape(T, MIX_WIDTH), h2, kv, w_o[layer].astype(BF16),
            ln_mix_g[layer], ln_mix_b[layer], w_router[layer], b_router[layer], S, alpha)
        dest, block_e, n_used = _route(top_idx[:, :TOP_K], T)
        xs = _dispatch(x1s, dest, (block_e.shape[0] + 1) * MOE_ROWS, D // LANES)
        y_rows = _experts(xs, block_e, n_used, layer, w_gate, b_gate, w_up, b_up, w_down, b_down)
        x2 = _combine(x1, gates, dest, y_rows, ln_ffn_g[layer], ln_ffn_b[layer], alpha)
    return x2.reshape(B, S, D)
```

```python
import functools
import math

import jax
import jax.numpy as jnp
from jax import lax
from jax.experimental import pallas as pl
from jax.experimental.pallas import tpu as pltpu
from jax.experimental.pallas import tpu_sc as plsc

MIX_WIDTH = 768
MEM_HEADS = 4
MEM_HEAD_DIM = 64
MEM_WIDTH = MEM_HEADS * MEM_HEAD_DIM
HGRN_HEADS = 6
HGRN_DK = 128
MOBA_HEADS = 12
MOBA_HEAD_DIM = 64
MOBA_BLOCK = 256
MOBA_TOPK = 3
N_EXPERTS = 32
TOP_K = 4
SWIGLU_ALPHA = 1.702
SWIGLU_LIMIT = 7.0
LN_EPS = 1e-5
RMS_EPS = 1e-6

LANES = 128
SUBLANES = 8
VMEM_LIMIT_BYTES = 56 * 1024 * 1024

INPROJ_ROWS = 512
HGRN_CHUNK = 64
HGRN_ROWS = 512
POST_ROWS = 256
MOBA_TILE = 256
MOBA_TILE_GROUP = 4
MOBA_PLACE_GROUP = 8
MOE_ROWS = 256
COMBINE_ROWS = 256
SC_WINDOW = 128
SC_IDX_ROWS = 8

BF16 = jnp.bfloat16
F32 = jnp.float32
NEG_BIG = -1e30

_NT = (((1,), (1,)), ((), ()))
_TN = (((0,), (0,)), ((), ()))


def _alibi_slope_list(n):
    def pow2(m):
        start = 2.0 ** (-(2.0 ** -(math.log2(m) - 3)))
        return [start ** (i + 1) for i in range(m)]
    if math.log2(n).is_integer():
        return pow2(n)
    c = 2 ** math.floor(math.log2(n))
    return pow2(c) + _alibi_slope_list(2 * c)[0::2][:n - c]


def _sigmoid(x):
    return 1.0 / (1.0 + jnp.exp(-x))


def _params(*sem):
    return pltpu.CompilerParams(dimension_semantics=sem, vmem_limit_bytes=VMEM_LIMIT_BYTES)


def _store_subrows(ref, value, first=0, stride=None):
    sub = value.shape[1] // LANES
    stride = stride or sub
    for c in range(sub):
        ref[pl.ds(first + c, value.shape[0], stride=stride), :] = value[:, c * LANES:(c + 1) * LANES]


def _load_subrows(ref, rows, sub, first=0, stride=None):
    stride = stride or sub
    return jnp.concatenate(
        [ref[pl.ds(first + c, rows, stride=stride), :] for c in range(sub)], axis=1)


def _inproj_kernel(x_ref, w_ref, h_ref, *moba_refs):
    h = jnp.dot(x_ref[...].astype(BF16), w_ref[...], preferred_element_type=F32)
    h_ref[...] = h
    if moba_refs:
        hb_ref, q6_ref = moba_refs
        hb_ref[...] = h.astype(BF16)
        for p in range(q6_ref.shape[0]):
            q6_ref[p] = h[:, p * LANES:(p + 1) * LANES]


def _inproj(x2, w_bf16, for_moba):
    T, D = x2.shape
    N = w_bf16.shape[1]
    tm = INPROJ_ROWS
    out_shape = [jax.ShapeDtypeStruct((T, N), F32)]
    out_specs = [pl.BlockSpec((tm, N), lambda i: (i, 0))]
    if for_moba:
        NP = MOBA_HEADS // 2
        out_shape += [jax.ShapeDtypeStruct((T, N), BF16), jax.ShapeDtypeStruct((NP, T, LANES), F32)]
        out_specs += [pl.BlockSpec((tm, N), lambda i: (i, 0)),
                      pl.BlockSpec((NP, tm, LANES), lambda i: (0, i, 0))]
    return pl.pallas_call(
        _inproj_kernel,
        grid=(T // tm,),
        in_specs=[pl.BlockSpec((tm, D), lambda i: (i, 0)),
                  pl.BlockSpec((D, N), lambda i: (0, 0))],
        out_specs=out_specs,
        out_shape=out_shape,
        compiler_params=_params("parallel"),
        name="inproj",
    )(x2, w_bf16)


def _cumsum_rows(x, row):
    n = x.shape[0]
    sh = 1
    while sh < n:
        x = x + jnp.where(row >= sh, pltpu.roll(x, sh, 0), 0.0)
        sh *= 2
    return x


def _bcast_row(a, group, r):
    n = a.shape[0]
    a3 = a.reshape(n // group, group, LANES)
    return jnp.broadcast_to(a3[:, r:r + 1, :], a3.shape).reshape(n, LANES)


def _hgrn_chunk(qr, fr, v, gr, lb, ng, e_sum, st_t):
    C = qr.shape[0]
    row = lax.broadcasted_iota(jnp.int32, (C, LANES), 0)
    rr = lax.broadcasted_iota(jnp.int32, (C, C), 0)
    cc = lax.broadcasted_iota(jnp.int32, (C, C), 1)

    q = qr * _sigmoid(qr)
    forget = lb + (1.0 - lb) * _sigmoid(fr)
    k = 1.0 - forget
    G = _cumsum_rows(jnp.log(forget), row)

    parts = []
    for s in range(SUBLANES):
        Gs = _bcast_row(G, SUBLANES, s)
        ks = _bcast_row(k, SUBLANES, s)
        parts.append((q * ks * jnp.exp(jnp.minimum(G - Gs, 0.0))).astype(BF16))
    a_diag = jnp.dot(jnp.concatenate(parts, axis=1), e_sum, preferred_element_type=F32)
    A = jnp.where(((rr >> 3) == (cc >> 3)) & (cc <= rr), a_diag, 0.0)

    m = SUBLANES
    while m < C:
        lg = int(math.log2(m))
        Gr = _bcast_row(G, 2 * m, m - 1)
        second = ((row >> lg) & 1) == 1
        qm = jnp.where(second, q * jnp.exp(jnp.minimum(G - Gr, 0.0)), 0.0)
        km = jnp.where(second, 0.0, k * jnp.exp(jnp.minimum(Gr - G, 0.0)))
        am = lax.dot_general(qm.astype(BF16), km.astype(BF16), _NT, preferred_element_type=F32)
        A = A + jnp.where((rr >> (lg + 1)) == (cc >> (lg + 1)), am, 0.0)
        m *= 2

    vb = v.astype(BF16)
    o = jnp.dot(A.astype(BF16), vb, preferred_element_type=F32)
    o = o + lax.dot_general((q * jnp.exp(G)).astype(BF16), st_t.astype(BF16), _NT,
                            preferred_element_type=F32)
    g_end = G[C - 1:C, :]
    kd = (k * jnp.exp(g_end - G)).astype(BF16)
    st_new = st_t * jnp.exp(g_end) + lax.dot_general(vb, kd, _TN, preferred_element_type=F32)

    ms = jnp.mean(o * o, axis=-1, keepdims=True)
    out = o * lax.rsqrt(ms + RMS_EPS) * ng * _sigmoid(gr)
    return out, st_new


def _hgrn_kernel(q_ref, f_ref, i_ref, g_ref, lb_ref, ng_ref, e_ref, o_ref, st_ref, *, chunk):
    @pl.when(pl.program_id(2) == 0)
    def _():
        st_ref[...] = jnp.zeros_like(st_ref)

    lb = lb_ref[0]
    ng = ng_ref[...]
    e_sum = e_ref[...]
    n_chunks = q_ref.shape[1] // chunk
    for c in range(n_chunks):
        sl = pl.ds(c * chunk, chunk)
        out, st_new = _hgrn_chunk(q_ref[0, sl, :], f_ref[0, sl, :], i_ref[0, sl, :],
                                  g_ref[0, sl, :], lb, ng, e_sum, st_ref[...])
        st_ref[...] = st_new
        o_ref[0, sl, :] = out


def _hgrn_mixer(h3, lb, norm_g):
    B, S, _ = h3.shape
    ts = min(HGRN_ROWS, S)
    C = HGRN_CHUNK
    H = HGRN_HEADS
    e_sum = (jnp.arange(SUBLANES * LANES)[:, None] // LANES == jnp.arange(C)[None, :] % SUBLANES
             ).astype(BF16)
    col = lambda off: pl.BlockSpec((1, ts, LANES), lambda b, h, s, off=off: (b, s, off + h))
    return pl.pallas_call(
        functools.partial(_hgrn_kernel, chunk=C),
        grid=(B, H, S // ts),
        in_specs=[col(0), col(H), col(2 * H), col(3 * H),
                  pl.BlockSpec((1, 1, LANES), lambda b, h, s: (h, 0, 0)),
                  pl.BlockSpec((1, LANES), lambda b, h, s: (0, 0)),
                  pl.BlockSpec((SUBLANES * LANES, C), lambda b, h, s: (0, 0))],
        out_specs=pl.BlockSpec((1, ts, LANES), lambda b, h, s: (b, s, h)),
        out_shape=jax.ShapeDtypeStruct((B, S, MIX_WIDTH), F32),
        scratch_shapes=[pltpu.VMEM((HGRN_DK, HGRN_DK), F32)],
        compiler_params=_params("parallel", "parallel", "arbitrary"),
        name="hgrn",
    )(h3, h3, h3, h3, lb.reshape(H, 1, LANES), norm_g.reshape(1, LANES), e_sum)


def _moba_kernel(sl_ref, q_ref, kf_ref, kb_ref, vb_ref, o_ref, kmean_ref):
    BLK = MOBA_BLOCK
    p = pl.program_id(1)
    i = pl.program_id(2)

    @pl.when(i == 0)
    def _():
        kmean_ref[...] = jnp.zeros_like(kmean_ref)

    qf = q_ref[0]
    own = pl.ds(pl.multiple_of(i * BLK, BLK), BLK)
    k_own = kb_ref[0, own, :]
    v_own = vb_ref[0, own, :]

    lane = lax.broadcasted_iota(jnp.int32, (BLK, LANES), 1)
    rr = lax.broadcasted_iota(jnp.int32, (BLK, BLK), 0)
    cc = lax.broadcasted_iota(jnp.int32, (BLK, BLK), 1)
    rel = (rr - cc).astype(F32)
    scale = MOBA_HEAD_DIM ** -0.5

    outs = []
    for hh in range(2):
        head = (lane < MOBA_HEAD_DIM) if hh == 0 else (lane >= MOBA_HEAD_DIM)
        slope = sl_ref[2 * p + hh]
        qh = jnp.where(head, qf, 0.0)

        gate = lax.dot_general(qh, kmean_ref[...], _NT, precision=lax.Precision.HIGHEST,
                               preferred_element_type=F32)
        past = lane < i
        g = jnp.where(past, gate, -jnp.inf)
        sel = jnp.zeros((BLK, LANES), jnp.bool_)
        for _ in range(MOBA_TOPK):
            mx = jnp.max(g, axis=1, keepdims=True)
            idx = jnp.min(jnp.where(g == mx, lane, LANES), axis=1, keepdims=True)
            pick = lane == idx
            sel = sel | pick
            g = jnp.where(pick, -jnp.inf, g)
        sel_bias = jnp.where(sel & past, 0.0, NEG_BIG)
        q_aug = jnp.concatenate([(qh * scale).astype(BF16), sel_bias.astype(BF16)], axis=1)

        bias = -slope * rel
        s = lax.dot_general((qh * scale).astype(BF16), k_own, _NT, preferred_element_type=F32)
        s = s + jnp.where(cc <= rr, bias, -jnp.inf)
        m0 = jnp.max(s, axis=1, keepdims=True)
        p0 = jnp.exp(s - m0)
        l0 = jnp.sum(p0, axis=1, keepdims=True)
        acc0 = jnp.dot(p0.astype(BF16), v_own, preferred_element_type=F32)

        def body(n, carry):
            m, l, acc = carry
            blk = pl.ds(pl.multiple_of(n * BLK, BLK), BLK)
            onehot = jnp.where(lane == n, 1.0, 0.0).astype(BF16)
            k_aug = jnp.concatenate([kb_ref[0, blk, :], onehot], axis=1)
            sn = lax.dot_general(q_aug, k_aug, _NT, preferred_element_type=F32)
            sn = sn + (bias - slope * ((i - n) * BLK).astype(F32))
            m_new = jnp.maximum(m, jnp.max(sn, axis=1, keepdims=True))
            a = jnp.exp(m - m_new)
            pn = jnp.exp(sn - m_new)
            l = a * l + jnp.sum(pn, axis=1, keepdims=True)
            acc = a * acc + jnp.dot(pn.astype(BF16), vb_ref[0, blk, :],
                                    preferred_element_type=F32)
            return m_new, l, acc

        _, l, acc = lax.fori_loop(0, i, body, (m0, l0, acc0))
        outs.append(acc / l)

    o_ref[0] = jnp.where(lane < MOBA_HEAD_DIM, outs[0], outs[1])
    kmean_ref[pl.ds(i, 1), :] = jnp.mean(kf_ref[0], axis=0, keepdims=True)


def _moba_mixer(h3, hb3):
    B, S, _ = h3.shape
    BLK = MOBA_BLOCK
    NP = MOBA_HEADS // 2
    slopes = jnp.asarray(_alibi_slope_list(MOBA_HEADS), F32)
    grid_spec = pltpu.PrefetchScalarGridSpec(
        num_scalar_prefetch=1,
        grid=(B, NP, S // BLK),
        in_specs=[pl.BlockSpec((1, BLK, LANES), lambda b, p, i, sl: (b, i, p)),
                  pl.BlockSpec((1, BLK, LANES), lambda b, p, i, sl: (b, i, NP + p)),
                  pl.BlockSpec((1, S, LANES), lambda b, p, i, sl: (b, 0, NP + p)),
                  pl.BlockSpec((1, S, LANES), lambda b, p, i, sl: (b, 0, 2 * NP + p))],
        out_specs=pl.BlockSpec((1, BLK, LANES), lambda b, p, i, sl: (b, i, p)),
        scratch_shapes=[pltpu.VMEM((LANES, LANES), F32)],
    )
    return pl.pallas_call(
        _moba_kernel,
        grid_spec=grid_spec,
        out_shape=jax.ShapeDtypeStruct((B, S, MIX_WIDTH), F32),
        compiler_params=_params("parallel", "parallel", "arbitrary"),
        name="moba",
    )(slopes, h3, h3, hb3, hb3)


def _moba_top3(qh, kmean, i, lane):
    gate = lax.dot_general(qh, kmean, _NT, precision=lax.Precision.HIGHEST,
                           preferred_element_type=F32)
    g = jnp.where(lane < i, gate, -jnp.inf)
    picks = []
    for _ in range(MOBA_TOPK):
        mx = jnp.max(g, axis=1, keepdims=True)
        idx = jnp.min(jnp.where(g == mx, lane, LANES), axis=1, keepdims=True)
        picks.append((idx, (mx > -jnp.inf) & (idx < i)))
        g = jnp.where(lane == idx, -jnp.inf, g)
    return picks


def _moba_select_kernel(q_ref, kf_ref, sr_ref, cnt_ref, kmean_ref):
    BLK = MOBA_BLOCK
    i = pl.program_id(2)

    @pl.when(i == 0)
    def _():
        kmean_ref[...] = jnp.zeros_like(kmean_ref)

    qf = q_ref[0]
    lane = lax.broadcasted_iota(jnp.int32, (BLK, LANES), 1)
    rr = lax.broadcasted_iota(jnp.int32, (BLK, BLK), 0)
    cc = lax.broadcasted_iota(jnp.int32, (BLK, BLK), 1)
    before = jnp.where(cc < rr, 1.0, 0.0).astype(BF16)
    out = jnp.zeros((BLK, LANES), jnp.int32)
    counts = []
    for hh in range(2):
        head = (lane >> 6) == hh
        picks = _moba_top3(jnp.where(head, qf, 0.0), kmean_ref[...], i, lane)
        chosen = jnp.zeros((BLK, LANES), F32)
        for idx, valid in picks:
            chosen = chosen + jnp.where((lane == idx) & valid, 1.0, 0.0)
        earlier = jnp.dot(before, chosen.astype(BF16), preferred_element_type=F32)
        for j, (idx, valid) in enumerate(picks):
            rep = hh * MOBA_TOPK + j
            rank = jnp.sum(jnp.where(lane == idx, earlier, 0.0), axis=1, keepdims=True)
            out = jnp.where(lane == rep, jnp.where(valid, idx, -1), out)
            out = jnp.where(lane == SUBLANES + rep, rank.astype(jnp.int32), out)
        total = earlier[BLK - 1:BLK, :] + chosen[BLK - 1:BLK, :]
        counts.append(jnp.broadcast_to(total, (SUBLANES, LANES)))
    sr_ref[0, 0] = out
    cnt_ref[0, 0] = jnp.concatenate(counts, axis=0)
    kmean_ref[pl.ds(i, 1), :] = jnp.mean(kf_ref[0], axis=0, keepdims=True)


def _moba_dest_kernel(sr_ref, tab_ref, idx_ref, *, trash):
    BLK = MOBA_BLOCK
    blk = sr_ref[0, 0]
    tab = tab_ref[0, 0]
    lane = lax.broadcasted_iota(jnp.int32, (BLK, LANES), 1)
    d = jnp.full((BLK, LANES), float(trash), F32)
    for rep in range(2 * MOBA_TOPK):
        hh = rep // MOBA_TOPK
        sel = blk[:, rep:rep + 1]
        rank = blk[:, SUBLANES + rep:SUBLANES + rep + 1]
        start = jnp.sum(jnp.where(lane == sel, tab[hh:hh + 1, :], 0.0), axis=1, keepdims=True)
        dest = jnp.where(sel >= 0, start + rank.astype(F32), float(trash))
        d = jnp.where(lane == rep, dest, d)
    idx_ref[...] = d.T[:SC_IDX_ROWS, :].astype(jnp.int32)


def _moba_tile_kernel(tb_ref, tp_ref, th_ref, tn_ref, nu_ref, sl_ref, q_ref, k_ref, v_ref, o_ref):
    t = pl.program_id(0)

    @pl.when(t < nu_ref[0])
    def _():
        tq = q_ref.shape[0]
        hh = th_ref[t]
        slope = sl_ref[2 * tp_ref[t] + hh]
        lane = lax.broadcasted_iota(jnp.int32, (tq, LANES), 1)
        head = (lane >> 6) == hh
        q = jnp.where(head, q_ref[...] * (MOBA_HEAD_DIM ** -0.5), 0.0).astype(BF16)
        s = lax.dot_general(q, k_ref[0], _NT, preferred_element_type=F32)
        kpos = lax.broadcasted_iota(jnp.int32, s.shape, 1).astype(F32)
        s = s + slope * kpos
        m = jnp.max(s, axis=1, keepdims=True)
        p = jnp.exp(s - m)
        l = jnp.sum(p, axis=1, keepdims=True)
        o = jnp.dot(p.astype(BF16), v_ref[0], preferred_element_type=F32)
        half = lane & (MOBA_HEAD_DIM - 1)
        stats = jnp.where(half == 0, m, jnp.where(half == 1, l, 0.0))
        o_ref[...] = jnp.where(head, o, stats)


def _moba_merge_kernel(sl_ref, q_ref, k_ref, v_ref, sr_ref, pg_ref, o_ref):
    BLK = MOBA_BLOCK
    p = pl.program_id(1)
    i = pl.program_id(2)
    qf = q_ref[0]
    k_own = k_ref[0]
    v_own = v_ref[0]
    blk = sr_ref[0, 0]
    lane = lax.broadcasted_iota(jnp.int32, (BLK, LANES), 1)
    rr = lax.broadcasted_iota(jnp.int32, (BLK, BLK), 0)
    cc = lax.broadcasted_iota(jnp.int32, (BLK, BLK), 1)
    rel = (rr - cc).astype(F32)
    row = lax.broadcasted_iota(jnp.int32, (BLK, 1), 0)
    outs = []
    for hh in range(2):
        head = (lane >> 6) == hh
        slope = sl_ref[2 * p + hh]
        qh = jnp.where(head, qf * (MOBA_HEAD_DIM ** -0.5), 0.0).astype(BF16)
        s = lax.dot_general(qh, k_own, _NT, preferred_element_type=F32)
        s = s + jnp.where(cc <= rr, -slope * rel, -jnp.inf)
        m = jnp.max(s, axis=1, keepdims=True)
        pr = jnp.exp(s - m)
        ms = [m]
        ls = [jnp.sum(pr, axis=1, keepdims=True)]
        accs = [jnp.dot(pr.astype(BF16), v_own, preferred_element_type=F32)]
        stat_lane = MOBA_HEAD_DIM * (1 - hh)
        for j in range(MOBA_TOPK):
            rep = hh * MOBA_TOPK + j
            part = pg_ref[rep, 0]
            sel = blk[:, rep:rep + 1]
            valid = sel >= 0
            shift = -slope * ((i - sel) * BLK + row).astype(F32)
            ms.append(jnp.where(valid, part[:, stat_lane:stat_lane + 1] + shift, -jnp.inf))
            ls.append(jnp.where(valid, part[:, stat_lane + 1:stat_lane + 2], 0.0))
            accs.append(jnp.where(valid, part, 0.0))
        top = functools.reduce(jnp.maximum, ms)
        ws = [jnp.exp(mk - top) for mk in ms]
        den = sum(w * lk for w, lk in zip(ws, ls))
        num = sum(w * ak for w, ak in zip(ws, accs))
        outs.append(num / den)
    o_ref[0] = jnp.where(lane < MOBA_HEAD_DIM, outs[0], outs[1])


def _moba_sparse_mixer(h3, hb3, q6):
    B, S, _ = h3.shape
    T = B * S
    BLK = MOBA_BLOCK
    TQ = MOBA_TILE
    NB = S // BLK
    NP = MOBA_HEADS // 2
    n_rep = 2 * MOBA_TOPK
    slopes = jnp.asarray(_alibi_slope_list(MOBA_HEADS), F32)
    step = lambda *rest: pl.BlockSpec((1, 1) + rest, lambda b, p, i: (b, p, i) + (0,) * (len(rest) - 1))

    selrank, counts = pl.pallas_call(
        _moba_select_kernel,
        grid=(B, NP, NB),
        in_specs=[pl.BlockSpec((1, BLK, LANES), lambda b, p, i: (b, i, p)),
                  pl.BlockSpec((1, BLK, LANES), lambda b, p, i: (b, i, NP + p))],
        out_specs=[step(BLK, LANES), step(2 * SUBLANES, LANES)],
        out_shape=[jax.ShapeDtypeStruct((B, NP, S, LANES), jnp.int32),
                   jax.ShapeDtypeStruct((B, NP, NB * 2 * SUBLANES, LANES), F32)],
        scratch_shapes=[pltpu.VMEM((LANES, LANES), F32)],
        compiler_params=_params("parallel", "parallel", "arbitrary"),
        name="moba_select",
    )(h3, h3)

    cnt = counts.reshape(B, NP, NB, 2, SUBLANES, LANES)[:, :, :, :, 0, :NB].astype(jnp.int32)
    base = jnp.cumsum(cnt, axis=2) - cnt
    total = jnp.sum(cnt, axis=2)
    padded = (total + TQ - 1) // TQ * TQ
    pend = jnp.cumsum(padded.reshape(-1))
    seg_start = (pend - padded.reshape(-1)).reshape(B, NP, 1, 2, NB)
    table = jnp.zeros((B, NP, NB, SUBLANES, LANES), F32).at[:, :, :, :2, :NB].set(
        (seg_start + base).astype(F32)).reshape(B, NP, NB * SUBLANES, LANES)
    n_seg = B * NP * 2 * NB
    max_tiles = (T * NP * n_rep) // TQ + n_seg
    tile_seg = jnp.minimum(
        jnp.searchsorted(pend, jnp.arange(max_tiles, dtype=jnp.int32) * TQ, side='right'),
        n_seg - 1).astype(jnp.int32)
    tile_n = tile_seg % NB
    tile_h = (tile_seg // NB) % 2
    tile_p = (tile_seg // (2 * NB)) % NP
    tile_b = tile_seg // (2 * NB * NP)
    n_used = (pend[-1] // TQ).astype(jnp.int32).reshape(1)
    trash = max_tiles * TQ

    idx = pl.pallas_call(
        functools.partial(_moba_dest_kernel, trash=trash),
        grid=(B, NP, NB),
        in_specs=[step(BLK, LANES), step(SUBLANES, LANES)],
        out_specs=pl.BlockSpec((SC_IDX_ROWS, BLK), lambda b, p, i: (0, p * (T // BLK) + b * NB + i)),
        out_shape=jax.ShapeDtypeStruct((SC_IDX_ROWS, NP * T), jnp.int32),
        compiler_params=_params("parallel", "parallel", "parallel"),
        name="moba_dest",
    )(selrank, table)

    n_rows = (max_tiles + 1) * TQ
    qs = _sc_scatter_rows(q6.reshape(NP * T, LANES), idx, n_rep, n_rows)

    live = lambda t, tb, tp, th, tn, nu, sl: (jnp.where(t < nu[0], t, max_tiles), 0)
    kv_blk = lambda off: pl.BlockSpec(
        (1, BLK, LANES), lambda t, tb, tp, th, tn, nu, sl, off=off: (tb[t], tn[t], off + tp[t]))
    part = pl.pallas_call(
        _moba_tile_kernel,
        grid_spec=pltpu.PrefetchScalarGridSpec(
            num_scalar_prefetch=6,
            grid=(max_tiles,),
            in_specs=[pl.BlockSpec((TQ, LANES), live), kv_blk(NP), kv_blk(2 * NP)],
            out_specs=pl.BlockSpec((TQ, LANES), live)),
        out_shape=jax.ShapeDtypeStruct((n_rows, LANES), F32),
        compiler_params=_params("arbitrary"),
        name="moba_tiles",
    )(tile_b, tile_p, tile_h, tile_n, n_used, slopes, qs, hb3, hb3)

    pg = _sc_gather_rows(part, idx[:n_rep].reshape(-1)).reshape(n_rep, NP, T, LANES)

    own = lambda off: pl.BlockSpec((1, BLK, LANES), lambda b, p, i, sl, off=off: (b, i, off + p))
    return pl.pallas_call(
        _moba_merge_kernel,
        grid_spec=pltpu.PrefetchScalarGridSpec(
            num_scalar_prefetch=1,
            grid=(B, NP, NB),
            in_specs=[own(0), own(NP), own(2 * NP),
                      pl.BlockSpec((1, 1, BLK, LANES), lambda b, p, i, sl: (b, p, i, 0)),
                      pl.BlockSpec((n_rep, 1, BLK, LANES), lambda b, p, i, sl: (0, p, b * NB + i, 0))],
            out_specs=pl.BlockSpec((1, BLK, LANES), lambda b, p, i, sl: (b, i, p))),
        out_shape=jax.ShapeDtypeStruct((B, S, MIX_WIDTH), F32),
        compiler_params=_params("parallel", "parallel", "parallel"),
        name="moba_merge",
    )(slopes, h3, hb3, hb3, selrank, pg)


def _moba_pick_kernel(q_ref, kf_ref, sr_ref, cnt_ref, kmean_ref):
    BLK = MOBA_BLOCK
    nbp = kmean_ref.shape[0]
    i = pl.program_id(2)

    @pl.when(i == 0)
    def _():
        kmean_ref[...] = jnp.zeros_like(kmean_ref)

    qf = q_ref[0]
    lane = lax.broadcasted_iota(jnp.int32, (BLK, LANES), 1)
    nrow = lax.broadcasted_iota(jnp.int32, (nbp, BLK), 0)
    orow = lax.broadcasted_iota(jnp.int32, (2 * SUBLANES, BLK), 0)
    lane_c = lax.broadcasted_iota(jnp.int32, (nbp, LANES), 1)
    qa = lax.broadcasted_iota(jnp.int32, (BLK, BLK), 0)
    qb = lax.broadcasted_iota(jnp.int32, (BLK, BLK), 1)
    before = jnp.where(qa < qb, 1.0, 0.0).astype(BF16)
    out = jnp.zeros((2 * SUBLANES, BLK), jnp.int32)
    cnt = jnp.zeros((nbp, LANES), F32)
    for hh in range(2):
        qh = jnp.where((lane >> 6) == hh, qf, 0.0)
        gate = lax.dot_general(kmean_ref[...], qh, _NT, precision=lax.Precision.HIGHEST,
                               preferred_element_type=F32)
        g = jnp.where(nrow < i, gate, -jnp.inf)
        picks = []
        for _ in range(MOBA_TOPK):
            mx = jnp.max(g, axis=0, keepdims=True)
            idx = jnp.min(jnp.where(g == mx, nrow, nbp), axis=0, keepdims=True)
            picks.append((idx, (mx > -jnp.inf) & (idx < i)))
            g = jnp.where(nrow == idx, -jnp.inf, g)
        chosen = jnp.zeros((nbp, BLK), F32)
        for idx, valid in picks:
            chosen = chosen + jnp.where((nrow == idx) & valid, 1.0, 0.0)
        earlier = jnp.dot(chosen.astype(BF16), before, preferred_element_type=F32)
        for j, (idx, valid) in enumerate(picks):
            rep = hh * MOBA_TOPK + j
            rank = jnp.sum(jnp.where(nrow == idx, earlier, 0.0), axis=0, keepdims=True)
            out = jnp.where(orow == rep, jnp.where(valid, idx, -1), out)
            out = jnp.where(orow == SUBLANES + rep, rank.astype(jnp.int32), out)
        cnt = jnp.where(lane_c == hh, jnp.sum(chosen, axis=1, keepdims=True), cnt)
    sr_ref[0, 0] = out
    cnt_ref[0, 0] = cnt
    kmean_ref[pl.ds(i, 1), :] = jnp.mean(kf_ref[0], axis=0, keepdims=True)


def _moba_place_kernel(sr_ref, tab_ref, idx_ref, *, group, spare):
    BLK = MOBA_BLOCK
    n_rep = 2 * MOBA_TOPK
    b = pl.program_id(0)
    p = pl.program_id(1)
    ig = pl.program_id(2)
    nrow = lax.broadcasted_iota(jnp.int32, (LANES, BLK), 0)
    orow = lax.broadcasted_iota(jnp.int32, (SC_IDX_ROWS, BLK), 0)
    qpos = lax.broadcasted_iota(jnp.int32, (1, BLK), 1)
    for g in range(group):
        blk = sr_ref[0, 0, :, g * BLK:(g + 1) * BLK]
        tab_t = tab_ref[0, 0, g * SUBLANES:(g + 1) * SUBLANES, :].T
        early = jnp.minimum(ig * group + g, MOBA_TOPK - 1)
        out = jnp.zeros((SC_IDX_ROWS, BLK), jnp.int32)
        for rep in range(n_rep):
            hh = rep // MOBA_TOPK
            sel = blk[rep:rep + 1, :]
            rank = blk[SUBLANES + rep:SUBLANES + rep + 1, :]
            start = jnp.sum(jnp.where(nrow == sel, tab_t[:, hh:hh + 1], 0.0), axis=0, keepdims=True)
            unused = spare + (((b * pl.num_programs(1) + p) * MOBA_TOPK + early) * n_rep + rep) * BLK
            dest = jnp.where(sel >= 0, start.astype(jnp.int32) + rank, unused + qpos)
            out = jnp.where(orow == rep, dest, out)
        idx_ref[:, g * BLK:(g + 1) * BLK] = out


def _moba_group_kernel(tb_ref, tp_ref, th_ref, tn_ref, nu_ref, sl_ref, q_ref, *refs, group):
    k_refs, v_refs, o_ref = refs[:group], refs[group:2 * group], refs[2 * group]
    t = pl.program_id(0)
    tq = q_ref.shape[0] // group

    @pl.when(t * group < nu_ref[0])
    def _():
        lane = lax.broadcasted_iota(jnp.int32, (tq, LANES), 1)
        kpos = lax.broadcasted_iota(jnp.int32, (tq, MOBA_BLOCK), 1)
        for g in range(group):
            tt = t * group + g
            hh = th_ref[tt]
            slope = sl_ref[2 * tp_ref[tt] + hh]
            head = (lane >> 6) == hh
            rows = pl.ds(g * tq, tq)
            q = jnp.where(head, q_ref[rows, :] * (MOBA_HEAD_DIM ** -0.5), 0.0).astype(BF16)
            s = lax.dot_general(q, k_refs[g][0], _NT, preferred_element_type=F32)
            s = s + slope * (kpos + tn_ref[tt] * MOBA_BLOCK).astype(F32)
            m = jnp.max(s, axis=1, keepdims=True)
            pr = jnp.exp(s - m)
            l = jnp.sum(pr, axis=1, keepdims=True)
            o = jnp.dot(pr.astype(BF16), v_refs[g][0], preferred_element_type=F32) / l
            o_ref[rows, :] = jnp.where(head, o, m + jnp.log(l))


def _moba_join_kernel(sl_ref, q_ref, k_ref, v_ref, pg_ref, o_ref):
    BLK = MOBA_BLOCK
    p = pl.program_id(1)
    i = pl.program_id(2)
    qf = q_ref[0]
    k_own = k_ref[0]
    v_own = v_ref[0]
    lane = lax.broadcasted_iota(jnp.int32, (BLK, LANES), 1)
    rr = lax.broadcasted_iota(jnp.int32, (BLK, BLK), 0)
    cc = lax.broadcasted_iota(jnp.int32, (BLK, BLK), 1)
    key_pos = (cc + i * BLK).astype(F32)
    outs = []
    for hh in range(2):
        head = (lane >> 6) == hh
        slope = sl_ref[2 * p + hh]
        qh = jnp.where(head, qf * (MOBA_HEAD_DIM ** -0.5), 0.0).astype(BF16)
        s = lax.dot_general(qh, k_own, _NT, preferred_element_type=F32)
        s = jnp.where(cc <= rr, s + slope * key_pos, -jnp.inf)
        m = jnp.max(s, axis=1, keepdims=True)
        pr = jnp.exp(s - m)
        l = jnp.sum(pr, axis=1, keepdims=True)
        lses = [m + jnp.log(l)]
        vals = [jnp.dot(pr.astype(BF16), v_own, preferred_element_type=F32) / l]
        for j in range(MOBA_TOPK):
            part = pg_ref[hh * MOBA_TOPK + j, 0]
            has_block = j < i
            lses.append(jnp.where(has_block, pltpu.roll(part, MOBA_HEAD_DIM, 1), -jnp.inf))
            vals.append(jnp.where(has_block, part, 0.0))
        top = functools.reduce(jnp.maximum, lses)
        ws = [jnp.exp(x - top) for x in lses]
        outs.append(sum(w * v for w, v in zip(ws, vals)) / sum(ws))
    o_ref[0] = jnp.where(lane < MOBA_HEAD_DIM, outs[0], outs[1])


def _moba_mixer_v2(h3, hb3, q6):
    B, S, _ = h3.shape
    T = B * S
    BLK = MOBA_BLOCK
    TQ = MOBA_TILE
    G = MOBA_TILE_GROUP
    NB = S // BLK
    GI = math.gcd(NB, MOBA_PLACE_GROUP)
    nbp = -(-NB // SUBLANES) * SUBLANES
    NP = MOBA_HEADS // 2
    n_rep = 2 * MOBA_TOPK
    slopes = jnp.asarray(_alibi_slope_list(MOBA_HEADS), F32)

    selrank, counts = pl.pallas_call(
        _moba_pick_kernel,
        grid=(B, NP, NB),
        in_specs=[pl.BlockSpec((1, BLK, LANES), lambda b, p, i: (b, i, p)),
                  pl.BlockSpec((1, BLK, LANES), lambda b, p, i: (b, i, NP + p))],
        out_specs=[pl.BlockSpec((1, 1, 2 * SUBLANES, BLK), lambda b, p, i: (b, p, 0, i)),
                   pl.BlockSpec((1, 1, nbp, LANES), lambda b, p, i: (b, p, i, 0))],
        out_shape=[jax.ShapeDtypeStruct((B, NP, 2 * SUBLANES, S), jnp.int32),
                   jax.ShapeDtypeStruct((B, NP, NB * nbp, LANES), F32)],
        scratch_shapes=[pltpu.VMEM((nbp, LANES), F32)],
        compiler_params=_params("parallel", "parallel", "arbitrary"),
        name="moba_pick",
    )(h3, h3)

    cnt = counts.reshape(B, NP, NB, nbp, LANES)[:, :, :, :NB, :2].astype(jnp.int32)
    cnt = cnt.transpose(0, 1, 2, 4, 3)
    base = jnp.cumsum(cnt, axis=2) - cnt
    total = jnp.sum(cnt, axis=2)
    padded = (total + TQ - 1) // TQ * TQ
    pend = jnp.cumsum(padded.reshape(-1))
    seg_start = (pend - padded.reshape(-1)).reshape(B, NP, 1, 2, NB)
    table = jnp.zeros((B, NP, NB, SUBLANES, LANES), F32).at[:, :, :, :2, :NB].set(
        (seg_start + base).astype(F32)).reshape(B, NP, NB * SUBLANES, LANES)
    n_seg = B * NP * 2 * NB
    max_tiles = -(-((T * NP * n_rep) // TQ + n_seg) // G) * G
    first_row = jnp.arange(max_tiles, dtype=jnp.int32) * TQ
    tile_seg = jnp.minimum(jnp.sum(pend[None, :] <= first_row[:, None], axis=1), n_seg - 1)
    tile_n = (tile_seg % NB).astype(jnp.int32)
    tile_h = ((tile_seg // NB) % 2).astype(jnp.int32)
    tile_p = ((tile_seg // (2 * NB)) % NP).astype(jnp.int32)
    tile_b = (tile_seg // (2 * NB * NP)).astype(jnp.int32)
    n_used = (pend[-1] // TQ).astype(jnp.int32).reshape(1)
    spare = max_tiles * TQ
    n_rows = spare + max(B * NP * MOBA_TOPK * n_rep * BLK, G * TQ)

    idx = pl.pallas_call(
        functools.partial(_moba_place_kernel, group=GI, spare=spare),
        grid=(B, NP, NB // GI),
        in_specs=[pl.BlockSpec((1, 1, 2 * SUBLANES, GI * BLK), lambda b, p, i: (b, p, 0, i)),
                  pl.BlockSpec((1, 1, GI * SUBLANES, LANES), lambda b, p, i: (b, p, i, 0))],
        out_specs=pl.BlockSpec((SC_IDX_ROWS, GI * BLK),
                               lambda b, p, i: (0, p * (T // (GI * BLK)) + b * (NB // GI) + i)),
        out_shape=jax.ShapeDtypeStruct((SC_IDX_ROWS, NP * T), jnp.int32),
        compiler_params=_params("parallel", "parallel", "parallel"),
        name="moba_place",
    )(selrank, table)

    qs = _sc_scatter_rows(q6.reshape(NP * T, LANES), idx, n_rep, n_rows)

    live = lambda t, tb, tp, th, tn, nu, sl: (jnp.where(t * G < nu[0], t, max_tiles // G), 0)
    kv_blk = lambda off, g: pl.BlockSpec(
        (1, BLK, LANES),
        lambda t, tb, tp, th, tn, nu, sl: (tb[t * G + g], tn[t * G + g], off + tp[t * G + g]))
    part = pl.pallas_call(
        functools.partial(_moba_group_kernel, group=G),
        grid_spec=pltpu.PrefetchScalarGridSpec(
            num_scalar_prefetch=6,
            grid=(max_tiles // G,),
            in_specs=([pl.BlockSpec((G * TQ, LANES), live)]
                      + [kv_blk(NP, g) for g in range(G)] + [kv_blk(2 * NP, g) for g in range(G)]),
            out_specs=pl.BlockSpec((G * TQ, LANES), live)),
        out_shape=jax.ShapeDtypeStruct((n_rows, LANES), F32),
        compiler_params=_params("arbitrary"),
        name="moba_tiles",
    )(tile_b, tile_p, tile_h, tile_n, n_used, slopes, qs, *([hb3] * (2 * G)))

    pg = _sc_gather_rows(part, idx[:n_rep].reshape(-1)).reshape(n_rep, NP, T, LANES)

    own = lambda off: pl.BlockSpec((1, BLK, LANES), lambda b, p, i, sl, off=off: (b, i, off + p))
    return pl.pallas_call(
        _moba_join_kernel,
        grid_spec=pltpu.PrefetchScalarGridSpec(
            num_scalar_prefetch=1,
            grid=(B, NP, NB),
            in_specs=[own(0), own(NP), own(2 * NP),
                      pl.BlockSpec((n_rep, 1, BLK, LANES), lambda b, p, i, sl: (0, p, b * NB + i, 0))],
            out_specs=pl.BlockSpec((1, BLK, LANES), lambda b, p, i, sl: (b, i, p))),
        out_shape=jax.ShapeDtypeStruct((B, S, MIX_WIDTH), F32),
        compiler_params=_params("parallel", "parallel", "parallel"),
        name="moba_join",
    )(slopes, h3, hb3, hb3, pg)


def _memkv_kernel(mem_ref, w_ref, kv_ref):
    kv_ref[0] = jnp.dot(mem_ref[0].astype(BF16), w_ref[...].astype(BF16),
                        preferred_element_type=F32).astype(BF16)


def _memkv(mem, w_kv):
    B, M, D = mem.shape
    N = w_kv.shape[1]
    return pl.pallas_call(
        _memkv_kernel,
        grid=(B,),
        in_specs=[pl.BlockSpec((1, M, D), lambda b: (b, 0, 0)),
                  pl.BlockSpec((D, N), lambda b: (0, 0))],
        out_specs=pl.BlockSpec((1, M, N), lambda b: (b, 0, 0)),
        out_shape=jax.ShapeDtypeStruct((B, M, N), BF16),
        compiler_params=_params("parallel"),
        name="memkv",
    )(mem, w_kv)


def _layer_norm(z, g, b):
    mu = jnp.mean(z, axis=-1, keepdims=True)
    zc = z - mu
    var = jnp.mean(zc * zc, axis=-1, keepdims=True)
    return zc * lax.rsqrt(var + LN_EPS) * g + b


def _post_kernel(x_ref, mix_ref, mq_ref, kv_ref, wo_ref, g_ref, b_ref, wr_ref, br_ref,
                 x1_ref, x1s_ref, idx_ref, gate_ref, *, alpha):
    tm = x_ref.shape[0]
    mq = mq_ref[...]
    kv = kv_ref[0]
    km = kv[:, :MEM_WIDTH]
    vm = kv[:, MEM_WIDTH:]
    lane = lax.broadcasted_iota(jnp.int32, (tm, MEM_WIDTH), 1)
    scale = MEM_HEAD_DIM ** -0.5
    mo = jnp.zeros((tm, MEM_WIDTH), F32)
    for hd in range(MEM_HEADS):
        head = (lane >> 6) == hd
        qh = jnp.where(head, mq * scale, 0.0).astype(BF16)
        s = lax.dot_general(qh, km, _NT, preferred_element_type=F32)
        m = jnp.max(s, axis=1, keepdims=True)
        p = jnp.exp(s - m)
        l = jnp.sum(p, axis=1, keepdims=True)
        oh = jnp.dot(p.astype(BF16), vm, preferred_element_type=F32) / l
        mo = jnp.where(head, oh, mo)

    y = jnp.dot(mix_ref[...].astype(BF16), wo_ref[:MIX_WIDTH, :], preferred_element_type=F32)
    y = y + jnp.dot(mo.astype(BF16), wo_ref[MIX_WIDTH:, :], preferred_element_type=F32)
    x1 = _layer_norm(alpha * x_ref[...] + y, g_ref[...], b_ref[...])
    x1_ref[...] = x1
    _store_subrows(x1s_ref, x1)

    logits = jnp.dot(x1, wr_ref[...], precision=lax.Precision.HIGHEST,
                     preferred_element_type=F32) + br_ref[...]
    lane_e = lax.broadcasted_iota(jnp.int32, (tm, LANES), 1)
    g = jnp.where(lane_e < N_EXPERTS, logits, -jnp.inf)
    idx_out = jnp.zeros((tm, LANES), jnp.int32)
    val_out = jnp.full((tm, LANES), -jnp.inf, F32)
    for kk in range(TOP_K):
        mx = jnp.max(g, axis=1, keepdims=True)
        idx = jnp.min(jnp.where(g == mx, lane_e, LANES), axis=1, keepdims=True)
        idx_out = jnp.where(lane_e == kk, idx, idx_out)
        val_out = jnp.where(lane_e == kk, mx, val_out)
        g = jnp.where(lane_e == idx, -jnp.inf, g)
    vmax = jnp.max(val_out, axis=1, keepdims=True)
    ev = jnp.exp(val_out - vmax)
    idx_ref[...] = idx_out
    gate_ref[...] = ev / jnp.sum(ev, axis=1, keepdims=True)


def _post_mixer(x2, mix2, h2, kv, w_o_bf16, ln_g, ln_b, w_router, b_router, seq_len, alpha):
    T, D = x2.shape
    tm = POST_ROWS
    N = h2.shape[1]
    M = kv.shape[1]
    mq_col = (N - MEM_WIDTH) // MEM_WIDTH
    tiles_per_seq = seq_len // tm
    wr = jnp.zeros((D, LANES), F32).at[:, :N_EXPERTS].set(w_router)
    br = jnp.zeros((1, LANES), F32).at[0, :N_EXPERTS].set(b_router)
    row = lambda n: pl.BlockSpec((tm, n), lambda i: (i, 0))
    full = lambda a, b: pl.BlockSpec((a, b), lambda i: (0, 0))
    return pl.pallas_call(
        functools.partial(_post_kernel, alpha=alpha),
        grid=(T // tm,),
        in_specs=[row(D), row(MIX_WIDTH),
                  pl.BlockSpec((tm, MEM_WIDTH), lambda i: (i, mq_col)),
                  pl.BlockSpec((1, M, 2 * MEM_WIDTH), lambda i: (i // tiles_per_seq, 0, 0)),
                  full(D, D), full(1, D), full(1, D), full(D, LANES), full(1, LANES)],
        out_specs=[row(D), pl.BlockSpec((tm * (D // LANES), LANES), lambda i: (i, 0)),
                   row(LANES), row(LANES)],
        out_shape=[jax.ShapeDtypeStruct((T, D), F32),
                   jax.ShapeDtypeStruct((T * (D // LANES), LANES), F32),
                   jax.ShapeDtypeStruct((T, LANES), jnp.int32),
                   jax.ShapeDtypeStruct((T, LANES), F32)],
        compiler_params=_params("parallel"),
        name="post_mixer",
    )(x2, mix2, h2, kv, w_o_bf16, ln_g.reshape(1, D), ln_b.reshape(1, D), wr, br)


def _sc_mesh():
    return plsc.VectorSubcoreMesh(core_axis_name="core", subcore_axis_name="subcore")


def _sc_scatter_rows(rows, idx, n_rep, n_out):
    R, W = rows.shape

    @functools.partial(pl.kernel, out_type=jax.ShapeDtypeStruct((n_out, W), rows.dtype),
                       mesh=_sc_mesh(), scratch_types=[])
    def scatter(x_hbm, i_hbm, o_hbm):
        def body(x_vmem, i_vmem):
            for r in range(n_rep):
                pltpu.sync_copy(x_vmem, o_hbm.at[i_vmem.at[r]])

        pltpu.emit_pipeline(
            body, grid=(R // SC_WINDOW,),
            in_specs=[pl.BlockSpec((SC_WINDOW, W), lambda i: (i, 0)),
                      pl.BlockSpec((SC_IDX_ROWS, SC_WINDOW), lambda i: (0, i))],
            out_specs=[], core_axis_name=("core", "subcore"),
            dimension_semantics=(pltpu.PARALLEL,), trace_scopes=False)(x_hbm, i_hbm)

    return scatter(rows, idx)


def _sc_gather_rows(table, idx):
    n = idx.shape[0]
    W = table.shape[1]

    @functools.partial(pl.kernel, out_type=jax.ShapeDtypeStruct((n, W), table.dtype),
                       mesh=_sc_mesh(), scratch_types=[])
    def gather(t_hbm, i_hbm, o_hbm):
        def body(i_vmem, o_vmem):
            pltpu.sync_copy(t_hbm.at[i_vmem.at[0]], o_vmem)

        pltpu.emit_pipeline(
            body, grid=(n // SC_WINDOW,),
            in_specs=[pl.BlockSpec((1, SC_WINDOW), lambda i: (0, i))],
            out_specs=[pl.BlockSpec((SC_WINDOW, W), lambda i: (i, 0))],
            core_axis_name=("core", "subcore"),
            dimension_semantics=(pltpu.PARALLEL,), trace_scopes=False)(i_hbm, o_hbm)

    return gather(table, idx.reshape(1, n))


def _route(top_idx, n_tokens):
    rb = MOE_ROWS
    tk = n_tokens * TOP_K
    flat_e = top_idx.reshape(-1)
    onehot = (flat_e[:, None] == jnp.arange(N_EXPERTS, dtype=jnp.int32)[None, :]).astype(jnp.int32)
    csum = jnp.cumsum(onehot, axis=0)
    rank = jnp.sum(onehot * csum, axis=1) - 1
    counts = csum[-1]
    padded = (counts + rb - 1) // rb * rb
    pend = jnp.cumsum(padded)
    pstart = pend - padded
    dest = (pstart[flat_e] + rank).astype(jnp.int32).reshape(n_tokens, TOP_K)
    n_blocks = tk // rb + N_EXPERTS
    first_row = jnp.arange(n_blocks, dtype=jnp.int32) * rb
    block_e = jnp.minimum(jnp.sum(pend[None, :] <= first_row[:, None], axis=1),
                          N_EXPERTS - 1).astype(jnp.int32)
    n_used = (pend[-1] // rb).astype(jnp.int32).reshape(1)
    return dest, block_e, n_used


def _dispatch(x1s, dest, n_rows, sub):
    T = dest.shape[0]
    c = jnp.arange(sub, dtype=jnp.int32)
    idx = (dest.T[:, :, None] * sub + c[None, None, :]).reshape(TOP_K, T * sub)
    idx = jnp.concatenate([idx, jnp.zeros((SC_IDX_ROWS - TOP_K, T * sub), jnp.int32)], axis=0)
    return _sc_scatter_rows(x1s, idx, TOP_K, n_rows * sub)


def _expert_kernel(be_ref, nu_ref, x_ref, wg_ref, bg_ref, wu_ref, bu_ref, wd_ref, bd_ref,
                   y_ref, wgb, wub, wdb):
    i = pl.program_id(0)
    prev = be_ref[jnp.maximum(i - 1, 0)]

    @pl.when((i == 0) | (be_ref[i] != prev))
    def _():
        wgb[...] = wg_ref[0, 0].astype(BF16)
        wub[...] = wu_ref[0, 0].astype(BF16)
        wdb[...] = wd_ref[0, 0].astype(BF16)

    @pl.when(i < nu_ref[0])
    def _():
        sub = wgb.shape[0] // LANES
        xb = _load_subrows(x_ref, x_ref.shape[0] // sub, sub).astype(BF16)
        gate = jnp.dot(xb, wgb[...], preferred_element_type=F32) + bg_ref[0, 0]
        gate = jnp.minimum(gate, SWIGLU_LIMIT)
        up = jnp.dot(xb, wub[...], preferred_element_type=F32) + bu_ref[0, 0]
        up = jnp.clip(up, -SWIGLU_LIMIT, SWIGLU_LIMIT)
        hid = gate * _sigmoid(SWIGLU_ALPHA * gate) * (up + 1.0)
        y = jnp.dot(hid.astype(BF16), wdb[...], preferred_element_type=F32) + bd_ref[0, 0]
        _store_subrows(y_ref, y)


def _experts(xs, block_e, n_used, layer, w_gate, b_gate, w_up, b_up, w_down, b_down):
    rb = MOE_ROWS
    n_blocks = block_e.shape[0]
    E, D, F = w_gate.shape[1:]
    sub = D // LANES
    wspec = lambda a, b: pl.BlockSpec((1, 1, a, b), lambda i, be, nu: (layer, be[i], 0, 0))
    live = lambda i, be, nu: (jnp.where(i < nu[0], i, n_blocks), 0)
    grid_spec = pltpu.PrefetchScalarGridSpec(
        num_scalar_prefetch=2,
        grid=(n_blocks,),
        in_specs=[pl.BlockSpec((rb * sub, LANES), live),
                  wspec(D, F), wspec(1, F), wspec(D, F), wspec(1, F), wspec(F, D), wspec(1, D)],
        out_specs=pl.BlockSpec((rb * sub, LANES), live),
        scratch_shapes=[pltpu.VMEM((D, F), BF16), pltpu.VMEM((D, F), BF16),
                        pltpu.VMEM((F, D), BF16)],
    )
    depth = w_gate.shape[0]
    return pl.pallas_call(
        _expert_kernel,
        grid_spec=grid_spec,
        out_shape=jax.ShapeDtypeStruct(xs.shape, F32),
        compiler_params=_params("arbitrary"),
        name="experts",
    )(block_e, n_used, xs, w_gate, b_gate.reshape(depth, E, 1, F),
      w_up, b_up.reshape(depth, E, 1, F), w_down, b_down.reshape(depth, E, 1, D))


def _combine_kernel(x1_ref, gate_ref, y_ref, g_ref, b_ref, o_ref, *, alpha):
    tm, D = x1_ref.shape
    sub = D // LANES
    gates = gate_ref[...]
    f = jnp.zeros(x1_ref.shape, F32)
    for kk in range(TOP_K):
        f = f + gates[:, kk:kk + 1] * _load_subrows(y_ref, tm, sub, kk * sub, TOP_K * sub)
    o_ref[...] = _layer_norm(alpha * x1_ref[...] + f, g_ref[...], b_ref[...])


def _combine(x1, gates, dest, y_rows, ln_g, ln_b, alpha):
    T, D = x1.shape
    sub = D // LANES
    tm = COMBINE_ROWS
    c = jnp.arange(sub, dtype=jnp.int32)
    idx = (dest[:, :, None] * sub + c[None, None, :]).reshape(T * TOP_K * sub)
    yg = _sc_gather_rows(y_rows, idx)
    row = lambda n: pl.BlockSpec((tm, n), lambda i: (i, 0))
    full = lambda a, b: pl.BlockSpec((a, b), lambda i: (0, 0))
    return pl.pallas_call(
        functools.partial(_combine_kernel, alpha=alpha),
        grid=(T // tm,),
        in_specs=[row(D), row(LANES), pl.BlockSpec((tm * TOP_K * sub, LANES), lambda i: (i, 0)),
                  full(1, D), full(1, D)],
        out_specs=row(D),
        out_shape=jax.ShapeDtypeStruct((T, D), F32),
        compiler_params=_params("parallel"),
        name="combine",
    )(x1, gates, yg, ln_g.reshape(1, D), ln_b.reshape(1, D))


def kernel(x, mem, w_in_hgrn, hgrn_lb_logits, hgrn_norm_g, w_in_moba, w_mem_kv, w_o,
           ln_mix_g, ln_mix_b, w_router, b_router, w_gate, b_gate, w_up, b_up,
           w_down, b_down, ln_ffn_g, ln_ffn_b):
    B, S, D = x.shape
    T = B * S
    depth = w_o.shape[0]
    alpha = (2 * depth) ** 0.25

    p_lb = jax.nn.softmax(hgrn_lb_logits.astype(F32), axis=0)
    lower_bounds = jnp.cumsum(p_lb, axis=0) - p_lb[0]

    chains = [x[b] for b in range(B)]
    for layer in range(depth):
        j = layer // 2
        w_in = (w_in_hgrn if layer % 2 == 0 else w_in_moba)[j].astype(BF16)
        w_out = w_o[layer].astype(BF16)
        kv = _memkv(mem, w_mem_kv[layer])
        for b in range(B):
            x2 = chains[b]
            if layer % 2 == 0:
                (h2,) = _inproj(x2, w_in, for_moba=False)
                mix = _hgrn_mixer(h2.reshape(1, S, -1), lower_bounds[j], hgrn_norm_g[j])
            else:
                h2, hb2, q6 = _inproj(x2, w_in, for_moba=True)
                mix = _moba_mixer_v2(h2.reshape(1, S, -1), hb2.reshape(1, S, -1), q6)
            x1, x1s, top_idx, gates = _post_mixer(
                x2, mix.reshape(S, MIX_WIDTH), h2, kv[b:b + 1], w_out,
                ln_mix_g[layer], ln_mix_b[layer], w_router[layer], b_router[layer], S, alpha)
            dest, block_e, n_used = _route(top_idx[:, :TOP_K], S)
            xs = _dispatch(x1s, dest, (block_e.shape[0] + 1) * MOE_ROWS, D // LANES)
            y_rows = _experts(xs, block_e, n_used, layer,
                              w_gate, b_gate, w_up, b_up, w_down, b_down)
            chains[b] = _combine(x1, gates, dest, y_rows, ln_ffn_g[layer], ln_ffn_b[layer], alpha)
    return jnp.stack(chains)
```

```python
import functools
import math

import jax
import jax.numpy as jnp
from jax import lax
from jax.experimental import pallas as pl
from jax.experimental.pallas import tpu as pltpu
from jax.experimental.pallas import tpu_sc as plsc

MIX_WIDTH = 768
MEM_HEADS = 4
MEM_HEAD_DIM = 64
MEM_WIDTH = MEM_HEADS * MEM_HEAD_DIM
HGRN_HEADS = 6
HGRN_DK = 128
MOBA_HEADS = 12
MOBA_HEAD_DIM = 64
MOBA_BLOCK = 256
MOBA_TOPK = 3
N_EXPERTS = 32
TOP_K = 4
SWIGLU_ALPHA = 1.702
SWIGLU_LIMIT = 7.0
LN_EPS = 1e-5
RMS_EPS = 1e-6

LANES = 128
SUBLANES = 8
VMEM_LIMIT_BYTES = 56 * 1024 * 1024

INPROJ_ROWS = 512
HGRN_CHUNK = 64
HGRN_ROWS = 512
POST_ROWS = 256
MOBA_TILE = 256
MOBA_TILE_GROUP = 4
MOBA_PLACE_GROUP = 8
MOE_ROWS = 256
COMBINE_ROWS = 256
SC_WINDOW = 128
SC_IDX_ROWS = 8
SC_CHUNK = 32

BF16 = jnp.bfloat16
F32 = jnp.float32
NEG_BIG = -1e30

_NT = (((1,), (1,)), ((), ()))
_TN = (((0,), (0,)), ((), ()))


def _alibi_slope_list(n):
    def pow2(m):
        start = 2.0 ** (-(2.0 ** -(math.log2(m) - 3)))
        return [start ** (i + 1) for i in range(m)]
    if math.log2(n).is_integer():
        return pow2(n)
    c = 2 ** math.floor(math.log2(n))
    return pow2(c) + _alibi_slope_list(2 * c)[0::2][:n - c]


def _sigmoid(x):
    return 1.0 / (1.0 + jnp.exp(-x))


def _params(*sem):
    return pltpu.CompilerParams(dimension_semantics=sem, vmem_limit_bytes=VMEM_LIMIT_BYTES)


def _store_subrows(ref, value, first=0, stride=None):
    sub = value.shape[1] // LANES
    stride = stride or sub
    for c in range(sub):
        ref[pl.ds(first + c, value.shape[0], stride=stride), :] = value[:, c * LANES:(c + 1) * LANES]


def _load_subrows(ref, rows, sub, first=0, stride=None):
    stride = stride or sub
    return jnp.concatenate(
        [ref[pl.ds(first + c, rows, stride=stride), :] for c in range(sub)], axis=1)


def _inproj_kernel(x_ref, w_ref, h_ref, *moba_refs):
    h = jnp.dot(x_ref[...].astype(BF16), w_ref[...], preferred_element_type=F32)
    h_ref[...] = h
    if moba_refs:
        hb_ref, q6_ref = moba_refs
        hb_ref[...] = h.astype(BF16)
        for p in range(q6_ref.shape[0]):
            q6_ref[p] = h[:, p * LANES:(p + 1) * LANES]


def _inproj(x2, w_bf16, for_moba):
    T, D = x2.shape
    N = w_bf16.shape[1]
    tm = INPROJ_ROWS
    out_shape = [jax.ShapeDtypeStruct((T, N), F32)]
    out_specs = [pl.BlockSpec((tm, N), lambda i: (i, 0))]
    if for_moba:
        NP = MOBA_HEADS // 2
        out_shape += [jax.ShapeDtypeStruct((T, N), BF16), jax.ShapeDtypeStruct((NP, T, LANES), F32)]
        out_specs += [pl.BlockSpec((tm, N), lambda i: (i, 0)),
                      pl.BlockSpec((NP, tm, LANES), lambda i: (0, i, 0))]
    return pl.pallas_call(
        _inproj_kernel,
        grid=(T // tm,),
        in_specs=[pl.BlockSpec((tm, D), lambda i: (i, 0)),
                  pl.BlockSpec((D, N), lambda i: (0, 0))],
        out_specs=out_specs,
        out_shape=out_shape,
        compiler_params=_params("parallel"),
        name="inproj",
    )(x2, w_bf16)


def _cumsum_rows(x, row):
    n = x.shape[0]
    sh = 1
    while sh < n:
        x = x + jnp.where(row >= sh, pltpu.roll(x, sh, 0), 0.0)
        sh *= 2
    return x


def _bcast_row(a, group, r):
    n = a.shape[0]
    a3 = a.reshape(n // group, group, LANES)
    return jnp.broadcast_to(a3[:, r:r + 1, :], a3.shape).reshape(n, LANES)


def _hgrn_chunk(qr, fr, v, gr, lb, ng, e_sum, st_t):
    C = qr.shape[0]
    row = lax.broadcasted_iota(jnp.int32, (C, LANES), 0)
    rr = lax.broadcasted_iota(jnp.int32, (C, C), 0)
    cc = lax.broadcasted_iota(jnp.int32, (C, C), 1)

    q = qr * _sigmoid(qr)
    forget = lb + (1.0 - lb) * _sigmoid(fr)
    k = 1.0 - forget
    G = _cumsum_rows(jnp.log(forget), row)

    parts = []
    for s in range(SUBLANES):
        Gs = _bcast_row(G, SUBLANES, s)
        ks = _bcast_row(k, SUBLANES, s)
        parts.append((q * ks * jnp.exp(jnp.minimum(G - Gs, 0.0))).astype(BF16))
    a_diag = jnp.dot(jnp.concatenate(parts, axis=1), e_sum, preferred_element_type=F32)
    A = jnp.where(((rr >> 3) == (cc >> 3)) & (cc <= rr), a_diag, 0.0)

    m = SUBLANES
    while m < C:
        lg = int(math.log2(m))
        Gr = _bcast_row(G, 2 * m, m - 1)
        second = ((row >> lg) & 1) == 1
        qm = jnp.where(second, q * jnp.exp(jnp.minimum(G - Gr, 0.0)), 0.0)
        km = jnp.where(second, 0.0, k * jnp.exp(jnp.minimum(Gr - G, 0.0)))
        am = lax.dot_general(qm.astype(BF16), km.astype(BF16), _NT, preferred_element_type=F32)
        A = A + jnp.where((rr >> (lg + 1)) == (cc >> (lg + 1)), am, 0.0)
        m *= 2

    vb = v.astype(BF16)
    o = jnp.dot(A.astype(BF16), vb, preferred_element_type=F32)
    o = o + lax.dot_general((q * jnp.exp(G)).astype(BF16), st_t.astype(BF16), _NT,
                            preferred_element_type=F32)
    g_end = G[C - 1:C, :]
    kd = (k * jnp.exp(g_end - G)).astype(BF16)
    st_new = st_t * jnp.exp(g_end) + lax.dot_general(vb, kd, _TN, preferred_element_type=F32)

    ms = jnp.mean(o * o, axis=-1, keepdims=True)
    out = o * lax.rsqrt(ms + RMS_EPS) * ng * _sigmoid(gr)
    return out, st_new


def _hgrn_kernel(q_ref, f_ref, i_ref, g_ref, lb_ref, ng_ref, e_ref, o_ref, st_ref, *, chunk):
    @pl.when(pl.program_id(2) == 0)
    def _():
        st_ref[...] = jnp.zeros_like(st_ref)

    lb = lb_ref[0]
    ng = ng_ref[...]
    e_sum = e_ref[...]
    n_chunks = q_ref.shape[1] // chunk
    for c in range(n_chunks):
        sl = pl.ds(c * chunk, chunk)
        out, st_new = _hgrn_chunk(q_ref[0, sl, :], f_ref[0, sl, :], i_ref[0, sl, :],
                                  g_ref[0, sl, :], lb, ng, e_sum, st_ref[...])
        st_ref[...] = st_new
        o_ref[0, sl, :] = out


def _hgrn_mixer(h3, lb, norm_g):
    B, S, _ = h3.shape
    ts = min(HGRN_ROWS, S)
    C = HGRN_CHUNK
    H = HGRN_HEADS
    e_sum = (jnp.arange(SUBLANES * LANES)[:, None] // LANES == jnp.arange(C)[None, :] % SUBLANES
             ).astype(BF16)
    col = lambda off: pl.BlockSpec((1, ts, LANES), lambda b, h, s, off=off: (b, s, off + h))
    return pl.pallas_call(
        functools.partial(_hgrn_kernel, chunk=C),
        grid=(B, H, S // ts),
        in_specs=[col(0), col(H), col(2 * H), col(3 * H),
                  pl.BlockSpec((1, 1, LANES), lambda b, h, s: (h, 0, 0)),
                  pl.BlockSpec((1, LANES), lambda b, h, s: (0, 0)),
                  pl.BlockSpec((SUBLANES * LANES, C), lambda b, h, s: (0, 0))],
        out_specs=pl.BlockSpec((1, ts, LANES), lambda b, h, s: (b, s, h)),
        out_shape=jax.ShapeDtypeStruct((B, S, MIX_WIDTH), F32),
        scratch_shapes=[pltpu.VMEM((HGRN_DK, HGRN_DK), F32)],
        compiler_params=_params("parallel", "parallel", "arbitrary"),
        name="hgrn",
    )(h3, h3, h3, h3, lb.reshape(H, 1, LANES), norm_g.reshape(1, LANES), e_sum)


def _moba_kernel(sl_ref, q_ref, kf_ref, kb_ref, vb_ref, o_ref, kmean_ref):
    BLK = MOBA_BLOCK
    p = pl.program_id(1)
    i = pl.program_id(2)

    @pl.when(i == 0)
    def _():
        kmean_ref[...] = jnp.zeros_like(kmean_ref)

    qf = q_ref[0]
    own = pl.ds(pl.multiple_of(i * BLK, BLK), BLK)
    k_own = kb_ref[0, own, :]
    v_own = vb_ref[0, own, :]

    lane = lax.broadcasted_iota(jnp.int32, (BLK, LANES), 1)
    rr = lax.broadcasted_iota(jnp.int32, (BLK, BLK), 0)
    cc = lax.broadcasted_iota(jnp.int32, (BLK, BLK), 1)
    rel = (rr - cc).astype(F32)
    scale = MOBA_HEAD_DIM ** -0.5

    outs = []
    for hh in range(2):
        head = (lane < MOBA_HEAD_DIM) if hh == 0 else (lane >= MOBA_HEAD_DIM)
        slope = sl_ref[2 * p + hh]
        qh = jnp.where(head, qf, 0.0)

        gate = lax.dot_general(qh, kmean_ref[...], _NT, precision=lax.Precision.HIGHEST,
                               preferred_element_type=F32)
        past = lane < i
        g = jnp.where(past, gate, -jnp.inf)
        sel = jnp.zeros((BLK, LANES), jnp.bool_)
        for _ in range(MOBA_TOPK):
            mx = jnp.max(g, axis=1, keepdims=True)
            idx = jnp.min(jnp.where(g == mx, lane, LANES), axis=1, keepdims=True)
            pick = lane == idx
            sel = sel | pick
            g = jnp.where(pick, -jnp.inf, g)
        sel_bias = jnp.where(sel & past, 0.0, NEG_BIG)
        q_aug = jnp.concatenate([(qh * scale).astype(BF16), sel_bias.astype(BF16)], axis=1)

        bias = -slope * rel
        s = lax.dot_general((qh * scale).astype(BF16), k_own, _NT, preferred_element_type=F32)
        s = s + jnp.where(cc <= rr, bias, -jnp.inf)
        m0 = jnp.max(s, axis=1, keepdims=True)
        p0 = jnp.exp(s - m0)
        l0 = jnp.sum(p0, axis=1, keepdims=True)
        acc0 = jnp.dot(p0.astype(BF16), v_own, preferred_element_type=F32)

        def body(n, carry):
            m, l, acc = carry
            blk = pl.ds(pl.multiple_of(n * BLK, BLK), BLK)
            onehot = jnp.where(lane == n, 1.0, 0.0).astype(BF16)
            k_aug = jnp.concatenate([kb_ref[0, blk, :], onehot], axis=1)
            sn = lax.dot_general(q_aug, k_aug, _NT, preferred_element_type=F32)
            sn = sn + (bias - slope * ((i - n) * BLK).astype(F32))
            m_new = jnp.maximum(m, jnp.max(sn, axis=1, keepdims=True))
            a = jnp.exp(m - m_new)
            pn = jnp.exp(sn - m_new)
            l = a * l + jnp.sum(pn, axis=1, keepdims=True)
            acc = a * acc + jnp.dot(pn.astype(BF16), vb_ref[0, blk, :],
                                    preferred_element_type=F32)
            return m_new, l, acc

        _, l, acc = lax.fori_loop(0, i, body, (m0, l0, acc0))
        outs.append(acc / l)

    o_ref[0] = jnp.where(lane < MOBA_HEAD_DIM, outs[0], outs[1])
    kmean_ref[pl.ds(i, 1), :] = jnp.mean(kf_ref[0], axis=0, keepdims=True)


def _moba_mixer(h3, hb3):
    B, S, _ = h3.shape
    BLK = MOBA_BLOCK
    NP = MOBA_HEADS // 2
    slopes = jnp.asarray(_alibi_slope_list(MOBA_HEADS), F32)
    grid_spec = pltpu.PrefetchScalarGridSpec(
        num_scalar_prefetch=1,
        grid=(B, NP, S // BLK),
        in_specs=[pl.BlockSpec((1, BLK, LANES), lambda b, p, i, sl: (b, i, p)),
                  pl.BlockSpec((1, BLK, LANES), lambda b, p, i, sl: (b, i, NP + p)),
                  pl.BlockSpec((1, S, LANES), lambda b, p, i, sl: (b, 0, NP + p)),
                  pl.BlockSpec((1, S, LANES), lambda b, p, i, sl: (b, 0, 2 * NP + p))],
        out_specs=pl.BlockSpec((1, BLK, LANES), lambda b, p, i, sl: (b, i, p)),
        scratch_shapes=[pltpu.VMEM((LANES, LANES), F32)],
    )
    return pl.pallas_call(
        _moba_kernel,
        grid_spec=grid_spec,
        out_shape=jax.ShapeDtypeStruct((B, S, MIX_WIDTH), F32),
        compiler_params=_params("parallel", "parallel", "arbitrary"),
        name="moba",
    )(slopes, h3, h3, hb3, hb3)


def _moba_top3(qh, kmean, i, lane):
    gate = lax.dot_general(qh, kmean, _NT, precision=lax.Precision.HIGHEST,
                           preferred_element_type=F32)
    g = jnp.where(lane < i, gate, -jnp.inf)
    picks = []
    for _ in range(MOBA_TOPK):
        mx = jnp.max(g, axis=1, keepdims=True)
        idx = jnp.min(jnp.where(g == mx, lane, LANES), axis=1, keepdims=True)
        picks.append((idx, (mx > -jnp.inf) & (idx < i)))
        g = jnp.where(lane == idx, -jnp.inf, g)
    return picks


def _moba_select_kernel(q_ref, kf_ref, sr_ref, cnt_ref, kmean_ref):
    BLK = MOBA_BLOCK
    i = pl.program_id(2)

    @pl.when(i == 0)
    def _():
        kmean_ref[...] = jnp.zeros_like(kmean_ref)

    qf = q_ref[0]
    lane = lax.broadcasted_iota(jnp.int32, (BLK, LANES), 1)
    rr = lax.broadcasted_iota(jnp.int32, (BLK, BLK), 0)
    cc = lax.broadcasted_iota(jnp.int32, (BLK, BLK), 1)
    before = jnp.where(cc < rr, 1.0, 0.0).astype(BF16)
    out = jnp.zeros((BLK, LANES), jnp.int32)
    counts = []
    for hh in range(2):
        head = (lane >> 6) == hh
        picks = _moba_top3(jnp.where(head, qf, 0.0), kmean_ref[...], i, lane)
        chosen = jnp.zeros((BLK, LANES), F32)
        for idx, valid in picks:
            chosen = chosen + jnp.where((lane == idx) & valid, 1.0, 0.0)
        earlier = jnp.dot(before, chosen.astype(BF16), preferred_element_type=F32)
        for j, (idx, valid) in enumerate(picks):
            rep = hh * MOBA_TOPK + j
            rank = jnp.sum(jnp.where(lane == idx, earlier, 0.0), axis=1, keepdims=True)
            out = jnp.where(lane == rep, jnp.where(valid, idx, -1), out)
            out = jnp.where(lane == SUBLANES + rep, rank.astype(jnp.int32), out)
        total = earlier[BLK - 1:BLK, :] + chosen[BLK - 1:BLK, :]
        counts.append(jnp.broadcast_to(total, (SUBLANES, LANES)))
    sr_ref[0, 0] = out
    cnt_ref[0, 0] = jnp.concatenate(counts, axis=0)
    kmean_ref[pl.ds(i, 1), :] = jnp.mean(kf_ref[0], axis=0, keepdims=True)


def _moba_dest_kernel(sr_ref, tab_ref, idx_ref, *, trash):
    BLK = MOBA_BLOCK
    blk = sr_ref[0, 0]
    tab = tab_ref[0, 0]
    lane = lax.broadcasted_iota(jnp.int32, (BLK, LANES), 1)
    d = jnp.full((BLK, LANES), float(trash), F32)
    for rep in range(2 * MOBA_TOPK):
        hh = rep // MOBA_TOPK
        sel = blk[:, rep:rep + 1]
        rank = blk[:, SUBLANES + rep:SUBLANES + rep + 1]
        start = jnp.sum(jnp.where(lane == sel, tab[hh:hh + 1, :], 0.0), axis=1, keepdims=True)
        dest = jnp.where(sel >= 0, start + rank.astype(F32), float(trash))
        d = jnp.where(lane == rep, dest, d)
    idx_ref[...] = d.T[:SC_IDX_ROWS, :].astype(jnp.int32)


def _moba_tile_kernel(tb_ref, tp_ref, th_ref, tn_ref, nu_ref, sl_ref, q_ref, k_ref, v_ref, o_ref):
    t = pl.program_id(0)

    @pl.when(t < nu_ref[0])
    def _():
        tq = q_ref.shape[0]
        hh = th_ref[t]
        slope = sl_ref[2 * tp_ref[t] + hh]
        lane = lax.broadcasted_iota(jnp.int32, (tq, LANES), 1)
        head = (lane >> 6) == hh
        q = jnp.where(head, q_ref[...] * (MOBA_HEAD_DIM ** -0.5), 0.0).astype(BF16)
        s = lax.dot_general(q, k_ref[0], _NT, preferred_element_type=F32)
        kpos = lax.broadcasted_iota(jnp.int32, s.shape, 1).astype(F32)
        s = s + slope * kpos
        m = jnp.max(s, axis=1, keepdims=True)
        p = jnp.exp(s - m)
        l = jnp.sum(p, axis=1, keepdims=True)
        o = jnp.dot(p.astype(BF16), v_ref[0], preferred_element_type=F32)
        half = lane & (MOBA_HEAD_DIM - 1)
        stats = jnp.where(half == 0, m, jnp.where(half == 1, l, 0.0))
        o_ref[...] = jnp.where(head, o, stats)


def _moba_merge_kernel(sl_ref, q_ref, k_ref, v_ref, sr_ref, pg_ref, o_ref):
    BLK = MOBA_BLOCK
    p = pl.program_id(1)
    i = pl.program_id(2)
    qf = q_ref[0]
    k_own = k_ref[0]
    v_own = v_ref[0]
    blk = sr_ref[0, 0]
    lane = lax.broadcasted_iota(jnp.int32, (BLK, LANES), 1)
    rr = lax.broadcasted_iota(jnp.int32, (BLK, BLK), 0)
    cc = lax.broadcasted_iota(jnp.int32, (BLK, BLK), 1)
    rel = (rr - cc).astype(F32)
    row = lax.broadcasted_iota(jnp.int32, (BLK, 1), 0)
    outs = []
    for hh in range(2):
        head = (lane >> 6) == hh
        slope = sl_ref[2 * p + hh]
        qh = jnp.where(head, qf * (MOBA_HEAD_DIM ** -0.5), 0.0).astype(BF16)
        s = lax.dot_general(qh, k_own, _NT, preferred_element_type=F32)
        s = s + jnp.where(cc <= rr, -slope * rel, -jnp.inf)
        m = jnp.max(s, axis=1, keepdims=True)
        pr = jnp.exp(s - m)
        ms = [m]
        ls = [jnp.sum(pr, axis=1, keepdims=True)]
        accs = [jnp.dot(pr.astype(BF16), v_own, preferred_element_type=F32)]
        stat_lane = MOBA_HEAD_DIM * (1 - hh)
        for j in range(MOBA_TOPK):
            rep = hh * MOBA_TOPK + j
            part = pg_ref[rep, 0]
            sel = blk[:, rep:rep + 1]
            valid = sel >= 0
            shift = -slope * ((i - sel) * BLK + row).astype(F32)
            ms.append(jnp.where(valid, part[:, stat_lane:stat_lane + 1] + shift, -jnp.inf))
            ls.append(jnp.where(valid, part[:, stat_lane + 1:stat_lane + 2], 0.0))
            accs.append(jnp.where(valid, part, 0.0))
        top = functools.reduce(jnp.maximum, ms)
        ws = [jnp.exp(mk - top) for mk in ms]
        den = sum(w * lk for w, lk in zip(ws, ls))
        num = sum(w * ak for w, ak in zip(ws, accs))
        outs.append(num / den)
    o_ref[0] = jnp.where(lane < MOBA_HEAD_DIM, outs[0], outs[1])


def _moba_sparse_mixer(h3, hb3, q6):
    B, S, _ = h3.shape
    T = B * S
    BLK = MOBA_BLOCK
    TQ = MOBA_TILE
    NB = S // BLK
    NP = MOBA_HEADS // 2
    n_rep = 2 * MOBA_TOPK
    slopes = jnp.asarray(_alibi_slope_list(MOBA_HEADS), F32)
    step = lambda *rest: pl.BlockSpec((1, 1) + rest, lambda b, p, i: (b, p, i) + (0,) * (len(rest) - 1))

    selrank, counts = pl.pallas_call(
        _moba_select_kernel,
        grid=(B, NP, NB),
        in_specs=[pl.BlockSpec((1, BLK, LANES), lambda b, p, i: (b, i, p)),
                  pl.BlockSpec((1, BLK, LANES), lambda b, p, i: (b, i, NP + p))],
        out_specs=[step(BLK, LANES), step(2 * SUBLANES, LANES)],
        out_shape=[jax.ShapeDtypeStruct((B, NP, S, LANES), jnp.int32),
                   jax.ShapeDtypeStruct((B, NP, NB * 2 * SUBLANES, LANES), F32)],
        scratch_shapes=[pltpu.VMEM((LANES, LANES), F32)],
        compiler_params=_params("parallel", "parallel", "arbitrary"),
        name="moba_select",
    )(h3, h3)

    cnt = counts.reshape(B, NP, NB, 2, SUBLANES, LANES)[:, :, :, :, 0, :NB].astype(jnp.int32)
    base = jnp.cumsum(cnt, axis=2) - cnt
    total = jnp.sum(cnt, axis=2)
    padded = (total + TQ - 1) // TQ * TQ
    pend = jnp.cumsum(padded.reshape(-1))
    seg_start = (pend - padded.reshape(-1)).reshape(B, NP, 1, 2, NB)
    table = jnp.zeros((B, NP, NB, SUBLANES, LANES), F32).at[:, :, :, :2, :NB].set(
        (seg_start + base).astype(F32)).reshape(B, NP, NB * SUBLANES, LANES)
    n_seg = B * NP * 2 * NB
    max_tiles = (T * NP * n_rep) // TQ + n_seg
    tile_seg = jnp.minimum(
        jnp.searchsorted(pend, jnp.arange(max_tiles, dtype=jnp.int32) * TQ, side='right'),
        n_seg - 1).astype(jnp.int32)
    tile_n = tile_seg % NB
    tile_h = (tile_seg // NB) % 2
    tile_p = (tile_seg // (2 * NB)) % NP
    tile_b = tile_seg // (2 * NB * NP)
    n_used = (pend[-1] // TQ).astype(jnp.int32).reshape(1)
    trash = max_tiles * TQ

    idx = pl.pallas_call(
        functools.partial(_moba_dest_kernel, trash=trash),
        grid=(B, NP, NB),
        in_specs=[step(BLK, LANES), step(SUBLANES, LANES)],
        out_specs=pl.BlockSpec((SC_IDX_ROWS, BLK), lambda b, p, i: (0, p * (T // BLK) + b * NB + i)),
        out_shape=jax.ShapeDtypeStruct((SC_IDX_ROWS, NP * T), jnp.int32),
        compiler_params=_params("parallel", "parallel", "parallel"),
        name="moba_dest",
    )(selrank, table)

    n_rows = (max_tiles + 1) * TQ
    qs = _sc_scatter_rows(q6.reshape(NP * T, LANES), idx, n_rep, n_rows)

    live = lambda t, tb, tp, th, tn, nu, sl: (jnp.where(t < nu[0], t, max_tiles), 0)
    kv_blk = lambda off: pl.BlockSpec(
        (1, BLK, LANES), lambda t, tb, tp, th, tn, nu, sl, off=off: (tb[t], tn[t], off + tp[t]))
    part = pl.pallas_call(
        _moba_tile_kernel,
        grid_spec=pltpu.PrefetchScalarGridSpec(
            num_scalar_prefetch=6,
            grid=(max_tiles,),
            in_specs=[pl.BlockSpec((TQ, LANES), live), kv_blk(NP), kv_blk(2 * NP)],
            out_specs=pl.BlockSpec((TQ, LANES), live)),
        out_shape=jax.ShapeDtypeStruct((n_rows, LANES), F32),
        compiler_params=_params("arbitrary"),
        name="moba_tiles",
    )(tile_b, tile_p, tile_h, tile_n, n_used, slopes, qs, hb3, hb3)

    pg = _sc_gather_rows(part, idx[:n_rep].reshape(-1)).reshape(n_rep, NP, T, LANES)

    own = lambda off: pl.BlockSpec((1, BLK, LANES), lambda b, p, i, sl, off=off: (b, i, off + p))
    return pl.pallas_call(
        _moba_merge_kernel,
        grid_spec=pltpu.PrefetchScalarGridSpec(
            num_scalar_prefetch=1,
            grid=(B, NP, NB),
            in_specs=[own(0), own(NP), own(2 * NP),
                      pl.BlockSpec((1, 1, BLK, LANES), lambda b, p, i, sl: (b, p, i, 0)),
                      pl.BlockSpec((n_rep, 1, BLK, LANES), lambda b, p, i, sl: (0, p, b * NB + i, 0))],
            out_specs=pl.BlockSpec((1, BLK, LANES), lambda b, p, i, sl: (b, i, p))),
        out_shape=jax.ShapeDtypeStruct((B, S, MIX_WIDTH), F32),
        compiler_params=_params("parallel", "parallel", "parallel"),
        name="moba_merge",
    )(slopes, h3, hb3, hb3, selrank, pg)


def _moba_pick_kernel(q_ref, kf_ref, sr_ref, cnt_ref, kmean_ref):
    BLK = MOBA_BLOCK
    nbp = kmean_ref.shape[0]
    i = pl.program_id(2)

    @pl.when(i == 0)
    def _():
        kmean_ref[...] = jnp.zeros_like(kmean_ref)

    qf = q_ref[0]
    lane = lax.broadcasted_iota(jnp.int32, (BLK, LANES), 1)
    nrow = lax.broadcasted_iota(jnp.int32, (nbp, BLK), 0)
    orow = lax.broadcasted_iota(jnp.int32, (2 * SUBLANES, BLK), 0)
    lane_c = lax.broadcasted_iota(jnp.int32, (nbp, LANES), 1)
    qa = lax.broadcasted_iota(jnp.int32, (BLK, BLK), 0)
    qb = lax.broadcasted_iota(jnp.int32, (BLK, BLK), 1)
    before = jnp.where(qa < qb, 1.0, 0.0).astype(BF16)
    out = jnp.zeros((2 * SUBLANES, BLK), jnp.int32)
    cnt = jnp.zeros((nbp, LANES), F32)
    for hh in range(2):
        qh = jnp.where((lane >> 6) == hh, qf, 0.0)
        gate = lax.dot_general(kmean_ref[...], qh, _NT, precision=lax.Precision.HIGHEST,
                               preferred_element_type=F32)
        g = jnp.where(nrow < i, gate, -jnp.inf)
        picks = []
        for _ in range(MOBA_TOPK):
            mx = jnp.max(g, axis=0, keepdims=True)
            idx = jnp.min(jnp.where(g == mx, nrow, nbp), axis=0, keepdims=True)
            picks.append((idx, (mx > -jnp.inf) & (idx < i)))
            g = jnp.where(nrow == idx, -jnp.inf, g)
        chosen = jnp.zeros((nbp, BLK), F32)
        for idx, valid in picks:
            chosen = chosen + jnp.where((nrow == idx) & valid, 1.0, 0.0)
        earlier = jnp.dot(chosen.astype(BF16), before, preferred_element_type=F32)
        for j, (idx, valid) in enumerate(picks):
            rep = hh * MOBA_TOPK + j
            rank = jnp.sum(jnp.where(nrow == idx, earlier, 0.0), axis=0, keepdims=True)
            out = jnp.where(orow == rep, jnp.where(valid, idx, -1), out)
            out = jnp.where(orow == SUBLANES + rep, rank.astype(jnp.int32), out)
        cnt = jnp.where(lane_c == hh, jnp.sum(chosen, axis=1, keepdims=True), cnt)
    sr_ref[0, 0] = out
    cnt_ref[0, 0] = cnt
    kmean_ref[pl.ds(i, 1), :] = jnp.mean(kf_ref[0], axis=0, keepdims=True)


def _moba_place_kernel(sr_ref, tab_ref, idx_ref, *, group, spare):
    BLK = MOBA_BLOCK
    n_rep = 2 * MOBA_TOPK
    b = pl.program_id(0)
    p = pl.program_id(1)
    ig = pl.program_id(2)
    nrow = lax.broadcasted_iota(jnp.int32, (LANES, BLK), 0)
    orow = lax.broadcasted_iota(jnp.int32, (SC_IDX_ROWS, BLK), 0)
    qpos = lax.broadcasted_iota(jnp.int32, (1, BLK), 1)
    for g in range(group):
        blk = sr_ref[0, 0, :, g * BLK:(g + 1) * BLK]
        tab_t = tab_ref[0, 0, g * SUBLANES:(g + 1) * SUBLANES, :].T
        early = jnp.minimum(ig * group + g, MOBA_TOPK - 1)
        out = jnp.zeros((SC_IDX_ROWS, BLK), jnp.int32)
        for rep in range(n_rep):
            hh = rep // MOBA_TOPK
            sel = blk[rep:rep + 1, :]
            rank = blk[SUBLANES + rep:SUBLANES + rep + 1, :]
            start = jnp.sum(jnp.where(nrow == sel, tab_t[:, hh:hh + 1], 0.0), axis=0, keepdims=True)
            unused = spare + (((b * pl.num_programs(1) + p) * MOBA_TOPK + early) * n_rep + rep) * BLK
            dest = jnp.where(sel >= 0, start.astype(jnp.int32) + rank, unused + qpos)
            out = jnp.where(orow == rep, dest, out)
        idx_ref[:, g * BLK:(g + 1) * BLK] = out


def _moba_group_kernel(tb_ref, tp_ref, th_ref, tn_ref, nu_ref, sl_ref, q_ref, *refs, group):
    k_refs, v_refs, o_ref = refs[:group], refs[group:2 * group], refs[2 * group]
    t = pl.program_id(0)
    tq = q_ref.shape[0] // group

    @pl.when(t * group < nu_ref[0])
    def _():
        lane = lax.broadcasted_iota(jnp.int32, (tq, LANES), 1)
        kpos = lax.broadcasted_iota(jnp.int32, (1, MOBA_BLOCK), 1)
        for g in range(group):
            tt = t * group + g
            hh = th_ref[tt]
            slope = sl_ref[2 * tp_ref[tt] + hh]
            head = (lane >> 6) == hh
            rows = pl.ds(g * tq, tq)
            q = jnp.where(head, q_ref[rows, :] * (MOBA_HEAD_DIM ** -0.5), 0.0).astype(BF16)
            s = lax.dot_general(q, k_refs[g][0], _NT, preferred_element_type=F32)
            s = s + slope * (kpos + tn_ref[tt] * MOBA_BLOCK).astype(F32)
            m = jnp.max(s, axis=1, keepdims=True)
            pr = jnp.exp(s - m)
            l = jnp.sum(pr, axis=1, keepdims=True)
            o = jnp.dot(pr.astype(BF16), v_refs[g][0], preferred_element_type=F32) / l
            o_ref[rows, :] = jnp.where(head, o, m + jnp.log(l))


def _moba_join_kernel(sl_ref, q_ref, k_ref, v_ref, pg_ref, o_ref):
    BLK = MOBA_BLOCK
    p = pl.program_id(1)
    i = pl.program_id(2)
    qf = q_ref[0]
    k_own = k_ref[0]
    v_own = v_ref[0]
    lane = lax.broadcasted_iota(jnp.int32, (BLK, LANES), 1)
    rr = lax.broadcasted_iota(jnp.int32, (BLK, BLK), 0)
    cc = lax.broadcasted_iota(jnp.int32, (BLK, BLK), 1)
    key_pos = (lax.broadcasted_iota(jnp.int32, (1, BLK), 1) + i * BLK).astype(F32)
    outs = []
    for hh in range(2):
        head = (lane >> 6) == hh
        slope = sl_ref[2 * p + hh]
        qh = jnp.where(head, qf * (MOBA_HEAD_DIM ** -0.5), 0.0).astype(BF16)
        s = lax.dot_general(qh, k_own, _NT, preferred_element_type=F32)
        s = jnp.where(cc <= rr, s + slope * key_pos, -jnp.inf)
        m = jnp.max(s, axis=1, keepdims=True)
        pr = jnp.exp(s - m)
        l = jnp.sum(pr, axis=1, keepdims=True)
        lses = [m + jnp.log(l)]
        vals = [jnp.dot(pr.astype(BF16), v_own, preferred_element_type=F32) / l]
        for j in range(MOBA_TOPK):
            part = pg_ref[hh * MOBA_TOPK + j, 0]
            has_block = j < i
            lses.append(jnp.where(has_block, pltpu.roll(part, MOBA_HEAD_DIM, 1), -jnp.inf))
            vals.append(jnp.where(has_block, part, 0.0))
        top = functools.reduce(jnp.maximum, lses)
        ws = [jnp.exp(x - top) for x in lses]
        outs.append(sum(w * v for w, v in zip(ws, vals)) / sum(ws))
    o_ref[0] = jnp.where(lane < MOBA_HEAD_DIM, outs[0], outs[1])


def _moba_mixer_v2(h3, hb3, q6):
    B, S, _ = h3.shape
    T = B * S
    BLK = MOBA_BLOCK
    TQ = MOBA_TILE
    G = MOBA_TILE_GROUP
    NB = S // BLK
    GI = math.gcd(NB, MOBA_PLACE_GROUP)
    nbp = -(-NB // SUBLANES) * SUBLANES
    NP = MOBA_HEADS // 2
    n_rep = 2 * MOBA_TOPK
    slopes = jnp.asarray(_alibi_slope_list(MOBA_HEADS), F32)

    selrank, counts = pl.pallas_call(
        _moba_pick_kernel,
        grid=(B, NP, NB),
        in_specs=[pl.BlockSpec((1, BLK, LANES), lambda b, p, i: (b, i, p)),
                  pl.BlockSpec((1, BLK, LANES), lambda b, p, i: (b, i, NP + p))],
        out_specs=[pl.BlockSpec((1, 1, 2 * SUBLANES, BLK), lambda b, p, i: (b, p, 0, i)),
                   pl.BlockSpec((1, 1, nbp, LANES), lambda b, p, i: (b, p, i, 0))],
        out_shape=[jax.ShapeDtypeStruct((B, NP, 2 * SUBLANES, S), jnp.int32),
                   jax.ShapeDtypeStruct((B, NP, NB * nbp, LANES), F32)],
        scratch_shapes=[pltpu.VMEM((nbp, LANES), F32)],
        compiler_params=_params("parallel", "parallel", "arbitrary"),
        name="moba_pick",
    )(h3, h3)

    cnt = counts.reshape(B, NP, NB, nbp, LANES)[:, :, :, :NB, :2].astype(jnp.int32)
    cnt = cnt.transpose(0, 1, 2, 4, 3)
    base = jnp.cumsum(cnt, axis=2) - cnt
    total = jnp.sum(cnt, axis=2)
    padded = (total + TQ - 1) // TQ * TQ
    pend = jnp.cumsum(padded.reshape(-1))
    seg_start = (pend - padded.reshape(-1)).reshape(B, NP, 1, 2, NB)
    table = jnp.zeros((B, NP, NB, SUBLANES, LANES), F32).at[:, :, :, :2, :NB].set(
        (seg_start + base).astype(F32)).reshape(B, NP, NB * SUBLANES, LANES)
    n_seg = B * NP * 2 * NB
    max_tiles = -(-((T * NP * n_rep) // TQ + n_seg) // G) * G
    first_row = jnp.arange(max_tiles, dtype=jnp.int32) * TQ
    tile_seg = jnp.minimum(jnp.sum(pend[None, :] <= first_row[:, None], axis=1), n_seg - 1)
    tile_n = (tile_seg % NB).astype(jnp.int32)
    tile_h = ((tile_seg // NB) % 2).astype(jnp.int32)
    tile_p = ((tile_seg // (2 * NB)) % NP).astype(jnp.int32)
    tile_b = (tile_seg // (2 * NB * NP)).astype(jnp.int32)
    n_used = (pend[-1] // TQ).astype(jnp.int32).reshape(1)
    spare = max_tiles * TQ
    n_rows = spare + max(B * NP * MOBA_TOPK * n_rep * BLK, G * TQ)

    idx = pl.pallas_call(
        functools.partial(_moba_place_kernel, group=GI, spare=spare),
        grid=(B, NP, NB // GI),
        in_specs=[pl.BlockSpec((1, 1, 2 * SUBLANES, GI * BLK), lambda b, p, i: (b, p, 0, i)),
                  pl.BlockSpec((1, 1, GI * SUBLANES, LANES), lambda b, p, i: (b, p, i, 0))],
        out_specs=pl.BlockSpec((SC_IDX_ROWS, GI * BLK),
                               lambda b, p, i: (0, p * (T // (GI * BLK)) + b * (NB // GI) + i)),
        out_shape=jax.ShapeDtypeStruct((SC_IDX_ROWS, NP * T), jnp.int32),
        compiler_params=_params("parallel", "parallel", "parallel"),
        name="moba_place",
    )(selrank, table)

    qs = _sc_scatter_rows(q6.reshape(NP * T, LANES), idx, n_rep, n_rows)

    live = lambda t, tb, tp, th, tn, nu, sl: (jnp.where(t * G < nu[0], t, max_tiles // G), 0)
    kv_blk = lambda off, g: pl.BlockSpec(
        (1, BLK, LANES),
        lambda t, tb, tp, th, tn, nu, sl: (tb[t * G + g], tn[t * G + g], off + tp[t * G + g]))
    part = pl.pallas_call(
        functools.partial(_moba_group_kernel, group=G),
        grid_spec=pltpu.PrefetchScalarGridSpec(
            num_scalar_prefetch=6,
            grid=(max_tiles // G,),
            in_specs=([pl.BlockSpec((G * TQ, LANES), live)]
                      + [kv_blk(NP, g) for g in range(G)] + [kv_blk(2 * NP, g) for g in range(G)]),
            out_specs=pl.BlockSpec((G * TQ, LANES), live)),
        out_shape=jax.ShapeDtypeStruct((n_rows, LANES), F32),
        compiler_params=_params("arbitrary"),
        name="moba_tiles",
    )(tile_b, tile_p, tile_h, tile_n, n_used, slopes, qs, *([hb3] * (2 * G)))

    pg = _sc_gather_rows(part, idx[:n_rep].reshape(-1)).reshape(n_rep, NP, T, LANES)

    own = lambda off: pl.BlockSpec((1, BLK, LANES), lambda b, p, i, sl, off=off: (b, i, off + p))
    return pl.pallas_call(
        _moba_join_kernel,
        grid_spec=pltpu.PrefetchScalarGridSpec(
            num_scalar_prefetch=1,
            grid=(B, NP, NB),
            in_specs=[own(0), own(NP), own(2 * NP),
                      pl.BlockSpec((n_rep, 1, BLK, LANES), lambda b, p, i, sl: (0, p, b * NB + i, 0))],
            out_specs=pl.BlockSpec((1, BLK, LANES), lambda b, p, i, sl: (b, i, p))),
        out_shape=jax.ShapeDtypeStruct((B, S, MIX_WIDTH), F32),
        compiler_params=_params("parallel", "parallel", "parallel"),
        name="moba_join",
    )(slopes, h3, hb3, hb3, pg)


def _memkv_kernel(mem_ref, w_ref, kv_ref):
    kv_ref[0] = jnp.dot(mem_ref[0].astype(BF16), w_ref[...].astype(BF16),
                        preferred_element_type=F32).astype(BF16)


def _memkv(mem, w_kv):
    B, M, D = mem.shape
    N = w_kv.shape[1]
    return pl.pallas_call(
        _memkv_kernel,
        grid=(B,),
        in_specs=[pl.BlockSpec((1, M, D), lambda b: (b, 0, 0)),
                  pl.BlockSpec((D, N), lambda b: (0, 0))],
        out_specs=pl.BlockSpec((1, M, N), lambda b: (b, 0, 0)),
        out_shape=jax.ShapeDtypeStruct((B, M, N), BF16),
        compiler_params=_params("parallel"),
        name="memkv",
    )(mem, w_kv)


def _layer_norm(z, g, b):
    mu = jnp.mean(z, axis=-1, keepdims=True)
    zc = z - mu
    var = jnp.mean(zc * zc, axis=-1, keepdims=True)
    return zc * lax.rsqrt(var + LN_EPS) * g + b


def _post_kernel(x_ref, mix_ref, mq_ref, kv_ref, wo_ref, g_ref, b_ref, wr_ref, br_ref,
                 x1_ref, x1s_ref, idx_ref, gate_ref, *, alpha):
    tm = x_ref.shape[0]
    mq = mq_ref[...]
    kv = kv_ref[0]
    km = kv[:, :MEM_WIDTH]
    vm = kv[:, MEM_WIDTH:]
    lane = lax.broadcasted_iota(jnp.int32, (tm, MEM_WIDTH), 1)
    scale = MEM_HEAD_DIM ** -0.5
    mo = jnp.zeros((tm, MEM_WIDTH), F32)
    for hd in range(MEM_HEADS):
        head = (lane >> 6) == hd
        qh = jnp.where(head, mq * scale, 0.0).astype(BF16)
        s = lax.dot_general(qh, km, _NT, preferred_element_type=F32)
        m = jnp.max(s, axis=1, keepdims=True)
        p = jnp.exp(s - m)
        l = jnp.sum(p, axis=1, keepdims=True)
        oh = jnp.dot(p.astype(BF16), vm, preferred_element_type=F32) / l
        mo = jnp.where(head, oh, mo)

    y = jnp.dot(mix_ref[...].astype(BF16), wo_ref[:MIX_WIDTH, :], preferred_element_type=F32)
    y = y + jnp.dot(mo.astype(BF16), wo_ref[MIX_WIDTH:, :], preferred_element_type=F32)
    x1 = _layer_norm(alpha * x_ref[...] + y, g_ref[...], b_ref[...])
    x1_ref[...] = x1
    _store_subrows(x1s_ref, x1)

    x_hi = x1.astype(BF16)
    x_lo = (x1 - x_hi.astype(F32)).astype(BF16)
    logits = (jnp.dot(x_hi, wr_ref[0], preferred_element_type=F32)
              + jnp.dot(x_lo, wr_ref[0], preferred_element_type=F32)
              + jnp.dot(x_hi, wr_ref[1], preferred_element_type=F32) + br_ref[...])
    lane_e = lax.broadcasted_iota(jnp.int32, (tm, LANES), 1)
    g = jnp.where(lane_e < N_EXPERTS, logits, -jnp.inf)
    idx_out = jnp.zeros((tm, LANES), jnp.int32)
    val_out = jnp.full((tm, LANES), -jnp.inf, F32)
    for kk in range(TOP_K):
        mx = jnp.max(g, axis=1, keepdims=True)
        idx = jnp.min(jnp.where(g == mx, lane_e, LANES), axis=1, keepdims=True)
        idx_out = jnp.where(lane_e == kk, idx, idx_out)
        val_out = jnp.where(lane_e == kk, mx, val_out)
        g = jnp.where(lane_e == idx, -jnp.inf, g)
    vmax = jnp.max(val_out, axis=1, keepdims=True)
    ev = jnp.exp(val_out - vmax)
    idx_ref[...] = idx_out
    gate_ref[...] = ev / jnp.sum(ev, axis=1, keepdims=True)


def _post_mixer(x2, mix2, h2, kv, w_o_bf16, ln_g, ln_b, w_router, b_router, seq_len, alpha):
    T, D = x2.shape
    tm = POST_ROWS
    N = h2.shape[1]
    M = kv.shape[1]
    mq_col = (N - MEM_WIDTH) // MEM_WIDTH
    tiles_per_seq = seq_len // tm
    wr = jnp.zeros((D, LANES), F32).at[:, :N_EXPERTS].set(w_router)
    wr_hi = wr.astype(BF16)
    wr = jnp.stack([wr_hi, (wr - wr_hi.astype(F32)).astype(BF16)])
    br = jnp.zeros((1, LANES), F32).at[0, :N_EXPERTS].set(b_router)
    row = lambda n: pl.BlockSpec((tm, n), lambda i: (i, 0))
    full = lambda a, b: pl.BlockSpec((a, b), lambda i: (0, 0))
    return pl.pallas_call(
        functools.partial(_post_kernel, alpha=alpha),
        grid=(T // tm,),
        in_specs=[row(D), row(MIX_WIDTH),
                  pl.BlockSpec((tm, MEM_WIDTH), lambda i: (i, mq_col)),
                  pl.BlockSpec((1, M, 2 * MEM_WIDTH), lambda i: (i // tiles_per_seq, 0, 0)),
                  full(D, D), full(1, D), full(1, D),
                  pl.BlockSpec((2, D, LANES), lambda i: (0, 0, 0)), full(1, LANES)],
        out_specs=[row(D), pl.BlockSpec((tm * (D // LANES), LANES), lambda i: (i, 0)),
                   row(LANES), row(LANES)],
        out_shape=[jax.ShapeDtypeStruct((T, D), F32),
                   jax.ShapeDtypeStruct((T * (D // LANES), LANES), F32),
                   jax.ShapeDtypeStruct((T, LANES), jnp.int32),
                   jax.ShapeDtypeStruct((T, LANES), F32)],
        compiler_params=_params("parallel"),
        name="post_mixer",
    )(x2, mix2, h2, kv, w_o_bf16, ln_g.reshape(1, D), ln_b.reshape(1, D), wr, br)


def _sc_mesh():
    return plsc.VectorSubcoreMesh(core_axis_name="core", subcore_axis_name="subcore")


def _sc_scatter_rows(rows, idx, n_rep, n_out):
    R, W = rows.shape

    @functools.partial(pl.kernel, out_type=jax.ShapeDtypeStruct((n_out, W), rows.dtype),
                       mesh=_sc_mesh(), scratch_types=[])
    def scatter(x_hbm, i_hbm, o_hbm):
        def body(x_vmem, i_vmem):
            for r in range(n_rep):
                pltpu.sync_copy(x_vmem, o_hbm.at[i_vmem.at[r]])

        pltpu.emit_pipeline(
            body, grid=(R // SC_WINDOW,),
            in_specs=[pl.BlockSpec((SC_WINDOW, W), lambda i: (i, 0)),
                      pl.BlockSpec((SC_IDX_ROWS, SC_WINDOW), lambda i: (0, i))],
            out_specs=[], core_axis_name=("core", "subcore"),
            dimension_semantics=(pltpu.PARALLEL,), trace_scopes=False)(x_hbm, i_hbm)

    return scatter(rows, idx)


def _sc_gather_rows(table, idx):
    n = idx.shape[0]
    W = table.shape[1]

    @functools.partial(pl.kernel, out_type=jax.ShapeDtypeStruct((n, W), table.dtype),
                       mesh=_sc_mesh(), scratch_types=[])
    def gather(t_hbm, i_hbm, o_hbm):
        def body(i_vmem, o_vmem):
            pltpu.sync_copy(t_hbm.at[i_vmem.at[0]], o_vmem)

        pltpu.emit_pipeline(
            body, grid=(n // SC_WINDOW,),
            in_specs=[pl.BlockSpec((1, SC_WINDOW), lambda i: (0, i))],
            out_specs=[pl.BlockSpec((SC_WINDOW, W), lambda i: (i, 0))],
            core_axis_name=("core", "subcore"),
            dimension_semantics=(pltpu.PARALLEL,), trace_scopes=False)(i_hbm, o_hbm)

    return gather(table, idx.reshape(1, n))


def _sc_workers():
    info = pltpu.get_tpu_info().sparse_core
    return info.num_cores, info.num_cores * info.num_subcores


def _sc_scatter_slabs(rows, idx, n_rep, n_out):
    R, S, W = rows.shape
    n_cores, n_workers = _sc_workers()
    per_worker = (R // SC_WINDOW) // n_workers
    assert per_worker * n_workers * SC_WINDOW == R

    @functools.partial(pl.kernel, out_type=jax.ShapeDtypeStruct((n_out, S, W), rows.dtype),
                       mesh=_sc_mesh(),
                       scratch_types=[pltpu.VMEM((SC_IDX_ROWS, SC_WINDOW), jnp.int32),
                                      pltpu.VMEM((SC_CHUNK, S, W), rows.dtype)])
    def scatter(x_hbm, i_hbm, o_hbm, ibuf, buf):
        wid = lax.axis_index("subcore") * n_cores + lax.axis_index("core")

        @pl.loop(0, per_worker)
        def _(s):
            first = (wid * per_worker + s) * SC_WINDOW
            pltpu.sync_copy(i_hbm.at[:, pl.ds(first, SC_WINDOW)], ibuf)
            for c in range(SC_WINDOW // SC_CHUNK):
                pltpu.sync_copy(x_hbm.at[pl.ds(first + c * SC_CHUNK, SC_CHUNK)], buf)
                for r in range(n_rep):
                    pltpu.sync_copy(buf, o_hbm.at[ibuf.at[r, pl.ds(c * SC_CHUNK, SC_CHUNK)]])

    return scatter(rows, idx)


def _sc_gather_slabs(table, idx):
    n = idx.shape[0]
    S, W = table.shape[1:]
    n_cores, n_workers = _sc_workers()
    per_worker = (n // SC_WINDOW) // n_workers
    assert per_worker * n_workers * SC_WINDOW == n
    n_chunks = SC_WINDOW // SC_CHUNK

    @functools.partial(pl.kernel, out_type=jax.ShapeDtypeStruct((n, S, W), table.dtype),
                       mesh=_sc_mesh(),
                       scratch_types=[pltpu.VMEM((1, SC_WINDOW), jnp.int32),
                                      pltpu.VMEM((2, SC_CHUNK, S, W), table.dtype),
                                      pltpu.SemaphoreType.DMA((2,)), pltpu.SemaphoreType.DMA((2,))])
    def gather(t_hbm, i_hbm, o_hbm, ibuf, buf, fetch_sem, store_sem):
        wid = lax.axis_index("subcore") * n_cores + lax.axis_index("core")

        @pl.loop(0, per_worker)
        def _(s):
            blk = wid * per_worker + s
            pltpu.sync_copy(i_hbm.at[pl.ds(blk, 1)], ibuf)

            def fetch(c):
                return pltpu.make_async_copy(
                    t_hbm.at[ibuf.at[0, pl.ds(c * SC_CHUNK, SC_CHUNK)]], buf.at[c % 2],
                    fetch_sem.at[c % 2])

            def store(c):
                return pltpu.make_async_copy(
                    buf.at[c % 2], o_hbm.at[pl.ds(blk * SC_WINDOW + c * SC_CHUNK, SC_CHUNK)],
                    store_sem.at[c % 2])

            fetch(0).start()
            for c in range(n_chunks):
                if c + 1 < n_chunks:
                    if c >= 1:
                        store(c - 1).wait()
                    fetch(c + 1).start()
                fetch(c).wait()
                store(c).start()
            store(n_chunks - 2).wait()
            store(n_chunks - 1).wait()

    return gather(table, idx.reshape(n // SC_WINDOW, SC_WINDOW))


def _route(top_idx, n_tokens):
    rb = MOE_ROWS
    tk = n_tokens * TOP_K
    flat_e = top_idx.reshape(-1)
    onehot = (flat_e[:, None] == jnp.arange(N_EXPERTS, dtype=jnp.int32)[None, :]).astype(jnp.int32)
    csum = jnp.cumsum(onehot, axis=0)
    rank = jnp.sum(onehot * csum, axis=1) - 1
    counts = csum[-1]
    padded = (counts + rb - 1) // rb * rb
    pend = jnp.cumsum(padded)
    pstart = pend - padded
    dest = (pstart[flat_e] + rank).astype(jnp.int32).reshape(n_tokens, TOP_K)
    n_blocks = tk // rb + N_EXPERTS
    first_row = jnp.arange(n_blocks, dtype=jnp.int32) * rb
    block_e = jnp.minimum(jnp.sum(pend[None, :] <= first_row[:, None], axis=1),
                          N_EXPERTS - 1).astype(jnp.int32)
    n_used = (pend[-1] // rb).astype(jnp.int32).reshape(1)
    return dest, block_e, n_used


def _dispatch(x1s, dest, n_rows, sub):
    T = dest.shape[0]
    idx = jnp.concatenate([dest.T, jnp.zeros((SC_IDX_ROWS - TOP_K, T), jnp.int32)], axis=0)
    xs = _sc_scatter_slabs(x1s.reshape(T, sub, LANES), idx, TOP_K, n_rows)
    return xs.reshape(n_rows * sub, LANES)


def _expert_kernel(be_ref, nu_ref, x_ref, wg_ref, bg_ref, wu_ref, bu_ref, wd_ref, bd_ref,
                   y_ref, wgb, wub, wdb):
    i = pl.program_id(0)
    prev = be_ref[jnp.maximum(i - 1, 0)]

    @pl.when((i == 0) | (be_ref[i] != prev))
    def _():
        wgb[...] = wg_ref[0, 0].astype(BF16)
        wub[...] = wu_ref[0, 0].astype(BF16)
        wdb[...] = wd_ref[0, 0].astype(BF16)

    @pl.when(i < nu_ref[0])
    def _():
        sub = wgb.shape[0] // LANES
        xb = _load_subrows(x_ref, x_ref.shape[0] // sub, sub).astype(BF16)
        gate = jnp.dot(xb, wgb[...], preferred_element_type=F32) + bg_ref[0, 0]
        gate = jnp.minimum(gate, SWIGLU_LIMIT)
        up = jnp.dot(xb, wub[...], preferred_element_type=F32) + bu_ref[0, 0]
        up = jnp.clip(up, -SWIGLU_LIMIT, SWIGLU_LIMIT)
        hid = gate * _sigmoid(SWIGLU_ALPHA * gate) * (up + 1.0)
        y = jnp.dot(hid.astype(BF16), wdb[...], preferred_element_type=F32) + bd_ref[0, 0]
        _store_subrows(y_ref, y)


def _experts(xs, block_e, n_used, layer, w_gate, b_gate, w_up, b_up, w_down, b_down):
    rb = MOE_ROWS
    n_blocks = block_e.shape[0]
    E, D, F = w_gate.shape[1:]
    sub = D // LANES
    wspec = lambda a, b: pl.BlockSpec((1, 1, a, b), lambda i, be, nu: (layer, be[i], 0, 0))
    live = lambda i, be, nu: (jnp.where(i < nu[0], i, n_blocks), 0)
    grid_spec = pltpu.PrefetchScalarGridSpec(
        num_scalar_prefetch=2,
        grid=(n_blocks,),
        in_specs=[pl.BlockSpec((rb * sub, LANES), live),
                  wspec(D, F), wspec(1, F), wspec(D, F), wspec(1, F), wspec(F, D), wspec(1, D)],
        out_specs=pl.BlockSpec((rb * sub, LANES), live),
        scratch_shapes=[pltpu.VMEM((D, F), BF16), pltpu.VMEM((D, F), BF16),
                        pltpu.VMEM((F, D), BF16)],
    )
    depth = w_gate.shape[0]
    return pl.pallas_call(
        _expert_kernel,
        grid_spec=grid_spec,
        out_shape=jax.ShapeDtypeStruct(xs.shape, F32),
        compiler_params=_params("arbitrary"),
        name="experts",
    )(block_e, n_used, xs, w_gate, b_gate.reshape(depth, E, 1, F),
      w_up, b_up.reshape(depth, E, 1, F), w_down, b_down.reshape(depth, E, 1, D))


def _combine_kernel(x1_ref, gate_ref, y_ref, g_ref, b_ref, o_ref, *, alpha):
    tm, D = x1_ref.shape
    sub = D // LANES
    gates = gate_ref[...]
    f = jnp.zeros(x1_ref.shape, F32)
    for kk in range(TOP_K):
        f = f + gates[:, kk:kk + 1] * _load_subrows(y_ref, tm, sub, kk * sub, TOP_K * sub)
    o_ref[...] = _layer_norm(alpha * x1_ref[...] + f, g_ref[...], b_ref[...])


def _combine(x1, gates, dest, y_rows, ln_g, ln_b, alpha):
    T, D = x1.shape
    sub = D // LANES
    tm = COMBINE_ROWS
    yg = _sc_gather_slabs(y_rows.reshape(-1, sub, LANES), dest.reshape(-1)).reshape(-1, LANES)
    row = lambda n: pl.BlockSpec((tm, n), lambda i: (i, 0))
    full = lambda a, b: pl.BlockSpec((a, b), lambda i: (0, 0))
    return pl.pallas_call(
        functools.partial(_combine_kernel, alpha=alpha),
        grid=(T // tm,),
        in_specs=[row(D), row(LANES), pl.BlockSpec((tm * TOP_K * sub, LANES), lambda i: (i, 0)),
                  full(1, D), full(1, D)],
        out_specs=row(D),
        out_shape=jax.ShapeDtypeStruct((T, D), F32),
        compiler_params=_params("parallel"),
        name="combine",
    )(x1, gates, yg, ln_g.reshape(1, D), ln_b.reshape(1, D))


def kernel(x, mem, w_in_hgrn, hgrn_lb_logits, hgrn_norm_g, w_in_moba, w_mem_kv, w_o,
           ln_mix_g, ln_mix_b, w_router, b_router, w_gate, b_gate, w_up, b_up,
           w_down, b_down, ln_ffn_g, ln_ffn_b):
    B, S, D = x.shape
    T = B * S
    depth = w_o.shape[0]
    alpha = (2 * depth) ** 0.25

    p_lb = jax.nn.softmax(hgrn_lb_logits.astype(F32), axis=0)
    lower_bounds = jnp.cumsum(p_lb, axis=0) - p_lb[0]

    x2 = x.reshape(T, D)
    for layer in range(depth):
        j = layer // 2
        if layer % 2 == 0:
            (h2,) = _inproj(x2, w_in_hgrn[j].astype(BF16), for_moba=False)
            mix = _hgrn_mixer(h2.reshape(B, S, -1), lower_bounds[j], hgrn_norm_g[j])
        else:
            h2, hb2, q6 = _inproj(x2, w_in_moba[j].astype(BF16), for_moba=True)
            mix = _moba_mixer_v2(h2.reshape(B, S, -1), hb2.reshape(B, S, -1), q6)
        kv = _memkv(mem, w_mem_kv[layer])
        x1, x1s, top_idx, gates = _post_mixer(
            x2, mix.reshape(T, MIX_WIDTH), h2, kv, w_o[layer].astype(BF16),
            ln_mix_g[layer], ln_mix_b[layer], w_router[layer], b_router[layer], S, alpha)
        dest, block_e, n_used = _route(top_idx[:, :TOP_K], T)
        xs = _dispatch(x1s, dest, (block_e.shape[0] + 1) * MOE_ROWS, D // LANES)
        y_rows = _experts(xs, block_e, n_used, layer, w_gate, b_gate, w_up, b_up, w_down, b_down)
        x2 = _combine(x1, gates, dest, y_rows, ln_ffn_g[layer], ln_ffn_b[layer], alpha)
    return x2.reshape(B, S, D)
```

```python
import functools
import math

import jax
import jax.numpy as jnp
from jax import lax
from jax.experimental import pallas as pl
from jax.experimental.pallas import tpu as pltpu
from jax.experimental.pallas import tpu_sc as plsc

MIX_WIDTH = 768
MEM_HEADS = 4
MEM_HEAD_DIM = 64
MEM_WIDTH = MEM_HEADS * MEM_HEAD_DIM
HGRN_HEADS = 6
HGRN_DK = 128
MOBA_HEADS = 12
MOBA_HEAD_DIM = 64
MOBA_BLOCK = 256
MOBA_TOPK = 3
N_EXPERTS = 32
TOP_K = 4
SWIGLU_ALPHA = 1.702
SWIGLU_LIMIT = 7.0
LN_EPS = 1e-5
RMS_EPS = 1e-6

LANES = 128
SUBLANES = 8
VMEM_LIMIT_BYTES = 56 * 1024 * 1024

INPROJ_ROWS = 512
HGRN_CHUNK = 64
HGRN_ROWS = 512
POST_ROWS = 256
MOBA_TILE = 256
MOBA_TILE_GROUP = 8
MOBA_PLACE_GROUP = 8
MOE_ROWS = 512
COMBINE_ROWS = 256
SC_WINDOW = 128
SC_IDX_ROWS = 8
SC_CHUNK = 32

BF16 = jnp.bfloat16
F32 = jnp.float32
NEG_BIG = -1e30

_NT = (((1,), (1,)), ((), ()))
_TN = (((0,), (0,)), ((), ()))


def _alibi_slope_list(n):
    def pow2(m):
        start = 2.0 ** (-(2.0 ** -(math.log2(m) - 3)))
        return [start ** (i + 1) for i in range(m)]
    if math.log2(n).is_integer():
        return pow2(n)
    c = 2 ** math.floor(math.log2(n))
    return pow2(c) + _alibi_slope_list(2 * c)[0::2][:n - c]


def _sigmoid(x):
    return 1.0 / (1.0 + jnp.exp(-x))


def _params(*sem):
    return pltpu.CompilerParams(dimension_semantics=sem, vmem_limit_bytes=VMEM_LIMIT_BYTES)


def _store_subrows(ref, value, first=0, stride=None):
    sub = value.shape[1] // LANES
    stride = stride or sub
    for c in range(sub):
        ref[pl.ds(first + c, value.shape[0], stride=stride), :] = value[:, c * LANES:(c + 1) * LANES]


def _load_subrows(ref, rows, sub, first=0, stride=None):
    stride = stride or sub
    return jnp.concatenate(
        [ref[pl.ds(first + c, rows, stride=stride), :] for c in range(sub)], axis=1)


def _inproj_kernel(x_ref, w_ref, h_ref, *moba_refs):
    h = jnp.dot(x_ref[...].astype(BF16), w_ref[...], preferred_element_type=F32)
    h_ref[...] = h
    if moba_refs:
        kv_ref, q6_ref = moba_refs
        n_pairs = q6_ref.shape[0]
        for p in range(n_pairs):
            q6_ref[p] = h[:, p * LANES:(p + 1) * LANES]
            for part in range(2):
                col = (1 + part) * MIX_WIDTH + p * LANES
                kv_ref[:, (2 * p + part) * LANES:(2 * p + part + 1) * LANES] = (
                    h[:, col:col + LANES].astype(BF16))


def _inproj(x2, w_bf16, for_moba):
    T, D = x2.shape
    N = w_bf16.shape[1]
    tm = INPROJ_ROWS
    out_shape = [jax.ShapeDtypeStruct((T, N), F32)]
    out_specs = [pl.BlockSpec((tm, N), lambda i: (i, 0))]
    if for_moba:
        NP = MOBA_HEADS // 2
        out_shape += [jax.ShapeDtypeStruct((T, 2 * MIX_WIDTH), BF16),
                      jax.ShapeDtypeStruct((NP, T, LANES), F32)]
        out_specs += [pl.BlockSpec((tm, 2 * MIX_WIDTH), lambda i: (i, 0)),
                      pl.BlockSpec((NP, tm, LANES), lambda i: (0, i, 0))]
    return pl.pallas_call(
        _inproj_kernel,
        grid=(T // tm,),
        in_specs=[pl.BlockSpec((tm, D), lambda i: (i, 0)),
                  pl.BlockSpec((D, N), lambda i: (0, 0))],
        out_specs=out_specs,
        out_shape=out_shape,
        compiler_params=_params("parallel"),
        name="inproj",
    )(x2, w_bf16)


def _cumsum_rows(x, row):
    n = x.shape[0]
    sh = 1
    while sh < n:
        x = x + jnp.where(row >= sh, pltpu.roll(x, sh, 0), 0.0)
        sh *= 2
    return x


def _bcast_row(a, group, r):
    n = a.shape[0]
    a3 = a.reshape(n // group, group, LANES)
    return jnp.broadcast_to(a3[:, r:r + 1, :], a3.shape).reshape(n, LANES)


def _hgrn_chunk(qr, fr, v, gr, lb, ng, e_sum, st_t):
    C = qr.shape[0]
    row = lax.broadcasted_iota(jnp.int32, (C, LANES), 0)
    rr = lax.broadcasted_iota(jnp.int32, (C, C), 0)
    cc = lax.broadcasted_iota(jnp.int32, (C, C), 1)

    q = qr * _sigmoid(qr)
    forget = lb + (1.0 - lb) * _sigmoid(fr)
    k = 1.0 - forget
    G = _cumsum_rows(jnp.log(forget), row)

    parts = []
    for s in range(SUBLANES):
        Gs = _bcast_row(G, SUBLANES, s)
        ks = _bcast_row(k, SUBLANES, s)
        parts.append((q * ks * jnp.exp(jnp.minimum(G - Gs, 0.0))).astype(BF16))
    a_diag = jnp.dot(jnp.concatenate(parts, axis=1), e_sum, preferred_element_type=F32)
    A = jnp.where(((rr >> 3) == (cc >> 3)) & (cc <= rr), a_diag, 0.0)

    m = SUBLANES
    while m < C:
        lg = int(math.log2(m))
        Gr = _bcast_row(G, 2 * m, m - 1)
        second = ((row >> lg) & 1) == 1
        qm = jnp.where(second, q * jnp.exp(jnp.minimum(G - Gr, 0.0)), 0.0)
        km = jnp.where(second, 0.0, k * jnp.exp(jnp.minimum(Gr - G, 0.0)))
        am = lax.dot_general(qm.astype(BF16), km.astype(BF16), _NT, preferred_element_type=F32)
        A = A + jnp.where((rr >> (lg + 1)) == (cc >> (lg + 1)), am, 0.0)
        m *= 2

    vb = v.astype(BF16)
    o = jnp.dot(A.astype(BF16), vb, preferred_element_type=F32)
    o = o + lax.dot_general((q * jnp.exp(G)).astype(BF16), st_t.astype(BF16), _NT,
                            preferred_element_type=F32)
    g_end = G[C - 1:C, :]
    kd = (k * jnp.exp(g_end - G)).astype(BF16)
    st_new = st_t * jnp.exp(g_end) + lax.dot_general(vb, kd, _TN, preferred_element_type=F32)

    ms = jnp.mean(o * o, axis=-1, keepdims=True)
    out = o * lax.rsqrt(ms + RMS_EPS) * ng * _sigmoid(gr)
    return out, st_new


def _hgrn_kernel(q_ref, f_ref, i_ref, g_ref, lb_ref, ng_ref, e_ref, o_ref, st_ref, *, chunk):
    @pl.when(pl.program_id(2) == 0)
    def _():
        st_ref[...] = jnp.zeros_like(st_ref)

    lb = lb_ref[0]
    ng = ng_ref[...]
    e_sum = e_ref[...]
    n_chunks = q_ref.shape[1] // chunk
    for c in range(n_chunks):
        sl = pl.ds(c * chunk, chunk)
        out, st_new = _hgrn_chunk(q_ref[0, sl, :], f_ref[0, sl, :], i_ref[0, sl, :],
                                  g_ref[0, sl, :], lb, ng, e_sum, st_ref[...])
        st_ref[...] = st_new
        o_ref[0, sl, :] = out


def _hgrn_mixer(h3, lb, norm_g):
    B, S, _ = h3.shape
    ts = min(HGRN_ROWS, S)
    C = HGRN_CHUNK
    H = HGRN_HEADS
    e_sum = (jnp.arange(SUBLANES * LANES)[:, None] // LANES == jnp.arange(C)[None, :] % SUBLANES
             ).astype(BF16)
    col = lambda off: pl.BlockSpec((1, ts, LANES), lambda b, h, s, off=off: (b, s, off + h))
    return pl.pallas_call(
        functools.partial(_hgrn_kernel, chunk=C),
        grid=(B, H, S // ts),
        in_specs=[col(0), col(H), col(2 * H), col(3 * H),
                  pl.BlockSpec((1, 1, LANES), lambda b, h, s: (h, 0, 0)),
                  pl.BlockSpec((1, LANES), lambda b, h, s: (0, 0)),
                  pl.BlockSpec((SUBLANES * LANES, C), lambda b, h, s: (0, 0))],
        out_specs=pl.BlockSpec((1, ts, LANES), lambda b, h, s: (b, s, h)),
        out_shape=jax.ShapeDtypeStruct((B, S, MIX_WIDTH), F32),
        scratch_shapes=[pltpu.VMEM((HGRN_DK, HGRN_DK), F32)],
        compiler_params=_params("parallel", "parallel", "arbitrary"),
        name="hgrn",
    )(h3, h3, h3, h3, lb.reshape(H, 1, LANES), norm_g.reshape(1, LANES), e_sum)


def _moba_kernel(sl_ref, q_ref, kf_ref, kb_ref, vb_ref, o_ref, kmean_ref):
    BLK = MOBA_BLOCK
    p = pl.program_id(1)
    i = pl.program_id(2)

    @pl.when(i == 0)
    def _():
        kmean_ref[...] = jnp.zeros_like(kmean_ref)

    qf = q_ref[0]
    own = pl.ds(pl.multiple_of(i * BLK, BLK), BLK)
    k_own = kb_ref[0, own, :]
    v_own = vb_ref[0, own, :]

    lane = lax.broadcasted_iota(jnp.int32, (BLK, LANES), 1)
    rr = lax.broadcasted_iota(jnp.int32, (BLK, BLK), 0)
    cc = lax.broadcasted_iota(jnp.int32, (BLK, BLK), 1)
    rel = (rr - cc).astype(F32)
    scale = MOBA_HEAD_DIM ** -0.5

    outs = []
    for hh in range(2):
        head = (lane < MOBA_HEAD_DIM) if hh == 0 else (lane >= MOBA_HEAD_DIM)
        slope = sl_ref[2 * p + hh]
        qh = jnp.where(head, qf, 0.0)

        gate = lax.dot_general(qh, kmean_ref[...], _NT, precision=lax.Precision.HIGHEST,
                               preferred_element_type=F32)
        past = lane < i
        g = jnp.where(past, gate, -jnp.inf)
        sel = jnp.zeros((BLK, LANES), jnp.bool_)
        for _ in range(MOBA_TOPK):
            mx = jnp.max(g, axis=1, keepdims=True)
            idx = jnp.min(jnp.where(g == mx, lane, LANES), axis=1, keepdims=True)
            pick = lane == idx
            sel = sel | pick
            g = jnp.where(pick, -jnp.inf, g)
        sel_bias = jnp.where(sel & past, 0.0, NEG_BIG)
        q_aug = jnp.concatenate([(qh * scale).astype(BF16), sel_bias.astype(BF16)], axis=1)

        bias = -slope * rel
        s = lax.dot_general((qh * scale).astype(BF16), k_own, _NT, preferred_element_type=F32)
        s = s + jnp.where(cc <= rr, bias, -jnp.inf)
        m0 = jnp.max(s, axis=1, keepdims=True)
        p0 = jnp.exp(s - m0)
        l0 = jnp.sum(p0, axis=1, keepdims=True)
        acc0 = jnp.dot(p0.astype(BF16), v_own, preferred_element_type=F32)

        def body(n, carry):
            m, l, acc = carry
            blk = pl.ds(pl.multiple_of(n * BLK, BLK), BLK)
            onehot = jnp.where(lane == n, 1.0, 0.0).astype(BF16)
            k_aug = jnp.concatenate([kb_ref[0, blk, :], onehot], axis=1)
            sn = lax.dot_general(q_aug, k_aug, _NT, preferred_element_type=F32)
            sn = sn + (bias - slope * ((i - n) * BLK).astype(F32))
            m_new = jnp.maximum(m, jnp.max(sn, axis=1, keepdims=True))
            a = jnp.exp(m - m_new)
            pn = jnp.exp(sn - m_new)
            l = a * l + jnp.sum(pn, axis=1, keepdims=True)
            acc = a * acc + jnp.dot(pn.astype(BF16), vb_ref[0, blk, :],
                                    preferred_element_type=F32)
            return m_new, l, acc

        _, l, acc = lax.fori_loop(0, i, body, (m0, l0, acc0))
        outs.append(acc / l)

    o_ref[0] = jnp.where(lane < MOBA_HEAD_DIM, outs[0], outs[1])
    kmean_ref[pl.ds(i, 1), :] = jnp.mean(kf_ref[0], axis=0, keepdims=True)


def _moba_mixer(h3, hb3):
    B, S, _ = h3.shape
    BLK = MOBA_BLOCK
    NP = MOBA_HEADS // 2
    slopes = jnp.asarray(_alibi_slope_list(MOBA_HEADS), F32)
    grid_spec = pltpu.PrefetchScalarGridSpec(
        num_scalar_prefetch=1,
        grid=(B, NP, S // BLK),
        in_specs=[pl.BlockSpec((1, BLK, LANES), lambda b, p, i, sl: (b, i, p)),
                  pl.BlockSpec((1, BLK, LANES), lambda b, p, i, sl: (b, i, NP + p)),
                  pl.BlockSpec((1, S, LANES), lambda b, p, i, sl: (b, 0, NP + p)),
                  pl.BlockSpec((1, S, LANES), lambda b, p, i, sl: (b, 0, 2 * NP + p))],
        out_specs=pl.BlockSpec((1, BLK, LANES), lambda b, p, i, sl: (b, i, p)),
        scratch_shapes=[pltpu.VMEM((LANES, LANES), F32)],
    )
    return pl.pallas_call(
        _moba_kernel,
        grid_spec=grid_spec,
        out_shape=jax.ShapeDtypeStruct((B, S, MIX_WIDTH), F32),
        compiler_params=_params("parallel", "parallel", "arbitrary"),
        name="moba",
    )(slopes, h3, h3, hb3, hb3)


def _moba_top3(qh, kmean, i, lane):
    gate = lax.dot_general(qh, kmean, _NT, precision=lax.Precision.HIGHEST,
                           preferred_element_type=F32)
    g = jnp.where(lane < i, gate, -jnp.inf)
    picks = []
    for _ in range(MOBA_TOPK):
        mx = jnp.max(g, axis=1, keepdims=True)
        idx = jnp.min(jnp.where(g == mx, lane, LANES), axis=1, keepdims=True)
        picks.append((idx, (mx > -jnp.inf) & (idx < i)))
        g = jnp.where(lane == idx, -jnp.inf, g)
    return picks


def _moba_select_kernel(q_ref, kf_ref, sr_ref, cnt_ref, kmean_ref):
    BLK = MOBA_BLOCK
    i = pl.program_id(2)

    @pl.when(i == 0)
    def _():
        kmean_ref[...] = jnp.zeros_like(kmean_ref)

    qf = q_ref[0]
    lane = lax.broadcasted_iota(jnp.int32, (BLK, LANES), 1)
    rr = lax.broadcasted_iota(jnp.int32, (BLK, BLK), 0)
    cc = lax.broadcasted_iota(jnp.int32, (BLK, BLK), 1)
    before = jnp.where(cc < rr, 1.0, 0.0).astype(BF16)
    out = jnp.zeros((BLK, LANES), jnp.int32)
    counts = []
    for hh in range(2):
        head = (lane >> 6) == hh
        picks = _moba_top3(jnp.where(head, qf, 0.0), kmean_ref[...], i, lane)
        chosen = jnp.zeros((BLK, LANES), F32)
        for idx, valid in picks:
            chosen = chosen + jnp.where((lane == idx) & valid, 1.0, 0.0)
        earlier = jnp.dot(before, chosen.astype(BF16), preferred_element_type=F32)
        for j, (idx, valid) in enumerate(picks):
            rep = hh * MOBA_TOPK + j
            rank = jnp.sum(jnp.where(lane == idx, earlier, 0.0), axis=1, keepdims=True)
            out = jnp.where(lane == rep, jnp.where(valid, idx, -1), out)
            out = jnp.where(lane == SUBLANES + rep, rank.astype(jnp.int32), out)
        total = earlier[BLK - 1:BLK, :] + chosen[BLK - 1:BLK, :]
        counts.append(jnp.broadcast_to(total, (SUBLANES, LANES)))
    sr_ref[0, 0] = out
    cnt_ref[0, 0] = jnp.concatenate(counts, axis=0)
    kmean_ref[pl.ds(i, 1), :] = jnp.mean(kf_ref[0], axis=0, keepdims=True)


def _moba_dest_kernel(sr_ref, tab_ref, idx_ref, *, trash):
    BLK = MOBA_BLOCK
    blk = sr_ref[0, 0]
    tab = tab_ref[0, 0]
    lane = lax.broadcasted_iota(jnp.int32, (BLK, LANES), 1)
    d = jnp.full((BLK, LANES), float(trash), F32)
    for rep in range(2 * MOBA_TOPK):
        hh = rep // MOBA_TOPK
        sel = blk[:, rep:rep + 1]
        rank = blk[:, SUBLANES + rep:SUBLANES + rep + 1]
        start = jnp.sum(jnp.where(lane == sel, tab[hh:hh + 1, :], 0.0), axis=1, keepdims=True)
        dest = jnp.where(sel >= 0, start + rank.astype(F32), float(trash))
        d = jnp.where(lane == rep, dest, d)
    idx_ref[...] = d.T[:SC_IDX_ROWS, :].astype(jnp.int32)


def _moba_tile_kernel(tb_ref, tp_ref, th_ref, tn_ref, nu_ref, sl_ref, q_ref, k_ref, v_ref, o_ref):
    t = pl.program_id(0)

    @pl.when(t < nu_ref[0])
    def _():
        tq = q_ref.shape[0]
        hh = th_ref[t]
        slope = sl_ref[2 * tp_ref[t] + hh]
        lane = lax.broadcasted_iota(jnp.int32, (tq, LANES), 1)
        head = (lane >> 6) == hh
        q = jnp.where(head, q_ref[...] * (MOBA_HEAD_DIM ** -0.5), 0.0).astype(BF16)
        s = lax.dot_general(q, k_ref[0], _NT, preferred_element_type=F32)
        kpos = lax.broadcasted_iota(jnp.int32, s.shape, 1).astype(F32)
        s = s + slope * kpos
        m = jnp.max(s, axis=1, keepdims=True)
        p = jnp.exp(s - m)
        l = jnp.sum(p, axis=1, keepdims=True)
        o = jnp.dot(p.astype(BF16), v_ref[0], preferred_element_type=F32)
        half = lane & (MOBA_HEAD_DIM - 1)
        stats = jnp.where(half == 0, m, jnp.where(half == 1, l, 0.0))
        o_ref[...] = jnp.where(head, o, stats)


def _moba_merge_kernel(sl_ref, q_ref, k_ref, v_ref, sr_ref, pg_ref, o_ref):
    BLK = MOBA_BLOCK
    p = pl.program_id(1)
    i = pl.program_id(2)
    qf = q_ref[0]
    k_own = k_ref[0]
    v_own = v_ref[0]
    blk = sr_ref[0, 0]
    lane = lax.broadcasted_iota(jnp.int32, (BLK, LANES), 1)
    rr = lax.broadcasted_iota(jnp.int32, (BLK, BLK), 0)
    cc = lax.broadcasted_iota(jnp.int32, (BLK, BLK), 1)
    rel = (rr - cc).astype(F32)
    row = lax.broadcasted_iota(jnp.int32, (BLK, 1), 0)
    outs = []
    for hh in range(2):
        head = (lane >> 6) == hh
        slope = sl_ref[2 * p + hh]
        qh = jnp.where(head, qf * (MOBA_HEAD_DIM ** -0.5), 0.0).astype(BF16)
        s = lax.dot_general(qh, k_own, _NT, preferred_element_type=F32)
        s = s + jnp.where(cc <= rr, -slope * rel, -jnp.inf)
        m = jnp.max(s, axis=1, keepdims=True)
        pr = jnp.exp(s - m)
        ms = [m]
        ls = [jnp.sum(pr, axis=1, keepdims=True)]
        accs = [jnp.dot(pr.astype(BF16), v_own, preferred_element_type=F32)]
        stat_lane = MOBA_HEAD_DIM * (1 - hh)
        for j in range(MOBA_TOPK):
            rep = hh * MOBA_TOPK + j
            part = pg_ref[rep, 0]
            sel = blk[:, rep:rep + 1]
            valid = sel >= 0
            shift = -slope * ((i - sel) * BLK + row).astype(F32)
            ms.append(jnp.where(valid, part[:, stat_lane:stat_lane + 1] + shift, -jnp.inf))
            ls.append(jnp.where(valid, part[:, stat_lane + 1:stat_lane + 2], 0.0))
            accs.append(jnp.where(valid, part, 0.0))
        top = functools.reduce(jnp.maximum, ms)
        ws = [jnp.exp(mk - top) for mk in ms]
        den = sum(w * lk for w, lk in zip(ws, ls))
        num = sum(w * ak for w, ak in zip(ws, accs))
        outs.append(num / den)
    o_ref[0] = jnp.where(lane < MOBA_HEAD_DIM, outs[0], outs[1])


def _moba_sparse_mixer(h3, hb3, q6):
    B, S, _ = h3.shape
    T = B * S
    BLK = MOBA_BLOCK
    TQ = MOBA_TILE
    NB = S // BLK
    NP = MOBA_HEADS // 2
    n_rep = 2 * MOBA_TOPK
    slopes = jnp.asarray(_alibi_slope_list(MOBA_HEADS), F32)
    step = lambda *rest: pl.BlockSpec((1, 1) + rest, lambda b, p, i: (b, p, i) + (0,) * (len(rest) - 1))

    selrank, counts = pl.pallas_call(
        _moba_select_kernel,
        grid=(B, NP, NB),
        in_specs=[pl.BlockSpec((1, BLK, LANES), lambda b, p, i: (b, i, p)),
                  pl.BlockSpec((1, BLK, LANES), lambda b, p, i: (b, i, NP + p))],
        out_specs=[step(BLK, LANES), step(2 * SUBLANES, LANES)],
        out_shape=[jax.ShapeDtypeStruct((B, NP, S, LANES), jnp.int32),
                   jax.ShapeDtypeStruct((B, NP, NB * 2 * SUBLANES, LANES), F32)],
        scratch_shapes=[pltpu.VMEM((LANES, LANES), F32)],
        compiler_params=_params("parallel", "parallel", "arbitrary"),
        name="moba_select",
    )(h3, h3)

    cnt = counts.reshape(B, NP, NB, 2, SUBLANES, LANES)[:, :, :, :, 0, :NB].astype(jnp.int32)
    base = jnp.cumsum(cnt, axis=2) - cnt
    total = jnp.sum(cnt, axis=2)
    padded = (total + TQ - 1) // TQ * TQ
    pend = jnp.cumsum(padded.reshape(-1))
    seg_start = (pend - padded.reshape(-1)).reshape(B, NP, 1, 2, NB)
    table = jnp.zeros((B, NP, NB, SUBLANES, LANES), F32).at[:, :, :, :2, :NB].set(
        (seg_start + base).astype(F32)).reshape(B, NP, NB * SUBLANES, LANES)
    n_seg = B * NP * 2 * NB
    max_tiles = (T * NP * n_rep) // TQ + n_seg
    tile_seg = jnp.minimum(
        jnp.searchsorted(pend, jnp.arange(max_tiles, dtype=jnp.int32) * TQ, side='right'),
        n_seg - 1).astype(jnp.int32)
    tile_n = tile_seg % NB
    tile_h = (tile_seg // NB) % 2
    tile_p = (tile_seg // (2 * NB)) % NP
    tile_b = tile_seg // (2 * NB * NP)
    n_used = (pend[-1] // TQ).astype(jnp.int32).reshape(1)
    trash = max_tiles * TQ

    idx = pl.pallas_call(
        functools.partial(_moba_dest_kernel, trash=trash),
        grid=(B, NP, NB),
        in_specs=[step(BLK, LANES), step(SUBLANES, LANES)],
        out_specs=pl.BlockSpec((SC_IDX_ROWS, BLK), lambda b, p, i: (0, p * (T // BLK) + b * NB + i)),
        out_shape=jax.ShapeDtypeStruct((SC_IDX_ROWS, NP * T), jnp.int32),
        compiler_params=_params("parallel", "parallel", "parallel"),
        name="moba_dest",
    )(selrank, table)

    n_rows = (max_tiles + 1) * TQ
    qs = _sc_scatter_rows(q6.reshape(NP * T, LANES), idx, n_rep, n_rows)

    live = lambda t, tb, tp, th, tn, nu, sl: (jnp.where(t < nu[0], t, max_tiles), 0)
    kv_blk = lambda off: pl.BlockSpec(
        (1, BLK, LANES), lambda t, tb, tp, th, tn, nu, sl, off=off: (tb[t], tn[t], off + tp[t]))
    part = pl.pallas_call(
        _moba_tile_kernel,
        grid_spec=pltpu.PrefetchScalarGridSpec(
            num_scalar_prefetch=6,
            grid=(max_tiles,),
            in_specs=[pl.BlockSpec((TQ, LANES), live), kv_blk(NP), kv_blk(2 * NP)],
            out_specs=pl.BlockSpec((TQ, LANES), live)),
        out_shape=jax.ShapeDtypeStruct((n_rows, LANES), F32),
        compiler_params=_params("arbitrary"),
        name="moba_tiles",
    )(tile_b, tile_p, tile_h, tile_n, n_used, slopes, qs, hb3, hb3)

    pg = _sc_gather_rows(part, idx[:n_rep].reshape(-1)).reshape(n_rep, NP, T, LANES)

    own = lambda off: pl.BlockSpec((1, BLK, LANES), lambda b, p, i, sl, off=off: (b, i, off + p))
    return pl.pallas_call(
        _moba_merge_kernel,
        grid_spec=pltpu.PrefetchScalarGridSpec(
            num_scalar_prefetch=1,
            grid=(B, NP, NB),
            in_specs=[own(0), own(NP), own(2 * NP),
                      pl.BlockSpec((1, 1, BLK, LANES), lambda b, p, i, sl: (b, p, i, 0)),
                      pl.BlockSpec((n_rep, 1, BLK, LANES), lambda b, p, i, sl: (0, p, b * NB + i, 0))],
            out_specs=pl.BlockSpec((1, BLK, LANES), lambda b, p, i, sl: (b, i, p))),
        out_shape=jax.ShapeDtypeStruct((B, S, MIX_WIDTH), F32),
        compiler_params=_params("parallel", "parallel", "parallel"),
        name="moba_merge",
    )(slopes, h3, hb3, hb3, selrank, pg)


def _moba_pick_kernel(q_ref, kf_ref, sr_ref, cnt_ref, kmean_ref):
    BLK = MOBA_BLOCK
    nbp = kmean_ref.shape[0]
    i = pl.program_id(2)

    @pl.when(i == 0)
    def _():
        kmean_ref[...] = jnp.zeros_like(kmean_ref)

    qf = q_ref[0]
    lane = lax.broadcasted_iota(jnp.int32, (BLK, LANES), 1)
    nrow = lax.broadcasted_iota(jnp.int32, (nbp, BLK), 0)
    orow = lax.broadcasted_iota(jnp.int32, (2 * SUBLANES, BLK), 0)
    lane_c = lax.broadcasted_iota(jnp.int32, (nbp, LANES), 1)
    qa = lax.broadcasted_iota(jnp.int32, (BLK, BLK), 0)
    qb = lax.broadcasted_iota(jnp.int32, (BLK, BLK), 1)
    before = jnp.where(qa < qb, 1.0, 0.0).astype(BF16)
    out = jnp.zeros((2 * SUBLANES, BLK), jnp.int32)
    cnt = jnp.zeros((nbp, LANES), F32)
    for hh in range(2):
        qh = jnp.where((lane >> 6) == hh, qf, 0.0)
        gate = lax.dot_general(kmean_ref[...], qh, _NT, precision=lax.Precision.HIGHEST,
                               preferred_element_type=F32)
        g = jnp.where(nrow < i, gate, -jnp.inf)
        picks = []
        for _ in range(MOBA_TOPK):
            mx = jnp.max(g, axis=0, keepdims=True)
            idx = jnp.min(jnp.where(g == mx, nrow, nbp), axis=0, keepdims=True)
            picks.append((idx, (mx > -jnp.inf) & (idx < i)))
            g = jnp.where(nrow == idx, -jnp.inf, g)
        chosen = jnp.zeros((nbp, BLK), F32)
        for idx, valid in picks:
            chosen = chosen + jnp.where((nrow == idx) & valid, 1.0, 0.0)
        earlier = jnp.dot(chosen.astype(BF16), before, preferred_element_type=F32)
        for j, (idx, valid) in enumerate(picks):
            rep = hh * MOBA_TOPK + j
            rank = jnp.sum(jnp.where(nrow == idx, earlier, 0.0), axis=0, keepdims=True)
            out = jnp.where(orow == rep, jnp.where(valid, idx, -1), out)
            out = jnp.where(orow == SUBLANES + rep, rank.astype(jnp.int32), out)
        cnt = jnp.where(lane_c == hh, jnp.sum(chosen, axis=1, keepdims=True), cnt)
    sr_ref[0, 0] = out
    cnt_ref[0, 0] = cnt
    kmean_ref[pl.ds(i, 1), :] = jnp.mean(kf_ref[0], axis=0, keepdims=True)


def _moba_place_kernel(sr_ref, tab_ref, idx_ref, *, group, spare):
    BLK = MOBA_BLOCK
    n_rep = 2 * MOBA_TOPK
    b = pl.program_id(0)
    p = pl.program_id(1)
    ig = pl.program_id(2)
    nrow = lax.broadcasted_iota(jnp.int32, (LANES, BLK), 0)
    orow = lax.broadcasted_iota(jnp.int32, (SC_IDX_ROWS, BLK), 0)
    qpos = lax.broadcasted_iota(jnp.int32, (1, BLK), 1)
    for g in range(group):
        blk = sr_ref[0, 0, :, g * BLK:(g + 1) * BLK]
        tab_t = tab_ref[0, 0, g * SUBLANES:(g + 1) * SUBLANES, :].T
        early = jnp.minimum(ig * group + g, MOBA_TOPK - 1)
        out = jnp.zeros((SC_IDX_ROWS, BLK), jnp.int32)
        for rep in range(n_rep):
            hh = rep // MOBA_TOPK
            sel = blk[rep:rep + 1, :]
            rank = blk[SUBLANES + rep:SUBLANES + rep + 1, :]
            start = jnp.sum(jnp.where(nrow == sel, tab_t[:, hh:hh + 1], 0.0), axis=0, keepdims=True)
            unused = spare + (((b * pl.num_programs(1) + p) * MOBA_TOPK + early) * n_rep + rep) * BLK
            dest = jnp.where(sel >= 0, start.astype(jnp.int32) + rank, unused + qpos)
            out = jnp.where(orow == rep, dest, out)
        idx_ref[:, g * BLK:(g + 1) * BLK] = out


def _moba_group_kernel(tb_ref, tp_ref, th_ref, tn_ref, nu_ref, sl_ref, q_ref, *refs, group):
    kv_refs, o_ref = refs[:group], refs[group]
    t = pl.program_id(0)
    tq = q_ref.shape[0] // group

    @pl.when(t * group < nu_ref[0])
    def _():
        lane = lax.broadcasted_iota(jnp.int32, (tq, LANES), 1)
        kpos = lax.broadcasted_iota(jnp.int32, (1, MOBA_BLOCK), 1)
        for g in range(group):
            tt = t * group + g
            hh = th_ref[tt]
            slope = sl_ref[2 * tp_ref[tt] + hh]
            head = (lane >> 6) == hh
            rows = pl.ds(g * tq, tq)
            q = jnp.where(head, q_ref[rows, :] * (MOBA_HEAD_DIM ** -0.5), 0.0).astype(BF16)
            s = lax.dot_general(q, kv_refs[g][0, :, :LANES], _NT, preferred_element_type=F32)
            s = s + slope * (kpos + tn_ref[tt] * MOBA_BLOCK).astype(F32)
            m = jnp.max(s, axis=1, keepdims=True)
            pr = jnp.exp(s - m)
            l = jnp.sum(pr, axis=1, keepdims=True)
            o = jnp.dot(pr.astype(BF16), kv_refs[g][0, :, LANES:], preferred_element_type=F32) / l
            o_ref[rows, :] = jnp.where(head, o, m + jnp.log(l))


def _moba_join_kernel(sl_ref, q_ref, kv_ref, pg_ref, o_ref):
    BLK = MOBA_BLOCK
    p = pl.program_id(1)
    i = pl.program_id(2)
    qf = q_ref[0]
    k_own = kv_ref[0, :, :LANES]
    v_own = kv_ref[0, :, LANES:]
    lane = lax.broadcasted_iota(jnp.int32, (BLK, LANES), 1)
    rr = lax.broadcasted_iota(jnp.int32, (BLK, BLK), 0)
    cc = lax.broadcasted_iota(jnp.int32, (BLK, BLK), 1)
    key_pos = (lax.broadcasted_iota(jnp.int32, (1, BLK), 1) + i * BLK).astype(F32)
    outs = []
    for hh in range(2):
        head = (lane >> 6) == hh
        slope = sl_ref[2 * p + hh]
        qh = jnp.where(head, qf * (MOBA_HEAD_DIM ** -0.5), 0.0).astype(BF16)
        s = lax.dot_general(qh, k_own, _NT, preferred_element_type=F32)
        s = jnp.where(cc <= rr, s + slope * key_pos, -jnp.inf)
        m = jnp.max(s, axis=1, keepdims=True)
        pr = jnp.exp(s - m)
        l = jnp.sum(pr, axis=1, keepdims=True)
        lses = [m + jnp.log(l)]
        vals = [jnp.dot(pr.astype(BF16), v_own, preferred_element_type=F32) / l]
        for j in range(MOBA_TOPK):
            part = pg_ref[hh * MOBA_TOPK + j, 0]
            has_block = j < i
            lses.append(jnp.where(has_block, pltpu.roll(part, MOBA_HEAD_DIM, 1), -jnp.inf))
            vals.append(jnp.where(has_block, part, 0.0))
        top = functools.reduce(jnp.maximum, lses)
        ws = [jnp.exp(x - top) for x in lses]
        outs.append(sum(w * v for w, v in zip(ws, vals)) / sum(ws))
    o_ref[0] = jnp.where(lane < MOBA_HEAD_DIM, outs[0], outs[1])


def _moba_mixer_v2(h3, kv3, q6):
    B, S, _ = h3.shape
    T = B * S
    BLK = MOBA_BLOCK
    TQ = MOBA_TILE
    G = MOBA_TILE_GROUP
    NB = S // BLK
    GI = math.gcd(NB, MOBA_PLACE_GROUP)
    nbp = -(-NB // SUBLANES) * SUBLANES
    NP = MOBA_HEADS // 2
    n_rep = 2 * MOBA_TOPK
    slopes = jnp.asarray(_alibi_slope_list(MOBA_HEADS), F32)

    selrank, counts = pl.pallas_call(
        _moba_pick_kernel,
        grid=(B, NP, NB),
        in_specs=[pl.BlockSpec((1, BLK, LANES), lambda b, p, i: (b, i, p)),
                  pl.BlockSpec((1, BLK, LANES), lambda b, p, i: (b, i, NP + p))],
        out_specs=[pl.BlockSpec((1, 1, 2 * SUBLANES, BLK), lambda b, p, i: (b, p, 0, i)),
                   pl.BlockSpec((1, 1, nbp, LANES), lambda b, p, i: (b, p, i, 0))],
        out_shape=[jax.ShapeDtypeStruct((B, NP, 2 * SUBLANES, S), jnp.int32),
                   jax.ShapeDtypeStruct((B, NP, NB * nbp, LANES), F32)],
        scratch_shapes=[pltpu.VMEM((nbp, LANES), F32)],
        compiler_params=_params("parallel", "parallel", "arbitrary"),
        name="moba_pick",
    )(h3, h3)

    cnt = counts.reshape(B, NP, NB, nbp, LANES)[:, :, :, :NB, :2].astype(jnp.int32)
    cnt = cnt.transpose(0, 1, 2, 4, 3)
    base = jnp.cumsum(cnt, axis=2) - cnt
    total = jnp.sum(cnt, axis=2)
    padded = (total + TQ - 1) // TQ * TQ
    pend = jnp.cumsum(padded.reshape(-1))
    seg_start = (pend - padded.reshape(-1)).reshape(B, NP, 1, 2, NB)
    table = jnp.zeros((B, NP, NB, SUBLANES, LANES), F32).at[:, :, :, :2, :NB].set(
        (seg_start + base).astype(F32)).reshape(B, NP, NB * SUBLANES, LANES)
    n_seg = B * NP * 2 * NB
    max_tiles = -(-((T * NP * n_rep) // TQ + n_seg) // G) * G
    first_row = jnp.arange(max_tiles, dtype=jnp.int32) * TQ
    tile_seg = jnp.minimum(jnp.sum(pend[None, :] <= first_row[:, None], axis=1), n_seg - 1)
    tile_n = (tile_seg % NB).astype(jnp.int32)
    tile_h = ((tile_seg // NB) % 2).astype(jnp.int32)
    tile_p = ((tile_seg // (2 * NB)) % NP).astype(jnp.int32)
    tile_b = (tile_seg // (2 * NB * NP)).astype(jnp.int32)
    n_used = (pend[-1] // TQ).astype(jnp.int32).reshape(1)
    spare = max_tiles * TQ
    n_rows = spare + max(B * NP * MOBA_TOPK * n_rep * BLK, G * TQ)

    idx = pl.pallas_call(
        functools.partial(_moba_place_kernel, group=GI, spare=spare),
        grid=(B, NP, NB // GI),
        in_specs=[pl.BlockSpec((1, 1, 2 * SUBLANES, GI * BLK), lambda b, p, i: (b, p, 0, i)),
                  pl.BlockSpec((1, 1, GI * SUBLANES, LANES), lambda b, p, i: (b, p, i, 0))],
        out_specs=pl.BlockSpec((SC_IDX_ROWS, GI * BLK),
                               lambda b, p, i: (0, p * (T // (GI * BLK)) + b * (NB // GI) + i)),
        out_shape=jax.ShapeDtypeStruct((SC_IDX_ROWS, NP * T), jnp.int32),
        compiler_params=_params("parallel", "parallel", "parallel"),
        name="moba_place",
    )(selrank, table)

    qs = _sc_scatter_rows(q6.reshape(NP * T, LANES), idx, n_rep, n_rows)

    live = lambda t, tb, tp, th, tn, nu, sl: (jnp.where(t * G < nu[0], t, max_tiles // G), 0)
    kv_blk = lambda g: pl.BlockSpec(
        (1, BLK, 2 * LANES),
        lambda t, tb, tp, th, tn, nu, sl: (tb[t * G + g], tn[t * G + g], tp[t * G + g]))
    part = pl.pallas_call(
        functools.partial(_moba_group_kernel, group=G),
        grid_spec=pltpu.PrefetchScalarGridSpec(
            num_scalar_prefetch=6,
            grid=(max_tiles // G,),
            in_specs=[pl.BlockSpec((G * TQ, LANES), live)] + [kv_blk(g) for g in range(G)],
            out_specs=pl.BlockSpec((G * TQ, LANES), live)),
        out_shape=jax.ShapeDtypeStruct((n_rows, LANES), F32),
        compiler_params=_params("arbitrary"),
        name="moba_tiles",
    )(tile_b, tile_p, tile_h, tile_n, n_used, slopes, qs, *([kv3] * G))

    pg = _sc_gather_rows(part, idx[:n_rep].reshape(-1)).reshape(n_rep, NP, T, LANES)

    return pl.pallas_call(
        _moba_join_kernel,
        grid_spec=pltpu.PrefetchScalarGridSpec(
            num_scalar_prefetch=1,
            grid=(B, NP, NB),
            in_specs=[pl.BlockSpec((1, BLK, LANES), lambda b, p, i, sl: (b, i, p)),
                      pl.BlockSpec((1, BLK, 2 * LANES), lambda b, p, i, sl: (b, i, p)),
                      pl.BlockSpec((n_rep, 1, BLK, LANES), lambda b, p, i, sl: (0, p, b * NB + i, 0))],
            out_specs=pl.BlockSpec((1, BLK, LANES), lambda b, p, i, sl: (b, i, p))),
        out_shape=jax.ShapeDtypeStruct((B, S, MIX_WIDTH), F32),
        compiler_params=_params("parallel", "parallel", "parallel"),
        name="moba_join",
    )(slopes, h3, kv3, pg)


def _memkv_kernel(mem_ref, w_ref, kv_ref):
    kv_ref[0] = jnp.dot(mem_ref[0].astype(BF16), w_ref[...].astype(BF16),
                        preferred_element_type=F32).astype(BF16)


def _memkv(mem, w_kv):
    B, M, D = mem.shape
    N = w_kv.shape[1]
    return pl.pallas_call(
        _memkv_kernel,
        grid=(B,),
        in_specs=[pl.BlockSpec((1, M, D), lambda b: (b, 0, 0)),
                  pl.BlockSpec((D, N), lambda b: (0, 0))],
        out_specs=pl.BlockSpec((1, M, N), lambda b: (b, 0, 0)),
        out_shape=jax.ShapeDtypeStruct((B, M, N), BF16),
        compiler_params=_params("parallel"),
        name="memkv",
    )(mem, w_kv)


def _layer_norm(z, g, b):
    mu = jnp.mean(z, axis=-1, keepdims=True)
    zc = z - mu
    var = jnp.mean(zc * zc, axis=-1, keepdims=True)
    return zc * lax.rsqrt(var + LN_EPS) * g + b


def _post_kernel(x_ref, mix_ref, mq_ref, kv_ref, wo_ref, g_ref, b_ref, wr_ref, br_ref,
                 x1_ref, x1s_ref, idx_ref, gate_ref, *, alpha):
    tm = x_ref.shape[0]
    mq = mq_ref[...]
    kv = kv_ref[0]
    km = kv[:, :MEM_WIDTH]
    vm = kv[:, MEM_WIDTH:]
    lane = lax.broadcasted_iota(jnp.int32, (tm, MEM_WIDTH), 1)
    scale = MEM_HEAD_DIM ** -0.5
    mo = jnp.zeros((tm, MEM_WIDTH), F32)
    for hd in range(MEM_HEADS):
        head = (lane >> 6) == hd
        qh = jnp.where(head, mq * scale, 0.0).astype(BF16)
        s = lax.dot_general(qh, km, _NT, preferred_element_type=F32)
        m = jnp.max(s, axis=1, keepdims=True)
        p = jnp.exp(s - m)
        l = jnp.sum(p, axis=1, keepdims=True)
        oh = jnp.dot(p.astype(BF16), vm, preferred_element_type=F32) / l
        mo = jnp.where(head, oh, mo)

    y = jnp.dot(mix_ref[...].astype(BF16), wo_ref[:MIX_WIDTH, :], preferred_element_type=F32)
    y = y + jnp.dot(mo.astype(BF16), wo_ref[MIX_WIDTH:, :], preferred_element_type=F32)
    x1 = _layer_norm(alpha * x_ref[...] + y, g_ref[...], b_ref[...])
    x1_ref[...] = x1
    _store_subrows(x1s_ref, x1)

    x_hi = x1.astype(BF16)
    x_lo = (x1 - x_hi.astype(F32)).astype(BF16)
    logits = (jnp.dot(x_hi, wr_ref[0], preferred_element_type=F32)
              + jnp.dot(x_lo, wr_ref[0], preferred_element_type=F32)
              + jnp.dot(x_hi, wr_ref[1], preferred_element_type=F32) + br_ref[...])
    lane_e = lax.broadcasted_iota(jnp.int32, (tm, LANES), 1)
    g = jnp.where(lane_e < N_EXPERTS, logits, -jnp.inf)
    idx_out = jnp.zeros((tm, LANES), jnp.int32)
    val_out = jnp.full((tm, LANES), -jnp.inf, F32)
    for kk in range(TOP_K):
        mx = jnp.max(g, axis=1, keepdims=True)
        idx = jnp.min(jnp.where(g == mx, lane_e, LANES), axis=1, keepdims=True)
        idx_out = jnp.where(lane_e == kk, idx, idx_out)
        val_out = jnp.where(lane_e == kk, mx, val_out)
        g = jnp.where(lane_e == idx, -jnp.inf, g)
    vmax = jnp.max(val_out, axis=1, keepdims=True)
    ev = jnp.exp(val_out - vmax)
    idx_ref[...] = idx_out
    gate_ref[...] = ev / jnp.sum(ev, axis=1, keepdims=True)


def _post_mixer(x2, mix2, h2, kv, w_o_bf16, ln_g, ln_b, w_router, b_router, seq_len, alpha):
    T, D = x2.shape
    tm = POST_ROWS
    N = h2.shape[1]
    M = kv.shape[1]
    mq_col = (N - MEM_WIDTH) // MEM_WIDTH
    tiles_per_seq = seq_len // tm
    wr = jnp.zeros((D, LANES), F32).at[:, :N_EXPERTS].set(w_router)
    wr_hi = wr.astype(BF16)
    wr = jnp.stack([wr_hi, (wr - wr_hi.astype(F32)).astype(BF16)])
    br = jnp.zeros((1, LANES), F32).at[0, :N_EXPERTS].set(b_router)
    row = lambda n: pl.BlockSpec((tm, n), lambda i: (i, 0))
    full = lambda a, b: pl.BlockSpec((a, b), lambda i: (0, 0))
    return pl.pallas_call(
        functools.partial(_post_kernel, alpha=alpha),
        grid=(T // tm,),
        in_specs=[row(D), row(MIX_WIDTH),
                  pl.BlockSpec((tm, MEM_WIDTH), lambda i: (i, mq_col)),
                  pl.BlockSpec((1, M, 2 * MEM_WIDTH), lambda i: (i // tiles_per_seq, 0, 0)),
                  full(D, D), full(1, D), full(1, D),
                  pl.BlockSpec((2, D, LANES), lambda i: (0, 0, 0)), full(1, LANES)],
        out_specs=[row(D), pl.BlockSpec((tm * (D // LANES), LANES), lambda i: (i, 0)),
                   row(LANES), row(LANES)],
        out_shape=[jax.ShapeDtypeStruct((T, D), F32),
                   jax.ShapeDtypeStruct((T * (D // LANES), LANES), F32),
                   jax.ShapeDtypeStruct((T, LANES), jnp.int32),
                   jax.ShapeDtypeStruct((T, LANES), F32)],
        compiler_params=_params("parallel"),
        name="post_mixer",
    )(x2, mix2, h2, kv, w_o_bf16, ln_g.reshape(1, D), ln_b.reshape(1, D), wr, br)


def _sc_mesh():
    return plsc.VectorSubcoreMesh(core_axis_name="core", subcore_axis_name="subcore")


def _sc_scatter_rows(rows, idx, n_rep, n_out):
    R, W = rows.shape

    @functools.partial(pl.kernel, out_type=jax.ShapeDtypeStruct((n_out, W), rows.dtype),
                       mesh=_sc_mesh(), scratch_types=[])
    def scatter(x_hbm, i_hbm, o_hbm):
        def body(x_vmem, i_vmem):
            for r in range(n_rep):
                pltpu.sync_copy(x_vmem, o_hbm.at[i_vmem.at[r]])

        pltpu.emit_pipeline(
            body, grid=(R // SC_WINDOW,),
            in_specs=[pl.BlockSpec((SC_WINDOW, W), lambda i: (i, 0)),
                      pl.BlockSpec((SC_IDX_ROWS, SC_WINDOW), lambda i: (0, i))],
            out_specs=[], core_axis_name=("core", "subcore"),
            dimension_semantics=(pltpu.PARALLEL,), trace_scopes=False)(x_hbm, i_hbm)

    return scatter(rows, idx)


def _sc_gather_rows(table, idx):
    n = idx.shape[0]
    W = table.shape[1]

    @functools.partial(pl.kernel, out_type=jax.ShapeDtypeStruct((n, W), table.dtype),
                       mesh=_sc_mesh(), scratch_types=[])
    def gather(t_hbm, i_hbm, o_hbm):
        def body(i_vmem, o_vmem):
            pltpu.sync_copy(t_hbm.at[i_vmem.at[0]], o_vmem)

        pltpu.emit_pipeline(
            body, grid=(n // SC_WINDOW,),
            in_specs=[pl.BlockSpec((1, SC_WINDOW), lambda i: (0, i))],
            out_specs=[pl.BlockSpec((SC_WINDOW, W), lambda i: (i, 0))],
            core_axis_name=("core", "subcore"),
            dimension_semantics=(pltpu.PARALLEL,), trace_scopes=False)(i_hbm, o_hbm)

    return gather(table, idx.reshape(1, n))


def _sc_workers():
    info = pltpu.get_tpu_info().sparse_core
    return info.num_cores, info.num_cores * info.num_subcores


def _sc_scatter_slabs(rows, idx, n_rep, n_out):
    R, S, W = rows.shape
    n_cores, n_workers = _sc_workers()
    per_worker = (R // SC_WINDOW) // n_workers
    assert per_worker * n_workers * SC_WINDOW == R

    @functools.partial(pl.kernel, out_type=jax.ShapeDtypeStruct((n_out, S, W), rows.dtype),
                       mesh=_sc_mesh(),
                       scratch_types=[pltpu.VMEM((SC_IDX_ROWS, SC_WINDOW), jnp.int32),
                                      pltpu.VMEM((SC_CHUNK, S, W), rows.dtype)])
    def scatter(x_hbm, i_hbm, o_hbm, ibuf, buf):
        wid = lax.axis_index("subcore") * n_cores + lax.axis_index("core")

        @pl.loop(0, per_worker)
        def _(s):
            first = (wid * per_worker + s) * SC_WINDOW
            pltpu.sync_copy(i_hbm.at[:, pl.ds(first, SC_WINDOW)], ibuf)
            for c in range(SC_WINDOW // SC_CHUNK):
                pltpu.sync_copy(x_hbm.at[pl.ds(first + c * SC_CHUNK, SC_CHUNK)], buf)
                for r in range(n_rep):
                    pltpu.sync_copy(buf, o_hbm.at[ibuf.at[r, pl.ds(c * SC_CHUNK, SC_CHUNK)]])

    return scatter(rows, idx)


def _sc_gather_slabs(table, idx):
    n = idx.shape[0]
    S, W = table.shape[1:]
    n_cores, n_workers = _sc_workers()
    per_worker = (n // SC_WINDOW) // n_workers
    assert per_worker * n_workers * SC_WINDOW == n
    n_chunks = SC_WINDOW // SC_CHUNK

    @functools.partial(pl.kernel, out_type=jax.ShapeDtypeStruct((n, S, W), table.dtype),
                       mesh=_sc_mesh(),
                       scratch_types=[pltpu.VMEM((1, SC_WINDOW), jnp.int32),
                                      pltpu.VMEM((2, SC_CHUNK, S, W), table.dtype),
                                      pltpu.SemaphoreType.DMA((2,)), pltpu.SemaphoreType.DMA((2,))])
    def gather(t_hbm, i_hbm, o_hbm, ibuf, buf, fetch_sem, store_sem):
        wid = lax.axis_index("subcore") * n_cores + lax.axis_index("core")

        @pl.loop(0, per_worker)
        def _(s):
            blk = wid * per_worker + s
            pltpu.sync_copy(i_hbm.at[pl.ds(blk, 1)], ibuf)

            def fetch(c):
                return pltpu.make_async_copy(
                    t_hbm.at[ibuf.at[0, pl.ds(c * SC_CHUNK, SC_CHUNK)]], buf.at[c % 2],
                    fetch_sem.at[c % 2])

            def store(c):
                return pltpu.make_async_copy(
                    buf.at[c % 2], o_hbm.at[pl.ds(blk * SC_WINDOW + c * SC_CHUNK, SC_CHUNK)],
                    store_sem.at[c % 2])

            fetch(0).start()
            for c in range(n_chunks):
                if c + 1 < n_chunks:
                    if c >= 1:
                        store(c - 1).wait()
                    fetch(c + 1).start()
                fetch(c).wait()
                store(c).start()
            store(n_chunks - 2).wait()
            store(n_chunks - 1).wait()

    return gather(table, idx.reshape(n // SC_WINDOW, SC_WINDOW))


def _route(top_idx, n_tokens):
    rb = MOE_ROWS
    tk = n_tokens * TOP_K
    flat_e = top_idx.reshape(-1)
    onehot = (flat_e[:, None] == jnp.arange(N_EXPERTS, dtype=jnp.int32)[None, :]).astype(jnp.int32)
    csum = jnp.cumsum(onehot, axis=0)
    rank = jnp.sum(onehot * csum, axis=1) - 1
    counts = csum[-1]
    padded = (counts + rb - 1) // rb * rb
    pend = jnp.cumsum(padded)
    pstart = pend - padded
    dest = (pstart[flat_e] + rank).astype(jnp.int32).reshape(n_tokens, TOP_K)
    n_blocks = tk // rb + N_EXPERTS
    first_row = jnp.arange(n_blocks, dtype=jnp.int32) * rb
    block_e = jnp.minimum(jnp.sum(pend[None, :] <= first_row[:, None], axis=1),
                          N_EXPERTS - 1).astype(jnp.int32)
    n_used = (pend[-1] // rb).astype(jnp.int32).reshape(1)
    return dest, block_e, n_used


def _dispatch(x1s, dest, n_rows, sub):
    T = dest.shape[0]
    idx = jnp.concatenate([dest.T, jnp.zeros((SC_IDX_ROWS - TOP_K, T), jnp.int32)], axis=0)
    xs = _sc_scatter_slabs(x1s.reshape(T, sub, LANES), idx, TOP_K, n_rows)
    return xs.reshape(n_rows * sub, LANES)


def _expert_kernel(be_ref, nu_ref, x_ref, wg_ref, bg_ref, wu_ref, bu_ref, wd_ref, bd_ref,
                   y_ref, wgb, wub, wdb):
    i = pl.program_id(0)
    prev = be_ref[jnp.maximum(i - 1, 0)]

    @pl.when((i == 0) | (be_ref[i] != prev))
    def _():
        wgb[...] = wg_ref[0, 0].astype(BF16)
        wub[...] = wu_ref[0, 0].astype(BF16)
        wdb[...] = wd_ref[0, 0].astype(BF16)

    @pl.when(i < nu_ref[0])
    def _():
        sub = wgb.shape[0] // LANES
        xb = _load_subrows(x_ref, x_ref.shape[0] // sub, sub).astype(BF16)
        gate = jnp.dot(xb, wgb[...], preferred_element_type=F32) + bg_ref[0, 0]
        gate = jnp.minimum(gate, SWIGLU_LIMIT)
        up = jnp.dot(xb, wub[...], preferred_element_type=F32) + bu_ref[0, 0]
        up = jnp.clip(up, -SWIGLU_LIMIT, SWIGLU_LIMIT)
        hid = gate * _sigmoid(SWIGLU_ALPHA * gate) * (up + 1.0)
        y = jnp.dot(hid.astype(BF16), wdb[...], preferred_element_type=F32) + bd_ref[0, 0]
        _store_subrows(y_ref, y)


def _experts(xs, block_e, n_used, layer, w_gate, b_gate, w_up, b_up, w_down, b_down):
    rb = MOE_ROWS
    n_blocks = block_e.shape[0]
    E, D, F = w_gate.shape[1:]
    sub = D // LANES
    wspec = lambda a, b: pl.BlockSpec((1, 1, a, b), lambda i, be, nu: (layer, be[i], 0, 0))
    live = lambda i, be, nu: (jnp.where(i < nu[0], i, n_blocks), 0)
    grid_spec = pltpu.PrefetchScalarGridSpec(
        num_scalar_prefetch=2,
        grid=(n_blocks,),
        in_specs=[pl.BlockSpec((rb * sub, LANES), live),
                  wspec(D, F), wspec(1, F), wspec(D, F), wspec(1, F), wspec(F, D), wspec(1, D)],
        out_specs=pl.BlockSpec((rb * sub, LANES), live),
        scratch_shapes=[pltpu.VMEM((D, F), BF16), pltpu.VMEM((D, F), BF16),
                        pltpu.VMEM((F, D), BF16)],
    )
    depth = w_gate.shape[0]
    return pl.pallas_call(
        _expert_kernel,
        grid_spec=grid_spec,
        out_shape=jax.ShapeDtypeStruct(xs.shape, F32),
        compiler_params=_params("arbitrary"),
        name="experts",
    )(block_e, n_used, xs, w_gate, b_gate.reshape(depth, E, 1, F),
      w_up, b_up.reshape(depth, E, 1, F), w_down, b_down.reshape(depth, E, 1, D))


def _combine_kernel(x1_ref, gate_ref, y_ref, g_ref, b_ref, o_ref, *, alpha):
    tm, D = x1_ref.shape
    sub = D // LANES
    gates = gate_ref[...]
    f = jnp.zeros(x1_ref.shape, F32)
    for kk in range(TOP_K):
        f = f + gates[:, kk:kk + 1] * _load_subrows(y_ref, tm, sub, kk * sub, TOP_K * sub)
    o_ref[...] = _layer_norm(alpha * x1_ref[...] + f, g_ref[...], b_ref[...])


def _combine(x1, gates, dest, y_rows, ln_g, ln_b, alpha):
    T, D = x1.shape
    sub = D // LANES
    tm = COMBINE_ROWS
    yg = _sc_gather_slabs(y_rows.reshape(-1, sub, LANES), dest.reshape(-1)).reshape(-1, LANES)
    row = lambda n: pl.BlockSpec((tm, n), lambda i: (i, 0))
    full = lambda a, b: pl.BlockSpec((a, b), lambda i: (0, 0))
    return pl.pallas_call(
        functools.partial(_combine_kernel, alpha=alpha),
        grid=(T // tm,),
        in_specs=[row(D), row(LANES), pl.BlockSpec((tm * TOP_K * sub, LANES), lambda i: (i, 0)),
                  full(1, D), full(1, D)],
        out_specs=row(D),
        out_shape=jax.ShapeDtypeStruct((T, D), F32),
        compiler_params=_params("parallel"),
        name="combine",
    )(x1, gates, yg, ln_g.reshape(1, D), ln_b.reshape(1, D))


def kernel(x, mem, w_in_hgrn, hgrn_lb_logits, hgrn_norm_g, w_in_moba, w_mem_kv, w_o,
           ln_mix_g, ln_mix_b, w_router, b_router, w_gate, b_gate, w_up, b_up,
           w_down, b_down, ln_ffn_g, ln_ffn_b):
    B, S, D = x.shape
    T = B * S
    depth = w_o.shape[0]
    alpha = (2 * depth) ** 0.25

    p_lb = jax.nn.softmax(hgrn_lb_logits.astype(F32), axis=0)
    lower_bounds = jnp.cumsum(p_lb, axis=0) - p_lb[0]

    x2 = x.reshape(T, D)
    for layer in range(depth):
        j = layer // 2
        if layer % 2 == 0:
            (h2,) = _inproj(x2, w_in_hgrn[j].astype(BF16), for_moba=False)
            mix = _hgrn_mixer(h2.reshape(B, S, -1), lower_bounds[j], hgrn_norm_g[j])
        else:
            h2, kv2, q6 = _inproj(x2, w_in_moba[j].astype(BF16), for_moba=True)
            mix = _moba_mixer_v2(h2.reshape(B, S, -1), kv2.reshape(B, S, -1), q6)
        kv = _memkv(mem, w_mem_kv[layer])
        x1, x1s, top_idx, gates = _post_mixer(
            x2, mix.reshape(T, MIX_WIDTH), h2, kv, w_o[layer].astype(BF16),
            ln_mix_g[layer], ln_mix_b[layer], w_router[layer], b_router[layer], S, alpha)
        dest, block_e, n_used = _route(top_idx[:, :TOP_K], T)
        xs = _dispatch(x1s, dest, (block_e.shape[0] + 1) * MOE_ROWS, D // LANES)
        y_rows = _experts(xs, block_e, n_used, layer, w_gate, b_gate, w_up, b_up, w_down, b_down)
        x2 = _combine(x1, gates, dest, y_rows, ln_ffn_g[layer], ln_ffn_b[layer], alpha)
    return x2.reshape(B, S, D)
```

```python
import functools
import math

import jax
import jax.numpy as jnp
from jax import lax
from jax.experimental import pallas as pl
from jax.experimental.pallas import tpu as pltpu
from jax.experimental.pallas import tpu_sc as plsc

MIX_WIDTH = 768
MEM_HEADS = 4
MEM_HEAD_DIM = 64
MEM_WIDTH = MEM_HEADS * MEM_HEAD_DIM
HGRN_HEADS = 6
HGRN_DK = 128
MOBA_HEADS = 12
MOBA_HEAD_DIM = 64
MOBA_BLOCK = 256
MOBA_TOPK = 3
N_EXPERTS = 32
TOP_K = 4
SWIGLU_ALPHA = 1.702
SWIGLU_LIMIT = 7.0
LN_EPS = 1e-5
RMS_EPS = 1e-6

LANES = 128
SUBLANES = 8
VMEM_LIMIT_BYTES = 56 * 1024 * 1024

INPROJ_ROWS = 512
HGRN_CHUNK = 64
HGRN_ROWS = 512
POST_ROWS = 256
MOBA_TILE = 256
MOBA_TILE_GROUP = 8
MOBA_PLACE_GROUP = 8
MOBA_PICK_GROUP = 4
MOE_ROWS = 512
COMBINE_ROWS = 256
SC_WINDOW = 128
SC_IDX_ROWS = 8
SC_CHUNK = 32

BF16 = jnp.bfloat16
F32 = jnp.float32
NEG_BIG = -1e30

_NT = (((1,), (1,)), ((), ()))
_TN = (((0,), (0,)), ((), ()))


def _alibi_slope_list(n):
    def pow2(m):
        start = 2.0 ** (-(2.0 ** -(math.log2(m) - 3)))
        return [start ** (i + 1) for i in range(m)]
    if math.log2(n).is_integer():
        return pow2(n)
    c = 2 ** math.floor(math.log2(n))
    return pow2(c) + _alibi_slope_list(2 * c)[0::2][:n - c]


def _sigmoid(x):
    return 1.0 / (1.0 + jnp.exp(-x))


def _params(*sem):
    return pltpu.CompilerParams(dimension_semantics=sem, vmem_limit_bytes=VMEM_LIMIT_BYTES)


def _store_subrows(ref, value, first=0, stride=None):
    sub = value.shape[1] // LANES
    stride = stride or sub
    for c in range(sub):
        ref[pl.ds(first + c, value.shape[0], stride=stride), :] = value[:, c * LANES:(c + 1) * LANES]


def _load_subrows(ref, rows, sub, first=0, stride=None):
    stride = stride or sub
    return jnp.concatenate(
        [ref[pl.ds(first + c, rows, stride=stride), :] for c in range(sub)], axis=1)


def _inproj_kernel(x_ref, w_ref, h_ref, *moba_refs):
    h = jnp.dot(x_ref[...].astype(BF16), w_ref[...], preferred_element_type=F32)
    h_ref[...] = h
    if moba_refs:
        kv_ref, q6_ref = moba_refs
        n_pairs = q6_ref.shape[0]
        for p in range(n_pairs):
            q6_ref[p] = h[:, p * LANES:(p + 1) * LANES]
            for part in range(2):
                col = (1 + part) * MIX_WIDTH + p * LANES
                kv_ref[:, (2 * p + part) * LANES:(2 * p + part + 1) * LANES] = (
                    h[:, col:col + LANES].astype(BF16))


def _inproj(x2, w_bf16, for_moba):
    T, D = x2.shape
    N = w_bf16.shape[1]
    tm = INPROJ_ROWS
    out_shape = [jax.ShapeDtypeStruct((T, N), F32)]
    out_specs = [pl.BlockSpec((tm, N), lambda i: (i, 0))]
    if for_moba:
        NP = MOBA_HEADS // 2
        out_shape += [jax.ShapeDtypeStruct((T, 2 * MIX_WIDTH), BF16),
                      jax.ShapeDtypeStruct((NP, T, LANES), F32)]
        out_specs += [pl.BlockSpec((tm, 2 * MIX_WIDTH), lambda i: (i, 0)),
                      pl.BlockSpec((NP, tm, LANES), lambda i: (0, i, 0))]
    return pl.pallas_call(
        _inproj_kernel,
        grid=(T // tm,),
        in_specs=[pl.BlockSpec((tm, D), lambda i: (i, 0)),
                  pl.BlockSpec((D, N), lambda i: (0, 0))],
        out_specs=out_specs,
        out_shape=out_shape,
        compiler_params=_params("parallel"),
        name="inproj",
    )(x2, w_bf16)


def _cumsum_rows(x, row):
    n = x.shape[0]
    sh = 1
    while sh < n:
        x = x + jnp.where(row >= sh, pltpu.roll(x, sh, 0), 0.0)
        sh *= 2
    return x


def _bcast_row(a, group, r):
    n = a.shape[0]
    a3 = a.reshape(n // group, group, LANES)
    return jnp.broadcast_to(a3[:, r:r + 1, :], a3.shape).reshape(n, LANES)


def _hgrn_chunk(qr, fr, v, gr, lb, ng, e_sum, st_t):
    C = qr.shape[0]
    row = lax.broadcasted_iota(jnp.int32, (C, LANES), 0)
    rr = lax.broadcasted_iota(jnp.int32, (C, C), 0)
    cc = lax.broadcasted_iota(jnp.int32, (C, C), 1)

    q = qr * _sigmoid(qr)
    forget = lb + (1.0 - lb) * _sigmoid(fr)
    k = 1.0 - forget
    G = _cumsum_rows(jnp.log(forget), row)

    parts = []
    for s in range(SUBLANES):
        Gs = _bcast_row(G, SUBLANES, s)
        ks = _bcast_row(k, SUBLANES, s)
        parts.append((q * ks * jnp.exp(jnp.minimum(G - Gs, 0.0))).astype(BF16))
    a_diag = jnp.dot(jnp.concatenate(parts, axis=1), e_sum, preferred_element_type=F32)
    A = jnp.where(((rr >> 3) == (cc >> 3)) & (cc <= rr), a_diag, 0.0)

    m = SUBLANES
    while m < C:
        lg = int(math.log2(m))
        Gr = _bcast_row(G, 2 * m, m - 1)
        second = ((row >> lg) & 1) == 1
        qm = jnp.where(second, q * jnp.exp(jnp.minimum(G - Gr, 0.0)), 0.0)
        km = jnp.where(second, 0.0, k * jnp.exp(jnp.minimum(Gr - G, 0.0)))
        am = lax.dot_general(qm.astype(BF16), km.astype(BF16), _NT, preferred_element_type=F32)
        A = A + jnp.where((rr >> (lg + 1)) == (cc >> (lg + 1)), am, 0.0)
        m *= 2

    vb = v.astype(BF16)
    o = jnp.dot(A.astype(BF16), vb, preferred_element_type=F32)
    o = o + lax.dot_general((q * jnp.exp(G)).astype(BF16), st_t.astype(BF16), _NT,
                            preferred_element_type=F32)
    g_end = G[C - 1:C, :]
    kd = (k * jnp.exp(g_end - G)).astype(BF16)
    st_new = st_t * jnp.exp(g_end) + lax.dot_general(vb, kd, _TN, preferred_element_type=F32)

    ms = jnp.mean(o * o, axis=-1, keepdims=True)
    out = o * lax.rsqrt(ms + RMS_EPS) * ng * _sigmoid(gr)
    return out, st_new


def _hgrn_kernel(q_ref, f_ref, i_ref, g_ref, lb_ref, ng_ref, e_ref, o_ref, st_ref, *, chunk):
    @pl.when(pl.program_id(2) == 0)
    def _():
        st_ref[...] = jnp.zeros_like(st_ref)

    lb = lb_ref[0]
    ng = ng_ref[...]
    e_sum = e_ref[...]
    n_chunks = q_ref.shape[1] // chunk
    for c in range(n_chunks):
        sl = pl.ds(c * chunk, chunk)
        out, st_new = _hgrn_chunk(q_ref[0, sl, :], f_ref[0, sl, :], i_ref[0, sl, :],
                                  g_ref[0, sl, :], lb, ng, e_sum, st_ref[...])
        st_ref[...] = st_new
        o_ref[0, sl, :] = out


def _hgrn_mixer(h3, lb, norm_g):
    B, S, _ = h3.shape
    ts = min(HGRN_ROWS, S)
    C = HGRN_CHUNK
    H = HGRN_HEADS
    e_sum = (jnp.arange(SUBLANES * LANES)[:, None] // LANES == jnp.arange(C)[None, :] % SUBLANES
             ).astype(BF16)
    col = lambda off: pl.BlockSpec((1, ts, LANES), lambda b, h, s, off=off: (b, s, off + h))
    return pl.pallas_call(
        functools.partial(_hgrn_kernel, chunk=C),
        grid=(B, H, S // ts),
        in_specs=[col(0), col(H), col(2 * H), col(3 * H),
                  pl.BlockSpec((1, 1, LANES), lambda b, h, s: (h, 0, 0)),
                  pl.BlockSpec((1, LANES), lambda b, h, s: (0, 0)),
                  pl.BlockSpec((SUBLANES * LANES, C), lambda b, h, s: (0, 0))],
        out_specs=pl.BlockSpec((1, ts, LANES), lambda b, h, s: (b, s, h)),
        out_shape=jax.ShapeDtypeStruct((B, S, MIX_WIDTH), F32),
        scratch_shapes=[pltpu.VMEM((HGRN_DK, HGRN_DK), F32)],
        compiler_params=_params("parallel", "parallel", "arbitrary"),
        name="hgrn",
    )(h3, h3, h3, h3, lb.reshape(H, 1, LANES), norm_g.reshape(1, LANES), e_sum)


def _moba_kernel(sl_ref, q_ref, kf_ref, kb_ref, vb_ref, o_ref, kmean_ref):
    BLK = MOBA_BLOCK
    p = pl.program_id(1)
    i = pl.program_id(2)

    @pl.when(i == 0)
    def _():
        kmean_ref[...] = jnp.zeros_like(kmean_ref)

    qf = q_ref[0]
    own = pl.ds(pl.multiple_of(i * BLK, BLK), BLK)
    k_own = kb_ref[0, own, :]
    v_own = vb_ref[0, own, :]

    lane = lax.broadcasted_iota(jnp.int32, (BLK, LANES), 1)
    rr = lax.broadcasted_iota(jnp.int32, (BLK, BLK), 0)
    cc = lax.broadcasted_iota(jnp.int32, (BLK, BLK), 1)
    rel = (rr - cc).astype(F32)
    scale = MOBA_HEAD_DIM ** -0.5

    outs = []
    for hh in range(2):
        head = (lane < MOBA_HEAD_DIM) if hh == 0 else (lane >= MOBA_HEAD_DIM)
        slope = sl_ref[2 * p + hh]
        qh = jnp.where(head, qf, 0.0)

        gate = lax.dot_general(qh, kmean_ref[...], _NT, precision=lax.Precision.HIGHEST,
                               preferred_element_type=F32)
        past = lane < i
        g = jnp.where(past, gate, -jnp.inf)
        sel = jnp.zeros((BLK, LANES), jnp.bool_)
        for _ in range(MOBA_TOPK):
            mx = jnp.max(g, axis=1, keepdims=True)
            idx = jnp.min(jnp.where(g == mx, lane, LANES), axis=1, keepdims=True)
            pick = lane == idx
            sel = sel | pick
            g = jnp.where(pick, -jnp.inf, g)
        sel_bias = jnp.where(sel & past, 0.0, NEG_BIG)
        q_aug = jnp.concatenate([(qh * scale).astype(BF16), sel_bias.astype(BF16)], axis=1)

        bias = -slope * rel
        s = lax.dot_general((qh * scale).astype(BF16), k_own, _NT, preferred_element_type=F32)
        s = s + jnp.where(cc <= rr, bias, -jnp.inf)
        m0 = jnp.max(s, axis=1, keepdims=True)
        p0 = jnp.exp(s - m0)
        l0 = jnp.sum(p0, axis=1, keepdims=True)
        acc0 = jnp.dot(p0.astype(BF16), v_own, preferred_element_type=F32)

        def body(n, carry):
            m, l, acc = carry
            blk = pl.ds(pl.multiple_of(n * BLK, BLK), BLK)
            onehot = jnp.where(lane == n, 1.0, 0.0).astype(BF16)
            k_aug = jnp.concatenate([kb_ref[0, blk, :], onehot], axis=1)
            sn = lax.dot_general(q_aug, k_aug, _NT, preferred_element_type=F32)
            sn = sn + (bias - slope * ((i - n) * BLK).astype(F32))
            m_new = jnp.maximum(m, jnp.max(sn, axis=1, keepdims=True))
            a = jnp.exp(m - m_new)
            pn = jnp.exp(sn - m_new)
            l = a * l + jnp.sum(pn, axis=1, keepdims=True)
            acc = a * acc + jnp.dot(pn.astype(BF16), vb_ref[0, blk, :],
                                    preferred_element_type=F32)
            return m_new, l, acc

        _, l, acc = lax.fori_loop(0, i, body, (m0, l0, acc0))
        outs.append(acc / l)

    o_ref[0] = jnp.where(lane < MOBA_HEAD_DIM, outs[0], outs[1])
    kmean_ref[pl.ds(i, 1), :] = jnp.mean(kf_ref[0], axis=0, keepdims=True)


def _moba_mixer(h3, hb3):
    B, S, _ = h3.shape
    BLK = MOBA_BLOCK
    NP = MOBA_HEADS // 2
    slopes = jnp.asarray(_alibi_slope_list(MOBA_HEADS), F32)
    grid_spec = pltpu.PrefetchScalarGridSpec(
        num_scalar_prefetch=1,
        grid=(B, NP, S // BLK),
        in_specs=[pl.BlockSpec((1, BLK, LANES), lambda b, p, i, sl: (b, i, p)),
                  pl.BlockSpec((1, BLK, LANES), lambda b, p, i, sl: (b, i, NP + p)),
                  pl.BlockSpec((1, S, LANES), lambda b, p, i, sl: (b, 0, NP + p)),
                  pl.BlockSpec((1, S, LANES), lambda b, p, i, sl: (b, 0, 2 * NP + p))],
        out_specs=pl.BlockSpec((1, BLK, LANES), lambda b, p, i, sl: (b, i, p)),
        scratch_shapes=[pltpu.VMEM((LANES, LANES), F32)],
    )
    return pl.pallas_call(
        _moba_kernel,
        grid_spec=grid_spec,
        out_shape=jax.ShapeDtypeStruct((B, S, MIX_WIDTH), F32),
        compiler_params=_params("parallel", "parallel", "arbitrary"),
        name="moba",
    )(slopes, h3, h3, hb3, hb3)


def _moba_top3(qh, kmean, i, lane):
    gate = lax.dot_general(qh, kmean, _NT, precision=lax.Precision.HIGHEST,
                           preferred_element_type=F32)
    g = jnp.where(lane < i, gate, -jnp.inf)
    picks = []
    for _ in range(MOBA_TOPK):
        mx = jnp.max(g, axis=1, keepdims=True)
        idx = jnp.min(jnp.where(g == mx, lane, LANES), axis=1, keepdims=True)
        picks.append((idx, (mx > -jnp.inf) & (idx < i)))
        g = jnp.where(lane == idx, -jnp.inf, g)
    return picks


def _moba_select_kernel(q_ref, kf_ref, sr_ref, cnt_ref, kmean_ref):
    BLK = MOBA_BLOCK
    i = pl.program_id(2)

    @pl.when(i == 0)
    def _():
        kmean_ref[...] = jnp.zeros_like(kmean_ref)

    qf = q_ref[0]
    lane = lax.broadcasted_iota(jnp.int32, (BLK, LANES), 1)
    rr = lax.broadcasted_iota(jnp.int32, (BLK, BLK), 0)
    cc = lax.broadcasted_iota(jnp.int32, (BLK, BLK), 1)
    before = jnp.where(cc < rr, 1.0, 0.0).astype(BF16)
    out = jnp.zeros((BLK, LANES), jnp.int32)
    counts = []
    for hh in range(2):
        head = (lane >> 6) == hh
        picks = _moba_top3(jnp.where(head, qf, 0.0), kmean_ref[...], i, lane)
        chosen = jnp.zeros((BLK, LANES), F32)
        for idx, valid in picks:
            chosen = chosen + jnp.where((lane == idx) & valid, 1.0, 0.0)
        earlier = jnp.dot(before, chosen.astype(BF16), preferred_element_type=F32)
        for j, (idx, valid) in enumerate(picks):
            rep = hh * MOBA_TOPK + j
            rank = jnp.sum(jnp.where(lane == idx, earlier, 0.0), axis=1, keepdims=True)
            out = jnp.where(lane == rep, jnp.where(valid, idx, -1), out)
            out = jnp.where(lane == SUBLANES + rep, rank.astype(jnp.int32), out)
        total = earlier[BLK - 1:BLK, :] + chosen[BLK - 1:BLK, :]
        counts.append(jnp.broadcast_to(total, (SUBLANES, LANES)))
    sr_ref[0, 0] = out
    cnt_ref[0, 0] = jnp.concatenate(counts, axis=0)
    kmean_ref[pl.ds(i, 1), :] = jnp.mean(kf_ref[0], axis=0, keepdims=True)


def _moba_dest_kernel(sr_ref, tab_ref, idx_ref, *, trash):
    BLK = MOBA_BLOCK
    blk = sr_ref[0, 0]
    tab = tab_ref[0, 0]
    lane = lax.broadcasted_iota(jnp.int32, (BLK, LANES), 1)
    d = jnp.full((BLK, LANES), float(trash), F32)
    for rep in range(2 * MOBA_TOPK):
        hh = rep // MOBA_TOPK
        sel = blk[:, rep:rep + 1]
        rank = blk[:, SUBLANES + rep:SUBLANES + rep + 1]
        start = jnp.sum(jnp.where(lane == sel, tab[hh:hh + 1, :], 0.0), axis=1, keepdims=True)
        dest = jnp.where(sel >= 0, start + rank.astype(F32), float(trash))
        d = jnp.where(lane == rep, dest, d)
    idx_ref[...] = d.T[:SC_IDX_ROWS, :].astype(jnp.int32)


def _moba_tile_kernel(tb_ref, tp_ref, th_ref, tn_ref, nu_ref, sl_ref, q_ref, k_ref, v_ref, o_ref):
    t = pl.program_id(0)

    @pl.when(t < nu_ref[0])
    def _():
        tq = q_ref.shape[0]
        hh = th_ref[t]
        slope = sl_ref[2 * tp_ref[t] + hh]
        lane = lax.broadcasted_iota(jnp.int32, (tq, LANES), 1)
        head = (lane >> 6) == hh
        q = jnp.where(head, q_ref[...] * (MOBA_HEAD_DIM ** -0.5), 0.0).astype(BF16)
        s = lax.dot_general(q, k_ref[0], _NT, preferred_element_type=F32)
        kpos = lax.broadcasted_iota(jnp.int32, s.shape, 1).astype(F32)
        s = s + slope * kpos
        m = jnp.max(s, axis=1, keepdims=True)
        p = jnp.exp(s - m)
        l = jnp.sum(p, axis=1, keepdims=True)
        o = jnp.dot(p.astype(BF16), v_ref[0], preferred_element_type=F32)
        half = lane & (MOBA_HEAD_DIM - 1)
        stats = jnp.where(half == 0, m, jnp.where(half == 1, l, 0.0))
        o_ref[...] = jnp.where(head, o, stats)


def _moba_merge_kernel(sl_ref, q_ref, k_ref, v_ref, sr_ref, pg_ref, o_ref):
    BLK = MOBA_BLOCK
    p = pl.program_id(1)
    i = pl.program_id(2)
    qf = q_ref[0]
    k_own = k_ref[0]
    v_own = v_ref[0]
    blk = sr_ref[0, 0]
    lane = lax.broadcasted_iota(jnp.int32, (BLK, LANES), 1)
    rr = lax.broadcasted_iota(jnp.int32, (BLK, BLK), 0)
    cc = lax.broadcasted_iota(jnp.int32, (BLK, BLK), 1)
    rel = (rr - cc).astype(F32)
    row = lax.broadcasted_iota(jnp.int32, (BLK, 1), 0)
    outs = []
    for hh in range(2):
        head = (lane >> 6) == hh
        slope = sl_ref[2 * p + hh]
        qh = jnp.where(head, qf * (MOBA_HEAD_DIM ** -0.5), 0.0).astype(BF16)
        s = lax.dot_general(qh, k_own, _NT, preferred_element_type=F32)
        s = s + jnp.where(cc <= rr, -slope * rel, -jnp.inf)
        m = jnp.max(s, axis=1, keepdims=True)
        pr = jnp.exp(s - m)
        ms = [m]
        ls = [jnp.sum(pr, axis=1, keepdims=True)]
        accs = [jnp.dot(pr.astype(BF16), v_own, preferred_element_type=F32)]
        stat_lane = MOBA_HEAD_DIM * (1 - hh)
        for j in range(MOBA_TOPK):
            rep = hh * MOBA_TOPK + j
            part = pg_ref[rep, 0]
            sel = blk[:, rep:rep + 1]
            valid = sel >= 0
            shift = -slope * ((i - sel) * BLK + row).astype(F32)
            ms.append(jnp.where(valid, part[:, stat_lane:stat_lane + 1] + shift, -jnp.inf))
            ls.append(jnp.where(valid, part[:, stat_lane + 1:stat_lane + 2], 0.0))
            accs.append(jnp.where(valid, part, 0.0))
        top = functools.reduce(jnp.maximum, ms)
        ws = [jnp.exp(mk - top) for mk in ms]
        den = sum(w * lk for w, lk in zip(ws, ls))
        num = sum(w * ak for w, ak in zip(ws, accs))
        outs.append(num / den)
    o_ref[0] = jnp.where(lane < MOBA_HEAD_DIM, outs[0], outs[1])


def _moba_sparse_mixer(h3, hb3, q6):
    B, S, _ = h3.shape
    T = B * S
    BLK = MOBA_BLOCK
    TQ = MOBA_TILE
    NB = S // BLK
    NP = MOBA_HEADS // 2
    n_rep = 2 * MOBA_TOPK
    slopes = jnp.asarray(_alibi_slope_list(MOBA_HEADS), F32)
    step = lambda *rest: pl.BlockSpec((1, 1) + rest, lambda b, p, i: (b, p, i) + (0,) * (len(rest) - 1))

    selrank, counts = pl.pallas_call(
        _moba_select_kernel,
        grid=(B, NP, NB),
        in_specs=[pl.BlockSpec((1, BLK, LANES), lambda b, p, i: (b, i, p)),
                  pl.BlockSpec((1, BLK, LANES), lambda b, p, i: (b, i, NP + p))],
        out_specs=[step(BLK, LANES), step(2 * SUBLANES, LANES)],
        out_shape=[jax.ShapeDtypeStruct((B, NP, S, LANES), jnp.int32),
                   jax.ShapeDtypeStruct((B, NP, NB * 2 * SUBLANES, LANES), F32)],
        scratch_shapes=[pltpu.VMEM((LANES, LANES), F32)],
        compiler_params=_params("parallel", "parallel", "arbitrary"),
        name="moba_select",
    )(h3, h3)

    cnt = counts.reshape(B, NP, NB, 2, SUBLANES, LANES)[:, :, :, :, 0, :NB].astype(jnp.int32)
    base = jnp.cumsum(cnt, axis=2) - cnt
    total = jnp.sum(cnt, axis=2)
    padded = (total + TQ - 1) // TQ * TQ
    pend = jnp.cumsum(padded.reshape(-1))
    seg_start = (pend - padded.reshape(-1)).reshape(B, NP, 1, 2, NB)
    table = jnp.zeros((B, NP, NB, SUBLANES, LANES), F32).at[:, :, :, :2, :NB].set(
        (seg_start + base).astype(F32)).reshape(B, NP, NB * SUBLANES, LANES)
    n_seg = B * NP * 2 * NB
    max_tiles = (T * NP * n_rep) // TQ + n_seg
    tile_seg = jnp.minimum(
        jnp.searchsorted(pend, jnp.arange(max_tiles, dtype=jnp.int32) * TQ, side='right'),
        n_seg - 1).astype(jnp.int32)
    tile_n = tile_seg % NB
    tile_h = (tile_seg // NB) % 2
    tile_p = (tile_seg // (2 * NB)) % NP
    tile_b = tile_seg // (2 * NB * NP)
    n_used = (pend[-1] // TQ).astype(jnp.int32).reshape(1)
    trash = max_tiles * TQ

    idx = pl.pallas_call(
        functools.partial(_moba_dest_kernel, trash=trash),
        grid=(B, NP, NB),
        in_specs=[step(BLK, LANES), step(SUBLANES, LANES)],
        out_specs=pl.BlockSpec((SC_IDX_ROWS, BLK), lambda b, p, i: (0, p * (T // BLK) + b * NB + i)),
        out_shape=jax.ShapeDtypeStruct((SC_IDX_ROWS, NP * T), jnp.int32),
        compiler_params=_params("parallel", "parallel", "parallel"),
        name="moba_dest",
    )(selrank, table)

    n_rows = (max_tiles + 1) * TQ
    qs = _sc_scatter_rows(q6.reshape(NP * T, LANES), idx, n_rep, n_rows)

    live = lambda t, tb, tp, th, tn, nu, sl: (jnp.where(t < nu[0], t, max_tiles), 0)
    kv_blk = lambda off: pl.BlockSpec(
        (1, BLK, LANES), lambda t, tb, tp, th, tn, nu, sl, off=off: (tb[t], tn[t], off + tp[t]))
    part = pl.pallas_call(
        _moba_tile_kernel,
        grid_spec=pltpu.PrefetchScalarGridSpec(
            num_scalar_prefetch=6,
            grid=(max_tiles,),
            in_specs=[pl.BlockSpec((TQ, LANES), live), kv_blk(NP), kv_blk(2 * NP)],
            out_specs=pl.BlockSpec((TQ, LANES), live)),
        out_shape=jax.ShapeDtypeStruct((n_rows, LANES), F32),
        compiler_params=_params("arbitrary"),
        name="moba_tiles",
    )(tile_b, tile_p, tile_h, tile_n, n_used, slopes, qs, hb3, hb3)

    pg = _sc_gather_rows(part, idx[:n_rep].reshape(-1)).reshape(n_rep, NP, T, LANES)

    own = lambda off: pl.BlockSpec((1, BLK, LANES), lambda b, p, i, sl, off=off: (b, i, off + p))
    return pl.pallas_call(
        _moba_merge_kernel,
        grid_spec=pltpu.PrefetchScalarGridSpec(
            num_scalar_prefetch=1,
            grid=(B, NP, NB),
            in_specs=[own(0), own(NP), own(2 * NP),
                      pl.BlockSpec((1, 1, BLK, LANES), lambda b, p, i, sl: (b, p, i, 0)),
                      pl.BlockSpec((n_rep, 1, BLK, LANES), lambda b, p, i, sl: (0, p, b * NB + i, 0))],
            out_specs=pl.BlockSpec((1, BLK, LANES), lambda b, p, i, sl: (b, i, p))),
        out_shape=jax.ShapeDtypeStruct((B, S, MIX_WIDTH), F32),
        compiler_params=_params("parallel", "parallel", "parallel"),
        name="moba_merge",
    )(slopes, h3, hb3, hb3, selrank, pg)


def _moba_pick_kernel(q_ref, kf_ref, sr_ref, cnt_ref, kmean_ref, *, qblocks):
    BLK = MOBA_BLOCK
    W = qblocks * BLK
    nbp = kmean_ref.shape[0]
    i0 = pl.program_id(2) * qblocks

    @pl.when(pl.program_id(2) == 0)
    def _():
        kmean_ref[...] = jnp.zeros_like(kmean_ref)

    for j in range(qblocks):
        kmean_ref[pl.ds(i0 + j, 1), :] = jnp.mean(kf_ref[0, j * BLK:(j + 1) * BLK, :], axis=0,
                                                  keepdims=True)
    km = kmean_ref[...]
    lane_k = lax.broadcasted_iota(jnp.int32, (nbp, LANES), 1)
    km2 = jnp.concatenate([jnp.where(lane_k < MOBA_HEAD_DIM, km, 0.0),
                           jnp.where(lane_k >= MOBA_HEAD_DIM, km, 0.0)], axis=0)
    gate = lax.dot_general(km2, q_ref[0], _NT, precision=lax.Precision.HIGHEST,
                           preferred_element_type=F32).reshape(2, nbp, W)
    nblk = lax.broadcasted_iota(jnp.int32, (2, nbp, W), 1)
    qi = i0 + (lax.broadcasted_iota(jnp.int32, (2, 1, W), 2) >> int(math.log2(BLK)))
    g = jnp.where(nblk < qi, gate, -jnp.inf)
    picks = []
    for _ in range(MOBA_TOPK):
        mx = jnp.max(g, axis=1, keepdims=True)
        idx = jnp.min(jnp.where(g == mx, nblk, nbp), axis=1, keepdims=True)
        picks.append((idx, (mx > -jnp.inf) & (idx < qi)))
        g = jnp.where(nblk == idx, -jnp.inf, g)
    chosen = jnp.zeros((2, nbp, W), F32)
    for idx, valid in picks:
        chosen = chosen + jnp.where((nblk == idx) & valid, 1.0, 0.0)
    qa = lax.broadcasted_iota(jnp.int32, (W, W), 0)
    qc = lax.broadcasted_iota(jnp.int32, (W, W), 1)
    shift = int(math.log2(BLK))
    before = jnp.where((qa < qc) & ((qa >> shift) == (qc >> shift)), 1.0, 0.0).astype(BF16)
    earlier = jnp.dot(chosen.reshape(2 * nbp, W).astype(BF16), before,
                      preferred_element_type=F32).reshape(2, nbp, W)
    orow = lax.broadcasted_iota(jnp.int32, (2 * SUBLANES, W), 0)
    out = jnp.zeros((2 * SUBLANES, W), jnp.int32)
    for j, (idx, valid) in enumerate(picks):
        rank = jnp.sum(jnp.where(nblk == idx, earlier, 0.0), axis=1, keepdims=True).astype(jnp.int32)
        sel = jnp.where(valid, idx, -1)
        for hh in range(2):
            rep = hh * MOBA_TOPK + j
            out = jnp.where(orow == rep, sel[hh], out)
            out = jnp.where(orow == SUBLANES + rep, rank[hh], out)
    sr_ref[0, 0] = out
    lane_c = lax.broadcasted_iota(jnp.int32, (nbp, LANES), 1)
    for j in range(qblocks):
        per_block = jnp.sum(chosen[:, :, j * BLK:(j + 1) * BLK], axis=2, keepdims=True)
        cnt_ref[0, 0, j * nbp:(j + 1) * nbp, :] = jnp.where(
            lane_c == 0, per_block[0], jnp.where(lane_c == 1, per_block[1], 0.0))


def _moba_place_kernel(sr_ref, tab_ref, idx_ref, *, group, spare):
    BLK = MOBA_BLOCK
    n_rep = 2 * MOBA_TOPK
    b = pl.program_id(0)
    p = pl.program_id(1)
    ig = pl.program_id(2)
    nrow = lax.broadcasted_iota(jnp.int32, (LANES, BLK), 0)
    orow = lax.broadcasted_iota(jnp.int32, (SC_IDX_ROWS, BLK), 0)
    qpos = lax.broadcasted_iota(jnp.int32, (1, BLK), 1)
    for g in range(group):
        blk = sr_ref[0, 0, :, g * BLK:(g + 1) * BLK]
        tab_t = tab_ref[0, 0, g * SUBLANES:(g + 1) * SUBLANES, :].T
        early = jnp.minimum(ig * group + g, MOBA_TOPK - 1)
        out = jnp.zeros((SC_IDX_ROWS, BLK), jnp.int32)
        for rep in range(n_rep):
            hh = rep // MOBA_TOPK
            sel = blk[rep:rep + 1, :]
            rank = blk[SUBLANES + rep:SUBLANES + rep + 1, :]
            start = jnp.sum(jnp.where(nrow == sel, tab_t[:, hh:hh + 1], 0.0), axis=0, keepdims=True)
            unused = spare + (((b * pl.num_programs(1) + p) * MOBA_TOPK + early) * n_rep + rep) * BLK
            dest = jnp.where(sel >= 0, start.astype(jnp.int32) + rank, unused + qpos)
            out = jnp.where(orow == rep, dest, out)
        idx_ref[:, g * BLK:(g + 1) * BLK] = out


def _moba_group_kernel(tb_ref, tp_ref, th_ref, tn_ref, nu_ref, sl_ref, q_ref, *refs, group):
    kv_refs, o_ref = refs[:group], refs[group]
    t = pl.program_id(0)
    tq = q_ref.shape[0] // group

    @pl.when(t * group < nu_ref[0])
    def _():
        lane = lax.broadcasted_iota(jnp.int32, (tq, LANES), 1)
        kpos = lax.broadcasted_iota(jnp.int32, (1, MOBA_BLOCK), 1)
        for g in range(group):
            tt = t * group + g
            hh = th_ref[tt]
            slope = sl_ref[2 * tp_ref[tt] + hh]
            head = (lane >> 6) == hh
            rows = pl.ds(g * tq, tq)
            q = jnp.where(head, q_ref[rows, :] * (MOBA_HEAD_DIM ** -0.5), 0.0).astype(BF16)
            s = lax.dot_general(q, kv_refs[g][0, :, :LANES], _NT, preferred_element_type=F32)
            s = s + slope * (kpos + tn_ref[tt] * MOBA_BLOCK).astype(F32)
            m = jnp.max(s, axis=1, keepdims=True)
            pr = jnp.exp(s - m)
            l = jnp.sum(pr, axis=1, keepdims=True)
            o = jnp.dot(pr.astype(BF16), kv_refs[g][0, :, LANES:], preferred_element_type=F32) / l
            o_ref[rows, :] = jnp.where(head, o, m + jnp.log(l))


def _moba_own_kernel(sl_ref, q_ref, kv_ref, o_ref, lse_ref):
    BLK = MOBA_BLOCK
    p = pl.program_id(1)
    i = pl.program_id(2)
    qf = q_ref[0]
    k_own = kv_ref[0, :, :LANES]
    v_own = kv_ref[0, :, LANES:]
    lane = lax.broadcasted_iota(jnp.int32, (BLK, LANES), 1)
    rr = lax.broadcasted_iota(jnp.int32, (BLK, BLK), 0)
    cc = lax.broadcasted_iota(jnp.int32, (BLK, BLK), 1)
    key_pos = (lax.broadcasted_iota(jnp.int32, (1, BLK), 1) + i * BLK).astype(F32)
    outs, lses = [], []
    for hh in range(2):
        head = (lane >> 6) == hh
        slope = sl_ref[2 * p + hh]
        qh = jnp.where(head, qf * (MOBA_HEAD_DIM ** -0.5), 0.0).astype(BF16)
        s = lax.dot_general(qh, k_own, _NT, preferred_element_type=F32)
        s = jnp.where(cc <= rr, s + slope * key_pos, -jnp.inf)
        m = jnp.max(s, axis=1, keepdims=True)
        pr = jnp.exp(s - m)
        l = jnp.sum(pr, axis=1, keepdims=True)
        lses.append(m + jnp.log(l))
        outs.append(jnp.dot(pr.astype(BF16), v_own, preferred_element_type=F32) / l)
    first = lane < MOBA_HEAD_DIM
    o_ref[0] = jnp.where(first, outs[0], outs[1])
    lse_ref[0] = jnp.where(first, lses[0], lses[1])


def _moba_merge_lse_kernel(o_ref, lse_ref, pg_ref, out_ref):
    i = pl.program_id(2)
    first = lax.broadcasted_iota(jnp.int32, o_ref.shape[1:], 1) < MOBA_HEAD_DIM
    lses = [lse_ref[0]]
    vals = [o_ref[0]]
    for j in range(MOBA_TOPK):
        p0 = pg_ref[j, 0]
        p1 = pg_ref[MOBA_TOPK + j, 0]
        has_block = j < i
        stats = pltpu.roll(jnp.where(first, p1, p0), MOBA_HEAD_DIM, 1)
        lses.append(jnp.where(has_block, stats, -jnp.inf))
        vals.append(jnp.where(has_block, jnp.where(first, p0, p1), 0.0))
    top = functools.reduce(jnp.maximum, lses)
    ws = [jnp.exp(x - top) for x in lses]
    out_ref[0] = sum(w * v for w, v in zip(ws, vals)) / sum(ws)


def _moba_mixer_v2(h3, kv3, q6):
    B, S, _ = h3.shape
    T = B * S
    BLK = MOBA_BLOCK
    TQ = MOBA_TILE
    G = MOBA_TILE_GROUP
    NB = S // BLK
    GI = math.gcd(NB, MOBA_PLACE_GROUP)
    nbp = -(-NB // SUBLANES) * SUBLANES
    NP = MOBA_HEADS // 2
    n_rep = 2 * MOBA_TOPK
    slopes = jnp.asarray(_alibi_slope_list(MOBA_HEADS), F32)

    QB = math.gcd(NB, MOBA_PICK_GROUP)
    selrank, counts = pl.pallas_call(
        functools.partial(_moba_pick_kernel, qblocks=QB),
        grid=(B, NP, NB // QB),
        in_specs=[pl.BlockSpec((1, QB * BLK, LANES), lambda b, p, i: (b, i, p)),
                  pl.BlockSpec((1, QB * BLK, LANES), lambda b, p, i: (b, i, NP + p))],
        out_specs=[pl.BlockSpec((1, 1, 2 * SUBLANES, QB * BLK), lambda b, p, i: (b, p, 0, i)),
                   pl.BlockSpec((1, 1, QB * nbp, LANES), lambda b, p, i: (b, p, i, 0))],
        out_shape=[jax.ShapeDtypeStruct((B, NP, 2 * SUBLANES, S), jnp.int32),
                   jax.ShapeDtypeStruct((B, NP, NB * nbp, LANES), F32)],
        scratch_shapes=[pltpu.VMEM((nbp, LANES), F32)],
        compiler_params=_params("parallel", "parallel", "arbitrary"),
        name="moba_pick",
    )(h3, h3)

    blk_spec = lambda w: pl.BlockSpec((1, BLK, w), lambda b, p, i, sl: (b, i, p))
    own_o, own_lse = pl.pallas_call(
        _moba_own_kernel,
        grid_spec=pltpu.PrefetchScalarGridSpec(
            num_scalar_prefetch=1,
            grid=(B, NP, NB),
            in_specs=[blk_spec(LANES), blk_spec(2 * LANES)],
            out_specs=[blk_spec(LANES), blk_spec(LANES)]),
        out_shape=[jax.ShapeDtypeStruct((B, S, MIX_WIDTH), F32)] * 2,
        compiler_params=_params("parallel", "parallel", "parallel"),
        name="moba_own",
    )(slopes, h3, kv3)

    cnt = counts.reshape(B, NP, NB, nbp, LANES)[:, :, :, :NB, :2].astype(jnp.int32)
    cnt = cnt.transpose(0, 1, 2, 4, 3)
    base = jnp.cumsum(cnt, axis=2) - cnt
    total = jnp.sum(cnt, axis=2)
    padded = (total + TQ - 1) // TQ * TQ
    pend = jnp.cumsum(padded.reshape(-1))
    seg_start = (pend - padded.reshape(-1)).reshape(B, NP, 1, 2, NB)
    table = jnp.zeros((B, NP, NB, SUBLANES, LANES), F32).at[:, :, :, :2, :NB].set(
        (seg_start + base).astype(F32)).reshape(B, NP, NB * SUBLANES, LANES)
    n_seg = B * NP * 2 * NB
    max_tiles = -(-((T * NP * n_rep) // TQ + n_seg) // G) * G
    first_row = jnp.arange(max_tiles, dtype=jnp.int32) * TQ
    tile_seg = jnp.minimum(jnp.sum(pend[None, :] <= first_row[:, None], axis=1), n_seg - 1)
    tile_n = (tile_seg % NB).astype(jnp.int32)
    tile_h = ((tile_seg // NB) % 2).astype(jnp.int32)
    tile_p = ((tile_seg // (2 * NB)) % NP).astype(jnp.int32)
    tile_b = (tile_seg // (2 * NB * NP)).astype(jnp.int32)
    n_used = (pend[-1] // TQ).astype(jnp.int32).reshape(1)
    spare = max_tiles * TQ
    n_rows = spare + max(B * NP * MOBA_TOPK * n_rep * BLK, G * TQ)

    idx = pl.pallas_call(
        functools.partial(_moba_place_kernel, group=GI, spare=spare),
        grid=(B, NP, NB // GI),
        in_specs=[pl.BlockSpec((1, 1, 2 * SUBLANES, GI * BLK), lambda b, p, i: (b, p, 0, i)),
                  pl.BlockSpec((1, 1, GI * SUBLANES, LANES), lambda b, p, i: (b, p, i, 0))],
        out_specs=pl.BlockSpec((SC_IDX_ROWS, GI * BLK),
                               lambda b, p, i: (0, p * (T // (GI * BLK)) + b * (NB // GI) + i)),
        out_shape=jax.ShapeDtypeStruct((SC_IDX_ROWS, NP * T), jnp.int32),
        compiler_params=_params("parallel", "parallel", "parallel"),
        name="moba_place",
    )(selrank, table)

    qs = _sc_scatter_rows(q6.reshape(NP * T, LANES), idx, n_rep, n_rows)

    live = lambda t, tb, tp, th, tn, nu, sl: (jnp.where(t * G < nu[0], t, max_tiles // G), 0)
    kv_blk = lambda g: pl.BlockSpec(
        (1, BLK, 2 * LANES),
        lambda t, tb, tp, th, tn, nu, sl: (tb[t * G + g], tn[t * G + g], tp[t * G + g]))
    part = pl.pallas_call(
        functools.partial(_moba_group_kernel, group=G),
        grid_spec=pltpu.PrefetchScalarGridSpec(
            num_scalar_prefetch=6,
            grid=(max_tiles // G,),
            in_specs=[pl.BlockSpec((G * TQ, LANES), live)] + [kv_blk(g) for g in range(G)],
            out_specs=pl.BlockSpec((G * TQ, LANES), live)),
        out_shape=jax.ShapeDtypeStruct((n_rows, LANES), F32),
        compiler_params=_params("arbitrary"),
        name="moba_tiles",
    )(tile_b, tile_p, tile_h, tile_n, n_used, slopes, qs, *([kv3] * G))

    pg = _sc_gather_rows(part, idx[:n_rep].reshape(-1)).reshape(n_rep, NP, T, LANES)

    blk = pl.BlockSpec((1, BLK, LANES), lambda b, p, i: (b, i, p))
    return pl.pallas_call(
        _moba_merge_lse_kernel,
        grid=(B, NP, NB),
        in_specs=[blk, blk,
                  pl.BlockSpec((n_rep, 1, BLK, LANES), lambda b, p, i: (0, p, b * NB + i, 0))],
        out_specs=blk,
        out_shape=jax.ShapeDtypeStruct((B, S, MIX_WIDTH), F32),
        compiler_params=_params("parallel", "parallel", "parallel"),
        name="moba_merge",
    )(own_o, own_lse, pg)


def _memkv_kernel(mem_ref, w_ref, kv_ref):
    kv_ref[0] = jnp.dot(mem_ref[0].astype(BF16), w_ref[...].astype(BF16),
                        preferred_element_type=F32).astype(BF16)


def _memkv(mem, w_kv):
    B, M, D = mem.shape
    N = w_kv.shape[1]
    return pl.pallas_call(
        _memkv_kernel,
        grid=(B,),
        in_specs=[pl.BlockSpec((1, M, D), lambda b: (b, 0, 0)),
                  pl.BlockSpec((D, N), lambda b: (0, 0))],
        out_specs=pl.BlockSpec((1, M, N), lambda b: (b, 0, 0)),
        out_shape=jax.ShapeDtypeStruct((B, M, N), BF16),
        compiler_params=_params("parallel"),
        name="memkv",
    )(mem, w_kv)


def _layer_norm(z, g, b):
    mu = jnp.mean(z, axis=-1, keepdims=True)
    zc = z - mu
    var = jnp.mean(zc * zc, axis=-1, keepdims=True)
    return zc * lax.rsqrt(var + LN_EPS) * g + b


def _post_kernel(x_ref, mix_ref, mq_ref, kv_ref, wo_ref, g_ref, b_ref, wr_ref, br_ref,
                 x1_ref, x1s_ref, idx_ref, gate_ref, *, alpha):
    tm = x_ref.shape[0]
    mq = mq_ref[...]
    kv = kv_ref[0]
    km = kv[:, :MEM_WIDTH]
    vm = kv[:, MEM_WIDTH:]
    lane = lax.broadcasted_iota(jnp.int32, (tm, MEM_WIDTH), 1)
    scale = MEM_HEAD_DIM ** -0.5
    mo = jnp.zeros((tm, MEM_WIDTH), F32)
    for hd in range(MEM_HEADS):
        head = (lane >> 6) == hd
        qh = jnp.where(head, mq * scale, 0.0).astype(BF16)
        s = lax.dot_general(qh, km, _NT, preferred_element_type=F32)
        m = jnp.max(s, axis=1, keepdims=True)
        p = jnp.exp(s - m)
        l = jnp.sum(p, axis=1, keepdims=True)
        oh = jnp.dot(p.astype(BF16), vm, preferred_element_type=F32) / l
        mo = jnp.where(head, oh, mo)

    y = jnp.dot(mix_ref[...].astype(BF16), wo_ref[:MIX_WIDTH, :], preferred_element_type=F32)
    y = y + jnp.dot(mo.astype(BF16), wo_ref[MIX_WIDTH:, :], preferred_element_type=F32)
    x1 = _layer_norm(alpha * x_ref[...] + y, g_ref[...], b_ref[...])
    x1_ref[...] = x1
    _store_subrows(x1s_ref, x1)

    x_hi = x1.astype(BF16)
    x_lo = (x1 - x_hi.astype(F32)).astype(BF16)
    logits = (jnp.dot(x_hi, wr_ref[0], preferred_element_type=F32)
              + jnp.dot(x_lo, wr_ref[0], preferred_element_type=F32)
              + jnp.dot(x_hi, wr_ref[1], preferred_element_type=F32) + br_ref[...])
    lane_e = lax.broadcasted_iota(jnp.int32, (tm, LANES), 1)
    g = jnp.where(lane_e < N_EXPERTS, logits, -jnp.inf)
    idx_out = jnp.zeros((tm, LANES), jnp.int32)
    val_out = jnp.full((tm, LANES), -jnp.inf, F32)
    for kk in range(TOP_K):
        mx = jnp.max(g, axis=1, keepdims=True)
        idx = jnp.min(jnp.where(g == mx, lane_e, LANES), axis=1, keepdims=True)
        idx_out = jnp.where(lane_e == kk, idx, idx_out)
        val_out = jnp.where(lane_e == kk, mx, val_out)
        g = jnp.where(lane_e == idx, -jnp.inf, g)
    vmax = jnp.max(val_out, axis=1, keepdims=True)
    ev = jnp.exp(val_out - vmax)
    idx_ref[...] = idx_out
    gate_ref[...] = ev / jnp.sum(ev, axis=1, keepdims=True)


def _post_mixer(x2, mix2, h2, kv, w_o_bf16, ln_g, ln_b, w_router, b_router, seq_len, alpha):
    T, D = x2.shape
    tm = POST_ROWS
    N = h2.shape[1]
    M = kv.shape[1]
    mq_col = (N - MEM_WIDTH) // MEM_WIDTH
    tiles_per_seq = seq_len // tm
    wr = jnp.zeros((D, LANES), F32).at[:, :N_EXPERTS].set(w_router)
    wr_hi = wr.astype(BF16)
    wr = jnp.stack([wr_hi, (wr - wr_hi.astype(F32)).astype(BF16)])
    br = jnp.zeros((1, LANES), F32).at[0, :N_EXPERTS].set(b_router)
    row = lambda n: pl.BlockSpec((tm, n), lambda i: (i, 0))
    full = lambda a, b: pl.BlockSpec((a, b), lambda i: (0, 0))
    return pl.pallas_call(
        functools.partial(_post_kernel, alpha=alpha),
        grid=(T // tm,),
        in_specs=[row(D), row(MIX_WIDTH),
                  pl.BlockSpec((tm, MEM_WIDTH), lambda i: (i, mq_col)),
                  pl.BlockSpec((1, M, 2 * MEM_WIDTH), lambda i: (i // tiles_per_seq, 0, 0)),
                  full(D, D), full(1, D), full(1, D),
                  pl.BlockSpec((2, D, LANES), lambda i: (0, 0, 0)), full(1, LANES)],
        out_specs=[row(D), pl.BlockSpec((tm * (D // LANES), LANES), lambda i: (i, 0)),
                   row(LANES), row(LANES)],
        out_shape=[jax.ShapeDtypeStruct((T, D), F32),
                   jax.ShapeDtypeStruct((T * (D // LANES), LANES), F32),
                   jax.ShapeDtypeStruct((T, LANES), jnp.int32),
                   jax.ShapeDtypeStruct((T, LANES), F32)],
        compiler_params=_params("parallel"),
        name="post_mixer",
    )(x2, mix2, h2, kv, w_o_bf16, ln_g.reshape(1, D), ln_b.reshape(1, D), wr, br)


def _sc_mesh():
    return plsc.VectorSubcoreMesh(core_axis_name="core", subcore_axis_name="subcore")


def _sc_scatter_rows(rows, idx, n_rep, n_out):
    R, W = rows.shape

    @functools.partial(pl.kernel, out_type=jax.ShapeDtypeStruct((n_out, W), rows.dtype),
                       mesh=_sc_mesh(), scratch_types=[])
    def scatter(x_hbm, i_hbm, o_hbm):
        def body(x_vmem, i_vmem):
            for r in range(n_rep):
                pltpu.sync_copy(x_vmem, o_hbm.at[i_vmem.at[r]])

        pltpu.emit_pipeline(
            body, grid=(R // SC_WINDOW,),
            in_specs=[pl.BlockSpec((SC_WINDOW, W), lambda i: (i, 0)),
                      pl.BlockSpec((SC_IDX_ROWS, SC_WINDOW), lambda i: (0, i))],
            out_specs=[], core_axis_name=("core", "subcore"),
            dimension_semantics=(pltpu.PARALLEL,), trace_scopes=False)(x_hbm, i_hbm)

    return scatter(rows, idx)


def _sc_gather_rows(table, idx):
    n = idx.shape[0]
    W = table.shape[1]

    @functools.partial(pl.kernel, out_type=jax.ShapeDtypeStruct((n, W), table.dtype),
                       mesh=_sc_mesh(), scratch_types=[])
    def gather(t_hbm, i_hbm, o_hbm):
        def body(i_vmem, o_vmem):
            pltpu.sync_copy(t_hbm.at[i_vmem.at[0]], o_vmem)

        pltpu.emit_pipeline(
            body, grid=(n // SC_WINDOW,),
            in_specs=[pl.BlockSpec((1, SC_WINDOW), lambda i: (0, i))],
            out_specs=[pl.BlockSpec((SC_WINDOW, W), lambda i: (i, 0))],
            core_axis_name=("core", "subcore"),
            dimension_semantics=(pltpu.PARALLEL,), trace_scopes=False)(i_hbm, o_hbm)

    return gather(table, idx.reshape(1, n))


def _sc_workers():
    info = pltpu.get_tpu_info().sparse_core
    return info.num_cores, info.num_cores * info.num_subcores


def _sc_scatter_slabs(rows, idx, n_rep, n_out):
    R, S, W = rows.shape
    n_cores, n_workers = _sc_workers()
    per_worker = (R // SC_WINDOW) // n_workers
    assert per_worker * n_workers * SC_WINDOW == R

    @functools.partial(pl.kernel, out_type=jax.ShapeDtypeStruct((n_out, S, W), rows.dtype),
                       mesh=_sc_mesh(),
                       scratch_types=[pltpu.VMEM((SC_IDX_ROWS, SC_WINDOW), jnp.int32),
                                      pltpu.VMEM((SC_CHUNK, S, W), rows.dtype)])
    def scatter(x_hbm, i_hbm, o_hbm, ibuf, buf):
        wid = lax.axis_index("subcore") * n_cores + lax.axis_index("core")

        @pl.loop(0, per_worker)
        def _(s):
            first = (wid * per_worker + s) * SC_WINDOW
            pltpu.sync_copy(i_hbm.at[:, pl.ds(first, SC_WINDOW)], ibuf)
            for c in range(SC_WINDOW // SC_CHUNK):
                pltpu.sync_copy(x_hbm.at[pl.ds(first + c * SC_CHUNK, SC_CHUNK)], buf)
                for r in range(n_rep):
                    pltpu.sync_copy(buf, o_hbm.at[ibuf.at[r, pl.ds(c * SC_CHUNK, SC_CHUNK)]])

    return scatter(rows, idx)


def _sc_gather_slabs(table, idx):
    n = idx.shape[0]
    S, W = table.shape[1:]
    n_cores, n_workers = _sc_workers()
    per_worker = (n // SC_WINDOW) // n_workers
    assert per_worker * n_workers * SC_WINDOW == n
    n_chunks = SC_WINDOW // SC_CHUNK

    @functools.partial(pl.kernel, out_type=jax.ShapeDtypeStruct((n, S, W), table.dtype),
                       mesh=_sc_mesh(),
                       scratch_types=[pltpu.VMEM((1, SC_WINDOW), jnp.int32),
                                      pltpu.VMEM((2, SC_CHUNK, S, W), table.dtype),
                                      pltpu.SemaphoreType.DMA((2,)), pltpu.SemaphoreType.DMA((2,))])
    def gather(t_hbm, i_hbm, o_hbm, ibuf, buf, fetch_sem, store_sem):
        wid = lax.axis_index("subcore") * n_cores + lax.axis_index("core")

        @pl.loop(0, per_worker)
        def _(s):
            blk = wid * per_worker + s
            pltpu.sync_copy(i_hbm.at[pl.ds(blk, 1)], ibuf)

            def fetch(c):
                return pltpu.make_async_copy(
                    t_hbm.at[ibuf.at[0, pl.ds(c * SC_CHUNK, SC_CHUNK)]], buf.at[c % 2],
                    fetch_sem.at[c % 2])

            def store(c):
                return pltpu.make_async_copy(
                    buf.at[c % 2], o_hbm.at[pl.ds(blk * SC_WINDOW + c * SC_CHUNK, SC_CHUNK)],
                    store_sem.at[c % 2])

            fetch(0).start()
            for c in range(n_chunks):
                if c + 1 < n_chunks:
                    if c >= 1:
                        store(c - 1).wait()
                    fetch(c + 1).start()
                fetch(c).wait()
                store(c).start()
            store(n_chunks - 2).wait()
            store(n_chunks - 1).wait()

    return gather(table, idx.reshape(n // SC_WINDOW, SC_WINDOW))


def _route(top_idx, n_tokens):
    rb = MOE_ROWS
    tk = n_tokens * TOP_K
    flat_e = top_idx.reshape(-1)
    onehot = (flat_e[:, None] == jnp.arange(N_EXPERTS, dtype=jnp.int32)[None, :]).astype(jnp.int32)
    csum = jnp.cumsum(onehot, axis=0)
    rank = jnp.sum(onehot * csum, axis=1) - 1
    counts = csum[-1]
    padded = (counts + rb - 1) // rb * rb
    pend = jnp.cumsum(padded)
    pstart = pend - padded
    dest = (pstart[flat_e] + rank).astype(jnp.int32).reshape(n_tokens, TOP_K)
    n_blocks = tk // rb + N_EXPERTS
    first_row = jnp.arange(n_blocks, dtype=jnp.int32) * rb
    block_e = jnp.minimum(jnp.sum(pend[None, :] <= first_row[:, None], axis=1),
                          N_EXPERTS - 1).astype(jnp.int32)
    n_used = (pend[-1] // rb).astype(jnp.int32).reshape(1)
    return dest, block_e, n_used


def _dispatch(x1s, dest, n_rows, sub):
    T = dest.shape[0]
    idx = jnp.concatenate([dest.T, jnp.zeros((SC_IDX_ROWS - TOP_K, T), jnp.int32)], axis=0)
    xs = _sc_scatter_slabs(x1s.reshape(T, sub, LANES), idx, TOP_K, n_rows)
    return xs.reshape(n_rows * sub, LANES)


def _expert_kernel(be_ref, nu_ref, x_ref, wg_ref, bg_ref, wu_ref, bu_ref, wd_ref, bd_ref,
                   y_ref, wgb, wub, wdb):
    i = pl.program_id(0)
    prev = be_ref[jnp.maximum(i - 1, 0)]

    @pl.when((i == 0) | (be_ref[i] != prev))
    def _():
        wgb[...] = wg_ref[0, 0].astype(BF16)
        wub[...] = wu_ref[0, 0].astype(BF16)
        wdb[...] = wd_ref[0, 0].astype(BF16)

    @pl.when(i < nu_ref[0])
    def _():
        sub = wgb.shape[0] // LANES
        xb = _load_subrows(x_ref, x_ref.shape[0] // sub, sub).astype(BF16)
        gate = jnp.dot(xb, wgb[...], preferred_element_type=F32) + bg_ref[0, 0]
        gate = jnp.minimum(gate, SWIGLU_LIMIT)
        up = jnp.dot(xb, wub[...], preferred_element_type=F32) + bu_ref[0, 0]
        up = jnp.clip(up, -SWIGLU_LIMIT, SWIGLU_LIMIT)
        hid = gate * _sigmoid(SWIGLU_ALPHA * gate) * (up + 1.0)
        y = jnp.dot(hid.astype(BF16), wdb[...], preferred_element_type=F32) + bd_ref[0, 0]
        _store_subrows(y_ref, y)


def _experts(xs, block_e, n_used, layer, w_gate, b_gate, w_up, b_up, w_down, b_down):
    rb = MOE_ROWS
    n_blocks = block_e.shape[0]
    E, D, F = w_gate.shape[1:]
    sub = D // LANES
    wspec = lambda a, b: pl.BlockSpec((1, 1, a, b), lambda i, be, nu: (layer, be[i], 0, 0))
    live = lambda i, be, nu: (jnp.where(i < nu[0], i, n_blocks), 0)
    grid_spec = pltpu.PrefetchScalarGridSpec(
        num_scalar_prefetch=2,
        grid=(n_blocks,),
        in_specs=[pl.BlockSpec((rb * sub, LANES), live),
                  wspec(D, F), wspec(1, F), wspec(D, F), wspec(1, F), wspec(F, D), wspec(1, D)],
        out_specs=pl.BlockSpec((rb * sub, LANES), live),
        scratch_shapes=[pltpu.VMEM((D, F), BF16), pltpu.VMEM((D, F), BF16),
                        pltpu.VMEM((F, D), BF16)],
    )
    depth = w_gate.shape[0]
    return pl.pallas_call(
        _expert_kernel,
        grid_spec=grid_spec,
        out_shape=jax.ShapeDtypeStruct(xs.shape, F32),
        compiler_params=_params("arbitrary"),
        name="experts",
    )(block_e, n_used, xs, w_gate, b_gate.reshape(depth, E, 1, F),
      w_up, b_up.reshape(depth, E, 1, F), w_down, b_down.reshape(depth, E, 1, D))


def _combine_kernel(x1_ref, gate_ref, y_ref, g_ref, b_ref, o_ref, *, alpha):
    tm, D = x1_ref.shape
    sub = D // LANES
    gates = gate_ref[...]
    f = jnp.zeros(x1_ref.shape, F32)
    for kk in range(TOP_K):
        f = f + gates[:, kk:kk + 1] * _load_subrows(y_ref, tm, sub, kk * sub, TOP_K * sub)
    o_ref[...] = _layer_norm(alpha * x1_ref[...] + f, g_ref[...], b_ref[...])


def _combine(x1, gates, dest, y_rows, ln_g, ln_b, alpha):
    T, D = x1.shape
    sub = D // LANES
    tm = COMBINE_ROWS
    yg = _sc_gather_slabs(y_rows.reshape(-1, sub, LANES), dest.reshape(-1)).reshape(-1, LANES)
    row = lambda n: pl.BlockSpec((tm, n), lambda i: (i, 0))
    full = lambda a, b: pl.BlockSpec((a, b), lambda i: (0, 0))
    return pl.pallas_call(
        functools.partial(_combine_kernel, alpha=alpha),
        grid=(T // tm,),
        in_specs=[row(D), row(LANES), pl.BlockSpec((tm * TOP_K * sub, LANES), lambda i: (i, 0)),
                  full(1, D), full(1, D)],
        out_specs=row(D),
        out_shape=jax.ShapeDtypeStruct((T, D), F32),
        compiler_params=_params("parallel"),
        name="combine",
    )(x1, gates, yg, ln_g.reshape(1, D), ln_b.reshape(1, D))


def kernel(x, mem, w_in_hgrn, hgrn_lb_logits, hgrn_norm_g, w_in_moba, w_mem_kv, w_o,
           ln_mix_g, ln_mix_b, w_router, b_router, w_gate, b_gate, w_up, b_up,
           w_down, b_down, ln_ffn_g, ln_ffn_b):
    B, S, D = x.shape
    T = B * S
    depth = w_o.shape[0]
    alpha = (2 * depth) ** 0.25

    p_lb = jax.nn.softmax(hgrn_lb_logits.astype(F32), axis=0)
    lower_bounds = jnp.cumsum(p_lb, axis=0) - p_lb[0]

    x2 = x.reshape(T, D)
    for layer in range(depth):
        j = layer // 2
        if layer % 2 == 0:
            (h2,) = _inproj(x2, w_in_hgrn[j].astype(BF16), for_moba=False)
            mix = _hgrn_mixer(h2.reshape(B, S, -1), lower_bounds[j], hgrn_norm_g[j])
        else:
            h2, kv2, q6 = _inproj(x2, w_in_moba[j].astype(BF16), for_moba=True)
            mix = _moba_mixer_v2(h2.reshape(B, S, -1), kv2.reshape(B, S, -1), q6)
        kv = _memkv(mem, w_mem_kv[layer])
        x1, x1s, top_idx, gates = _post_mixer(
            x2, mix.reshape(T, MIX_WIDTH), h2, kv, w_o[layer].astype(BF16),
            ln_mix_g[layer], ln_mix_b[layer], w_router[layer], b_router[layer], S, alpha)
        dest, block_e, n_used = _route(top_idx[:, :TOP_K], T)
        xs = _dispatch(x1s, dest, (block_e.shape[0] + 1) * MOE_ROWS, D // LANES)
        y_rows = _experts(xs, block_e, n_used, layer, w_gate, b_gate, w_up, b_up, w_down, b_down)
        x2 = _combine(x1, gates, dest, y_rows, ln_ffn_g[layer], ln_ffn_b[layer], alpha)
    return x2.reshape(B, S, D)
```

```python
import functools
import math

import jax
import jax.numpy as jnp
from jax import lax
from jax.experimental import pallas as pl
from jax.experimental.pallas import tpu as pltpu
from jax.experimental.pallas import tpu_sc as plsc

MIX_WIDTH = 768
MEM_HEADS = 4
MEM_HEAD_DIM = 64
MEM_WIDTH = MEM_HEADS * MEM_HEAD_DIM
HGRN_HEADS = 6
HGRN_DK = 128
MOBA_HEADS = 12
MOBA_HEAD_DIM = 64
MOBA_BLOCK = 256
MOBA_TOPK = 3
N_EXPERTS = 32
TOP_K = 4
SWIGLU_ALPHA = 1.702
SWIGLU_LIMIT = 7.0
LN_EPS = 1e-5
RMS_EPS = 1e-6

LANES = 128
SUBLANES = 8
VMEM_LIMIT_BYTES = 56 * 1024 * 1024

INPROJ_ROWS = 512
HGRN_CHUNK = 64
HGRN_ROWS = 512
POST_ROWS = 256
MOBA_TILE = 256
MOBA_TILE_GROUP = 8
MOBA_PLACE_GROUP = 8
MOBA_PICK_GROUP = 4
MOBA_MERGE_GROUP = 4
MOE_ROWS = 512
COMBINE_ROWS = 256
SC_WINDOW = 128
SC_IDX_ROWS = 8
SC_CHUNK = 32

BF16 = jnp.bfloat16
F32 = jnp.float32
NEG_BIG = -1e30

_NT = (((1,), (1,)), ((), ()))
_TN = (((0,), (0,)), ((), ()))


def _alibi_slope_list(n):
    def pow2(m):
        start = 2.0 ** (-(2.0 ** -(math.log2(m) - 3)))
        return [start ** (i + 1) for i in range(m)]
    if math.log2(n).is_integer():
        return pow2(n)
    c = 2 ** math.floor(math.log2(n))
    return pow2(c) + _alibi_slope_list(2 * c)[0::2][:n - c]


def _sigmoid(x):
    return 1.0 / (1.0 + jnp.exp(-x))


def _params(*sem):
    return pltpu.CompilerParams(dimension_semantics=sem, vmem_limit_bytes=VMEM_LIMIT_BYTES)


def _store_subrows(ref, value, first=0, stride=None):
    sub = value.shape[1] // LANES
    stride = stride or sub
    for c in range(sub):
        ref[pl.ds(first + c, value.shape[0], stride=stride), :] = value[:, c * LANES:(c + 1) * LANES]


def _load_subrows(ref, rows, sub, first=0, stride=None):
    stride = stride or sub
    return jnp.concatenate(
        [ref[pl.ds(first + c, rows, stride=stride), :] for c in range(sub)], axis=1)


def _inproj_kernel(x_ref, w_ref, h_ref, *moba_refs):
    h = jnp.dot(x_ref[...].astype(BF16), w_ref[...], preferred_element_type=F32)
    h_ref[...] = h
    if moba_refs:
        kv_ref, q6_ref = moba_refs
        n_pairs = q6_ref.shape[0]
        for p in range(n_pairs):
            q6_ref[p] = h[:, p * LANES:(p + 1) * LANES]
            for part in range(2):
                col = (1 + part) * MIX_WIDTH + p * LANES
                kv_ref[:, (2 * p + part) * LANES:(2 * p + part + 1) * LANES] = (
                    h[:, col:col + LANES].astype(BF16))


def _inproj(x2, w_bf16, for_moba):
    T, D = x2.shape
    N = w_bf16.shape[1]
    tm = INPROJ_ROWS
    out_shape = [jax.ShapeDtypeStruct((T, N), F32)]
    out_specs = [pl.BlockSpec((tm, N), lambda i: (i, 0))]
    if for_moba:
        NP = MOBA_HEADS // 2
        out_shape += [jax.ShapeDtypeStruct((T, 2 * MIX_WIDTH), BF16),
                      jax.ShapeDtypeStruct((NP, T, LANES), F32)]
        out_specs += [pl.BlockSpec((tm, 2 * MIX_WIDTH), lambda i: (i, 0)),
                      pl.BlockSpec((NP, tm, LANES), lambda i: (0, i, 0))]
    return pl.pallas_call(
        _inproj_kernel,
        grid=(T // tm,),
        in_specs=[pl.BlockSpec((tm, D), lambda i: (i, 0)),
                  pl.BlockSpec((D, N), lambda i: (0, 0))],
        out_specs=out_specs,
        out_shape=out_shape,
        compiler_params=_params("parallel"),
        name="inproj",
    )(x2, w_bf16)


def _cumsum_rows(x, row):
    n = x.shape[0]
    sh = 1
    while sh < n:
        x = x + jnp.where(row >= sh, pltpu.roll(x, sh, 0), 0.0)
        sh *= 2
    return x


def _bcast_row(a, group, r):
    n = a.shape[0]
    a3 = a.reshape(n // group, group, LANES)
    return jnp.broadcast_to(a3[:, r:r + 1, :], a3.shape).reshape(n, LANES)


def _hgrn_chunk(qr, fr, v, gr, lb, ng, e_sum, st_t):
    C = qr.shape[0]
    row = lax.broadcasted_iota(jnp.int32, (C, LANES), 0)
    rr = lax.broadcasted_iota(jnp.int32, (C, C), 0)
    cc = lax.broadcasted_iota(jnp.int32, (C, C), 1)

    q = qr * _sigmoid(qr)
    forget = lb + (1.0 - lb) * _sigmoid(fr)
    k = 1.0 - forget
    G = _cumsum_rows(jnp.log(forget), row)

    parts = []
    for s in range(SUBLANES):
        Gs = _bcast_row(G, SUBLANES, s)
        ks = _bcast_row(k, SUBLANES, s)
        parts.append((q * ks * jnp.exp(jnp.minimum(G - Gs, 0.0))).astype(BF16))
    a_diag = jnp.dot(jnp.concatenate(parts, axis=1), e_sum, preferred_element_type=F32)
    A = jnp.where(((rr >> 3) == (cc >> 3)) & (cc <= rr), a_diag, 0.0)

    m = SUBLANES
    while m < C:
        lg = int(math.log2(m))
        Gr = _bcast_row(G, 2 * m, m - 1)
        second = ((row >> lg) & 1) == 1
        qm = jnp.where(second, q * jnp.exp(jnp.minimum(G - Gr, 0.0)), 0.0)
        km = jnp.where(second, 0.0, k * jnp.exp(jnp.minimum(Gr - G, 0.0)))
        am = lax.dot_general(qm.astype(BF16), km.astype(BF16), _NT, preferred_element_type=F32)
        A = A + jnp.where((rr >> (lg + 1)) == (cc >> (lg + 1)), am, 0.0)
        m *= 2

    vb = v.astype(BF16)
    o = jnp.dot(A.astype(BF16), vb, preferred_element_type=F32)
    o = o + lax.dot_general((q * jnp.exp(G)).astype(BF16), st_t.astype(BF16), _NT,
                            preferred_element_type=F32)
    g_end = G[C - 1:C, :]
    kd = (k * jnp.exp(g_end - G)).astype(BF16)
    st_new = st_t * jnp.exp(g_end) + lax.dot_general(vb, kd, _TN, preferred_element_type=F32)

    ms = jnp.mean(o * o, axis=-1, keepdims=True)
    out = o * lax.rsqrt(ms + RMS_EPS) * ng * _sigmoid(gr)
    return out, st_new


def _hgrn_kernel(q_ref, f_ref, i_ref, g_ref, lb_ref, ng_ref, e_ref, o_ref, st_ref, *, chunk):
    @pl.when(pl.program_id(2) == 0)
    def _():
        st_ref[...] = jnp.zeros_like(st_ref)

    lb = lb_ref[0]
    ng = ng_ref[...]
    e_sum = e_ref[...]
    n_chunks = q_ref.shape[1] // chunk
    for c in range(n_chunks):
        sl = pl.ds(c * chunk, chunk)
        out, st_new = _hgrn_chunk(q_ref[0, sl, :], f_ref[0, sl, :], i_ref[0, sl, :],
                                  g_ref[0, sl, :], lb, ng, e_sum, st_ref[...])
        st_ref[...] = st_new
        o_ref[0, sl, :] = out


def _hgrn_mixer(h3, lb, norm_g):
    B, S, _ = h3.shape
    ts = min(HGRN_ROWS, S)
    C = HGRN_CHUNK
    H = HGRN_HEADS
    e_sum = (jnp.arange(SUBLANES * LANES)[:, None] // LANES == jnp.arange(C)[None, :] % SUBLANES
             ).astype(BF16)
    col = lambda off: pl.BlockSpec((1, ts, LANES), lambda b, h, s, off=off: (b, s, off + h))
    return pl.pallas_call(
        functools.partial(_hgrn_kernel, chunk=C),
        grid=(B, H, S // ts),
        in_specs=[col(0), col(H), col(2 * H), col(3 * H),
                  pl.BlockSpec((1, 1, LANES), lambda b, h, s: (h, 0, 0)),
                  pl.BlockSpec((1, LANES), lambda b, h, s: (0, 0)),
                  pl.BlockSpec((SUBLANES * LANES, C), lambda b, h, s: (0, 0))],
        out_specs=pl.BlockSpec((1, ts, LANES), lambda b, h, s: (b, s, h)),
        out_shape=jax.ShapeDtypeStruct((B, S, MIX_WIDTH), F32),
        scratch_shapes=[pltpu.VMEM((HGRN_DK, HGRN_DK), F32)],
        compiler_params=_params("parallel", "parallel", "arbitrary"),
        name="hgrn",
    )(h3, h3, h3, h3, lb.reshape(H, 1, LANES), norm_g.reshape(1, LANES), e_sum)


def _moba_kernel(sl_ref, q_ref, kf_ref, kb_ref, vb_ref, o_ref, kmean_ref):
    BLK = MOBA_BLOCK
    p = pl.program_id(1)
    i = pl.program_id(2)

    @pl.when(i == 0)
    def _():
        kmean_ref[...] = jnp.zeros_like(kmean_ref)

    qf = q_ref[0]
    own = pl.ds(pl.multiple_of(i * BLK, BLK), BLK)
    k_own = kb_ref[0, own, :]
    v_own = vb_ref[0, own, :]

    lane = lax.broadcasted_iota(jnp.int32, (BLK, LANES), 1)
    rr = lax.broadcasted_iota(jnp.int32, (BLK, BLK), 0)
    cc = lax.broadcasted_iota(jnp.int32, (BLK, BLK), 1)
    rel = (rr - cc).astype(F32)
    scale = MOBA_HEAD_DIM ** -0.5

    outs = []
    for hh in range(2):
        head = (lane < MOBA_HEAD_DIM) if hh == 0 else (lane >= MOBA_HEAD_DIM)
        slope = sl_ref[2 * p + hh]
        qh = jnp.where(head, qf, 0.0)

        gate = lax.dot_general(qh, kmean_ref[...], _NT, precision=lax.Precision.HIGHEST,
                               preferred_element_type=F32)
        past = lane < i
        g = jnp.where(past, gate, -jnp.inf)
        sel = jnp.zeros((BLK, LANES), jnp.bool_)
        for _ in range(MOBA_TOPK):
            mx = jnp.max(g, axis=1, keepdims=True)
            idx = jnp.min(jnp.where(g == mx, lane, LANES), axis=1, keepdims=True)
            pick = lane == idx
            sel = sel | pick
            g = jnp.where(pick, -jnp.inf, g)
        sel_bias = jnp.where(sel & past, 0.0, NEG_BIG)
        q_aug = jnp.concatenate([(qh * scale).astype(BF16), sel_bias.astype(BF16)], axis=1)

        bias = -slope * rel
        s = lax.dot_general((qh * scale).astype(BF16), k_own, _NT, preferred_element_type=F32)
        s = s + jnp.where(cc <= rr, bias, -jnp.inf)
        m0 = jnp.max(s, axis=1, keepdims=True)
        p0 = jnp.exp(s - m0)
        l0 = jnp.sum(p0, axis=1, keepdims=True)
        acc0 = jnp.dot(p0.astype(BF16), v_own, preferred_element_type=F32)

        def body(n, carry):
            m, l, acc = carry
            blk = pl.ds(pl.multiple_of(n * BLK, BLK), BLK)
            onehot = jnp.where(lane == n, 1.0, 0.0).astype(BF16)
            k_aug = jnp.concatenate([kb_ref[0, blk, :], onehot], axis=1)
            sn = lax.dot_general(q_aug, k_aug, _NT, preferred_element_type=F32)
            sn = sn + (bias - slope * ((i - n) * BLK).astype(F32))
            m_new = jnp.maximum(m, jnp.max(sn, axis=1, keepdims=True))
            a = jnp.exp(m - m_new)
            pn = jnp.exp(sn - m_new)
            l = a * l + jnp.sum(pn, axis=1, keepdims=True)
            acc = a * acc + jnp.dot(pn.astype(BF16), vb_ref[0, blk, :],
                                    preferred_element_type=F32)
            return m_new, l, acc

        _, l, acc = lax.fori_loop(0, i, body, (m0, l0, acc0))
        outs.append(acc / l)

    o_ref[0] = jnp.where(lane < MOBA_HEAD_DIM, outs[0], outs[1])
    kmean_ref[pl.ds(i, 1), :] = jnp.mean(kf_ref[0], axis=0, keepdims=True)


def _moba_mixer(h3, hb3):
    B, S, _ = h3.shape
    BLK = MOBA_BLOCK
    NP = MOBA_HEADS // 2
    slopes = jnp.asarray(_alibi_slope_list(MOBA_HEADS), F32)
    grid_spec = pltpu.PrefetchScalarGridSpec(
        num_scalar_prefetch=1,
        grid=(B, NP, S // BLK),
        in_specs=[pl.BlockSpec((1, BLK, LANES), lambda b, p, i, sl: (b, i, p)),
                  pl.BlockSpec((1, BLK, LANES), lambda b, p, i, sl: (b, i, NP + p)),
                  pl.BlockSpec((1, S, LANES), lambda b, p, i, sl: (b, 0, NP + p)),
                  pl.BlockSpec((1, S, LANES), lambda b, p, i, sl: (b, 0, 2 * NP + p))],
        out_specs=pl.BlockSpec((1, BLK, LANES), lambda b, p, i, sl: (b, i, p)),
        scratch_shapes=[pltpu.VMEM((LANES, LANES), F32)],
    )
    return pl.pallas_call(
        _moba_kernel,
        grid_spec=grid_spec,
        out_shape=jax.ShapeDtypeStruct((B, S, MIX_WIDTH), F32),
        compiler_params=_params("parallel", "parallel", "arbitrary"),
        name="moba",
    )(slopes, h3, h3, hb3, hb3)


def _moba_top3(qh, kmean, i, lane):
    gate = lax.dot_general(qh, kmean, _NT, precision=lax.Precision.HIGHEST,
                           preferred_element_type=F32)
    g = jnp.where(lane < i, gate, -jnp.inf)
    picks = []
    for _ in range(MOBA_TOPK):
        mx = jnp.max(g, axis=1, keepdims=True)
        idx = jnp.min(jnp.where(g == mx, lane, LANES), axis=1, keepdims=True)
        picks.append((idx, (mx > -jnp.inf) & (idx < i)))
        g = jnp.where(lane == idx, -jnp.inf, g)
    return picks


def _moba_select_kernel(q_ref, kf_ref, sr_ref, cnt_ref, kmean_ref):
    BLK = MOBA_BLOCK
    i = pl.program_id(2)

    @pl.when(i == 0)
    def _():
        kmean_ref[...] = jnp.zeros_like(kmean_ref)

    qf = q_ref[0]
    lane = lax.broadcasted_iota(jnp.int32, (BLK, LANES), 1)
    rr = lax.broadcasted_iota(jnp.int32, (BLK, BLK), 0)
    cc = lax.broadcasted_iota(jnp.int32, (BLK, BLK), 1)
    before = jnp.where(cc < rr, 1.0, 0.0).astype(BF16)
    out = jnp.zeros((BLK, LANES), jnp.int32)
    counts = []
    for hh in range(2):
        head = (lane >> 6) == hh
        picks = _moba_top3(jnp.where(head, qf, 0.0), kmean_ref[...], i, lane)
        chosen = jnp.zeros((BLK, LANES), F32)
        for idx, valid in picks:
            chosen = chosen + jnp.where((lane == idx) & valid, 1.0, 0.0)
        earlier = jnp.dot(before, chosen.astype(BF16), preferred_element_type=F32)
        for j, (idx, valid) in enumerate(picks):
            rep = hh * MOBA_TOPK + j
            rank = jnp.sum(jnp.where(lane == idx, earlier, 0.0), axis=1, keepdims=True)
            out = jnp.where(lane == rep, jnp.where(valid, idx, -1), out)
            out = jnp.where(lane == SUBLANES + rep, rank.astype(jnp.int32), out)
        total = earlier[BLK - 1:BLK, :] + chosen[BLK - 1:BLK, :]
        counts.append(jnp.broadcast_to(total, (SUBLANES, LANES)))
    sr_ref[0, 0] = out
    cnt_ref[0, 0] = jnp.concatenate(counts, axis=0)
    kmean_ref[pl.ds(i, 1), :] = jnp.mean(kf_ref[0], axis=0, keepdims=True)


def _moba_dest_kernel(sr_ref, tab_ref, idx_ref, *, trash):
    BLK = MOBA_BLOCK
    blk = sr_ref[0, 0]
    tab = tab_ref[0, 0]
    lane = lax.broadcasted_iota(jnp.int32, (BLK, LANES), 1)
    d = jnp.full((BLK, LANES), float(trash), F32)
    for rep in range(2 * MOBA_TOPK):
        hh = rep // MOBA_TOPK
        sel = blk[:, rep:rep + 1]
        rank = blk[:, SUBLANES + rep:SUBLANES + rep + 1]
        start = jnp.sum(jnp.where(lane == sel, tab[hh:hh + 1, :], 0.0), axis=1, keepdims=True)
        dest = jnp.where(sel >= 0, start + rank.astype(F32), float(trash))
        d = jnp.where(lane == rep, dest, d)
    idx_ref[...] = d.T[:SC_IDX_ROWS, :].astype(jnp.int32)


def _moba_tile_kernel(tb_ref, tp_ref, th_ref, tn_ref, nu_ref, sl_ref, q_ref, k_ref, v_ref, o_ref):
    t = pl.program_id(0)

    @pl.when(t < nu_ref[0])
    def _():
        tq = q_ref.shape[0]
        hh = th_ref[t]
        slope = sl_ref[2 * tp_ref[t] + hh]
        lane = lax.broadcasted_iota(jnp.int32, (tq, LANES), 1)
        head = (lane >> 6) == hh
        q = jnp.where(head, q_ref[...] * (MOBA_HEAD_DIM ** -0.5), 0.0).astype(BF16)
        s = lax.dot_general(q, k_ref[0], _NT, preferred_element_type=F32)
        kpos = lax.broadcasted_iota(jnp.int32, s.shape, 1).astype(F32)
        s = s + slope * kpos
        m = jnp.max(s, axis=1, keepdims=True)
        p = jnp.exp(s - m)
        l = jnp.sum(p, axis=1, keepdims=True)
        o = jnp.dot(p.astype(BF16), v_ref[0], preferred_element_type=F32)
        half = lane & (MOBA_HEAD_DIM - 1)
        stats = jnp.where(half == 0, m, jnp.where(half == 1, l, 0.0))
        o_ref[...] = jnp.where(head, o, stats)


def _moba_merge_kernel(sl_ref, q_ref, k_ref, v_ref, sr_ref, pg_ref, o_ref):
    BLK = MOBA_BLOCK
    p = pl.program_id(1)
    i = pl.program_id(2)
    qf = q_ref[0]
    k_own = k_ref[0]
    v_own = v_ref[0]
    blk = sr_ref[0, 0]
    lane = lax.broadcasted_iota(jnp.int32, (BLK, LANES), 1)
    rr = lax.broadcasted_iota(jnp.int32, (BLK, BLK), 0)
    cc = lax.broadcasted_iota(jnp.int32, (BLK, BLK), 1)
    rel = (rr - cc).astype(F32)
    row = lax.broadcasted_iota(jnp.int32, (BLK, 1), 0)
    outs = []
    for hh in range(2):
        head = (lane >> 6) == hh
        slope = sl_ref[2 * p + hh]
        qh = jnp.where(head, qf * (MOBA_HEAD_DIM ** -0.5), 0.0).astype(BF16)
        s = lax.dot_general(qh, k_own, _NT, preferred_element_type=F32)
        s = s + jnp.where(cc <= rr, -slope * rel, -jnp.inf)
        m = jnp.max(s, axis=1, keepdims=True)
        pr = jnp.exp(s - m)
        ms = [m]
        ls = [jnp.sum(pr, axis=1, keepdims=True)]
        accs = [jnp.dot(pr.astype(BF16), v_own, preferred_element_type=F32)]
        stat_lane = MOBA_HEAD_DIM * (1 - hh)
        for j in range(MOBA_TOPK):
            rep = hh * MOBA_TOPK + j
            part = pg_ref[rep, 0]
            sel = blk[:, rep:rep + 1]
            valid = sel >= 0
            shift = -slope * ((i - sel) * BLK + row).astype(F32)
            ms.append(jnp.where(valid, part[:, stat_lane:stat_lane + 1] + shift, -jnp.inf))
            ls.append(jnp.where(valid, part[:, stat_lane + 1:stat_lane + 2], 0.0))
            accs.append(jnp.where(valid, part, 0.0))
        top = functools.reduce(jnp.maximum, ms)
        ws = [jnp.exp(mk - top) for mk in ms]
        den = sum(w * lk for w, lk in zip(ws, ls))
        num = sum(w * ak for w, ak in zip(ws, accs))
        outs.append(num / den)
    o_ref[0] = jnp.where(lane < MOBA_HEAD_DIM, outs[0], outs[1])


def _moba_sparse_mixer(h3, hb3, q6):
    B, S, _ = h3.shape
    T = B * S
    BLK = MOBA_BLOCK
    TQ = MOBA_TILE
    NB = S // BLK
    NP = MOBA_HEADS // 2
    n_rep = 2 * MOBA_TOPK
    slopes = jnp.asarray(_alibi_slope_list(MOBA_HEADS), F32)
    step = lambda *rest: pl.BlockSpec((1, 1) + rest, lambda b, p, i: (b, p, i) + (0,) * (len(rest) - 1))

    selrank, counts = pl.pallas_call(
        _moba_select_kernel,
        grid=(B, NP, NB),
        in_specs=[pl.BlockSpec((1, BLK, LANES), lambda b, p, i: (b, i, p)),
                  pl.BlockSpec((1, BLK, LANES), lambda b, p, i: (b, i, NP + p))],
        out_specs=[step(BLK, LANES), step(2 * SUBLANES, LANES)],
        out_shape=[jax.ShapeDtypeStruct((B, NP, S, LANES), jnp.int32),
                   jax.ShapeDtypeStruct((B, NP, NB * 2 * SUBLANES, LANES), F32)],
        scratch_shapes=[pltpu.VMEM((LANES, LANES), F32)],
        compiler_params=_params("parallel", "parallel", "arbitrary"),
        name="moba_select",
    )(h3, h3)

    cnt = counts.reshape(B, NP, NB, 2, SUBLANES, LANES)[:, :, :, :, 0, :NB].astype(jnp.int32)
    base = jnp.cumsum(cnt, axis=2) - cnt
    total = jnp.sum(cnt, axis=2)
    padded = (total + TQ - 1) // TQ * TQ
    pend = jnp.cumsum(padded.reshape(-1))
    seg_start = (pend - padded.reshape(-1)).reshape(B, NP, 1, 2, NB)
    table = jnp.zeros((B, NP, NB, SUBLANES, LANES), F32).at[:, :, :, :2, :NB].set(
        (seg_start + base).astype(F32)).reshape(B, NP, NB * SUBLANES, LANES)
    n_seg = B * NP * 2 * NB
    max_tiles = (T * NP * n_rep) // TQ + n_seg
    tile_seg = jnp.minimum(
        jnp.searchsorted(pend, jnp.arange(max_tiles, dtype=jnp.int32) * TQ, side='right'),
        n_seg - 1).astype(jnp.int32)
    tile_n = tile_seg % NB
    tile_h = (tile_seg // NB) % 2
    tile_p = (tile_seg // (2 * NB)) % NP
    tile_b = tile_seg // (2 * NB * NP)
    n_used = (pend[-1] // TQ).astype(jnp.int32).reshape(1)
    trash = max_tiles * TQ

    idx = pl.pallas_call(
        functools.partial(_moba_dest_kernel, trash=trash),
        grid=(B, NP, NB),
        in_specs=[step(BLK, LANES), step(SUBLANES, LANES)],
        out_specs=pl.BlockSpec((SC_IDX_ROWS, BLK), lambda b, p, i: (0, p * (T // BLK) + b * NB + i)),
        out_shape=jax.ShapeDtypeStruct((SC_IDX_ROWS, NP * T), jnp.int32),
        compiler_params=_params("parallel", "parallel", "parallel"),
        name="moba_dest",
    )(selrank, table)

    n_rows = (max_tiles + 1) * TQ
    qs = _sc_scatter_rows(q6.reshape(NP * T, LANES), idx, n_rep, n_rows)

    live = lambda t, tb, tp, th, tn, nu, sl: (jnp.where(t < nu[0], t, max_tiles), 0)
    kv_blk = lambda off: pl.BlockSpec(
        (1, BLK, LANES), lambda t, tb, tp, th, tn, nu, sl, off=off: (tb[t], tn[t], off + tp[t]))
    part = pl.pallas_call(
        _moba_tile_kernel,
        grid_spec=pltpu.PrefetchScalarGridSpec(
            num_scalar_prefetch=6,
            grid=(max_tiles,),
            in_specs=[pl.BlockSpec((TQ, LANES), live), kv_blk(NP), kv_blk(2 * NP)],
            out_specs=pl.BlockSpec((TQ, LANES), live)),
        out_shape=jax.ShapeDtypeStruct((n_rows, LANES), F32),
        compiler_params=_params("arbitrary"),
        name="moba_tiles",
    )(tile_b, tile_p, tile_h, tile_n, n_used, slopes, qs, hb3, hb3)

    pg = _sc_gather_rows(part, idx[:n_rep].reshape(-1)).reshape(n_rep, NP, T, LANES)

    own = lambda off: pl.BlockSpec((1, BLK, LANES), lambda b, p, i, sl, off=off: (b, i, off + p))
    return pl.pallas_call(
        _moba_merge_kernel,
        grid_spec=pltpu.PrefetchScalarGridSpec(
            num_scalar_prefetch=1,
            grid=(B, NP, NB),
            in_specs=[own(0), own(NP), own(2 * NP),
                      pl.BlockSpec((1, 1, BLK, LANES), lambda b, p, i, sl: (b, p, i, 0)),
                      pl.BlockSpec((n_rep, 1, BLK, LANES), lambda b, p, i, sl: (0, p, b * NB + i, 0))],
            out_specs=pl.BlockSpec((1, BLK, LANES), lambda b, p, i, sl: (b, i, p))),
        out_shape=jax.ShapeDtypeStruct((B, S, MIX_WIDTH), F32),
        compiler_params=_params("parallel", "parallel", "parallel"),
        name="moba_merge",
    )(slopes, h3, hb3, hb3, selrank, pg)


def _moba_pick_kernel(q_ref, kf_ref, sr_ref, cnt_ref, kmean_ref, *, qblocks):
    BLK = MOBA_BLOCK
    W = qblocks * BLK
    nbp = kmean_ref.shape[0]
    i0 = pl.program_id(2) * qblocks

    @pl.when(pl.program_id(2) == 0)
    def _():
        kmean_ref[...] = jnp.zeros_like(kmean_ref)

    for j in range(qblocks):
        kmean_ref[pl.ds(i0 + j, 1), :] = jnp.mean(kf_ref[0, j * BLK:(j + 1) * BLK, :], axis=0,
                                                  keepdims=True)
    km = kmean_ref[...]
    lane_k = lax.broadcasted_iota(jnp.int32, (nbp, LANES), 1)
    km2 = jnp.concatenate([jnp.where(lane_k < MOBA_HEAD_DIM, km, 0.0),
                           jnp.where(lane_k >= MOBA_HEAD_DIM, km, 0.0)], axis=0)
    gate = lax.dot_general(km2, q_ref[0], _NT, precision=lax.Precision.HIGHEST,
                           preferred_element_type=F32).reshape(2, nbp, W)
    nblk = lax.broadcasted_iota(jnp.int32, (2, nbp, W), 1)
    qi = i0 + (lax.broadcasted_iota(jnp.int32, (2, 1, W), 2) >> int(math.log2(BLK)))
    g = jnp.where(nblk < qi, gate, -jnp.inf)
    picks = []
    for _ in range(MOBA_TOPK):
        mx = jnp.max(g, axis=1, keepdims=True)
        idx = jnp.min(jnp.where(g == mx, nblk, nbp), axis=1, keepdims=True)
        picks.append((idx, (mx > -jnp.inf) & (idx < qi)))
        g = jnp.where(nblk == idx, -jnp.inf, g)
    chosen = jnp.zeros((2, nbp, W), F32)
    for idx, valid in picks:
        chosen = chosen + jnp.where((nblk == idx) & valid, 1.0, 0.0)
    qa = lax.broadcasted_iota(jnp.int32, (W, W), 0)
    qc = lax.broadcasted_iota(jnp.int32, (W, W), 1)
    shift = int(math.log2(BLK))
    before = jnp.where((qa < qc) & ((qa >> shift) == (qc >> shift)), 1.0, 0.0).astype(BF16)
    earlier = jnp.dot(chosen.reshape(2 * nbp, W).astype(BF16), before,
                      preferred_element_type=F32).reshape(2, nbp, W)
    orow = lax.broadcasted_iota(jnp.int32, (2 * SUBLANES, W), 0)
    out = jnp.zeros((2 * SUBLANES, W), jnp.int32)
    for j, (idx, valid) in enumerate(picks):
        rank = jnp.sum(jnp.where(nblk == idx, earlier, 0.0), axis=1, keepdims=True).astype(jnp.int32)
        sel = jnp.where(valid, idx, -1)
        for hh in range(2):
            rep = hh * MOBA_TOPK + j
            out = jnp.where(orow == rep, sel[hh], out)
            out = jnp.where(orow == SUBLANES + rep, rank[hh], out)
    sr_ref[0, 0] = out
    lane_c = lax.broadcasted_iota(jnp.int32, (nbp, LANES), 1)
    for j in range(qblocks):
        per_block = jnp.sum(chosen[:, :, j * BLK:(j + 1) * BLK], axis=2, keepdims=True)
        cnt_ref[0, 0, j * nbp:(j + 1) * nbp, :] = jnp.where(
            lane_c == 0, per_block[0], jnp.where(lane_c == 1, per_block[1], 0.0))


def _moba_place_kernel(sr_ref, tab_ref, idx_ref, *, group, spare):
    BLK = MOBA_BLOCK
    n_rep = 2 * MOBA_TOPK
    b = pl.program_id(0)
    p = pl.program_id(1)
    ig = pl.program_id(2)
    nrow = lax.broadcasted_iota(jnp.int32, (LANES, BLK), 0)
    orow = lax.broadcasted_iota(jnp.int32, (SC_IDX_ROWS, BLK), 0)
    qpos = lax.broadcasted_iota(jnp.int32, (1, BLK), 1)
    for g in range(group):
        blk = sr_ref[0, 0, :, g * BLK:(g + 1) * BLK]
        tab_t = tab_ref[0, 0, g * SUBLANES:(g + 1) * SUBLANES, :].T
        early = jnp.minimum(ig * group + g, MOBA_TOPK - 1)
        out = jnp.zeros((SC_IDX_ROWS, BLK), jnp.int32)
        for rep in range(n_rep):
            hh = rep // MOBA_TOPK
            sel = blk[rep:rep + 1, :]
            rank = blk[SUBLANES + rep:SUBLANES + rep + 1, :]
            start = jnp.sum(jnp.where(nrow == sel, tab_t[:, hh:hh + 1], 0.0), axis=0, keepdims=True)
            unused = spare + (((b * pl.num_programs(1) + p) * MOBA_TOPK + early) * n_rep + rep) * BLK
            dest = jnp.where(sel >= 0, start.astype(jnp.int32) + rank, unused + qpos)
            out = jnp.where(orow == rep, dest, out)
        idx_ref[:, g * BLK:(g + 1) * BLK] = out


def _moba_group_kernel(tb_ref, tp_ref, th_ref, tn_ref, nu_ref, sl_ref, q_ref, *refs, group):
    kv_refs, o_ref = refs[:group], refs[group]
    t = pl.program_id(0)
    tq = q_ref.shape[0] // group

    @pl.when(t * group < nu_ref[0])
    def _():
        lane = lax.broadcasted_iota(jnp.int32, (tq, LANES), 1)
        kpos = lax.broadcasted_iota(jnp.int32, (1, MOBA_BLOCK), 1)
        for g in range(group):
            tt = t * group + g
            hh = th_ref[tt]
            slope = sl_ref[2 * tp_ref[tt] + hh]
            head = (lane >> 6) == hh
            rows = pl.ds(g * tq, tq)
            q = jnp.where(head, q_ref[rows, :] * (MOBA_HEAD_DIM ** -0.5), 0.0).astype(BF16)
            s = lax.dot_general(q, kv_refs[g][0, :, :LANES], _NT, preferred_element_type=F32)
            s = s + slope * (kpos + tn_ref[tt] * MOBA_BLOCK).astype(F32)
            m = jnp.max(s, axis=1, keepdims=True)
            pr = jnp.exp(s - m)
            l = jnp.sum(pr, axis=1, keepdims=True)
            o = jnp.dot(pr.astype(BF16), kv_refs[g][0, :, LANES:], preferred_element_type=F32) / l
            o_ref[rows, :] = jnp.where(head, o, m + jnp.log(l))


def _moba_own_kernel(sl_ref, q_ref, kv_ref, o_ref, lse_ref):
    BLK = MOBA_BLOCK
    p = pl.program_id(1)
    i = pl.program_id(2)
    qf = q_ref[0]
    k_own = kv_ref[0, :, :LANES]
    v_own = kv_ref[0, :, LANES:]
    lane = lax.broadcasted_iota(jnp.int32, (BLK, LANES), 1)
    rr = lax.broadcasted_iota(jnp.int32, (BLK, BLK), 0)
    cc = lax.broadcasted_iota(jnp.int32, (BLK, BLK), 1)
    key_pos = (lax.broadcasted_iota(jnp.int32, (1, BLK), 1) + i * BLK).astype(F32)
    outs, lses = [], []
    for hh in range(2):
        head = (lane >> 6) == hh
        slope = sl_ref[2 * p + hh]
        qh = jnp.where(head, qf * (MOBA_HEAD_DIM ** -0.5), 0.0).astype(BF16)
        s = lax.dot_general(qh, k_own, _NT, preferred_element_type=F32)
        s = jnp.where(cc <= rr, s + slope * key_pos, -jnp.inf)
        m = jnp.max(s, axis=1, keepdims=True)
        pr = jnp.exp(s - m)
        l = jnp.sum(pr, axis=1, keepdims=True)
        lses.append(m + jnp.log(l))
        outs.append(jnp.dot(pr.astype(BF16), v_own, preferred_element_type=F32) / l)
    first = lane < MOBA_HEAD_DIM
    o_ref[0] = jnp.where(first, outs[0], outs[1])
    lse_ref[0] = jnp.where(first, lses[0], lses[1])


def _moba_merge_lse_kernel(o_ref, lse_ref, pg_ref, out_ref):
    BLK = MOBA_BLOCK
    first = lax.broadcasted_iota(jnp.int32, (BLK, LANES), 1) < MOBA_HEAD_DIM
    for h in range(o_ref.shape[1] // BLK):
        i = pl.program_id(2) * (o_ref.shape[1] // BLK) + h
        rows = pl.ds(h * BLK, BLK)
        lses = [lse_ref[0, rows, :]]
        vals = [o_ref[0, rows, :]]
        for j in range(MOBA_TOPK):
            p0 = pg_ref[j, 0, rows, :]
            p1 = pg_ref[MOBA_TOPK + j, 0, rows, :]
            has_block = j < i
            stats = pltpu.roll(jnp.where(first, p1, p0), MOBA_HEAD_DIM, 1)
            lses.append(jnp.where(has_block, stats, -jnp.inf))
            vals.append(jnp.where(has_block, jnp.where(first, p0, p1), 0.0))
        top = functools.reduce(jnp.maximum, lses)
        ws = [jnp.exp(x - top) for x in lses]
        out_ref[0, rows, :] = sum(w * v for w, v in zip(ws, vals)) / sum(ws)


def _moba_mixer_v2(h3, kv3, q6):
    B, S, _ = h3.shape
    T = B * S
    BLK = MOBA_BLOCK
    TQ = MOBA_TILE
    G = MOBA_TILE_GROUP
    NB = S // BLK
    GI = math.gcd(NB, MOBA_PLACE_GROUP)
    nbp = -(-NB // SUBLANES) * SUBLANES
    NP = MOBA_HEADS // 2
    n_rep = 2 * MOBA_TOPK
    slopes = jnp.asarray(_alibi_slope_list(MOBA_HEADS), F32)

    QB = math.gcd(NB, MOBA_PICK_GROUP)
    selrank, counts = pl.pallas_call(
        functools.partial(_moba_pick_kernel, qblocks=QB),
        grid=(B, NP, NB // QB),
        in_specs=[pl.BlockSpec((1, QB * BLK, LANES), lambda b, p, i: (b, i, p)),
                  pl.BlockSpec((1, QB * BLK, LANES), lambda b, p, i: (b, i, NP + p))],
        out_specs=[pl.BlockSpec((1, 1, 2 * SUBLANES, QB * BLK), lambda b, p, i: (b, p, 0, i)),
                   pl.BlockSpec((1, 1, QB * nbp, LANES), lambda b, p, i: (b, p, i, 0))],
        out_shape=[jax.ShapeDtypeStruct((B, NP, 2 * SUBLANES, S), jnp.int32),
                   jax.ShapeDtypeStruct((B, NP, NB * nbp, LANES), F32)],
        scratch_shapes=[pltpu.VMEM((nbp, LANES), F32)],
        compiler_params=_params("parallel", "parallel", "arbitrary"),
        name="moba_pick",
    )(h3, h3)

    blk_spec = lambda w: pl.BlockSpec((1, BLK, w), lambda b, p, i, sl: (b, i, p))
    own_o, own_lse = pl.pallas_call(
        _moba_own_kernel,
        grid_spec=pltpu.PrefetchScalarGridSpec(
            num_scalar_prefetch=1,
            grid=(B, NP, NB),
            in_specs=[blk_spec(LANES), blk_spec(2 * LANES)],
            out_specs=[blk_spec(LANES), blk_spec(LANES)]),
        out_shape=[jax.ShapeDtypeStruct((B, S, MIX_WIDTH), F32)] * 2,
        compiler_params=_params("parallel", "parallel", "parallel"),
        name="moba_own",
    )(slopes, h3, kv3)

    cnt = counts.reshape(B, NP, NB, nbp, LANES)[:, :, :, :NB, :2].astype(jnp.int32)
    cnt = cnt.transpose(0, 1, 2, 4, 3)
    base = jnp.cumsum(cnt, axis=2) - cnt
    total = jnp.sum(cnt, axis=2)
    padded = (total + TQ - 1) // TQ * TQ
    pend = jnp.cumsum(padded.reshape(-1))
    seg_start = (pend - padded.reshape(-1)).reshape(B, NP, 1, 2, NB)
    table = jnp.zeros((B, NP, NB, SUBLANES, LANES), F32).at[:, :, :, :2, :NB].set(
        (seg_start + base).astype(F32)).reshape(B, NP, NB * SUBLANES, LANES)
    n_seg = B * NP * 2 * NB
    max_tiles = -(-((T * NP * n_rep) // TQ + n_seg) // G) * G
    first_row = jnp.arange(max_tiles, dtype=jnp.int32) * TQ
    tile_seg = jnp.minimum(jnp.sum(pend[None, :] <= first_row[:, None], axis=1), n_seg - 1)
    tile_n = (tile_seg % NB).astype(jnp.int32)
    tile_h = ((tile_seg // NB) % 2).astype(jnp.int32)
    tile_p = ((tile_seg // (2 * NB)) % NP).astype(jnp.int32)
    tile_b = (tile_seg // (2 * NB * NP)).astype(jnp.int32)
    n_used = (pend[-1] // TQ).astype(jnp.int32).reshape(1)
    spare = max_tiles * TQ
    n_rows = spare + max(B * NP * MOBA_TOPK * n_rep * BLK, G * TQ)

    idx = pl.pallas_call(
        functools.partial(_moba_place_kernel, group=GI, spare=spare),
        grid=(B, NP, NB // GI),
        in_specs=[pl.BlockSpec((1, 1, 2 * SUBLANES, GI * BLK), lambda b, p, i: (b, p, 0, i)),
                  pl.BlockSpec((1, 1, GI * SUBLANES, LANES), lambda b, p, i: (b, p, i, 0))],
        out_specs=pl.BlockSpec((SC_IDX_ROWS, GI * BLK),
                               lambda b, p, i: (0, p * (T // (GI * BLK)) + b * (NB // GI) + i)),
        out_shape=jax.ShapeDtypeStruct((SC_IDX_ROWS, NP * T), jnp.int32),
        compiler_params=_params("parallel", "parallel", "parallel"),
        name="moba_place",
    )(selrank, table)

    qs = _sc_scatter_rows(q6.reshape(NP * T, LANES), idx, n_rep, n_rows)

    live = lambda t, tb, tp, th, tn, nu, sl: (jnp.where(t * G < nu[0], t, max_tiles // G), 0)
    kv_blk = lambda g: pl.BlockSpec(
        (1, BLK, 2 * LANES),
        lambda t, tb, tp, th, tn, nu, sl: (tb[t * G + g], tn[t * G + g], tp[t * G + g]))
    part = pl.pallas_call(
        functools.partial(_moba_group_kernel, group=G),
        grid_spec=pltpu.PrefetchScalarGridSpec(
            num_scalar_prefetch=6,
            grid=(max_tiles // G,),
            in_specs=[pl.BlockSpec((G * TQ, LANES), live)] + [kv_blk(g) for g in range(G)],
            out_specs=pl.BlockSpec((G * TQ, LANES), live)),
        out_shape=jax.ShapeDtypeStruct((n_rows, LANES), F32),
        compiler_params=_params("arbitrary"),
        name="moba_tiles",
    )(tile_b, tile_p, tile_h, tile_n, n_used, slopes, qs, *([kv3] * G))

    pg = _sc_gather_rows(part, idx[:n_rep].reshape(-1)).reshape(n_rep, NP, T, LANES)

    MB = math.gcd(NB, MOBA_MERGE_GROUP)
    blk = pl.BlockSpec((1, MB * BLK, LANES), lambda b, p, i: (b, i, p))
    return pl.pallas_call(
        _moba_merge_lse_kernel,
        grid=(B, NP, NB // MB),
        in_specs=[blk, blk,
                  pl.BlockSpec((n_rep, 1, MB * BLK, LANES),
                               lambda b, p, i: (0, p, b * (NB // MB) + i, 0))],
        out_specs=blk,
        out_shape=jax.ShapeDtypeStruct((B, S, MIX_WIDTH), F32),
        compiler_params=_params("parallel", "parallel", "parallel"),
        name="moba_merge",
    )(own_o, own_lse, pg)


def _memkv_kernel(mem_ref, w_ref, kv_ref):
    kv_ref[0] = jnp.dot(mem_ref[0].astype(BF16), w_ref[...].astype(BF16),
                        preferred_element_type=F32).astype(BF16)


def _memkv(mem, w_kv):
    B, M, D = mem.shape
    N = w_kv.shape[1]
    return pl.pallas_call(
        _memkv_kernel,
        grid=(B,),
        in_specs=[pl.BlockSpec((1, M, D), lambda b: (b, 0, 0)),
                  pl.BlockSpec((D, N), lambda b: (0, 0))],
        out_specs=pl.BlockSpec((1, M, N), lambda b: (b, 0, 0)),
        out_shape=jax.ShapeDtypeStruct((B, M, N), BF16),
        compiler_params=_params("parallel"),
        name="memkv",
    )(mem, w_kv)


def _layer_norm(z, g, b):
    mu = jnp.mean(z, axis=-1, keepdims=True)
    zc = z - mu
    var = jnp.mean(zc * zc, axis=-1, keepdims=True)
    return zc * lax.rsqrt(var + LN_EPS) * g + b


def _post_kernel(x_ref, mix_ref, mq_ref, kv_ref, wo_ref, g_ref, b_ref, wr_ref, br_ref,
                 x1_ref, x1s_ref, idx_ref, gate_ref, *, alpha):
    tm = x_ref.shape[0]
    mq = mq_ref[...]
    kv = kv_ref[0]
    km = kv[:, :MEM_WIDTH]
    vm = kv[:, MEM_WIDTH:]
    lane = lax.broadcasted_iota(jnp.int32, (tm, MEM_WIDTH), 1)
    scale = MEM_HEAD_DIM ** -0.5
    mo = jnp.zeros((tm, MEM_WIDTH), F32)
    for hd in range(MEM_HEADS):
        head = (lane >> 6) == hd
        qh = jnp.where(head, mq * scale, 0.0).astype(BF16)
        s = lax.dot_general(qh, km, _NT, preferred_element_type=F32)
        m = jnp.max(s, axis=1, keepdims=True)
        p = jnp.exp(s - m)
        l = jnp.sum(p, axis=1, keepdims=True)
        oh = jnp.dot(p.astype(BF16), vm, preferred_element_type=F32) / l
        mo = jnp.where(head, oh, mo)

    y = jnp.dot(mix_ref[...].astype(BF16), wo_ref[:MIX_WIDTH, :], preferred_element_type=F32)
    y = y + jnp.dot(mo.astype(BF16), wo_ref[MIX_WIDTH:, :], preferred_element_type=F32)
    x1 = _layer_norm(alpha * x_ref[...] + y, g_ref[...], b_ref[...])
    x1_ref[...] = x1
    _store_subrows(x1s_ref, x1)

    x_hi = x1.astype(BF16)
    x_lo = (x1 - x_hi.astype(F32)).astype(BF16)
    hi = jnp.dot(x_hi, wr_ref[...], preferred_element_type=F32)
    lo = jnp.dot(x_lo, wr_ref[:, :LANES], preferred_element_type=F32)
    logits = hi[:, :LANES] + hi[:, LANES:] + lo
    g = logits.T[:N_EXPERTS] + br_ref[...]
    erow = lax.broadcasted_iota(jnp.int32, (N_EXPERTS, tm), 0)
    orow = lax.broadcasted_iota(jnp.int32, (SUBLANES, tm), 0)
    idx_out = jnp.zeros((SUBLANES, tm), jnp.int32)
    vals = []
    for kk in range(TOP_K):
        mx = jnp.max(g, axis=0, keepdims=True)
        idx = jnp.min(jnp.where(g == mx, erow, N_EXPERTS), axis=0, keepdims=True)
        idx_out = jnp.where(orow == kk, idx, idx_out)
        vals.append(mx)
        g = jnp.where(erow == idx, -jnp.inf, g)
    evs = [jnp.exp(v - vals[0]) for v in vals]
    den = sum(evs)
    gate_out = jnp.zeros((SUBLANES, tm), F32)
    for kk in range(TOP_K):
        gate_out = jnp.where(orow == kk, evs[kk] / den, gate_out)
    idx_ref[...] = idx_out
    gate_ref[...] = gate_out


def _post_mixer(x2, mix2, h2, kv, w_o_bf16, ln_g, ln_b, w_router, b_router, seq_len, alpha):
    T, D = x2.shape
    tm = POST_ROWS
    N = h2.shape[1]
    M = kv.shape[1]
    mq_col = (N - MEM_WIDTH) // MEM_WIDTH
    tiles_per_seq = seq_len // tm
    wr = jnp.zeros((D, LANES), F32).at[:, :N_EXPERTS].set(w_router)
    wr_hi = wr.astype(BF16)
    wr = jnp.concatenate([wr_hi, (wr - wr_hi.astype(F32)).astype(BF16)], axis=1)
    br = b_router.reshape(N_EXPERTS, 1)
    row = lambda n: pl.BlockSpec((tm, n), lambda i: (i, 0))
    full = lambda a, b: pl.BlockSpec((a, b), lambda i: (0, 0))
    per_token = pl.BlockSpec((SUBLANES, tm), lambda i: (0, i))
    return pl.pallas_call(
        functools.partial(_post_kernel, alpha=alpha),
        grid=(T // tm,),
        in_specs=[row(D), row(MIX_WIDTH),
                  pl.BlockSpec((tm, MEM_WIDTH), lambda i: (i, mq_col)),
                  pl.BlockSpec((1, M, 2 * MEM_WIDTH), lambda i: (i // tiles_per_seq, 0, 0)),
                  full(D, D), full(1, D), full(1, D),
                  full(D, 2 * LANES), full(N_EXPERTS, 1)],
        out_specs=[row(D), pl.BlockSpec((tm * (D // LANES), LANES), lambda i: (i, 0)),
                   per_token, per_token],
        out_shape=[jax.ShapeDtypeStruct((T, D), F32),
                   jax.ShapeDtypeStruct((T * (D // LANES), LANES), F32),
                   jax.ShapeDtypeStruct((SUBLANES, T), jnp.int32),
                   jax.ShapeDtypeStruct((SUBLANES, T), F32)],
        compiler_params=_params("parallel"),
        name="post_mixer",
    )(x2, mix2, h2, kv, w_o_bf16, ln_g.reshape(1, D), ln_b.reshape(1, D), wr, br)


def _sc_mesh():
    return plsc.VectorSubcoreMesh(core_axis_name="core", subcore_axis_name="subcore")


def _sc_scatter_rows(rows, idx, n_rep, n_out):
    R, W = rows.shape

    @functools.partial(pl.kernel, out_type=jax.ShapeDtypeStruct((n_out, W), rows.dtype),
                       mesh=_sc_mesh(), scratch_types=[])
    def scatter(x_hbm, i_hbm, o_hbm):
        def body(x_vmem, i_vmem):
            for r in range(n_rep):
                pltpu.sync_copy(x_vmem, o_hbm.at[i_vmem.at[r]])

        pltpu.emit_pipeline(
            body, grid=(R // SC_WINDOW,),
            in_specs=[pl.BlockSpec((SC_WINDOW, W), lambda i: (i, 0)),
                      pl.BlockSpec((SC_IDX_ROWS, SC_WINDOW), lambda i: (0, i))],
            out_specs=[], core_axis_name=("core", "subcore"),
            dimension_semantics=(pltpu.PARALLEL,), trace_scopes=False)(x_hbm, i_hbm)

    return scatter(rows, idx)


def _sc_gather_rows(table, idx):
    n = idx.shape[0]
    W = table.shape[1]

    @functools.partial(pl.kernel, out_type=jax.ShapeDtypeStruct((n, W), table.dtype),
                       mesh=_sc_mesh(), scratch_types=[])
    def gather(t_hbm, i_hbm, o_hbm):
        def body(i_vmem, o_vmem):
            pltpu.sync_copy(t_hbm.at[i_vmem.at[0]], o_vmem)

        pltpu.emit_pipeline(
            body, grid=(n // SC_WINDOW,),
            in_specs=[pl.BlockSpec((1, SC_WINDOW), lambda i: (0, i))],
            out_specs=[pl.BlockSpec((SC_WINDOW, W), lambda i: (i, 0))],
            core_axis_name=("core", "subcore"),
            dimension_semantics=(pltpu.PARALLEL,), trace_scopes=False)(i_hbm, o_hbm)

    return gather(table, idx.reshape(1, n))


def _sc_workers():
    info = pltpu.get_tpu_info().sparse_core
    return info.num_cores, info.num_cores * info.num_subcores


def _sc_scatter_slabs(rows, idx, n_rep, n_out):
    R, S, W = rows.shape
    n_cores, n_workers = _sc_workers()
    per_worker = (R // SC_WINDOW) // n_workers
    assert per_worker * n_workers * SC_WINDOW == R

    @functools.partial(pl.kernel, out_type=jax.ShapeDtypeStruct((n_out, S, W), rows.dtype),
                       mesh=_sc_mesh(),
                       scratch_types=[pltpu.VMEM((SC_IDX_ROWS, SC_WINDOW), jnp.int32),
                                      pltpu.VMEM((SC_CHUNK, S, W), rows.dtype)])
    def scatter(x_hbm, i_hbm, o_hbm, ibuf, buf):
        wid = lax.axis_index("subcore") * n_cores + lax.axis_index("core")

        @pl.loop(0, per_worker)
        def _(s):
            first = (wid * per_worker + s) * SC_WINDOW
            pltpu.sync_copy(i_hbm.at[:, pl.ds(first, SC_WINDOW)], ibuf)
            for c in range(SC_WINDOW // SC_CHUNK):
                pltpu.sync_copy(x_hbm.at[pl.ds(first + c * SC_CHUNK, SC_CHUNK)], buf)
                for r in range(n_rep):
                    pltpu.sync_copy(buf, o_hbm.at[ibuf.at[r, pl.ds(c * SC_CHUNK, SC_CHUNK)]])

    return scatter(rows, idx)


def _sc_gather_slabs(table, idx):
    n = idx.shape[0]
    S, W = table.shape[1:]
    n_cores, n_workers = _sc_workers()
    per_worker = (n // SC_WINDOW) // n_workers
    assert per_worker * n_workers * SC_WINDOW == n
    n_chunks = SC_WINDOW // SC_CHUNK

    @functools.partial(pl.kernel, out_type=jax.ShapeDtypeStruct((n, S, W), table.dtype),
                       mesh=_sc_mesh(),
                       scratch_types=[pltpu.VMEM((1, SC_WINDOW), jnp.int32),
                                      pltpu.VMEM((2, SC_CHUNK, S, W), table.dtype),
                                      pltpu.SemaphoreType.DMA((2,)), pltpu.SemaphoreType.DMA((2,))])
    def gather(t_hbm, i_hbm, o_hbm, ibuf, buf, fetch_sem, store_sem):
        wid = lax.axis_index("subcore") * n_cores + lax.axis_index("core")

        @pl.loop(0, per_worker)
        def _(s):
            blk = wid * per_worker + s
            pltpu.sync_copy(i_hbm.at[pl.ds(blk, 1)], ibuf)

            def fetch(c):
                return pltpu.make_async_copy(
                    t_hbm.at[ibuf.at[0, pl.ds(c * SC_CHUNK, SC_CHUNK)]], buf.at[c % 2],
                    fetch_sem.at[c % 2])

            def store(c):
                return pltpu.make_async_copy(
                    buf.at[c % 2], o_hbm.at[pl.ds(blk * SC_WINDOW + c * SC_CHUNK, SC_CHUNK)],
                    store_sem.at[c % 2])

            fetch(0).start()
            for c in range(n_chunks):
                if c + 1 < n_chunks:
                    if c >= 1:
                        store(c - 1).wait()
                    fetch(c + 1).start()
                fetch(c).wait()
                store(c).start()
            store(n_chunks - 2).wait()
            store(n_chunks - 1).wait()

    return gather(table, idx.reshape(n // SC_WINDOW, SC_WINDOW))


def _route(top_idx, n_tokens):
    rb = MOE_ROWS
    tk = n_tokens * TOP_K
    flat_e = top_idx.reshape(-1)
    onehot = (flat_e[:, None] == jnp.arange(N_EXPERTS, dtype=jnp.int32)[None, :]).astype(jnp.int32)
    csum = jnp.cumsum(onehot, axis=0)
    rank = jnp.sum(onehot * csum, axis=1) - 1
    counts = csum[-1]
    padded = (counts + rb - 1) // rb * rb
    pend = jnp.cumsum(padded)
    pstart = pend - padded
    dest = (pstart[flat_e] + rank).astype(jnp.int32).reshape(n_tokens, TOP_K)
    n_blocks = tk // rb + N_EXPERTS
    first_row = jnp.arange(n_blocks, dtype=jnp.int32) * rb
    block_e = jnp.minimum(jnp.sum(pend[None, :] <= first_row[:, None], axis=1),
                          N_EXPERTS - 1).astype(jnp.int32)
    n_used = (pend[-1] // rb).astype(jnp.int32).reshape(1)
    return dest, block_e, n_used


def _dispatch(x1s, dest, n_rows, sub):
    T = dest.shape[0]
    idx = jnp.concatenate([dest.T, jnp.zeros((SC_IDX_ROWS - TOP_K, T), jnp.int32)], axis=0)
    xs = _sc_scatter_slabs(x1s.reshape(T, sub, LANES), idx, TOP_K, n_rows)
    return xs.reshape(n_rows * sub, LANES)


def _expert_kernel(be_ref, nu_ref, x_ref, wg_ref, bg_ref, wu_ref, bu_ref, wd_ref, bd_ref,
                   y_ref, wgb, wub, wdb):
    i = pl.program_id(0)
    prev = be_ref[jnp.maximum(i - 1, 0)]

    @pl.when((i == 0) | (be_ref[i] != prev))
    def _():
        wgb[...] = wg_ref[0, 0].astype(BF16)
        wub[...] = wu_ref[0, 0].astype(BF16)
        wdb[...] = wd_ref[0, 0].astype(BF16)

    @pl.when(i < nu_ref[0])
    def _():
        sub = wgb.shape[0] // LANES
        xb = _load_subrows(x_ref, x_ref.shape[0] // sub, sub).astype(BF16)
        gate = jnp.dot(xb, wgb[...], preferred_element_type=F32) + bg_ref[0, 0]
        gate = jnp.minimum(gate, SWIGLU_LIMIT)
        up = jnp.dot(xb, wub[...], preferred_element_type=F32) + bu_ref[0, 0]
        up = jnp.clip(up, -SWIGLU_LIMIT, SWIGLU_LIMIT)
        hid = gate * _sigmoid(SWIGLU_ALPHA * gate) * (up + 1.0)
        y = jnp.dot(hid.astype(BF16), wdb[...], preferred_element_type=F32) + bd_ref[0, 0]
        _store_subrows(y_ref, y)


def _experts(xs, block_e, n_used, layer, w_gate, b_gate, w_up, b_up, w_down, b_down):
    rb = MOE_ROWS
    n_blocks = block_e.shape[0]
    E, D, F = w_gate.shape[1:]
    sub = D // LANES
    wspec = lambda a, b: pl.BlockSpec((1, 1, a, b), lambda i, be, nu: (layer, be[i], 0, 0))
    live = lambda i, be, nu: (jnp.where(i < nu[0], i, n_blocks), 0)
    grid_spec = pltpu.PrefetchScalarGridSpec(
        num_scalar_prefetch=2,
        grid=(n_blocks,),
        in_specs=[pl.BlockSpec((rb * sub, LANES), live),
                  wspec(D, F), wspec(1, F), wspec(D, F), wspec(1, F), wspec(F, D), wspec(1, D)],
        out_specs=pl.BlockSpec((rb * sub, LANES), live),
        scratch_shapes=[pltpu.VMEM((D, F), BF16), pltpu.VMEM((D, F), BF16),
                        pltpu.VMEM((F, D), BF16)],
    )
    depth = w_gate.shape[0]
    return pl.pallas_call(
        _expert_kernel,
        grid_spec=grid_spec,
        out_shape=jax.ShapeDtypeStruct(xs.shape, F32),
        compiler_params=_params("arbitrary"),
        name="experts",
    )(block_e, n_used, xs, w_gate, b_gate.reshape(depth, E, 1, F),
      w_up, b_up.reshape(depth, E, 1, F), w_down, b_down.reshape(depth, E, 1, D))


def _combine_kernel(x1_ref, gate_ref, y_ref, g_ref, b_ref, o_ref, *, alpha):
    tm, D = x1_ref.shape
    sub = D // LANES
    gates = gate_ref[...].T
    f = jnp.zeros(x1_ref.shape, F32)
    for kk in range(TOP_K):
        f = f + gates[:, kk:kk + 1] * _load_subrows(y_ref, tm, sub, kk * sub, TOP_K * sub)
    o_ref[...] = _layer_norm(alpha * x1_ref[...] + f, g_ref[...], b_ref[...])


def _combine(x1, gates, dest, y_rows, ln_g, ln_b, alpha):
    T, D = x1.shape
    sub = D // LANES
    tm = COMBINE_ROWS
    yg = _sc_gather_slabs(y_rows.reshape(-1, sub, LANES), dest.reshape(-1)).reshape(-1, LANES)
    row = lambda n: pl.BlockSpec((tm, n), lambda i: (i, 0))
    full = lambda a, b: pl.BlockSpec((a, b), lambda i: (0, 0))
    return pl.pallas_call(
        functools.partial(_combine_kernel, alpha=alpha),
        grid=(T // tm,),
        in_specs=[row(D), pl.BlockSpec((SUBLANES, tm), lambda i: (0, i)),
                  pl.BlockSpec((tm * TOP_K * sub, LANES), lambda i: (i, 0)),
                  full(1, D), full(1, D)],
        out_specs=row(D),
        out_shape=jax.ShapeDtypeStruct((T, D), F32),
        compiler_params=_params("parallel"),
        name="combine",
    )(x1, gates, yg, ln_g.reshape(1, D), ln_b.reshape(1, D))


def kernel(x, mem, w_in_hgrn, hgrn_lb_logits, hgrn_norm_g, w_in_moba, w_mem_kv, w_o,
           ln_mix_g, ln_mix_b, w_router, b_router, w_gate, b_gate, w_up, b_up,
           w_down, b_down, ln_ffn_g, ln_ffn_b):
    B, S, D = x.shape
    T = B * S
    depth = w_o.shape[0]
    alpha = (2 * depth) ** 0.25

    p_lb = jax.nn.softmax(hgrn_lb_logits.astype(F32), axis=0)
    lower_bounds = jnp.cumsum(p_lb, axis=0) - p_lb[0]

    x2 = x.reshape(T, D)
    for layer in range(depth):
        j = layer // 2
        if layer % 2 == 0:
            (h2,) = _inproj(x2, w_in_hgrn[j].astype(BF16), for_moba=False)
            mix = _hgrn_mixer(h2.reshape(B, S, -1), lower_bounds[j], hgrn_norm_g[j])
        else:
            h2, kv2, q6 = _inproj(x2, w_in_moba[j].astype(BF16), for_moba=True)
            mix = _moba_mixer_v2(h2.reshape(B, S, -1), kv2.reshape(B, S, -1), q6)
        kv = _memkv(mem, w_mem_kv[layer])
        x1, x1s, top_idx, gates = _post_mixer(
            x2, mix.reshape(T, MIX_WIDTH), h2, kv, w_o[layer].astype(BF16),
            ln_mix_g[layer], ln_mix_b[layer], w_router[layer], b_router[layer], S, alpha)
        dest, block_e, n_used = _route(top_idx[:TOP_K].T, T)
        xs = _dispatch(x1s, dest, (block_e.shape[0] + 1) * MOE_ROWS, D // LANES)
        y_rows = _experts(xs, block_e, n_used, layer, w_gate, b_gate, w_up, b_up, w_down, b_down)
        x2 = _combine(x1, gates, dest, y_rows, ln_ffn_g[layer], ln_ffn_b[layer], alpha)
    return x2.reshape(B, S, D)
```

```python
import functools
import math

import jax
import jax.numpy as jnp
from jax import lax
from jax.experimental import pallas as pl
from jax.experimental.pallas import tpu as pltpu
from jax.experimental.pallas import tpu_sc as plsc

MIX_WIDTH = 768
MEM_HEADS = 4
MEM_HEAD_DIM = 64
MEM_WIDTH = MEM_HEADS * MEM_HEAD_DIM
HGRN_HEADS = 6
HGRN_DK = 128
MOBA_HEADS = 12
MOBA_HEAD_DIM = 64
MOBA_BLOCK = 256
MOBA_TOPK = 3
N_EXPERTS = 32
TOP_K = 4
SWIGLU_ALPHA = 1.702
SWIGLU_LIMIT = 7.0
LN_EPS = 1e-5
RMS_EPS = 1e-6

LANES = 128
SUBLANES = 8
VMEM_LIMIT_BYTES = 56 * 1024 * 1024

INPROJ_ROWS = 512
HGRN_CHUNK = 64
HGRN_ROWS = 512
POST_ROWS = 256
MOBA_TILE = 256
MOBA_TILE_GROUP = 8
MOBA_PLACE_GROUP = 8
MOBA_PICK_GROUP = 4
MOBA_MERGE_GROUP = 4
MOBA_OWN_GROUP = 4
MOE_ROWS = 512
COMBINE_ROWS = 256
SC_WINDOW = 128
SC_IDX_ROWS = 8
SC_CHUNK = 32

BF16 = jnp.bfloat16
F32 = jnp.float32

_NT = (((1,), (1,)), ((), ()))
_TN = (((0,), (0,)), ((), ()))


def _alibi_slope_list(n):
    def pow2(m):
        start = 2.0 ** (-(2.0 ** -(math.log2(m) - 3)))
        return [start ** (i + 1) for i in range(m)]
    if math.log2(n).is_integer():
        return pow2(n)
    c = 2 ** math.floor(math.log2(n))
    return pow2(c) + _alibi_slope_list(2 * c)[0::2][:n - c]


def _sigmoid(x):
    return 1.0 / (1.0 + jnp.exp(-x))


def _params(*sem):
    return pltpu.CompilerParams(dimension_semantics=sem, vmem_limit_bytes=VMEM_LIMIT_BYTES)


def _store_subrows(ref, value, first=0, stride=None):
    sub = value.shape[1] // LANES
    stride = stride or sub
    for c in range(sub):
        ref[pl.ds(first + c, value.shape[0], stride=stride), :] = value[:, c * LANES:(c + 1) * LANES]


def _load_subrows(ref, rows, sub, first=0, stride=None):
    stride = stride or sub
    return jnp.concatenate(
        [ref[pl.ds(first + c, rows, stride=stride), :] for c in range(sub)], axis=1)


def _inproj_kernel(x_ref, w_ref, h_ref, *moba_refs):
    h = jnp.dot(x_ref[...].astype(BF16), w_ref[...], preferred_element_type=F32)
    h_ref[...] = h
    if moba_refs:
        kv_ref, q6_ref = moba_refs
        n_pairs = q6_ref.shape[0]
        for p in range(n_pairs):
            q6_ref[p] = h[:, p * LANES:(p + 1) * LANES]
            for part in range(2):
                col = (1 + part) * MIX_WIDTH + p * LANES
                kv_ref[:, (2 * p + part) * LANES:(2 * p + part + 1) * LANES] = (
                    h[:, col:col + LANES].astype(BF16))


def _inproj(x2, w_bf16, for_moba):
    T, D = x2.shape
    N = w_bf16.shape[1]
    tm = INPROJ_ROWS
    out_shape = [jax.ShapeDtypeStruct((T, N), F32)]
    out_specs = [pl.BlockSpec((tm, N), lambda i: (i, 0))]
    if for_moba:
        NP = MOBA_HEADS // 2
        out_shape += [jax.ShapeDtypeStruct((T, 2 * MIX_WIDTH), BF16),
                      jax.ShapeDtypeStruct((NP, T, LANES), F32)]
        out_specs += [pl.BlockSpec((tm, 2 * MIX_WIDTH), lambda i: (i, 0)),
                      pl.BlockSpec((NP, tm, LANES), lambda i: (0, i, 0))]
    return pl.pallas_call(
        _inproj_kernel,
        grid=(T // tm,),
        in_specs=[pl.BlockSpec((tm, D), lambda i: (i, 0)),
                  pl.BlockSpec((D, N), lambda i: (0, 0))],
        out_specs=out_specs,
        out_shape=out_shape,
        compiler_params=_params("parallel"),
        name="inproj",
    )(x2, w_bf16)


def _cumsum_rows(x, row):
    n = x.shape[0]
    sh = 1
    while sh < n:
        x = x + jnp.where(row >= sh, pltpu.roll(x, sh, 0), 0.0)
        sh *= 2
    return x


def _bcast_row(a, group, r):
    n = a.shape[0]
    a3 = a.reshape(n // group, group, LANES)
    return jnp.broadcast_to(a3[:, r:r + 1, :], a3.shape).reshape(n, LANES)


def _hgrn_chunk(qr, fr, v, gr, lb, ng, e_sum, st_t):
    C = qr.shape[0]
    row = lax.broadcasted_iota(jnp.int32, (C, LANES), 0)
    rr = lax.broadcasted_iota(jnp.int32, (C, C), 0)
    cc = lax.broadcasted_iota(jnp.int32, (C, C), 1)

    q = qr * _sigmoid(qr)
    forget = lb + (1.0 - lb) * _sigmoid(fr)
    k = 1.0 - forget
    G = _cumsum_rows(jnp.log(forget), row)

    parts = []
    for s in range(SUBLANES):
        Gs = _bcast_row(G, SUBLANES, s)
        ks = _bcast_row(k, SUBLANES, s)
        parts.append((q * ks * jnp.exp(jnp.minimum(G - Gs, 0.0))).astype(BF16))
    a_diag = jnp.dot(jnp.concatenate(parts, axis=1), e_sum, preferred_element_type=F32)
    A = jnp.where(((rr >> 3) == (cc >> 3)) & (cc <= rr), a_diag, 0.0)

    m = SUBLANES
    while m < C:
        lg = int(math.log2(m))
        Gr = _bcast_row(G, 2 * m, m - 1)
        second = ((row >> lg) & 1) == 1
        qm = jnp.where(second, q * jnp.exp(jnp.minimum(G - Gr, 0.0)), 0.0)
        km = jnp.where(second, 0.0, k * jnp.exp(jnp.minimum(Gr - G, 0.0)))
        am = lax.dot_general(qm.astype(BF16), km.astype(BF16), _NT, preferred_element_type=F32)
        A = A + jnp.where((rr >> (lg + 1)) == (cc >> (lg + 1)), am, 0.0)
        m *= 2

    vb = v.astype(BF16)
    o = jnp.dot(A.astype(BF16), vb, preferred_element_type=F32)
    o = o + lax.dot_general((q * jnp.exp(G)).astype(BF16), st_t.astype(BF16), _NT,
                            preferred_element_type=F32)
    g_end = G[C - 1:C, :]
    kd = (k * jnp.exp(g_end - G)).astype(BF16)
    st_new = st_t * jnp.exp(g_end) + lax.dot_general(vb, kd, _TN, preferred_element_type=F32)

    ms = jnp.mean(o * o, axis=-1, keepdims=True)
    out = o * lax.rsqrt(ms + RMS_EPS) * ng * _sigmoid(gr)
    return out, st_new


def _hgrn_kernel(q_ref, f_ref, i_ref, g_ref, lb_ref, ng_ref, e_ref, o_ref, st_ref, *, chunk):
    @pl.when(pl.program_id(2) == 0)
    def _():
        st_ref[...] = jnp.zeros_like(st_ref)

    lb = lb_ref[0]
    ng = ng_ref[...]
    e_sum = e_ref[...]
    n_chunks = q_ref.shape[1] // chunk
    for c in range(n_chunks):
        sl = pl.ds(c * chunk, chunk)
        out, st_new = _hgrn_chunk(q_ref[0, sl, :], f_ref[0, sl, :], i_ref[0, sl, :],
                                  g_ref[0, sl, :], lb, ng, e_sum, st_ref[...])
        st_ref[...] = st_new
        o_ref[0, sl, :] = out


def _hgrn_mixer(h3, lb, norm_g):
    B, S, _ = h3.shape
    ts = min(HGRN_ROWS, S)
    C = HGRN_CHUNK
    H = HGRN_HEADS
    e_sum = (jnp.arange(SUBLANES * LANES)[:, None] // LANES == jnp.arange(C)[None, :] % SUBLANES
             ).astype(BF16)
    col = lambda off: pl.BlockSpec((1, ts, LANES), lambda b, h, s, off=off: (b, s, off + h))
    return pl.pallas_call(
        functools.partial(_hgrn_kernel, chunk=C),
        grid=(B, H, S // ts),
        in_specs=[col(0), col(H), col(2 * H), col(3 * H),
                  pl.BlockSpec((1, 1, LANES), lambda b, h, s: (h, 0, 0)),
                  pl.BlockSpec((1, LANES), lambda b, h, s: (0, 0)),
                  pl.BlockSpec((SUBLANES * LANES, C), lambda b, h, s: (0, 0))],
        out_specs=pl.BlockSpec((1, ts, LANES), lambda b, h, s: (b, s, h)),
        out_shape=jax.ShapeDtypeStruct((B, S, MIX_WIDTH), F32),
        scratch_shapes=[pltpu.VMEM((HGRN_DK, HGRN_DK), F32)],
        compiler_params=_params("parallel", "parallel", "arbitrary"),
        name="hgrn",
    )(h3, h3, h3, h3, lb.reshape(H, 1, LANES), norm_g.reshape(1, LANES), e_sum)


def _moba_pick_kernel(q_ref, kf_ref, sr_ref, cnt_ref, kmean_ref, *, qblocks):
    BLK = MOBA_BLOCK
    W = qblocks * BLK
    nbp = kmean_ref.shape[0]
    i0 = pl.program_id(2) * qblocks

    @pl.when(pl.program_id(2) == 0)
    def _():
        kmean_ref[...] = jnp.zeros_like(kmean_ref)

    for j in range(qblocks):
        kmean_ref[pl.ds(i0 + j, 1), :] = jnp.mean(kf_ref[0, j * BLK:(j + 1) * BLK, :], axis=0,
                                                  keepdims=True)
    km = kmean_ref[...]
    lane_k = lax.broadcasted_iota(jnp.int32, (nbp, LANES), 1)
    km2 = jnp.concatenate([jnp.where(lane_k < MOBA_HEAD_DIM, km, 0.0),
                           jnp.where(lane_k >= MOBA_HEAD_DIM, km, 0.0)], axis=0)
    gate = lax.dot_general(km2, q_ref[0], _NT, precision=lax.Precision.HIGHEST,
                           preferred_element_type=F32).reshape(2, nbp, W)
    nblk = lax.broadcasted_iota(jnp.int32, (2, nbp, W), 1)
    qi = i0 + (lax.broadcasted_iota(jnp.int32, (2, 1, W), 2) >> int(math.log2(BLK)))
    g = jnp.where(nblk < qi, gate, -jnp.inf)
    picks = []
    for _ in range(MOBA_TOPK):
        mx = jnp.max(g, axis=1, keepdims=True)
        idx = jnp.min(jnp.where(g == mx, nblk, nbp), axis=1, keepdims=True)
        picks.append((idx, (mx > -jnp.inf) & (idx < qi)))
        g = jnp.where(nblk == idx, -jnp.inf, g)
    chosen = jnp.zeros((2, nbp, W), F32)
    for idx, valid in picks:
        chosen = chosen + jnp.where((nblk == idx) & valid, 1.0, 0.0)
    qa = lax.broadcasted_iota(jnp.int32, (W, W), 0)
    qc = lax.broadcasted_iota(jnp.int32, (W, W), 1)
    shift = int(math.log2(BLK))
    before = jnp.where((qa < qc) & ((qa >> shift) == (qc >> shift)), 1.0, 0.0).astype(BF16)
    earlier = jnp.dot(chosen.reshape(2 * nbp, W).astype(BF16), before,
                      preferred_element_type=F32).reshape(2, nbp, W)
    orow = lax.broadcasted_iota(jnp.int32, (2 * SUBLANES, W), 0)
    out = jnp.zeros((2 * SUBLANES, W), jnp.int32)
    for j, (idx, valid) in enumerate(picks):
        rank = jnp.sum(jnp.where(nblk == idx, earlier, 0.0), axis=1, keepdims=True).astype(jnp.int32)
        sel = jnp.where(valid, idx, -1)
        for hh in range(2):
            rep = hh * MOBA_TOPK + j
            out = jnp.where(orow == rep, sel[hh], out)
            out = jnp.where(orow == SUBLANES + rep, rank[hh], out)
    sr_ref[0, 0] = out
    lane_c = lax.broadcasted_iota(jnp.int32, (nbp, LANES), 1)
    for j in range(qblocks):
        per_block = jnp.sum(chosen[:, :, j * BLK:(j + 1) * BLK], axis=2, keepdims=True)
        cnt_ref[0, 0, j * nbp:(j + 1) * nbp, :] = jnp.where(
            lane_c == 0, per_block[0], jnp.where(lane_c == 1, per_block[1], 0.0))


def _moba_place_kernel(sr_ref, tab_ref, idx_ref, *, group, spare):
    BLK = MOBA_BLOCK
    n_rep = 2 * MOBA_TOPK
    b = pl.program_id(0)
    p = pl.program_id(1)
    ig = pl.program_id(2)
    nrow = lax.broadcasted_iota(jnp.int32, (LANES, BLK), 0)
    orow = lax.broadcasted_iota(jnp.int32, (SC_IDX_ROWS, BLK), 0)
    qpos = lax.broadcasted_iota(jnp.int32, (1, BLK), 1)
    for g in range(group):
        blk = sr_ref[0, 0, :, g * BLK:(g + 1) * BLK]
        tab_t = tab_ref[0, 0, g * SUBLANES:(g + 1) * SUBLANES, :].T
        early = jnp.minimum(ig * group + g, MOBA_TOPK - 1)
        out = jnp.zeros((SC_IDX_ROWS, BLK), jnp.int32)
        for rep in range(n_rep):
            hh = rep // MOBA_TOPK
            sel = blk[rep:rep + 1, :]
            rank = blk[SUBLANES + rep:SUBLANES + rep + 1, :]
            start = jnp.sum(jnp.where(nrow == sel, tab_t[:, hh:hh + 1], 0.0), axis=0, keepdims=True)
            unused = spare + (((b * pl.num_programs(1) + p) * MOBA_TOPK + early) * n_rep + rep) * BLK
            dest = jnp.where(sel >= 0, start.astype(jnp.int32) + rank, unused + qpos)
            out = jnp.where(orow == rep, dest, out)
        idx_ref[:, g * BLK:(g + 1) * BLK] = out


def _moba_tiles_kernel(tb_ref, tp_ref, th_ref, tn_ref, nu_ref, sl_ref, q_ref, *refs, group):
    kv_refs, o_ref = refs[:group], refs[group]
    t = pl.program_id(0)
    tq = q_ref.shape[0] // group

    @pl.when(t * group < nu_ref[0])
    def _():
        lane = lax.broadcasted_iota(jnp.int32, (tq, LANES), 1)
        kpos = lax.broadcasted_iota(jnp.int32, (1, MOBA_BLOCK), 1)
        for g in range(group):
            tt = t * group + g
            hh = th_ref[tt]
            slope = sl_ref[2 * tp_ref[tt] + hh]
            head = (lane >> 6) == hh
            rows = pl.ds(g * tq, tq)
            q = jnp.where(head, q_ref[rows, :] * (MOBA_HEAD_DIM ** -0.5), 0.0).astype(BF16)
            s = lax.dot_general(q, kv_refs[g][0, :, :LANES], _NT, preferred_element_type=F32)
            s = s + slope * (kpos + tn_ref[tt] * MOBA_BLOCK).astype(F32)
            m = jnp.max(s, axis=1, keepdims=True)
            pr = jnp.exp(s - m)
            l = jnp.sum(pr, axis=1, keepdims=True)
            o = jnp.dot(pr.astype(BF16), kv_refs[g][0, :, LANES:], preferred_element_type=F32) / l
            o_ref[rows, :] = jnp.where(head, o, m + jnp.log(l))


def _moba_own_kernel(sl_ref, q_ref, kv_ref, o_ref, lse_ref):
    BLK = MOBA_BLOCK
    p = pl.program_id(1)
    n_blocks = q_ref.shape[1] // BLK
    lane = lax.broadcasted_iota(jnp.int32, (BLK, LANES), 1)
    rr = lax.broadcasted_iota(jnp.int32, (BLK, BLK), 0)
    cc = lax.broadcasted_iota(jnp.int32, (BLK, BLK), 1)
    first = lane < MOBA_HEAD_DIM
    for h in range(n_blocks):
        i = pl.program_id(2) * n_blocks + h
        rows = pl.ds(h * BLK, BLK)
        qf = q_ref[0, rows, :]
        k_own = kv_ref[0, rows, :LANES]
        v_own = kv_ref[0, rows, LANES:]
        key_pos = (lax.broadcasted_iota(jnp.int32, (1, BLK), 1) + i * BLK).astype(F32)
        outs, lses = [], []
        for hh in range(2):
            head = (lane >> 6) == hh
            slope = sl_ref[2 * p + hh]
            qh = jnp.where(head, qf * (MOBA_HEAD_DIM ** -0.5), 0.0).astype(BF16)
            s = lax.dot_general(qh, k_own, _NT, preferred_element_type=F32)
            s = jnp.where(cc <= rr, s + slope * key_pos, -jnp.inf)
            m = jnp.max(s, axis=1, keepdims=True)
            pr = jnp.exp(s - m)
            l = jnp.sum(pr, axis=1, keepdims=True)
            lses.append(m + jnp.log(l))
            outs.append(jnp.dot(pr.astype(BF16), v_own, preferred_element_type=F32) / l)
        o_ref[0, rows, :] = jnp.where(first, outs[0], outs[1])
        lse_ref[0, rows, :] = jnp.where(first, lses[0], lses[1])


def _moba_merge_kernel(o_ref, lse_ref, pg_ref, out_ref):
    BLK = MOBA_BLOCK
    first = lax.broadcasted_iota(jnp.int32, (BLK, LANES), 1) < MOBA_HEAD_DIM
    for h in range(o_ref.shape[1] // BLK):
        i = pl.program_id(2) * (o_ref.shape[1] // BLK) + h
        rows = pl.ds(h * BLK, BLK)
        lses = [lse_ref[0, rows, :]]
        vals = [o_ref[0, rows, :]]
        for j in range(MOBA_TOPK):
            p0 = pg_ref[j, 0, rows, :]
            p1 = pg_ref[MOBA_TOPK + j, 0, rows, :]
            has_block = j < i
            stats = pltpu.roll(jnp.where(first, p1, p0), MOBA_HEAD_DIM, 1)
            lses.append(jnp.where(has_block, stats, -jnp.inf))
            vals.append(jnp.where(has_block, jnp.where(first, p0, p1), 0.0))
        top = functools.reduce(jnp.maximum, lses)
        ws = [jnp.exp(x - top) for x in lses]
        out_ref[0, rows, :] = sum(w * v for w, v in zip(ws, vals)) / sum(ws)


def _moba_mixer(h3, kv3, q6):
    B, S, _ = h3.shape
    T = B * S
    BLK = MOBA_BLOCK
    TQ = MOBA_TILE
    G = MOBA_TILE_GROUP
    NB = S // BLK
    GI = math.gcd(NB, MOBA_PLACE_GROUP)
    nbp = -(-NB // SUBLANES) * SUBLANES
    NP = MOBA_HEADS // 2
    n_rep = 2 * MOBA_TOPK
    slopes = jnp.asarray(_alibi_slope_list(MOBA_HEADS), F32)

    QB = math.gcd(NB, MOBA_PICK_GROUP)
    selrank, counts = pl.pallas_call(
        functools.partial(_moba_pick_kernel, qblocks=QB),
        grid=(B, NP, NB // QB),
        in_specs=[pl.BlockSpec((1, QB * BLK, LANES), lambda b, p, i: (b, i, p)),
                  pl.BlockSpec((1, QB * BLK, LANES), lambda b, p, i: (b, i, NP + p))],
        out_specs=[pl.BlockSpec((1, 1, 2 * SUBLANES, QB * BLK), lambda b, p, i: (b, p, 0, i)),
                   pl.BlockSpec((1, 1, QB * nbp, LANES), lambda b, p, i: (b, p, i, 0))],
        out_shape=[jax.ShapeDtypeStruct((B, NP, 2 * SUBLANES, S), jnp.int32),
                   jax.ShapeDtypeStruct((B, NP, NB * nbp, LANES), F32)],
        scratch_shapes=[pltpu.VMEM((nbp, LANES), F32)],
        compiler_params=_params("parallel", "parallel", "arbitrary"),
        name="moba_pick",
    )(h3, h3)

    OB = math.gcd(NB, MOBA_OWN_GROUP)
    blk_spec = lambda w: pl.BlockSpec((1, OB * BLK, w), lambda b, p, i, sl: (b, i, p))
    own_o, own_lse = pl.pallas_call(
        _moba_own_kernel,
        grid_spec=pltpu.PrefetchScalarGridSpec(
            num_scalar_prefetch=1,
            grid=(B, NP, NB // OB),
            in_specs=[blk_spec(LANES), blk_spec(2 * LANES)],
            out_specs=[blk_spec(LANES), blk_spec(LANES)]),
        out_shape=[jax.ShapeDtypeStruct((B, S, MIX_WIDTH), F32)] * 2,
        compiler_params=_params("parallel", "parallel", "parallel"),
        name="moba_own",
    )(slopes, h3, kv3)

    cnt = counts.reshape(B, NP, NB, nbp, LANES)[:, :, :, :NB, :2].astype(jnp.int32)
    cnt = cnt.transpose(0, 1, 2, 4, 3)
    base = jnp.cumsum(cnt, axis=2) - cnt
    total = jnp.sum(cnt, axis=2)
    padded = (total + TQ - 1) // TQ * TQ
    pend = jnp.cumsum(padded.reshape(-1))
    seg_start = (pend - padded.reshape(-1)).reshape(B, NP, 1, 2, NB)
    table = jnp.zeros((B, NP, NB, SUBLANES, LANES), F32).at[:, :, :, :2, :NB].set(
        (seg_start + base).astype(F32)).reshape(B, NP, NB * SUBLANES, LANES)
    n_seg = B * NP * 2 * NB
    max_tiles = -(-((T * NP * n_rep) // TQ + n_seg) // G) * G
    first_row = jnp.arange(max_tiles, dtype=jnp.int32) * TQ
    tile_seg = jnp.minimum(jnp.sum(pend[None, :] <= first_row[:, None], axis=1), n_seg - 1)
    tile_n = (tile_seg % NB).astype(jnp.int32)
    tile_h = ((tile_seg // NB) % 2).astype(jnp.int32)
    tile_p = ((tile_seg // (2 * NB)) % NP).astype(jnp.int32)
    tile_b = (tile_seg // (2 * NB * NP)).astype(jnp.int32)
    n_used = (pend[-1] // TQ).astype(jnp.int32).reshape(1)
    spare = max_tiles * TQ
    n_rows = spare + max(B * NP * MOBA_TOPK * n_rep * BLK, G * TQ)

    idx = pl.pallas_call(
        functools.partial(_moba_place_kernel, group=GI, spare=spare),
        grid=(B, NP, NB // GI),
        in_specs=[pl.BlockSpec((1, 1, 2 * SUBLANES, GI * BLK), lambda b, p, i: (b, p, 0, i)),
                  pl.BlockSpec((1, 1, GI * SUBLANES, LANES), lambda b, p, i: (b, p, i, 0))],
        out_specs=pl.BlockSpec((SC_IDX_ROWS, GI * BLK),
                               lambda b, p, i: (0, p * (T // (GI * BLK)) + b * (NB // GI) + i)),
        out_shape=jax.ShapeDtypeStruct((SC_IDX_ROWS, NP * T), jnp.int32),
        compiler_params=_params("parallel", "parallel", "parallel"),
        name="moba_place",
    )(selrank, table)

    qs = _sc_scatter_rows(q6.reshape(NP * T, LANES), idx, n_rep, n_rows)

    live = lambda t, tb, tp, th, tn, nu, sl: (jnp.where(t * G < nu[0], t, max_tiles // G), 0)
    kv_blk = lambda g: pl.BlockSpec(
        (1, BLK, 2 * LANES),
        lambda t, tb, tp, th, tn, nu, sl: (tb[t * G + g], tn[t * G + g], tp[t * G + g]))
    part = pl.pallas_call(
        functools.partial(_moba_tiles_kernel, group=G),
        grid_spec=pltpu.PrefetchScalarGridSpec(
            num_scalar_prefetch=6,
            grid=(max_tiles // G,),
            in_specs=[pl.BlockSpec((G * TQ, LANES), live)] + [kv_blk(g) for g in range(G)],
            out_specs=pl.BlockSpec((G * TQ, LANES), live)),
        out_shape=jax.ShapeDtypeStruct((n_rows, LANES), F32),
        compiler_params=_params("arbitrary"),
        name="moba_tiles",
    )(tile_b, tile_p, tile_h, tile_n, n_used, slopes, qs, *([kv3] * G))

    pg = _sc_gather_rows(part, idx[:n_rep].reshape(-1)).reshape(n_rep, NP, T, LANES)

    MB = math.gcd(NB, MOBA_MERGE_GROUP)
    blk = pl.BlockSpec((1, MB * BLK, LANES), lambda b, p, i: (b, i, p))
    return pl.pallas_call(
        _moba_merge_kernel,
        grid=(B, NP, NB // MB),
        in_specs=[blk, blk,
                  pl.BlockSpec((n_rep, 1, MB * BLK, LANES),
                               lambda b, p, i: (0, p, b * (NB // MB) + i, 0))],
        out_specs=blk,
        out_shape=jax.ShapeDtypeStruct((B, S, MIX_WIDTH), F32),
        compiler_params=_params("parallel", "parallel", "parallel"),
        name="moba_merge",
    )(own_o, own_lse, pg)


def _memkv_kernel(mem_ref, w_ref, kv_ref):
    kv_ref[0] = jnp.dot(mem_ref[0].astype(BF16), w_ref[...].astype(BF16),
                        preferred_element_type=F32).astype(BF16)


def _memkv(mem, w_kv):
    B, M, D = mem.shape
    N = w_kv.shape[1]
    return pl.pallas_call(
        _memkv_kernel,
        grid=(B,),
        in_specs=[pl.BlockSpec((1, M, D), lambda b: (b, 0, 0)),
                  pl.BlockSpec((D, N), lambda b: (0, 0))],
        out_specs=pl.BlockSpec((1, M, N), lambda b: (b, 0, 0)),
        out_shape=jax.ShapeDtypeStruct((B, M, N), BF16),
        compiler_params=_params("parallel"),
        name="memkv",
    )(mem, w_kv)


def _layer_norm(z, g, b):
    mu = jnp.mean(z, axis=-1, keepdims=True)
    zc = z - mu
    var = jnp.mean(zc * zc, axis=-1, keepdims=True)
    return zc * lax.rsqrt(var + LN_EPS) * g + b


def _post_kernel(x_ref, mix_ref, mq_ref, kv_ref, wo_ref, g_ref, b_ref, wr_ref, br_ref,
                 x1_ref, x1s_ref, idx_ref, gate_ref, *, alpha):
    tm = x_ref.shape[0]
    mq = mq_ref[...]
    kv = kv_ref[0]
    km = kv[:, :MEM_WIDTH]
    vm = kv[:, MEM_WIDTH:]
    lane = lax.broadcasted_iota(jnp.int32, (tm, MEM_WIDTH), 1)
    scale = MEM_HEAD_DIM ** -0.5
    mo = jnp.zeros((tm, MEM_WIDTH), F32)
    for hd in range(MEM_HEADS):
        head = (lane >> 6) == hd
        qh = jnp.where(head, mq * scale, 0.0).astype(BF16)
        s = lax.dot_general(qh, km, _NT, preferred_element_type=F32)
        m = jnp.max(s, axis=1, keepdims=True)
        p = jnp.exp(s - m)
        l = jnp.sum(p, axis=1, keepdims=True)
        oh = jnp.dot(p.astype(BF16), vm, preferred_element_type=F32) / l
        mo = jnp.where(head, oh, mo)

    y = jnp.dot(mix_ref[...].astype(BF16), wo_ref[:MIX_WIDTH, :], preferred_element_type=F32)
    y = y + jnp.dot(mo.astype(BF16), wo_ref[MIX_WIDTH:, :], preferred_element_type=F32)
    x1 = _layer_norm(alpha * x_ref[...] + y, g_ref[...], b_ref[...])
    x1_ref[...] = x1
    _store_subrows(x1s_ref, x1)

    x_hi = x1.astype(BF16)
    x_lo = (x1 - x_hi.astype(F32)).astype(BF16)
    hi = jnp.dot(x_hi, wr_ref[...], preferred_element_type=F32)
    lo = jnp.dot(x_lo, wr_ref[:, :LANES], preferred_element_type=F32)
    logits = hi[:, :LANES] + hi[:, LANES:] + lo
    g = logits.T[:N_EXPERTS] + br_ref[...]
    erow = lax.broadcasted_iota(jnp.int32, (N_EXPERTS, tm), 0)
    orow = lax.broadcasted_iota(jnp.int32, (SUBLANES, tm), 0)
    idx_out = jnp.zeros((SUBLANES, tm), jnp.int32)
    vals = []
    for kk in range(TOP_K):
        mx = jnp.max(g, axis=0, keepdims=True)
        idx = jnp.min(jnp.where(g == mx, erow, N_EXPERTS), axis=0, keepdims=True)
        idx_out = jnp.where(orow == kk, idx, idx_out)
        vals.append(mx)
        g = jnp.where(erow == idx, -jnp.inf, g)
    evs = [jnp.exp(v - vals[0]) for v in vals]
    den = sum(evs)
    gate_out = jnp.zeros((SUBLANES, tm), F32)
    for kk in range(TOP_K):
        gate_out = jnp.where(orow == kk, evs[kk] / den, gate_out)
    idx_ref[...] = idx_out
    gate_ref[...] = gate_out


def _post_mixer(x2, mix2, h2, kv, w_o_bf16, ln_g, ln_b, w_router, b_router, seq_len, alpha):
    T, D = x2.shape
    tm = POST_ROWS
    N = h2.shape[1]
    M = kv.shape[1]
    mq_col = (N - MEM_WIDTH) // MEM_WIDTH
    tiles_per_seq = seq_len // tm
    wr = jnp.zeros((D, LANES), F32).at[:, :N_EXPERTS].set(w_router)
    wr_hi = wr.astype(BF16)
    wr = jnp.concatenate([wr_hi, (wr - wr_hi.astype(F32)).astype(BF16)], axis=1)
    br = b_router.reshape(N_EXPERTS, 1)
    row = lambda n: pl.BlockSpec((tm, n), lambda i: (i, 0))
    full = lambda a, b: pl.BlockSpec((a, b), lambda i: (0, 0))
    per_token = pl.BlockSpec((SUBLANES, tm), lambda i: (0, i))
    return pl.pallas_call(
        functools.partial(_post_kernel, alpha=alpha),
        grid=(T // tm,),
        in_specs=[row(D), row(MIX_WIDTH),
                  pl.BlockSpec((tm, MEM_WIDTH), lambda i: (i, mq_col)),
                  pl.BlockSpec((1, M, 2 * MEM_WIDTH), lambda i: (i // tiles_per_seq, 0, 0)),
                  full(D, D), full(1, D), full(1, D),
                  full(D, 2 * LANES), full(N_EXPERTS, 1)],
        out_specs=[row(D), pl.BlockSpec((tm * (D // LANES), LANES), lambda i: (i, 0)),
                   per_token, per_token],
        out_shape=[jax.ShapeDtypeStruct((T, D), F32),
                   jax.ShapeDtypeStruct((T * (D // LANES), LANES), F32),
                   jax.ShapeDtypeStruct((SUBLANES, T), jnp.int32),
                   jax.ShapeDtypeStruct((SUBLANES, T), F32)],
        compiler_params=_params("parallel"),
        name="post_mixer",
    )(x2, mix2, h2, kv, w_o_bf16, ln_g.reshape(1, D), ln_b.reshape(1, D), wr, br)


def _sc_mesh():
    return plsc.VectorSubcoreMesh(core_axis_name="core", subcore_axis_name="subcore")


def _sc_scatter_rows(rows, idx, n_rep, n_out):
    R, W = rows.shape

    @functools.partial(pl.kernel, out_type=jax.ShapeDtypeStruct((n_out, W), rows.dtype),
                       mesh=_sc_mesh(), scratch_types=[])
    def scatter(x_hbm, i_hbm, o_hbm):
        def body(x_vmem, i_vmem):
            for r in range(n_rep):
                pltpu.sync_copy(x_vmem, o_hbm.at[i_vmem.at[r]])

        pltpu.emit_pipeline(
            body, grid=(R // SC_WINDOW,),
            in_specs=[pl.BlockSpec((SC_WINDOW, W), lambda i: (i, 0)),
                      pl.BlockSpec((SC_IDX_ROWS, SC_WINDOW), lambda i: (0, i))],
            out_specs=[], core_axis_name=("core", "subcore"),
            dimension_semantics=(pltpu.PARALLEL,), trace_scopes=False)(x_hbm, i_hbm)

    return scatter(rows, idx)


def _sc_gather_rows(table, idx):
    n = idx.shape[0]
    W = table.shape[1]

    @functools.partial(pl.kernel, out_type=jax.ShapeDtypeStruct((n, W), table.dtype),
                       mesh=_sc_mesh(), scratch_types=[])
    def gather(t_hbm, i_hbm, o_hbm):
        def body(i_vmem, o_vmem):
            pltpu.sync_copy(t_hbm.at[i_vmem.at[0]], o_vmem)

        pltpu.emit_pipeline(
            body, grid=(n // SC_WINDOW,),
            in_specs=[pl.BlockSpec((1, SC_WINDOW), lambda i: (0, i))],
            out_specs=[pl.BlockSpec((SC_WINDOW, W), lambda i: (i, 0))],
            core_axis_name=("core", "subcore"),
            dimension_semantics=(pltpu.PARALLEL,), trace_scopes=False)(i_hbm, o_hbm)

    return gather(table, idx.reshape(1, n))


def _sc_workers():
    info = pltpu.get_tpu_info().sparse_core
    return info.num_cores, info.num_cores * info.num_subcores


def _sc_scatter_slabs(rows, idx, n_rep, n_out):
    R, S, W = rows.shape
    n_cores, n_workers = _sc_workers()
    per_worker = (R // SC_WINDOW) // n_workers
    assert per_worker * n_workers * SC_WINDOW == R

    @functools.partial(pl.kernel, out_type=jax.ShapeDtypeStruct((n_out, S, W), rows.dtype),
                       mesh=_sc_mesh(),
                       scratch_types=[pltpu.VMEM((SC_IDX_ROWS, SC_WINDOW), jnp.int32),
                                      pltpu.VMEM((SC_CHUNK, S, W), rows.dtype)])
    def scatter(x_hbm, i_hbm, o_hbm, ibuf, buf):
        wid = lax.axis_index("subcore") * n_cores + lax.axis_index("core")

        @pl.loop(0, per_worker)
        def _(s):
            first = (wid * per_worker + s) * SC_WINDOW
            pltpu.sync_copy(i_hbm.at[:, pl.ds(first, SC_WINDOW)], ibuf)
            for c in range(SC_WINDOW // SC_CHUNK):
                pltpu.sync_copy(x_hbm.at[pl.ds(first + c * SC_CHUNK, SC_CHUNK)], buf)
                for r in range(n_rep):
                    pltpu.sync_copy(buf, o_hbm.at[ibuf.at[r, pl.ds(c * SC_CHUNK, SC_CHUNK)]])

    return scatter(rows, idx)


def _sc_gather_slabs(table, idx):
    n = idx.shape[0]
    S, W = table.shape[1:]
    n_cores, n_workers = _sc_workers()
    per_worker = (n // SC_WINDOW) // n_workers
    assert per_worker * n_workers * SC_WINDOW == n
    n_chunks = SC_WINDOW // SC_CHUNK

    @functools.partial(pl.kernel, out_type=jax.ShapeDtypeStruct((n, S, W), table.dtype),
                       mesh=_sc_mesh(),
                       scratch_types=[pltpu.VMEM((1, SC_WINDOW), jnp.int32),
                                      pltpu.VMEM((2, SC_CHUNK, S, W), table.dtype),
                                      pltpu.SemaphoreType.DMA((2,)), pltpu.SemaphoreType.DMA((2,))])
    def gather(t_hbm, i_hbm, o_hbm, ibuf, buf, fetch_sem, store_sem):
        wid = lax.axis_index("subcore") * n_cores + lax.axis_index("core")

        @pl.loop(0, per_worker)
        def _(s):
            blk = wid * per_worker + s
            pltpu.sync_copy(i_hbm.at[pl.ds(blk, 1)], ibuf)

            def fetch(c):
                return pltpu.make_async_copy(
                    t_hbm.at[ibuf.at[0, pl.ds(c * SC_CHUNK, SC_CHUNK)]], buf.at[c % 2],
                    fetch_sem.at[c % 2])

            def store(c):
                return pltpu.make_async_copy(
                    buf.at[c % 2], o_hbm.at[pl.ds(blk * SC_WINDOW + c * SC_CHUNK, SC_CHUNK)],
                    store_sem.at[c % 2])

            fetch(0).start()
            for c in range(n_chunks):
                if c + 1 < n_chunks:
                    if c >= 1:
                        store(c - 1).wait()
                    fetch(c + 1).start()
                fetch(c).wait()
                store(c).start()
            store(n_chunks - 2).wait()
            store(n_chunks - 1).wait()

    return gather(table, idx.reshape(n // SC_WINDOW, SC_WINDOW))


def _route(top_idx, n_tokens):
    rb = MOE_ROWS
    tk = n_tokens * TOP_K
    flat_e = top_idx.reshape(-1)
    onehot = (flat_e[:, None] == jnp.arange(N_EXPERTS, dtype=jnp.int32)[None, :]).astype(jnp.int32)
    csum = jnp.cumsum(onehot, axis=0)
    rank = jnp.sum(onehot * csum, axis=1) - 1
    counts = csum[-1]
    padded = (counts + rb - 1) // rb * rb
    pend = jnp.cumsum(padded)
    pstart = pend - padded
    dest = (pstart[flat_e] + rank).astype(jnp.int32).reshape(n_tokens, TOP_K)
    n_blocks = tk // rb + N_EXPERTS
    first_row = jnp.arange(n_blocks, dtype=jnp.int32) * rb
    block_e = jnp.minimum(jnp.sum(pend[None, :] <= first_row[:, None], axis=1),
                          N_EXPERTS - 1).astype(jnp.int32)
    n_used = (pend[-1] // rb).astype(jnp.int32).reshape(1)
    return dest, block_e, n_used


def _dispatch(x1s, dest, n_rows, sub):
    T = dest.shape[0]
    idx = jnp.concatenate([dest.T, jnp.zeros((SC_IDX_ROWS - TOP_K, T), jnp.int32)], axis=0)
    xs = _sc_scatter_slabs(x1s.reshape(T, sub, LANES), idx, TOP_K, n_rows)
    return xs.reshape(n_rows * sub, LANES)


def _expert_kernel(be_ref, nu_ref, x_ref, wg_ref, bg_ref, wu_ref, bu_ref, wd_ref, bd_ref,
                   y_ref, wgb, wub, wdb):
    i = pl.program_id(0)
    prev = be_ref[jnp.maximum(i - 1, 0)]

    @pl.when((i == 0) | (be_ref[i] != prev))
    def _():
        wgb[...] = wg_ref[0, 0].astype(BF16)
        wub[...] = wu_ref[0, 0].astype(BF16)
        wdb[...] = wd_ref[0, 0].astype(BF16)

    @pl.when(i < nu_ref[0])
    def _():
        sub = wgb.shape[0] // LANES
        xb = _load_subrows(x_ref, x_ref.shape[0] // sub, sub).astype(BF16)
        gate = jnp.dot(xb, wgb[...], preferred_element_type=F32) + bg_ref[0, 0]
        gate = jnp.minimum(gate, SWIGLU_LIMIT)
        up = jnp.dot(xb, wub[...], preferred_element_type=F32) + bu_ref[0, 0]
        up = jnp.clip(up, -SWIGLU_LIMIT, SWIGLU_LIMIT)
        hid = gate * _sigmoid(SWIGLU_ALPHA * gate) * (up + 1.0)
        y = jnp.dot(hid.astype(BF16), wdb[...], preferred_element_type=F32) + bd_ref[0, 0]
        _store_subrows(y_ref, y)


def _experts(xs, block_e, n_used, layer, w_gate, b_gate, w_up, b_up, w_down, b_down):
    rb = MOE_ROWS
    n_blocks = block_e.shape[0]
    E, D, F = w_gate.shape[1:]
    sub = D // LANES
    wspec = lambda a, b: pl.BlockSpec((1, 1, a, b), lambda i, be, nu: (layer, be[i], 0, 0))
    live = lambda i, be, nu: (jnp.where(i < nu[0], i, n_blocks), 0)
    grid_spec = pltpu.PrefetchScalarGridSpec(
        num_scalar_prefetch=2,
        grid=(n_blocks,),
        in_specs=[pl.BlockSpec((rb * sub, LANES), live),
                  wspec(D, F), wspec(1, F), wspec(D, F), wspec(1, F), wspec(F, D), wspec(1, D)],
        out_specs=pl.BlockSpec((rb * sub, LANES), live),
        scratch_shapes=[pltpu.VMEM((D, F), BF16), pltpu.VMEM((D, F), BF16),
                        pltpu.VMEM((F, D), BF16)],
    )
    depth = w_gate.shape[0]
    return pl.pallas_call(
        _expert_kernel,
        grid_spec=grid_spec,
        out_shape=jax.ShapeDtypeStruct(xs.shape, F32),
        compiler_params=_params("arbitrary"),
        name="experts",
    )(block_e, n_used, xs, w_gate, b_gate.reshape(depth, E, 1, F),
      w_up, b_up.reshape(depth, E, 1, F), w_down, b_down.reshape(depth, E, 1, D))


def _combine_kernel(x1_ref, gate_ref, y_ref, g_ref, b_ref, o_ref, *, alpha):
    tm, D = x1_ref.shape
    sub = D // LANES
    gates = gate_ref[...].T
    f = jnp.zeros(x1_ref.shape, F32)
    for kk in range(TOP_K):
        f = f + gates[:, kk:kk + 1] * _load_subrows(y_ref, tm, sub, kk * sub, TOP_K * sub)
    o_ref[...] = _layer_norm(alpha * x1_ref[...] + f, g_ref[...], b_ref[...])


def _combine(x1, gates, dest, y_rows, ln_g, ln_b, alpha):
    T, D = x1.shape
    sub = D // LANES
    tm = COMBINE_ROWS
    yg = _sc_gather_slabs(y_rows.reshape(-1, sub, LANES), dest.reshape(-1)).reshape(-1, LANES)
    row = lambda n: pl.BlockSpec((tm, n), lambda i: (i, 0))
    full = lambda a, b: pl.BlockSpec((a, b), lambda i: (0, 0))
    return pl.pallas_call(
        functools.partial(_combine_kernel, alpha=alpha),
        grid=(T // tm,),
        in_specs=[row(D), pl.BlockSpec((SUBLANES, tm), lambda i: (0, i)),
                  pl.BlockSpec((tm * TOP_K * sub, LANES), lambda i: (i, 0)),
                  full(1, D), full(1, D)],
        out_specs=row(D),
        out_shape=jax.ShapeDtypeStruct((T, D), F32),
        compiler_params=_params("parallel"),
        name="combine",
    )(x1, gates, yg, ln_g.reshape(1, D), ln_b.reshape(1, D))


def kernel(x, mem, w_in_hgrn, hgrn_lb_logits, hgrn_norm_g, w_in_moba, w_mem_kv, w_o,
           ln_mix_g, ln_mix_b, w_router, b_router, w_gate, b_gate, w_up, b_up,
           w_down, b_down, ln_ffn_g, ln_ffn_b):
    B, S, D = x.shape
    T = B * S
    depth = w_o.shape[0]
    alpha = (2 * depth) ** 0.25

    p_lb = jax.nn.softmax(hgrn_lb_logits.astype(F32), axis=0)
    lower_bounds = jnp.cumsum(p_lb, axis=0) - p_lb[0]

    x2 = x.reshape(T, D)
    for layer in range(depth):
        j = layer // 2
        if layer % 2 == 0:
            (h2,) = _inproj(x2, w_in_hgrn[j].astype(BF16), for_moba=False)
            mix = _hgrn_mixer(h2.reshape(B, S, -1), lower_bounds[j], hgrn_norm_g[j])
        else:
            h2, kv2, q6 = _inproj(x2, w_in_moba[j].astype(BF16), for_moba=True)
            mix = _moba_mixer(h2.reshape(B, S, -1), kv2.reshape(B, S, -1), q6)
        kv = _memkv(mem, w_mem_kv[layer])
        x1, x1s, top_idx, gates = _post_mixer(
            x2, mix.reshape(T, MIX_WIDTH), h2, kv, w_o[layer].astype(BF16),
            ln_mix_g[layer], ln_mix_b[layer], w_router[layer], b_router[layer], S, alpha)
        dest, block_e, n_used = _route(top_idx[:TOP_K].T, T)
        xs = _dispatch(x1s, dest, (block_e.shape[0] + 1) * MOE_ROWS, D // LANES)
        y_rows = _experts(xs, block_e, n_used, layer, w_gate, b_gate, w_up, b_up, w_down, b_down)
        x2 = _combine(x1, gates, dest, y_rows, ln_ffn_g[layer], ln_ffn_b[layer], alpha)
    return x2.reshape(B, S, D)
```

```python
import functools
import math

import jax
import jax.numpy as jnp
from jax import lax
from jax.experimental import pallas as pl
from jax.experimental.pallas import tpu as pltpu
from jax.experimental.pallas import tpu_sc as plsc

MIX_WIDTH = 768
MEM_HEADS = 4
MEM_HEAD_DIM = 64
MEM_WIDTH = MEM_HEADS * MEM_HEAD_DIM
HGRN_HEADS = 6
HGRN_DK = 128
MOBA_HEADS = 12
MOBA_HEAD_DIM = 64
MOBA_BLOCK = 256
MOBA_TOPK = 3
N_EXPERTS = 32
TOP_K = 4
SWIGLU_ALPHA = 1.702
SWIGLU_LIMIT = 7.0
LN_EPS = 1e-5
RMS_EPS = 1e-6

LANES = 128
SUBLANES = 8
VMEM_LIMIT_BYTES = 56 * 1024 * 1024

INPROJ_ROWS = 512
HGRN_CHUNK = 64
HGRN_ROWS = 512
POST_ROWS = 256
MOBA_TILE = 256
MOBA_TILE_GROUP = 8
MOBA_PLACE_GROUP = 8
MOBA_PICK_GROUP = 4
MOBA_MERGE_GROUP = 4
MOBA_OWN_GROUP = 4
MOE_ROWS = 512
COMBINE_ROWS = 256
SC_WINDOW = 128
SC_IDX_ROWS = 8
SC_CHUNK = 32

BF16 = jnp.bfloat16
F32 = jnp.float32

_NT = (((1,), (1,)), ((), ()))
_TN = (((0,), (0,)), ((), ()))


def _alibi_slope_list(n):
    def pow2(m):
        start = 2.0 ** (-(2.0 ** -(math.log2(m) - 3)))
        return [start ** (i + 1) for i in range(m)]
    if math.log2(n).is_integer():
        return pow2(n)
    c = 2 ** math.floor(math.log2(n))
    return pow2(c) + _alibi_slope_list(2 * c)[0::2][:n - c]


def _sigmoid(x):
    return 1.0 / (1.0 + jnp.exp(-x))


def _params(*sem):
    return pltpu.CompilerParams(dimension_semantics=sem, vmem_limit_bytes=VMEM_LIMIT_BYTES)


def _store_subrows(ref, value, first=0, stride=None):
    sub = value.shape[1] // LANES
    stride = stride or sub
    for c in range(sub):
        ref[pl.ds(first + c, value.shape[0], stride=stride), :] = value[:, c * LANES:(c + 1) * LANES]


def _load_subrows(ref, rows, sub, first=0, stride=None):
    stride = stride or sub
    return jnp.concatenate(
        [ref[pl.ds(first + c, rows, stride=stride), :] for c in range(sub)], axis=1)


def _inproj_kernel(x_ref, w_ref, h_ref, *moba_refs):
    h = jnp.dot(x_ref[...].astype(BF16), w_ref[...], preferred_element_type=F32)
    h_ref[...] = h
    if moba_refs:
        kv_ref, q6_ref = moba_refs
        n_pairs = q6_ref.shape[0]
        for p in range(n_pairs):
            q6_ref[p] = h[:, p * LANES:(p + 1) * LANES]
            for part in range(2):
                col = (1 + part) * MIX_WIDTH + p * LANES
                kv_ref[:, (2 * p + part) * LANES:(2 * p + part + 1) * LANES] = (
                    h[:, col:col + LANES].astype(BF16))


def _inproj(x2, w_bf16, for_moba):
    T, D = x2.shape
    N = w_bf16.shape[1]
    tm = INPROJ_ROWS
    out_shape = [jax.ShapeDtypeStruct((T, N), F32)]
    out_specs = [pl.BlockSpec((tm, N), lambda i: (i, 0))]
    if for_moba:
        NP = MOBA_HEADS // 2
        out_shape += [jax.ShapeDtypeStruct((T, 2 * MIX_WIDTH), BF16),
                      jax.ShapeDtypeStruct((NP, T, LANES), F32)]
        out_specs += [pl.BlockSpec((tm, 2 * MIX_WIDTH), lambda i: (i, 0)),
                      pl.BlockSpec((NP, tm, LANES), lambda i: (0, i, 0))]
    return pl.pallas_call(
        _inproj_kernel,
        grid=(T // tm,),
        in_specs=[pl.BlockSpec((tm, D), lambda i: (i, 0)),
                  pl.BlockSpec((D, N), lambda i: (0, 0))],
        out_specs=out_specs,
        out_shape=out_shape,
        compiler_params=_params("parallel"),
        name="inproj",
    )(x2, w_bf16)


def _cumsum_rows(x, row):
    n = x.shape[0]
    sh = 1
    while sh < n:
        x = x + jnp.where(row >= sh, pltpu.roll(x, sh, 0), 0.0)
        sh *= 2
    return x


def _bcast_row(a, group, r):
    n = a.shape[0]
    a3 = a.reshape(n // group, group, LANES)
    return jnp.broadcast_to(a3[:, r:r + 1, :], a3.shape).reshape(n, LANES)


def _hgrn_chunk(qr, fr, v, gr, lb, ng, e_sum, st_t):
    C = qr.shape[0]
    row = lax.broadcasted_iota(jnp.int32, (C, LANES), 0)
    rr = lax.broadcasted_iota(jnp.int32, (C, C), 0)
    cc = lax.broadcasted_iota(jnp.int32, (C, C), 1)

    q = qr * _sigmoid(qr)
    forget = lb + (1.0 - lb) * _sigmoid(fr)
    k = 1.0 - forget
    G = _cumsum_rows(jnp.log(forget), row)

    z = jnp.log(k) - G
    parts = []
    for s in range(SUBLANES):
        parts.append((q * jnp.exp(jnp.minimum(G + _bcast_row(z, SUBLANES, s), 0.0))).astype(BF16))
    a_diag = jnp.dot(jnp.concatenate(parts, axis=1), e_sum, preferred_element_type=F32)
    A = jnp.where(((rr >> 3) == (cc >> 3)) & (cc <= rr), a_diag, 0.0)

    m = SUBLANES
    while m < C:
        lg = int(math.log2(m))
        Gr = _bcast_row(G, 2 * m, m - 1)
        second = ((row >> lg) & 1) == 1
        qm = q * jnp.exp(jnp.where(second, G - Gr, -jnp.inf))
        km = k * jnp.exp(jnp.where(second, -jnp.inf, Gr - G))
        am = lax.dot_general(qm.astype(BF16), km.astype(BF16), _NT, preferred_element_type=F32)
        A = A + jnp.where((rr >> (lg + 1)) == (cc >> (lg + 1)), am, 0.0)
        m *= 2

    vb = v.astype(BF16)
    o = jnp.dot(A.astype(BF16), vb, preferred_element_type=F32)
    o = o + lax.dot_general((q * jnp.exp(G)).astype(BF16), st_t.astype(BF16), _NT,
                            preferred_element_type=F32)
    g_end = G[C - 1:C, :]
    kd = (k * jnp.exp(g_end - G)).astype(BF16)
    st_new = st_t * jnp.exp(g_end) + lax.dot_general(vb, kd, _TN, preferred_element_type=F32)

    ms = jnp.mean(o * o, axis=-1, keepdims=True)
    out = o * lax.rsqrt(ms + RMS_EPS) * ng * _sigmoid(gr)
    return out, st_new


def _hgrn_kernel(q_ref, f_ref, i_ref, g_ref, lb_ref, ng_ref, e_ref, o_ref, st_ref, *, chunk):
    @pl.when(pl.program_id(2) == 0)
    def _():
        st_ref[...] = jnp.zeros_like(st_ref)

    lb = lb_ref[0]
    ng = ng_ref[...]
    e_sum = e_ref[...]
    n_chunks = q_ref.shape[1] // chunk
    for c in range(n_chunks):
        sl = pl.ds(c * chunk, chunk)
        out, st_new = _hgrn_chunk(q_ref[0, sl, :], f_ref[0, sl, :], i_ref[0, sl, :],
                                  g_ref[0, sl, :], lb, ng, e_sum, st_ref[...])
        st_ref[...] = st_new
        o_ref[0, sl, :] = out


def _hgrn_mixer(h3, lb, norm_g):
    B, S, _ = h3.shape
    ts = min(HGRN_ROWS, S)
    C = HGRN_CHUNK
    H = HGRN_HEADS
    e_sum = (jnp.arange(SUBLANES * LANES)[:, None] // LANES == jnp.arange(C)[None, :] % SUBLANES
             ).astype(BF16)
    col = lambda off: pl.BlockSpec((1, ts, LANES), lambda b, h, s, off=off: (b, s, off + h))
    return pl.pallas_call(
        functools.partial(_hgrn_kernel, chunk=C),
        grid=(B, H, S // ts),
        in_specs=[col(0), col(H), col(2 * H), col(3 * H),
                  pl.BlockSpec((1, 1, LANES), lambda b, h, s: (h, 0, 0)),
                  pl.BlockSpec((1, LANES), lambda b, h, s: (0, 0)),
                  pl.BlockSpec((SUBLANES * LANES, C), lambda b, h, s: (0, 0))],
        out_specs=pl.BlockSpec((1, ts, LANES), lambda b, h, s: (b, s, h)),
        out_shape=jax.ShapeDtypeStruct((B, S, MIX_WIDTH), F32),
        scratch_shapes=[pltpu.VMEM((HGRN_DK, HGRN_DK), F32)],
        compiler_params=_params("parallel", "parallel", "arbitrary"),
        name="hgrn",
    )(h3, h3, h3, h3, lb.reshape(H, 1, LANES), norm_g.reshape(1, LANES), e_sum)


def _moba_pick_kernel(q_ref, kf_ref, sr_ref, cnt_ref, kmean_ref, *, qblocks):
    BLK = MOBA_BLOCK
    W = qblocks * BLK
    nbp = kmean_ref.shape[0]
    i0 = pl.program_id(2) * qblocks

    @pl.when(pl.program_id(2) == 0)
    def _():
        kmean_ref[...] = jnp.zeros_like(kmean_ref)

    for j in range(qblocks):
        kmean_ref[pl.ds(i0 + j, 1), :] = jnp.mean(kf_ref[0, j * BLK:(j + 1) * BLK, :], axis=0,
                                                  keepdims=True)
    km = kmean_ref[...]
    lane_k = lax.broadcasted_iota(jnp.int32, (nbp, LANES), 1)
    km2 = jnp.concatenate([jnp.where(lane_k < MOBA_HEAD_DIM, km, 0.0),
                           jnp.where(lane_k >= MOBA_HEAD_DIM, km, 0.0)], axis=0)
    gate = lax.dot_general(km2, q_ref[0], _NT, precision=lax.Precision.HIGHEST,
                           preferred_element_type=F32).reshape(2, nbp, W)
    nblk = lax.broadcasted_iota(jnp.int32, (2, nbp, W), 1)
    qi = i0 + (lax.broadcasted_iota(jnp.int32, (2, 1, W), 2) >> int(math.log2(BLK)))
    g = jnp.where(nblk < qi, gate, -jnp.inf)
    picks = []
    for _ in range(MOBA_TOPK):
        mx = jnp.max(g, axis=1, keepdims=True)
        idx = jnp.min(jnp.where(g == mx, nblk, nbp), axis=1, keepdims=True)
        picks.append((idx, (mx > -jnp.inf) & (idx < qi)))
        g = jnp.where(nblk == idx, -jnp.inf, g)
    chosen = jnp.zeros((2, nbp, W), F32)
    for idx, valid in picks:
        chosen = chosen + jnp.where((nblk == idx) & valid, 1.0, 0.0)
    qa = lax.broadcasted_iota(jnp.int32, (W, W), 0)
    qc = lax.broadcasted_iota(jnp.int32, (W, W), 1)
    shift = int(math.log2(BLK))
    before = jnp.where((qa < qc) & ((qa >> shift) == (qc >> shift)), 1.0, 0.0).astype(BF16)
    earlier = jnp.dot(chosen.reshape(2 * nbp, W).astype(BF16), before,
                      preferred_element_type=F32).reshape(2, nbp, W)
    orow = lax.broadcasted_iota(jnp.int32, (2 * SUBLANES, W), 0)
    out = jnp.zeros((2 * SUBLANES, W), jnp.int32)
    for j, (idx, valid) in enumerate(picks):
        rank = jnp.sum(jnp.where(nblk == idx, earlier, 0.0), axis=1, keepdims=True).astype(jnp.int32)
        sel = jnp.where(valid, idx, -1)
        for hh in range(2):
            rep = hh * MOBA_TOPK + j
            out = jnp.where(orow == rep, sel[hh], out)
            out = jnp.where(orow == SUBLANES + rep, rank[hh], out)
    sr_ref[0, 0] = out
    lane_c = lax.broadcasted_iota(jnp.int32, (nbp, LANES), 1)
    for j in range(qblocks):
        per_block = jnp.sum(chosen[:, :, j * BLK:(j + 1) * BLK], axis=2, keepdims=True)
        cnt_ref[0, 0, j * nbp:(j + 1) * nbp, :] = jnp.where(
            lane_c == 0, per_block[0], jnp.where(lane_c == 1, per_block[1], 0.0))


def _moba_place_kernel(sr_ref, tab_ref, idx_ref, *, group, spare):
    BLK = MOBA_BLOCK
    n_rep = 2 * MOBA_TOPK
    b = pl.program_id(0)
    p = pl.program_id(1)
    ig = pl.program_id(2)
    nrow = lax.broadcasted_iota(jnp.int32, (LANES, BLK), 0)
    orow = lax.broadcasted_iota(jnp.int32, (SC_IDX_ROWS, BLK), 0)
    qpos = lax.broadcasted_iota(jnp.int32, (1, BLK), 1)
    for g in range(group):
        blk = sr_ref[0, 0, :, g * BLK:(g + 1) * BLK]
        tab_t = tab_ref[0, 0, g * SUBLANES:(g + 1) * SUBLANES, :].T
        early = jnp.minimum(ig * group + g, MOBA_TOPK - 1)
        out = jnp.zeros((SC_IDX_ROWS, BLK), jnp.int32)
        for rep in range(n_rep):
            hh = rep // MOBA_TOPK
            sel = blk[rep:rep + 1, :]
            rank = blk[SUBLANES + rep:SUBLANES + rep + 1, :]
            start = jnp.sum(jnp.where(nrow == sel, tab_t[:, hh:hh + 1], 0.0), axis=0, keepdims=True)
            unused = spare + (((b * pl.num_programs(1) + p) * MOBA_TOPK + early) * n_rep + rep) * BLK
            dest = jnp.where(sel >= 0, start.astype(jnp.int32) + rank, unused + qpos)
            out = jnp.where(orow == rep, dest, out)
        idx_ref[:, g * BLK:(g + 1) * BLK] = out


def _moba_tiles_kernel(tb_ref, tp_ref, th_ref, tn_ref, nu_ref, sl_ref, q_ref, *refs, group):
    kv_refs, o_ref = refs[:group], refs[group]
    t = pl.program_id(0)
    tq = q_ref.shape[0] // group

    @pl.when(t * group < nu_ref[0])
    def _():
        lane = lax.broadcasted_iota(jnp.int32, (tq, LANES), 1)
        kpos = lax.broadcasted_iota(jnp.int32, (1, MOBA_BLOCK), 1)
        for g in range(group):
            tt = t * group + g
            hh = th_ref[tt]
            slope = sl_ref[2 * tp_ref[tt] + hh]
            head = (lane >> 6) == hh
            rows = pl.ds(g * tq, tq)
            q = jnp.where(head, q_ref[rows, :] * (MOBA_HEAD_DIM ** -0.5), 0.0).astype(BF16)
            s = lax.dot_general(q, kv_refs[g][0, :, :LANES], _NT, preferred_element_type=F32)
            s = s + slope * (kpos + tn_ref[tt] * MOBA_BLOCK).astype(F32)
            m = jnp.max(s, axis=1, keepdims=True)
            pr = jnp.exp(s - m)
            l = jnp.sum(pr, axis=1, keepdims=True)
            o = jnp.dot(pr.astype(BF16), kv_refs[g][0, :, LANES:], preferred_element_type=F32) / l
            o_ref[rows, :] = jnp.where(head, o, m + jnp.log(l))


def _moba_own_kernel(sl_ref, q_ref, kv_ref, o_ref, lse_ref):
    BLK = MOBA_BLOCK
    p = pl.program_id(1)
    n_blocks = q_ref.shape[1] // BLK
    lane = lax.broadcasted_iota(jnp.int32, (BLK, LANES), 1)
    rr = lax.broadcasted_iota(jnp.int32, (BLK, BLK), 0)
    cc = lax.broadcasted_iota(jnp.int32, (BLK, BLK), 1)
    first = lane < MOBA_HEAD_DIM
    for h in range(n_blocks):
        i = pl.program_id(2) * n_blocks + h
        rows = pl.ds(h * BLK, BLK)
        qf = q_ref[0, rows, :]
        k_own = kv_ref[0, rows, :LANES]
        v_own = kv_ref[0, rows, LANES:]
        key_pos = (lax.broadcasted_iota(jnp.int32, (1, BLK), 1) + i * BLK).astype(F32)
        outs, lses = [], []
        for hh in range(2):
            head = (lane >> 6) == hh
            slope = sl_ref[2 * p + hh]
            qh = jnp.where(head, qf * (MOBA_HEAD_DIM ** -0.5), 0.0).astype(BF16)
            s = lax.dot_general(qh, k_own, _NT, preferred_element_type=F32)
            s = jnp.where(cc <= rr, s + slope * key_pos, -jnp.inf)
            m = jnp.max(s, axis=1, keepdims=True)
            pr = jnp.exp(s - m)
            l = jnp.sum(pr, axis=1, keepdims=True)
            lses.append(m + jnp.log(l))
            outs.append(jnp.dot(pr.astype(BF16), v_own, preferred_element_type=F32) / l)
        o_ref[0, rows, :] = jnp.where(first, outs[0], outs[1])
        lse_ref[0, rows, :] = jnp.where(first, lses[0], lses[1])


def _moba_merge_kernel(o_ref, lse_ref, pg_ref, out_ref):
    BLK = MOBA_BLOCK
    first = lax.broadcasted_iota(jnp.int32, (BLK, LANES), 1) < MOBA_HEAD_DIM
    for h in range(o_ref.shape[1] // BLK):
        i = pl.program_id(2) * (o_ref.shape[1] // BLK) + h
        rows = pl.ds(h * BLK, BLK)
        lses = [lse_ref[0, rows, :]]
        vals = [o_ref[0, rows, :]]
        for j in range(MOBA_TOPK):
            p0 = pg_ref[j, 0, rows, :]
            p1 = pg_ref[MOBA_TOPK + j, 0, rows, :]
            has_block = j < i
            stats = pltpu.roll(jnp.where(first, p1, p0), MOBA_HEAD_DIM, 1)
            lses.append(jnp.where(has_block, stats, -jnp.inf))
            vals.append(jnp.where(has_block, jnp.where(first, p0, p1), 0.0))
        top = functools.reduce(jnp.maximum, lses)
        ws = [jnp.exp(x - top) for x in lses]
        out_ref[0, rows, :] = sum(w * v for w, v in zip(ws, vals)) / sum(ws)


def _moba_mixer(h3, kv3, q6):
    B, S, _ = h3.shape
    T = B * S
    BLK = MOBA_BLOCK
    TQ = MOBA_TILE
    G = MOBA_TILE_GROUP
    NB = S // BLK
    GI = math.gcd(NB, MOBA_PLACE_GROUP)
    nbp = -(-NB // SUBLANES) * SUBLANES
    NP = MOBA_HEADS // 2
    n_rep = 2 * MOBA_TOPK
    slopes = jnp.asarray(_alibi_slope_list(MOBA_HEADS), F32)

    QB = math.gcd(NB, MOBA_PICK_GROUP)
    selrank, counts = pl.pallas_call(
        functools.partial(_moba_pick_kernel, qblocks=QB),
        grid=(B, NP, NB // QB),
        in_specs=[pl.BlockSpec((1, QB * BLK, LANES), lambda b, p, i: (b, i, p)),
                  pl.BlockSpec((1, QB * BLK, LANES), lambda b, p, i: (b, i, NP + p))],
        out_specs=[pl.BlockSpec((1, 1, 2 * SUBLANES, QB * BLK), lambda b, p, i: (b, p, 0, i)),
                   pl.BlockSpec((1, 1, QB * nbp, LANES), lambda b, p, i: (b, p, i, 0))],
        out_shape=[jax.ShapeDtypeStruct((B, NP, 2 * SUBLANES, S), jnp.int32),
                   jax.ShapeDtypeStruct((B, NP, NB * nbp, LANES), F32)],
        scratch_shapes=[pltpu.VMEM((nbp, LANES), F32)],
        compiler_params=_params("parallel", "parallel", "arbitrary"),
        name="moba_pick",
    )(h3, h3)

    OB = math.gcd(NB, MOBA_OWN_GROUP)
    blk_spec = lambda w: pl.BlockSpec((1, OB * BLK, w), lambda b, p, i, sl: (b, i, p))
    own_o, own_lse = pl.pallas_call(
        _moba_own_kernel,
        grid_spec=pltpu.PrefetchScalarGridSpec(
            num_scalar_prefetch=1,
            grid=(B, NP, NB // OB),
            in_specs=[blk_spec(LANES), blk_spec(2 * LANES)],
            out_specs=[blk_spec(LANES), blk_spec(LANES)]),
        out_shape=[jax.ShapeDtypeStruct((B, S, MIX_WIDTH), F32)] * 2,
        compiler_params=_params("parallel", "parallel", "parallel"),
        name="moba_own",
    )(slopes, h3, kv3)

    cnt = counts.reshape(B, NP, NB, nbp, LANES)[:, :, :, :NB, :2].astype(jnp.int32)
    cnt = cnt.transpose(0, 1, 2, 4, 3)
    base = jnp.cumsum(cnt, axis=2) - cnt
    total = jnp.sum(cnt, axis=2)
    padded = (total + TQ - 1) // TQ * TQ
    pend = jnp.cumsum(padded.reshape(-1))
    seg_start = (pend - padded.reshape(-1)).reshape(B, NP, 1, 2, NB)
    table = jnp.zeros((B, NP, NB, SUBLANES, LANES), F32).at[:, :, :, :2, :NB].set(
        (seg_start + base).astype(F32)).reshape(B, NP, NB * SUBLANES, LANES)
    n_seg = B * NP * 2 * NB
    max_tiles = -(-((T * NP * n_rep) // TQ + n_seg) // G) * G
    first_row = jnp.arange(max_tiles, dtype=jnp.int32) * TQ
    tile_seg = jnp.minimum(jnp.sum(pend[None, :] <= first_row[:, None], axis=1), n_seg - 1)
    tile_n = (tile_seg % NB).astype(jnp.int32)
    tile_h = ((tile_seg // NB) % 2).astype(jnp.int32)
    tile_p = ((tile_seg // (2 * NB)) % NP).astype(jnp.int32)
    tile_b = (tile_seg // (2 * NB * NP)).astype(jnp.int32)
    n_used = (pend[-1] // TQ).astype(jnp.int32).reshape(1)
    spare = max_tiles * TQ
    n_rows = spare + max(B * NP * MOBA_TOPK * n_rep * BLK, G * TQ)

    idx = pl.pallas_call(
        functools.partial(_moba_place_kernel, group=GI, spare=spare),
        grid=(B, NP, NB // GI),
        in_specs=[pl.BlockSpec((1, 1, 2 * SUBLANES, GI * BLK), lambda b, p, i: (b, p, 0, i)),
                  pl.BlockSpec((1, 1, GI * SUBLANES, LANES), lambda b, p, i: (b, p, i, 0))],
        out_specs=pl.BlockSpec((SC_IDX_ROWS, GI * BLK),
                               lambda b, p, i: (0, p * (T // (GI * BLK)) + b * (NB // GI) + i)),
        out_shape=jax.ShapeDtypeStruct((SC_IDX_ROWS, NP * T), jnp.int32),
        compiler_params=_params("parallel", "parallel", "parallel"),
        name="moba_place",
    )(selrank, table)

    qs = _sc_scatter_rows(q6.reshape(NP * T, LANES), idx, n_rep, n_rows)

    live = lambda t, tb, tp, th, tn, nu, sl: (jnp.where(t * G < nu[0], t, max_tiles // G), 0)
    kv_blk = lambda g: pl.BlockSpec(
        (1, BLK, 2 * LANES),
        lambda t, tb, tp, th, tn, nu, sl: (tb[t * G + g], tn[t * G + g], tp[t * G + g]))
    part = pl.pallas_call(
        functools.partial(_moba_tiles_kernel, group=G),
        grid_spec=pltpu.PrefetchScalarGridSpec(
            num_scalar_prefetch=6,
            grid=(max_tiles // G,),
            in_specs=[pl.BlockSpec((G * TQ, LANES), live)] + [kv_blk(g) for g in range(G)],
            out_specs=pl.BlockSpec((G * TQ, LANES), live)),
        out_shape=jax.ShapeDtypeStruct((n_rows, LANES), F32),
        compiler_params=_params("arbitrary"),
        name="moba_tiles",
    )(tile_b, tile_p, tile_h, tile_n, n_used, slopes, qs, *([kv3] * G))

    pg = _sc_gather_rows(part, idx[:n_rep].reshape(-1)).reshape(n_rep, NP, T, LANES)

    MB = math.gcd(NB, MOBA_MERGE_GROUP)
    blk = pl.BlockSpec((1, MB * BLK, LANES), lambda b, p, i: (b, i, p))
    return pl.pallas_call(
        _moba_merge_kernel,
        grid=(B, NP, NB // MB),
        in_specs=[blk, blk,
                  pl.BlockSpec((n_rep, 1, MB * BLK, LANES),
                               lambda b, p, i: (0, p, b * (NB // MB) + i, 0))],
        out_specs=blk,
        out_shape=jax.ShapeDtypeStruct((B, S, MIX_WIDTH), F32),
        compiler_params=_params("parallel", "parallel", "parallel"),
        name="moba_merge",
    )(own_o, own_lse, pg)


def _memkv_kernel(mem_ref, w_ref, kv_ref):
    kv_ref[0] = jnp.dot(mem_ref[0].astype(BF16), w_ref[...].astype(BF16),
                        preferred_element_type=F32).astype(BF16)


def _memkv(mem, w_kv):
    B, M, D = mem.shape
    N = w_kv.shape[1]
    return pl.pallas_call(
        _memkv_kernel,
        grid=(B,),
        in_specs=[pl.BlockSpec((1, M, D), lambda b: (b, 0, 0)),
                  pl.BlockSpec((D, N), lambda b: (0, 0))],
        out_specs=pl.BlockSpec((1, M, N), lambda b: (b, 0, 0)),
        out_shape=jax.ShapeDtypeStruct((B, M, N), BF16),
        compiler_params=_params("parallel"),
        name="memkv",
    )(mem, w_kv)


def _layer_norm(z, g, b):
    mu = jnp.mean(z, axis=-1, keepdims=True)
    zc = z - mu
    var = jnp.mean(zc * zc, axis=-1, keepdims=True)
    return zc * lax.rsqrt(var + LN_EPS) * g + b


def _post_kernel(x_ref, mix_ref, mq_ref, kv_ref, wo_ref, g_ref, b_ref, wr_ref, br_ref,
                 x1_ref, x1s_ref, idx_ref, gate_ref, rank_ref, cnt_ref, run_ref, *, alpha):
    tm = x_ref.shape[0]
    mq = mq_ref[...]
    kv = kv_ref[0]
    km = kv[:, :MEM_WIDTH]
    vm = kv[:, MEM_WIDTH:]
    lane = lax.broadcasted_iota(jnp.int32, (tm, MEM_WIDTH), 1)
    scale = MEM_HEAD_DIM ** -0.5
    mo = jnp.zeros((tm, MEM_WIDTH), F32)
    for hd in range(MEM_HEADS):
        head = (lane >> 6) == hd
        qh = jnp.where(head, mq * scale, 0.0).astype(BF16)
        s = lax.dot_general(qh, km, _NT, preferred_element_type=F32)
        m = jnp.max(s, axis=1, keepdims=True)
        p = jnp.exp(s - m)
        l = jnp.sum(p, axis=1, keepdims=True)
        oh = jnp.dot(p.astype(BF16), vm, preferred_element_type=F32) / l
        mo = jnp.where(head, oh, mo)

    y = jnp.dot(mix_ref[...].astype(BF16), wo_ref[:MIX_WIDTH, :], preferred_element_type=F32)
    y = y + jnp.dot(mo.astype(BF16), wo_ref[MIX_WIDTH:, :], preferred_element_type=F32)
    x1 = _layer_norm(alpha * x_ref[...] + y, g_ref[...], b_ref[...])
    x1_ref[...] = x1
    _store_subrows(x1s_ref, x1)

    x_hi = x1.astype(BF16)
    x_lo = (x1 - x_hi.astype(F32)).astype(BF16)
    hi = jnp.dot(x_hi, wr_ref[...], preferred_element_type=F32)
    lo = jnp.dot(x_lo, wr_ref[:, :LANES], preferred_element_type=F32)
    logits = hi[:, :LANES] + hi[:, LANES:] + lo
    g = logits.T[:N_EXPERTS] + br_ref[...]
    erow = lax.broadcasted_iota(jnp.int32, (N_EXPERTS, tm), 0)
    orow = lax.broadcasted_iota(jnp.int32, (SUBLANES, tm), 0)
    idx_out = jnp.zeros((SUBLANES, tm), jnp.int32)
    vals, picks = [], []
    chosen = jnp.zeros((N_EXPERTS, tm), F32)
    for kk in range(TOP_K):
        mx = jnp.max(g, axis=0, keepdims=True)
        idx = jnp.min(jnp.where(g == mx, erow, N_EXPERTS), axis=0, keepdims=True)
        idx_out = jnp.where(orow == kk, idx, idx_out)
        vals.append(mx)
        picks.append(idx)
        chosen = chosen + jnp.where(erow == idx, 1.0, 0.0)
        g = jnp.where(erow == idx, -jnp.inf, g)
    evs = [jnp.exp(v - vals[0]) for v in vals]
    den = sum(evs)
    gate_out = jnp.zeros((SUBLANES, tm), F32)
    for kk in range(TOP_K):
        gate_out = jnp.where(orow == kk, evs[kk] / den, gate_out)
    idx_ref[...] = idx_out
    gate_ref[...] = gate_out

    @pl.when(pl.program_id(0) == 0)
    def _():
        run_ref[...] = jnp.zeros_like(run_ref)

    ta = lax.broadcasted_iota(jnp.int32, (tm, tm), 0)
    tb = lax.broadcasted_iota(jnp.int32, (tm, tm), 1)
    before = jnp.where(ta < tb, 1.0, 0.0).astype(BF16)
    earlier = jnp.dot(chosen.astype(BF16), before, preferred_element_type=F32) + run_ref[:, :1]
    rank_out = jnp.zeros((SUBLANES, tm), jnp.int32)
    for kk in range(TOP_K):
        rank = jnp.sum(jnp.where(erow == picks[kk], earlier, 0.0), axis=0, keepdims=True)
        rank_out = jnp.where(orow == kk, rank.astype(jnp.int32), rank_out)
    rank_ref[...] = rank_out
    run_ref[...] = run_ref[...] + jnp.sum(chosen, axis=1, keepdims=True)
    cnt_ref[...] = run_ref[...]


def _post_mixer(x2, mix2, h2, kv, w_o_bf16, ln_g, ln_b, w_router, b_router, seq_len, alpha):
    T, D = x2.shape
    tm = POST_ROWS
    N = h2.shape[1]
    M = kv.shape[1]
    mq_col = (N - MEM_WIDTH) // MEM_WIDTH
    tiles_per_seq = seq_len // tm
    wr = jnp.zeros((D, LANES), F32).at[:, :N_EXPERTS].set(w_router)
    wr_hi = wr.astype(BF16)
    wr = jnp.concatenate([wr_hi, (wr - wr_hi.astype(F32)).astype(BF16)], axis=1)
    br = b_router.reshape(N_EXPERTS, 1)
    row = lambda n: pl.BlockSpec((tm, n), lambda i: (i, 0))
    full = lambda a, b: pl.BlockSpec((a, b), lambda i: (0, 0))
    per_token = pl.BlockSpec((SUBLANES, tm), lambda i: (0, i))
    return pl.pallas_call(
        functools.partial(_post_kernel, alpha=alpha),
        grid=(T // tm,),
        in_specs=[row(D), row(MIX_WIDTH),
                  pl.BlockSpec((tm, MEM_WIDTH), lambda i: (i, mq_col)),
                  pl.BlockSpec((1, M, 2 * MEM_WIDTH), lambda i: (i // tiles_per_seq, 0, 0)),
                  full(D, D), full(1, D), full(1, D),
                  full(D, 2 * LANES), full(N_EXPERTS, 1)],
        out_specs=[row(D), pl.BlockSpec((tm * (D // LANES), LANES), lambda i: (i, 0)),
                   per_token, per_token, per_token, full(N_EXPERTS, LANES)],
        out_shape=[jax.ShapeDtypeStruct((T, D), F32),
                   jax.ShapeDtypeStruct((T * (D // LANES), LANES), F32),
                   jax.ShapeDtypeStruct((SUBLANES, T), jnp.int32),
                   jax.ShapeDtypeStruct((SUBLANES, T), F32),
                   jax.ShapeDtypeStruct((SUBLANES, T), jnp.int32),
                   jax.ShapeDtypeStruct((N_EXPERTS, LANES), F32)],
        scratch_shapes=[pltpu.VMEM((N_EXPERTS, LANES), F32)],
        compiler_params=_params("arbitrary"),
        name="post_mixer",
    )(x2, mix2, h2, kv, w_o_bf16, ln_g.reshape(1, D), ln_b.reshape(1, D), wr, br)


def _sc_mesh():
    return plsc.VectorSubcoreMesh(core_axis_name="core", subcore_axis_name="subcore")


def _sc_scatter_rows(rows, idx, n_rep, n_out):
    R, W = rows.shape

    @functools.partial(pl.kernel, out_type=jax.ShapeDtypeStruct((n_out, W), rows.dtype),
                       mesh=_sc_mesh(), scratch_types=[])
    def scatter(x_hbm, i_hbm, o_hbm):
        def body(x_vmem, i_vmem):
            for r in range(n_rep):
                pltpu.sync_copy(x_vmem, o_hbm.at[i_vmem.at[r]])

        pltpu.emit_pipeline(
            body, grid=(R // SC_WINDOW,),
            in_specs=[pl.BlockSpec((SC_WINDOW, W), lambda i: (i, 0)),
                      pl.BlockSpec((SC_IDX_ROWS, SC_WINDOW), lambda i: (0, i))],
            out_specs=[], core_axis_name=("core", "subcore"),
            dimension_semantics=(pltpu.PARALLEL,), trace_scopes=False)(x_hbm, i_hbm)

    return scatter(rows, idx)


def _sc_gather_rows(table, idx):
    n = idx.shape[0]
    W = table.shape[1]

    @functools.partial(pl.kernel, out_type=jax.ShapeDtypeStruct((n, W), table.dtype),
                       mesh=_sc_mesh(), scratch_types=[])
    def gather(t_hbm, i_hbm, o_hbm):
        def body(i_vmem, o_vmem):
            pltpu.sync_copy(t_hbm.at[i_vmem.at[0]], o_vmem)

        pltpu.emit_pipeline(
            body, grid=(n // SC_WINDOW,),
            in_specs=[pl.BlockSpec((1, SC_WINDOW), lambda i: (0, i))],
            out_specs=[pl.BlockSpec((SC_WINDOW, W), lambda i: (i, 0))],
            core_axis_name=("core", "subcore"),
            dimension_semantics=(pltpu.PARALLEL,), trace_scopes=False)(i_hbm, o_hbm)

    return gather(table, idx.reshape(1, n))


def _sc_workers():
    info = pltpu.get_tpu_info().sparse_core
    return info.num_cores, info.num_cores * info.num_subcores


def _sc_scatter_slabs(rows, idx, n_rep, n_out):
    R, S, W = rows.shape
    n_cores, n_workers = _sc_workers()
    per_worker = (R // SC_WINDOW) // n_workers
    assert per_worker * n_workers * SC_WINDOW == R

    @functools.partial(pl.kernel, out_type=jax.ShapeDtypeStruct((n_out, S, W), rows.dtype),
                       mesh=_sc_mesh(),
                       scratch_types=[pltpu.VMEM((SC_IDX_ROWS, SC_WINDOW), jnp.int32),
                                      pltpu.VMEM((SC_CHUNK, S, W), rows.dtype)])
    def scatter(x_hbm, i_hbm, o_hbm, ibuf, buf):
        wid = lax.axis_index("subcore") * n_cores + lax.axis_index("core")

        @pl.loop(0, per_worker)
        def _(s):
            first = (wid * per_worker + s) * SC_WINDOW
            pltpu.sync_copy(i_hbm.at[:, pl.ds(first, SC_WINDOW)], ibuf)
            for c in range(SC_WINDOW // SC_CHUNK):
                pltpu.sync_copy(x_hbm.at[pl.ds(first + c * SC_CHUNK, SC_CHUNK)], buf)
                for r in range(n_rep):
                    pltpu.sync_copy(buf, o_hbm.at[ibuf.at[r, pl.ds(c * SC_CHUNK, SC_CHUNK)]])

    return scatter(rows, idx)


def _sc_gather_slabs(table, idx):
    n = idx.shape[0]
    S, W = table.shape[1:]
    n_cores, n_workers = _sc_workers()
    per_worker = (n // SC_WINDOW) // n_workers
    assert per_worker * n_workers * SC_WINDOW == n
    n_chunks = SC_WINDOW // SC_CHUNK

    @functools.partial(pl.kernel, out_type=jax.ShapeDtypeStruct((n, S, W), table.dtype),
                       mesh=_sc_mesh(),
                       scratch_types=[pltpu.VMEM((1, SC_WINDOW), jnp.int32),
                                      pltpu.VMEM((2, SC_CHUNK, S, W), table.dtype),
                                      pltpu.SemaphoreType.DMA((2,)), pltpu.SemaphoreType.DMA((2,))])
    def gather(t_hbm, i_hbm, o_hbm, ibuf, buf, fetch_sem, store_sem):
        wid = lax.axis_index("subcore") * n_cores + lax.axis_index("core")

        @pl.loop(0, per_worker)
        def _(s):
            blk = wid * per_worker + s
            pltpu.sync_copy(i_hbm.at[pl.ds(blk, 1)], ibuf)

            def fetch(c):
                return pltpu.make_async_copy(
                    t_hbm.at[ibuf.at[0, pl.ds(c * SC_CHUNK, SC_CHUNK)]], buf.at[c % 2],
                    fetch_sem.at[c % 2])

            def store(c):
                return pltpu.make_async_copy(
                    buf.at[c % 2], o_hbm.at[pl.ds(blk * SC_WINDOW + c * SC_CHUNK, SC_CHUNK)],
                    store_sem.at[c % 2])

            fetch(0).start()
            for c in range(n_chunks):
                if c + 1 < n_chunks:
                    if c >= 1:
                        store(c - 1).wait()
                    fetch(c + 1).start()
                fetch(c).wait()
                store(c).start()
            store(n_chunks - 2).wait()
            store(n_chunks - 1).wait()

    return gather(table, idx.reshape(n // SC_WINDOW, SC_WINDOW))


def _route(top_idx, rank, counts):
    rb = MOE_ROWS
    n_tokens = top_idx.shape[1]
    tk = n_tokens * TOP_K
    padded = (counts + rb - 1) // rb * rb
    pend = jnp.cumsum(padded)
    pstart = pend - padded
    experts = jnp.arange(N_EXPERTS, dtype=jnp.int32)
    start = jnp.sum(jnp.where(top_idx[:TOP_K, :, None] == experts, pstart, 0), axis=2)
    dest = (start + rank[:TOP_K]).astype(jnp.int32)
    n_blocks = tk // rb + N_EXPERTS
    first_row = jnp.arange(n_blocks, dtype=jnp.int32) * rb
    block_e = jnp.minimum(jnp.sum(pend[None, :] <= first_row[:, None], axis=1),
                          N_EXPERTS - 1).astype(jnp.int32)
    n_used = (pend[-1] // rb).astype(jnp.int32).reshape(1)
    return dest, block_e, n_used


def _dispatch(x1s, dest, n_rows, sub):
    T = dest.shape[1]
    idx = jnp.concatenate([dest, jnp.zeros((SC_IDX_ROWS - TOP_K, T), jnp.int32)], axis=0)
    xs = _sc_scatter_slabs(x1s.reshape(T, sub, LANES), idx, TOP_K, n_rows)
    return xs.reshape(n_rows * sub, LANES)


def _expert_kernel(be_ref, nu_ref, x_ref, wg_ref, bg_ref, wu_ref, bu_ref, wd_ref, bd_ref,
                   y_ref, wgb, wub, wdb):
    i = pl.program_id(0)
    prev = be_ref[jnp.maximum(i - 1, 0)]

    @pl.when((i == 0) | (be_ref[i] != prev))
    def _():
        wgb[...] = wg_ref[0, 0].astype(BF16)
        wub[...] = wu_ref[0, 0].astype(BF16)
        wdb[...] = wd_ref[0, 0].astype(BF16)

    @pl.when(i < nu_ref[0])
    def _():
        sub = wgb.shape[0] // LANES
        xb = _load_subrows(x_ref, x_ref.shape[0] // sub, sub).astype(BF16)
        gate = jnp.dot(xb, wgb[...], preferred_element_type=F32) + bg_ref[0, 0]
        gate = jnp.minimum(gate, SWIGLU_LIMIT)
        up = jnp.dot(xb, wub[...], preferred_element_type=F32) + bu_ref[0, 0]
        up = jnp.clip(up, -SWIGLU_LIMIT, SWIGLU_LIMIT)
        hid = gate * _sigmoid(SWIGLU_ALPHA * gate) * (up + 1.0)
        y = jnp.dot(hid.astype(BF16), wdb[...], preferred_element_type=F32) + bd_ref[0, 0]
        _store_subrows(y_ref, y)


def _experts(xs, block_e, n_used, layer, w_gate, b_gate, w_up, b_up, w_down, b_down):
    rb = MOE_ROWS
    n_blocks = block_e.shape[0]
    E, D, F = w_gate.shape[1:]
    sub = D // LANES
    wspec = lambda a, b: pl.BlockSpec((1, 1, a, b), lambda i, be, nu: (layer, be[i], 0, 0))
    live = lambda i, be, nu: (jnp.where(i < nu[0], i, n_blocks), 0)
    grid_spec = pltpu.PrefetchScalarGridSpec(
        num_scalar_prefetch=2,
        grid=(n_blocks,),
        in_specs=[pl.BlockSpec((rb * sub, LANES), live),
                  wspec(D, F), wspec(1, F), wspec(D, F), wspec(1, F), wspec(F, D), wspec(1, D)],
        out_specs=pl.BlockSpec((rb * sub, LANES), live),
        scratch_shapes=[pltpu.VMEM((D, F), BF16), pltpu.VMEM((D, F), BF16),
                        pltpu.VMEM((F, D), BF16)],
    )
    depth = w_gate.shape[0]
    return pl.pallas_call(
        _expert_kernel,
        grid_spec=grid_spec,
        out_shape=jax.ShapeDtypeStruct(xs.shape, F32),
        compiler_params=_params("arbitrary"),
        name="experts",
    )(block_e, n_used, xs, w_gate, b_gate.reshape(depth, E, 1, F),
      w_up, b_up.reshape(depth, E, 1, F), w_down, b_down.reshape(depth, E, 1, D))


def _combine_kernel(x1_ref, gate_ref, y_ref, g_ref, b_ref, o_ref, *, alpha):
    tm, D = x1_ref.shape
    sub = D // LANES
    gates = gate_ref[...].T
    f = jnp.zeros(x1_ref.shape, F32)
    for kk in range(TOP_K):
        f = f + gates[:, kk:kk + 1] * _load_subrows(y_ref, tm, sub, kk * sub, TOP_K * sub)
    o_ref[...] = _layer_norm(alpha * x1_ref[...] + f, g_ref[...], b_ref[...])


def _combine(x1, gates, dest, y_rows, ln_g, ln_b, alpha):
    T, D = x1.shape
    sub = D // LANES
    tm = COMBINE_ROWS
    yg = _sc_gather_slabs(y_rows.reshape(-1, sub, LANES), dest.T.reshape(-1)).reshape(-1, LANES)
    row = lambda n: pl.BlockSpec((tm, n), lambda i: (i, 0))
    full = lambda a, b: pl.BlockSpec((a, b), lambda i: (0, 0))
    return pl.pallas_call(
        functools.partial(_combine_kernel, alpha=alpha),
        grid=(T // tm,),
        in_specs=[row(D), pl.BlockSpec((SUBLANES, tm), lambda i: (0, i)),
                  pl.BlockSpec((tm * TOP_K * sub, LANES), lambda i: (i, 0)),
                  full(1, D), full(1, D)],
        out_specs=row(D),
        out_shape=jax.ShapeDtypeStruct((T, D), F32),
        compiler_params=_params("parallel"),
        name="combine",
    )(x1, gates, yg, ln_g.reshape(1, D), ln_b.reshape(1, D))


def kernel(x, mem, w_in_hgrn, hgrn_lb_logits, hgrn_norm_g, w_in_moba, w_mem_kv, w_o,
           ln_mix_g, ln_mix_b, w_router, b_router, w_gate, b_gate, w_up, b_up,
           w_down, b_down, ln_ffn_g, ln_ffn_b):
    B, S, D = x.shape
    T = B * S
    depth = w_o.shape[0]
    alpha = (2 * depth) ** 0.25

    p_lb = jax.nn.softmax(hgrn_lb_logits.astype(F32), axis=0)
    lower_bounds = jnp.cumsum(p_lb, axis=0) - p_lb[0]

    x2 = x.reshape(T, D)
    for layer in range(depth):
        j = layer // 2
        if layer % 2 == 0:
            (h2,) = _inproj(x2, w_in_hgrn[j].astype(BF16), for_moba=False)
            mix = _hgrn_mixer(h2.reshape(B, S, -1), lower_bounds[j], hgrn_norm_g[j])
        else:
            h2, kv2, q6 = _inproj(x2, w_in_moba[j].astype(BF16), for_moba=True)
            mix = _moba_mixer(h2.reshape(B, S, -1), kv2.reshape(B, S, -1), q6)
        kv = _memkv(mem, w_mem_kv[layer])
        x1, x1s, top_idx, gates, rank, counts = _post_mixer(
            x2, mix.reshape(T, MIX_WIDTH), h2, kv, w_o[layer].astype(BF16),
            ln_mix_g[layer], ln_mix_b[layer], w_router[layer], b_router[layer], S, alpha)
        dest, block_e, n_used = _route(top_idx, rank, counts[:, 0].astype(jnp.int32))
        xs = _dispatch(x1s, dest, (block_e.shape[0] + 1) * MOE_ROWS, D // LANES)
        y_rows = _experts(xs, block_e, n_used, layer, w_gate, b_gate, w_up, b_up, w_down, b_down)
        x2 = _combine(x1, gates, dest, y_rows, ln_ffn_g[layer], ln_ffn_b[layer], alpha)
    return x2.reshape(B, S, D)
```

```python
import functools
import math

import jax
import jax.numpy as jnp
from jax import lax
from jax.experimental import pallas as pl
from jax.experimental.pallas import tpu as pltpu
from jax.experimental.pallas import tpu_sc as plsc

MIX_WIDTH = 768
MEM_HEADS = 4
MEM_HEAD_DIM = 64
MEM_WIDTH = MEM_HEADS * MEM_HEAD_DIM
HGRN_HEADS = 6
HGRN_DK = 128
MOBA_HEADS = 12
MOBA_HEAD_DIM = 64
MOBA_BLOCK = 256
MOBA_TOPK = 3
N_EXPERTS = 32
TOP_K = 4
SWIGLU_ALPHA = 1.702
SWIGLU_LIMIT = 7.0
LN_EPS = 1e-5
RMS_EPS = 1e-6

LANES = 128
SUBLANES = 8
VMEM_LIMIT_BYTES = 56 * 1024 * 1024

INPROJ_ROWS = 512
HGRN_CHUNK = 64
HGRN_ROWS = 512
POST_ROWS = 512
MOBA_TILE = 256
MOBA_TILE_GROUP = 16
MOBA_PLACE_GROUP = 8
MOBA_PICK_GROUP = 4
MOBA_MERGE_GROUP = 4
MOBA_OWN_GROUP = 4
MOE_ROWS = 512
COMBINE_ROWS = 512
SC_WINDOW = 128
SC_IDX_ROWS = 8
SC_CHUNK = 32

BF16 = jnp.bfloat16
F32 = jnp.float32

_NT = (((1,), (1,)), ((), ()))
_TN = (((0,), (0,)), ((), ()))


def _alibi_slope_list(n):
    def pow2(m):
        start = 2.0 ** (-(2.0 ** -(math.log2(m) - 3)))
        return [start ** (i + 1) for i in range(m)]
    if math.log2(n).is_integer():
        return pow2(n)
    c = 2 ** math.floor(math.log2(n))
    return pow2(c) + _alibi_slope_list(2 * c)[0::2][:n - c]


def _sigmoid(x):
    return 1.0 / (1.0 + jnp.exp(-x))


def _params(*sem):
    return pltpu.CompilerParams(dimension_semantics=sem, vmem_limit_bytes=VMEM_LIMIT_BYTES)


def _store_subrows(ref, value, first=0, stride=None):
    sub = value.shape[1] // LANES
    stride = stride or sub
    for c in range(sub):
        ref[pl.ds(first + c, value.shape[0], stride=stride), :] = value[:, c * LANES:(c + 1) * LANES]


def _load_subrows(ref, rows, sub, first=0, stride=None):
    stride = stride or sub
    return jnp.concatenate(
        [ref[pl.ds(first + c, rows, stride=stride), :] for c in range(sub)], axis=1)


def _inproj_kernel(x_ref, w_ref, h_ref, *moba_refs):
    h = jnp.dot(x_ref[...].astype(BF16), w_ref[...], preferred_element_type=F32)
    h_ref[...] = h
    if moba_refs:
        kv_ref, q6_ref = moba_refs
        n_pairs = q6_ref.shape[0]
        for p in range(n_pairs):
            q6_ref[p] = h[:, p * LANES:(p + 1) * LANES] * (MOBA_HEAD_DIM ** -0.5)
            for part in range(2):
                col = (1 + part) * MIX_WIDTH + p * LANES
                kv_ref[:, (2 * p + part) * LANES:(2 * p + part + 1) * LANES] = (
                    h[:, col:col + LANES].astype(BF16))


def _inproj(x2, w_bf16, for_moba):
    T, D = x2.shape
    N = w_bf16.shape[1]
    tm = INPROJ_ROWS
    out_shape = [jax.ShapeDtypeStruct((T, N), F32)]
    out_specs = [pl.BlockSpec((tm, N), lambda i: (i, 0))]
    if for_moba:
        NP = MOBA_HEADS // 2
        out_shape += [jax.ShapeDtypeStruct((T, 2 * MIX_WIDTH), BF16),
                      jax.ShapeDtypeStruct((NP, T, LANES), F32)]
        out_specs += [pl.BlockSpec((tm, 2 * MIX_WIDTH), lambda i: (i, 0)),
                      pl.BlockSpec((NP, tm, LANES), lambda i: (0, i, 0))]
    return pl.pallas_call(
        _inproj_kernel,
        grid=(T // tm,),
        in_specs=[pl.BlockSpec((tm, D), lambda i: (i, 0)),
                  pl.BlockSpec((D, N), lambda i: (0, 0))],
        out_specs=out_specs,
        out_shape=out_shape,
        compiler_params=_params("parallel"),
        name="inproj",
    )(x2, w_bf16)


def _cumsum_rows(x, row):
    n = x.shape[0]
    sh = 1
    while sh < n:
        x = x + jnp.where(row >= sh, pltpu.roll(x, sh, 0), 0.0)
        sh *= 2
    return x


def _bcast_row(a, group, r):
    n = a.shape[0]
    a3 = a.reshape(n // group, group, LANES)
    return jnp.broadcast_to(a3[:, r:r + 1, :], a3.shape).reshape(n, LANES)


def _hgrn_chunk(qr, fr, v, gr, lb, ng, e_sum, st_t):
    C = qr.shape[0]
    row = lax.broadcasted_iota(jnp.int32, (C, LANES), 0)
    rr = lax.broadcasted_iota(jnp.int32, (C, C), 0)
    cc = lax.broadcasted_iota(jnp.int32, (C, C), 1)

    q = qr * _sigmoid(qr)
    forget = lb + (1.0 - lb) * _sigmoid(fr)
    k = 1.0 - forget
    G = _cumsum_rows(jnp.log(forget), row)

    z = jnp.log(k) - G
    parts = []
    for s in range(SUBLANES):
        parts.append((q * jnp.exp(jnp.minimum(G + _bcast_row(z, SUBLANES, s), 0.0))).astype(BF16))
    a_diag = jnp.dot(jnp.concatenate(parts, axis=1), e_sum, preferred_element_type=F32)
    A = jnp.where(((rr >> 3) == (cc >> 3)) & (cc <= rr), a_diag, 0.0)

    m = SUBLANES
    while m < C:
        lg = int(math.log2(m))
        Gr = _bcast_row(G, 2 * m, m - 1)
        second = ((row >> lg) & 1) == 1
        qm = q * jnp.exp(jnp.where(second, G - Gr, -jnp.inf))
        km = k * jnp.exp(jnp.where(second, -jnp.inf, Gr - G))
        am = lax.dot_general(qm.astype(BF16), km.astype(BF16), _NT, preferred_element_type=F32)
        A = A + jnp.where((rr >> (lg + 1)) == (cc >> (lg + 1)), am, 0.0)
        m *= 2

    vb = v.astype(BF16)
    o = jnp.dot(A.astype(BF16), vb, preferred_element_type=F32)
    o = o + lax.dot_general((q * jnp.exp(G)).astype(BF16), st_t.astype(BF16), _NT,
                            preferred_element_type=F32)
    g_end = G[C - 1:C, :]
    kd = (k * jnp.exp(g_end - G)).astype(BF16)
    st_new = st_t * jnp.exp(g_end) + lax.dot_general(vb, kd, _TN, preferred_element_type=F32)

    ms = jnp.mean(o * o, axis=-1, keepdims=True)
    out = o * lax.rsqrt(ms + RMS_EPS) * ng * _sigmoid(gr)
    return out, st_new


def _hgrn_kernel(q_ref, f_ref, i_ref, g_ref, lb_ref, ng_ref, e_ref, o_ref, st_ref, *, chunk):
    @pl.when(pl.program_id(2) == 0)
    def _():
        st_ref[...] = jnp.zeros_like(st_ref)

    lb = lb_ref[0]
    ng = ng_ref[...]
    e_sum = e_ref[...]
    n_chunks = q_ref.shape[1] // chunk
    for c in range(n_chunks):
        sl = pl.ds(c * chunk, chunk)
        out, st_new = _hgrn_chunk(q_ref[0, sl, :], f_ref[0, sl, :], i_ref[0, sl, :],
                                  g_ref[0, sl, :], lb, ng, e_sum, st_ref[...])
        st_ref[...] = st_new
        o_ref[0, sl, :] = out


def _hgrn_mixer(h3, lb, norm_g):
    B, S, _ = h3.shape
    ts = min(HGRN_ROWS, S)
    C = HGRN_CHUNK
    H = HGRN_HEADS
    e_sum = (jnp.arange(SUBLANES * LANES)[:, None] // LANES == jnp.arange(C)[None, :] % SUBLANES
             ).astype(BF16)
    col = lambda off: pl.BlockSpec((1, ts, LANES), lambda b, h, s, off=off: (b, s, off + h))
    return pl.pallas_call(
        functools.partial(_hgrn_kernel, chunk=C),
        grid=(B, H, S // ts),
        in_specs=[col(0), col(H), col(2 * H), col(3 * H),
                  pl.BlockSpec((1, 1, LANES), lambda b, h, s: (h, 0, 0)),
                  pl.BlockSpec((1, LANES), lambda b, h, s: (0, 0)),
                  pl.BlockSpec((SUBLANES * LANES, C), lambda b, h, s: (0, 0))],
        out_specs=pl.BlockSpec((1, ts, LANES), lambda b, h, s: (b, s, h)),
        out_shape=jax.ShapeDtypeStruct((B, S, MIX_WIDTH), F32),
        scratch_shapes=[pltpu.VMEM((HGRN_DK, HGRN_DK), F32)],
        compiler_params=_params("parallel", "parallel", "arbitrary"),
        name="hgrn",
    )(h3, h3, h3, h3, lb.reshape(H, 1, LANES), norm_g.reshape(1, LANES), e_sum)


def _moba_pick_kernel(q_ref, kf_ref, sr_ref, cnt_ref, kmean_ref, *, qblocks):
    BLK = MOBA_BLOCK
    W = qblocks * BLK
    nbp = kmean_ref.shape[0]
    i0 = pl.program_id(2) * qblocks

    @pl.when(pl.program_id(2) == 0)
    def _():
        kmean_ref[...] = jnp.zeros_like(kmean_ref)

    for j in range(qblocks):
        kmean_ref[pl.ds(i0 + j, 1), :] = jnp.mean(kf_ref[0, j * BLK:(j + 1) * BLK, :], axis=0,
                                                  keepdims=True)
    km = kmean_ref[...]
    lane_k = lax.broadcasted_iota(jnp.int32, (nbp, LANES), 1)
    km2 = jnp.concatenate([jnp.where(lane_k < MOBA_HEAD_DIM, km, 0.0),
                           jnp.where(lane_k >= MOBA_HEAD_DIM, km, 0.0)], axis=0)
    gate = lax.dot_general(km2, q_ref[0], _NT, precision=lax.Precision.HIGHEST,
                           preferred_element_type=F32).reshape(2, nbp, W)
    nblk = lax.broadcasted_iota(jnp.int32, (2, nbp, W), 1)
    qi = i0 + (lax.broadcasted_iota(jnp.int32, (2, 1, W), 2) >> int(math.log2(BLK)))
    g = jnp.where(nblk < qi, gate, -jnp.inf)
    picks = []
    for _ in range(MOBA_TOPK):
        mx = jnp.max(g, axis=1, keepdims=True)
        idx = jnp.min(jnp.where(g == mx, nblk, nbp), axis=1, keepdims=True)
        picks.append((idx, (mx > -jnp.inf) & (idx < qi)))
        g = jnp.where(nblk == idx, -jnp.inf, g)
    chosen = jnp.zeros((2, nbp, W), F32)
    for idx, valid in picks:
        chosen = chosen + jnp.where((nblk == idx) & valid, 1.0, 0.0)
    qa = lax.broadcasted_iota(jnp.int32, (W, W), 0)
    qc = lax.broadcasted_iota(jnp.int32, (W, W), 1)
    shift = int(math.log2(BLK))
    before = jnp.where((qa < qc) & ((qa >> shift) == (qc >> shift)), 1.0, 0.0).astype(BF16)
    earlier = jnp.dot(chosen.reshape(2 * nbp, W).astype(BF16), before,
                      preferred_element_type=F32).reshape(2, nbp, W)
    orow = lax.broadcasted_iota(jnp.int32, (2 * SUBLANES, W), 0)
    out = jnp.zeros((2 * SUBLANES, W), jnp.int32)
    for j, (idx, valid) in enumerate(picks):
        rank = jnp.sum(jnp.where(nblk == idx, earlier, 0.0), axis=1, keepdims=True).astype(jnp.int32)
        sel = jnp.where(valid, idx, -1)
        for hh in range(2):
            rep = hh * MOBA_TOPK + j
            out = jnp.where(orow == rep, sel[hh], out)
            out = jnp.where(orow == SUBLANES + rep, rank[hh], out)
    sr_ref[0, 0] = out
    lane_c = lax.broadcasted_iota(jnp.int32, (nbp, LANES), 1)
    for j in range(qblocks):
        per_block = jnp.sum(chosen[:, :, j * BLK:(j + 1) * BLK], axis=2, keepdims=True)
        cnt_ref[0, 0, j * nbp:(j + 1) * nbp, :] = jnp.where(
            lane_c == 0, per_block[0], jnp.where(lane_c == 1, per_block[1], 0.0))


def _moba_place_kernel(sr_ref, tab_ref, idx_ref, *, group, spare):
    BLK = MOBA_BLOCK
    n_rep = 2 * MOBA_TOPK
    b = pl.program_id(0)
    p = pl.program_id(1)
    ig = pl.program_id(2)
    nrow = lax.broadcasted_iota(jnp.int32, (LANES, BLK), 0)
    orow = lax.broadcasted_iota(jnp.int32, (SC_IDX_ROWS, BLK), 0)
    qpos = lax.broadcasted_iota(jnp.int32, (1, BLK), 1)
    for g in range(group):
        blk = sr_ref[0, 0, :, g * BLK:(g + 1) * BLK]
        tab_t = tab_ref[0, 0, g * SUBLANES:(g + 1) * SUBLANES, :].T
        early = jnp.minimum(ig * group + g, MOBA_TOPK - 1)
        out = jnp.zeros((SC_IDX_ROWS, BLK), jnp.int32)
        for rep in range(n_rep):
            hh = rep // MOBA_TOPK
            sel = blk[rep:rep + 1, :]
            rank = blk[SUBLANES + rep:SUBLANES + rep + 1, :]
            start = jnp.sum(jnp.where(nrow == sel, tab_t[:, hh:hh + 1], 0.0), axis=0, keepdims=True)
            unused = spare + (((b * pl.num_programs(1) + p) * MOBA_TOPK + early) * n_rep + rep) * BLK
            dest = jnp.where(sel >= 0, start.astype(jnp.int32) + rank, unused + qpos)
            out = jnp.where(orow == rep, dest, out)
        idx_ref[:, g * BLK:(g + 1) * BLK] = out


def _moba_tiles_kernel(tb_ref, tp_ref, th_ref, tn_ref, nu_ref, sl_ref, q_ref, *refs, group):
    kv_refs, o_ref = refs[:group], refs[group]
    t = pl.program_id(0)
    tq = q_ref.shape[0] // group

    @pl.when(t * group < nu_ref[0])
    def _():
        lane = lax.broadcasted_iota(jnp.int32, (tq, LANES), 1)
        kpos = lax.broadcasted_iota(jnp.int32, (1, MOBA_BLOCK), 1)
        for g in range(group):
            tt = t * group + g
            hh = th_ref[tt]
            slope = sl_ref[2 * tp_ref[tt] + hh]
            head = (lane >> 6) == hh
            rows = pl.ds(g * tq, tq)
            q = jnp.where(head, q_ref[rows, :], 0.0).astype(BF16)
            s = lax.dot_general(q, kv_refs[g][0, :, :LANES], _NT, preferred_element_type=F32)
            s = s + slope * (kpos + tn_ref[tt] * MOBA_BLOCK).astype(F32)
            m = jnp.max(s, axis=1, keepdims=True)
            pr = jnp.exp(s - m)
            l = jnp.sum(pr, axis=1, keepdims=True)
            o = jnp.dot(pr.astype(BF16), kv_refs[g][0, :, LANES:], preferred_element_type=F32) / l
            o_ref[rows, :] = jnp.where(head, o, m + jnp.log(l))


def _moba_own_kernel(sl_ref, q_ref, kv_ref, o_ref, lse_ref):
    BLK = MOBA_BLOCK
    p = pl.program_id(1)
    n_blocks = q_ref.shape[1] // BLK
    lane = lax.broadcasted_iota(jnp.int32, (BLK, LANES), 1)
    rr = lax.broadcasted_iota(jnp.int32, (BLK, BLK), 0)
    cc = lax.broadcasted_iota(jnp.int32, (BLK, BLK), 1)
    first = lane < MOBA_HEAD_DIM
    for h in range(n_blocks):
        i = pl.program_id(2) * n_blocks + h
        rows = pl.ds(h * BLK, BLK)
        qf = q_ref[0, rows, :]
        k_own = kv_ref[0, rows, :LANES]
        v_own = kv_ref[0, rows, LANES:]
        key_pos = (lax.broadcasted_iota(jnp.int32, (1, BLK), 1) + i * BLK).astype(F32)
        outs, lses = [], []
        for hh in range(2):
            head = (lane >> 6) == hh
            slope = sl_ref[2 * p + hh]
            qh = jnp.where(head, qf * (MOBA_HEAD_DIM ** -0.5), 0.0).astype(BF16)
            s = lax.dot_general(qh, k_own, _NT, preferred_element_type=F32)
            s = jnp.where(cc <= rr, s + slope * key_pos, -jnp.inf)
            m = jnp.max(s, axis=1, keepdims=True)
            pr = jnp.exp(s - m)
            l = jnp.sum(pr, axis=1, keepdims=True)
            lses.append(m + jnp.log(l))
            outs.append(jnp.dot(pr.astype(BF16), v_own, preferred_element_type=F32) / l)
        o_ref[0, rows, :] = jnp.where(first, outs[0], outs[1])
        lse_ref[0, rows, :] = jnp.where(first, lses[0], lses[1])


def _moba_merge_kernel(o_ref, lse_ref, pg_ref, out_ref):
    BLK = MOBA_BLOCK
    first = lax.broadcasted_iota(jnp.int32, (BLK, LANES), 1) < MOBA_HEAD_DIM
    for h in range(o_ref.shape[1] // BLK):
        i = pl.program_id(2) * (o_ref.shape[1] // BLK) + h
        rows = pl.ds(h * BLK, BLK)
        lses = [lse_ref[0, rows, :]]
        vals = [o_ref[0, rows, :]]
        for j in range(MOBA_TOPK):
            p0 = pg_ref[j, 0, rows, :]
            p1 = pg_ref[MOBA_TOPK + j, 0, rows, :]
            has_block = j < i
            stats = pltpu.roll(jnp.where(first, p1, p0), MOBA_HEAD_DIM, 1)
            lses.append(jnp.where(has_block, stats, -jnp.inf))
            vals.append(jnp.where(has_block, jnp.where(first, p0, p1), 0.0))
        top = functools.reduce(jnp.maximum, lses)
        ws = [jnp.exp(x - top) for x in lses]
        out_ref[0, rows, :] = sum(w * v for w, v in zip(ws, vals)) / sum(ws)


def _moba_mixer(h3, kv3, q6):
    B, S, _ = h3.shape
    T = B * S
    BLK = MOBA_BLOCK
    TQ = MOBA_TILE
    G = MOBA_TILE_GROUP
    NB = S // BLK
    GI = math.gcd(NB, MOBA_PLACE_GROUP)
    nbp = -(-NB // SUBLANES) * SUBLANES
    NP = MOBA_HEADS // 2
    n_rep = 2 * MOBA_TOPK
    slopes = jnp.asarray(_alibi_slope_list(MOBA_HEADS), F32)

    QB = math.gcd(NB, MOBA_PICK_GROUP)
    selrank, counts = pl.pallas_call(
        functools.partial(_moba_pick_kernel, qblocks=QB),
        grid=(B, NP, NB // QB),
        in_specs=[pl.BlockSpec((1, QB * BLK, LANES), lambda b, p, i: (b, i, p)),
                  pl.BlockSpec((1, QB * BLK, LANES), lambda b, p, i: (b, i, NP + p))],
        out_specs=[pl.BlockSpec((1, 1, 2 * SUBLANES, QB * BLK), lambda b, p, i: (b, p, 0, i)),
                   pl.BlockSpec((1, 1, QB * nbp, LANES), lambda b, p, i: (b, p, i, 0))],
        out_shape=[jax.ShapeDtypeStruct((B, NP, 2 * SUBLANES, S), jnp.int32),
                   jax.ShapeDtypeStruct((B, NP, NB * nbp, LANES), F32)],
        scratch_shapes=[pltpu.VMEM((nbp, LANES), F32)],
        compiler_params=_params("parallel", "parallel", "arbitrary"),
        name="moba_pick",
    )(h3, h3)

    OB = math.gcd(NB, MOBA_OWN_GROUP)
    blk_spec = lambda w: pl.BlockSpec((1, OB * BLK, w), lambda b, p, i, sl: (b, i, p))
    own_o, own_lse = pl.pallas_call(
        _moba_own_kernel,
        grid_spec=pltpu.PrefetchScalarGridSpec(
            num_scalar_prefetch=1,
            grid=(B, NP, NB // OB),
            in_specs=[blk_spec(LANES), blk_spec(2 * LANES)],
            out_specs=[blk_spec(LANES), blk_spec(LANES)]),
        out_shape=[jax.ShapeDtypeStruct((B, S, MIX_WIDTH), F32)] * 2,
        compiler_params=_params("parallel", "parallel", "parallel"),
        name="moba_own",
    )(slopes, h3, kv3)

    cnt = counts.reshape(B, NP, NB, nbp, LANES)[:, :, :, :NB, :2].astype(jnp.int32)
    cnt = cnt.transpose(0, 1, 2, 4, 3)
    base = jnp.cumsum(cnt, axis=2) - cnt
    total = jnp.sum(cnt, axis=2)
    padded = (total + TQ - 1) // TQ * TQ
    pend = jnp.cumsum(padded.reshape(-1))
    seg_start = (pend - padded.reshape(-1)).reshape(B, NP, 1, 2, NB)
    table = jnp.zeros((B, NP, NB, SUBLANES, LANES), F32).at[:, :, :, :2, :NB].set(
        (seg_start + base).astype(F32)).reshape(B, NP, NB * SUBLANES, LANES)
    n_seg = B * NP * 2 * NB
    max_tiles = -(-((T * NP * n_rep) // TQ + n_seg) // G) * G
    first_row = jnp.arange(max_tiles, dtype=jnp.int32) * TQ
    tile_seg = jnp.minimum(jnp.sum(pend[None, :] <= first_row[:, None], axis=1), n_seg - 1)
    tile_n = (tile_seg % NB).astype(jnp.int32)
    tile_h = ((tile_seg // NB) % 2).astype(jnp.int32)
    tile_p = ((tile_seg // (2 * NB)) % NP).astype(jnp.int32)
    tile_b = (tile_seg // (2 * NB * NP)).astype(jnp.int32)
    n_used = (pend[-1] // TQ).astype(jnp.int32).reshape(1)
    spare = max_tiles * TQ
    n_rows = spare + max(B * NP * MOBA_TOPK * n_rep * BLK, G * TQ)

    idx = pl.pallas_call(
        functools.partial(_moba_place_kernel, group=GI, spare=spare),
        grid=(B, NP, NB // GI),
        in_specs=[pl.BlockSpec((1, 1, 2 * SUBLANES, GI * BLK), lambda b, p, i: (b, p, 0, i)),
                  pl.BlockSpec((1, 1, GI * SUBLANES, LANES), lambda b, p, i: (b, p, i, 0))],
        out_specs=pl.BlockSpec((SC_IDX_ROWS, GI * BLK),
                               lambda b, p, i: (0, p * (T // (GI * BLK)) + b * (NB // GI) + i)),
        out_shape=jax.ShapeDtypeStruct((SC_IDX_ROWS, NP * T), jnp.int32),
        compiler_params=_params("parallel", "parallel", "parallel"),
        name="moba_place",
    )(selrank, table)

    qs = _sc_scatter_rows(q6.reshape(NP * T, LANES), idx, n_rep, n_rows)

    live = lambda t, tb, tp, th, tn, nu, sl: (jnp.where(t * G < nu[0], t, max_tiles // G), 0)
    kv_blk = lambda g: pl.BlockSpec(
        (1, BLK, 2 * LANES),
        lambda t, tb, tp, th, tn, nu, sl: (tb[t * G + g], tn[t * G + g], tp[t * G + g]))
    part = pl.pallas_call(
        functools.partial(_moba_tiles_kernel, group=G),
        grid_spec=pltpu.PrefetchScalarGridSpec(
            num_scalar_prefetch=6,
            grid=(max_tiles // G,),
            in_specs=[pl.BlockSpec((G * TQ, LANES), live)] + [kv_blk(g) for g in range(G)],
            out_specs=pl.BlockSpec((G * TQ, LANES), live)),
        out_shape=jax.ShapeDtypeStruct((n_rows, LANES), F32),
        compiler_params=_params("arbitrary"),
        name="moba_tiles",
    )(tile_b, tile_p, tile_h, tile_n, n_used, slopes, qs, *([kv3] * G))

    pg = _sc_gather_rows(part, idx[:n_rep].reshape(-1)).reshape(n_rep, NP, T, LANES)

    MB = math.gcd(NB, MOBA_MERGE_GROUP)
    blk = pl.BlockSpec((1, MB * BLK, LANES), lambda b, p, i: (b, i, p))
    return pl.pallas_call(
        _moba_merge_kernel,
        grid=(B, NP, NB // MB),
        in_specs=[blk, blk,
                  pl.BlockSpec((n_rep, 1, MB * BLK, LANES),
                               lambda b, p, i: (0, p, b * (NB // MB) + i, 0))],
        out_specs=blk,
        out_shape=jax.ShapeDtypeStruct((B, S, MIX_WIDTH), F32),
        compiler_params=_params("parallel", "parallel", "parallel"),
        name="moba_merge",
    )(own_o, own_lse, pg)


def _memkv_kernel(mem_ref, w_ref, kv_ref):
    kv_ref[0] = jnp.dot(mem_ref[0].astype(BF16), w_ref[...].astype(BF16),
                        preferred_element_type=F32).astype(BF16)


def _memkv(mem, w_kv):
    B, M, D = mem.shape
    N = w_kv.shape[1]
    return pl.pallas_call(
        _memkv_kernel,
        grid=(B,),
        in_specs=[pl.BlockSpec((1, M, D), lambda b: (b, 0, 0)),
                  pl.BlockSpec((D, N), lambda b: (0, 0))],
        out_specs=pl.BlockSpec((1, M, N), lambda b: (b, 0, 0)),
        out_shape=jax.ShapeDtypeStruct((B, M, N), BF16),
        compiler_params=_params("parallel"),
        name="memkv",
    )(mem, w_kv)


def _layer_norm(z, g, b):
    mu = jnp.mean(z, axis=-1, keepdims=True)
    zc = z - mu
    var = jnp.mean(zc * zc, axis=-1, keepdims=True)
    return zc * lax.rsqrt(var + LN_EPS) * g + b


def _post_kernel(x_ref, mix_ref, mq_ref, kv_ref, wo_ref, g_ref, b_ref, wr_ref, br_ref,
                 x1_ref, x1s_ref, idx_ref, gate_ref, rank_ref, cnt_ref, run_ref, *, alpha):
    tm = x_ref.shape[0]
    mq = mq_ref[...]
    kv = kv_ref[0]
    km = kv[:, :MEM_WIDTH]
    vm = kv[:, MEM_WIDTH:]
    lane = lax.broadcasted_iota(jnp.int32, (tm, MEM_WIDTH), 1)
    scale = MEM_HEAD_DIM ** -0.5
    mo = jnp.zeros((tm, MEM_WIDTH), F32)
    for hd in range(MEM_HEADS):
        head = (lane >> 6) == hd
        qh = jnp.where(head, mq * scale, 0.0).astype(BF16)
        s = lax.dot_general(qh, km, _NT, preferred_element_type=F32)
        m = jnp.max(s, axis=1, keepdims=True)
        p = jnp.exp(s - m)
        l = jnp.sum(p, axis=1, keepdims=True)
        oh = jnp.dot(p.astype(BF16), vm, preferred_element_type=F32) / l
        mo = jnp.where(head, oh, mo)

    y = jnp.dot(mix_ref[...].astype(BF16), wo_ref[:MIX_WIDTH, :], preferred_element_type=F32)
    y = y + jnp.dot(mo.astype(BF16), wo_ref[MIX_WIDTH:, :], preferred_element_type=F32)
    x1 = _layer_norm(alpha * x_ref[...] + y, g_ref[...], b_ref[...])
    x1_ref[...] = x1
    _store_subrows(x1s_ref, x1)

    x_hi = x1.astype(BF16)
    x_lo = (x1 - x_hi.astype(F32)).astype(BF16)
    hi = jnp.dot(x_hi, wr_ref[...], preferred_element_type=F32)
    lo = jnp.dot(x_lo, wr_ref[:, :LANES], preferred_element_type=F32)
    logits = hi[:, :LANES] + hi[:, LANES:] + lo
    g = logits.T[:N_EXPERTS] + br_ref[...]
    erow = lax.broadcasted_iota(jnp.int32, (N_EXPERTS, tm), 0)
    orow = lax.broadcasted_iota(jnp.int32, (SUBLANES, tm), 0)
    idx_out = jnp.zeros((SUBLANES, tm), jnp.int32)
    vals, picks = [], []
    chosen = jnp.zeros((N_EXPERTS, tm), F32)
    for kk in range(TOP_K):
        mx = jnp.max(g, axis=0, keepdims=True)
        idx = jnp.min(jnp.where(g == mx, erow, N_EXPERTS), axis=0, keepdims=True)
        idx_out = jnp.where(orow == kk, idx, idx_out)
        vals.append(mx)
        picks.append(idx)
        chosen = chosen + jnp.where(erow == idx, 1.0, 0.0)
        g = jnp.where(erow == idx, -jnp.inf, g)
    evs = [jnp.exp(v - vals[0]) for v in vals]
    den = sum(evs)
    gate_out = jnp.zeros((SUBLANES, tm), F32)
    for kk in range(TOP_K):
        gate_out = jnp.where(orow == kk, evs[kk] / den, gate_out)
    idx_ref[...] = idx_out
    gate_ref[...] = gate_out

    @pl.when(pl.program_id(0) == 0)
    def _():
        run_ref[...] = jnp.zeros_like(run_ref)

    ta = lax.broadcasted_iota(jnp.int32, (tm, tm), 0)
    tb = lax.broadcasted_iota(jnp.int32, (tm, tm), 1)
    before = jnp.where(ta < tb, 1.0, 0.0).astype(BF16)
    earlier = jnp.dot(chosen.astype(BF16), before, preferred_element_type=F32) + run_ref[:, :1]
    rank_out = jnp.zeros((SUBLANES, tm), jnp.int32)
    for kk in range(TOP_K):
        rank = jnp.sum(jnp.where(erow == picks[kk], earlier, 0.0), axis=0, keepdims=True)
        rank_out = jnp.where(orow == kk, rank.astype(jnp.int32), rank_out)
    rank_ref[...] = rank_out
    run_ref[...] = run_ref[...] + jnp.sum(chosen, axis=1, keepdims=True)
    cnt_ref[...] = run_ref[...]


def _post_mixer(x2, mix2, h2, kv, w_o_bf16, ln_g, ln_b, w_router, b_router, seq_len, alpha):
    T, D = x2.shape
    tm = POST_ROWS
    N = h2.shape[1]
    M = kv.shape[1]
    mq_col = (N - MEM_WIDTH) // MEM_WIDTH
    tiles_per_seq = seq_len // tm
    wr = jnp.zeros((D, LANES), F32).at[:, :N_EXPERTS].set(w_router)
    wr_hi = wr.astype(BF16)
    wr = jnp.concatenate([wr_hi, (wr - wr_hi.astype(F32)).astype(BF16)], axis=1)
    br = b_router.reshape(N_EXPERTS, 1)
    row = lambda n: pl.BlockSpec((tm, n), lambda i: (i, 0))
    full = lambda a, b: pl.BlockSpec((a, b), lambda i: (0, 0))
    per_token = pl.BlockSpec((SUBLANES, tm), lambda i: (0, i))
    return pl.pallas_call(
        functools.partial(_post_kernel, alpha=alpha),
        grid=(T // tm,),
        in_specs=[row(D), row(MIX_WIDTH),
                  pl.BlockSpec((tm, MEM_WIDTH), lambda i: (i, mq_col)),
                  pl.BlockSpec((1, M, 2 * MEM_WIDTH), lambda i: (i // tiles_per_seq, 0, 0)),
                  full(D, D), full(1, D), full(1, D),
                  full(D, 2 * LANES), full(N_EXPERTS, 1)],
        out_specs=[row(D), pl.BlockSpec((tm * (D // LANES), LANES), lambda i: (i, 0)),
                   per_token, per_token, per_token, full(N_EXPERTS, LANES)],
        out_shape=[jax.ShapeDtypeStruct((T, D), F32),
                   jax.ShapeDtypeStruct((T * (D // LANES), LANES), F32),
                   jax.ShapeDtypeStruct((SUBLANES, T), jnp.int32),
                   jax.ShapeDtypeStruct((SUBLANES, T), F32),
                   jax.ShapeDtypeStruct((SUBLANES, T), jnp.int32),
                   jax.ShapeDtypeStruct((N_EXPERTS, LANES), F32)],
        scratch_shapes=[pltpu.VMEM((N_EXPERTS, LANES), F32)],
        compiler_params=_params("arbitrary"),
        name="post_mixer",
    )(x2, mix2, h2, kv, w_o_bf16, ln_g.reshape(1, D), ln_b.reshape(1, D), wr, br)


def _sc_mesh():
    return plsc.VectorSubcoreMesh(core_axis_name="core", subcore_axis_name="subcore")


def _sc_scatter_rows(rows, idx, n_rep, n_out):
    R, W = rows.shape

    @functools.partial(pl.kernel, out_type=jax.ShapeDtypeStruct((n_out, W), rows.dtype),
                       mesh=_sc_mesh(), scratch_types=[])
    def scatter(x_hbm, i_hbm, o_hbm):
        def body(x_vmem, i_vmem):
            for r in range(n_rep):
                pltpu.sync_copy(x_vmem, o_hbm.at[i_vmem.at[r]])

        pltpu.emit_pipeline(
            body, grid=(R // SC_WINDOW,),
            in_specs=[pl.BlockSpec((SC_WINDOW, W), lambda i: (i, 0)),
                      pl.BlockSpec((SC_IDX_ROWS, SC_WINDOW), lambda i: (0, i))],
            out_specs=[], core_axis_name=("core", "subcore"),
            dimension_semantics=(pltpu.PARALLEL,), trace_scopes=False)(x_hbm, i_hbm)

    return scatter(rows, idx)


def _sc_gather_rows(table, idx):
    n = idx.shape[0]
    W = table.shape[1]

    @functools.partial(pl.kernel, out_type=jax.ShapeDtypeStruct((n, W), table.dtype),
                       mesh=_sc_mesh(), scratch_types=[])
    def gather(t_hbm, i_hbm, o_hbm):
        def body(i_vmem, o_vmem):
            pltpu.sync_copy(t_hbm.at[i_vmem.at[0]], o_vmem)

        pltpu.emit_pipeline(
            body, grid=(n // SC_WINDOW,),
            in_specs=[pl.BlockSpec((1, SC_WINDOW), lambda i: (0, i))],
            out_specs=[pl.BlockSpec((SC_WINDOW, W), lambda i: (i, 0))],
            core_axis_name=("core", "subcore"),
            dimension_semantics=(pltpu.PARALLEL,), trace_scopes=False)(i_hbm, o_hbm)

    return gather(table, idx.reshape(1, n))


def _sc_workers():
    info = pltpu.get_tpu_info().sparse_core
    return info.num_cores, info.num_cores * info.num_subcores


def _sc_scatter_slabs(rows, idx, n_rep, n_out):
    R, S, W = rows.shape
    n_cores, n_workers = _sc_workers()
    per_worker = (R // SC_WINDOW) // n_workers
    assert per_worker * n_workers * SC_WINDOW == R

    @functools.partial(pl.kernel, out_type=jax.ShapeDtypeStruct((n_out, S, W), rows.dtype),
                       mesh=_sc_mesh(),
                       scratch_types=[pltpu.VMEM((SC_IDX_ROWS, SC_WINDOW), jnp.int32),
                                      pltpu.VMEM((SC_CHUNK, S, W), rows.dtype)])
    def scatter(x_hbm, i_hbm, o_hbm, ibuf, buf):
        wid = lax.axis_index("subcore") * n_cores + lax.axis_index("core")

        @pl.loop(0, per_worker)
        def _(s):
            first = (wid * per_worker + s) * SC_WINDOW
            pltpu.sync_copy(i_hbm.at[:, pl.ds(first, SC_WINDOW)], ibuf)
            for c in range(SC_WINDOW // SC_CHUNK):
                pltpu.sync_copy(x_hbm.at[pl.ds(first + c * SC_CHUNK, SC_CHUNK)], buf)
                for r in range(n_rep):
                    pltpu.sync_copy(buf, o_hbm.at[ibuf.at[r, pl.ds(c * SC_CHUNK, SC_CHUNK)]])

    return scatter(rows, idx)


def _sc_gather_slabs(table, idx):
    n = idx.shape[0]
    S, W = table.shape[1:]
    n_cores, n_workers = _sc_workers()
    per_worker = (n // SC_WINDOW) // n_workers
    assert per_worker * n_workers * SC_WINDOW == n
    n_chunks = SC_WINDOW // SC_CHUNK

    @functools.partial(pl.kernel, out_type=jax.ShapeDtypeStruct((n, S, W), table.dtype),
                       mesh=_sc_mesh(),
                       scratch_types=[pltpu.VMEM((1, SC_WINDOW), jnp.int32),
                                      pltpu.VMEM((2, SC_CHUNK, S, W), table.dtype),
                                      pltpu.SemaphoreType.DMA((2,)), pltpu.SemaphoreType.DMA((2,))])
    def gather(t_hbm, i_hbm, o_hbm, ibuf, buf, fetch_sem, store_sem):
        wid = lax.axis_index("subcore") * n_cores + lax.axis_index("core")

        @pl.loop(0, per_worker)
        def _(s):
            blk = wid * per_worker + s
            pltpu.sync_copy(i_hbm.at[pl.ds(blk, 1)], ibuf)

            def fetch(c):
                return pltpu.make_async_copy(
                    t_hbm.at[ibuf.at[0, pl.ds(c * SC_CHUNK, SC_CHUNK)]], buf.at[c % 2],
                    fetch_sem.at[c % 2])

            def store(c):
                return pltpu.make_async_copy(
                    buf.at[c % 2], o_hbm.at[pl.ds(blk * SC_WINDOW + c * SC_CHUNK, SC_CHUNK)],
                    store_sem.at[c % 2])

            fetch(0).start()
            for c in range(n_chunks):
                if c + 1 < n_chunks:
                    if c >= 1:
                        store(c - 1).wait()
                    fetch(c + 1).start()
                fetch(c).wait()
                store(c).start()
            store(n_chunks - 2).wait()
            store(n_chunks - 1).wait()

    return gather(table, idx.reshape(n // SC_WINDOW, SC_WINDOW))


def _route(top_idx, rank, counts):
    rb = MOE_ROWS
    n_tokens = top_idx.shape[1]
    tk = n_tokens * TOP_K
    padded = (counts + rb - 1) // rb * rb
    pend = jnp.cumsum(padded)
    pstart = pend - padded
    experts = jnp.arange(N_EXPERTS, dtype=jnp.int32)
    start = jnp.sum(jnp.where(top_idx[:TOP_K, :, None] == experts, pstart, 0), axis=2)
    dest = (start + rank[:TOP_K]).astype(jnp.int32)
    n_blocks = tk // rb + N_EXPERTS
    first_row = jnp.arange(n_blocks, dtype=jnp.int32) * rb
    block_e = jnp.minimum(jnp.sum(pend[None, :] <= first_row[:, None], axis=1),
                          N_EXPERTS - 1).astype(jnp.int32)
    n_used = (pend[-1] // rb).astype(jnp.int32).reshape(1)
    return dest, block_e, n_used


def _dispatch(x1s, dest, n_rows, sub):
    T = dest.shape[1]
    idx = jnp.concatenate([dest, jnp.zeros((SC_IDX_ROWS - TOP_K, T), jnp.int32)], axis=0)
    xs = _sc_scatter_slabs(x1s.reshape(T, sub, LANES), idx, TOP_K, n_rows)
    return xs.reshape(n_rows * sub, LANES)


def _expert_kernel(be_ref, nu_ref, x_ref, wg_ref, bg_ref, wu_ref, bu_ref, wd_ref, bd_ref,
                   y_ref, wgb, wub, wdb):
    i = pl.program_id(0)
    prev = be_ref[jnp.maximum(i - 1, 0)]

    @pl.when((i == 0) | (be_ref[i] != prev))
    def _():
        wgb[...] = wg_ref[0, 0].astype(BF16)
        wub[...] = wu_ref[0, 0].astype(BF16)
        wdb[...] = wd_ref[0, 0].astype(BF16)

    @pl.when(i < nu_ref[0])
    def _():
        sub = wgb.shape[0] // LANES
        xb = _load_subrows(x_ref, x_ref.shape[0] // sub, sub).astype(BF16)
        gate = jnp.dot(xb, wgb[...], preferred_element_type=F32) + bg_ref[0, 0]
        gate = jnp.minimum(gate, SWIGLU_LIMIT)
        up = jnp.dot(xb, wub[...], preferred_element_type=F32) + bu_ref[0, 0]
        up = jnp.clip(up, -SWIGLU_LIMIT, SWIGLU_LIMIT)
        hid = gate * _sigmoid(SWIGLU_ALPHA * gate) * (up + 1.0)
        y = jnp.dot(hid.astype(BF16), wdb[...], preferred_element_type=F32) + bd_ref[0, 0]
        _store_subrows(y_ref, y)


def _experts(xs, block_e, n_used, layer, w_gate, b_gate, w_up, b_up, w_down, b_down):
    rb = MOE_ROWS
    n_blocks = block_e.shape[0]
    E, D, F = w_gate.shape[1:]
    sub = D // LANES
    wspec = lambda a, b: pl.BlockSpec((1, 1, a, b), lambda i, be, nu: (layer, be[i], 0, 0))
    live = lambda i, be, nu: (jnp.where(i < nu[0], i, n_blocks), 0)
    grid_spec = pltpu.PrefetchScalarGridSpec(
        num_scalar_prefetch=2,
        grid=(n_blocks,),
        in_specs=[pl.BlockSpec((rb * sub, LANES), live),
                  wspec(D, F), wspec(1, F), wspec(D, F), wspec(1, F), wspec(F, D), wspec(1, D)],
        out_specs=pl.BlockSpec((rb * sub, LANES), live),
        scratch_shapes=[pltpu.VMEM((D, F), BF16), pltpu.VMEM((D, F), BF16),
                        pltpu.VMEM((F, D), BF16)],
    )
    depth = w_gate.shape[0]
    return pl.pallas_call(
        _expert_kernel,
        grid_spec=grid_spec,
        out_shape=jax.ShapeDtypeStruct(xs.shape, F32),
        compiler_params=_params("arbitrary"),
        name="experts",
    )(block_e, n_used, xs, w_gate, b_gate.reshape(depth, E, 1, F),
      w_up, b_up.reshape(depth, E, 1, F), w_down, b_down.reshape(depth, E, 1, D))


def _combine_kernel(x1_ref, gate_ref, y_ref, g_ref, b_ref, o_ref, *, alpha):
    tm, D = x1_ref.shape
    sub = D // LANES
    gates = gate_ref[...].T
    f = jnp.zeros(x1_ref.shape, F32)
    for kk in range(TOP_K):
        f = f + gates[:, kk:kk + 1] * _load_subrows(y_ref, tm, sub, kk * sub, TOP_K * sub)
    o_ref[...] = _layer_norm(alpha * x1_ref[...] + f, g_ref[...], b_ref[...])


def _combine(x1, gates, dest, y_rows, ln_g, ln_b, alpha):
    T, D = x1.shape
    sub = D // LANES
    tm = COMBINE_ROWS
    yg = _sc_gather_slabs(y_rows.reshape(-1, sub, LANES), dest.T.reshape(-1)).reshape(-1, LANES)
    row = lambda n: pl.BlockSpec((tm, n), lambda i: (i, 0))
    full = lambda a, b: pl.BlockSpec((a, b), lambda i: (0, 0))
    return pl.pallas_call(
        functools.partial(_combine_kernel, alpha=alpha),
        grid=(T // tm,),
        in_specs=[row(D), pl.BlockSpec((SUBLANES, tm), lambda i: (0, i)),
                  pl.BlockSpec((tm * TOP_K * sub, LANES), lambda i: (i, 0)),
                  full(1, D), full(1, D)],
        out_specs=row(D),
        out_shape=jax.ShapeDtypeStruct((T, D), F32),
        compiler_params=_params("parallel"),
        name="combine",
    )(x1, gates, yg, ln_g.reshape(1, D), ln_b.reshape(1, D))


def kernel(x, mem, w_in_hgrn, hgrn_lb_logits, hgrn_norm_g, w_in_moba, w_mem_kv, w_o,
           ln_mix_g, ln_mix_b, w_router, b_router, w_gate, b_gate, w_up, b_up,
           w_down, b_down, ln_ffn_g, ln_ffn_b):
    B, S, D = x.shape
    T = B * S
    depth = w_o.shape[0]
    alpha = (2 * depth) ** 0.25

    p_lb = jax.nn.softmax(hgrn_lb_logits.astype(F32), axis=0)
    lower_bounds = jnp.cumsum(p_lb, axis=0) - p_lb[0]

    x2 = x.reshape(T, D)
    for layer in range(depth):
        j = layer // 2
        if layer % 2 == 0:
            (h2,) = _inproj(x2, w_in_hgrn[j].astype(BF16), for_moba=False)
            mix = _hgrn_mixer(h2.reshape(B, S, -1), lower_bounds[j], hgrn_norm_g[j])
        else:
            h2, kv2, q6 = _inproj(x2, w_in_moba[j].astype(BF16), for_moba=True)
            mix = _moba_mixer(h2.reshape(B, S, -1), kv2.reshape(B, S, -1), q6)
        kv = _memkv(mem, w_mem_kv[layer])
        x1, x1s, top_idx, gates, rank, counts = _post_mixer(
            x2, mix.reshape(T, MIX_WIDTH), h2, kv, w_o[layer].astype(BF16),
            ln_mix_g[layer], ln_mix_b[layer], w_router[layer], b_router[layer], S, alpha)
        dest, block_e, n_used = _route(top_idx, rank, counts[:, 0].astype(jnp.int32))
        xs = _dispatch(x1s, dest, (block_e.shape[0] + 1) * MOE_ROWS, D // LANES)
        y_rows = _experts(xs, block_e, n_used, layer, w_gate, b_gate, w_up, b_up, w_down, b_down)
        x2 = _combine(x1, gates, dest, y_rows, ln_ffn_g[layer], ln_ffn_b[layer], alpha)
    return x2.reshape(B, S, D)
```

```python
import functools
import math

import jax
import jax.numpy as jnp
from jax import lax
from jax.experimental import pallas as pl
from jax.experimental.pallas import tpu as pltpu
from jax.experimental.pallas import tpu_sc as plsc

MIX_WIDTH = 768
MEM_HEADS = 4
MEM_HEAD_DIM = 64
MEM_WIDTH = MEM_HEADS * MEM_HEAD_DIM
HGRN_HEADS = 6
HGRN_DK = 128
MOBA_HEADS = 12
MOBA_HEAD_DIM = 64
MOBA_BLOCK = 256
MOBA_TOPK = 3
N_EXPERTS = 32
TOP_K = 4
SWIGLU_ALPHA = 1.702
SWIGLU_LIMIT = 7.0
LN_EPS = 1e-5
RMS_EPS = 1e-6

LANES = 128
SUBLANES = 8
VMEM_LIMIT_BYTES = 56 * 1024 * 1024

INPROJ_ROWS = 512
HGRN_CHUNK = 64
HGRN_ROWS = 1024
POST_ROWS = 512
MOBA_TILE = 256
MOBA_TILE_GROUP = 16
MOBA_PLACE_GROUP = 8
MOBA_PICK_GROUP = 4
MOBA_MERGE_GROUP = 8
MOBA_OWN_GROUP = 8
MOE_ROWS = 512
COMBINE_ROWS = 512
SC_WINDOW = 128
SC_IDX_ROWS = 8
SC_CHUNK = 32

BF16 = jnp.bfloat16
F32 = jnp.float32

_NT = (((1,), (1,)), ((), ()))
_TN = (((0,), (0,)), ((), ()))


def _alibi_slope_list(n):
    def pow2(m):
        start = 2.0 ** (-(2.0 ** -(math.log2(m) - 3)))
        return [start ** (i + 1) for i in range(m)]
    if math.log2(n).is_integer():
        return pow2(n)
    c = 2 ** math.floor(math.log2(n))
    return pow2(c) + _alibi_slope_list(2 * c)[0::2][:n - c]


def _sigmoid(x):
    return 1.0 / (1.0 + jnp.exp(-x))


def _params(*sem):
    return pltpu.CompilerParams(dimension_semantics=sem, vmem_limit_bytes=VMEM_LIMIT_BYTES)


def _store_subrows(ref, value, first=0, stride=None):
    sub = value.shape[1] // LANES
    stride = stride or sub
    for c in range(sub):
        ref[pl.ds(first + c, value.shape[0], stride=stride), :] = value[:, c * LANES:(c + 1) * LANES]


def _load_subrows(ref, rows, sub, first=0, stride=None):
    stride = stride or sub
    return jnp.concatenate(
        [ref[pl.ds(first + c, rows, stride=stride), :] for c in range(sub)], axis=1)


def _inproj_kernel(x_ref, w_ref, h_ref, *moba_refs):
    h = jnp.dot(x_ref[...].astype(BF16), w_ref[...], preferred_element_type=F32)
    h_ref[...] = h
    if moba_refs:
        kv_ref, q6_ref = moba_refs
        n_pairs = q6_ref.shape[0]
        for p in range(n_pairs):
            q6_ref[p] = h[:, p * LANES:(p + 1) * LANES] * (MOBA_HEAD_DIM ** -0.5)
            for part in range(2):
                col = (1 + part) * MIX_WIDTH + p * LANES
                kv_ref[:, (2 * p + part) * LANES:(2 * p + part + 1) * LANES] = (
                    h[:, col:col + LANES].astype(BF16))


def _inproj(x2, w_bf16, for_moba):
    T, D = x2.shape
    N = w_bf16.shape[1]
    tm = INPROJ_ROWS
    out_shape = [jax.ShapeDtypeStruct((T, N), F32)]
    out_specs = [pl.BlockSpec((tm, N), lambda i: (i, 0))]
    if for_moba:
        NP = MOBA_HEADS // 2
        out_shape += [jax.ShapeDtypeStruct((T, 2 * MIX_WIDTH), BF16),
                      jax.ShapeDtypeStruct((NP, T, LANES), F32)]
        out_specs += [pl.BlockSpec((tm, 2 * MIX_WIDTH), lambda i: (i, 0)),
                      pl.BlockSpec((NP, tm, LANES), lambda i: (0, i, 0))]
    return pl.pallas_call(
        _inproj_kernel,
        grid=(T // tm,),
        in_specs=[pl.BlockSpec((tm, D), lambda i: (i, 0)),
                  pl.BlockSpec((D, N), lambda i: (0, 0))],
        out_specs=out_specs,
        out_shape=out_shape,
        compiler_params=_params("parallel"),
        name="inproj",
    )(x2, w_bf16)


def _cumsum_rows(x, row):
    n = x.shape[0]
    sh = 1
    while sh < n:
        x = x + jnp.where(row >= sh, pltpu.roll(x, sh, 0), 0.0)
        sh *= 2
    return x


def _bcast_row(a, group, r):
    n = a.shape[0]
    a3 = a.reshape(n // group, group, LANES)
    return jnp.broadcast_to(a3[:, r:r + 1, :], a3.shape).reshape(n, LANES)


def _hgrn_chunk(qr, fr, v, gr, lb, ng, e_sum, st_t):
    C = qr.shape[0]
    row = lax.broadcasted_iota(jnp.int32, (C, LANES), 0)
    rr = lax.broadcasted_iota(jnp.int32, (C, C), 0)
    cc = lax.broadcasted_iota(jnp.int32, (C, C), 1)

    q = qr * _sigmoid(qr)
    forget = lb + (1.0 - lb) * _sigmoid(fr)
    k = 1.0 - forget
    G = _cumsum_rows(jnp.log(forget), row)

    z = jnp.log(k) - G
    parts = []
    for s in range(SUBLANES):
        parts.append((q * jnp.exp(jnp.minimum(G + _bcast_row(z, SUBLANES, s), 0.0))).astype(BF16))
    a_diag = jnp.dot(jnp.concatenate(parts, axis=1), e_sum, preferred_element_type=F32)
    A = jnp.where(((rr >> 3) == (cc >> 3)) & (cc <= rr), a_diag, 0.0)

    m = SUBLANES
    while m < C:
        lg = int(math.log2(m))
        Gr = _bcast_row(G, 2 * m, m - 1)
        second = ((row >> lg) & 1) == 1
        qm = q * jnp.exp(jnp.where(second, G - Gr, -jnp.inf))
        km = k * jnp.exp(jnp.where(second, -jnp.inf, Gr - G))
        am = lax.dot_general(qm.astype(BF16), km.astype(BF16), _NT, preferred_element_type=F32)
        A = A + jnp.where((rr >> (lg + 1)) == (cc >> (lg + 1)), am, 0.0)
        m *= 2

    vb = v.astype(BF16)
    o = jnp.dot(A.astype(BF16), vb, preferred_element_type=F32)
    o = o + lax.dot_general((q * jnp.exp(G)).astype(BF16), st_t.astype(BF16), _NT,
                            preferred_element_type=F32)
    g_end = G[C - 1:C, :]
    kd = (k * jnp.exp(g_end - G)).astype(BF16)
    st_new = st_t * jnp.exp(g_end) + lax.dot_general(vb, kd, _TN, preferred_element_type=F32)

    ms = jnp.mean(o * o, axis=-1, keepdims=True)
    out = o * lax.rsqrt(ms + RMS_EPS) * ng * _sigmoid(gr)
    return out, st_new


def _hgrn_kernel(q_ref, f_ref, i_ref, g_ref, lb_ref, ng_ref, e_ref, o_ref, st_ref, *, chunk):
    @pl.when(pl.program_id(2) == 0)
    def _():
        st_ref[...] = jnp.zeros_like(st_ref)

    lb = lb_ref[0]
    ng = ng_ref[...]
    e_sum = e_ref[...]
    n_chunks = q_ref.shape[1] // chunk
    for c in range(n_chunks):
        sl = pl.ds(c * chunk, chunk)
        out, st_new = _hgrn_chunk(q_ref[0, sl, :], f_ref[0, sl, :], i_ref[0, sl, :],
                                  g_ref[0, sl, :], lb, ng, e_sum, st_ref[...])
        st_ref[...] = st_new
        o_ref[0, sl, :] = out


def _hgrn_mixer(h3, lb, norm_g):
    B, S, _ = h3.shape
    ts = min(HGRN_ROWS, S)
    C = HGRN_CHUNK
    H = HGRN_HEADS
    e_sum = (jnp.arange(SUBLANES * LANES)[:, None] // LANES == jnp.arange(C)[None, :] % SUBLANES
             ).astype(BF16)
    col = lambda off: pl.BlockSpec((1, ts, LANES), lambda b, h, s, off=off: (b, s, off + h))
    return pl.pallas_call(
        functools.partial(_hgrn_kernel, chunk=C),
        grid=(B, H, S // ts),
        in_specs=[col(0), col(H), col(2 * H), col(3 * H),
                  pl.BlockSpec((1, 1, LANES), lambda b, h, s: (h, 0, 0)),
                  pl.BlockSpec((1, LANES), lambda b, h, s: (0, 0)),
                  pl.BlockSpec((SUBLANES * LANES, C), lambda b, h, s: (0, 0))],
        out_specs=pl.BlockSpec((1, ts, LANES), lambda b, h, s: (b, s, h)),
        out_shape=jax.ShapeDtypeStruct((B, S, MIX_WIDTH), F32),
        scratch_shapes=[pltpu.VMEM((HGRN_DK, HGRN_DK), F32)],
        compiler_params=_params("parallel", "parallel", "arbitrary"),
        name="hgrn",
    )(h3, h3, h3, h3, lb.reshape(H, 1, LANES), norm_g.reshape(1, LANES), e_sum)


def _moba_pick_kernel(q_ref, kf_ref, sr_ref, cnt_ref, kmean_ref, *, qblocks):
    BLK = MOBA_BLOCK
    W = qblocks * BLK
    nbp = kmean_ref.shape[0]
    i0 = pl.program_id(2) * qblocks

    @pl.when(pl.program_id(2) == 0)
    def _():
        kmean_ref[...] = jnp.zeros_like(kmean_ref)

    for j in range(qblocks):
        kmean_ref[pl.ds(i0 + j, 1), :] = jnp.mean(kf_ref[0, j * BLK:(j + 1) * BLK, :], axis=0,
                                                  keepdims=True)
    km = kmean_ref[...]
    lane_k = lax.broadcasted_iota(jnp.int32, (nbp, LANES), 1)
    km2 = jnp.concatenate([jnp.where(lane_k < MOBA_HEAD_DIM, km, 0.0),
                           jnp.where(lane_k >= MOBA_HEAD_DIM, km, 0.0)], axis=0)
    gate = lax.dot_general(km2, q_ref[0], _NT, precision=lax.Precision.HIGHEST,
                           preferred_element_type=F32).reshape(2, nbp, W)
    nblk = lax.broadcasted_iota(jnp.int32, (2, nbp, W), 1)
    qi = i0 + (lax.broadcasted_iota(jnp.int32, (2, 1, W), 2) >> int(math.log2(BLK)))
    g = jnp.where(nblk < qi, gate, -jnp.inf)
    picks = []
    for _ in range(MOBA_TOPK):
        mx = jnp.max(g, axis=1, keepdims=True)
        idx = jnp.min(jnp.where(g == mx, nblk, nbp), axis=1, keepdims=True)
        picks.append((idx, (mx > -jnp.inf) & (idx < qi)))
        g = jnp.where(nblk == idx, -jnp.inf, g)
    chosen = jnp.zeros((2, nbp, W), F32)
    for idx, valid in picks:
        chosen = chosen + jnp.where((nblk == idx) & valid, 1.0, 0.0)
    qa = lax.broadcasted_iota(jnp.int32, (W, W), 0)
    qc = lax.broadcasted_iota(jnp.int32, (W, W), 1)
    shift = int(math.log2(BLK))
    before = jnp.where((qa < qc) & ((qa >> shift) == (qc >> shift)), 1.0, 0.0).astype(BF16)
    earlier = jnp.dot(chosen.reshape(2 * nbp, W).astype(BF16), before,
                      preferred_element_type=F32).reshape(2, nbp, W)
    orow = lax.broadcasted_iota(jnp.int32, (2 * SUBLANES, W), 0)
    out = jnp.zeros((2 * SUBLANES, W), jnp.int32)
    for j, (idx, valid) in enumerate(picks):
        rank = jnp.sum(jnp.where(nblk == idx, earlier, 0.0), axis=1, keepdims=True).astype(jnp.int32)
        sel = jnp.where(valid, idx, -1)
        for hh in range(2):
            rep = hh * MOBA_TOPK + j
            out = jnp.where(orow == rep, sel[hh], out)
            out = jnp.where(orow == SUBLANES + rep, rank[hh], out)
    sr_ref[0, 0] = out
    lane_c = lax.broadcasted_iota(jnp.int32, (nbp, LANES), 1)
    for j in range(qblocks):
        per_block = jnp.sum(chosen[:, :, j * BLK:(j + 1) * BLK], axis=2, keepdims=True)
        cnt_ref[0, 0, j * nbp:(j + 1) * nbp, :] = jnp.where(
            lane_c == 0, per_block[0], jnp.where(lane_c == 1, per_block[1], 0.0))


def _moba_place_kernel(sr_ref, tab_ref, idx_ref, *, group, spare):
    BLK = MOBA_BLOCK
    n_rep = 2 * MOBA_TOPK
    b = pl.program_id(0)
    p = pl.program_id(1)
    ig = pl.program_id(2)
    nrow = lax.broadcasted_iota(jnp.int32, (LANES, BLK), 0)
    orow = lax.broadcasted_iota(jnp.int32, (SC_IDX_ROWS, BLK), 0)
    qpos = lax.broadcasted_iota(jnp.int32, (1, BLK), 1)
    for g in range(group):
        blk = sr_ref[0, 0, :, g * BLK:(g + 1) * BLK]
        tab_t = tab_ref[0, 0, g * SUBLANES:(g + 1) * SUBLANES, :].T
        early = jnp.minimum(ig * group + g, MOBA_TOPK - 1)
        out = jnp.zeros((SC_IDX_ROWS, BLK), jnp.int32)
        for rep in range(n_rep):
            hh = rep // MOBA_TOPK
            sel = blk[rep:rep + 1, :]
            rank = blk[SUBLANES + rep:SUBLANES + rep + 1, :]
            start = jnp.sum(jnp.where(nrow == sel, tab_t[:, hh:hh + 1], 0.0), axis=0, keepdims=True)
            unused = spare + (((b * pl.num_programs(1) + p) * MOBA_TOPK + early) * n_rep + rep) * BLK
            dest = jnp.where(sel >= 0, start.astype(jnp.int32) + rank, unused + qpos)
            out = jnp.where(orow == rep, dest, out)
        idx_ref[:, g * BLK:(g + 1) * BLK] = out


def _moba_tiles_kernel(tb_ref, tp_ref, th_ref, tn_ref, nu_ref, sl_ref, q_ref, *refs, group):
    kv_refs, o_ref = refs[:group], refs[group]
    t = pl.program_id(0)
    tq = q_ref.shape[0] // group

    @pl.when(t * group < nu_ref[0])
    def _():
        lane = lax.broadcasted_iota(jnp.int32, (tq, LANES), 1)
        kpos = lax.broadcasted_iota(jnp.int32, (1, MOBA_BLOCK), 1)
        for g in range(group):
            tt = t * group + g
            hh = th_ref[tt]
            slope = sl_ref[2 * tp_ref[tt] + hh]
            head = (lane >> 6) == hh
            rows = pl.ds(g * tq, tq)
            q = jnp.where(head, q_ref[rows, :], 0.0).astype(BF16)
            s = lax.dot_general(q, kv_refs[g][0, :, :LANES], _NT, preferred_element_type=F32)
            s = s + slope * (kpos + tn_ref[tt] * MOBA_BLOCK).astype(F32)
            m = jnp.max(s, axis=1, keepdims=True)
            pr = jnp.exp(s - m)
            l = jnp.sum(pr, axis=1, keepdims=True)
            o = jnp.dot(pr.astype(BF16), kv_refs[g][0, :, LANES:], preferred_element_type=F32) / l
            o_ref[rows, :] = jnp.where(head, o, m + jnp.log(l))


def _moba_own_kernel(sl_ref, q_ref, kv_ref, o_ref, lse_ref):
    BLK = MOBA_BLOCK
    p = pl.program_id(1)
    n_blocks = q_ref.shape[1] // BLK
    lane = lax.broadcasted_iota(jnp.int32, (BLK, LANES), 1)
    rr = lax.broadcasted_iota(jnp.int32, (BLK, BLK), 0)
    cc = lax.broadcasted_iota(jnp.int32, (BLK, BLK), 1)
    first = lane < MOBA_HEAD_DIM
    for h in range(n_blocks):
        i = pl.program_id(2) * n_blocks + h
        rows = pl.ds(h * BLK, BLK)
        qf = q_ref[0, rows, :]
        k_own = kv_ref[0, rows, :LANES]
        v_own = kv_ref[0, rows, LANES:]
        key_pos = (lax.broadcasted_iota(jnp.int32, (1, BLK), 1) + i * BLK).astype(F32)
        outs, lses = [], []
        for hh in range(2):
            head = (lane >> 6) == hh
            slope = sl_ref[2 * p + hh]
            qh = jnp.where(head, qf * (MOBA_HEAD_DIM ** -0.5), 0.0).astype(BF16)
            s = lax.dot_general(qh, k_own, _NT, preferred_element_type=F32)
            s = jnp.where(cc <= rr, s + slope * key_pos, -jnp.inf)
            m = jnp.max(s, axis=1, keepdims=True)
            pr = jnp.exp(s - m)
            l = jnp.sum(pr, axis=1, keepdims=True)
            lses.append(m + jnp.log(l))
            outs.append(jnp.dot(pr.astype(BF16), v_own, preferred_element_type=F32) / l)
        o_ref[0, rows, :] = jnp.where(first, outs[0], outs[1])
        lse_ref[0, rows, :] = jnp.where(first, lses[0], lses[1])


def _moba_merge_kernel(o_ref, lse_ref, pg_ref, out_ref):
    BLK = MOBA_BLOCK
    first = lax.broadcasted_iota(jnp.int32, (BLK, LANES), 1) < MOBA_HEAD_DIM
    for h in range(o_ref.shape[1] // BLK):
        i = pl.program_id(2) * (o_ref.shape[1] // BLK) + h
        rows = pl.ds(h * BLK, BLK)
        lses = [lse_ref[0, rows, :]]
        vals = [o_ref[0, rows, :]]
        for j in range(MOBA_TOPK):
            p0 = pg_ref[j, 0, rows, :]
            p1 = pg_ref[MOBA_TOPK + j, 0, rows, :]
            has_block = j < i
            stats = pltpu.roll(jnp.where(first, p1, p0), MOBA_HEAD_DIM, 1)
            lses.append(jnp.where(has_block, stats, -jnp.inf))
            vals.append(jnp.where(has_block, jnp.where(first, p0, p1), 0.0))
        top = functools.reduce(jnp.maximum, lses)
        ws = [jnp.exp(x - top) for x in lses]
        out_ref[0, rows, :] = sum(w * v for w, v in zip(ws, vals)) / sum(ws)


def _moba_mixer(h3, kv3, q6):
    B, S, _ = h3.shape
    T = B * S
    BLK = MOBA_BLOCK
    TQ = MOBA_TILE
    G = MOBA_TILE_GROUP
    NB = S // BLK
    GI = math.gcd(NB, MOBA_PLACE_GROUP)
    nbp = -(-NB // SUBLANES) * SUBLANES
    NP = MOBA_HEADS // 2
    n_rep = 2 * MOBA_TOPK
    slopes = jnp.asarray(_alibi_slope_list(MOBA_HEADS), F32)

    QB = math.gcd(NB, MOBA_PICK_GROUP)
    selrank, counts = pl.pallas_call(
        functools.partial(_moba_pick_kernel, qblocks=QB),
        grid=(B, NP, NB // QB),
        in_specs=[pl.BlockSpec((1, QB * BLK, LANES), lambda b, p, i: (b, i, p)),
                  pl.BlockSpec((1, QB * BLK, LANES), lambda b, p, i: (b, i, NP + p))],
        out_specs=[pl.BlockSpec((1, 1, 2 * SUBLANES, QB * BLK), lambda b, p, i: (b, p, 0, i)),
                   pl.BlockSpec((1, 1, QB * nbp, LANES), lambda b, p, i: (b, p, i, 0))],
        out_shape=[jax.ShapeDtypeStruct((B, NP, 2 * SUBLANES, S), jnp.int32),
                   jax.ShapeDtypeStruct((B, NP, NB * nbp, LANES), F32)],
        scratch_shapes=[pltpu.VMEM((nbp, LANES), F32)],
        compiler_params=_params("parallel", "parallel", "arbitrary"),
        name="moba_pick",
    )(h3, h3)

    OB = math.gcd(NB, MOBA_OWN_GROUP)
    blk_spec = lambda w: pl.BlockSpec((1, OB * BLK, w), lambda b, p, i, sl: (b, i, p))
    own_o, own_lse = pl.pallas_call(
        _moba_own_kernel,
        grid_spec=pltpu.PrefetchScalarGridSpec(
            num_scalar_prefetch=1,
            grid=(B, NP, NB // OB),
            in_specs=[blk_spec(LANES), blk_spec(2 * LANES)],
            out_specs=[blk_spec(LANES), blk_spec(LANES)]),
        out_shape=[jax.ShapeDtypeStruct((B, S, MIX_WIDTH), F32)] * 2,
        compiler_params=_params("parallel", "parallel", "parallel"),
        name="moba_own",
    )(slopes, h3, kv3)

    cnt = counts.reshape(B, NP, NB, nbp, LANES)[:, :, :, :NB, :2].astype(jnp.int32)
    cnt = cnt.transpose(0, 1, 2, 4, 3)
    base = jnp.cumsum(cnt, axis=2) - cnt
    total = jnp.sum(cnt, axis=2)
    padded = (total + TQ - 1) // TQ * TQ
    pend = jnp.cumsum(padded.reshape(-1))
    seg_start = (pend - padded.reshape(-1)).reshape(B, NP, 1, 2, NB)
    table = jnp.zeros((B, NP, NB, SUBLANES, LANES), F32).at[:, :, :, :2, :NB].set(
        (seg_start + base).astype(F32)).reshape(B, NP, NB * SUBLANES, LANES)
    n_seg = B * NP * 2 * NB
    max_tiles = -(-((T * NP * n_rep) // TQ + n_seg) // G) * G
    first_row = jnp.arange(max_tiles, dtype=jnp.int32) * TQ
    tile_seg = jnp.minimum(jnp.sum(pend[None, :] <= first_row[:, None], axis=1), n_seg - 1)
    tile_n = (tile_seg % NB).astype(jnp.int32)
    tile_h = ((tile_seg // NB) % 2).astype(jnp.int32)
    tile_p = ((tile_seg // (2 * NB)) % NP).astype(jnp.int32)
    tile_b = (tile_seg // (2 * NB * NP)).astype(jnp.int32)
    n_used = (pend[-1] // TQ).astype(jnp.int32).reshape(1)
    spare = max_tiles * TQ
    n_rows = spare + max(B * NP * MOBA_TOPK * n_rep * BLK, G * TQ)

    idx = pl.pallas_call(
        functools.partial(_moba_place_kernel, group=GI, spare=spare),
        grid=(B, NP, NB // GI),
        in_specs=[pl.BlockSpec((1, 1, 2 * SUBLANES, GI * BLK), lambda b, p, i: (b, p, 0, i)),
                  pl.BlockSpec((1, 1, GI * SUBLANES, LANES), lambda b, p, i: (b, p, i, 0))],
        out_specs=pl.BlockSpec((SC_IDX_ROWS, GI * BLK),
                               lambda b, p, i: (0, p * (T // (GI * BLK)) + b * (NB // GI) + i)),
        out_shape=jax.ShapeDtypeStruct((SC_IDX_ROWS, NP * T), jnp.int32),
        compiler_params=_params("parallel", "parallel", "parallel"),
        name="moba_place",
    )(selrank, table)

    qs = _sc_scatter_rows(q6.reshape(NP * T, LANES), idx, n_rep, n_rows)

    live = lambda t, tb, tp, th, tn, nu, sl: (jnp.where(t * G < nu[0], t, max_tiles // G), 0)
    kv_blk = lambda g: pl.BlockSpec(
        (1, BLK, 2 * LANES),
        lambda t, tb, tp, th, tn, nu, sl: (tb[t * G + g], tn[t * G + g], tp[t * G + g]))
    part = pl.pallas_call(
        functools.partial(_moba_tiles_kernel, group=G),
        grid_spec=pltpu.PrefetchScalarGridSpec(
            num_scalar_prefetch=6,
            grid=(max_tiles // G,),
            in_specs=[pl.BlockSpec((G * TQ, LANES), live)] + [kv_blk(g) for g in range(G)],
            out_specs=pl.BlockSpec((G * TQ, LANES), live)),
        out_shape=jax.ShapeDtypeStruct((n_rows, LANES), F32),
        compiler_params=_params("arbitrary"),
        name="moba_tiles",
    )(tile_b, tile_p, tile_h, tile_n, n_used, slopes, qs, *([kv3] * G))

    pg = _sc_gather_rows(part, idx[:n_rep].reshape(-1)).reshape(n_rep, NP, T, LANES)

    MB = math.gcd(NB, MOBA_MERGE_GROUP)
    blk = pl.BlockSpec((1, MB * BLK, LANES), lambda b, p, i: (b, i, p))
    return pl.pallas_call(
        _moba_merge_kernel,
        grid=(B, NP, NB // MB),
        in_specs=[blk, blk,
                  pl.BlockSpec((n_rep, 1, MB * BLK, LANES),
                               lambda b, p, i: (0, p, b * (NB // MB) + i, 0))],
        out_specs=blk,
        out_shape=jax.ShapeDtypeStruct((B, S, MIX_WIDTH), F32),
        compiler_params=_params("parallel", "parallel", "parallel"),
        name="moba_merge",
    )(own_o, own_lse, pg)


def _memkv_kernel(mem_ref, w_ref, kv_ref):
    kv_ref[0] = jnp.dot(mem_ref[0].astype(BF16), w_ref[...].astype(BF16),
                        preferred_element_type=F32).astype(BF16)


def _memkv(mem, w_kv):
    B, M, D = mem.shape
    N = w_kv.shape[1]
    return pl.pallas_call(
        _memkv_kernel,
        grid=(B,),
        in_specs=[pl.BlockSpec((1, M, D), lambda b: (b, 0, 0)),
                  pl.BlockSpec((D, N), lambda b: (0, 0))],
        out_specs=pl.BlockSpec((1, M, N), lambda b: (b, 0, 0)),
        out_shape=jax.ShapeDtypeStruct((B, M, N), BF16),
        compiler_params=_params("parallel"),
        name="memkv",
    )(mem, w_kv)


def _layer_norm(z, g, b):
    mu = jnp.mean(z, axis=-1, keepdims=True)
    zc = z - mu
    var = jnp.mean(zc * zc, axis=-1, keepdims=True)
    return zc * lax.rsqrt(var + LN_EPS) * g + b


def _post_kernel(x_ref, mix_ref, mq_ref, kv_ref, wo_ref, g_ref, b_ref, wr_ref, br_ref,
                 x1_ref, x1s_ref, idx_ref, gate_ref, rank_ref, cnt_ref, run_ref, *, alpha):
    tm = x_ref.shape[0]
    mq = mq_ref[...]
    kv = kv_ref[0]
    km = kv[:, :MEM_WIDTH]
    vm = kv[:, MEM_WIDTH:]
    lane = lax.broadcasted_iota(jnp.int32, (tm, MEM_WIDTH), 1)
    scale = MEM_HEAD_DIM ** -0.5
    mo = jnp.zeros((tm, MEM_WIDTH), F32)
    for hd in range(MEM_HEADS):
        head = (lane >> 6) == hd
        qh = jnp.where(head, mq * scale, 0.0).astype(BF16)
        s = lax.dot_general(qh, km, _NT, preferred_element_type=F32)
        m = jnp.max(s, axis=1, keepdims=True)
        p = jnp.exp(s - m)
        l = jnp.sum(p, axis=1, keepdims=True)
        oh = jnp.dot(p.astype(BF16), vm, preferred_element_type=F32) / l
        mo = jnp.where(head, oh, mo)

    y = jnp.dot(mix_ref[...].astype(BF16), wo_ref[:MIX_WIDTH, :], preferred_element_type=F32)
    y = y + jnp.dot(mo.astype(BF16), wo_ref[MIX_WIDTH:, :], preferred_element_type=F32)
    x1 = _layer_norm(alpha * x_ref[...] + y, g_ref[...], b_ref[...])
    x1_ref[...] = x1
    _store_subrows(x1s_ref, x1)

    x_hi = x1.astype(BF16)
    x_lo = (x1 - x_hi.astype(F32)).astype(BF16)
    hi = jnp.dot(x_hi, wr_ref[...], preferred_element_type=F32)
    lo = jnp.dot(x_lo, wr_ref[:, :LANES], preferred_element_type=F32)
    logits = hi[:, :LANES] + hi[:, LANES:] + lo
    g = logits.T[:N_EXPERTS] + br_ref[...]
    erow = lax.broadcasted_iota(jnp.int32, (N_EXPERTS, tm), 0)
    orow = lax.broadcasted_iota(jnp.int32, (SUBLANES, tm), 0)
    idx_out = jnp.zeros((SUBLANES, tm), jnp.int32)
    vals, picks = [], []
    chosen = jnp.zeros((N_EXPERTS, tm), F32)
    for kk in range(TOP_K):
        mx = jnp.max(g, axis=0, keepdims=True)
        idx = jnp.min(jnp.where(g == mx, erow, N_EXPERTS), axis=0, keepdims=True)
        idx_out = jnp.where(orow == kk, idx, idx_out)
        vals.append(mx)
        picks.append(idx)
        chosen = chosen + jnp.where(erow == idx, 1.0, 0.0)
        g = jnp.where(erow == idx, -jnp.inf, g)
    evs = [jnp.exp(v - vals[0]) for v in vals]
    den = sum(evs)
    gate_out = jnp.zeros((SUBLANES, tm), F32)
    for kk in range(TOP_K):
        gate_out = jnp.where(orow == kk, evs[kk] / den, gate_out)
    idx_ref[...] = idx_out
    gate_ref[...] = gate_out

    @pl.when(pl.program_id(0) == 0)
    def _():
        run_ref[...] = jnp.zeros_like(run_ref)

    ta = lax.broadcasted_iota(jnp.int32, (tm, tm), 0)
    tb = lax.broadcasted_iota(jnp.int32, (tm, tm), 1)
    before = jnp.where(ta < tb, 1.0, 0.0).astype(BF16)
    earlier = jnp.dot(chosen.astype(BF16), before, preferred_element_type=F32) + run_ref[:, :1]
    rank_out = jnp.zeros((SUBLANES, tm), jnp.int32)
    for kk in range(TOP_K):
        rank = jnp.sum(jnp.where(erow == picks[kk], earlier, 0.0), axis=0, keepdims=True)
        rank_out = jnp.where(orow == kk, rank.astype(jnp.int32), rank_out)
    rank_ref[...] = rank_out
    run_ref[...] = run_ref[...] + jnp.sum(chosen, axis=1, keepdims=True)
    cnt_ref[...] = run_ref[...]


def _post_mixer(x2, mix2, h2, kv, w_o_bf16, ln_g, ln_b, w_router, b_router, seq_len, alpha):
    T, D = x2.shape
    tm = POST_ROWS
    N = h2.shape[1]
    M = kv.shape[1]
    mq_col = (N - MEM_WIDTH) // MEM_WIDTH
    tiles_per_seq = seq_len // tm
    wr = jnp.zeros((D, LANES), F32).at[:, :N_EXPERTS].set(w_router)
    wr_hi = wr.astype(BF16)
    wr = jnp.concatenate([wr_hi, (wr - wr_hi.astype(F32)).astype(BF16)], axis=1)
    br = b_router.reshape(N_EXPERTS, 1)
    row = lambda n: pl.BlockSpec((tm, n), lambda i: (i, 0))
    full = lambda a, b: pl.BlockSpec((a, b), lambda i: (0, 0))
    per_token = pl.BlockSpec((SUBLANES, tm), lambda i: (0, i))
    return pl.pallas_call(
        functools.partial(_post_kernel, alpha=alpha),
        grid=(T // tm,),
        in_specs=[row(D), row(MIX_WIDTH),
                  pl.BlockSpec((tm, MEM_WIDTH), lambda i: (i, mq_col)),
                  pl.BlockSpec((1, M, 2 * MEM_WIDTH), lambda i: (i // tiles_per_seq, 0, 0)),
                  full(D, D), full(1, D), full(1, D),
                  full(D, 2 * LANES), full(N_EXPERTS, 1)],
        out_specs=[row(D), pl.BlockSpec((tm * (D // LANES), LANES), lambda i: (i, 0)),
                   per_token, per_token, per_token, full(N_EXPERTS, LANES)],
        out_shape=[jax.ShapeDtypeStruct((T, D), F32),
                   jax.ShapeDtypeStruct((T * (D // LANES), LANES), F32),
                   jax.ShapeDtypeStruct((SUBLANES, T), jnp.int32),
                   jax.ShapeDtypeStruct((SUBLANES, T), F32),
                   jax.ShapeDtypeStruct((SUBLANES, T), jnp.int32),
                   jax.ShapeDtypeStruct((N_EXPERTS, LANES), F32)],
        scratch_shapes=[pltpu.VMEM((N_EXPERTS, LANES), F32)],
        compiler_params=_params("arbitrary"),
        name="post_mixer",
    )(x2, mix2, h2, kv, w_o_bf16, ln_g.reshape(1, D), ln_b.reshape(1, D), wr, br)


def _sc_mesh():
    return plsc.VectorSubcoreMesh(core_axis_name="core", subcore_axis_name="subcore")


def _sc_scatter_rows(rows, idx, n_rep, n_out):
    R, W = rows.shape

    @functools.partial(pl.kernel, out_type=jax.ShapeDtypeStruct((n_out, W), rows.dtype),
                       mesh=_sc_mesh(), scratch_types=[])
    def scatter(x_hbm, i_hbm, o_hbm):
        def body(x_vmem, i_vmem):
            for r in range(n_rep):
                pltpu.sync_copy(x_vmem, o_hbm.at[i_vmem.at[r]])

        pltpu.emit_pipeline(
            body, grid=(R // SC_WINDOW,),
            in_specs=[pl.BlockSpec((SC_WINDOW, W), lambda i: (i, 0)),
                      pl.BlockSpec((SC_IDX_ROWS, SC_WINDOW), lambda i: (0, i))],
            out_specs=[], core_axis_name=("core", "subcore"),
            dimension_semantics=(pltpu.PARALLEL,), trace_scopes=False)(x_hbm, i_hbm)

    return scatter(rows, idx)


def _sc_gather_rows(table, idx):
    n = idx.shape[0]
    W = table.shape[1]

    @functools.partial(pl.kernel, out_type=jax.ShapeDtypeStruct((n, W), table.dtype),
                       mesh=_sc_mesh(), scratch_types=[])
    def gather(t_hbm, i_hbm, o_hbm):
        def body(i_vmem, o_vmem):
            pltpu.sync_copy(t_hbm.at[i_vmem.at[0]], o_vmem)

        pltpu.emit_pipeline(
            body, grid=(n // SC_WINDOW,),
            in_specs=[pl.BlockSpec((1, SC_WINDOW), lambda i: (0, i))],
            out_specs=[pl.BlockSpec((SC_WINDOW, W), lambda i: (i, 0))],
            core_axis_name=("core", "subcore"),
            dimension_semantics=(pltpu.PARALLEL,), trace_scopes=False)(i_hbm, o_hbm)

    return gather(table, idx.reshape(1, n))


def _sc_workers():
    info = pltpu.get_tpu_info().sparse_core
    return info.num_cores, info.num_cores * info.num_subcores


def _sc_scatter_slabs(rows, idx, n_rep, n_out):
    R, S, W = rows.shape
    n_cores, n_workers = _sc_workers()
    per_worker = (R // SC_WINDOW) // n_workers
    assert per_worker * n_workers * SC_WINDOW == R

    @functools.partial(pl.kernel, out_type=jax.ShapeDtypeStruct((n_out, S, W), rows.dtype),
                       mesh=_sc_mesh(),
                       scratch_types=[pltpu.VMEM((SC_IDX_ROWS, SC_WINDOW), jnp.int32),
                                      pltpu.VMEM((SC_CHUNK, S, W), rows.dtype)])
    def scatter(x_hbm, i_hbm, o_hbm, ibuf, buf):
        wid = lax.axis_index("subcore") * n_cores + lax.axis_index("core")

        @pl.loop(0, per_worker)
        def _(s):
            first = (wid * per_worker + s) * SC_WINDOW
            pltpu.sync_copy(i_hbm.at[:, pl.ds(first, SC_WINDOW)], ibuf)
            for c in range(SC_WINDOW // SC_CHUNK):
                pltpu.sync_copy(x_hbm.at[pl.ds(first + c * SC_CHUNK, SC_CHUNK)], buf)
                for r in range(n_rep):
                    pltpu.sync_copy(buf, o_hbm.at[ibuf.at[r, pl.ds(c * SC_CHUNK, SC_CHUNK)]])

    return scatter(rows, idx)


def _sc_gather_slabs(table, idx):
    n = idx.shape[0]
    S, W = table.shape[1:]
    n_cores, n_workers = _sc_workers()
    per_worker = (n // SC_WINDOW) // n_workers
    assert per_worker * n_workers * SC_WINDOW == n
    n_chunks = SC_WINDOW // SC_CHUNK

    @functools.partial(pl.kernel, out_type=jax.ShapeDtypeStruct((n, S, W), table.dtype),
                       mesh=_sc_mesh(),
                       scratch_types=[pltpu.VMEM((1, SC_WINDOW), jnp.int32),
                                      pltpu.VMEM((2, SC_CHUNK, S, W), table.dtype),
                                      pltpu.SemaphoreType.DMA((2,)), pltpu.SemaphoreType.DMA((2,))])
    def gather(t_hbm, i_hbm, o_hbm, ibuf, buf, fetch_sem, store_sem):
        wid = lax.axis_index("subcore") * n_cores + lax.axis_index("core")

        @pl.loop(0, per_worker)
        def _(s):
            blk = wid * per_worker + s
            pltpu.sync_copy(i_hbm.at[pl.ds(blk, 1)], ibuf)

            def fetch(c):
                return pltpu.make_async_copy(
                    t_hbm.at[ibuf.at[0, pl.ds(c * SC_CHUNK, SC_CHUNK)]], buf.at[c % 2],
                    fetch_sem.at[c % 2])

            def store(c):
                return pltpu.make_async_copy(
                    buf.at[c % 2], o_hbm.at[pl.ds(blk * SC_WINDOW + c * SC_CHUNK, SC_CHUNK)],
                    store_sem.at[c % 2])

            fetch(0).start()
            for c in range(n_chunks):
                if c + 1 < n_chunks:
                    if c >= 1:
                        store(c - 1).wait()
                    fetch(c + 1).start()
                fetch(c).wait()
                store(c).start()
            store(n_chunks - 2).wait()
            store(n_chunks - 1).wait()

    return gather(table, idx.reshape(n // SC_WINDOW, SC_WINDOW))


def _route(top_idx, rank, counts):
    rb = MOE_ROWS
    n_tokens = top_idx.shape[1]
    tk = n_tokens * TOP_K
    padded = (counts + rb - 1) // rb * rb
    pend = jnp.cumsum(padded)
    pstart = pend - padded
    experts = jnp.arange(N_EXPERTS, dtype=jnp.int32)
    start = jnp.sum(jnp.where(top_idx[:TOP_K, :, None] == experts, pstart, 0), axis=2)
    dest = (start + rank[:TOP_K]).astype(jnp.int32)
    n_blocks = tk // rb + N_EXPERTS
    first_row = jnp.arange(n_blocks, dtype=jnp.int32) * rb
    block_e = jnp.minimum(jnp.sum(pend[None, :] <= first_row[:, None], axis=1),
                          N_EXPERTS - 1).astype(jnp.int32)
    n_used = (pend[-1] // rb).astype(jnp.int32).reshape(1)
    return dest, block_e, n_used


def _dispatch(x1s, dest, n_rows, sub):
    T = dest.shape[1]
    idx = jnp.concatenate([dest, jnp.zeros((SC_IDX_ROWS - TOP_K, T), jnp.int32)], axis=0)
    xs = _sc_scatter_slabs(x1s.reshape(T, sub, LANES), idx, TOP_K, n_rows)
    return xs.reshape(n_rows * sub, LANES)


def _expert_kernel(be_ref, nu_ref, x_ref, wg_ref, bg_ref, wu_ref, bu_ref, wd_ref, bd_ref,
                   y_ref, wgb, wub, wdb):
    i = pl.program_id(0)
    prev = be_ref[jnp.maximum(i - 1, 0)]

    @pl.when((i == 0) | (be_ref[i] != prev))
    def _():
        wgb[...] = wg_ref[0, 0].astype(BF16)
        wub[...] = wu_ref[0, 0].astype(BF16)
        wdb[...] = wd_ref[0, 0].astype(BF16)

    @pl.when(i < nu_ref[0])
    def _():
        sub = wgb.shape[0] // LANES
        xb = _load_subrows(x_ref, x_ref.shape[0] // sub, sub).astype(BF16)
        gate = jnp.dot(xb, wgb[...], preferred_element_type=F32) + bg_ref[0, 0]
        gate = jnp.minimum(gate, SWIGLU_LIMIT)
        up = jnp.dot(xb, wub[...], preferred_element_type=F32) + bu_ref[0, 0]
        up = jnp.clip(up, -SWIGLU_LIMIT, SWIGLU_LIMIT)
        hid = gate * _sigmoid(SWIGLU_ALPHA * gate) * (up + 1.0)
        y = jnp.dot(hid.astype(BF16), wdb[...], preferred_element_type=F32) + bd_ref[0, 0]
        _store_subrows(y_ref, y)


def _experts(xs, block_e, n_used, layer, w_gate, b_gate, w_up, b_up, w_down, b_down):
    rb = MOE_ROWS
    n_blocks = block_e.shape[0]
    E, D, F = w_gate.shape[1:]
    sub = D // LANES
    wspec = lambda a, b: pl.BlockSpec((1, 1, a, b), lambda i, be, nu: (layer, be[i], 0, 0))
    live = lambda i, be, nu: (jnp.where(i < nu[0], i, n_blocks), 0)
    grid_spec = pltpu.PrefetchScalarGridSpec(
        num_scalar_prefetch=2,
        grid=(n_blocks,),
        in_specs=[pl.BlockSpec((rb * sub, LANES), live),
                  wspec(D, F), wspec(1, F), wspec(D, F), wspec(1, F), wspec(F, D), wspec(1, D)],
        out_specs=pl.BlockSpec((rb * sub, LANES), live),
        scratch_shapes=[pltpu.VMEM((D, F), BF16), pltpu.VMEM((D, F), BF16),
                        pltpu.VMEM((F, D), BF16)],
    )
    depth = w_gate.shape[0]
    return pl.pallas_call(
        _expert_kernel,
        grid_spec=grid_spec,
        out_shape=jax.ShapeDtypeStruct(xs.shape, F32),
        compiler_params=_params("arbitrary"),
        name="experts",
    )(block_e, n_used, xs, w_gate, b_gate.reshape(depth, E, 1, F),
      w_up, b_up.reshape(depth, E, 1, F), w_down, b_down.reshape(depth, E, 1, D))


def _combine_kernel(x1_ref, gate_ref, y_ref, g_ref, b_ref, o_ref, *, alpha):
    tm, D = x1_ref.shape
    sub = D // LANES
    gates = gate_ref[...].T
    f = jnp.zeros(x1_ref.shape, F32)
    for kk in range(TOP_K):
        f = f + gates[:, kk:kk + 1] * _load_subrows(y_ref, tm, sub, kk * sub, TOP_K * sub)
    o_ref[...] = _layer_norm(alpha * x1_ref[...] + f, g_ref[...], b_ref[...])


def _combine(x1, gates, dest, y_rows, ln_g, ln_b, alpha):
    T, D = x1.shape
    sub = D // LANES
    tm = COMBINE_ROWS
    yg = _sc_gather_slabs(y_rows.reshape(-1, sub, LANES), dest.T.reshape(-1)).reshape(-1, LANES)
    row = lambda n: pl.BlockSpec((tm, n), lambda i: (i, 0))
    full = lambda a, b: pl.BlockSpec((a, b), lambda i: (0, 0))
    return pl.pallas_call(
        functools.partial(_combine_kernel, alpha=alpha),
        grid=(T // tm,),
        in_specs=[row(D), pl.BlockSpec((SUBLANES, tm), lambda i: (0, i)),
                  pl.BlockSpec((tm * TOP_K * sub, LANES), lambda i: (i, 0)),
                  full(1, D), full(1, D)],
        out_specs=row(D),
        out_shape=jax.ShapeDtypeStruct((T, D), F32),
        compiler_params=_params("parallel"),
        name="combine",
    )(x1, gates, yg, ln_g.reshape(1, D), ln_b.reshape(1, D))


def kernel(x, mem, w_in_hgrn, hgrn_lb_logits, hgrn_norm_g, w_in_moba, w_mem_kv, w_o,
           ln_mix_g, ln_mix_b, w_router, b_router, w_gate, b_gate, w_up, b_up,
           w_down, b_down, ln_ffn_g, ln_ffn_b):
    B, S, D = x.shape
    T = B * S
    depth = w_o.shape[0]
    alpha = (2 * depth) ** 0.25

    p_lb = jax.nn.softmax(hgrn_lb_logits.astype(F32), axis=0)
    lower_bounds = jnp.cumsum(p_lb, axis=0) - p_lb[0]

    x2 = x.reshape(T, D)
    for layer in range(depth):
        j = layer // 2
        if layer % 2 == 0:
            (h2,) = _inproj(x2, w_in_hgrn[j].astype(BF16), for_moba=False)
            mix = _hgrn_mixer(h2.reshape(B, S, -1), lower_bounds[j], hgrn_norm_g[j])
        else:
            h2, kv2, q6 = _inproj(x2, w_in_moba[j].astype(BF16), for_moba=True)
            mix = _moba_mixer(h2.reshape(B, S, -1), kv2.reshape(B, S, -1), q6)
        kv = _memkv(mem, w_mem_kv[layer])
        x1, x1s, top_idx, gates, rank, counts = _post_mixer(
            x2, mix.reshape(T, MIX_WIDTH), h2, kv, w_o[layer].astype(BF16),
            ln_mix_g[layer], ln_mix_b[layer], w_router[layer], b_router[layer], S, alpha)
        dest, block_e, n_used = _route(top_idx, rank, counts[:, 0].astype(jnp.int32))
        xs = _dispatch(x1s, dest, (block_e.shape[0] + 1) * MOE_ROWS, D // LANES)
        y_rows = _experts(xs, block_e, n_used, layer, w_gate, b_gate, w_up, b_up, w_down, b_down)
        x2 = _combine(x1, gates, dest, y_rows, ln_ffn_g[layer], ln_ffn_b[layer], alpha)
    return x2.reshape(B, S, D)
```

```python
import functools
import math

import jax
import jax.numpy as jnp
from jax import lax
from jax.experimental import pallas as pl
from jax.experimental.pallas import tpu as pltpu
from jax.experimental.pallas import tpu_sc as plsc

MIX_WIDTH = 768
MEM_HEADS = 4
MEM_HEAD_DIM = 64
MEM_WIDTH = MEM_HEADS * MEM_HEAD_DIM
HGRN_HEADS = 6
HGRN_DK = 128
MOBA_HEADS = 12
MOBA_HEAD_DIM = 64
MOBA_BLOCK = 256
MOBA_TOPK = 3
N_EXPERTS = 32
TOP_K = 4
SWIGLU_ALPHA = 1.702
SWIGLU_LIMIT = 7.0
LN_EPS = 1e-5
RMS_EPS = 1e-6

LANES = 128
SUBLANES = 8
VMEM_LIMIT_BYTES = 56 * 1024 * 1024

INPROJ_ROWS = 512
HGRN_CHUNK = 64
HGRN_ROWS = 1024
POST_ROWS = 512
MOBA_TILE = 256
MOBA_TILE_GROUP = 32
MOBA_PLACE_GROUP = 8
MOBA_PICK_GROUP = 8
MOBA_MERGE_GROUP = 8
MOBA_OWN_GROUP = 8
MOE_ROWS = 512
COMBINE_ROWS = 512
SC_WINDOW = 128
SC_IDX_ROWS = 8
SC_CHUNK = 32

BF16 = jnp.bfloat16
F32 = jnp.float32

_NT = (((1,), (1,)), ((), ()))
_TN = (((0,), (0,)), ((), ()))


def _alibi_slope_list(n):
    def pow2(m):
        start = 2.0 ** (-(2.0 ** -(math.log2(m) - 3)))
        return [start ** (i + 1) for i in range(m)]
    if math.log2(n).is_integer():
        return pow2(n)
    c = 2 ** math.floor(math.log2(n))
    return pow2(c) + _alibi_slope_list(2 * c)[0::2][:n - c]


def _sigmoid(x):
    return 1.0 / (1.0 + jnp.exp(-x))


def _params(*sem):
    return pltpu.CompilerParams(dimension_semantics=sem, vmem_limit_bytes=VMEM_LIMIT_BYTES)


def _store_subrows(ref, value, first=0, stride=None):
    sub = value.shape[1] // LANES
    stride = stride or sub
    for c in range(sub):
        ref[pl.ds(first + c, value.shape[0], stride=stride), :] = value[:, c * LANES:(c + 1) * LANES]


def _load_subrows(ref, rows, sub, first=0, stride=None):
    stride = stride or sub
    return jnp.concatenate(
        [ref[pl.ds(first + c, rows, stride=stride), :] for c in range(sub)], axis=1)


def _inproj_kernel(x_ref, w_ref, h_ref, *moba_refs):
    h = jnp.dot(x_ref[...].astype(BF16), w_ref[...], preferred_element_type=F32)
    h_ref[...] = h
    if moba_refs:
        kv_ref, q6_ref = moba_refs
        n_pairs = q6_ref.shape[0]
        for p in range(n_pairs):
            q6_ref[p] = h[:, p * LANES:(p + 1) * LANES] * (MOBA_HEAD_DIM ** -0.5)
            for part in range(2):
                col = (1 + part) * MIX_WIDTH + p * LANES
                kv_ref[:, (2 * p + part) * LANES:(2 * p + part + 1) * LANES] = (
                    h[:, col:col + LANES].astype(BF16))


def _inproj(x2, w_bf16, for_moba):
    T, D = x2.shape
    N = w_bf16.shape[1]
    tm = INPROJ_ROWS
    out_shape = [jax.ShapeDtypeStruct((T, N), F32)]
    out_specs = [pl.BlockSpec((tm, N), lambda i: (i, 0))]
    if for_moba:
        NP = MOBA_HEADS // 2
        out_shape += [jax.ShapeDtypeStruct((T, 2 * MIX_WIDTH), BF16),
                      jax.ShapeDtypeStruct((NP, T, LANES), F32)]
        out_specs += [pl.BlockSpec((tm, 2 * MIX_WIDTH), lambda i: (i, 0)),
                      pl.BlockSpec((NP, tm, LANES), lambda i: (0, i, 0))]
    return pl.pallas_call(
        _inproj_kernel,
        grid=(T // tm,),
        in_specs=[pl.BlockSpec((tm, D), lambda i: (i, 0)),
                  pl.BlockSpec((D, N), lambda i: (0, 0))],
        out_specs=out_specs,
        out_shape=out_shape,
        compiler_params=_params("parallel"),
        name="inproj",
    )(x2, w_bf16)


def _cumsum_rows(x, row):
    n = x.shape[0]
    sh = 1
    while sh < n:
        x = x + jnp.where(row >= sh, pltpu.roll(x, sh, 0), 0.0)
        sh *= 2
    return x


def _bcast_row(a, group, r):
    n = a.shape[0]
    a3 = a.reshape(n // group, group, LANES)
    return jnp.broadcast_to(a3[:, r:r + 1, :], a3.shape).reshape(n, LANES)


def _hgrn_chunk(qr, fr, v, gr, lb, ng, e_sum, st_t):
    C = qr.shape[0]
    row = lax.broadcasted_iota(jnp.int32, (C, LANES), 0)
    rr = lax.broadcasted_iota(jnp.int32, (C, C), 0)
    cc = lax.broadcasted_iota(jnp.int32, (C, C), 1)

    q = qr * _sigmoid(qr)
    forget = lb + (1.0 - lb) * _sigmoid(fr)
    k = 1.0 - forget
    G = _cumsum_rows(jnp.log(forget), row)

    z = jnp.log(k) - G
    parts = []
    for s in range(SUBLANES):
        parts.append((q * jnp.exp(jnp.minimum(G + _bcast_row(z, SUBLANES, s), 0.0))).astype(BF16))
    a_diag = jnp.dot(jnp.concatenate(parts, axis=1), e_sum, preferred_element_type=F32)
    A = jnp.where(((rr >> 3) == (cc >> 3)) & (cc <= rr), a_diag, 0.0)

    m = SUBLANES
    while m < C:
        lg = int(math.log2(m))
        Gr = _bcast_row(G, 2 * m, m - 1)
        second = ((row >> lg) & 1) == 1
        qm = q * jnp.exp(jnp.where(second, G - Gr, -jnp.inf))
        km = k * jnp.exp(jnp.where(second, -jnp.inf, Gr - G))
        am = lax.dot_general(qm.astype(BF16), km.astype(BF16), _NT, preferred_element_type=F32)
        A = A + jnp.where((rr >> (lg + 1)) == (cc >> (lg + 1)), am, 0.0)
        m *= 2

    vb = v.astype(BF16)
    o = jnp.dot(A.astype(BF16), vb, preferred_element_type=F32)
    o = o + lax.dot_general((q * jnp.exp(G)).astype(BF16), st_t.astype(BF16), _NT,
                            preferred_element_type=F32)
    g_end = G[C - 1:C, :]
    kd = (k * jnp.exp(g_end - G)).astype(BF16)
    st_new = st_t * jnp.exp(g_end) + lax.dot_general(vb, kd, _TN, preferred_element_type=F32)

    ms = jnp.mean(o * o, axis=-1, keepdims=True)
    out = o * lax.rsqrt(ms + RMS_EPS) * ng * _sigmoid(gr)
    return out, st_new


def _hgrn_kernel(q_ref, f_ref, i_ref, g_ref, lb_ref, ng_ref, e_ref, o_ref, st_ref, *, chunk):
    @pl.when(pl.program_id(2) == 0)
    def _():
        st_ref[...] = jnp.zeros_like(st_ref)

    lb = lb_ref[0]
    ng = ng_ref[...]
    e_sum = e_ref[...]
    n_chunks = q_ref.shape[1] // chunk
    for c in range(n_chunks):
        sl = pl.ds(c * chunk, chunk)
        out, st_new = _hgrn_chunk(q_ref[0, sl, :], f_ref[0, sl, :], i_ref[0, sl, :],
                                  g_ref[0, sl, :], lb, ng, e_sum, st_ref[...])
        st_ref[...] = st_new
        o_ref[0, sl, :] = out


def _hgrn_mixer(h3, lb, norm_g):
    B, S, _ = h3.shape
    ts = min(HGRN_ROWS, S)
    C = HGRN_CHUNK
    H = HGRN_HEADS
    e_sum = (jnp.arange(SUBLANES * LANES)[:, None] // LANES == jnp.arange(C)[None, :] % SUBLANES
             ).astype(BF16)
    col = lambda off: pl.BlockSpec((1, ts, LANES), lambda b, h, s, off=off: (b, s, off + h))
    return pl.pallas_call(
        functools.partial(_hgrn_kernel, chunk=C),
        grid=(B, H, S // ts),
        in_specs=[col(0), col(H), col(2 * H), col(3 * H),
                  pl.BlockSpec((1, 1, LANES), lambda b, h, s: (h, 0, 0)),
                  pl.BlockSpec((1, LANES), lambda b, h, s: (0, 0)),
                  pl.BlockSpec((SUBLANES * LANES, C), lambda b, h, s: (0, 0))],
        out_specs=pl.BlockSpec((1, ts, LANES), lambda b, h, s: (b, s, h)),
        out_shape=jax.ShapeDtypeStruct((B, S, MIX_WIDTH), F32),
        scratch_shapes=[pltpu.VMEM((HGRN_DK, HGRN_DK), F32)],
        compiler_params=_params("parallel", "parallel", "arbitrary"),
        name="hgrn",
    )(h3, h3, h3, h3, lb.reshape(H, 1, LANES), norm_g.reshape(1, LANES), e_sum)


def _moba_pick_kernel(q_ref, kf_ref, sr_ref, cnt_ref, kmean_ref, *, qblocks):
    BLK = MOBA_BLOCK
    W = qblocks * BLK
    nbp = kmean_ref.shape[0]
    i0 = pl.program_id(2) * qblocks

    @pl.when(pl.program_id(2) == 0)
    def _():
        kmean_ref[...] = jnp.zeros_like(kmean_ref)

    for j in range(qblocks):
        kmean_ref[pl.ds(i0 + j, 1), :] = jnp.mean(kf_ref[0, j * BLK:(j + 1) * BLK, :], axis=0,
                                                  keepdims=True)
    km = kmean_ref[...]
    lane_k = lax.broadcasted_iota(jnp.int32, (nbp, LANES), 1)
    km2 = jnp.concatenate([jnp.where(lane_k < MOBA_HEAD_DIM, km, 0.0),
                           jnp.where(lane_k >= MOBA_HEAD_DIM, km, 0.0)], axis=0)
    gate = lax.dot_general(km2, q_ref[0], _NT, precision=lax.Precision.HIGHEST,
                           preferred_element_type=F32).reshape(2, nbp, W)
    nblk = lax.broadcasted_iota(jnp.int32, (2, nbp, W), 1)
    qi = i0 + (lax.broadcasted_iota(jnp.int32, (2, 1, W), 2) >> int(math.log2(BLK)))
    g = jnp.where(nblk < qi, gate, -jnp.inf)
    picks = []
    for _ in range(MOBA_TOPK):
        mx = jnp.max(g, axis=1, keepdims=True)
        idx = jnp.min(jnp.where(g == mx, nblk, nbp), axis=1, keepdims=True)
        picks.append((idx, (mx > -jnp.inf) & (idx < qi)))
        g = jnp.where(nblk == idx, -jnp.inf, g)
    chosen = jnp.zeros((2, nbp, W), F32)
    for idx, valid in picks:
        chosen = chosen + jnp.where((nblk == idx) & valid, 1.0, 0.0)
    qa = lax.broadcasted_iota(jnp.int32, (BLK, BLK), 0)
    qc = lax.broadcasted_iota(jnp.int32, (BLK, BLK), 1)
    before = jnp.where(qa < qc, 1.0, 0.0).astype(BF16)
    chosen2 = chosen.reshape(2 * nbp, W).astype(BF16)
    earlier = jnp.concatenate(
        [jnp.dot(chosen2[:, j * BLK:(j + 1) * BLK], before, preferred_element_type=F32)
         for j in range(qblocks)], axis=1).reshape(2, nbp, W)
    orow = lax.broadcasted_iota(jnp.int32, (2 * SUBLANES, W), 0)
    out = jnp.zeros((2 * SUBLANES, W), jnp.int32)
    for j, (idx, valid) in enumerate(picks):
        rank = jnp.sum(jnp.where(nblk == idx, earlier, 0.0), axis=1, keepdims=True).astype(jnp.int32)
        sel = jnp.where(valid, idx, -1)
        for hh in range(2):
            rep = hh * MOBA_TOPK + j
            out = jnp.where(orow == rep, sel[hh], out)
            out = jnp.where(orow == SUBLANES + rep, rank[hh], out)
    sr_ref[0, 0] = out
    lane_c = lax.broadcasted_iota(jnp.int32, (nbp, LANES), 1)
    for j in range(qblocks):
        per_block = jnp.sum(chosen[:, :, j * BLK:(j + 1) * BLK], axis=2, keepdims=True)
        cnt_ref[0, 0, j * nbp:(j + 1) * nbp, :] = jnp.where(
            lane_c == 0, per_block[0], jnp.where(lane_c == 1, per_block[1], 0.0))


def _moba_place_kernel(sr_ref, tab_ref, idx_ref, *, group, spare):
    BLK = MOBA_BLOCK
    n_rep = 2 * MOBA_TOPK
    b = pl.program_id(0)
    p = pl.program_id(1)
    ig = pl.program_id(2)
    nrow = lax.broadcasted_iota(jnp.int32, (LANES, BLK), 0)
    orow = lax.broadcasted_iota(jnp.int32, (SC_IDX_ROWS, BLK), 0)
    qpos = lax.broadcasted_iota(jnp.int32, (1, BLK), 1)
    for g in range(group):
        blk = sr_ref[0, 0, :, g * BLK:(g + 1) * BLK]
        tab_t = tab_ref[0, 0, g * SUBLANES:(g + 1) * SUBLANES, :].T
        early = jnp.minimum(ig * group + g, MOBA_TOPK - 1)
        out = jnp.zeros((SC_IDX_ROWS, BLK), jnp.int32)
        for rep in range(n_rep):
            hh = rep // MOBA_TOPK
            sel = blk[rep:rep + 1, :]
            rank = blk[SUBLANES + rep:SUBLANES + rep + 1, :]
            start = jnp.sum(jnp.where(nrow == sel, tab_t[:, hh:hh + 1], 0.0), axis=0, keepdims=True)
            unused = spare + (((b * pl.num_programs(1) + p) * MOBA_TOPK + early) * n_rep + rep) * BLK
            dest = jnp.where(sel >= 0, start.astype(jnp.int32) + rank, unused + qpos)
            out = jnp.where(orow == rep, dest, out)
        idx_ref[:, g * BLK:(g + 1) * BLK] = out


def _moba_tiles_kernel(tb_ref, tp_ref, th_ref, tn_ref, nu_ref, sl_ref, q_ref, *refs, group):
    kv_refs, o_ref = refs[:group], refs[group]
    t = pl.program_id(0)
    tq = q_ref.shape[0] // group

    @pl.when(t * group < nu_ref[0])
    def _():
        lane = lax.broadcasted_iota(jnp.int32, (tq, LANES), 1)
        kpos = lax.broadcasted_iota(jnp.int32, (1, MOBA_BLOCK), 1)
        for g in range(group):
            tt = t * group + g
            hh = th_ref[tt]
            slope = sl_ref[2 * tp_ref[tt] + hh]
            head = (lane >> 6) == hh
            rows = pl.ds(g * tq, tq)
            q = jnp.where(head, q_ref[rows, :], 0.0).astype(BF16)
            s = lax.dot_general(q, kv_refs[g][0, :, :LANES], _NT, preferred_element_type=F32)
            s = s + slope * (kpos + tn_ref[tt] * MOBA_BLOCK).astype(F32)
            m = jnp.max(s, axis=1, keepdims=True)
            pr = jnp.exp(s - m)
            l = jnp.sum(pr, axis=1, keepdims=True)
            o = jnp.dot(pr.astype(BF16), kv_refs[g][0, :, LANES:], preferred_element_type=F32) / l
            o_ref[rows, :] = jnp.where(head, o, m + jnp.log(l))


def _moba_own_kernel(sl_ref, q_ref, kv_ref, o_ref, lse_ref):
    BLK = MOBA_BLOCK
    p = pl.program_id(1)
    n_blocks = q_ref.shape[1] // BLK
    lane = lax.broadcasted_iota(jnp.int32, (BLK, LANES), 1)
    rr = lax.broadcasted_iota(jnp.int32, (BLK, BLK), 0)
    cc = lax.broadcasted_iota(jnp.int32, (BLK, BLK), 1)
    first = lane < MOBA_HEAD_DIM
    for h in range(n_blocks):
        i = pl.program_id(2) * n_blocks + h
        rows = pl.ds(h * BLK, BLK)
        qf = q_ref[0, rows, :]
        k_own = kv_ref[0, rows, :LANES]
        v_own = kv_ref[0, rows, LANES:]
        key_pos = (lax.broadcasted_iota(jnp.int32, (1, BLK), 1) + i * BLK).astype(F32)
        outs, lses = [], []
        for hh in range(2):
            head = (lane >> 6) == hh
            slope = sl_ref[2 * p + hh]
            qh = jnp.where(head, qf * (MOBA_HEAD_DIM ** -0.5), 0.0).astype(BF16)
            s = lax.dot_general(qh, k_own, _NT, preferred_element_type=F32)
            s = jnp.where(cc <= rr, s + slope * key_pos, -jnp.inf)
            m = jnp.max(s, axis=1, keepdims=True)
            pr = jnp.exp(s - m)
            l = jnp.sum(pr, axis=1, keepdims=True)
            lses.append(m + jnp.log(l))
            outs.append(jnp.dot(pr.astype(BF16), v_own, preferred_element_type=F32) / l)
        o_ref[0, rows, :] = jnp.where(first, outs[0], outs[1])
        lse_ref[0, rows, :] = jnp.where(first, lses[0], lses[1])


def _moba_merge_kernel(o_ref, lse_ref, pg_ref, out_ref):
    BLK = MOBA_BLOCK
    first = lax.broadcasted_iota(jnp.int32, (BLK, LANES), 1) < MOBA_HEAD_DIM
    for h in range(o_ref.shape[1] // BLK):
        i = pl.program_id(2) * (o_ref.shape[1] // BLK) + h
        rows = pl.ds(h * BLK, BLK)
        lses = [lse_ref[0, rows, :]]
        vals = [o_ref[0, rows, :]]
        for j in range(MOBA_TOPK):
            p0 = pg_ref[j, 0, rows, :]
            p1 = pg_ref[MOBA_TOPK + j, 0, rows, :]
            has_block = j < i
            stats = pltpu.roll(jnp.where(first, p1, p0), MOBA_HEAD_DIM, 1)
            lses.append(jnp.where(has_block, stats, -jnp.inf))
            vals.append(jnp.where(has_block, jnp.where(first, p0, p1), 0.0))
        top = functools.reduce(jnp.maximum, lses)
        ws = [jnp.exp(x - top) for x in lses]
        out_ref[0, rows, :] = sum(w * v for w, v in zip(ws, vals)) / sum(ws)


def _moba_mixer(h3, kv3, q6):
    B, S, _ = h3.shape
    T = B * S
    BLK = MOBA_BLOCK
    TQ = MOBA_TILE
    G = MOBA_TILE_GROUP
    NB = S // BLK
    GI = math.gcd(NB, MOBA_PLACE_GROUP)
    nbp = -(-NB // SUBLANES) * SUBLANES
    NP = MOBA_HEADS // 2
    n_rep = 2 * MOBA_TOPK
    slopes = jnp.asarray(_alibi_slope_list(MOBA_HEADS), F32)

    QB = math.gcd(NB, MOBA_PICK_GROUP)
    selrank, counts = pl.pallas_call(
        functools.partial(_moba_pick_kernel, qblocks=QB),
        grid=(B, NP, NB // QB),
        in_specs=[pl.BlockSpec((1, QB * BLK, LANES), lambda b, p, i: (b, i, p)),
                  pl.BlockSpec((1, QB * BLK, LANES), lambda b, p, i: (b, i, NP + p))],
        out_specs=[pl.BlockSpec((1, 1, 2 * SUBLANES, QB * BLK), lambda b, p, i: (b, p, 0, i)),
                   pl.BlockSpec((1, 1, QB * nbp, LANES), lambda b, p, i: (b, p, i, 0))],
        out_shape=[jax.ShapeDtypeStruct((B, NP, 2 * SUBLANES, S), jnp.int32),
                   jax.ShapeDtypeStruct((B, NP, NB * nbp, LANES), F32)],
        scratch_shapes=[pltpu.VMEM((nbp, LANES), F32)],
        compiler_params=_params("parallel", "parallel", "arbitrary"),
        name="moba_pick",
    )(h3, h3)

    OB = math.gcd(NB, MOBA_OWN_GROUP)
    blk_spec = lambda w: pl.BlockSpec((1, OB * BLK, w), lambda b, p, i, sl: (b, i, p))
    own_o, own_lse = pl.pallas_call(
        _moba_own_kernel,
        grid_spec=pltpu.PrefetchScalarGridSpec(
            num_scalar_prefetch=1,
            grid=(B, NP, NB // OB),
            in_specs=[blk_spec(LANES), blk_spec(2 * LANES)],
            out_specs=[blk_spec(LANES), blk_spec(LANES)]),
        out_shape=[jax.ShapeDtypeStruct((B, S, MIX_WIDTH), F32)] * 2,
        compiler_params=_params("parallel", "parallel", "parallel"),
        name="moba_own",
    )(slopes, h3, kv3)

    cnt = counts.reshape(B, NP, NB, nbp, LANES)[:, :, :, :NB, :2].astype(jnp.int32)
    cnt = cnt.transpose(0, 1, 2, 4, 3)
    base = jnp.cumsum(cnt, axis=2) - cnt
    total = jnp.sum(cnt, axis=2)
    padded = (total + TQ - 1) // TQ * TQ
    pend = jnp.cumsum(padded.reshape(-1))
    seg_start = (pend - padded.reshape(-1)).reshape(B, NP, 1, 2, NB)
    table = jnp.zeros((B, NP, NB, SUBLANES, LANES), F32).at[:, :, :, :2, :NB].set(
        (seg_start + base).astype(F32)).reshape(B, NP, NB * SUBLANES, LANES)
    n_seg = B * NP * 2 * NB
    max_tiles = -(-((T * NP * n_rep) // TQ + n_seg) // G) * G
    first_row = jnp.arange(max_tiles, dtype=jnp.int32) * TQ
    tile_seg = jnp.minimum(jnp.sum(pend[None, :] <= first_row[:, None], axis=1), n_seg - 1)
    tile_n = (tile_seg % NB).astype(jnp.int32)
    tile_h = ((tile_seg // NB) % 2).astype(jnp.int32)
    tile_p = ((tile_seg // (2 * NB)) % NP).astype(jnp.int32)
    tile_b = (tile_seg // (2 * NB * NP)).astype(jnp.int32)
    n_used = (pend[-1] // TQ).astype(jnp.int32).reshape(1)
    spare = max_tiles * TQ
    n_rows = spare + max(B * NP * MOBA_TOPK * n_rep * BLK, G * TQ)

    idx = pl.pallas_call(
        functools.partial(_moba_place_kernel, group=GI, spare=spare),
        grid=(B, NP, NB // GI),
        in_specs=[pl.BlockSpec((1, 1, 2 * SUBLANES, GI * BLK), lambda b, p, i: (b, p, 0, i)),
                  pl.BlockSpec((1, 1, GI * SUBLANES, LANES), lambda b, p, i: (b, p, i, 0))],
        out_specs=pl.BlockSpec((SC_IDX_ROWS, GI * BLK),
                               lambda b, p, i: (0, p * (T // (GI * BLK)) + b * (NB // GI) + i)),
        out_shape=jax.ShapeDtypeStruct((SC_IDX_ROWS, NP * T), jnp.int32),
        compiler_params=_params("parallel", "parallel", "parallel"),
        name="moba_place",
    )(selrank, table)

    qs = _sc_scatter_rows(q6.reshape(NP * T, LANES), idx, n_rep, n_rows)

    live = lambda t, tb, tp, th, tn, nu, sl: (jnp.where(t * G < nu[0], t, max_tiles // G), 0)
    kv_blk = lambda g: pl.BlockSpec(
        (1, BLK, 2 * LANES),
        lambda t, tb, tp, th, tn, nu, sl: (tb[t * G + g], tn[t * G + g], tp[t * G + g]))
    part = pl.pallas_call(
        functools.partial(_moba_tiles_kernel, group=G),
        grid_spec=pltpu.PrefetchScalarGridSpec(
            num_scalar_prefetch=6,
            grid=(max_tiles // G,),
            in_specs=[pl.BlockSpec((G * TQ, LANES), live)] + [kv_blk(g) for g in range(G)],
            out_specs=pl.BlockSpec((G * TQ, LANES), live)),
        out_shape=jax.ShapeDtypeStruct((n_rows, LANES), F32),
        compiler_params=_params("arbitrary"),
        name="moba_tiles",
    )(tile_b, tile_p, tile_h, tile_n, n_used, slopes, qs, *([kv3] * G))

    pg = _sc_gather_rows(part, idx[:n_rep].reshape(-1)).reshape(n_rep, NP, T, LANES)

    MB = math.gcd(NB, MOBA_MERGE_GROUP)
    blk = pl.BlockSpec((1, MB * BLK, LANES), lambda b, p, i: (b, i, p))
    return pl.pallas_call(
        _moba_merge_kernel,
        grid=(B, NP, NB // MB),
        in_specs=[blk, blk,
                  pl.BlockSpec((n_rep, 1, MB * BLK, LANES),
                               lambda b, p, i: (0, p, b * (NB // MB) + i, 0))],
        out_specs=blk,
        out_shape=jax.ShapeDtypeStruct((B, S, MIX_WIDTH), F32),
        compiler_params=_params("parallel", "parallel", "parallel"),
        name="moba_merge",
    )(own_o, own_lse, pg)


def _memkv_kernel(mem_ref, w_ref, kv_ref):
    kv_ref[0] = jnp.dot(mem_ref[0].astype(BF16), w_ref[...].astype(BF16),
                        preferred_element_type=F32).astype(BF16)


def _memkv(mem, w_kv):
    B, M, D = mem.shape
    N = w_kv.shape[1]
    return pl.pallas_call(
        _memkv_kernel,
        grid=(B,),
        in_specs=[pl.BlockSpec((1, M, D), lambda b: (b, 0, 0)),
                  pl.BlockSpec((D, N), lambda b: (0, 0))],
        out_specs=pl.BlockSpec((1, M, N), lambda b: (b, 0, 0)),
        out_shape=jax.ShapeDtypeStruct((B, M, N), BF16),
        compiler_params=_params("parallel"),
        name="memkv",
    )(mem, w_kv)


def _layer_norm(z, g, b):
    mu = jnp.mean(z, axis=-1, keepdims=True)
    zc = z - mu
    var = jnp.mean(zc * zc, axis=-1, keepdims=True)
    return zc * lax.rsqrt(var + LN_EPS) * g + b


def _post_kernel(x_ref, mix_ref, mq_ref, kv_ref, wo_ref, g_ref, b_ref, wr_ref, br_ref,
                 x1_ref, x1s_ref, idx_ref, gate_ref, rank_ref, cnt_ref, run_ref, *, alpha):
    tm = x_ref.shape[0]
    mq = mq_ref[...]
    kv = kv_ref[0]
    km = kv[:, :MEM_WIDTH]
    vm = kv[:, MEM_WIDTH:]
    lane = lax.broadcasted_iota(jnp.int32, (tm, MEM_WIDTH), 1)
    scale = MEM_HEAD_DIM ** -0.5
    mo = jnp.zeros((tm, MEM_WIDTH), F32)
    for hd in range(MEM_HEADS):
        head = (lane >> 6) == hd
        qh = jnp.where(head, mq * scale, 0.0).astype(BF16)
        s = lax.dot_general(qh, km, _NT, preferred_element_type=F32)
        m = jnp.max(s, axis=1, keepdims=True)
        p = jnp.exp(s - m)
        l = jnp.sum(p, axis=1, keepdims=True)
        oh = jnp.dot(p.astype(BF16), vm, preferred_element_type=F32) / l
        mo = jnp.where(head, oh, mo)

    y = jnp.dot(mix_ref[...].astype(BF16), wo_ref[:MIX_WIDTH, :], preferred_element_type=F32)
    y = y + jnp.dot(mo.astype(BF16), wo_ref[MIX_WIDTH:, :], preferred_element_type=F32)
    x1 = _layer_norm(alpha * x_ref[...] + y, g_ref[...], b_ref[...])
    x1_ref[...] = x1
    _store_subrows(x1s_ref, x1)

    x_hi = x1.astype(BF16)
    x_lo = (x1 - x_hi.astype(F32)).astype(BF16)
    hi = jnp.dot(x_hi, wr_ref[...], preferred_element_type=F32)
    lo = jnp.dot(x_lo, wr_ref[:, :LANES], preferred_element_type=F32)
    logits = hi[:, :LANES] + hi[:, LANES:] + lo
    g = logits.T[:N_EXPERTS] + br_ref[...]
    erow = lax.broadcasted_iota(jnp.int32, (N_EXPERTS, tm), 0)
    orow = lax.broadcasted_iota(jnp.int32, (SUBLANES, tm), 0)
    idx_out = jnp.zeros((SUBLANES, tm), jnp.int32)
    vals, picks = [], []
    chosen = jnp.zeros((N_EXPERTS, tm), F32)
    for kk in range(TOP_K):
        mx = jnp.max(g, axis=0, keepdims=True)
        idx = jnp.min(jnp.where(g == mx, erow, N_EXPERTS), axis=0, keepdims=True)
        idx_out = jnp.where(orow == kk, idx, idx_out)
        vals.append(mx)
        picks.append(idx)
        chosen = chosen + jnp.where(erow == idx, 1.0, 0.0)
        g = jnp.where(erow == idx, -jnp.inf, g)
    evs = [jnp.exp(v - vals[0]) for v in vals]
    den = sum(evs)
    gate_out = jnp.zeros((SUBLANES, tm), F32)
    for kk in range(TOP_K):
        gate_out = jnp.where(orow == kk, evs[kk] / den, gate_out)
    idx_ref[...] = idx_out
    gate_ref[...] = gate_out

    @pl.when(pl.program_id(0) == 0)
    def _():
        run_ref[...] = jnp.zeros_like(run_ref)

    ta = lax.broadcasted_iota(jnp.int32, (tm, tm), 0)
    tb = lax.broadcasted_iota(jnp.int32, (tm, tm), 1)
    before = jnp.where(ta < tb, 1.0, 0.0).astype(BF16)
    earlier = jnp.dot(chosen.astype(BF16), before, preferred_element_type=F32) + run_ref[:, :1]
    rank_out = jnp.zeros((SUBLANES, tm), jnp.int32)
    for kk in range(TOP_K):
        rank = jnp.sum(jnp.where(erow == picks[kk], earlier, 0.0), axis=0, keepdims=True)
        rank_out = jnp.where(orow == kk, rank.astype(jnp.int32), rank_out)
    rank_ref[...] = rank_out
    run_ref[...] = run_ref[...] + jnp.sum(chosen, axis=1, keepdims=True)
    cnt_ref[...] = run_ref[...]


def _post_mixer(x2, mix2, h2, kv, w_o_bf16, ln_g, ln_b, w_router, b_router, seq_len, alpha):
    T, D = x2.shape
    tm = POST_ROWS
    N = h2.shape[1]
    M = kv.shape[1]
    mq_col = (N - MEM_WIDTH) // MEM_WIDTH
    tiles_per_seq = seq_len // tm
    wr = jnp.zeros((D, LANES), F32).at[:, :N_EXPERTS].set(w_router)
    wr_hi = wr.astype(BF16)
    wr = jnp.concatenate([wr_hi, (wr - wr_hi.astype(F32)).astype(BF16)], axis=1)
    br = b_router.reshape(N_EXPERTS, 1)
    row = lambda n: pl.BlockSpec((tm, n), lambda i: (i, 0))
    full = lambda a, b: pl.BlockSpec((a, b), lambda i: (0, 0))
    per_token = pl.BlockSpec((SUBLANES, tm), lambda i: (0, i))
    return pl.pallas_call(
        functools.partial(_post_kernel, alpha=alpha),
        grid=(T // tm,),
        in_specs=[row(D), row(MIX_WIDTH),
                  pl.BlockSpec((tm, MEM_WIDTH), lambda i: (i, mq_col)),
                  pl.BlockSpec((1, M, 2 * MEM_WIDTH), lambda i: (i // tiles_per_seq, 0, 0)),
                  full(D, D), full(1, D), full(1, D),
                  full(D, 2 * LANES), full(N_EXPERTS, 1)],
        out_specs=[row(D), pl.BlockSpec((tm * (D // LANES), LANES), lambda i: (i, 0)),
                   per_token, per_token, per_token, full(N_EXPERTS, LANES)],
        out_shape=[jax.ShapeDtypeStruct((T, D), F32),
                   jax.ShapeDtypeStruct((T * (D // LANES), LANES), F32),
                   jax.ShapeDtypeStruct((SUBLANES, T), jnp.int32),
                   jax.ShapeDtypeStruct((SUBLANES, T), F32),
                   jax.ShapeDtypeStruct((SUBLANES, T), jnp.int32),
                   jax.ShapeDtypeStruct((N_EXPERTS, LANES), F32)],
        scratch_shapes=[pltpu.VMEM((N_EXPERTS, LANES), F32)],
        compiler_params=_params("arbitrary"),
        name="post_mixer",
    )(x2, mix2, h2, kv, w_o_bf16, ln_g.reshape(1, D), ln_b.reshape(1, D), wr, br)


def _sc_mesh():
    return plsc.VectorSubcoreMesh(core_axis_name="core", subcore_axis_name="subcore")


def _sc_scatter_rows(rows, idx, n_rep, n_out):
    R, W = rows.shape

    @functools.partial(pl.kernel, out_type=jax.ShapeDtypeStruct((n_out, W), rows.dtype),
                       mesh=_sc_mesh(), scratch_types=[])
    def scatter(x_hbm, i_hbm, o_hbm):
        def body(x_vmem, i_vmem):
            for r in range(n_rep):
                pltpu.sync_copy(x_vmem, o_hbm.at[i_vmem.at[r]])

        pltpu.emit_pipeline(
            body, grid=(R // SC_WINDOW,),
            in_specs=[pl.BlockSpec((SC_WINDOW, W), lambda i: (i, 0)),
                      pl.BlockSpec((SC_IDX_ROWS, SC_WINDOW), lambda i: (0, i))],
            out_specs=[], core_axis_name=("core", "subcore"),
            dimension_semantics=(pltpu.PARALLEL,), trace_scopes=False)(x_hbm, i_hbm)

    return scatter(rows, idx)


def _sc_gather_rows(table, idx):
    n = idx.shape[0]
    W = table.shape[1]

    @functools.partial(pl.kernel, out_type=jax.ShapeDtypeStruct((n, W), table.dtype),
                       mesh=_sc_mesh(), scratch_types=[])
    def gather(t_hbm, i_hbm, o_hbm):
        def body(i_vmem, o_vmem):
            pltpu.sync_copy(t_hbm.at[i_vmem.at[0]], o_vmem)

        pltpu.emit_pipeline(
            body, grid=(n // SC_WINDOW,),
            in_specs=[pl.BlockSpec((1, SC_WINDOW), lambda i: (0, i))],
            out_specs=[pl.BlockSpec((SC_WINDOW, W), lambda i: (i, 0))],
            core_axis_name=("core", "subcore"),
            dimension_semantics=(pltpu.PARALLEL,), trace_scopes=False)(i_hbm, o_hbm)

    return gather(table, idx.reshape(1, n))


def _sc_workers():
    info = pltpu.get_tpu_info().sparse_core
    return info.num_cores, info.num_cores * info.num_subcores


def _sc_scatter_slabs(rows, idx, n_rep, n_out):
    R, S, W = rows.shape
    n_cores, n_workers = _sc_workers()
    per_worker = (R // SC_WINDOW) // n_workers
    assert per_worker * n_workers * SC_WINDOW == R

    @functools.partial(pl.kernel, out_type=jax.ShapeDtypeStruct((n_out, S, W), rows.dtype),
                       mesh=_sc_mesh(),
                       scratch_types=[pltpu.VMEM((SC_IDX_ROWS, SC_WINDOW), jnp.int32),
                                      pltpu.VMEM((SC_CHUNK, S, W), rows.dtype)])
    def scatter(x_hbm, i_hbm, o_hbm, ibuf, buf):
        wid = lax.axis_index("subcore") * n_cores + lax.axis_index("core")

        @pl.loop(0, per_worker)
        def _(s):
            first = (wid * per_worker + s) * SC_WINDOW
            pltpu.sync_copy(i_hbm.at[:, pl.ds(first, SC_WINDOW)], ibuf)
            for c in range(SC_WINDOW // SC_CHUNK):
                pltpu.sync_copy(x_hbm.at[pl.ds(first + c * SC_CHUNK, SC_CHUNK)], buf)
                for r in range(n_rep):
                    pltpu.sync_copy(buf, o_hbm.at[ibuf.at[r, pl.ds(c * SC_CHUNK, SC_CHUNK)]])

    return scatter(rows, idx)


def _sc_gather_slabs(table, idx):
    n = idx.shape[0]
    S, W = table.shape[1:]
    n_cores, n_workers = _sc_workers()
    per_worker = (n // SC_WINDOW) // n_workers
    assert per_worker * n_workers * SC_WINDOW == n
    n_chunks = SC_WINDOW // SC_CHUNK

    @functools.partial(pl.kernel, out_type=jax.ShapeDtypeStruct((n, S, W), table.dtype),
                       mesh=_sc_mesh(),
                       scratch_types=[pltpu.VMEM((1, SC_WINDOW), jnp.int32),
                                      pltpu.VMEM((2, SC_CHUNK, S, W), table.dtype),
                                      pltpu.SemaphoreType.DMA((2,)), pltpu.SemaphoreType.DMA((2,))])
    def gather(t_hbm, i_hbm, o_hbm, ibuf, buf, fetch_sem, store_sem):
        wid = lax.axis_index("subcore") * n_cores + lax.axis_index("core")

        @pl.loop(0, per_worker)
        def _(s):
            blk = wid * per_worker + s
            pltpu.sync_copy(i_hbm.at[pl.ds(blk, 1)], ibuf)

            def fetch(c):
                return pltpu.make_async_copy(
                    t_hbm.at[ibuf.at[0, pl.ds(c * SC_CHUNK, SC_CHUNK)]], buf.at[c % 2],
                    fetch_sem.at[c % 2])

            def store(c):
                return pltpu.make_async_copy(
                    buf.at[c % 2], o_hbm.at[pl.ds(blk * SC_WINDOW + c * SC_CHUNK, SC_CHUNK)],
                    store_sem.at[c % 2])

            fetch(0).start()
            for c in range(n_chunks):
                if c + 1 < n_chunks:
                    if c >= 1:
                        store(c - 1).wait()
                    fetch(c + 1).start()
                fetch(c).wait()
                store(c).start()
            store(n_chunks - 2).wait()
            store(n_chunks - 1).wait()

    return gather(table, idx.reshape(n // SC_WINDOW, SC_WINDOW))


def _route(top_idx, rank, counts):
    rb = MOE_ROWS
    n_tokens = top_idx.shape[1]
    tk = n_tokens * TOP_K
    padded = (counts + rb - 1) // rb * rb
    pend = jnp.cumsum(padded)
    pstart = pend - padded
    experts = jnp.arange(N_EXPERTS, dtype=jnp.int32)
    start = jnp.sum(jnp.where(top_idx[:TOP_K, :, None] == experts, pstart, 0), axis=2)
    dest = (start + rank[:TOP_K]).astype(jnp.int32)
    n_blocks = tk // rb + N_EXPERTS
    first_row = jnp.arange(n_blocks, dtype=jnp.int32) * rb
    block_e = jnp.minimum(jnp.sum(pend[None, :] <= first_row[:, None], axis=1),
                          N_EXPERTS - 1).astype(jnp.int32)
    n_used = (pend[-1] // rb).astype(jnp.int32).reshape(1)
    return dest, block_e, n_used


def _dispatch(x1s, dest, n_rows, sub):
    T = dest.shape[1]
    idx = jnp.concatenate([dest, jnp.zeros((SC_IDX_ROWS - TOP_K, T), jnp.int32)], axis=0)
    xs = _sc_scatter_slabs(x1s.reshape(T, sub, LANES), idx, TOP_K, n_rows)
    return xs.reshape(n_rows * sub, LANES)


def _expert_kernel(be_ref, nu_ref, x_ref, wg_ref, bg_ref, wu_ref, bu_ref, wd_ref, bd_ref,
                   y_ref, wgb, wub, wdb):
    i = pl.program_id(0)
    prev = be_ref[jnp.maximum(i - 1, 0)]

    @pl.when((i == 0) | (be_ref[i] != prev))
    def _():
        wgb[...] = wg_ref[0, 0].astype(BF16)
        wub[...] = wu_ref[0, 0].astype(BF16)
        wdb[...] = wd_ref[0, 0].astype(BF16)

    @pl.when(i < nu_ref[0])
    def _():
        sub = wgb.shape[0] // LANES
        xb = _load_subrows(x_ref, x_ref.shape[0] // sub, sub).astype(BF16)
        gate = jnp.dot(xb, wgb[...], preferred_element_type=F32) + bg_ref[0, 0]
        gate = jnp.minimum(gate, SWIGLU_LIMIT)
        up = jnp.dot(xb, wub[...], preferred_element_type=F32) + bu_ref[0, 0]
        up = jnp.clip(up, -SWIGLU_LIMIT, SWIGLU_LIMIT)
        hid = gate * _sigmoid(SWIGLU_ALPHA * gate) * (up + 1.0)
        y = jnp.dot(hid.astype(BF16), wdb[...], preferred_element_type=F32) + bd_ref[0, 0]
        _store_subrows(y_ref, y)


def _experts(xs, block_e, n_used, layer, w_gate, b_gate, w_up, b_up, w_down, b_down):
    rb = MOE_ROWS
    n_blocks = block_e.shape[0]
    E, D, F = w_gate.shape[1:]
    sub = D // LANES
    wspec = lambda a, b: pl.BlockSpec((1, 1, a, b), lambda i, be, nu: (layer, be[i], 0, 0))
    live = lambda i, be, nu: (jnp.where(i < nu[0], i, n_blocks), 0)
    grid_spec = pltpu.PrefetchScalarGridSpec(
        num_scalar_prefetch=2,
        grid=(n_blocks,),
        in_specs=[pl.BlockSpec((rb * sub, LANES), live),
                  wspec(D, F), wspec(1, F), wspec(D, F), wspec(1, F), wspec(F, D), wspec(1, D)],
        out_specs=pl.BlockSpec((rb * sub, LANES), live),
        scratch_shapes=[pltpu.VMEM((D, F), BF16), pltpu.VMEM((D, F), BF16),
                        pltpu.VMEM((F, D), BF16)],
    )
    depth = w_gate.shape[0]
    return pl.pallas_call(
        _expert_kernel,
        grid_spec=grid_spec,
        out_shape=jax.ShapeDtypeStruct(xs.shape, F32),
        compiler_params=_params("arbitrary"),
        name="experts",
    )(block_e, n_used, xs, w_gate, b_gate.reshape(depth, E, 1, F),
      w_up, b_up.reshape(depth, E, 1, F), w_down, b_down.reshape(depth, E, 1, D))


def _combine_kernel(x1_ref, gate_ref, y_ref, g_ref, b_ref, o_ref, *, alpha):
    tm, D = x1_ref.shape
    sub = D // LANES
    gates = gate_ref[...].T
    f = jnp.zeros(x1_ref.shape, F32)
    for kk in range(TOP_K):
        f = f + gates[:, kk:kk + 1] * _load_subrows(y_ref, tm, sub, kk * sub, TOP_K * sub)
    o_ref[...] = _layer_norm(alpha * x1_ref[...] + f, g_ref[...], b_ref[...])


def _combine(x1, gates, dest, y_rows, ln_g, ln_b, alpha):
    T, D = x1.shape
    sub = D // LANES
    tm = COMBINE_ROWS
    yg = _sc_gather_slabs(y_rows.reshape(-1, sub, LANES), dest.T.reshape(-1)).reshape(-1, LANES)
    row = lambda n: pl.BlockSpec((tm, n), lambda i: (i, 0))
    full = lambda a, b: pl.BlockSpec((a, b), lambda i: (0, 0))
    return pl.pallas_call(
        functools.partial(_combine_kernel, alpha=alpha),
        grid=(T // tm,),
        in_specs=[row(D), pl.BlockSpec((SUBLANES, tm), lambda i: (0, i)),
                  pl.BlockSpec((tm * TOP_K * sub, LANES), lambda i: (i, 0)),
                  full(1, D), full(1, D)],
        out_specs=row(D),
        out_shape=jax.ShapeDtypeStruct((T, D), F32),
        compiler_params=_params("parallel"),
        name="combine",
    )(x1, gates, yg, ln_g.reshape(1, D), ln_b.reshape(1, D))


def kernel(x, mem, w_in_hgrn, hgrn_lb_logits, hgrn_norm_g, w_in_moba, w_mem_kv, w_o,
           ln_mix_g, ln_mix_b, w_router, b_router, w_gate, b_gate, w_up, b_up,
           w_down, b_down, ln_ffn_g, ln_ffn_b):
    B, S, D = x.shape
    T = B * S
    depth = w_o.shape[0]
    alpha = (2 * depth) ** 0.25

    p_lb = jax.nn.softmax(hgrn_lb_logits.astype(F32), axis=0)
    lower_bounds = jnp.cumsum(p_lb, axis=0) - p_lb[0]

    x2 = x.reshape(T, D)
    for layer in range(depth):
        j = layer // 2
        if layer % 2 == 0:
            (h2,) = _inproj(x2, w_in_hgrn[j].astype(BF16), for_moba=False)
            mix = _hgrn_mixer(h2.reshape(B, S, -1), lower_bounds[j], hgrn_norm_g[j])
        else:
            h2, kv2, q6 = _inproj(x2, w_in_moba[j].astype(BF16), for_moba=True)
            mix = _moba_mixer(h2.reshape(B, S, -1), kv2.reshape(B, S, -1), q6)
        kv = _memkv(mem, w_mem_kv[layer])
        x1, x1s, top_idx, gates, rank, counts = _post_mixer(
            x2, mix.reshape(T, MIX_WIDTH), h2, kv, w_o[layer].astype(BF16),
            ln_mix_g[layer], ln_mix_b[layer], w_router[layer], b_router[layer], S, alpha)
        dest, block_e, n_used = _route(top_idx, rank, counts[:, 0].astype(jnp.int32))
        xs = _dispatch(x1s, dest, (block_e.shape[0] + 1) * MOE_ROWS, D // LANES)
        y_rows = _experts(xs, block_e, n_used, layer, w_gate, b_gate, w_up, b_up, w_down, b_down)
        x2 = _combine(x1, gates, dest, y_rows, ln_ffn_g[layer], ln_ffn_b[layer], alpha)
    return x2.reshape(B, S, D)
```

```python
import functools
import math

import jax
import jax.numpy as jnp
from jax import lax
from jax.experimental import pallas as pl
from jax.experimental.pallas import tpu as pltpu
from jax.experimental.pallas import tpu_sc as plsc

MIX_WIDTH = 768
MEM_HEADS = 4
MEM_HEAD_DIM = 64
MEM_WIDTH = MEM_HEADS * MEM_HEAD_DIM
HGRN_HEADS = 6
HGRN_DK = 128
MOBA_HEADS = 12
MOBA_HEAD_DIM = 64
MOBA_BLOCK = 256
MOBA_TOPK = 3
N_EXPERTS = 32
TOP_K = 4
SWIGLU_ALPHA = 1.702
SWIGLU_LIMIT = 7.0
LN_EPS = 1e-5
RMS_EPS = 1e-6

LANES = 128
SUBLANES = 8
VMEM_LIMIT_BYTES = 56 * 1024 * 1024

INPROJ_ROWS = 512
HGRN_CHUNK = 64
HGRN_ROWS = 2048
POST_ROWS = 512
MOBA_TILE = 256
MOBA_TILE_GROUP = 32
MOBA_PLACE_GROUP = 8
MOBA_PICK_GROUP = 16
MOBA_MERGE_GROUP = 16
MOBA_OWN_GROUP = 16
MOE_ROWS = 512
EXPERT_COLS = 256
COMBINE_ROWS = 512
SC_WINDOW = 128
SC_IDX_ROWS = 8
SC_CHUNK = 32

BF16 = jnp.bfloat16
F32 = jnp.float32

_NT = (((1,), (1,)), ((), ()))
_TN = (((0,), (0,)), ((), ()))


def _alibi_slope_list(n):
    def pow2(m):
        start = 2.0 ** (-(2.0 ** -(math.log2(m) - 3)))
        return [start ** (i + 1) for i in range(m)]
    if math.log2(n).is_integer():
        return pow2(n)
    c = 2 ** math.floor(math.log2(n))
    return pow2(c) + _alibi_slope_list(2 * c)[0::2][:n - c]


def _sigmoid(x):
    return 1.0 / (1.0 + jnp.exp(-x))


def _params(*sem):
    return pltpu.CompilerParams(dimension_semantics=sem, vmem_limit_bytes=VMEM_LIMIT_BYTES)


def _store_subrows(ref, value, first=0, stride=None):
    sub = value.shape[1] // LANES
    stride = stride or sub
    for c in range(sub):
        ref[pl.ds(first + c, value.shape[0], stride=stride), :] = value[:, c * LANES:(c + 1) * LANES]


def _load_subrows(ref, rows, sub, first=0, stride=None):
    stride = stride or sub
    return jnp.concatenate(
        [ref[pl.ds(first + c, rows, stride=stride), :] for c in range(sub)], axis=1)


def _inproj_kernel(x_ref, w_ref, h_ref, *moba_refs):
    h = jnp.dot(x_ref[...].astype(BF16), w_ref[...], preferred_element_type=F32)
    h_ref[...] = h
    if moba_refs:
        kv_ref, q6_ref = moba_refs
        n_pairs = q6_ref.shape[0]
        for p in range(n_pairs):
            q6_ref[p] = h[:, p * LANES:(p + 1) * LANES] * (MOBA_HEAD_DIM ** -0.5)
            for part in range(2):
                col = (1 + part) * MIX_WIDTH + p * LANES
                kv_ref[:, (2 * p + part) * LANES:(2 * p + part + 1) * LANES] = (
                    h[:, col:col + LANES].astype(BF16))


def _inproj(x2, w_bf16, for_moba):
    T, D = x2.shape
    N = w_bf16.shape[1]
    tm = INPROJ_ROWS
    out_shape = [jax.ShapeDtypeStruct((T, N), F32)]
    out_specs = [pl.BlockSpec((tm, N), lambda i: (i, 0))]
    if for_moba:
        NP = MOBA_HEADS // 2
        out_shape += [jax.ShapeDtypeStruct((T, 2 * MIX_WIDTH), BF16),
                      jax.ShapeDtypeStruct((NP, T, LANES), F32)]
        out_specs += [pl.BlockSpec((tm, 2 * MIX_WIDTH), lambda i: (i, 0)),
                      pl.BlockSpec((NP, tm, LANES), lambda i: (0, i, 0))]
    return pl.pallas_call(
        _inproj_kernel,
        grid=(T // tm,),
        in_specs=[pl.BlockSpec((tm, D), lambda i: (i, 0)),
                  pl.BlockSpec((D, N), lambda i: (0, 0))],
        out_specs=out_specs,
        out_shape=out_shape,
        compiler_params=_params("parallel"),
        name="inproj",
    )(x2, w_bf16)


def _cumsum_rows(x, row):
    n = x.shape[0]
    sh = 1
    while sh < n:
        x = x + jnp.where(row >= sh, pltpu.roll(x, sh, 0), 0.0)
        sh *= 2
    return x


def _bcast_row(a, group, r):
    n = a.shape[0]
    a3 = a.reshape(n // group, group, LANES)
    return jnp.broadcast_to(a3[:, r:r + 1, :], a3.shape).reshape(n, LANES)


def _hgrn_chunk(qr, fr, v, gr, lb, ng, e_sum, st_t):
    C = qr.shape[0]
    row = lax.broadcasted_iota(jnp.int32, (C, LANES), 0)
    rr = lax.broadcasted_iota(jnp.int32, (C, C), 0)
    cc = lax.broadcasted_iota(jnp.int32, (C, C), 1)

    q = qr * _sigmoid(qr)
    forget = lb + (1.0 - lb) * _sigmoid(fr)
    k = 1.0 - forget
    G = _cumsum_rows(jnp.log(forget), row)

    z = jnp.log(k) - G
    parts = []
    for s in range(SUBLANES):
        parts.append((q * jnp.exp(jnp.minimum(G + _bcast_row(z, SUBLANES, s), 0.0))).astype(BF16))
    a_diag = jnp.dot(jnp.concatenate(parts, axis=1), e_sum, preferred_element_type=F32)
    A = jnp.where(((rr >> 3) == (cc >> 3)) & (cc <= rr), a_diag, 0.0)

    m = SUBLANES
    while m < C:
        lg = int(math.log2(m))
        Gr = _bcast_row(G, 2 * m, m - 1)
        second = ((row >> lg) & 1) == 1
        qm = q * jnp.exp(jnp.where(second, G - Gr, -jnp.inf))
        km = k * jnp.exp(jnp.where(second, -jnp.inf, Gr - G))
        am = lax.dot_general(qm.astype(BF16), km.astype(BF16), _NT, preferred_element_type=F32)
        A = A + jnp.where((rr >> (lg + 1)) == (cc >> (lg + 1)), am, 0.0)
        m *= 2

    vb = v.astype(BF16)
    o = jnp.dot(A.astype(BF16), vb, preferred_element_type=F32)
    o = o + lax.dot_general((q * jnp.exp(G)).astype(BF16), st_t.astype(BF16), _NT,
                            preferred_element_type=F32)
    g_end = G[C - 1:C, :]
    kd = (k * jnp.exp(g_end - G)).astype(BF16)
    st_new = st_t * jnp.exp(g_end) + lax.dot_general(vb, kd, _TN, preferred_element_type=F32)

    ms = jnp.mean(o * o, axis=-1, keepdims=True)
    out = o * lax.rsqrt(ms + RMS_EPS) * ng * _sigmoid(gr)
    return out, st_new


def _hgrn_kernel(q_ref, f_ref, i_ref, g_ref, lb_ref, ng_ref, e_ref, o_ref, st_ref, *, chunk):
    @pl.when(pl.program_id(2) == 0)
    def _():
        st_ref[...] = jnp.zeros_like(st_ref)

    lb = lb_ref[0]
    ng = ng_ref[...]
    e_sum = e_ref[...]
    n_chunks = q_ref.shape[1] // chunk
    for c in range(n_chunks):
        sl = pl.ds(c * chunk, chunk)
        out, st_new = _hgrn_chunk(q_ref[0, sl, :], f_ref[0, sl, :], i_ref[0, sl, :],
                                  g_ref[0, sl, :], lb, ng, e_sum, st_ref[...])
        st_ref[...] = st_new
        o_ref[0, sl, :] = out


def _hgrn_mixer(h3, lb, norm_g):
    B, S, _ = h3.shape
    ts = min(HGRN_ROWS, S)
    C = HGRN_CHUNK
    H = HGRN_HEADS
    e_sum = (jnp.arange(SUBLANES * LANES)[:, None] // LANES == jnp.arange(C)[None, :] % SUBLANES
             ).astype(BF16)
    col = lambda off: pl.BlockSpec((1, ts, LANES), lambda b, h, s, off=off: (b, s, off + h))
    return pl.pallas_call(
        functools.partial(_hgrn_kernel, chunk=C),
        grid=(B, H, S // ts),
        in_specs=[col(0), col(H), col(2 * H), col(3 * H),
                  pl.BlockSpec((1, 1, LANES), lambda b, h, s: (h, 0, 0)),
                  pl.BlockSpec((1, LANES), lambda b, h, s: (0, 0)),
                  pl.BlockSpec((SUBLANES * LANES, C), lambda b, h, s: (0, 0))],
        out_specs=pl.BlockSpec((1, ts, LANES), lambda b, h, s: (b, s, h)),
        out_shape=jax.ShapeDtypeStruct((B, S, MIX_WIDTH), F32),
        scratch_shapes=[pltpu.VMEM((HGRN_DK, HGRN_DK), F32)],
        compiler_params=_params("parallel", "parallel", "arbitrary"),
        name="hgrn",
    )(h3, h3, h3, h3, lb.reshape(H, 1, LANES), norm_g.reshape(1, LANES), e_sum)


def _moba_pick_kernel(q_ref, kf_ref, sr_ref, cnt_ref, kmean_ref, *, qblocks):
    BLK = MOBA_BLOCK
    W = qblocks * BLK
    nbp = kmean_ref.shape[0]
    i0 = pl.program_id(2) * qblocks

    @pl.when(pl.program_id(2) == 0)
    def _():
        kmean_ref[...] = jnp.zeros_like(kmean_ref)

    for j in range(qblocks):
        kmean_ref[pl.ds(i0 + j, 1), :] = jnp.mean(kf_ref[0, j * BLK:(j + 1) * BLK, :], axis=0,
                                                  keepdims=True)
    km = kmean_ref[...]
    lane_k = lax.broadcasted_iota(jnp.int32, (nbp, LANES), 1)
    km2 = jnp.concatenate([jnp.where(lane_k < MOBA_HEAD_DIM, km, 0.0),
                           jnp.where(lane_k >= MOBA_HEAD_DIM, km, 0.0)], axis=0)
    gate = lax.dot_general(km2, q_ref[0], _NT, precision=lax.Precision.HIGHEST,
                           preferred_element_type=F32).reshape(2, nbp, W)
    nblk = lax.broadcasted_iota(jnp.int32, (2, nbp, W), 1)
    qi = i0 + (lax.broadcasted_iota(jnp.int32, (2, 1, W), 2) >> int(math.log2(BLK)))
    g = jnp.where(nblk < qi, gate, -jnp.inf)
    picks = []
    for _ in range(MOBA_TOPK):
        mx = jnp.max(g, axis=1, keepdims=True)
        idx = jnp.min(jnp.where(g == mx, nblk, nbp), axis=1, keepdims=True)
        picks.append((idx, (mx > -jnp.inf) & (idx < qi)))
        g = jnp.where(nblk == idx, -jnp.inf, g)
    chosen = jnp.zeros((2, nbp, W), F32)
    for idx, valid in picks:
        chosen = chosen + jnp.where((nblk == idx) & valid, 1.0, 0.0)
    qa = lax.broadcasted_iota(jnp.int32, (BLK, BLK), 0)
    qc = lax.broadcasted_iota(jnp.int32, (BLK, BLK), 1)
    before = jnp.where(qa < qc, 1.0, 0.0).astype(BF16)
    chosen2 = chosen.reshape(2 * nbp, W).astype(BF16)
    earlier = jnp.concatenate(
        [jnp.dot(chosen2[:, j * BLK:(j + 1) * BLK], before, preferred_element_type=F32)
         for j in range(qblocks)], axis=1).reshape(2, nbp, W)
    orow = lax.broadcasted_iota(jnp.int32, (2 * SUBLANES, W), 0)
    out = jnp.zeros((2 * SUBLANES, W), jnp.int32)
    for j, (idx, valid) in enumerate(picks):
        rank = jnp.sum(jnp.where(nblk == idx, earlier, 0.0), axis=1, keepdims=True).astype(jnp.int32)
        sel = jnp.where(valid, idx, -1)
        for hh in range(2):
            rep = hh * MOBA_TOPK + j
            out = jnp.where(orow == rep, sel[hh], out)
            out = jnp.where(orow == SUBLANES + rep, rank[hh], out)
    sr_ref[0, 0] = out
    lane_c = lax.broadcasted_iota(jnp.int32, (nbp, LANES), 1)
    for j in range(qblocks):
        per_block = jnp.sum(chosen[:, :, j * BLK:(j + 1) * BLK], axis=2, keepdims=True)
        cnt_ref[0, 0, j * nbp:(j + 1) * nbp, :] = jnp.where(
            lane_c == 0, per_block[0], jnp.where(lane_c == 1, per_block[1], 0.0))


def _moba_place_kernel(sr_ref, tab_ref, idx_ref, *, group, spare):
    BLK = MOBA_BLOCK
    n_rep = 2 * MOBA_TOPK
    b = pl.program_id(0)
    p = pl.program_id(1)
    ig = pl.program_id(2)
    nrow = lax.broadcasted_iota(jnp.int32, (LANES, BLK), 0)
    orow = lax.broadcasted_iota(jnp.int32, (SC_IDX_ROWS, BLK), 0)
    qpos = lax.broadcasted_iota(jnp.int32, (1, BLK), 1)
    for g in range(group):
        blk = sr_ref[0, 0, :, g * BLK:(g + 1) * BLK]
        tab_t = tab_ref[0, 0, g * SUBLANES:(g + 1) * SUBLANES, :].T
        early = jnp.minimum(ig * group + g, MOBA_TOPK - 1)
        out = jnp.zeros((SC_IDX_ROWS, BLK), jnp.int32)
        for rep in range(n_rep):
            hh = rep // MOBA_TOPK
            sel = blk[rep:rep + 1, :]
            rank = blk[SUBLANES + rep:SUBLANES + rep + 1, :]
            start = jnp.sum(jnp.where(nrow == sel, tab_t[:, hh:hh + 1], 0.0), axis=0, keepdims=True)
            unused = spare + (((b * pl.num_programs(1) + p) * MOBA_TOPK + early) * n_rep + rep) * BLK
            dest = jnp.where(sel >= 0, start.astype(jnp.int32) + rank, unused + qpos)
            out = jnp.where(orow == rep, dest, out)
        idx_ref[:, g * BLK:(g + 1) * BLK] = out


def _moba_tiles_kernel(tb_ref, tp_ref, th_ref, tn_ref, nu_ref, sl_ref, q_ref, *refs, group):
    kv_refs, o_ref = refs[:group], refs[group]
    t = pl.program_id(0)
    tq = q_ref.shape[0] // group

    @pl.when(t * group < nu_ref[0])
    def _():
        lane = lax.broadcasted_iota(jnp.int32, (tq, LANES), 1)
        kpos = lax.broadcasted_iota(jnp.int32, (1, MOBA_BLOCK), 1)
        for g in range(group):
            tt = t * group + g
            hh = th_ref[tt]
            slope = sl_ref[2 * tp_ref[tt] + hh]
            head = (lane >> 6) == hh
            rows = pl.ds(g * tq, tq)
            q = jnp.where(head, q_ref[rows, :], 0.0).astype(BF16)
            s = lax.dot_general(q, kv_refs[g][0, :, :LANES], _NT, preferred_element_type=F32)
            s = s + slope * (kpos + tn_ref[tt] * MOBA_BLOCK).astype(F32)
            m = jnp.max(s, axis=1, keepdims=True)
            pr = jnp.exp(s - m)
            l = jnp.sum(pr, axis=1, keepdims=True)
            o = jnp.dot(pr.astype(BF16), kv_refs[g][0, :, LANES:], preferred_element_type=F32) / l
            o_ref[rows, :] = jnp.where(head, o, m + jnp.log(l))


def _moba_own_kernel(sl_ref, q_ref, kv_ref, o_ref, lse_ref):
    BLK = MOBA_BLOCK
    p = pl.program_id(1)
    n_blocks = q_ref.shape[1] // BLK
    lane = lax.broadcasted_iota(jnp.int32, (BLK, LANES), 1)
    rr = lax.broadcasted_iota(jnp.int32, (BLK, BLK), 0)
    cc = lax.broadcasted_iota(jnp.int32, (BLK, BLK), 1)
    first = lane < MOBA_HEAD_DIM
    for h in range(n_blocks):
        i = pl.program_id(2) * n_blocks + h
        rows = pl.ds(h * BLK, BLK)
        qf = q_ref[0, rows, :]
        k_own = kv_ref[0, rows, :LANES]
        v_own = kv_ref[0, rows, LANES:]
        key_pos = (lax.broadcasted_iota(jnp.int32, (1, BLK), 1) + i * BLK).astype(F32)
        outs, lses = [], []
        for hh in range(2):
            head = (lane >> 6) == hh
            slope = sl_ref[2 * p + hh]
            qh = jnp.where(head, qf * (MOBA_HEAD_DIM ** -0.5), 0.0).astype(BF16)
            s = lax.dot_general(qh, k_own, _NT, preferred_element_type=F32)
            s = jnp.where(cc <= rr, s + slope * key_pos, -jnp.inf)
            m = jnp.max(s, axis=1, keepdims=True)
            pr = jnp.exp(s - m)
            l = jnp.sum(pr, axis=1, keepdims=True)
            lses.append(m + jnp.log(l))
            outs.append(jnp.dot(pr.astype(BF16), v_own, preferred_element_type=F32) / l)
        o_ref[0, rows, :] = jnp.where(first, outs[0], outs[1])
        lse_ref[0, rows, :] = jnp.where(first, lses[0], lses[1])


def _moba_merge_kernel(o_ref, lse_ref, pg_ref, out_ref):
    BLK = MOBA_BLOCK
    first = lax.broadcasted_iota(jnp.int32, (BLK, LANES), 1) < MOBA_HEAD_DIM
    for h in range(o_ref.shape[1] // BLK):
        i = pl.program_id(2) * (o_ref.shape[1] // BLK) + h
        rows = pl.ds(h * BLK, BLK)
        lses = [lse_ref[0, rows, :]]
        vals = [o_ref[0, rows, :]]
        for j in range(MOBA_TOPK):
            p0 = pg_ref[j, 0, rows, :]
            p1 = pg_ref[MOBA_TOPK + j, 0, rows, :]
            has_block = j < i
            stats = pltpu.roll(jnp.where(first, p1, p0), MOBA_HEAD_DIM, 1)
            lses.append(jnp.where(has_block, stats, -jnp.inf))
            vals.append(jnp.where(has_block, jnp.where(first, p0, p1), 0.0))
        top = functools.reduce(jnp.maximum, lses)
        ws = [jnp.exp(x - top) for x in lses]
        out_ref[0, rows, :] = sum(w * v for w, v in zip(ws, vals)) / sum(ws)


def _moba_mixer(h3, kv3, q6):
    B, S, _ = h3.shape
    T = B * S
    BLK = MOBA_BLOCK
    TQ = MOBA_TILE
    G = MOBA_TILE_GROUP
    NB = S // BLK
    GI = math.gcd(NB, MOBA_PLACE_GROUP)
    nbp = -(-NB // SUBLANES) * SUBLANES
    NP = MOBA_HEADS // 2
    n_rep = 2 * MOBA_TOPK
    slopes = jnp.asarray(_alibi_slope_list(MOBA_HEADS), F32)

    QB = math.gcd(NB, MOBA_PICK_GROUP)
    selrank, counts = pl.pallas_call(
        functools.partial(_moba_pick_kernel, qblocks=QB),
        grid=(B, NP, NB // QB),
        in_specs=[pl.BlockSpec((1, QB * BLK, LANES), lambda b, p, i: (b, i, p)),
                  pl.BlockSpec((1, QB * BLK, LANES), lambda b, p, i: (b, i, NP + p))],
        out_specs=[pl.BlockSpec((1, 1, 2 * SUBLANES, QB * BLK), lambda b, p, i: (b, p, 0, i)),
                   pl.BlockSpec((1, 1, QB * nbp, LANES), lambda b, p, i: (b, p, i, 0))],
        out_shape=[jax.ShapeDtypeStruct((B, NP, 2 * SUBLANES, S), jnp.int32),
                   jax.ShapeDtypeStruct((B, NP, NB * nbp, LANES), F32)],
        scratch_shapes=[pltpu.VMEM((nbp, LANES), F32)],
        compiler_params=_params("parallel", "parallel", "arbitrary"),
        name="moba_pick",
    )(h3, h3)

    OB = math.gcd(NB, MOBA_OWN_GROUP)
    blk_spec = lambda w: pl.BlockSpec((1, OB * BLK, w), lambda b, p, i, sl: (b, i, p))
    own_o, own_lse = pl.pallas_call(
        _moba_own_kernel,
        grid_spec=pltpu.PrefetchScalarGridSpec(
            num_scalar_prefetch=1,
            grid=(B, NP, NB // OB),
            in_specs=[blk_spec(LANES), blk_spec(2 * LANES)],
            out_specs=[blk_spec(LANES), blk_spec(LANES)]),
        out_shape=[jax.ShapeDtypeStruct((B, S, MIX_WIDTH), F32)] * 2,
        compiler_params=_params("parallel", "parallel", "parallel"),
        name="moba_own",
    )(slopes, h3, kv3)

    cnt = counts.reshape(B, NP, NB, nbp, LANES)[:, :, :, :NB, :2].astype(jnp.int32)
    cnt = cnt.transpose(0, 1, 2, 4, 3)
    base = jnp.cumsum(cnt, axis=2) - cnt
    total = jnp.sum(cnt, axis=2)
    padded = (total + TQ - 1) // TQ * TQ
    pend = jnp.cumsum(padded.reshape(-1))
    seg_start = (pend - padded.reshape(-1)).reshape(B, NP, 1, 2, NB)
    table = jnp.zeros((B, NP, NB, SUBLANES, LANES), F32).at[:, :, :, :2, :NB].set(
        (seg_start + base).astype(F32)).reshape(B, NP, NB * SUBLANES, LANES)
    n_seg = B * NP * 2 * NB
    max_tiles = -(-((T * NP * n_rep) // TQ + n_seg) // G) * G
    first_row = jnp.arange(max_tiles, dtype=jnp.int32) * TQ
    tile_seg = jnp.minimum(jnp.sum(pend[None, :] <= first_row[:, None], axis=1), n_seg - 1)
    tile_n = (tile_seg % NB).astype(jnp.int32)
    tile_h = ((tile_seg // NB) % 2).astype(jnp.int32)
    tile_p = ((tile_seg // (2 * NB)) % NP).astype(jnp.int32)
    tile_b = (tile_seg // (2 * NB * NP)).astype(jnp.int32)
    n_used = (pend[-1] // TQ).astype(jnp.int32).reshape(1)
    spare = max_tiles * TQ
    n_rows = spare + max(B * NP * MOBA_TOPK * n_rep * BLK, G * TQ)

    idx = pl.pallas_call(
        functools.partial(_moba_place_kernel, group=GI, spare=spare),
        grid=(B, NP, NB // GI),
        in_specs=[pl.BlockSpec((1, 1, 2 * SUBLANES, GI * BLK), lambda b, p, i: (b, p, 0, i)),
                  pl.BlockSpec((1, 1, GI * SUBLANES, LANES), lambda b, p, i: (b, p, i, 0))],
        out_specs=pl.BlockSpec((SC_IDX_ROWS, GI * BLK),
                               lambda b, p, i: (0, p * (T // (GI * BLK)) + b * (NB // GI) + i)),
        out_shape=jax.ShapeDtypeStruct((SC_IDX_ROWS, NP * T), jnp.int32),
        compiler_params=_params("parallel", "parallel", "parallel"),
        name="moba_place",
    )(selrank, table)

    qs = _sc_scatter_rows(q6.reshape(NP * T, LANES), idx, n_rep, n_rows)

    live = lambda t, tb, tp, th, tn, nu, sl: (jnp.where(t * G < nu[0], t, max_tiles // G), 0)
    kv_blk = lambda g: pl.BlockSpec(
        (1, BLK, 2 * LANES),
        lambda t, tb, tp, th, tn, nu, sl: (tb[t * G + g], tn[t * G + g], tp[t * G + g]))
    part = pl.pallas_call(
        functools.partial(_moba_tiles_kernel, group=G),
        grid_spec=pltpu.PrefetchScalarGridSpec(
            num_scalar_prefetch=6,
            grid=(max_tiles // G,),
            in_specs=[pl.BlockSpec((G * TQ, LANES), live)] + [kv_blk(g) for g in range(G)],
            out_specs=pl.BlockSpec((G * TQ, LANES), live)),
        out_shape=jax.ShapeDtypeStruct((n_rows, LANES), F32),
        compiler_params=_params("arbitrary"),
        name="moba_tiles",
    )(tile_b, tile_p, tile_h, tile_n, n_used, slopes, qs, *([kv3] * G))

    pg = _sc_gather_rows(part, idx[:n_rep].reshape(-1)).reshape(n_rep, NP, T, LANES)

    MB = math.gcd(NB, MOBA_MERGE_GROUP)
    blk = pl.BlockSpec((1, MB * BLK, LANES), lambda b, p, i: (b, i, p))
    return pl.pallas_call(
        _moba_merge_kernel,
        grid=(B, NP, NB // MB),
        in_specs=[blk, blk,
                  pl.BlockSpec((n_rep, 1, MB * BLK, LANES),
                               lambda b, p, i: (0, p, b * (NB // MB) + i, 0))],
        out_specs=blk,
        out_shape=jax.ShapeDtypeStruct((B, S, MIX_WIDTH), F32),
        compiler_params=_params("parallel", "parallel", "parallel"),
        name="moba_merge",
    )(own_o, own_lse, pg)


def _memkv_kernel(mem_ref, w_ref, kv_ref):
    kv_ref[0] = jnp.dot(mem_ref[0].astype(BF16), w_ref[...].astype(BF16),
                        preferred_element_type=F32).astype(BF16)


def _memkv(mem, w_kv):
    B, M, D = mem.shape
    N = w_kv.shape[1]
    return pl.pallas_call(
        _memkv_kernel,
        grid=(B,),
        in_specs=[pl.BlockSpec((1, M, D), lambda b: (b, 0, 0)),
                  pl.BlockSpec((D, N), lambda b: (0, 0))],
        out_specs=pl.BlockSpec((1, M, N), lambda b: (b, 0, 0)),
        out_shape=jax.ShapeDtypeStruct((B, M, N), BF16),
        compiler_params=_params("parallel"),
        name="memkv",
    )(mem, w_kv)


def _layer_norm(z, g, b):
    mu = jnp.mean(z, axis=-1, keepdims=True)
    zc = z - mu
    var = jnp.mean(zc * zc, axis=-1, keepdims=True)
    return zc * lax.rsqrt(var + LN_EPS) * g + b


def _post_kernel(x_ref, mix_ref, mq_ref, kv_ref, wo_ref, g_ref, b_ref, wr_ref, br_ref,
                 x1_ref, x1s_ref, idx_ref, gate_ref, rank_ref, cnt_ref, run_ref, *, alpha):
    tm = x_ref.shape[0]
    mq = mq_ref[...]
    kv = kv_ref[0]
    km = kv[:, :MEM_WIDTH]
    vm = kv[:, MEM_WIDTH:]
    lane = lax.broadcasted_iota(jnp.int32, (tm, MEM_WIDTH), 1)
    scale = MEM_HEAD_DIM ** -0.5
    mo = jnp.zeros((tm, MEM_WIDTH), F32)
    for hd in range(MEM_HEADS):
        head = (lane >> 6) == hd
        qh = jnp.where(head, mq * scale, 0.0).astype(BF16)
        s = lax.dot_general(qh, km, _NT, preferred_element_type=F32)
        m = jnp.max(s, axis=1, keepdims=True)
        p = jnp.exp(s - m)
        l = jnp.sum(p, axis=1, keepdims=True)
        oh = jnp.dot(p.astype(BF16), vm, preferred_element_type=F32) / l
        mo = jnp.where(head, oh, mo)

    y = jnp.dot(mix_ref[...].astype(BF16), wo_ref[:MIX_WIDTH, :], preferred_element_type=F32)
    y = y + jnp.dot(mo.astype(BF16), wo_ref[MIX_WIDTH:, :], preferred_element_type=F32)
    x1 = _layer_norm(alpha * x_ref[...] + y, g_ref[...], b_ref[...])
    x1_ref[...] = x1
    _store_subrows(x1s_ref, x1)

    x_hi = x1.astype(BF16)
    x_lo = (x1 - x_hi.astype(F32)).astype(BF16)
    hi = jnp.dot(x_hi, wr_ref[...], preferred_element_type=F32)
    lo = jnp.dot(x_lo, wr_ref[:, :LANES], preferred_element_type=F32)
    logits = hi[:, :LANES] + hi[:, LANES:] + lo
    g = logits.T[:N_EXPERTS] + br_ref[...]
    erow = lax.broadcasted_iota(jnp.int32, (N_EXPERTS, tm), 0)
    orow = lax.broadcasted_iota(jnp.int32, (SUBLANES, tm), 0)
    idx_out = jnp.zeros((SUBLANES, tm), jnp.int32)
    vals, picks = [], []
    chosen = jnp.zeros((N_EXPERTS, tm), F32)
    for kk in range(TOP_K):
        mx = jnp.max(g, axis=0, keepdims=True)
        idx = jnp.min(jnp.where(g == mx, erow, N_EXPERTS), axis=0, keepdims=True)
        idx_out = jnp.where(orow == kk, idx, idx_out)
        vals.append(mx)
        picks.append(idx)
        chosen = chosen + jnp.where(erow == idx, 1.0, 0.0)
        g = jnp.where(erow == idx, -jnp.inf, g)
    evs = [jnp.exp(v - vals[0]) for v in vals]
    den = sum(evs)
    gate_out = jnp.zeros((SUBLANES, tm), F32)
    for kk in range(TOP_K):
        gate_out = jnp.where(orow == kk, evs[kk] / den, gate_out)
    idx_ref[...] = idx_out
    gate_ref[...] = gate_out

    @pl.when(pl.program_id(0) == 0)
    def _():
        run_ref[...] = jnp.zeros_like(run_ref)

    ta = lax.broadcasted_iota(jnp.int32, (tm, tm), 0)
    tb = lax.broadcasted_iota(jnp.int32, (tm, tm), 1)
    before = jnp.where(ta < tb, 1.0, 0.0).astype(BF16)
    earlier = jnp.dot(chosen.astype(BF16), before, preferred_element_type=F32) + run_ref[:, :1]
    rank_out = jnp.zeros((SUBLANES, tm), jnp.int32)
    for kk in range(TOP_K):
        rank = jnp.sum(jnp.where(erow == picks[kk], earlier, 0.0), axis=0, keepdims=True)
        rank_out = jnp.where(orow == kk, rank.astype(jnp.int32), rank_out)
    rank_ref[...] = rank_out
    run_ref[...] = run_ref[...] + jnp.sum(chosen, axis=1, keepdims=True)
    cnt_ref[...] = run_ref[...]


def _post_mixer(x2, mix2, h2, kv, w_o_bf16, ln_g, ln_b, w_router, b_router, seq_len, alpha):
    T, D = x2.shape
    tm = POST_ROWS
    N = h2.shape[1]
    M = kv.shape[1]
    mq_col = (N - MEM_WIDTH) // MEM_WIDTH
    tiles_per_seq = seq_len // tm
    wr = jnp.zeros((D, LANES), F32).at[:, :N_EXPERTS].set(w_router)
    wr_hi = wr.astype(BF16)
    wr = jnp.concatenate([wr_hi, (wr - wr_hi.astype(F32)).astype(BF16)], axis=1)
    br = b_router.reshape(N_EXPERTS, 1)
    row = lambda n: pl.BlockSpec((tm, n), lambda i: (i, 0))
    full = lambda a, b: pl.BlockSpec((a, b), lambda i: (0, 0))
    per_token = pl.BlockSpec((SUBLANES, tm), lambda i: (0, i))
    return pl.pallas_call(
        functools.partial(_post_kernel, alpha=alpha),
        grid=(T // tm,),
        in_specs=[row(D), row(MIX_WIDTH),
                  pl.BlockSpec((tm, MEM_WIDTH), lambda i: (i, mq_col)),
                  pl.BlockSpec((1, M, 2 * MEM_WIDTH), lambda i: (i // tiles_per_seq, 0, 0)),
                  full(D, D), full(1, D), full(1, D),
                  full(D, 2 * LANES), full(N_EXPERTS, 1)],
        out_specs=[row(D), pl.BlockSpec((tm * (D // LANES), LANES), lambda i: (i, 0)),
                   per_token, per_token, per_token, full(N_EXPERTS, LANES)],
        out_shape=[jax.ShapeDtypeStruct((T, D), F32),
                   jax.ShapeDtypeStruct((T * (D // LANES), LANES), F32),
                   jax.ShapeDtypeStruct((SUBLANES, T), jnp.int32),
                   jax.ShapeDtypeStruct((SUBLANES, T), F32),
                   jax.ShapeDtypeStruct((SUBLANES, T), jnp.int32),
                   jax.ShapeDtypeStruct((N_EXPERTS, LANES), F32)],
        scratch_shapes=[pltpu.VMEM((N_EXPERTS, LANES), F32)],
        compiler_params=_params("arbitrary"),
        name="post_mixer",
    )(x2, mix2, h2, kv, w_o_bf16, ln_g.reshape(1, D), ln_b.reshape(1, D), wr, br)


def _sc_mesh():
    return plsc.VectorSubcoreMesh(core_axis_name="core", subcore_axis_name="subcore")


def _sc_scatter_rows(rows, idx, n_rep, n_out):
    R, W = rows.shape

    @functools.partial(pl.kernel, out_type=jax.ShapeDtypeStruct((n_out, W), rows.dtype),
                       mesh=_sc_mesh(), scratch_types=[])
    def scatter(x_hbm, i_hbm, o_hbm):
        def body(x_vmem, i_vmem):
            for r in range(n_rep):
                pltpu.sync_copy(x_vmem, o_hbm.at[i_vmem.at[r]])

        pltpu.emit_pipeline(
            body, grid=(R // SC_WINDOW,),
            in_specs=[pl.BlockSpec((SC_WINDOW, W), lambda i: (i, 0)),
                      pl.BlockSpec((SC_IDX_ROWS, SC_WINDOW), lambda i: (0, i))],
            out_specs=[], core_axis_name=("core", "subcore"),
            dimension_semantics=(pltpu.PARALLEL,), trace_scopes=False)(x_hbm, i_hbm)

    return scatter(rows, idx)


def _sc_gather_rows(table, idx):
    n = idx.shape[0]
    W = table.shape[1]

    @functools.partial(pl.kernel, out_type=jax.ShapeDtypeStruct((n, W), table.dtype),
                       mesh=_sc_mesh(), scratch_types=[])
    def gather(t_hbm, i_hbm, o_hbm):
        def body(i_vmem, o_vmem):
            pltpu.sync_copy(t_hbm.at[i_vmem.at[0]], o_vmem)

        pltpu.emit_pipeline(
            body, grid=(n // SC_WINDOW,),
            in_specs=[pl.BlockSpec((1, SC_WINDOW), lambda i: (0, i))],
            out_specs=[pl.BlockSpec((SC_WINDOW, W), lambda i: (i, 0))],
            core_axis_name=("core", "subcore"),
            dimension_semantics=(pltpu.PARALLEL,), trace_scopes=False)(i_hbm, o_hbm)

    return gather(table, idx.reshape(1, n))


def _sc_workers():
    info = pltpu.get_tpu_info().sparse_core
    return info.num_cores, info.num_cores * info.num_subcores


def _sc_scatter_slabs(rows, idx, n_rep, n_out):
    R, S, W = rows.shape
    n_cores, n_workers = _sc_workers()
    per_worker = (R // SC_WINDOW) // n_workers
    assert per_worker * n_workers * SC_WINDOW == R

    @functools.partial(pl.kernel, out_type=jax.ShapeDtypeStruct((n_out, S, W), rows.dtype),
                       mesh=_sc_mesh(),
                       scratch_types=[pltpu.VMEM((SC_IDX_ROWS, SC_WINDOW), jnp.int32),
                                      pltpu.VMEM((SC_CHUNK, S, W), rows.dtype)])
    def scatter(x_hbm, i_hbm, o_hbm, ibuf, buf):
        wid = lax.axis_index("subcore") * n_cores + lax.axis_index("core")

        @pl.loop(0, per_worker)
        def _(s):
            first = (wid * per_worker + s) * SC_WINDOW
            pltpu.sync_copy(i_hbm.at[:, pl.ds(first, SC_WINDOW)], ibuf)
            for c in range(SC_WINDOW // SC_CHUNK):
                pltpu.sync_copy(x_hbm.at[pl.ds(first + c * SC_CHUNK, SC_CHUNK)], buf)
                for r in range(n_rep):
                    pltpu.sync_copy(buf, o_hbm.at[ibuf.at[r, pl.ds(c * SC_CHUNK, SC_CHUNK)]])

    return scatter(rows, idx)


def _sc_gather_slabs(table, idx):
    n = idx.shape[0]
    S, W = table.shape[1:]
    n_cores, n_workers = _sc_workers()
    per_worker = (n // SC_WINDOW) // n_workers
    assert per_worker * n_workers * SC_WINDOW == n
    n_chunks = SC_WINDOW // SC_CHUNK

    @functools.partial(pl.kernel, out_type=jax.ShapeDtypeStruct((n, S, W), table.dtype),
                       mesh=_sc_mesh(),
                       scratch_types=[pltpu.VMEM((1, SC_WINDOW), jnp.int32),
                                      pltpu.VMEM((2, SC_CHUNK, S, W), table.dtype),
                                      pltpu.SemaphoreType.DMA((2,)), pltpu.SemaphoreType.DMA((2,))])
    def gather(t_hbm, i_hbm, o_hbm, ibuf, buf, fetch_sem, store_sem):
        wid = lax.axis_index("subcore") * n_cores + lax.axis_index("core")

        @pl.loop(0, per_worker)
        def _(s):
            blk = wid * per_worker + s
            pltpu.sync_copy(i_hbm.at[pl.ds(blk, 1)], ibuf)

            def fetch(c):
                return pltpu.make_async_copy(
                    t_hbm.at[ibuf.at[0, pl.ds(c * SC_CHUNK, SC_CHUNK)]], buf.at[c % 2],
                    fetch_sem.at[c % 2])

            def store(c):
                return pltpu.make_async_copy(
                    buf.at[c % 2], o_hbm.at[pl.ds(blk * SC_WINDOW + c * SC_CHUNK, SC_CHUNK)],
                    store_sem.at[c % 2])

            fetch(0).start()
            for c in range(n_chunks):
                if c + 1 < n_chunks:
                    if c >= 1:
                        store(c - 1).wait()
                    fetch(c + 1).start()
                fetch(c).wait()
                store(c).start()
            store(n_chunks - 2).wait()
            store(n_chunks - 1).wait()

    return gather(table, idx.reshape(n // SC_WINDOW, SC_WINDOW))


def _route(top_idx, rank, counts):
    rb = MOE_ROWS
    n_tokens = top_idx.shape[1]
    tk = n_tokens * TOP_K
    padded = (counts + rb - 1) // rb * rb
    pend = jnp.cumsum(padded)
    pstart = pend - padded
    experts = jnp.arange(N_EXPERTS, dtype=jnp.int32)
    start = jnp.sum(jnp.where(top_idx[:TOP_K, :, None] == experts, pstart, 0), axis=2)
    dest = (start + rank[:TOP_K]).astype(jnp.int32)
    n_blocks = tk // rb + N_EXPERTS
    first_row = jnp.arange(n_blocks, dtype=jnp.int32) * rb
    block_e = jnp.minimum(jnp.sum(pend[None, :] <= first_row[:, None], axis=1),
                          N_EXPERTS - 1).astype(jnp.int32)
    n_used = (pend[-1] // rb).astype(jnp.int32).reshape(1)
    return dest, block_e, n_used


def _dispatch(x1s, dest, n_rows, sub):
    T = dest.shape[1]
    idx = jnp.concatenate([dest, jnp.zeros((SC_IDX_ROWS - TOP_K, T), jnp.int32)], axis=0)
    xs = _sc_scatter_slabs(x1s.reshape(T, sub, LANES), idx, TOP_K, n_rows)
    return xs.reshape(n_rows * sub, LANES)


def _expert_kernel(be_ref, nu_ref, x_ref, wg_ref, bg_ref, wu_ref, bu_ref, wd_ref, bd_ref,
                   y_ref, wgb, wub, wdb):
    i = pl.program_id(0)
    prev = be_ref[jnp.maximum(i - 1, 0)]

    @pl.when((i == 0) | (be_ref[i] != prev))
    def _():
        wgb[...] = wg_ref[0, 0].astype(BF16)
        wub[...] = wu_ref[0, 0].astype(BF16)
        wdb[...] = wd_ref[0, 0].astype(BF16)

    @pl.when(i < nu_ref[0])
    def _():
        sub = wgb.shape[0] // LANES
        xb = _load_subrows(x_ref, x_ref.shape[0] // sub, sub).astype(BF16)
        hid = []
        for n in range(wgb.shape[1] // EXPERT_COLS):
            cols = slice(n * EXPERT_COLS, (n + 1) * EXPERT_COLS)
            gate = jnp.dot(xb, wgb[:, cols], preferred_element_type=F32) + bg_ref[0, 0][:, cols]
            gate = jnp.minimum(gate, SWIGLU_LIMIT)
            up = jnp.dot(xb, wub[:, cols], preferred_element_type=F32) + bu_ref[0, 0][:, cols]
            up = jnp.clip(up, -SWIGLU_LIMIT, SWIGLU_LIMIT)
            hid.append((gate * _sigmoid(SWIGLU_ALPHA * gate) * (up + 1.0)).astype(BF16))
        y = jnp.dot(jnp.concatenate(hid, axis=1), wdb[...], preferred_element_type=F32) + bd_ref[0, 0]
        _store_subrows(y_ref, y)


def _experts(xs, block_e, n_used, layer, w_gate, b_gate, w_up, b_up, w_down, b_down):
    rb = MOE_ROWS
    n_blocks = block_e.shape[0]
    E, D, F = w_gate.shape[1:]
    sub = D // LANES
    wspec = lambda a, b: pl.BlockSpec((1, 1, a, b), lambda i, be, nu: (layer, be[i], 0, 0))
    live = lambda i, be, nu: (jnp.where(i < nu[0], i, n_blocks), 0)
    grid_spec = pltpu.PrefetchScalarGridSpec(
        num_scalar_prefetch=2,
        grid=(n_blocks,),
        in_specs=[pl.BlockSpec((rb * sub, LANES), live),
                  wspec(D, F), wspec(1, F), wspec(D, F), wspec(1, F), wspec(F, D), wspec(1, D)],
        out_specs=pl.BlockSpec((rb * sub, LANES), live),
        scratch_shapes=[pltpu.VMEM((D, F), BF16), pltpu.VMEM((D, F), BF16),
                        pltpu.VMEM((F, D), BF16)],
    )
    depth = w_gate.shape[0]
    return pl.pallas_call(
        _expert_kernel,
        grid_spec=grid_spec,
        out_shape=jax.ShapeDtypeStruct(xs.shape, F32),
        compiler_params=_params("arbitrary"),
        name="experts",
    )(block_e, n_used, xs, w_gate, b_gate.reshape(depth, E, 1, F),
      w_up, b_up.reshape(depth, E, 1, F), w_down, b_down.reshape(depth, E, 1, D))


def _combine_kernel(x1_ref, gate_ref, y_ref, g_ref, b_ref, o_ref, *, alpha):
    tm, D = x1_ref.shape
    sub = D // LANES
    gates = gate_ref[...].T
    f = jnp.zeros(x1_ref.shape, F32)
    for kk in range(TOP_K):
        f = f + gates[:, kk:kk + 1] * _load_subrows(y_ref, tm, sub, kk * sub, TOP_K * sub)
    o_ref[...] = _layer_norm(alpha * x1_ref[...] + f, g_ref[...], b_ref[...])


def _combine(x1, gates, dest, y_rows, ln_g, ln_b, alpha):
    T, D = x1.shape
    sub = D // LANES
    tm = COMBINE_ROWS
    yg = _sc_gather_slabs(y_rows.reshape(-1, sub, LANES), dest.T.reshape(-1)).reshape(-1, LANES)
    row = lambda n: pl.BlockSpec((tm, n), lambda i: (i, 0))
    full = lambda a, b: pl.BlockSpec((a, b), lambda i: (0, 0))
    return pl.pallas_call(
        functools.partial(_combine_kernel, alpha=alpha),
        grid=(T // tm,),
        in_specs=[row(D), pl.BlockSpec((SUBLANES, tm), lambda i: (0, i)),
                  pl.BlockSpec((tm * TOP_K * sub, LANES), lambda i: (i, 0)),
                  full(1, D), full(1, D)],
        out_specs=row(D),
        out_shape=jax.ShapeDtypeStruct((T, D), F32),
        compiler_params=_params("parallel"),
        name="combine",
    )(x1, gates, yg, ln_g.reshape(1, D), ln_b.reshape(1, D))


def kernel(x, mem, w_in_hgrn, hgrn_lb_logits, hgrn_norm_g, w_in_moba, w_mem_kv, w_o,
           ln_mix_g, ln_mix_b, w_router, b_router, w_gate, b_gate, w_up, b_up,
           w_down, b_down, ln_ffn_g, ln_ffn_b):
    B, S, D = x.shape
    T = B * S
    depth = w_o.shape[0]
    alpha = (2 * depth) ** 0.25

    p_lb = jax.nn.softmax(hgrn_lb_logits.astype(F32), axis=0)
    lower_bounds = jnp.cumsum(p_lb, axis=0) - p_lb[0]

    x2 = x.reshape(T, D)
    for layer in range(depth):
        j = layer // 2
        if layer % 2 == 0:
            (h2,) = _inproj(x2, w_in_hgrn[j].astype(BF16), for_moba=False)
            mix = _hgrn_mixer(h2.reshape(B, S, -1), lower_bounds[j], hgrn_norm_g[j])
        else:
            h2, kv2, q6 = _inproj(x2, w_in_moba[j].astype(BF16), for_moba=True)
            mix = _moba_mixer(h2.reshape(B, S, -1), kv2.reshape(B, S, -1), q6)
        kv = _memkv(mem, w_mem_kv[layer])
        x1, x1s, top_idx, gates, rank, counts = _post_mixer(
            x2, mix.reshape(T, MIX_WIDTH), h2, kv, w_o[layer].astype(BF16),
            ln_mix_g[layer], ln_mix_b[layer], w_router[layer], b_router[layer], S, alpha)
        dest, block_e, n_used = _route(top_idx, rank, counts[:, 0].astype(jnp.int32))
        xs = _dispatch(x1s, dest, (block_e.shape[0] + 1) * MOE_ROWS, D // LANES)
        y_rows = _experts(xs, block_e, n_used, layer, w_gate, b_gate, w_up, b_up, w_down, b_down)
        x2 = _combine(x1, gates, dest, y_rows, ln_ffn_g[layer], ln_ffn_b[layer], alpha)
    return x2.reshape(B, S, D)
```

```python
import functools
import math

import jax
import jax.numpy as jnp
from jax import lax
from jax.experimental import pallas as pl
from jax.experimental.pallas import tpu as pltpu
from jax.experimental.pallas import tpu_sc as plsc

MIX_WIDTH = 768
MEM_HEADS = 4
MEM_HEAD_DIM = 64
MEM_WIDTH = MEM_HEADS * MEM_HEAD_DIM
HGRN_HEADS = 6
HGRN_DK = 128
MOBA_HEADS = 12
MOBA_HEAD_DIM = 64
MOBA_BLOCK = 256
MOBA_TOPK = 3
N_EXPERTS = 32
TOP_K = 4
SWIGLU_ALPHA = 1.702
SWIGLU_LIMIT = 7.0
LN_EPS = 1e-5
RMS_EPS = 1e-6

LANES = 128
SUBLANES = 8
VMEM_LIMIT_BYTES = 56 * 1024 * 1024

INPROJ_ROWS = 512
HGRN_CHUNK = 64
HGRN_ROWS = 2048
POST_ROWS = 512
MOBA_TILE = 256
MOBA_TILE_GROUP = 32
MOBA_PLACE_GROUP = 8
MOBA_PICK_GROUP = 16
MOBA_MERGE_GROUP = 16
MOBA_OWN_GROUP = 16
MOE_ROWS = 512
COMBINE_ROWS = 512
SC_WINDOW = 128
SC_IDX_ROWS = 8
SC_CHUNK = 32

BF16 = jnp.bfloat16
F32 = jnp.float32

_NT = (((1,), (1,)), ((), ()))
_TN = (((0,), (0,)), ((), ()))


def _alibi_slope_list(n):
    def pow2(m):
        start = 2.0 ** (-(2.0 ** -(math.log2(m) - 3)))
        return [start ** (i + 1) for i in range(m)]
    if math.log2(n).is_integer():
        return pow2(n)
    c = 2 ** math.floor(math.log2(n))
    return pow2(c) + _alibi_slope_list(2 * c)[0::2][:n - c]


def _sigmoid(x):
    return 1.0 / (1.0 + jnp.exp(-x))


def _params(*sem):
    return pltpu.CompilerParams(dimension_semantics=sem, vmem_limit_bytes=VMEM_LIMIT_BYTES)


def _store_subrows(ref, value, first=0, stride=None):
    sub = value.shape[1] // LANES
    stride = stride or sub
    for c in range(sub):
        ref[pl.ds(first + c, value.shape[0], stride=stride), :] = value[:, c * LANES:(c + 1) * LANES]


def _load_subrows(ref, rows, sub, first=0, stride=None):
    stride = stride or sub
    return jnp.concatenate(
        [ref[pl.ds(first + c, rows, stride=stride), :] for c in range(sub)], axis=1)


def _inproj_kernel(x_ref, w_ref, h_ref, *moba_refs):
    h = jnp.dot(x_ref[...].astype(BF16), w_ref[...], preferred_element_type=F32)
    h_ref[...] = h
    if moba_refs:
        kv_ref, q6_ref = moba_refs
        n_pairs = q6_ref.shape[0]
        for p in range(n_pairs):
            q6_ref[p] = h[:, p * LANES:(p + 1) * LANES] * (MOBA_HEAD_DIM ** -0.5)
            for part in range(2):
                col = (1 + part) * MIX_WIDTH + p * LANES
                kv_ref[:, (2 * p + part) * LANES:(2 * p + part + 1) * LANES] = (
                    h[:, col:col + LANES].astype(BF16))


def _inproj(x2, w_bf16, for_moba):
    T, D = x2.shape
    N = w_bf16.shape[1]
    tm = INPROJ_ROWS
    out_shape = [jax.ShapeDtypeStruct((T, N), F32)]
    out_specs = [pl.BlockSpec((tm, N), lambda i: (i, 0))]
    if for_moba:
        NP = MOBA_HEADS // 2
        out_shape += [jax.ShapeDtypeStruct((T, 2 * MIX_WIDTH), BF16),
                      jax.ShapeDtypeStruct((NP, T, LANES), F32)]
        out_specs += [pl.BlockSpec((tm, 2 * MIX_WIDTH), lambda i: (i, 0)),
                      pl.BlockSpec((NP, tm, LANES), lambda i: (0, i, 0))]
    return pl.pallas_call(
        _inproj_kernel,
        grid=(T // tm,),
        in_specs=[pl.BlockSpec((tm, D), lambda i: (i, 0)),
                  pl.BlockSpec((D, N), lambda i: (0, 0))],
        out_specs=out_specs,
        out_shape=out_shape,
        compiler_params=_params("parallel"),
        name="inproj",
    )(x2, w_bf16)


def _cumsum_rows(x, row):
    n = x.shape[0]
    sh = 1
    while sh < n:
        x = x + jnp.where(row >= sh, pltpu.roll(x, sh, 0), 0.0)
        sh *= 2
    return x


def _bcast_row(a, group, r):
    n = a.shape[0]
    a3 = a.reshape(n // group, group, LANES)
    return jnp.broadcast_to(a3[:, r:r + 1, :], a3.shape).reshape(n, LANES)


def _hgrn_chunk(qr, fr, v, gr, lb, ng, e_sum, st_t):
    C = qr.shape[0]
    row = lax.broadcasted_iota(jnp.int32, (C, LANES), 0)
    rr = lax.broadcasted_iota(jnp.int32, (C, C), 0)
    cc = lax.broadcasted_iota(jnp.int32, (C, C), 1)

    q = qr * _sigmoid(qr)
    forget = lb + (1.0 - lb) * _sigmoid(fr)
    k = 1.0 - forget
    G = _cumsum_rows(jnp.log(forget), row)

    z = jnp.log(k) - G
    parts = []
    for s in range(SUBLANES):
        parts.append((q * jnp.exp(jnp.minimum(G + _bcast_row(z, SUBLANES, s), 0.0))).astype(BF16))
    a_diag = jnp.dot(jnp.concatenate(parts, axis=1), e_sum, preferred_element_type=F32)
    A = jnp.where(((rr >> 3) == (cc >> 3)) & (cc <= rr), a_diag, 0.0)

    m = SUBLANES
    while m < C:
        lg = int(math.log2(m))
        Gr = _bcast_row(G, 2 * m, m - 1)
        second = ((row >> lg) & 1) == 1
        qm = q * jnp.exp(jnp.where(second, G - Gr, -jnp.inf))
        km = k * jnp.exp(jnp.where(second, -jnp.inf, Gr - G))
        am = lax.dot_general(qm.astype(BF16), km.astype(BF16), _NT, preferred_element_type=F32)
        A = A + jnp.where((rr >> (lg + 1)) == (cc >> (lg + 1)), am, 0.0)
        m *= 2

    vb = v.astype(BF16)
    o = jnp.dot(A.astype(BF16), vb, preferred_element_type=F32)
    o = o + lax.dot_general((q * jnp.exp(G)).astype(BF16), st_t.astype(BF16), _NT,
                            preferred_element_type=F32)
    g_end = G[C - 1:C, :]
    kd = (k * jnp.exp(g_end - G)).astype(BF16)
    st_new = st_t * jnp.exp(g_end) + lax.dot_general(vb, kd, _TN, preferred_element_type=F32)

    ms = jnp.mean(o * o, axis=-1, keepdims=True)
    out = o * lax.rsqrt(ms + RMS_EPS) * ng * _sigmoid(gr)
    return out, st_new


def _hgrn_kernel(q_ref, f_ref, i_ref, g_ref, lb_ref, ng_ref, e_ref, o_ref, st_ref, *, chunk):
    @pl.when(pl.program_id(2) == 0)
    def _():
        st_ref[...] = jnp.zeros_like(st_ref)

    lb = lb_ref[0]
    ng = ng_ref[...]
    e_sum = e_ref[...]
    n_chunks = q_ref.shape[1] // chunk
    for c in range(n_chunks):
        sl = pl.ds(c * chunk, chunk)
        out, st_new = _hgrn_chunk(q_ref[0, sl, :], f_ref[0, sl, :], i_ref[0, sl, :],
                                  g_ref[0, sl, :], lb, ng, e_sum, st_ref[...])
        st_ref[...] = st_new
        o_ref[0, sl, :] = out


def _hgrn_mixer(h3, lb, norm_g):
    B, S, _ = h3.shape
    ts = min(HGRN_ROWS, S)
    C = HGRN_CHUNK
    H = HGRN_HEADS
    e_sum = (jnp.arange(SUBLANES * LANES)[:, None] // LANES == jnp.arange(C)[None, :] % SUBLANES
             ).astype(BF16)
    col = lambda off: pl.BlockSpec((1, ts, LANES), lambda b, h, s, off=off: (b, s, off + h))
    return pl.pallas_call(
        functools.partial(_hgrn_kernel, chunk=C),
        grid=(B, H, S // ts),
        in_specs=[col(0), col(H), col(2 * H), col(3 * H),
                  pl.BlockSpec((1, 1, LANES), lambda b, h, s: (h, 0, 0)),
                  pl.BlockSpec((1, LANES), lambda b, h, s: (0, 0)),
                  pl.BlockSpec((SUBLANES * LANES, C), lambda b, h, s: (0, 0))],
        out_specs=pl.BlockSpec((1, ts, LANES), lambda b, h, s: (b, s, h)),
        out_shape=jax.ShapeDtypeStruct((B, S, MIX_WIDTH), F32),
        scratch_shapes=[pltpu.VMEM((HGRN_DK, HGRN_DK), F32)],
        compiler_params=_params("parallel", "parallel", "arbitrary"),
        name="hgrn",
    )(h3, h3, h3, h3, lb.reshape(H, 1, LANES), norm_g.reshape(1, LANES), e_sum)


def _moba_pick_kernel(q_ref, kf_ref, sr_ref, cnt_ref, kmean_ref, *, qblocks):
    BLK = MOBA_BLOCK
    W = qblocks * BLK
    nbp = kmean_ref.shape[0]
    i0 = pl.program_id(2) * qblocks

    @pl.when(pl.program_id(2) == 0)
    def _():
        kmean_ref[...] = jnp.zeros_like(kmean_ref)

    for j in range(qblocks):
        kmean_ref[pl.ds(i0 + j, 1), :] = jnp.mean(kf_ref[0, j * BLK:(j + 1) * BLK, :], axis=0,
                                                  keepdims=True)
    km = kmean_ref[...]
    lane_k = lax.broadcasted_iota(jnp.int32, (nbp, LANES), 1)
    km2 = jnp.concatenate([jnp.where(lane_k < MOBA_HEAD_DIM, km, 0.0),
                           jnp.where(lane_k >= MOBA_HEAD_DIM, km, 0.0)], axis=0)
    gate = lax.dot_general(km2, q_ref[0], _NT, precision=lax.Precision.HIGHEST,
                           preferred_element_type=F32).reshape(2, nbp, W)
    nblk = lax.broadcasted_iota(jnp.int32, (2, nbp, W), 1)
    qi = i0 + (lax.broadcasted_iota(jnp.int32, (2, 1, W), 2) >> int(math.log2(BLK)))
    g = jnp.where(nblk < qi, gate, -jnp.inf)
    picks = []
    for _ in range(MOBA_TOPK):
        mx = jnp.max(g, axis=1, keepdims=True)
        idx = jnp.min(jnp.where(g == mx, nblk, nbp), axis=1, keepdims=True)
        picks.append((idx, (mx > -jnp.inf) & (idx < qi)))
        g = jnp.where(nblk == idx, -jnp.inf, g)
    chosen = jnp.zeros((2, nbp, W), F32)
    for idx, valid in picks:
        chosen = chosen + jnp.where((nblk == idx) & valid, 1.0, 0.0)
    qa = lax.broadcasted_iota(jnp.int32, (BLK, BLK), 0)
    qc = lax.broadcasted_iota(jnp.int32, (BLK, BLK), 1)
    before = jnp.where(qa < qc, 1.0, 0.0).astype(BF16)
    chosen2 = chosen.reshape(2 * nbp, W).astype(BF16)
    earlier = jnp.concatenate(
        [jnp.dot(chosen2[:, j * BLK:(j + 1) * BLK], before, preferred_element_type=F32)
         for j in range(qblocks)], axis=1).reshape(2, nbp, W)
    orow = lax.broadcasted_iota(jnp.int32, (2 * SUBLANES, W), 0)
    out = jnp.zeros((2 * SUBLANES, W), jnp.int32)
    for j, (idx, valid) in enumerate(picks):
        rank = jnp.sum(jnp.where(nblk == idx, earlier, 0.0), axis=1, keepdims=True).astype(jnp.int32)
        sel = jnp.where(valid, idx, -1)
        for hh in range(2):
            rep = hh * MOBA_TOPK + j
            out = jnp.where(orow == rep, sel[hh], out)
            out = jnp.where(orow == SUBLANES + rep, rank[hh], out)
    sr_ref[0, 0] = out
    lane_c = lax.broadcasted_iota(jnp.int32, (nbp, LANES), 1)
    for j in range(qblocks):
        per_block = jnp.sum(chosen[:, :, j * BLK:(j + 1) * BLK], axis=2, keepdims=True)
        cnt_ref[0, 0, j * nbp:(j + 1) * nbp, :] = jnp.where(
            lane_c == 0, per_block[0], jnp.where(lane_c == 1, per_block[1], 0.0))


def _moba_place_kernel(sr_ref, tab_ref, idx_ref, *, group, spare):
    BLK = MOBA_BLOCK
    n_rep = 2 * MOBA_TOPK
    b = pl.program_id(0)
    p = pl.program_id(1)
    ig = pl.program_id(2)
    nrow = lax.broadcasted_iota(jnp.int32, (LANES, BLK), 0)
    orow = lax.broadcasted_iota(jnp.int32, (SC_IDX_ROWS, BLK), 0)
    qpos = lax.broadcasted_iota(jnp.int32, (1, BLK), 1)
    for g in range(group):
        blk = sr_ref[0, 0, :, g * BLK:(g + 1) * BLK]
        tab_t = tab_ref[0, 0, g * SUBLANES:(g + 1) * SUBLANES, :].T
        early = jnp.minimum(ig * group + g, MOBA_TOPK - 1)
        out = jnp.zeros((SC_IDX_ROWS, BLK), jnp.int32)
        for rep in range(n_rep):
            hh = rep // MOBA_TOPK
            sel = blk[rep:rep + 1, :]
            rank = blk[SUBLANES + rep:SUBLANES + rep + 1, :]
            start = jnp.sum(jnp.where(nrow == sel, tab_t[:, hh:hh + 1], 0.0), axis=0, keepdims=True)
            unused = spare + (((b * pl.num_programs(1) + p) * MOBA_TOPK + early) * n_rep + rep) * BLK
            dest = jnp.where(sel >= 0, start.astype(jnp.int32) + rank, unused + qpos)
            out = jnp.where(orow == rep, dest, out)
        idx_ref[:, g * BLK:(g + 1) * BLK] = out


def _moba_tiles_kernel(tb_ref, tp_ref, th_ref, tn_ref, nu_ref, sl_ref, q_ref, *refs, group):
    kv_refs, o_ref = refs[:group], refs[group]
    t = pl.program_id(0)
    tq = q_ref.shape[0] // group

    @pl.when(t * group < nu_ref[0])
    def _():
        lane = lax.broadcasted_iota(jnp.int32, (tq, LANES), 1)
        kpos = lax.broadcasted_iota(jnp.int32, (1, MOBA_BLOCK), 1)
        for g in range(group):
            tt = t * group + g
            hh = th_ref[tt]
            slope = sl_ref[2 * tp_ref[tt] + hh]
            head = (lane >> 6) == hh
            rows = pl.ds(g * tq, tq)
            q = jnp.where(head, q_ref[rows, :], 0.0).astype(BF16)
            s = lax.dot_general(q, kv_refs[g][0, :, :LANES], _NT, preferred_element_type=F32)
            s = s + slope * (kpos + tn_ref[tt] * MOBA_BLOCK).astype(F32)
            m = jnp.max(s, axis=1, keepdims=True)
            pr = jnp.exp(s - m)
            l = jnp.sum(pr, axis=1, keepdims=True)
            o = jnp.dot(pr.astype(BF16), kv_refs[g][0, :, LANES:], preferred_element_type=F32) / l
            o_ref[rows, :] = jnp.where(head, o, m + jnp.log(l))


def _moba_own_kernel(sl_ref, q_ref, kv_ref, o_ref, lse_ref):
    BLK = MOBA_BLOCK
    p = pl.program_id(1)
    n_blocks = q_ref.shape[1] // BLK
    lane = lax.broadcasted_iota(jnp.int32, (BLK, LANES), 1)
    rr = lax.broadcasted_iota(jnp.int32, (BLK, BLK), 0)
    cc = lax.broadcasted_iota(jnp.int32, (BLK, BLK), 1)
    first = lane < MOBA_HEAD_DIM
    for h in range(n_blocks):
        i = pl.program_id(2) * n_blocks + h
        rows = pl.ds(h * BLK, BLK)
        qf = q_ref[0, rows, :]
        k_own = kv_ref[0, rows, :LANES]
        v_own = kv_ref[0, rows, LANES:]
        key_pos = (lax.broadcasted_iota(jnp.int32, (1, BLK), 1) + i * BLK).astype(F32)
        outs, lses = [], []
        for hh in range(2):
            head = (lane >> 6) == hh
            slope = sl_ref[2 * p + hh]
            qh = jnp.where(head, qf * (MOBA_HEAD_DIM ** -0.5), 0.0).astype(BF16)
            s = lax.dot_general(qh, k_own, _NT, preferred_element_type=F32)
            s = jnp.where(cc <= rr, s + slope * key_pos, -jnp.inf)
            m = jnp.max(s, axis=1, keepdims=True)
            pr = jnp.exp(s - m)
            l = jnp.sum(pr, axis=1, keepdims=True)
            lses.append(m + jnp.log(l))
            outs.append(jnp.dot(pr.astype(BF16), v_own, preferred_element_type=F32) / l)
        o_ref[0, rows, :] = jnp.where(first, outs[0], outs[1])
        lse_ref[0, rows, :] = jnp.where(first, lses[0], lses[1])


def _moba_merge_kernel(o_ref, lse_ref, pg_ref, out_ref):
    BLK = MOBA_BLOCK
    first = lax.broadcasted_iota(jnp.int32, (BLK, LANES), 1) < MOBA_HEAD_DIM
    for h in range(o_ref.shape[1] // BLK):
        i = pl.program_id(2) * (o_ref.shape[1] // BLK) + h
        rows = pl.ds(h * BLK, BLK)
        lses = [lse_ref[0, rows, :]]
        vals = [o_ref[0, rows, :]]
        for j in range(MOBA_TOPK):
            p0 = pg_ref[j, 0, rows, :]
            p1 = pg_ref[MOBA_TOPK + j, 0, rows, :]
            has_block = j < i
            stats = pltpu.roll(jnp.where(first, p1, p0), MOBA_HEAD_DIM, 1)
            lses.append(jnp.where(has_block, stats, -jnp.inf))
            vals.append(jnp.where(has_block, jnp.where(first, p0, p1), 0.0))
        top = functools.reduce(jnp.maximum, lses)
        ws = [jnp.exp(x - top) for x in lses]
        out_ref[0, rows, :] = sum(w * v for w, v in zip(ws, vals)) / sum(ws)


def _moba_mixer(h3, kv3, q6):
    B, S, _ = h3.shape
    T = B * S
    BLK = MOBA_BLOCK
    TQ = MOBA_TILE
    G = MOBA_TILE_GROUP
    NB = S // BLK
    GI = math.gcd(NB, MOBA_PLACE_GROUP)
    nbp = -(-NB // SUBLANES) * SUBLANES
    NP = MOBA_HEADS // 2
    n_rep = 2 * MOBA_TOPK
    slopes = jnp.asarray(_alibi_slope_list(MOBA_HEADS), F32)

    QB = math.gcd(NB, MOBA_PICK_GROUP)
    selrank, counts = pl.pallas_call(
        functools.partial(_moba_pick_kernel, qblocks=QB),
        grid=(B, NP, NB // QB),
        in_specs=[pl.BlockSpec((1, QB * BLK, LANES), lambda b, p, i: (b, i, p)),
                  pl.BlockSpec((1, QB * BLK, LANES), lambda b, p, i: (b, i, NP + p))],
        out_specs=[pl.BlockSpec((1, 1, 2 * SUBLANES, QB * BLK), lambda b, p, i: (b, p, 0, i)),
                   pl.BlockSpec((1, 1, QB * nbp, LANES), lambda b, p, i: (b, p, i, 0))],
        out_shape=[jax.ShapeDtypeStruct((B, NP, 2 * SUBLANES, S), jnp.int32),
                   jax.ShapeDtypeStruct((B, NP, NB * nbp, LANES), F32)],
        scratch_shapes=[pltpu.VMEM((nbp, LANES), F32)],
        compiler_params=_params("parallel", "parallel", "arbitrary"),
        name="moba_pick",
    )(h3, h3)

    OB = math.gcd(NB, MOBA_OWN_GROUP)
    blk_spec = lambda w: pl.BlockSpec((1, OB * BLK, w), lambda b, p, i, sl: (b, i, p))
    own_o, own_lse = pl.pallas_call(
        _moba_own_kernel,
        grid_spec=pltpu.PrefetchScalarGridSpec(
            num_scalar_prefetch=1,
            grid=(B, NP, NB // OB),
            in_specs=[blk_spec(LANES), blk_spec(2 * LANES)],
            out_specs=[blk_spec(LANES), blk_spec(LANES)]),
        out_shape=[jax.ShapeDtypeStruct((B, S, MIX_WIDTH), F32)] * 2,
        compiler_params=_params("parallel", "parallel", "parallel"),
        name="moba_own",
    )(slopes, h3, kv3)

    cnt = counts.reshape(B, NP, NB, nbp, LANES)[:, :, :, :NB, :2].astype(jnp.int32)
    cnt = cnt.transpose(0, 1, 2, 4, 3)
    base = jnp.cumsum(cnt, axis=2) - cnt
    total = jnp.sum(cnt, axis=2)
    padded = (total + TQ - 1) // TQ * TQ
    pend = jnp.cumsum(padded.reshape(-1))
    seg_start = (pend - padded.reshape(-1)).reshape(B, NP, 1, 2, NB)
    table = jnp.zeros((B, NP, NB, SUBLANES, LANES), F32).at[:, :, :, :2, :NB].set(
        (seg_start + base).astype(F32)).reshape(B, NP, NB * SUBLANES, LANES)
    n_seg = B * NP * 2 * NB
    max_tiles = -(-((T * NP * n_rep) // TQ + n_seg) // G) * G
    first_row = jnp.arange(max_tiles, dtype=jnp.int32) * TQ
    tile_seg = jnp.minimum(jnp.sum(pend[None, :] <= first_row[:, None], axis=1), n_seg - 1)
    tile_n = (tile_seg % NB).astype(jnp.int32)
    tile_h = ((tile_seg // NB) % 2).astype(jnp.int32)
    tile_p = ((tile_seg // (2 * NB)) % NP).astype(jnp.int32)
    tile_b = (tile_seg // (2 * NB * NP)).astype(jnp.int32)
    n_used = (pend[-1] // TQ).astype(jnp.int32).reshape(1)
    spare = max_tiles * TQ
    n_rows = spare + max(B * NP * MOBA_TOPK * n_rep * BLK, G * TQ)

    idx = pl.pallas_call(
        functools.partial(_moba_place_kernel, group=GI, spare=spare),
        grid=(B, NP, NB // GI),
        in_specs=[pl.BlockSpec((1, 1, 2 * SUBLANES, GI * BLK), lambda b, p, i: (b, p, 0, i)),
                  pl.BlockSpec((1, 1, GI * SUBLANES, LANES), lambda b, p, i: (b, p, i, 0))],
        out_specs=pl.BlockSpec((SC_IDX_ROWS, GI * BLK),
                               lambda b, p, i: (0, p * (T // (GI * BLK)) + b * (NB // GI) + i)),
        out_shape=jax.ShapeDtypeStruct((SC_IDX_ROWS, NP * T), jnp.int32),
        compiler_params=_params("parallel", "parallel", "parallel"),
        name="moba_place",
    )(selrank, table)

    qs = _sc_scatter_rows(q6.reshape(NP * T, LANES), idx, n_rep, n_rows)

    live = lambda t, tb, tp, th, tn, nu, sl: (jnp.where(t * G < nu[0], t, max_tiles // G), 0)
    kv_blk = lambda g: pl.BlockSpec(
        (1, BLK, 2 * LANES),
        lambda t, tb, tp, th, tn, nu, sl: (tb[t * G + g], tn[t * G + g], tp[t * G + g]))
    part = pl.pallas_call(
        functools.partial(_moba_tiles_kernel, group=G),
        grid_spec=pltpu.PrefetchScalarGridSpec(
            num_scalar_prefetch=6,
            grid=(max_tiles // G,),
            in_specs=[pl.BlockSpec((G * TQ, LANES), live)] + [kv_blk(g) for g in range(G)],
            out_specs=pl.BlockSpec((G * TQ, LANES), live)),
        out_shape=jax.ShapeDtypeStruct((n_rows, LANES), F32),
        compiler_params=_params("arbitrary"),
        name="moba_tiles",
    )(tile_b, tile_p, tile_h, tile_n, n_used, slopes, qs, *([kv3] * G))

    pg = _sc_gather_rows(part, idx[:n_rep].reshape(-1)).reshape(n_rep, NP, T, LANES)

    MB = math.gcd(NB, MOBA_MERGE_GROUP)
    blk = pl.BlockSpec((1, MB * BLK, LANES), lambda b, p, i: (b, i, p))
    return pl.pallas_call(
        _moba_merge_kernel,
        grid=(B, NP, NB // MB),
        in_specs=[blk, blk,
                  pl.BlockSpec((n_rep, 1, MB * BLK, LANES),
                               lambda b, p, i: (0, p, b * (NB // MB) + i, 0))],
        out_specs=blk,
        out_shape=jax.ShapeDtypeStruct((B, S, MIX_WIDTH), F32),
        compiler_params=_params("parallel", "parallel", "parallel"),
        name="moba_merge",
    )(own_o, own_lse, pg)


def _memkv_kernel(mem_ref, w_ref, kv_ref):
    kv_ref[0] = jnp.dot(mem_ref[0].astype(BF16), w_ref[...].astype(BF16),
                        preferred_element_type=F32).astype(BF16)


def _memkv(mem, w_kv):
    B, M, D = mem.shape
    N = w_kv.shape[1]
    return pl.pallas_call(
        _memkv_kernel,
        grid=(B,),
        in_specs=[pl.BlockSpec((1, M, D), lambda b: (b, 0, 0)),
                  pl.BlockSpec((D, N), lambda b: (0, 0))],
        out_specs=pl.BlockSpec((1, M, N), lambda b: (b, 0, 0)),
        out_shape=jax.ShapeDtypeStruct((B, M, N), BF16),
        compiler_params=_params("parallel"),
        name="memkv",
    )(mem, w_kv)


def _layer_norm(z, g, b):
    mu = jnp.mean(z, axis=-1, keepdims=True)
    zc = z - mu
    var = jnp.mean(zc * zc, axis=-1, keepdims=True)
    return zc * lax.rsqrt(var + LN_EPS) * g + b


def _post_kernel(x_ref, mix_ref, mq_ref, kv_ref, wo_ref, g_ref, b_ref, wr_ref, br_ref,
                 x1_ref, x1s_ref, idx_ref, gate_ref, rank_ref, cnt_ref, run_ref, *, alpha):
    tm = x_ref.shape[0]
    mq = mq_ref[...]
    kv = kv_ref[0]
    km = kv[:, :MEM_WIDTH]
    vm = kv[:, MEM_WIDTH:]
    lane = lax.broadcasted_iota(jnp.int32, (tm, MEM_WIDTH), 1)
    scale = MEM_HEAD_DIM ** -0.5
    mo = jnp.zeros((tm, MEM_WIDTH), F32)
    for hd in range(MEM_HEADS):
        head = (lane >> 6) == hd
        qh = jnp.where(head, mq * scale, 0.0).astype(BF16)
        s = lax.dot_general(qh, km, _NT, preferred_element_type=F32)
        m = jnp.max(s, axis=1, keepdims=True)
        p = jnp.exp(s - m)
        l = jnp.sum(p, axis=1, keepdims=True)
        oh = jnp.dot(p.astype(BF16), vm, preferred_element_type=F32) / l
        mo = jnp.where(head, oh, mo)

    y = jnp.dot(mix_ref[...].astype(BF16), wo_ref[:MIX_WIDTH, :], preferred_element_type=F32)
    y = y + jnp.dot(mo.astype(BF16), wo_ref[MIX_WIDTH:, :], preferred_element_type=F32)
    x1 = _layer_norm(alpha * x_ref[...] + y, g_ref[...], b_ref[...])
    x1_ref[...] = x1
    _store_subrows(x1s_ref, x1)

    x_hi = x1.astype(BF16)
    x_lo = (x1 - x_hi.astype(F32)).astype(BF16)
    hi = jnp.dot(x_hi, wr_ref[...], preferred_element_type=F32)
    lo = jnp.dot(x_lo, wr_ref[:, :LANES], preferred_element_type=F32)
    logits = hi[:, :LANES] + hi[:, LANES:] + lo
    g = logits.T[:N_EXPERTS] + br_ref[...]
    erow = lax.broadcasted_iota(jnp.int32, (N_EXPERTS, tm), 0)
    orow = lax.broadcasted_iota(jnp.int32, (SUBLANES, tm), 0)
    idx_out = jnp.zeros((SUBLANES, tm), jnp.int32)
    vals, picks = [], []
    chosen = jnp.zeros((N_EXPERTS, tm), F32)
    for kk in range(TOP_K):
        mx = jnp.max(g, axis=0, keepdims=True)
        idx = jnp.min(jnp.where(g == mx, erow, N_EXPERTS), axis=0, keepdims=True)
        idx_out = jnp.where(orow == kk, idx, idx_out)
        vals.append(mx)
        picks.append(idx)
        chosen = chosen + jnp.where(erow == idx, 1.0, 0.0)
        g = jnp.where(erow == idx, -jnp.inf, g)
    evs = [jnp.exp(v - vals[0]) for v in vals]
    den = sum(evs)
    gate_out = jnp.zeros((SUBLANES, tm), F32)
    for kk in range(TOP_K):
        gate_out = jnp.where(orow == kk, evs[kk] / den, gate_out)
    idx_ref[...] = idx_out
    gate_ref[...] = gate_out

    @pl.when(pl.program_id(0) == 0)
    def _():
        run_ref[...] = jnp.zeros_like(run_ref)

    ta = lax.broadcasted_iota(jnp.int32, (tm, tm), 0)
    tb = lax.broadcasted_iota(jnp.int32, (tm, tm), 1)
    before = jnp.where(ta < tb, 1.0, 0.0).astype(BF16)
    earlier = jnp.dot(chosen.astype(BF16), before, preferred_element_type=F32) + run_ref[:, :1]
    rank_out = jnp.zeros((SUBLANES, tm), jnp.int32)
    for kk in range(TOP_K):
        rank = jnp.sum(jnp.where(erow == picks[kk], earlier, 0.0), axis=0, keepdims=True)
        rank_out = jnp.where(orow == kk, rank.astype(jnp.int32), rank_out)
    rank_ref[...] = rank_out
    run_ref[...] = run_ref[...] + jnp.sum(chosen, axis=1, keepdims=True)
    cnt_ref[...] = run_ref[...]


def _post_mixer(x2, mix2, h2, kv, w_o_bf16, ln_g, ln_b, w_router, b_router, seq_len, alpha):
    T, D = x2.shape
    tm = POST_ROWS
    N = h2.shape[1]
    M = kv.shape[1]
    mq_col = (N - MEM_WIDTH) // MEM_WIDTH
    tiles_per_seq = seq_len // tm
    wr = jnp.zeros((D, LANES), F32).at[:, :N_EXPERTS].set(w_router)
    wr_hi = wr.astype(BF16)
    wr = jnp.concatenate([wr_hi, (wr - wr_hi.astype(F32)).astype(BF16)], axis=1)
    br = b_router.reshape(N_EXPERTS, 1)
    row = lambda n: pl.BlockSpec((tm, n), lambda i: (i, 0))
    full = lambda a, b: pl.BlockSpec((a, b), lambda i: (0, 0))
    per_token = pl.BlockSpec((SUBLANES, tm), lambda i: (0, i))
    return pl.pallas_call(
        functools.partial(_post_kernel, alpha=alpha),
        grid=(T // tm,),
        in_specs=[row(D), row(MIX_WIDTH),
                  pl.BlockSpec((tm, MEM_WIDTH), lambda i: (i, mq_col)),
                  pl.BlockSpec((1, M, 2 * MEM_WIDTH), lambda i: (i // tiles_per_seq, 0, 0)),
                  full(D, D), full(1, D), full(1, D),
                  full(D, 2 * LANES), full(N_EXPERTS, 1)],
        out_specs=[row(D), pl.BlockSpec((tm * (D // LANES), LANES), lambda i: (i, 0)),
                   per_token, per_token, per_token, full(N_EXPERTS, LANES)],
        out_shape=[jax.ShapeDtypeStruct((T, D), F32),
                   jax.ShapeDtypeStruct((T * (D // LANES), LANES), F32),
                   jax.ShapeDtypeStruct((SUBLANES, T), jnp.int32),
                   jax.ShapeDtypeStruct((SUBLANES, T), F32),
                   jax.ShapeDtypeStruct((SUBLANES, T), jnp.int32),
                   jax.ShapeDtypeStruct((N_EXPERTS, LANES), F32)],
        scratch_shapes=[pltpu.VMEM((N_EXPERTS, LANES), F32)],
        compiler_params=_params("arbitrary"),
        name="post_mixer",
    )(x2, mix2, h2, kv, w_o_bf16, ln_g.reshape(1, D), ln_b.reshape(1, D), wr, br)


def _sc_mesh():
    return plsc.VectorSubcoreMesh(core_axis_name="core", subcore_axis_name="subcore")


def _sc_scatter_rows(rows, idx, n_rep, n_out):
    R, W = rows.shape

    @functools.partial(pl.kernel, out_type=jax.ShapeDtypeStruct((n_out, W), rows.dtype),
                       mesh=_sc_mesh(), scratch_types=[])
    def scatter(x_hbm, i_hbm, o_hbm):
        def body(x_vmem, i_vmem):
            for r in range(n_rep):
                pltpu.sync_copy(x_vmem, o_hbm.at[i_vmem.at[r]])

        pltpu.emit_pipeline(
            body, grid=(R // SC_WINDOW,),
            in_specs=[pl.BlockSpec((SC_WINDOW, W), lambda i: (i, 0)),
                      pl.BlockSpec((SC_IDX_ROWS, SC_WINDOW), lambda i: (0, i))],
            out_specs=[], core_axis_name=("core", "subcore"),
            dimension_semantics=(pltpu.PARALLEL,), trace_scopes=False)(x_hbm, i_hbm)

    return scatter(rows, idx)


def _sc_gather_rows(table, idx):
    n = idx.shape[0]
    W = table.shape[1]

    @functools.partial(pl.kernel, out_type=jax.ShapeDtypeStruct((n, W), table.dtype),
                       mesh=_sc_mesh(), scratch_types=[])
    def gather(t_hbm, i_hbm, o_hbm):
        def body(i_vmem, o_vmem):
            pltpu.sync_copy(t_hbm.at[i_vmem.at[0]], o_vmem)

        pltpu.emit_pipeline(
            body, grid=(n // SC_WINDOW,),
            in_specs=[pl.BlockSpec((1, SC_WINDOW), lambda i: (0, i))],
            out_specs=[pl.BlockSpec((SC_WINDOW, W), lambda i: (i, 0))],
            core_axis_name=("core", "subcore"),
            dimension_semantics=(pltpu.PARALLEL,), trace_scopes=False)(i_hbm, o_hbm)

    return gather(table, idx.reshape(1, n))


def _sc_workers():
    info = pltpu.get_tpu_info().sparse_core
    return info.num_cores, info.num_cores * info.num_subcores


def _sc_scatter_slabs(rows, idx, n_rep, n_out):
    R, S, W = rows.shape
    n_cores, n_workers = _sc_workers()
    per_worker = (R // SC_WINDOW) // n_workers
    assert per_worker * n_workers * SC_WINDOW == R

    @functools.partial(pl.kernel, out_type=jax.ShapeDtypeStruct((n_out, S, W), rows.dtype),
                       mesh=_sc_mesh(),
                       scratch_types=[pltpu.VMEM((SC_IDX_ROWS, SC_WINDOW), jnp.int32),
                                      pltpu.VMEM((SC_CHUNK, S, W), rows.dtype)])
    def scatter(x_hbm, i_hbm, o_hbm, ibuf, buf):
        wid = lax.axis_index("subcore") * n_cores + lax.axis_index("core")

        @pl.loop(0, per_worker)
        def _(s):
            first = (wid * per_worker + s) * SC_WINDOW
            pltpu.sync_copy(i_hbm.at[:, pl.ds(first, SC_WINDOW)], ibuf)
            for c in range(SC_WINDOW // SC_CHUNK):
                pltpu.sync_copy(x_hbm.at[pl.ds(first + c * SC_CHUNK, SC_CHUNK)], buf)
                for r in range(n_rep):
                    pltpu.sync_copy(buf, o_hbm.at[ibuf.at[r, pl.ds(c * SC_CHUNK, SC_CHUNK)]])

    return scatter(rows, idx)


def _sc_gather_slabs(table, idx):
    n = idx.shape[0]
    S, W = table.shape[1:]
    n_cores, n_workers = _sc_workers()
    per_worker = (n // SC_WINDOW) // n_workers
    assert per_worker * n_workers * SC_WINDOW == n
    n_chunks = SC_WINDOW // SC_CHUNK

    @functools.partial(pl.kernel, out_type=jax.ShapeDtypeStruct((n, S, W), table.dtype),
                       mesh=_sc_mesh(),
                       scratch_types=[pltpu.VMEM((1, SC_WINDOW), jnp.int32),
                                      pltpu.VMEM((2, SC_CHUNK, S, W), table.dtype),
                                      pltpu.SemaphoreType.DMA((2,)), pltpu.SemaphoreType.DMA((2,))])
    def gather(t_hbm, i_hbm, o_hbm, ibuf, buf, fetch_sem, store_sem):
        wid = lax.axis_index("subcore") * n_cores + lax.axis_index("core")

        @pl.loop(0, per_worker)
        def _(s):
            blk = wid * per_worker + s
            pltpu.sync_copy(i_hbm.at[pl.ds(blk, 1)], ibuf)

            def fetch(c):
                return pltpu.make_async_copy(
                    t_hbm.at[ibuf.at[0, pl.ds(c * SC_CHUNK, SC_CHUNK)]], buf.at[c % 2],
                    fetch_sem.at[c % 2])

            def store(c):
                return pltpu.make_async_copy(
                    buf.at[c % 2], o_hbm.at[pl.ds(blk * SC_WINDOW + c * SC_CHUNK, SC_CHUNK)],
                    store_sem.at[c % 2])

            fetch(0).start()
            for c in range(n_chunks):
                if c + 1 < n_chunks:
                    if c >= 1:
                        store(c - 1).wait()
                    fetch(c + 1).start()
                fetch(c).wait()
                store(c).start()
            store(n_chunks - 2).wait()
            store(n_chunks - 1).wait()

    return gather(table, idx.reshape(n // SC_WINDOW, SC_WINDOW))


def _route(top_idx, rank, counts):
    rb = MOE_ROWS
    n_tokens = top_idx.shape[1]
    tk = n_tokens * TOP_K
    padded = (counts + rb - 1) // rb * rb
    pend = jnp.cumsum(padded)
    pstart = pend - padded
    experts = jnp.arange(N_EXPERTS, dtype=jnp.int32)
    start = jnp.sum(jnp.where(top_idx[:TOP_K, :, None] == experts, pstart, 0), axis=2)
    dest = (start + rank[:TOP_K]).astype(jnp.int32)
    n_blocks = tk // rb + N_EXPERTS
    first_row = jnp.arange(n_blocks, dtype=jnp.int32) * rb
    block_e = jnp.minimum(jnp.sum(pend[None, :] <= first_row[:, None], axis=1),
                          N_EXPERTS - 1).astype(jnp.int32)
    n_used = (pend[-1] // rb).astype(jnp.int32).reshape(1)
    return dest, block_e, n_used


def _dispatch(x1s, dest, n_rows, sub):
    T = dest.shape[1]
    idx = jnp.concatenate([dest, jnp.zeros((SC_IDX_ROWS - TOP_K, T), jnp.int32)], axis=0)
    xs = _sc_scatter_slabs(x1s.reshape(T, sub, LANES), idx, TOP_K, n_rows)
    return xs.reshape(n_rows * sub, LANES)


def _expert_kernel(be_ref, nu_ref, x_ref, wg_ref, bg_ref, wu_ref, bu_ref, wd_ref, bd_ref,
                   y_ref, wgb, wub, wdb):
    i = pl.program_id(0)
    prev = be_ref[jnp.maximum(i - 1, 0)]

    @pl.when((i == 0) | (be_ref[i] != prev))
    def _():
        wgb[...] = wg_ref[0, 0].astype(BF16)
        wub[...] = wu_ref[0, 0].astype(BF16)
        wdb[...] = wd_ref[0, 0].astype(BF16)

    @pl.when(i < nu_ref[0])
    def _():
        sub = wgb.shape[0] // LANES
        xb = _load_subrows(x_ref, x_ref.shape[0] // sub, sub).astype(BF16)
        gate = jnp.dot(xb, wgb[...], preferred_element_type=F32) + bg_ref[0, 0]
        gate = jnp.minimum(gate, SWIGLU_LIMIT)
        up = jnp.dot(xb, wub[...], preferred_element_type=F32) + bu_ref[0, 0]
        up = jnp.clip(up, -SWIGLU_LIMIT, SWIGLU_LIMIT)
        hid = gate * _sigmoid(SWIGLU_ALPHA * gate) * (up + 1.0)
        y = jnp.dot(hid.astype(BF16), wdb[...], preferred_element_type=F32) + bd_ref[0, 0]
        _store_subrows(y_ref, y)


def _experts(xs, block_e, n_used, layer, w_gate, b_gate, w_up, b_up, w_down, b_down):
    rb = MOE_ROWS
    n_blocks = block_e.shape[0]
    E, D, F = w_gate.shape[1:]
    sub = D // LANES
    wspec = lambda a, b: pl.BlockSpec((1, 1, a, b), lambda i, be, nu: (layer, be[i], 0, 0))
    live = lambda i, be, nu: (jnp.where(i < nu[0], i, n_blocks), 0)
    grid_spec = pltpu.PrefetchScalarGridSpec(
        num_scalar_prefetch=2,
        grid=(n_blocks,),
        in_specs=[pl.BlockSpec((rb * sub, LANES), live),
                  wspec(D, F), wspec(1, F), wspec(D, F), wspec(1, F), wspec(F, D), wspec(1, D)],
        out_specs=pl.BlockSpec((rb * sub, LANES), live),
        scratch_shapes=[pltpu.VMEM((D, F), BF16), pltpu.VMEM((D, F), BF16),
                        pltpu.VMEM((F, D), BF16)],
    )
    depth = w_gate.shape[0]
    return pl.pallas_call(
        _expert_kernel,
        grid_spec=grid_spec,
        out_shape=jax.ShapeDtypeStruct(xs.shape, F32),
        compiler_params=_params("arbitrary"),
        name="experts",
    )(block_e, n_used, xs, w_gate, b_gate.reshape(depth, E, 1, F),
      w_up, b_up.reshape(depth, E, 1, F), w_down, b_down.reshape(depth, E, 1, D))


def _combine_kernel(x1_ref, gate_ref, y_ref, g_ref, b_ref, o_ref, *, alpha):
    tm, D = x1_ref.shape
    sub = D // LANES
    gates = gate_ref[...].T
    f = jnp.zeros(x1_ref.shape, F32)
    for kk in range(TOP_K):
        f = f + gates[:, kk:kk + 1] * _load_subrows(y_ref, tm, sub, kk * sub, TOP_K * sub)
    o_ref[...] = _layer_norm(alpha * x1_ref[...] + f, g_ref[...], b_ref[...])


def _combine(x1, gates, dest, y_rows, ln_g, ln_b, alpha):
    T, D = x1.shape
    sub = D // LANES
    tm = COMBINE_ROWS
    yg = _sc_gather_slabs(y_rows.reshape(-1, sub, LANES), dest.T.reshape(-1)).reshape(-1, LANES)
    row = lambda n: pl.BlockSpec((tm, n), lambda i: (i, 0))
    full = lambda a, b: pl.BlockSpec((a, b), lambda i: (0, 0))
    return pl.pallas_call(
        functools.partial(_combine_kernel, alpha=alpha),
        grid=(T // tm,),
        in_specs=[row(D), pl.BlockSpec((SUBLANES, tm), lambda i: (0, i)),
                  pl.BlockSpec((tm * TOP_K * sub, LANES), lambda i: (i, 0)),
                  full(1, D), full(1, D)],
        out_specs=row(D),
        out_shape=jax.ShapeDtypeStruct((T, D), F32),
        compiler_params=_params("parallel"),
        name="combine",
    )(x1, gates, yg, ln_g.reshape(1, D), ln_b.reshape(1, D))


def kernel(x, mem, w_in_hgrn, hgrn_lb_logits, hgrn_norm_g, w_in_moba, w_mem_kv, w_o,
           ln_mix_g, ln_mix_b, w_router, b_router, w_gate, b_gate, w_up, b_up,
           w_down, b_down, ln_ffn_g, ln_ffn_b):
    B, S, D = x.shape
    T = B * S
    depth = w_o.shape[0]
    alpha = (2 * depth) ** 0.25

    p_lb = jax.nn.softmax(hgrn_lb_logits.astype(F32), axis=0)
    lower_bounds = jnp.cumsum(p_lb, axis=0) - p_lb[0]

    x2 = x.reshape(T, D)
    for layer in range(depth):
        j = layer // 2
        if layer % 2 == 0:
            (h2,) = _inproj(x2, w_in_hgrn[j].astype(BF16), for_moba=False)
            mix = _hgrn_mixer(h2.reshape(B, S, -1), lower_bounds[j], hgrn_norm_g[j])
        else:
            h2, kv2, q6 = _inproj(x2, w_in_moba[j].astype(BF16), for_moba=True)
            mix = _moba_mixer(h2.reshape(B, S, -1), kv2.reshape(B, S, -1), q6)
        kv = _memkv(mem, w_mem_kv[layer])
        x1, x1s, top_idx, gates, rank, counts = _post_mixer(
            x2, mix.reshape(T, MIX_WIDTH), h2, kv, w_o[layer].astype(BF16),
            ln_mix_g[layer], ln_mix_b[layer], w_router[layer], b_router[layer], S, alpha)
        dest, block_e, n_used = _route(top_idx, rank, counts[:, 0].astype(jnp.int32))
        xs = _dispatch(x1s, dest, (block_e.shape[0] + 1) * MOE_ROWS, D // LANES)
        y_rows = _experts(xs, block_e, n_used, layer, w_gate, b_gate, w_up, b_up, w_down, b_down)
        x2 = _combine(x1, gates, dest, y_rows, ln_ffn_g[layer], ln_ffn_b[layer], alpha)
    return x2.reshape(B, S, D)
```

```python
import functools
import math

import jax
import jax.numpy as jnp
from jax import lax
from jax.experimental import pallas as pl
from jax.experimental.pallas import tpu as pltpu
from jax.experimental.pallas import tpu_sc as plsc

MIX_WIDTH = 768
MEM_HEADS = 4
MEM_HEAD_DIM = 64
MEM_WIDTH = MEM_HEADS * MEM_HEAD_DIM
HGRN_HEADS = 6
HGRN_DK = 128
MOBA_HEADS = 12
MOBA_HEAD_DIM = 64
MOBA_BLOCK = 256
MOBA_TOPK = 3
N_EXPERTS = 32
TOP_K = 4
SWIGLU_ALPHA = 1.702
SWIGLU_LIMIT = 7.0
LN_EPS = 1e-5
RMS_EPS = 1e-6

LANES = 128
SUBLANES = 8
VMEM_LIMIT_BYTES = 56 * 1024 * 1024

INPROJ_ROWS = 512
HGRN_CHUNK = 64
HGRN_ROWS = 2048
POST_ROWS = 512
MOBA_TILE = 256
MOBA_TILE_GROUP = 32
MOBA_PLACE_GROUP = 8
MOBA_PICK_GROUP = 16
MOBA_MERGE_GROUP = 16
MOBA_OWN_GROUP = 16
MOE_ROWS = 512
COMBINE_ROWS = 512
SC_WINDOW = 128
SC_IDX_ROWS = 8
SC_CHUNK = 32

BF16 = jnp.bfloat16
F32 = jnp.float32

_NT = (((1,), (1,)), ((), ()))
_TN = (((0,), (0,)), ((), ()))


def _alibi_slope_list(n):
    def pow2(m):
        start = 2.0 ** (-(2.0 ** -(math.log2(m) - 3)))
        return [start ** (i + 1) for i in range(m)]
    if math.log2(n).is_integer():
        return pow2(n)
    c = 2 ** math.floor(math.log2(n))
    return pow2(c) + _alibi_slope_list(2 * c)[0::2][:n - c]


def _sigmoid(x):
    return 1.0 / (1.0 + jnp.exp(-x))


def _params(*sem):
    return pltpu.CompilerParams(dimension_semantics=sem, vmem_limit_bytes=VMEM_LIMIT_BYTES)


def _store_subrows(ref, value):
    sub = value.shape[1] // LANES
    for c in range(sub):
        ref[pl.ds(c, value.shape[0], stride=sub), :] = value[:, c * LANES:(c + 1) * LANES]


def _load_subrows(ref, rows, sub, first=0, stride=None):
    stride = stride or sub
    return jnp.concatenate(
        [ref[pl.ds(first + c, rows, stride=stride), :] for c in range(sub)], axis=1)


def _inproj_kernel(x_ref, w_ref, h_ref, *moba_refs):
    h = jnp.dot(x_ref[...].astype(BF16), w_ref[...], preferred_element_type=F32)
    h_ref[...] = h
    if moba_refs:
        kv_ref, q6_ref = moba_refs
        n_pairs = q6_ref.shape[0]
        for p in range(n_pairs):
            q6_ref[p] = h[:, p * LANES:(p + 1) * LANES] * (MOBA_HEAD_DIM ** -0.5)
            for part in range(2):
                col = (1 + part) * MIX_WIDTH + p * LANES
                kv_ref[:, (2 * p + part) * LANES:(2 * p + part + 1) * LANES] = (
                    h[:, col:col + LANES].astype(BF16))


def _inproj(x2, w_bf16, for_moba):
    T, D = x2.shape
    N = w_bf16.shape[1]
    tm = INPROJ_ROWS
    out_shape = [jax.ShapeDtypeStruct((T, N), F32)]
    out_specs = [pl.BlockSpec((tm, N), lambda i: (i, 0))]
    if for_moba:
        NP = MOBA_HEADS // 2
        out_shape += [jax.ShapeDtypeStruct((T, 2 * MIX_WIDTH), BF16),
                      jax.ShapeDtypeStruct((NP, T, LANES), F32)]
        out_specs += [pl.BlockSpec((tm, 2 * MIX_WIDTH), lambda i: (i, 0)),
                      pl.BlockSpec((NP, tm, LANES), lambda i: (0, i, 0))]
    return pl.pallas_call(
        _inproj_kernel,
        grid=(T // tm,),
        in_specs=[pl.BlockSpec((tm, D), lambda i: (i, 0)),
                  pl.BlockSpec((D, N), lambda i: (0, 0))],
        out_specs=out_specs,
        out_shape=out_shape,
        compiler_params=_params("parallel"),
        name="inproj",
    )(x2, w_bf16)


def _cumsum_rows(x, row):
    n = x.shape[0]
    sh = 1
    while sh < n:
        x = x + jnp.where(row >= sh, pltpu.roll(x, sh, 0), 0.0)
        sh *= 2
    return x


def _bcast_row(a, group, r):
    n = a.shape[0]
    a3 = a.reshape(n // group, group, LANES)
    return jnp.broadcast_to(a3[:, r:r + 1, :], a3.shape).reshape(n, LANES)


def _hgrn_chunk(qr, fr, v, gr, lb, ng, e_sum, st_t):
    C = qr.shape[0]
    row = lax.broadcasted_iota(jnp.int32, (C, LANES), 0)
    rr = lax.broadcasted_iota(jnp.int32, (C, C), 0)
    cc = lax.broadcasted_iota(jnp.int32, (C, C), 1)

    q = qr * _sigmoid(qr)
    forget = lb + (1.0 - lb) * _sigmoid(fr)
    k = 1.0 - forget
    G = _cumsum_rows(jnp.log(forget), row)

    z = jnp.log(k) - G
    parts = []
    for s in range(SUBLANES):
        parts.append((q * jnp.exp(jnp.minimum(G + _bcast_row(z, SUBLANES, s), 0.0))).astype(BF16))
    a_diag = jnp.dot(jnp.concatenate(parts, axis=1), e_sum, preferred_element_type=F32)
    A = jnp.where(((rr >> 3) == (cc >> 3)) & (cc <= rr), a_diag, 0.0)

    m = SUBLANES
    while m < C:
        lg = int(math.log2(m))
        Gr = _bcast_row(G, 2 * m, m - 1)
        second = ((row >> lg) & 1) == 1
        qm = q * jnp.exp(jnp.where(second, G - Gr, -jnp.inf))
        km = k * jnp.exp(jnp.where(second, -jnp.inf, Gr - G))
        am = lax.dot_general(qm.astype(BF16), km.astype(BF16), _NT, preferred_element_type=F32)
        A = A + jnp.where((rr >> (lg + 1)) == (cc >> (lg + 1)), am, 0.0)
        m *= 2

    vb = v.astype(BF16)
    o = jnp.dot(A.astype(BF16), vb, preferred_element_type=F32)
    o = o + lax.dot_general((q * jnp.exp(G)).astype(BF16), st_t.astype(BF16), _NT,
                            preferred_element_type=F32)
    g_end = G[C - 1:C, :]
    kd = (k * jnp.exp(g_end - G)).astype(BF16)
    st_new = st_t * jnp.exp(g_end) + lax.dot_general(vb, kd, _TN, preferred_element_type=F32)

    ms = jnp.mean(o * o, axis=-1, keepdims=True)
    out = o * lax.rsqrt(ms + RMS_EPS) * ng * _sigmoid(gr)
    return out, st_new


def _hgrn_kernel(q_ref, f_ref, i_ref, g_ref, lb_ref, ng_ref, e_ref, o_ref, st_ref, *, chunk):
    @pl.when(pl.program_id(2) == 0)
    def _():
        st_ref[...] = jnp.zeros_like(st_ref)

    lb = lb_ref[0]
    ng = ng_ref[...]
    e_sum = e_ref[...]
    n_chunks = q_ref.shape[1] // chunk
    for c in range(n_chunks):
        sl = pl.ds(c * chunk, chunk)
        out, st_new = _hgrn_chunk(q_ref[0, sl, :], f_ref[0, sl, :], i_ref[0, sl, :],
                                  g_ref[0, sl, :], lb, ng, e_sum, st_ref[...])
        st_ref[...] = st_new
        o_ref[0, sl, :] = out


def _hgrn_mixer(h3, lb, norm_g):
    B, S, _ = h3.shape
    ts = min(HGRN_ROWS, S)
    C = HGRN_CHUNK
    H = HGRN_HEADS
    e_sum = (jnp.arange(SUBLANES * LANES)[:, None] // LANES == jnp.arange(C)[None, :] % SUBLANES
             ).astype(BF16)
    col = lambda off: pl.BlockSpec((1, ts, LANES), lambda b, h, s, off=off: (b, s, off + h))
    return pl.pallas_call(
        functools.partial(_hgrn_kernel, chunk=C),
        grid=(B, H, S // ts),
        in_specs=[col(0), col(H), col(2 * H), col(3 * H),
                  pl.BlockSpec((1, 1, LANES), lambda b, h, s: (h, 0, 0)),
                  pl.BlockSpec((1, LANES), lambda b, h, s: (0, 0)),
                  pl.BlockSpec((SUBLANES * LANES, C), lambda b, h, s: (0, 0))],
        out_specs=pl.BlockSpec((1, ts, LANES), lambda b, h, s: (b, s, h)),
        out_shape=jax.ShapeDtypeStruct((B, S, MIX_WIDTH), F32),
        scratch_shapes=[pltpu.VMEM((HGRN_DK, HGRN_DK), F32)],
        compiler_params=_params("parallel", "parallel", "arbitrary"),
        name="hgrn",
    )(h3, h3, h3, h3, lb.reshape(H, 1, LANES), norm_g.reshape(1, LANES), e_sum)


def _moba_pick_kernel(q_ref, kf_ref, sr_ref, cnt_ref, kmean_ref, *, qblocks):
    BLK = MOBA_BLOCK
    W = qblocks * BLK
    nbp = kmean_ref.shape[0]
    i0 = pl.program_id(2) * qblocks

    @pl.when(pl.program_id(2) == 0)
    def _():
        kmean_ref[...] = jnp.zeros_like(kmean_ref)

    for j in range(qblocks):
        kmean_ref[pl.ds(i0 + j, 1), :] = jnp.mean(kf_ref[0, j * BLK:(j + 1) * BLK, :], axis=0,
                                                  keepdims=True)
    km = kmean_ref[...]
    lane_k = lax.broadcasted_iota(jnp.int32, (nbp, LANES), 1)
    km2 = jnp.concatenate([jnp.where(lane_k < MOBA_HEAD_DIM, km, 0.0),
                           jnp.where(lane_k >= MOBA_HEAD_DIM, km, 0.0)], axis=0)
    gate = lax.dot_general(km2, q_ref[0], _NT, precision=lax.Precision.HIGHEST,
                           preferred_element_type=F32).reshape(2, nbp, W)
    nblk = lax.broadcasted_iota(jnp.int32, (2, nbp, W), 1)
    qi = i0 + (lax.broadcasted_iota(jnp.int32, (2, 1, W), 2) >> int(math.log2(BLK)))
    g = jnp.where(nblk < qi, gate, -jnp.inf)
    picks = []
    for _ in range(MOBA_TOPK):
        mx = jnp.max(g, axis=1, keepdims=True)
        idx = jnp.min(jnp.where(g == mx, nblk, nbp), axis=1, keepdims=True)
        picks.append((idx, (mx > -jnp.inf) & (idx < qi)))
        g = jnp.where(nblk == idx, -jnp.inf, g)
    chosen = jnp.zeros((2, nbp, W), F32)
    for idx, valid in picks:
        chosen = chosen + jnp.where((nblk == idx) & valid, 1.0, 0.0)
    qa = lax.broadcasted_iota(jnp.int32, (BLK, BLK), 0)
    qc = lax.broadcasted_iota(jnp.int32, (BLK, BLK), 1)
    before = jnp.where(qa < qc, 1.0, 0.0).astype(BF16)
    chosen2 = chosen.reshape(2 * nbp, W).astype(BF16)
    earlier = jnp.concatenate(
        [jnp.dot(chosen2[:, j * BLK:(j + 1) * BLK], before, preferred_element_type=F32)
         for j in range(qblocks)], axis=1).reshape(2, nbp, W)
    orow = lax.broadcasted_iota(jnp.int32, (2 * SUBLANES, W), 0)
    out = jnp.zeros((2 * SUBLANES, W), jnp.int32)
    for j, (idx, valid) in enumerate(picks):
        rank = jnp.sum(jnp.where(nblk == idx, earlier, 0.0), axis=1, keepdims=True).astype(jnp.int32)
        sel = jnp.where(valid, idx, -1)
        for hh in range(2):
            rep = hh * MOBA_TOPK + j
            out = jnp.where(orow == rep, sel[hh], out)
            out = jnp.where(orow == SUBLANES + rep, rank[hh], out)
    sr_ref[0, 0] = out
    lane_c = lax.broadcasted_iota(jnp.int32, (nbp, LANES), 1)
    for j in range(qblocks):
        per_block = jnp.sum(chosen[:, :, j * BLK:(j + 1) * BLK], axis=2, keepdims=True)
        cnt_ref[0, 0, j * nbp:(j + 1) * nbp, :] = jnp.where(
            lane_c == 0, per_block[0], jnp.where(lane_c == 1, per_block[1], 0.0))


def _moba_place_kernel(sr_ref, tab_ref, idx_ref, *, group, spare):
    BLK = MOBA_BLOCK
    n_rep = 2 * MOBA_TOPK
    b = pl.program_id(0)
    p = pl.program_id(1)
    ig = pl.program_id(2)
    nrow = lax.broadcasted_iota(jnp.int32, (LANES, BLK), 0)
    orow = lax.broadcasted_iota(jnp.int32, (SC_IDX_ROWS, BLK), 0)
    qpos = lax.broadcasted_iota(jnp.int32, (1, BLK), 1)
    for g in range(group):
        blk = sr_ref[0, 0, :, g * BLK:(g + 1) * BLK]
        tab_t = tab_ref[0, 0, g * SUBLANES:(g + 1) * SUBLANES, :].T
        early = jnp.minimum(ig * group + g, MOBA_TOPK - 1)
        out = jnp.zeros((SC_IDX_ROWS, BLK), jnp.int32)
        for rep in range(n_rep):
            hh = rep // MOBA_TOPK
            sel = blk[rep:rep + 1, :]
            rank = blk[SUBLANES + rep:SUBLANES + rep + 1, :]
            start = jnp.sum(jnp.where(nrow == sel, tab_t[:, hh:hh + 1], 0.0), axis=0, keepdims=True)
            unused = spare + (((b * pl.num_programs(1) + p) * MOBA_TOPK + early) * n_rep + rep) * BLK
            dest = jnp.where(sel >= 0, start.astype(jnp.int32) + rank, unused + qpos)
            out = jnp.where(orow == rep, dest, out)
        idx_ref[:, g * BLK:(g + 1) * BLK] = out


def _moba_tiles_kernel(tb_ref, tp_ref, th_ref, tn_ref, nu_ref, sl_ref, q_ref, *refs, group):
    kv_refs, o_ref = refs[:group], refs[group]
    t = pl.program_id(0)
    tq = q_ref.shape[0] // group

    @pl.when(t * group < nu_ref[0])
    def _():
        lane = lax.broadcasted_iota(jnp.int32, (tq, LANES), 1)
        kpos = lax.broadcasted_iota(jnp.int32, (1, MOBA_BLOCK), 1)
        for g in range(group):
            tt = t * group + g
            hh = th_ref[tt]
            slope = sl_ref[2 * tp_ref[tt] + hh]
            head = (lane >> 6) == hh
            rows = pl.ds(g * tq, tq)
            q = jnp.where(head, q_ref[rows, :], 0.0).astype(BF16)
            s = lax.dot_general(q, kv_refs[g][0, :, :LANES], _NT, preferred_element_type=F32)
            s = s + slope * (kpos + tn_ref[tt] * MOBA_BLOCK).astype(F32)
            m = jnp.max(s, axis=1, keepdims=True)
            pr = jnp.exp(s - m)
            l = jnp.sum(pr, axis=1, keepdims=True)
            o = jnp.dot(pr.astype(BF16), kv_refs[g][0, :, LANES:], preferred_element_type=F32) / l
            o_ref[rows, :] = jnp.where(head, o, m + jnp.log(l))


def _moba_own_kernel(sl_ref, q_ref, kv_ref, o_ref, lse_ref):
    BLK = MOBA_BLOCK
    p = pl.program_id(1)
    n_blocks = q_ref.shape[1] // BLK
    lane = lax.broadcasted_iota(jnp.int32, (BLK, LANES), 1)
    rr = lax.broadcasted_iota(jnp.int32, (BLK, BLK), 0)
    cc = lax.broadcasted_iota(jnp.int32, (BLK, BLK), 1)
    first = lane < MOBA_HEAD_DIM
    for h in range(n_blocks):
        i = pl.program_id(2) * n_blocks + h
        rows = pl.ds(h * BLK, BLK)
        qf = q_ref[0, rows, :]
        k_own = kv_ref[0, rows, :LANES]
        v_own = kv_ref[0, rows, LANES:]
        key_pos = (lax.broadcasted_iota(jnp.int32, (1, BLK), 1) + i * BLK).astype(F32)
        outs, lses = [], []
        for hh in range(2):
            head = (lane >> 6) == hh
            slope = sl_ref[2 * p + hh]
            qh = jnp.where(head, qf * (MOBA_HEAD_DIM ** -0.5), 0.0).astype(BF16)
            s = lax.dot_general(qh, k_own, _NT, preferred_element_type=F32)
            s = jnp.where(cc <= rr, s + slope * key_pos, -jnp.inf)
            m = jnp.max(s, axis=1, keepdims=True)
            pr = jnp.exp(s - m)
            l = jnp.sum(pr, axis=1, keepdims=True)
            lses.append(m + jnp.log(l))
            outs.append(jnp.dot(pr.astype(BF16), v_own, preferred_element_type=F32) / l)
        o_ref[0, rows, :] = jnp.where(first, outs[0], outs[1])
        lse_ref[0, rows, :] = jnp.where(first, lses[0], lses[1])


def _moba_merge_kernel(o_ref, lse_ref, pg_ref, out_ref):
    BLK = MOBA_BLOCK
    first = lax.broadcasted_iota(jnp.int32, (BLK, LANES), 1) < MOBA_HEAD_DIM
    for h in range(o_ref.shape[1] // BLK):
        i = pl.program_id(2) * (o_ref.shape[1] // BLK) + h
        rows = pl.ds(h * BLK, BLK)
        lses = [lse_ref[0, rows, :]]
        vals = [o_ref[0, rows, :]]
        for j in range(MOBA_TOPK):
            p0 = pg_ref[j, 0, rows, :]
            p1 = pg_ref[MOBA_TOPK + j, 0, rows, :]
            has_block = j < i
            stats = pltpu.roll(jnp.where(first, p1, p0), MOBA_HEAD_DIM, 1)
            lses.append(jnp.where(has_block, stats, -jnp.inf))
            vals.append(jnp.where(has_block, jnp.where(first, p0, p1), 0.0))
        top = functools.reduce(jnp.maximum, lses)
        ws = [jnp.exp(x - top) for x in lses]
        out_ref[0, rows, :] = sum(w * v for w, v in zip(ws, vals)) / sum(ws)


def _moba_mixer(h3, kv3, q6):
    B, S, _ = h3.shape
    T = B * S
    BLK = MOBA_BLOCK
    TQ = MOBA_TILE
    G = MOBA_TILE_GROUP
    NB = S // BLK
    GI = math.gcd(NB, MOBA_PLACE_GROUP)
    nbp = -(-NB // SUBLANES) * SUBLANES
    NP = MOBA_HEADS // 2
    n_rep = 2 * MOBA_TOPK
    slopes = jnp.asarray(_alibi_slope_list(MOBA_HEADS), F32)

    QB = math.gcd(NB, MOBA_PICK_GROUP)
    selrank, counts = pl.pallas_call(
        functools.partial(_moba_pick_kernel, qblocks=QB),
        grid=(B, NP, NB // QB),
        in_specs=[pl.BlockSpec((1, QB * BLK, LANES), lambda b, p, i: (b, i, p)),
                  pl.BlockSpec((1, QB * BLK, LANES), lambda b, p, i: (b, i, NP + p))],
        out_specs=[pl.BlockSpec((1, 1, 2 * SUBLANES, QB * BLK), lambda b, p, i: (b, p, 0, i)),
                   pl.BlockSpec((1, 1, QB * nbp, LANES), lambda b, p, i: (b, p, i, 0))],
        out_shape=[jax.ShapeDtypeStruct((B, NP, 2 * SUBLANES, S), jnp.int32),
                   jax.ShapeDtypeStruct((B, NP, NB * nbp, LANES), F32)],
        scratch_shapes=[pltpu.VMEM((nbp, LANES), F32)],
        compiler_params=_params("parallel", "parallel", "arbitrary"),
        name="moba_pick",
    )(h3, h3)

    OB = math.gcd(NB, MOBA_OWN_GROUP)
    blk_spec = lambda w: pl.BlockSpec((1, OB * BLK, w), lambda b, p, i, sl: (b, i, p))
    own_o, own_lse = pl.pallas_call(
        _moba_own_kernel,
        grid_spec=pltpu.PrefetchScalarGridSpec(
            num_scalar_prefetch=1,
            grid=(B, NP, NB // OB),
            in_specs=[blk_spec(LANES), blk_spec(2 * LANES)],
            out_specs=[blk_spec(LANES), blk_spec(LANES)]),
        out_shape=[jax.ShapeDtypeStruct((B, S, MIX_WIDTH), F32)] * 2,
        compiler_params=_params("parallel", "parallel", "parallel"),
        name="moba_own",
    )(slopes, h3, kv3)

    cnt = counts.reshape(B, NP, NB, nbp, LANES)[:, :, :, :NB, :2].astype(jnp.int32)
    cnt = cnt.transpose(0, 1, 2, 4, 3)
    base = jnp.cumsum(cnt, axis=2) - cnt
    total = jnp.sum(cnt, axis=2)
    padded = (total + TQ - 1) // TQ * TQ
    pend = jnp.cumsum(padded.reshape(-1))
    seg_start = (pend - padded.reshape(-1)).reshape(B, NP, 1, 2, NB)
    table = jnp.zeros((B, NP, NB, SUBLANES, LANES), F32).at[:, :, :, :2, :NB].set(
        (seg_start + base).astype(F32)).reshape(B, NP, NB * SUBLANES, LANES)
    n_seg = B * NP * 2 * NB
    max_tiles = -(-((T * NP * n_rep) // TQ + n_seg) // G) * G
    first_row = jnp.arange(max_tiles, dtype=jnp.int32) * TQ
    tile_seg = jnp.minimum(jnp.sum(pend[None, :] <= first_row[:, None], axis=1), n_seg - 1)
    tile_n = (tile_seg % NB).astype(jnp.int32)
    tile_h = ((tile_seg // NB) % 2).astype(jnp.int32)
    tile_p = ((tile_seg // (2 * NB)) % NP).astype(jnp.int32)
    tile_b = (tile_seg // (2 * NB * NP)).astype(jnp.int32)
    n_used = (pend[-1] // TQ).astype(jnp.int32).reshape(1)
    spare = max_tiles * TQ
    n_rows = spare + max(B * NP * MOBA_TOPK * n_rep * BLK, G * TQ)

    idx = pl.pallas_call(
        functools.partial(_moba_place_kernel, group=GI, spare=spare),
        grid=(B, NP, NB // GI),
        in_specs=[pl.BlockSpec((1, 1, 2 * SUBLANES, GI * BLK), lambda b, p, i: (b, p, 0, i)),
                  pl.BlockSpec((1, 1, GI * SUBLANES, LANES), lambda b, p, i: (b, p, i, 0))],
        out_specs=pl.BlockSpec((SC_IDX_ROWS, GI * BLK),
                               lambda b, p, i: (0, p * (T // (GI * BLK)) + b * (NB // GI) + i)),
        out_shape=jax.ShapeDtypeStruct((SC_IDX_ROWS, NP * T), jnp.int32),
        compiler_params=_params("parallel", "parallel", "parallel"),
        name="moba_place",
    )(selrank, table)

    qs = _sc_scatter_rows(q6.reshape(NP * T, LANES), idx, n_rep, n_rows)

    live = lambda t, tb, tp, th, tn, nu, sl: (jnp.where(t * G < nu[0], t, max_tiles // G), 0)
    kv_blk = lambda g: pl.BlockSpec(
        (1, BLK, 2 * LANES),
        lambda t, tb, tp, th, tn, nu, sl: (tb[t * G + g], tn[t * G + g], tp[t * G + g]))
    part = pl.pallas_call(
        functools.partial(_moba_tiles_kernel, group=G),
        grid_spec=pltpu.PrefetchScalarGridSpec(
            num_scalar_prefetch=6,
            grid=(max_tiles // G,),
            in_specs=[pl.BlockSpec((G * TQ, LANES), live)] + [kv_blk(g) for g in range(G)],
            out_specs=pl.BlockSpec((G * TQ, LANES), live)),
        out_shape=jax.ShapeDtypeStruct((n_rows, LANES), F32),
        compiler_params=_params("arbitrary"),
        name="moba_tiles",
    )(tile_b, tile_p, tile_h, tile_n, n_used, slopes, qs, *([kv3] * G))

    pg = _sc_gather_rows(part, idx[:n_rep].reshape(-1)).reshape(n_rep, NP, T, LANES)

    MB = math.gcd(NB, MOBA_MERGE_GROUP)
    blk = pl.BlockSpec((1, MB * BLK, LANES), lambda b, p, i: (b, i, p))
    return pl.pallas_call(
        _moba_merge_kernel,
        grid=(B, NP, NB // MB),
        in_specs=[blk, blk,
                  pl.BlockSpec((n_rep, 1, MB * BLK, LANES),
                               lambda b, p, i: (0, p, b * (NB // MB) + i, 0))],
        out_specs=blk,
        out_shape=jax.ShapeDtypeStruct((B, S, MIX_WIDTH), F32),
        compiler_params=_params("parallel", "parallel", "parallel"),
        name="moba_merge",
    )(own_o, own_lse, pg)


def _memkv_kernel(mem_ref, w_ref, kv_ref):
    kv_ref[0] = jnp.dot(mem_ref[0].astype(BF16), w_ref[...].astype(BF16),
                        preferred_element_type=F32).astype(BF16)


def _memkv(mem, w_kv):
    B, M, D = mem.shape
    N = w_kv.shape[1]
    return pl.pallas_call(
        _memkv_kernel,
        grid=(B,),
        in_specs=[pl.BlockSpec((1, M, D), lambda b: (b, 0, 0)),
                  pl.BlockSpec((D, N), lambda b: (0, 0))],
        out_specs=pl.BlockSpec((1, M, N), lambda b: (b, 0, 0)),
        out_shape=jax.ShapeDtypeStruct((B, M, N), BF16),
        compiler_params=_params("parallel"),
        name="memkv",
    )(mem, w_kv)


def _layer_norm(z, g, b):
    mu = jnp.mean(z, axis=-1, keepdims=True)
    zc = z - mu
    var = jnp.mean(zc * zc, axis=-1, keepdims=True)
    return zc * lax.rsqrt(var + LN_EPS) * g + b


def _post_kernel(x_ref, mix_ref, mq_ref, kv_ref, wo_ref, g_ref, b_ref, wr_ref, br_ref,
                 x1_ref, x1s_ref, idx_ref, gate_ref, rank_ref, cnt_ref, run_ref, *, alpha):
    tm = x_ref.shape[0]
    mq = mq_ref[...]
    kv = kv_ref[0]
    km = kv[:, :MEM_WIDTH]
    vm = kv[:, MEM_WIDTH:]
    lane = lax.broadcasted_iota(jnp.int32, (tm, MEM_WIDTH), 1)
    scale = MEM_HEAD_DIM ** -0.5
    mo = jnp.zeros((tm, MEM_WIDTH), F32)
    for hd in range(MEM_HEADS):
        head = (lane >> 6) == hd
        qh = jnp.where(head, mq * scale, 0.0).astype(BF16)
        s = lax.dot_general(qh, km, _NT, preferred_element_type=F32)
        m = jnp.max(s, axis=1, keepdims=True)
        p = jnp.exp(s - m)
        l = jnp.sum(p, axis=1, keepdims=True)
        oh = jnp.dot(p.astype(BF16), vm, preferred_element_type=F32) / l
        mo = jnp.where(head, oh, mo)

    y = jnp.dot(mix_ref[...].astype(BF16), wo_ref[:MIX_WIDTH, :], preferred_element_type=F32)
    y = y + jnp.dot(mo.astype(BF16), wo_ref[MIX_WIDTH:, :], preferred_element_type=F32)
    x1 = _layer_norm(alpha * x_ref[...] + y, g_ref[...], b_ref[...])
    x1_ref[...] = x1
    _store_subrows(x1s_ref, x1)

    x_hi = x1.astype(BF16)
    x_lo = (x1 - x_hi.astype(F32)).astype(BF16)
    hi = jnp.dot(x_hi, wr_ref[...], preferred_element_type=F32)
    lo = jnp.dot(x_lo, wr_ref[:, :LANES], preferred_element_type=F32)
    logits = hi[:, :LANES] + hi[:, LANES:] + lo
    g = logits.T[:N_EXPERTS] + br_ref[...]
    erow = lax.broadcasted_iota(jnp.int32, (N_EXPERTS, tm), 0)
    orow = lax.broadcasted_iota(jnp.int32, (SUBLANES, tm), 0)
    idx_out = jnp.zeros((SUBLANES, tm), jnp.int32)
    vals, picks = [], []
    chosen = jnp.zeros((N_EXPERTS, tm), F32)
    for kk in range(TOP_K):
        mx = jnp.max(g, axis=0, keepdims=True)
        idx = jnp.min(jnp.where(g == mx, erow, N_EXPERTS), axis=0, keepdims=True)
        idx_out = jnp.where(orow == kk, idx, idx_out)
        vals.append(mx)
        picks.append(idx)
        chosen = chosen + jnp.where(erow == idx, 1.0, 0.0)
        g = jnp.where(erow == idx, -jnp.inf, g)
    evs = [jnp.exp(v - vals[0]) for v in vals]
    den = sum(evs)
    gate_out = jnp.zeros((SUBLANES, tm), F32)
    for kk in range(TOP_K):
        gate_out = jnp.where(orow == kk, evs[kk] / den, gate_out)
    idx_ref[...] = idx_out
    gate_ref[...] = gate_out

    @pl.when(pl.program_id(0) == 0)
    def _():
        run_ref[...] = jnp.zeros_like(run_ref)

    ta = lax.broadcasted_iota(jnp.int32, (tm, tm), 0)
    tb = lax.broadcasted_iota(jnp.int32, (tm, tm), 1)
    before = jnp.where(ta < tb, 1.0, 0.0).astype(BF16)
    earlier = jnp.dot(chosen.astype(BF16), before, preferred_element_type=F32) + run_ref[:, :1]
    rank_out = jnp.zeros((SUBLANES, tm), jnp.int32)
    for kk in range(TOP_K):
        rank = jnp.sum(jnp.where(erow == picks[kk], earlier, 0.0), axis=0, keepdims=True)
        rank_out = jnp.where(orow == kk, rank.astype(jnp.int32), rank_out)
    rank_ref[...] = rank_out
    run_ref[...] = run_ref[...] + jnp.sum(chosen, axis=1, keepdims=True)
    cnt_ref[...] = run_ref[...]


def _post_mixer(x2, mix2, h2, kv, w_o_bf16, ln_g, ln_b, w_router, b_router, seq_len, alpha):
    T, D = x2.shape
    tm = POST_ROWS
    N = h2.shape[1]
    M = kv.shape[1]
    mq_col = (N - MEM_WIDTH) // MEM_WIDTH
    tiles_per_seq = seq_len // tm
    wr = jnp.zeros((D, LANES), F32).at[:, :N_EXPERTS].set(w_router)
    wr_hi = wr.astype(BF16)
    wr = jnp.concatenate([wr_hi, (wr - wr_hi.astype(F32)).astype(BF16)], axis=1)
    br = b_router.reshape(N_EXPERTS, 1)
    row = lambda n: pl.BlockSpec((tm, n), lambda i: (i, 0))
    full = lambda a, b: pl.BlockSpec((a, b), lambda i: (0, 0))
    per_token = pl.BlockSpec((SUBLANES, tm), lambda i: (0, i))
    return pl.pallas_call(
        functools.partial(_post_kernel, alpha=alpha),
        grid=(T // tm,),
        in_specs=[row(D), row(MIX_WIDTH),
                  pl.BlockSpec((tm, MEM_WIDTH), lambda i: (i, mq_col)),
                  pl.BlockSpec((1, M, 2 * MEM_WIDTH), lambda i: (i // tiles_per_seq, 0, 0)),
                  full(D, D), full(1, D), full(1, D),
                  full(D, 2 * LANES), full(N_EXPERTS, 1)],
        out_specs=[row(D), pl.BlockSpec((tm * (D // LANES), LANES), lambda i: (i, 0)),
                   per_token, per_token, per_token, full(N_EXPERTS, LANES)],
        out_shape=[jax.ShapeDtypeStruct((T, D), F32),
                   jax.ShapeDtypeStruct((T * (D // LANES), LANES), F32),
                   jax.ShapeDtypeStruct((SUBLANES, T), jnp.int32),
                   jax.ShapeDtypeStruct((SUBLANES, T), F32),
                   jax.ShapeDtypeStruct((SUBLANES, T), jnp.int32),
                   jax.ShapeDtypeStruct((N_EXPERTS, LANES), F32)],
        scratch_shapes=[pltpu.VMEM((N_EXPERTS, LANES), F32)],
        compiler_params=_params("arbitrary"),
        name="post_mixer",
    )(x2, mix2, h2, kv, w_o_bf16, ln_g.reshape(1, D), ln_b.reshape(1, D), wr, br)


def _sc_mesh():
    return plsc.VectorSubcoreMesh(core_axis_name="core", subcore_axis_name="subcore")


def _sc_scatter_rows(rows, idx, n_rep, n_out):
    R, W = rows.shape

    @functools.partial(pl.kernel, out_type=jax.ShapeDtypeStruct((n_out, W), rows.dtype),
                       mesh=_sc_mesh(), scratch_types=[])
    def scatter(x_hbm, i_hbm, o_hbm):
        def body(x_vmem, i_vmem):
            for r in range(n_rep):
                pltpu.sync_copy(x_vmem, o_hbm.at[i_vmem.at[r]])

        pltpu.emit_pipeline(
            body, grid=(R // SC_WINDOW,),
            in_specs=[pl.BlockSpec((SC_WINDOW, W), lambda i: (i, 0)),
                      pl.BlockSpec((SC_IDX_ROWS, SC_WINDOW), lambda i: (0, i))],
            out_specs=[], core_axis_name=("core", "subcore"),
            dimension_semantics=(pltpu.PARALLEL,), trace_scopes=False)(x_hbm, i_hbm)

    return scatter(rows, idx)


def _sc_gather_rows(table, idx):
    n = idx.shape[0]
    W = table.shape[1]

    @functools.partial(pl.kernel, out_type=jax.ShapeDtypeStruct((n, W), table.dtype),
                       mesh=_sc_mesh(), scratch_types=[])
    def gather(t_hbm, i_hbm, o_hbm):
        def body(i_vmem, o_vmem):
            pltpu.sync_copy(t_hbm.at[i_vmem.at[0]], o_vmem)

        pltpu.emit_pipeline(
            body, grid=(n // SC_WINDOW,),
            in_specs=[pl.BlockSpec((1, SC_WINDOW), lambda i: (0, i))],
            out_specs=[pl.BlockSpec((SC_WINDOW, W), lambda i: (i, 0))],
            core_axis_name=("core", "subcore"),
            dimension_semantics=(pltpu.PARALLEL,), trace_scopes=False)(i_hbm, o_hbm)

    return gather(table, idx.reshape(1, n))


def _sc_workers():
    info = pltpu.get_tpu_info().sparse_core
    return info.num_cores, info.num_cores * info.num_subcores


def _sc_scatter_slabs(rows, idx, n_rep, n_out):
    R, S, W = rows.shape
    n_cores, n_workers = _sc_workers()
    per_worker = (R // SC_WINDOW) // n_workers
    assert per_worker * n_workers * SC_WINDOW == R

    @functools.partial(pl.kernel, out_type=jax.ShapeDtypeStruct((n_out, S, W), rows.dtype),
                       mesh=_sc_mesh(),
                       scratch_types=[pltpu.VMEM((SC_IDX_ROWS, SC_WINDOW), jnp.int32),
                                      pltpu.VMEM((SC_CHUNK, S, W), rows.dtype)])
    def scatter(x_hbm, i_hbm, o_hbm, ibuf, buf):
        wid = lax.axis_index("subcore") * n_cores + lax.axis_index("core")

        @pl.loop(0, per_worker)
        def _(s):
            first = (wid * per_worker + s) * SC_WINDOW
            pltpu.sync_copy(i_hbm.at[:, pl.ds(first, SC_WINDOW)], ibuf)
            for c in range(SC_WINDOW // SC_CHUNK):
                pltpu.sync_copy(x_hbm.at[pl.ds(first + c * SC_CHUNK, SC_CHUNK)], buf)
                for r in range(n_rep):
                    pltpu.sync_copy(buf, o_hbm.at[ibuf.at[r, pl.ds(c * SC_CHUNK, SC_CHUNK)]])

    return scatter(rows, idx)


def _sc_gather_slabs(table, idx):
    n = idx.shape[0]
    S, W = table.shape[1:]
    n_cores, n_workers = _sc_workers()
    per_worker = (n // SC_WINDOW) // n_workers
    assert per_worker * n_workers * SC_WINDOW == n
    n_chunks = SC_WINDOW // SC_CHUNK

    @functools.partial(pl.kernel, out_type=jax.ShapeDtypeStruct((n, S, W), table.dtype),
                       mesh=_sc_mesh(),
                       scratch_types=[pltpu.VMEM((1, SC_WINDOW), jnp.int32),
                                      pltpu.VMEM((2, SC_CHUNK, S, W), table.dtype),
                                      pltpu.SemaphoreType.DMA((2,)), pltpu.SemaphoreType.DMA((2,))])
    def gather(t_hbm, i_hbm, o_hbm, ibuf, buf, fetch_sem, store_sem):
        wid = lax.axis_index("subcore") * n_cores + lax.axis_index("core")

        @pl.loop(0, per_worker)
        def _(s):
            blk = wid * per_worker + s
            pltpu.sync_copy(i_hbm.at[pl.ds(blk, 1)], ibuf)

            def fetch(c):
                return pltpu.make_async_copy(
                    t_hbm.at[ibuf.at[0, pl.ds(c * SC_CHUNK, SC_CHUNK)]], buf.at[c % 2],
                    fetch_sem.at[c % 2])

            def store(c):
                return pltpu.make_async_copy(
                    buf.at[c % 2], o_hbm.at[pl.ds(blk * SC_WINDOW + c * SC_CHUNK, SC_CHUNK)],
                    store_sem.at[c % 2])

            fetch(0).start()
            for c in range(n_chunks):
                if c + 1 < n_chunks:
                    if c >= 1:
                        store(c - 1).wait()
                    fetch(c + 1).start()
                fetch(c).wait()
                store(c).start()
            store(n_chunks - 2).wait()
            store(n_chunks - 1).wait()

    return gather(table, idx.reshape(n // SC_WINDOW, SC_WINDOW))


def _route(top_idx, rank, counts):
    rb = MOE_ROWS
    n_tokens = top_idx.shape[1]
    tk = n_tokens * TOP_K
    padded = (counts + rb - 1) // rb * rb
    pend = jnp.cumsum(padded)
    pstart = pend - padded
    experts = jnp.arange(N_EXPERTS, dtype=jnp.int32)
    start = jnp.sum(jnp.where(top_idx[:TOP_K, :, None] == experts, pstart, 0), axis=2)
    dest = (start + rank[:TOP_K]).astype(jnp.int32)
    n_blocks = tk // rb + N_EXPERTS
    first_row = jnp.arange(n_blocks, dtype=jnp.int32) * rb
    block_e = jnp.minimum(jnp.sum(pend[None, :] <= first_row[:, None], axis=1),
                          N_EXPERTS - 1).astype(jnp.int32)
    n_used = (pend[-1] // rb).astype(jnp.int32).reshape(1)
    return dest, block_e, n_used


def _dispatch(x1s, dest, n_rows, sub):
    T = dest.shape[1]
    idx = jnp.concatenate([dest, jnp.zeros((SC_IDX_ROWS - TOP_K, T), jnp.int32)], axis=0)
    xs = _sc_scatter_slabs(x1s.reshape(T, sub, LANES), idx, TOP_K, n_rows)
    return xs.reshape(n_rows * sub, LANES)


def _expert_kernel(be_ref, nu_ref, x_ref, wg_ref, bg_ref, wu_ref, bu_ref, wd_ref, bd_ref,
                   y_ref, wgb, wub, wdb):
    i = pl.program_id(0)
    prev = be_ref[jnp.maximum(i - 1, 0)]

    @pl.when((i == 0) | (be_ref[i] != prev))
    def _():
        wgb[...] = wg_ref[0, 0].astype(BF16)
        wub[...] = wu_ref[0, 0].astype(BF16)
        wdb[...] = wd_ref[0, 0].astype(BF16)

    @pl.when(i < nu_ref[0])
    def _():
        sub = wgb.shape[0] // LANES
        xb = _load_subrows(x_ref, x_ref.shape[0] // sub, sub).astype(BF16)
        gate = jnp.dot(xb, wgb[...], preferred_element_type=F32) + bg_ref[0, 0]
        gate = jnp.minimum(gate, SWIGLU_LIMIT)
        up = jnp.dot(xb, wub[...], preferred_element_type=F32) + bu_ref[0, 0]
        up = jnp.clip(up, -SWIGLU_LIMIT, SWIGLU_LIMIT)
        hid = gate * _sigmoid(SWIGLU_ALPHA * gate) * (up + 1.0)
        y = jnp.dot(hid.astype(BF16), wdb[...], preferred_element_type=F32) + bd_ref[0, 0]
        _store_subrows(y_ref, y)


def _experts(xs, block_e, n_used, layer, w_gate, b_gate, w_up, b_up, w_down, b_down):
    rb = MOE_ROWS
    n_blocks = block_e.shape[0]
    E, D, F = w_gate.shape[1:]
    sub = D // LANES
    wspec = lambda a, b: pl.BlockSpec((1, 1, a, b), lambda i, be, nu: (layer, be[i], 0, 0))
    live = lambda i, be, nu: (jnp.where(i < nu[0], i, n_blocks), 0)
    grid_spec = pltpu.PrefetchScalarGridSpec(
        num_scalar_prefetch=2,
        grid=(n_blocks,),
        in_specs=[pl.BlockSpec((rb * sub, LANES), live),
                  wspec(D, F), wspec(1, F), wspec(D, F), wspec(1, F), wspec(F, D), wspec(1, D)],
        out_specs=pl.BlockSpec((rb * sub, LANES), live),
        scratch_shapes=[pltpu.VMEM((D, F), BF16), pltpu.VMEM((D, F), BF16),
                        pltpu.VMEM((F, D), BF16)],
    )
    depth = w_gate.shape[0]
    return pl.pallas_call(
        _expert_kernel,
        grid_spec=grid_spec,
        out_shape=jax.ShapeDtypeStruct(xs.shape, F32),
        compiler_params=_params("arbitrary"),
        name="experts",
    )(block_e, n_used, xs, w_gate, b_gate.reshape(depth, E, 1, F),
      w_up, b_up.reshape(depth, E, 1, F), w_down, b_down.reshape(depth, E, 1, D))


def _combine_kernel(x1_ref, gate_ref, y_ref, g_ref, b_ref, o_ref, *, alpha):
    tm, D = x1_ref.shape
    sub = D // LANES
    gates = gate_ref[...].T
    f = jnp.zeros(x1_ref.shape, F32)
    for kk in range(TOP_K):
        f = f + gates[:, kk:kk + 1] * _load_subrows(y_ref, tm, sub, kk * sub, TOP_K * sub)
    o_ref[...] = _layer_norm(alpha * x1_ref[...] + f, g_ref[...], b_ref[...])


def _combine(x1, gates, dest, y_rows, ln_g, ln_b, alpha):
    T, D = x1.shape
    sub = D // LANES
    tm = COMBINE_ROWS
    yg = _sc_gather_slabs(y_rows.reshape(-1, sub, LANES), dest.T.reshape(-1)).reshape(-1, LANES)
    row = lambda n: pl.BlockSpec((tm, n), lambda i: (i, 0))
    full = lambda a, b: pl.BlockSpec((a, b), lambda i: (0, 0))
    return pl.pallas_call(
        functools.partial(_combine_kernel, alpha=alpha),
        grid=(T // tm,),
        in_specs=[row(D), pl.BlockSpec((SUBLANES, tm), lambda i: (0, i)),
                  pl.BlockSpec((tm * TOP_K * sub, LANES), lambda i: (i, 0)),
                  full(1, D), full(1, D)],
        out_specs=row(D),
        out_shape=jax.ShapeDtypeStruct((T, D), F32),
        compiler_params=_params("parallel"),
        name="combine",
    )(x1, gates, yg, ln_g.reshape(1, D), ln_b.reshape(1, D))


def kernel(x, mem, w_in_hgrn, hgrn_lb_logits, hgrn_norm_g, w_in_moba, w_mem_kv, w_o,
           ln_mix_g, ln_mix_b, w_router, b_router, w_gate, b_gate, w_up, b_up,
           w_down, b_down, ln_ffn_g, ln_ffn_b):
    B, S, D = x.shape
    T = B * S
    depth = w_o.shape[0]
    alpha = (2 * depth) ** 0.25

    p_lb = jax.nn.softmax(hgrn_lb_logits.astype(F32), axis=0)
    lower_bounds = jnp.cumsum(p_lb, axis=0) - p_lb[0]

    x2 = x.reshape(T, D)
    for layer in range(depth):
        j = layer // 2
        if layer % 2 == 0:
            (h2,) = _inproj(x2, w_in_hgrn[j].astype(BF16), for_moba=False)
            mix = _hgrn_mixer(h2.reshape(B, S, -1), lower_bounds[j], hgrn_norm_g[j])
        else:
            h2, kv2, q6 = _inproj(x2, w_in_moba[j].astype(BF16), for_moba=True)
            mix = _moba_mixer(h2.reshape(B, S, -1), kv2.reshape(B, S, -1), q6)
        kv = _memkv(mem, w_mem_kv[layer])
        x1, x1s, top_idx, gates, rank, counts = _post_mixer(
            x2, mix.reshape(T, MIX_WIDTH), h2, kv, w_o[layer].astype(BF16),
            ln_mix_g[layer], ln_mix_b[layer], w_router[layer], b_router[layer], S, alpha)
        dest, block_e, n_used = _route(top_idx, rank, counts[:, 0].astype(jnp.int32))
        xs = _dispatch(x1s, dest, (block_e.shape[0] + 1) * MOE_ROWS, D // LANES)
        y_rows = _experts(xs, block_e, n_used, layer, w_gate, b_gate, w_up, b_up, w_down, b_down)
        x2 = _combine(x1, gates, dest, y_rows, ln_ffn_g[layer], ln_ffn_b[layer], alpha)
    return x2.reshape(B, S, D)
```

```python
import functools
import math

import jax
import jax.numpy as jnp
from jax import lax
from jax.experimental import pallas as pl
from jax.experimental.pallas import tpu as pltpu
from jax.experimental.pallas import tpu_sc as plsc

MIX_WIDTH = 768
MEM_HEADS = 4
MEM_HEAD_DIM = 64
MEM_WIDTH = MEM_HEADS * MEM_HEAD_DIM
HGRN_HEADS = 6
HGRN_DK = 128
MOBA_HEADS = 12
MOBA_HEAD_DIM = 64
MOBA_BLOCK = 256
MOBA_TOPK = 3
N_EXPERTS = 32
TOP_K = 4
SWIGLU_ALPHA = 1.702
SWIGLU_LIMIT = 7.0
LN_EPS = 1e-5
RMS_EPS = 1e-6

LANES = 128
SUBLANES = 8
VMEM_LIMIT_BYTES = 56 * 1024 * 1024

INPROJ_ROWS = 512
HGRN_CHUNK = 64
HGRN_ROWS = 2048
POST_ROWS = 512
MOBA_TILE = 256
MOBA_TILE_GROUP = 32
MOBA_PLACE_GROUP = 8
MOBA_PICK_GROUP = 16
MOBA_MERGE_GROUP = 16
MOBA_OWN_GROUP = 16
MOE_ROWS = 512
MOE_PARTS = 2
COMBINE_ROWS = 512
SC_WINDOW = 128
SC_IDX_ROWS = 8
SC_CHUNK = 32

BF16 = jnp.bfloat16
F32 = jnp.float32

_NT = (((1,), (1,)), ((), ()))
_TN = (((0,), (0,)), ((), ()))


def _alibi_slope_list(n):
    def pow2(m):
        start = 2.0 ** (-(2.0 ** -(math.log2(m) - 3)))
        return [start ** (i + 1) for i in range(m)]
    if math.log2(n).is_integer():
        return pow2(n)
    c = 2 ** math.floor(math.log2(n))
    return pow2(c) + _alibi_slope_list(2 * c)[0::2][:n - c]


def _sigmoid(x):
    return 1.0 / (1.0 + jnp.exp(-x))


def _params(*sem):
    return pltpu.CompilerParams(dimension_semantics=sem, vmem_limit_bytes=VMEM_LIMIT_BYTES)


def _store_subrows(ref, value):
    sub = value.shape[1] // LANES
    for c in range(sub):
        ref[pl.ds(c, value.shape[0], stride=sub), :] = value[:, c * LANES:(c + 1) * LANES]


def _load_subrows(ref, rows, sub, first=0, stride=None):
    stride = stride or sub
    return jnp.concatenate(
        [ref[pl.ds(first + c, rows, stride=stride), :] for c in range(sub)], axis=1)


def _inproj_kernel(x_ref, w_ref, h_ref, *moba_refs):
    h = jnp.dot(x_ref[...].astype(BF16), w_ref[...], preferred_element_type=F32)
    h_ref[...] = h
    if moba_refs:
        kv_ref, q6_ref = moba_refs
        n_pairs = q6_ref.shape[0]
        for p in range(n_pairs):
            q6_ref[p] = h[:, p * LANES:(p + 1) * LANES] * (MOBA_HEAD_DIM ** -0.5)
            for part in range(2):
                col = (1 + part) * MIX_WIDTH + p * LANES
                kv_ref[:, (2 * p + part) * LANES:(2 * p + part + 1) * LANES] = (
                    h[:, col:col + LANES].astype(BF16))


def _inproj(x2, w_bf16, for_moba):
    T, D = x2.shape
    N = w_bf16.shape[1]
    tm = INPROJ_ROWS
    out_shape = [jax.ShapeDtypeStruct((T, N), F32)]
    out_specs = [pl.BlockSpec((tm, N), lambda i: (i, 0))]
    if for_moba:
        NP = MOBA_HEADS // 2
        out_shape += [jax.ShapeDtypeStruct((T, 2 * MIX_WIDTH), BF16),
                      jax.ShapeDtypeStruct((NP, T, LANES), F32)]
        out_specs += [pl.BlockSpec((tm, 2 * MIX_WIDTH), lambda i: (i, 0)),
                      pl.BlockSpec((NP, tm, LANES), lambda i: (0, i, 0))]
    return pl.pallas_call(
        _inproj_kernel,
        grid=(T // tm,),
        in_specs=[pl.BlockSpec((tm, D), lambda i: (i, 0)),
                  pl.BlockSpec((D, N), lambda i: (0, 0))],
        out_specs=out_specs,
        out_shape=out_shape,
        compiler_params=_params("parallel"),
        name="inproj",
    )(x2, w_bf16)


def _cumsum_rows(x, row):
    n = x.shape[0]
    sh = 1
    while sh < n:
        x = x + jnp.where(row >= sh, pltpu.roll(x, sh, 0), 0.0)
        sh *= 2
    return x


def _bcast_row(a, group, r):
    n = a.shape[0]
    a3 = a.reshape(n // group, group, LANES)
    return jnp.broadcast_to(a3[:, r:r + 1, :], a3.shape).reshape(n, LANES)


def _hgrn_chunk(qr, fr, v, gr, lb, ng, e_sum, st_t):
    C = qr.shape[0]
    row = lax.broadcasted_iota(jnp.int32, (C, LANES), 0)
    rr = lax.broadcasted_iota(jnp.int32, (C, C), 0)
    cc = lax.broadcasted_iota(jnp.int32, (C, C), 1)

    q = qr * _sigmoid(qr)
    forget = lb + (1.0 - lb) * _sigmoid(fr)
    k = 1.0 - forget
    G = _cumsum_rows(jnp.log(forget), row)

    z = jnp.log(k) - G
    parts = []
    for s in range(SUBLANES):
        parts.append((q * jnp.exp(jnp.minimum(G + _bcast_row(z, SUBLANES, s), 0.0))).astype(BF16))
    a_diag = jnp.dot(jnp.concatenate(parts, axis=1), e_sum, preferred_element_type=F32)
    A = jnp.where(((rr >> 3) == (cc >> 3)) & (cc <= rr), a_diag, 0.0)

    m = SUBLANES
    while m < C:
        lg = int(math.log2(m))
        Gr = _bcast_row(G, 2 * m, m - 1)
        second = ((row >> lg) & 1) == 1
        qm = q * jnp.exp(jnp.where(second, G - Gr, -jnp.inf))
        km = k * jnp.exp(jnp.where(second, -jnp.inf, Gr - G))
        am = lax.dot_general(qm.astype(BF16), km.astype(BF16), _NT, preferred_element_type=F32)
        A = A + jnp.where((rr >> (lg + 1)) == (cc >> (lg + 1)), am, 0.0)
        m *= 2

    vb = v.astype(BF16)
    o = jnp.dot(A.astype(BF16), vb, preferred_element_type=F32)
    o = o + lax.dot_general((q * jnp.exp(G)).astype(BF16), st_t.astype(BF16), _NT,
                            preferred_element_type=F32)
    g_end = G[C - 1:C, :]
    kd = (k * jnp.exp(g_end - G)).astype(BF16)
    st_new = st_t * jnp.exp(g_end) + lax.dot_general(vb, kd, _TN, preferred_element_type=F32)

    ms = jnp.mean(o * o, axis=-1, keepdims=True)
    out = o * lax.rsqrt(ms + RMS_EPS) * ng * _sigmoid(gr)
    return out, st_new


def _hgrn_kernel(q_ref, f_ref, i_ref, g_ref, lb_ref, ng_ref, e_ref, o_ref, st_ref, *, chunk):
    @pl.when(pl.program_id(2) == 0)
    def _():
        st_ref[...] = jnp.zeros_like(st_ref)

    lb = lb_ref[0]
    ng = ng_ref[...]
    e_sum = e_ref[...]
    n_chunks = q_ref.shape[1] // chunk
    for c in range(n_chunks):
        sl = pl.ds(c * chunk, chunk)
        out, st_new = _hgrn_chunk(q_ref[0, sl, :], f_ref[0, sl, :], i_ref[0, sl, :],
                                  g_ref[0, sl, :], lb, ng, e_sum, st_ref[...])
        st_ref[...] = st_new
        o_ref[0, sl, :] = out


def _hgrn_mixer(h3, lb, norm_g):
    B, S, _ = h3.shape
    ts = min(HGRN_ROWS, S)
    C = HGRN_CHUNK
    H = HGRN_HEADS
    e_sum = (jnp.arange(SUBLANES * LANES)[:, None] // LANES == jnp.arange(C)[None, :] % SUBLANES
             ).astype(BF16)
    col = lambda off: pl.BlockSpec((1, ts, LANES), lambda b, h, s, off=off: (b, s, off + h))
    return pl.pallas_call(
        functools.partial(_hgrn_kernel, chunk=C),
        grid=(B, H, S // ts),
        in_specs=[col(0), col(H), col(2 * H), col(3 * H),
                  pl.BlockSpec((1, 1, LANES), lambda b, h, s: (h, 0, 0)),
                  pl.BlockSpec((1, LANES), lambda b, h, s: (0, 0)),
                  pl.BlockSpec((SUBLANES * LANES, C), lambda b, h, s: (0, 0))],
        out_specs=pl.BlockSpec((1, ts, LANES), lambda b, h, s: (b, s, h)),
        out_shape=jax.ShapeDtypeStruct((B, S, MIX_WIDTH), F32),
        scratch_shapes=[pltpu.VMEM((HGRN_DK, HGRN_DK), F32)],
        compiler_params=_params("parallel", "parallel", "arbitrary"),
        name="hgrn",
    )(h3, h3, h3, h3, lb.reshape(H, 1, LANES), norm_g.reshape(1, LANES), e_sum)


def _moba_pick_kernel(q_ref, kf_ref, sr_ref, cnt_ref, kmean_ref, *, qblocks):
    BLK = MOBA_BLOCK
    W = qblocks * BLK
    nbp = kmean_ref.shape[0]
    i0 = pl.program_id(2) * qblocks

    @pl.when(pl.program_id(2) == 0)
    def _():
        kmean_ref[...] = jnp.zeros_like(kmean_ref)

    for j in range(qblocks):
        kmean_ref[pl.ds(i0 + j, 1), :] = jnp.mean(kf_ref[0, j * BLK:(j + 1) * BLK, :], axis=0,
                                                  keepdims=True)
    km = kmean_ref[...]
    lane_k = lax.broadcasted_iota(jnp.int32, (nbp, LANES), 1)
    km2 = jnp.concatenate([jnp.where(lane_k < MOBA_HEAD_DIM, km, 0.0),
                           jnp.where(lane_k >= MOBA_HEAD_DIM, km, 0.0)], axis=0)
    gate = lax.dot_general(km2, q_ref[0], _NT, precision=lax.Precision.HIGHEST,
                           preferred_element_type=F32).reshape(2, nbp, W)
    nblk = lax.broadcasted_iota(jnp.int32, (2, nbp, W), 1)
    qi = i0 + (lax.broadcasted_iota(jnp.int32, (2, 1, W), 2) >> int(math.log2(BLK)))
    g = jnp.where(nblk < qi, gate, -jnp.inf)
    picks = []
    for _ in range(MOBA_TOPK):
        mx = jnp.max(g, axis=1, keepdims=True)
        idx = jnp.min(jnp.where(g == mx, nblk, nbp), axis=1, keepdims=True)
        picks.append((idx, (mx > -jnp.inf) & (idx < qi)))
        g = jnp.where(nblk == idx, -jnp.inf, g)
    chosen = jnp.zeros((2, nbp, W), F32)
    for idx, valid in picks:
        chosen = chosen + jnp.where((nblk == idx) & valid, 1.0, 0.0)
    qa = lax.broadcasted_iota(jnp.int32, (BLK, BLK), 0)
    qc = lax.broadcasted_iota(jnp.int32, (BLK, BLK), 1)
    before = jnp.where(qa < qc, 1.0, 0.0).astype(BF16)
    chosen2 = chosen.reshape(2 * nbp, W).astype(BF16)
    earlier = jnp.concatenate(
        [jnp.dot(chosen2[:, j * BLK:(j + 1) * BLK], before, preferred_element_type=F32)
         for j in range(qblocks)], axis=1).reshape(2, nbp, W)
    orow = lax.broadcasted_iota(jnp.int32, (2 * SUBLANES, W), 0)
    out = jnp.zeros((2 * SUBLANES, W), jnp.int32)
    for j, (idx, valid) in enumerate(picks):
        rank = jnp.sum(jnp.where(nblk == idx, earlier, 0.0), axis=1, keepdims=True).astype(jnp.int32)
        sel = jnp.where(valid, idx, -1)
        for hh in range(2):
            rep = hh * MOBA_TOPK + j
            out = jnp.where(orow == rep, sel[hh], out)
            out = jnp.where(orow == SUBLANES + rep, rank[hh], out)
    sr_ref[0, 0] = out
    lane_c = lax.broadcasted_iota(jnp.int32, (nbp, LANES), 1)
    for j in range(qblocks):
        per_block = jnp.sum(chosen[:, :, j * BLK:(j + 1) * BLK], axis=2, keepdims=True)
        cnt_ref[0, 0, j * nbp:(j + 1) * nbp, :] = jnp.where(
            lane_c == 0, per_block[0], jnp.where(lane_c == 1, per_block[1], 0.0))


def _moba_place_kernel(sr_ref, tab_ref, idx_ref, *, group, spare):
    BLK = MOBA_BLOCK
    n_rep = 2 * MOBA_TOPK
    b = pl.program_id(0)
    p = pl.program_id(1)
    ig = pl.program_id(2)
    nrow = lax.broadcasted_iota(jnp.int32, (LANES, BLK), 0)
    orow = lax.broadcasted_iota(jnp.int32, (SC_IDX_ROWS, BLK), 0)
    qpos = lax.broadcasted_iota(jnp.int32, (1, BLK), 1)
    for g in range(group):
        blk = sr_ref[0, 0, :, g * BLK:(g + 1) * BLK]
        tab_t = tab_ref[0, 0, g * SUBLANES:(g + 1) * SUBLANES, :].T
        early = jnp.minimum(ig * group + g, MOBA_TOPK - 1)
        out = jnp.zeros((SC_IDX_ROWS, BLK), jnp.int32)
        for rep in range(n_rep):
            hh = rep // MOBA_TOPK
            sel = blk[rep:rep + 1, :]
            rank = blk[SUBLANES + rep:SUBLANES + rep + 1, :]
            start = jnp.sum(jnp.where(nrow == sel, tab_t[:, hh:hh + 1], 0.0), axis=0, keepdims=True)
            unused = spare + (((b * pl.num_programs(1) + p) * MOBA_TOPK + early) * n_rep + rep) * BLK
            dest = jnp.where(sel >= 0, start.astype(jnp.int32) + rank, unused + qpos)
            out = jnp.where(orow == rep, dest, out)
        idx_ref[:, g * BLK:(g + 1) * BLK] = out


def _moba_tiles_kernel(tb_ref, tp_ref, th_ref, tn_ref, nu_ref, sl_ref, q_ref, *refs, group):
    kv_refs, o_ref = refs[:group], refs[group]
    t = pl.program_id(0)
    tq = q_ref.shape[0] // group

    @pl.when(t * group < nu_ref[0])
    def _():
        lane = lax.broadcasted_iota(jnp.int32, (tq, LANES), 1)
        kpos = lax.broadcasted_iota(jnp.int32, (1, MOBA_BLOCK), 1)
        for g in range(group):
            tt = t * group + g
            hh = th_ref[tt]
            slope = sl_ref[2 * tp_ref[tt] + hh]
            head = (lane >> 6) == hh
            rows = pl.ds(g * tq, tq)
            q = jnp.where(head, q_ref[rows, :], 0.0).astype(BF16)
            s = lax.dot_general(q, kv_refs[g][0, :, :LANES], _NT, preferred_element_type=F32)
            s = s + slope * (kpos + tn_ref[tt] * MOBA_BLOCK).astype(F32)
            m = jnp.max(s, axis=1, keepdims=True)
            pr = jnp.exp(s - m)
            l = jnp.sum(pr, axis=1, keepdims=True)
            o = jnp.dot(pr.astype(BF16), kv_refs[g][0, :, LANES:], preferred_element_type=F32) / l
            o_ref[rows, :] = jnp.where(head, o, m + jnp.log(l))


def _moba_own_kernel(sl_ref, q_ref, kv_ref, o_ref, lse_ref):
    BLK = MOBA_BLOCK
    p = pl.program_id(1)
    n_blocks = q_ref.shape[1] // BLK
    lane = lax.broadcasted_iota(jnp.int32, (BLK, LANES), 1)
    rr = lax.broadcasted_iota(jnp.int32, (BLK, BLK), 0)
    cc = lax.broadcasted_iota(jnp.int32, (BLK, BLK), 1)
    first = lane < MOBA_HEAD_DIM
    for h in range(n_blocks):
        i = pl.program_id(2) * n_blocks + h
        rows = pl.ds(h * BLK, BLK)
        qf = q_ref[0, rows, :]
        k_own = kv_ref[0, rows, :LANES]
        v_own = kv_ref[0, rows, LANES:]
        key_pos = (lax.broadcasted_iota(jnp.int32, (1, BLK), 1) + i * BLK).astype(F32)
        outs, lses = [], []
        for hh in range(2):
            head = (lane >> 6) == hh
            slope = sl_ref[2 * p + hh]
            qh = jnp.where(head, qf * (MOBA_HEAD_DIM ** -0.5), 0.0).astype(BF16)
            s = lax.dot_general(qh, k_own, _NT, preferred_element_type=F32)
            s = jnp.where(cc <= rr, s + slope * key_pos, -jnp.inf)
            m = jnp.max(s, axis=1, keepdims=True)
            pr = jnp.exp(s - m)
            l = jnp.sum(pr, axis=1, keepdims=True)
            lses.append(m + jnp.log(l))
            outs.append(jnp.dot(pr.astype(BF16), v_own, preferred_element_type=F32) / l)
        o_ref[0, rows, :] = jnp.where(first, outs[0], outs[1])
        lse_ref[0, rows, :] = jnp.where(first, lses[0], lses[1])


def _moba_merge_kernel(o_ref, lse_ref, pg_ref, out_ref):
    BLK = MOBA_BLOCK
    first = lax.broadcasted_iota(jnp.int32, (BLK, LANES), 1) < MOBA_HEAD_DIM
    for h in range(o_ref.shape[1] // BLK):
        i = pl.program_id(2) * (o_ref.shape[1] // BLK) + h
        rows = pl.ds(h * BLK, BLK)
        lses = [lse_ref[0, rows, :]]
        vals = [o_ref[0, rows, :]]
        for j in range(MOBA_TOPK):
            p0 = pg_ref[j, 0, rows, :]
            p1 = pg_ref[MOBA_TOPK + j, 0, rows, :]
            has_block = j < i
            stats = pltpu.roll(jnp.where(first, p1, p0), MOBA_HEAD_DIM, 1)
            lses.append(jnp.where(has_block, stats, -jnp.inf))
            vals.append(jnp.where(has_block, jnp.where(first, p0, p1), 0.0))
        top = functools.reduce(jnp.maximum, lses)
        ws = [jnp.exp(x - top) for x in lses]
        out_ref[0, rows, :] = sum(w * v for w, v in zip(ws, vals)) / sum(ws)


def _moba_mixer(h3, kv3, q6):
    B, S, _ = h3.shape
    T = B * S
    BLK = MOBA_BLOCK
    TQ = MOBA_TILE
    G = MOBA_TILE_GROUP
    NB = S // BLK
    GI = math.gcd(NB, MOBA_PLACE_GROUP)
    nbp = -(-NB // SUBLANES) * SUBLANES
    NP = MOBA_HEADS // 2
    n_rep = 2 * MOBA_TOPK
    slopes = jnp.asarray(_alibi_slope_list(MOBA_HEADS), F32)

    QB = math.gcd(NB, MOBA_PICK_GROUP)
    selrank, counts = pl.pallas_call(
        functools.partial(_moba_pick_kernel, qblocks=QB),
        grid=(B, NP, NB // QB),
        in_specs=[pl.BlockSpec((1, QB * BLK, LANES), lambda b, p, i: (b, i, p)),
                  pl.BlockSpec((1, QB * BLK, LANES), lambda b, p, i: (b, i, NP + p))],
        out_specs=[pl.BlockSpec((1, 1, 2 * SUBLANES, QB * BLK), lambda b, p, i: (b, p, 0, i)),
                   pl.BlockSpec((1, 1, QB * nbp, LANES), lambda b, p, i: (b, p, i, 0))],
        out_shape=[jax.ShapeDtypeStruct((B, NP, 2 * SUBLANES, S), jnp.int32),
                   jax.ShapeDtypeStruct((B, NP, NB * nbp, LANES), F32)],
        scratch_shapes=[pltpu.VMEM((nbp, LANES), F32)],
        compiler_params=_params("parallel", "parallel", "arbitrary"),
        name="moba_pick",
    )(h3, h3)

    OB = math.gcd(NB, MOBA_OWN_GROUP)
    blk_spec = lambda w: pl.BlockSpec((1, OB * BLK, w), lambda b, p, i, sl: (b, i, p))
    own_o, own_lse = pl.pallas_call(
        _moba_own_kernel,
        grid_spec=pltpu.PrefetchScalarGridSpec(
            num_scalar_prefetch=1,
            grid=(B, NP, NB // OB),
            in_specs=[blk_spec(LANES), blk_spec(2 * LANES)],
            out_specs=[blk_spec(LANES), blk_spec(LANES)]),
        out_shape=[jax.ShapeDtypeStruct((B, S, MIX_WIDTH), F32)] * 2,
        compiler_params=_params("parallel", "parallel", "parallel"),
        name="moba_own",
    )(slopes, h3, kv3)

    cnt = counts.reshape(B, NP, NB, nbp, LANES)[:, :, :, :NB, :2].astype(jnp.int32)
    cnt = cnt.transpose(0, 1, 2, 4, 3)
    base = jnp.cumsum(cnt, axis=2) - cnt
    total = jnp.sum(cnt, axis=2)
    padded = (total + TQ - 1) // TQ * TQ
    pend = jnp.cumsum(padded.reshape(-1))
    seg_start = (pend - padded.reshape(-1)).reshape(B, NP, 1, 2, NB)
    table = jnp.zeros((B, NP, NB, SUBLANES, LANES), F32).at[:, :, :, :2, :NB].set(
        (seg_start + base).astype(F32)).reshape(B, NP, NB * SUBLANES, LANES)
    n_seg = B * NP * 2 * NB
    max_tiles = -(-((T * NP * n_rep) // TQ + n_seg) // G) * G
    first_row = jnp.arange(max_tiles, dtype=jnp.int32) * TQ
    tile_seg = jnp.minimum(jnp.sum(pend[None, :] <= first_row[:, None], axis=1), n_seg - 1)
    tile_n = (tile_seg % NB).astype(jnp.int32)
    tile_h = ((tile_seg // NB) % 2).astype(jnp.int32)
    tile_p = ((tile_seg // (2 * NB)) % NP).astype(jnp.int32)
    tile_b = (tile_seg // (2 * NB * NP)).astype(jnp.int32)
    n_used = (pend[-1] // TQ).astype(jnp.int32).reshape(1)
    spare = max_tiles * TQ
    n_rows = spare + max(B * NP * MOBA_TOPK * n_rep * BLK, G * TQ)

    idx = pl.pallas_call(
        functools.partial(_moba_place_kernel, group=GI, spare=spare),
        grid=(B, NP, NB // GI),
        in_specs=[pl.BlockSpec((1, 1, 2 * SUBLANES, GI * BLK), lambda b, p, i: (b, p, 0, i)),
                  pl.BlockSpec((1, 1, GI * SUBLANES, LANES), lambda b, p, i: (b, p, i, 0))],
        out_specs=pl.BlockSpec((SC_IDX_ROWS, GI * BLK),
                               lambda b, p, i: (0, p * (T // (GI * BLK)) + b * (NB // GI) + i)),
        out_shape=jax.ShapeDtypeStruct((SC_IDX_ROWS, NP * T), jnp.int32),
        compiler_params=_params("parallel", "parallel", "parallel"),
        name="moba_place",
    )(selrank, table)

    qs = _sc_scatter_rows(q6.reshape(NP * T, LANES), idx, n_rep, n_rows)

    live = lambda t, tb, tp, th, tn, nu, sl: (jnp.where(t * G < nu[0], t, max_tiles // G), 0)
    kv_blk = lambda g: pl.BlockSpec(
        (1, BLK, 2 * LANES),
        lambda t, tb, tp, th, tn, nu, sl: (tb[t * G + g], tn[t * G + g], tp[t * G + g]))
    part = pl.pallas_call(
        functools.partial(_moba_tiles_kernel, group=G),
        grid_spec=pltpu.PrefetchScalarGridSpec(
            num_scalar_prefetch=6,
            grid=(max_tiles // G,),
            in_specs=[pl.BlockSpec((G * TQ, LANES), live)] + [kv_blk(g) for g in range(G)],
            out_specs=pl.BlockSpec((G * TQ, LANES), live)),
        out_shape=jax.ShapeDtypeStruct((n_rows, LANES), F32),
        compiler_params=_params("arbitrary"),
        name="moba_tiles",
    )(tile_b, tile_p, tile_h, tile_n, n_used, slopes, qs, *([kv3] * G))

    pg = _sc_gather_rows(part, idx[:n_rep].reshape(-1)).reshape(n_rep, NP, T, LANES)

    MB = math.gcd(NB, MOBA_MERGE_GROUP)
    blk = pl.BlockSpec((1, MB * BLK, LANES), lambda b, p, i: (b, i, p))
    return pl.pallas_call(
        _moba_merge_kernel,
        grid=(B, NP, NB // MB),
        in_specs=[blk, blk,
                  pl.BlockSpec((n_rep, 1, MB * BLK, LANES),
                               lambda b, p, i: (0, p, b * (NB // MB) + i, 0))],
        out_specs=blk,
        out_shape=jax.ShapeDtypeStruct((B, S, MIX_WIDTH), F32),
        compiler_params=_params("parallel", "parallel", "parallel"),
        name="moba_merge",
    )(own_o, own_lse, pg)


def _memkv_kernel(mem_ref, w_ref, kv_ref):
    kv_ref[0] = jnp.dot(mem_ref[0].astype(BF16), w_ref[...].astype(BF16),
                        preferred_element_type=F32).astype(BF16)


def _memkv(mem, w_kv):
    B, M, D = mem.shape
    N = w_kv.shape[1]
    return pl.pallas_call(
        _memkv_kernel,
        grid=(B,),
        in_specs=[pl.BlockSpec((1, M, D), lambda b: (b, 0, 0)),
                  pl.BlockSpec((D, N), lambda b: (0, 0))],
        out_specs=pl.BlockSpec((1, M, N), lambda b: (b, 0, 0)),
        out_shape=jax.ShapeDtypeStruct((B, M, N), BF16),
        compiler_params=_params("parallel"),
        name="memkv",
    )(mem, w_kv)


def _layer_norm(z, g, b):
    mu = jnp.mean(z, axis=-1, keepdims=True)
    zc = z - mu
    var = jnp.mean(zc * zc, axis=-1, keepdims=True)
    return zc * lax.rsqrt(var + LN_EPS) * g + b


def _post_kernel(x_ref, mix_ref, mq_ref, kv_ref, wo_ref, g_ref, b_ref, wr_ref, br_ref,
                 x1_ref, x1s_ref, idx_ref, gate_ref, rank_ref, cnt_ref, run_ref, *, alpha):
    tm = x_ref.shape[0]
    mq = mq_ref[...]
    kv = kv_ref[0]
    km = kv[:, :MEM_WIDTH]
    vm = kv[:, MEM_WIDTH:]
    lane = lax.broadcasted_iota(jnp.int32, (tm, MEM_WIDTH), 1)
    scale = MEM_HEAD_DIM ** -0.5
    mo = jnp.zeros((tm, MEM_WIDTH), F32)
    for hd in range(MEM_HEADS):
        head = (lane >> 6) == hd
        qh = jnp.where(head, mq * scale, 0.0).astype(BF16)
        s = lax.dot_general(qh, km, _NT, preferred_element_type=F32)
        m = jnp.max(s, axis=1, keepdims=True)
        p = jnp.exp(s - m)
        l = jnp.sum(p, axis=1, keepdims=True)
        oh = jnp.dot(p.astype(BF16), vm, preferred_element_type=F32) / l
        mo = jnp.where(head, oh, mo)

    y = jnp.dot(mix_ref[...].astype(BF16), wo_ref[:MIX_WIDTH, :], preferred_element_type=F32)
    y = y + jnp.dot(mo.astype(BF16), wo_ref[MIX_WIDTH:, :], preferred_element_type=F32)
    x1 = _layer_norm(alpha * x_ref[...] + y, g_ref[...], b_ref[...])
    x1_ref[...] = x1
    _store_subrows(x1s_ref, x1)

    x_hi = x1.astype(BF16)
    x_lo = (x1 - x_hi.astype(F32)).astype(BF16)
    hi = jnp.dot(x_hi, wr_ref[...], preferred_element_type=F32)
    lo = jnp.dot(x_lo, wr_ref[:, :LANES], preferred_element_type=F32)
    logits = hi[:, :LANES] + hi[:, LANES:] + lo
    g = logits.T[:N_EXPERTS] + br_ref[...]
    erow = lax.broadcasted_iota(jnp.int32, (N_EXPERTS, tm), 0)
    orow = lax.broadcasted_iota(jnp.int32, (SUBLANES, tm), 0)
    idx_out = jnp.zeros((SUBLANES, tm), jnp.int32)
    vals, picks = [], []
    chosen = jnp.zeros((N_EXPERTS, tm), F32)
    for kk in range(TOP_K):
        mx = jnp.max(g, axis=0, keepdims=True)
        idx = jnp.min(jnp.where(g == mx, erow, N_EXPERTS), axis=0, keepdims=True)
        idx_out = jnp.where(orow == kk, idx, idx_out)
        vals.append(mx)
        picks.append(idx)
        chosen = chosen + jnp.where(erow == idx, 1.0, 0.0)
        g = jnp.where(erow == idx, -jnp.inf, g)
    evs = [jnp.exp(v - vals[0]) for v in vals]
    den = sum(evs)
    gate_out = jnp.zeros((SUBLANES, tm), F32)
    for kk in range(TOP_K):
        gate_out = jnp.where(orow == kk, evs[kk] / den, gate_out)
    idx_ref[...] = idx_out
    gate_ref[...] = gate_out

    @pl.when(pl.program_id(0) == 0)
    def _():
        run_ref[...] = jnp.zeros_like(run_ref)

    ta = lax.broadcasted_iota(jnp.int32, (tm, tm), 0)
    tb = lax.broadcasted_iota(jnp.int32, (tm, tm), 1)
    before = jnp.where(ta < tb, 1.0, 0.0).astype(BF16)
    earlier = jnp.dot(chosen.astype(BF16), before, preferred_element_type=F32) + run_ref[:, :1]
    rank_out = jnp.zeros((SUBLANES, tm), jnp.int32)
    for kk in range(TOP_K):
        rank = jnp.sum(jnp.where(erow == picks[kk], earlier, 0.0), axis=0, keepdims=True)
        rank_out = jnp.where(orow == kk, rank.astype(jnp.int32), rank_out)
    rank_ref[...] = rank_out
    run_ref[...] = run_ref[...] + jnp.sum(chosen, axis=1, keepdims=True)
    cnt_ref[...] = run_ref[...]


def _post_mixer(x2, mix2, h2, kv, w_o_bf16, ln_g, ln_b, w_router, b_router, seq_len, alpha,
                part, n_parts):
    D = x2.shape[1]
    T = x2.shape[0] // n_parts
    tm = POST_ROWS
    N = h2.shape[1]
    M = kv.shape[1]
    mq_col = (N - MEM_WIDTH) // MEM_WIDTH
    tiles_per_seq = seq_len // tm
    off = part * (T // tm)
    wr = jnp.zeros((D, LANES), F32).at[:, :N_EXPERTS].set(w_router)
    wr_hi = wr.astype(BF16)
    wr = jnp.concatenate([wr_hi, (wr - wr_hi.astype(F32)).astype(BF16)], axis=1)
    br = b_router.reshape(N_EXPERTS, 1)
    row_in = lambda n: pl.BlockSpec((tm, n), lambda i: (i + off, 0))
    row = lambda n: pl.BlockSpec((tm, n), lambda i: (i, 0))
    full = lambda a, b: pl.BlockSpec((a, b), lambda i: (0, 0))
    per_token = pl.BlockSpec((SUBLANES, tm), lambda i: (0, i))
    return pl.pallas_call(
        functools.partial(_post_kernel, alpha=alpha),
        grid=(T // tm,),
        in_specs=[row_in(D), row_in(MIX_WIDTH),
                  pl.BlockSpec((tm, MEM_WIDTH), lambda i: (i + off, mq_col)),
                  pl.BlockSpec((1, M, 2 * MEM_WIDTH), lambda i: ((i + off) // tiles_per_seq, 0, 0)),
                  full(D, D), full(1, D), full(1, D),
                  full(D, 2 * LANES), full(N_EXPERTS, 1)],
        out_specs=[row(D), pl.BlockSpec((tm * (D // LANES), LANES), lambda i: (i, 0)),
                   per_token, per_token, per_token, full(N_EXPERTS, LANES)],
        out_shape=[jax.ShapeDtypeStruct((T, D), F32),
                   jax.ShapeDtypeStruct((T * (D // LANES), LANES), F32),
                   jax.ShapeDtypeStruct((SUBLANES, T), jnp.int32),
                   jax.ShapeDtypeStruct((SUBLANES, T), F32),
                   jax.ShapeDtypeStruct((SUBLANES, T), jnp.int32),
                   jax.ShapeDtypeStruct((N_EXPERTS, LANES), F32)],
        scratch_shapes=[pltpu.VMEM((N_EXPERTS, LANES), F32)],
        compiler_params=_params("arbitrary"),
        name="post_mixer",
    )(x2, mix2, h2, kv, w_o_bf16, ln_g.reshape(1, D), ln_b.reshape(1, D), wr, br)


def _sc_mesh():
    return plsc.VectorSubcoreMesh(core_axis_name="core", subcore_axis_name="subcore")


def _sc_scatter_rows(rows, idx, n_rep, n_out):
    R, W = rows.shape

    @functools.partial(pl.kernel, out_type=jax.ShapeDtypeStruct((n_out, W), rows.dtype),
                       mesh=_sc_mesh(), scratch_types=[])
    def scatter(x_hbm, i_hbm, o_hbm):
        def body(x_vmem, i_vmem):
            for r in range(n_rep):
                pltpu.sync_copy(x_vmem, o_hbm.at[i_vmem.at[r]])

        pltpu.emit_pipeline(
            body, grid=(R // SC_WINDOW,),
            in_specs=[pl.BlockSpec((SC_WINDOW, W), lambda i: (i, 0)),
                      pl.BlockSpec((SC_IDX_ROWS, SC_WINDOW), lambda i: (0, i))],
            out_specs=[], core_axis_name=("core", "subcore"),
            dimension_semantics=(pltpu.PARALLEL,), trace_scopes=False)(x_hbm, i_hbm)

    return scatter(rows, idx)


def _sc_gather_rows(table, idx):
    n = idx.shape[0]
    W = table.shape[1]

    @functools.partial(pl.kernel, out_type=jax.ShapeDtypeStruct((n, W), table.dtype),
                       mesh=_sc_mesh(), scratch_types=[])
    def gather(t_hbm, i_hbm, o_hbm):
        def body(i_vmem, o_vmem):
            pltpu.sync_copy(t_hbm.at[i_vmem.at[0]], o_vmem)

        pltpu.emit_pipeline(
            body, grid=(n // SC_WINDOW,),
            in_specs=[pl.BlockSpec((1, SC_WINDOW), lambda i: (0, i))],
            out_specs=[pl.BlockSpec((SC_WINDOW, W), lambda i: (i, 0))],
            core_axis_name=("core", "subcore"),
            dimension_semantics=(pltpu.PARALLEL,), trace_scopes=False)(i_hbm, o_hbm)

    return gather(table, idx.reshape(1, n))


def _sc_workers():
    info = pltpu.get_tpu_info().sparse_core
    return info.num_cores, info.num_cores * info.num_subcores


def _sc_scatter_slabs(rows, idx, n_rep, n_out):
    R, S, W = rows.shape
    n_cores, n_workers = _sc_workers()
    per_worker = (R // SC_WINDOW) // n_workers
    assert per_worker * n_workers * SC_WINDOW == R

    @functools.partial(pl.kernel, out_type=jax.ShapeDtypeStruct((n_out, S, W), rows.dtype),
                       mesh=_sc_mesh(),
                       scratch_types=[pltpu.VMEM((SC_IDX_ROWS, SC_WINDOW), jnp.int32),
                                      pltpu.VMEM((SC_CHUNK, S, W), rows.dtype)])
    def scatter(x_hbm, i_hbm, o_hbm, ibuf, buf):
        wid = lax.axis_index("subcore") * n_cores + lax.axis_index("core")

        @pl.loop(0, per_worker)
        def _(s):
            first = (wid * per_worker + s) * SC_WINDOW
            pltpu.sync_copy(i_hbm.at[:, pl.ds(first, SC_WINDOW)], ibuf)
            for c in range(SC_WINDOW // SC_CHUNK):
                pltpu.sync_copy(x_hbm.at[pl.ds(first + c * SC_CHUNK, SC_CHUNK)], buf)
                for r in range(n_rep):
                    pltpu.sync_copy(buf, o_hbm.at[ibuf.at[r, pl.ds(c * SC_CHUNK, SC_CHUNK)]])

    return scatter(rows, idx)


def _sc_gather_slabs(table, idx):
    n = idx.shape[0]
    S, W = table.shape[1:]
    n_cores, n_workers = _sc_workers()
    per_worker = (n // SC_WINDOW) // n_workers
    assert per_worker * n_workers * SC_WINDOW == n
    n_chunks = SC_WINDOW // SC_CHUNK

    @functools.partial(pl.kernel, out_type=jax.ShapeDtypeStruct((n, S, W), table.dtype),
                       mesh=_sc_mesh(),
                       scratch_types=[pltpu.VMEM((1, SC_WINDOW), jnp.int32),
                                      pltpu.VMEM((2, SC_CHUNK, S, W), table.dtype),
                                      pltpu.SemaphoreType.DMA((2,)), pltpu.SemaphoreType.DMA((2,))])
    def gather(t_hbm, i_hbm, o_hbm, ibuf, buf, fetch_sem, store_sem):
        wid = lax.axis_index("subcore") * n_cores + lax.axis_index("core")

        @pl.loop(0, per_worker)
        def _(s):
            blk = wid * per_worker + s
            pltpu.sync_copy(i_hbm.at[pl.ds(blk, 1)], ibuf)

            def fetch(c):
                return pltpu.make_async_copy(
                    t_hbm.at[ibuf.at[0, pl.ds(c * SC_CHUNK, SC_CHUNK)]], buf.at[c % 2],
                    fetch_sem.at[c % 2])

            def store(c):
                return pltpu.make_async_copy(
                    buf.at[c % 2], o_hbm.at[pl.ds(blk * SC_WINDOW + c * SC_CHUNK, SC_CHUNK)],
                    store_sem.at[c % 2])

            fetch(0).start()
            for c in range(n_chunks):
                if c + 1 < n_chunks:
                    if c >= 1:
                        store(c - 1).wait()
                    fetch(c + 1).start()
                fetch(c).wait()
                store(c).start()
            store(n_chunks - 2).wait()
            store(n_chunks - 1).wait()

    return gather(table, idx.reshape(n // SC_WINDOW, SC_WINDOW))


def _route(top_idx, rank, counts):
    rb = MOE_ROWS
    n_tokens = top_idx.shape[1]
    tk = n_tokens * TOP_K
    padded = (counts + rb - 1) // rb * rb
    pend = jnp.cumsum(padded)
    pstart = pend - padded
    experts = jnp.arange(N_EXPERTS, dtype=jnp.int32)
    start = jnp.sum(jnp.where(top_idx[:TOP_K, :, None] == experts, pstart, 0), axis=2)
    dest = (start + rank[:TOP_K]).astype(jnp.int32)
    n_blocks = tk // rb + N_EXPERTS
    first_row = jnp.arange(n_blocks, dtype=jnp.int32) * rb
    block_e = jnp.minimum(jnp.sum(pend[None, :] <= first_row[:, None], axis=1),
                          N_EXPERTS - 1).astype(jnp.int32)
    n_used = (pend[-1] // rb).astype(jnp.int32).reshape(1)
    return dest, block_e, n_used


def _dispatch(x1s, dest, n_rows, sub):
    T = dest.shape[1]
    idx = jnp.concatenate([dest, jnp.zeros((SC_IDX_ROWS - TOP_K, T), jnp.int32)], axis=0)
    xs = _sc_scatter_slabs(x1s.reshape(T, sub, LANES), idx, TOP_K, n_rows)
    return xs.reshape(n_rows * sub, LANES)


def _expert_kernel(be_ref, nu_ref, x_ref, wg_ref, bg_ref, wu_ref, bu_ref, wd_ref, bd_ref,
                   y_ref, wgb, wub, wdb):
    i = pl.program_id(0)
    prev = be_ref[jnp.maximum(i - 1, 0)]

    @pl.when((i == 0) | (be_ref[i] != prev))
    def _():
        wgb[...] = wg_ref[0, 0].astype(BF16)
        wub[...] = wu_ref[0, 0].astype(BF16)
        wdb[...] = wd_ref[0, 0].astype(BF16)

    @pl.when(i < nu_ref[0])
    def _():
        sub = wgb.shape[0] // LANES
        xb = _load_subrows(x_ref, x_ref.shape[0] // sub, sub).astype(BF16)
        gate = jnp.dot(xb, wgb[...], preferred_element_type=F32) + bg_ref[0, 0]
        gate = jnp.minimum(gate, SWIGLU_LIMIT)
        up = jnp.dot(xb, wub[...], preferred_element_type=F32) + bu_ref[0, 0]
        up = jnp.clip(up, -SWIGLU_LIMIT, SWIGLU_LIMIT)
        hid = gate * _sigmoid(SWIGLU_ALPHA * gate) * (up + 1.0)
        y = jnp.dot(hid.astype(BF16), wdb[...], preferred_element_type=F32) + bd_ref[0, 0]
        _store_subrows(y_ref, y)


def _experts(xs, block_e, n_used, layer, w_gate, b_gate, w_up, b_up, w_down, b_down):
    rb = MOE_ROWS
    n_blocks = block_e.shape[0]
    E, D, F = w_gate.shape[1:]
    sub = D // LANES
    wspec = lambda a, b: pl.BlockSpec((1, 1, a, b), lambda i, be, nu: (layer, be[i], 0, 0))
    live = lambda i, be, nu: (jnp.where(i < nu[0], i, n_blocks), 0)
    grid_spec = pltpu.PrefetchScalarGridSpec(
        num_scalar_prefetch=2,
        grid=(n_blocks,),
        in_specs=[pl.BlockSpec((rb * sub, LANES), live),
                  wspec(D, F), wspec(1, F), wspec(D, F), wspec(1, F), wspec(F, D), wspec(1, D)],
        out_specs=pl.BlockSpec((rb * sub, LANES), live),
        scratch_shapes=[pltpu.VMEM((D, F), BF16), pltpu.VMEM((D, F), BF16),
                        pltpu.VMEM((F, D), BF16)],
    )
    depth = w_gate.shape[0]
    return pl.pallas_call(
        _expert_kernel,
        grid_spec=grid_spec,
        out_shape=jax.ShapeDtypeStruct(xs.shape, F32),
        compiler_params=_params("arbitrary"),
        name="experts",
    )(block_e, n_used, xs, w_gate, b_gate.reshape(depth, E, 1, F),
      w_up, b_up.reshape(depth, E, 1, F), w_down, b_down.reshape(depth, E, 1, D))


def _combine_kernel(x1_ref, gate_ref, y_ref, g_ref, b_ref, *rest, alpha):
    o_ref = rest[-1]
    tm, D = x1_ref.shape
    sub = D // LANES
    gates = gate_ref[...].T
    f = jnp.zeros(x1_ref.shape, F32)
    for kk in range(TOP_K):
        f = f + gates[:, kk:kk + 1] * _load_subrows(y_ref, tm, sub, kk * sub, TOP_K * sub)
    o_ref[...] = _layer_norm(alpha * x1_ref[...] + f, g_ref[...], b_ref[...])


def _combine(x1, gates, dest, y_rows, ln_g, ln_b, alpha, part, n_parts, earlier):
    T, D = x1.shape
    sub = D // LANES
    tm = COMBINE_ROWS
    off = part * (T // tm)
    yg = _sc_gather_slabs(y_rows.reshape(-1, sub, LANES), dest.T.reshape(-1)).reshape(-1, LANES)
    row = lambda n: pl.BlockSpec((tm, n), lambda i: (i, 0))
    full = lambda a, b: pl.BlockSpec((a, b), lambda i: (0, 0))
    in_specs = [row(D), pl.BlockSpec((SUBLANES, tm), lambda i: (0, i)),
                pl.BlockSpec((tm * TOP_K * sub, LANES), lambda i: (i, 0)),
                full(1, D), full(1, D)]
    operands = [x1, gates, yg, ln_g.reshape(1, D), ln_b.reshape(1, D)]
    aliases = {}
    if earlier is not None:
        in_specs.append(pl.BlockSpec(memory_space=pl.ANY))
        operands.append(earlier)
        aliases = {len(operands) - 1: 0}
    return pl.pallas_call(
        functools.partial(_combine_kernel, alpha=alpha),
        grid=(T // tm,),
        in_specs=in_specs,
        out_specs=pl.BlockSpec((tm, D), lambda i: (i + off, 0)),
        out_shape=jax.ShapeDtypeStruct((n_parts * T, D), F32),
        input_output_aliases=aliases,
        compiler_params=_params("parallel"),
        name="combine",
    )(*operands)


def kernel(x, mem, w_in_hgrn, hgrn_lb_logits, hgrn_norm_g, w_in_moba, w_mem_kv, w_o,
           ln_mix_g, ln_mix_b, w_router, b_router, w_gate, b_gate, w_up, b_up,
           w_down, b_down, ln_ffn_g, ln_ffn_b):
    B, S, D = x.shape
    T = B * S
    depth = w_o.shape[0]
    alpha = (2 * depth) ** 0.25

    p_lb = jax.nn.softmax(hgrn_lb_logits.astype(F32), axis=0)
    lower_bounds = jnp.cumsum(p_lb, axis=0) - p_lb[0]

    x2 = x.reshape(T, D)
    for layer in range(depth):
        j = layer // 2
        if layer % 2 == 0:
            (h2,) = _inproj(x2, w_in_hgrn[j].astype(BF16), for_moba=False)
            mix = _hgrn_mixer(h2.reshape(B, S, -1), lower_bounds[j], hgrn_norm_g[j])
        else:
            h2, kv2, q6 = _inproj(x2, w_in_moba[j].astype(BF16), for_moba=True)
            mix = _moba_mixer(h2.reshape(B, S, -1), kv2.reshape(B, S, -1), q6)
        kv = _memkv(mem, w_mem_kv[layer])
        mix2 = mix.reshape(T, MIX_WIDTH)
        w_out = w_o[layer].astype(BF16)
        x_next = None
        for part in range(MOE_PARTS):
            x1, x1s, top_idx, gates, rank, counts = _post_mixer(
                x2, mix2, h2, kv, w_out, ln_mix_g[layer], ln_mix_b[layer],
                w_router[layer], b_router[layer], S, alpha, part, MOE_PARTS)
            dest, block_e, n_used = _route(top_idx, rank, counts[:, 0].astype(jnp.int32))
            xs = _dispatch(x1s, dest, (block_e.shape[0] + 1) * MOE_ROWS, D // LANES)
            y_rows = _experts(xs, block_e, n_used, layer,
                              w_gate, b_gate, w_up, b_up, w_down, b_down)
            x_next = _combine(x1, gates, dest, y_rows, ln_ffn_g[layer], ln_ffn_b[layer], alpha,
                              part, MOE_PARTS, x_next)
        x2 = x_next
    return x2.reshape(B, S, D)
```

```python
import functools
import math

import jax
import jax.numpy as jnp
from jax import lax
from jax.experimental import pallas as pl
from jax.experimental.pallas import tpu as pltpu
from jax.experimental.pallas import tpu_sc as plsc

MIX_WIDTH = 768
MEM_HEADS = 4
MEM_HEAD_DIM = 64
MEM_WIDTH = MEM_HEADS * MEM_HEAD_DIM
HGRN_HEADS = 6
HGRN_DK = 128
MOBA_HEADS = 12
MOBA_HEAD_DIM = 64
MOBA_BLOCK = 256
MOBA_TOPK = 3
N_EXPERTS = 32
TOP_K = 4
SWIGLU_ALPHA = 1.702
SWIGLU_LIMIT = 7.0
LN_EPS = 1e-5
RMS_EPS = 1e-6

LANES = 128
SUBLANES = 8
VMEM_LIMIT_BYTES = 56 * 1024 * 1024

INPROJ_ROWS = 512
HGRN_CHUNK = 64
HGRN_ROWS = 2048
POST_ROWS = 512
MOBA_TILE = 256
MOBA_TILE_GROUP = 32
MOBA_PLACE_GROUP = 8
MOBA_PICK_GROUP = 16
MOBA_MERGE_GROUP = 16
MOBA_OWN_GROUP = 16
MOE_ROWS = 512
COMBINE_ROWS = 512
SC_WINDOW = 128
SC_IDX_ROWS = 8
SC_CHUNK = 32

BF16 = jnp.bfloat16
F32 = jnp.float32

_NT = (((1,), (1,)), ((), ()))
_TN = (((0,), (0,)), ((), ()))


def _alibi_slope_list(n):
    def pow2(m):
        start = 2.0 ** (-(2.0 ** -(math.log2(m) - 3)))
        return [start ** (i + 1) for i in range(m)]
    if math.log2(n).is_integer():
        return pow2(n)
    c = 2 ** math.floor(math.log2(n))
    return pow2(c) + _alibi_slope_list(2 * c)[0::2][:n - c]


def _sigmoid(x):
    return 1.0 / (1.0 + jnp.exp(-x))


def _params(*sem):
    return pltpu.CompilerParams(dimension_semantics=sem, vmem_limit_bytes=VMEM_LIMIT_BYTES)


def _store_subrows(ref, value):
    sub = value.shape[1] // LANES
    for c in range(sub):
        ref[pl.ds(c, value.shape[0], stride=sub), :] = value[:, c * LANES:(c + 1) * LANES]


def _load_subrows(ref, rows, sub, first=0, stride=None):
    stride = stride or sub
    return jnp.concatenate(
        [ref[pl.ds(first + c, rows, stride=stride), :] for c in range(sub)], axis=1)


def _inproj_kernel(x_ref, w_ref, h_ref, *moba_refs):
    h = jnp.dot(x_ref[...].astype(BF16), w_ref[...], preferred_element_type=F32)
    h_ref[...] = h
    if moba_refs:
        kv_ref, q6_ref = moba_refs
        n_pairs = q6_ref.shape[0]
        for p in range(n_pairs):
            q6_ref[p] = h[:, p * LANES:(p + 1) * LANES] * (MOBA_HEAD_DIM ** -0.5)
            for part in range(2):
                col = (1 + part) * MIX_WIDTH + p * LANES
                kv_ref[:, (2 * p + part) * LANES:(2 * p + part + 1) * LANES] = (
                    h[:, col:col + LANES].astype(BF16))


def _inproj(x2, w_bf16, for_moba, part=0, n_parts=1):
    D = x2.shape[1]
    T = x2.shape[0] // n_parts
    N = w_bf16.shape[1]
    tm = INPROJ_ROWS
    off = part * (T // tm)
    out_shape = [jax.ShapeDtypeStruct((T, N), F32)]
    out_specs = [pl.BlockSpec((tm, N), lambda i: (i, 0))]
    if for_moba:
        NP = MOBA_HEADS // 2
        out_shape += [jax.ShapeDtypeStruct((T, 2 * MIX_WIDTH), BF16),
                      jax.ShapeDtypeStruct((NP, T, LANES), F32)]
        out_specs += [pl.BlockSpec((tm, 2 * MIX_WIDTH), lambda i: (i, 0)),
                      pl.BlockSpec((NP, tm, LANES), lambda i: (0, i, 0))]
    return pl.pallas_call(
        _inproj_kernel,
        grid=(T // tm,),
        in_specs=[pl.BlockSpec((tm, D), lambda i: (i + off, 0)),
                  pl.BlockSpec((D, N), lambda i: (0, 0))],
        out_specs=out_specs,
        out_shape=out_shape,
        compiler_params=_params("parallel"),
        name="inproj",
    )(x2, w_bf16)


def _cumsum_rows(x, row):
    n = x.shape[0]
    sh = 1
    while sh < n:
        x = x + jnp.where(row >= sh, pltpu.roll(x, sh, 0), 0.0)
        sh *= 2
    return x


def _bcast_row(a, group, r):
    n = a.shape[0]
    a3 = a.reshape(n // group, group, LANES)
    return jnp.broadcast_to(a3[:, r:r + 1, :], a3.shape).reshape(n, LANES)


def _hgrn_chunk(qr, fr, v, gr, lb, ng, e_sum, st_t):
    C = qr.shape[0]
    row = lax.broadcasted_iota(jnp.int32, (C, LANES), 0)
    rr = lax.broadcasted_iota(jnp.int32, (C, C), 0)
    cc = lax.broadcasted_iota(jnp.int32, (C, C), 1)

    q = qr * _sigmoid(qr)
    forget = lb + (1.0 - lb) * _sigmoid(fr)
    k = 1.0 - forget
    G = _cumsum_rows(jnp.log(forget), row)

    z = jnp.log(k) - G
    parts = []
    for s in range(SUBLANES):
        parts.append((q * jnp.exp(jnp.minimum(G + _bcast_row(z, SUBLANES, s), 0.0))).astype(BF16))
    a_diag = jnp.dot(jnp.concatenate(parts, axis=1), e_sum, preferred_element_type=F32)
    A = jnp.where(((rr >> 3) == (cc >> 3)) & (cc <= rr), a_diag, 0.0)

    m = SUBLANES
    while m < C:
        lg = int(math.log2(m))
        Gr = _bcast_row(G, 2 * m, m - 1)
        second = ((row >> lg) & 1) == 1
        qm = q * jnp.exp(jnp.where(second, G - Gr, -jnp.inf))
        km = k * jnp.exp(jnp.where(second, -jnp.inf, Gr - G))
        am = lax.dot_general(qm.astype(BF16), km.astype(BF16), _NT, preferred_element_type=F32)
        A = A + jnp.where((rr >> (lg + 1)) == (cc >> (lg + 1)), am, 0.0)
        m *= 2

    vb = v.astype(BF16)
    o = jnp.dot(A.astype(BF16), vb, preferred_element_type=F32)
    o = o + lax.dot_general((q * jnp.exp(G)).astype(BF16), st_t.astype(BF16), _NT,
                            preferred_element_type=F32)
    g_end = G[C - 1:C, :]
    kd = (k * jnp.exp(g_end - G)).astype(BF16)
    st_new = st_t * jnp.exp(g_end) + lax.dot_general(vb, kd, _TN, preferred_element_type=F32)

    ms = jnp.mean(o * o, axis=-1, keepdims=True)
    out = o * lax.rsqrt(ms + RMS_EPS) * ng * _sigmoid(gr)
    return out, st_new


def _hgrn_kernel(q_ref, f_ref, i_ref, g_ref, lb_ref, ng_ref, e_ref, o_ref, st_ref, *, chunk):
    @pl.when(pl.program_id(2) == 0)
    def _():
        st_ref[...] = jnp.zeros_like(st_ref)

    lb = lb_ref[0]
    ng = ng_ref[...]
    e_sum = e_ref[...]
    n_chunks = q_ref.shape[1] // chunk
    for c in range(n_chunks):
        sl = pl.ds(c * chunk, chunk)
        out, st_new = _hgrn_chunk(q_ref[0, sl, :], f_ref[0, sl, :], i_ref[0, sl, :],
                                  g_ref[0, sl, :], lb, ng, e_sum, st_ref[...])
        st_ref[...] = st_new
        o_ref[0, sl, :] = out


def _hgrn_mixer(h3, lb, norm_g):
    B, S, _ = h3.shape
    ts = min(HGRN_ROWS, S)
    C = HGRN_CHUNK
    H = HGRN_HEADS
    e_sum = (jnp.arange(SUBLANES * LANES)[:, None] // LANES == jnp.arange(C)[None, :] % SUBLANES
             ).astype(BF16)
    col = lambda off: pl.BlockSpec((1, ts, LANES), lambda b, h, s, off=off: (b, s, off + h))
    return pl.pallas_call(
        functools.partial(_hgrn_kernel, chunk=C),
        grid=(B, H, S // ts),
        in_specs=[col(0), col(H), col(2 * H), col(3 * H),
                  pl.BlockSpec((1, 1, LANES), lambda b, h, s: (h, 0, 0)),
                  pl.BlockSpec((1, LANES), lambda b, h, s: (0, 0)),
                  pl.BlockSpec((SUBLANES * LANES, C), lambda b, h, s: (0, 0))],
        out_specs=pl.BlockSpec((1, ts, LANES), lambda b, h, s: (b, s, h)),
        out_shape=jax.ShapeDtypeStruct((B, S, MIX_WIDTH), F32),
        scratch_shapes=[pltpu.VMEM((HGRN_DK, HGRN_DK), F32)],
        compiler_params=_params("parallel", "parallel", "arbitrary"),
        name="hgrn",
    )(h3, h3, h3, h3, lb.reshape(H, 1, LANES), norm_g.reshape(1, LANES), e_sum)


def _moba_pick_kernel(q_ref, kf_ref, sr_ref, cnt_ref, kmean_ref, *, qblocks):
    BLK = MOBA_BLOCK
    W = qblocks * BLK
    nbp = kmean_ref.shape[0]
    i0 = pl.program_id(2) * qblocks

    @pl.when(pl.program_id(2) == 0)
    def _():
        kmean_ref[...] = jnp.zeros_like(kmean_ref)

    for j in range(qblocks):
        kmean_ref[pl.ds(i0 + j, 1), :] = jnp.mean(kf_ref[0, j * BLK:(j + 1) * BLK, :], axis=0,
                                                  keepdims=True)
    km = kmean_ref[...]
    lane_k = lax.broadcasted_iota(jnp.int32, (nbp, LANES), 1)
    km2 = jnp.concatenate([jnp.where(lane_k < MOBA_HEAD_DIM, km, 0.0),
                           jnp.where(lane_k >= MOBA_HEAD_DIM, km, 0.0)], axis=0)
    gate = lax.dot_general(km2, q_ref[0], _NT, precision=lax.Precision.HIGHEST,
                           preferred_element_type=F32).reshape(2, nbp, W)
    nblk = lax.broadcasted_iota(jnp.int32, (2, nbp, W), 1)
    qi = i0 + (lax.broadcasted_iota(jnp.int32, (2, 1, W), 2) >> int(math.log2(BLK)))
    g = jnp.where(nblk < qi, gate, -jnp.inf)
    picks = []
    for _ in range(MOBA_TOPK):
        mx = jnp.max(g, axis=1, keepdims=True)
        idx = jnp.min(jnp.where(g == mx, nblk, nbp), axis=1, keepdims=True)
        picks.append((idx, (mx > -jnp.inf) & (idx < qi)))
        g = jnp.where(nblk == idx, -jnp.inf, g)
    chosen = jnp.zeros((2, nbp, W), F32)
    for idx, valid in picks:
        chosen = chosen + jnp.where((nblk == idx) & valid, 1.0, 0.0)
    qa = lax.broadcasted_iota(jnp.int32, (BLK, BLK), 0)
    qc = lax.broadcasted_iota(jnp.int32, (BLK, BLK), 1)
    before = jnp.where(qa < qc, 1.0, 0.0).astype(BF16)
    chosen2 = chosen.reshape(2 * nbp, W).astype(BF16)
    earlier = jnp.concatenate(
        [jnp.dot(chosen2[:, j * BLK:(j + 1) * BLK], before, preferred_element_type=F32)
         for j in range(qblocks)], axis=1).reshape(2, nbp, W)
    orow = lax.broadcasted_iota(jnp.int32, (2 * SUBLANES, W), 0)
    out = jnp.zeros((2 * SUBLANES, W), jnp.int32)
    for j, (idx, valid) in enumerate(picks):
        rank = jnp.sum(jnp.where(nblk == idx, earlier, 0.0), axis=1, keepdims=True).astype(jnp.int32)
        sel = jnp.where(valid, idx, -1)
        for hh in range(2):
            rep = hh * MOBA_TOPK + j
            out = jnp.where(orow == rep, sel[hh], out)
            out = jnp.where(orow == SUBLANES + rep, rank[hh], out)
    sr_ref[0, 0] = out
    lane_c = lax.broadcasted_iota(jnp.int32, (nbp, LANES), 1)
    for j in range(qblocks):
        per_block = jnp.sum(chosen[:, :, j * BLK:(j + 1) * BLK], axis=2, keepdims=True)
        cnt_ref[0, 0, j * nbp:(j + 1) * nbp, :] = jnp.where(
            lane_c == 0, per_block[0], jnp.where(lane_c == 1, per_block[1], 0.0))


def _moba_place_kernel(sr_ref, tab_ref, idx_ref, *, group, spare):
    BLK = MOBA_BLOCK
    n_rep = 2 * MOBA_TOPK
    b = pl.program_id(0)
    p = pl.program_id(1)
    ig = pl.program_id(2)
    nrow = lax.broadcasted_iota(jnp.int32, (LANES, BLK), 0)
    orow = lax.broadcasted_iota(jnp.int32, (SC_IDX_ROWS, BLK), 0)
    qpos = lax.broadcasted_iota(jnp.int32, (1, BLK), 1)
    for g in range(group):
        blk = sr_ref[0, 0, :, g * BLK:(g + 1) * BLK]
        tab_t = tab_ref[0, 0, g * SUBLANES:(g + 1) * SUBLANES, :].T
        early = jnp.minimum(ig * group + g, MOBA_TOPK - 1)
        out = jnp.zeros((SC_IDX_ROWS, BLK), jnp.int32)
        for rep in range(n_rep):
            hh = rep // MOBA_TOPK
            sel = blk[rep:rep + 1, :]
            rank = blk[SUBLANES + rep:SUBLANES + rep + 1, :]
            start = jnp.sum(jnp.where(nrow == sel, tab_t[:, hh:hh + 1], 0.0), axis=0, keepdims=True)
            unused = spare + (((b * pl.num_programs(1) + p) * MOBA_TOPK + early) * n_rep + rep) * BLK
            dest = jnp.where(sel >= 0, start.astype(jnp.int32) + rank, unused + qpos)
            out = jnp.where(orow == rep, dest, out)
        idx_ref[:, g * BLK:(g + 1) * BLK] = out


def _moba_tiles_kernel(tb_ref, tp_ref, th_ref, tn_ref, nu_ref, sl_ref, q_ref, *refs, group):
    kv_refs, o_ref = refs[:group], refs[group]
    t = pl.program_id(0)
    tq = q_ref.shape[0] // group

    @pl.when(t * group < nu_ref[0])
    def _():
        lane = lax.broadcasted_iota(jnp.int32, (tq, LANES), 1)
        kpos = lax.broadcasted_iota(jnp.int32, (1, MOBA_BLOCK), 1)
        for g in range(group):
            tt = t * group + g
            hh = th_ref[tt]
            slope = sl_ref[2 * tp_ref[tt] + hh]
            head = (lane >> 6) == hh
            rows = pl.ds(g * tq, tq)
            q = jnp.where(head, q_ref[rows, :], 0.0).astype(BF16)
            s = lax.dot_general(q, kv_refs[g][0, :, :LANES], _NT, preferred_element_type=F32)
            s = s + slope * (kpos + tn_ref[tt] * MOBA_BLOCK).astype(F32)
            m = jnp.max(s, axis=1, keepdims=True)
            pr = jnp.exp(s - m)
            l = jnp.sum(pr, axis=1, keepdims=True)
            o = jnp.dot(pr.astype(BF16), kv_refs[g][0, :, LANES:], preferred_element_type=F32) / l
            o_ref[rows, :] = jnp.where(head, o, m + jnp.log(l))


def _moba_own_kernel(sl_ref, q_ref, kv_ref, o_ref, lse_ref):
    BLK = MOBA_BLOCK
    p = pl.program_id(1)
    n_blocks = q_ref.shape[1] // BLK
    lane = lax.broadcasted_iota(jnp.int32, (BLK, LANES), 1)
    rr = lax.broadcasted_iota(jnp.int32, (BLK, BLK), 0)
    cc = lax.broadcasted_iota(jnp.int32, (BLK, BLK), 1)
    first = lane < MOBA_HEAD_DIM
    for h in range(n_blocks):
        i = pl.program_id(2) * n_blocks + h
        rows = pl.ds(h * BLK, BLK)
        qf = q_ref[0, rows, :]
        k_own = kv_ref[0, rows, :LANES]
        v_own = kv_ref[0, rows, LANES:]
        key_pos = (lax.broadcasted_iota(jnp.int32, (1, BLK), 1) + i * BLK).astype(F32)
        outs, lses = [], []
        for hh in range(2):
            head = (lane >> 6) == hh
            slope = sl_ref[2 * p + hh]
            qh = jnp.where(head, qf * (MOBA_HEAD_DIM ** -0.5), 0.0).astype(BF16)
            s = lax.dot_general(qh, k_own, _NT, preferred_element_type=F32)
            s = jnp.where(cc <= rr, s + slope * key_pos, -jnp.inf)
            m = jnp.max(s, axis=1, keepdims=True)
            pr = jnp.exp(s - m)
            l = jnp.sum(pr, axis=1, keepdims=True)
            lses.append(m + jnp.log(l))
            outs.append(jnp.dot(pr.astype(BF16), v_own, preferred_element_type=F32) / l)
        o_ref[0, rows, :] = jnp.where(first, outs[0], outs[1])
        lse_ref[0, rows, :] = jnp.where(first, lses[0], lses[1])


def _moba_merge_kernel(o_ref, lse_ref, pg_ref, out_ref):
    BLK = MOBA_BLOCK
    first = lax.broadcasted_iota(jnp.int32, (BLK, LANES), 1) < MOBA_HEAD_DIM
    for h in range(o_ref.shape[1] // BLK):
        i = pl.program_id(2) * (o_ref.shape[1] // BLK) + h
        rows = pl.ds(h * BLK, BLK)
        lses = [lse_ref[0, rows, :]]
        vals = [o_ref[0, rows, :]]
        for j in range(MOBA_TOPK):
            p0 = pg_ref[j, 0, rows, :]
            p1 = pg_ref[MOBA_TOPK + j, 0, rows, :]
            has_block = j < i
            stats = pltpu.roll(jnp.where(first, p1, p0), MOBA_HEAD_DIM, 1)
            lses.append(jnp.where(has_block, stats, -jnp.inf))
            vals.append(jnp.where(has_block, jnp.where(first, p0, p1), 0.0))
        top = functools.reduce(jnp.maximum, lses)
        ws = [jnp.exp(x - top) for x in lses]
        out_ref[0, rows, :] = sum(w * v for w, v in zip(ws, vals)) / sum(ws)


def _moba_mixer(h3, kv3, q6):
    B, S, _ = h3.shape
    T = B * S
    BLK = MOBA_BLOCK
    TQ = MOBA_TILE
    G = MOBA_TILE_GROUP
    NB = S // BLK
    GI = math.gcd(NB, MOBA_PLACE_GROUP)
    nbp = -(-NB // SUBLANES) * SUBLANES
    NP = MOBA_HEADS // 2
    n_rep = 2 * MOBA_TOPK
    slopes = jnp.asarray(_alibi_slope_list(MOBA_HEADS), F32)

    QB = math.gcd(NB, MOBA_PICK_GROUP)
    selrank, counts = pl.pallas_call(
        functools.partial(_moba_pick_kernel, qblocks=QB),
        grid=(B, NP, NB // QB),
        in_specs=[pl.BlockSpec((1, QB * BLK, LANES), lambda b, p, i: (b, i, p)),
                  pl.BlockSpec((1, QB * BLK, LANES), lambda b, p, i: (b, i, NP + p))],
        out_specs=[pl.BlockSpec((1, 1, 2 * SUBLANES, QB * BLK), lambda b, p, i: (b, p, 0, i)),
                   pl.BlockSpec((1, 1, QB * nbp, LANES), lambda b, p, i: (b, p, i, 0))],
        out_shape=[jax.ShapeDtypeStruct((B, NP, 2 * SUBLANES, S), jnp.int32),
                   jax.ShapeDtypeStruct((B, NP, NB * nbp, LANES), F32)],
        scratch_shapes=[pltpu.VMEM((nbp, LANES), F32)],
        compiler_params=_params("parallel", "parallel", "arbitrary"),
        name="moba_pick",
    )(h3, h3)

    OB = math.gcd(NB, MOBA_OWN_GROUP)
    blk_spec = lambda w: pl.BlockSpec((1, OB * BLK, w), lambda b, p, i, sl: (b, i, p))
    own_o, own_lse = pl.pallas_call(
        _moba_own_kernel,
        grid_spec=pltpu.PrefetchScalarGridSpec(
            num_scalar_prefetch=1,
            grid=(B, NP, NB // OB),
            in_specs=[blk_spec(LANES), blk_spec(2 * LANES)],
            out_specs=[blk_spec(LANES), blk_spec(LANES)]),
        out_shape=[jax.ShapeDtypeStruct((B, S, MIX_WIDTH), F32)] * 2,
        compiler_params=_params("parallel", "parallel", "parallel"),
        name="moba_own",
    )(slopes, h3, kv3)

    cnt = counts.reshape(B, NP, NB, nbp, LANES)[:, :, :, :NB, :2].astype(jnp.int32)
    cnt = cnt.transpose(0, 1, 2, 4, 3)
    base = jnp.cumsum(cnt, axis=2) - cnt
    total = jnp.sum(cnt, axis=2)
    padded = (total + TQ - 1) // TQ * TQ
    pend = jnp.cumsum(padded.reshape(-1))
    seg_start = (pend - padded.reshape(-1)).reshape(B, NP, 1, 2, NB)
    table = jnp.zeros((B, NP, NB, SUBLANES, LANES), F32).at[:, :, :, :2, :NB].set(
        (seg_start + base).astype(F32)).reshape(B, NP, NB * SUBLANES, LANES)
    n_seg = B * NP * 2 * NB
    max_tiles = -(-((T * NP * n_rep) // TQ + n_seg) // G) * G
    first_row = jnp.arange(max_tiles, dtype=jnp.int32) * TQ
    tile_seg = jnp.minimum(jnp.sum(pend[None, :] <= first_row[:, None], axis=1), n_seg - 1)
    tile_n = (tile_seg % NB).astype(jnp.int32)
    tile_h = ((tile_seg // NB) % 2).astype(jnp.int32)
    tile_p = ((tile_seg // (2 * NB)) % NP).astype(jnp.int32)
    tile_b = (tile_seg // (2 * NB * NP)).astype(jnp.int32)
    n_used = (pend[-1] // TQ).astype(jnp.int32).reshape(1)
    spare = max_tiles * TQ
    n_rows = spare + max(B * NP * MOBA_TOPK * n_rep * BLK, G * TQ)

    idx = pl.pallas_call(
        functools.partial(_moba_place_kernel, group=GI, spare=spare),
        grid=(B, NP, NB // GI),
        in_specs=[pl.BlockSpec((1, 1, 2 * SUBLANES, GI * BLK), lambda b, p, i: (b, p, 0, i)),
                  pl.BlockSpec((1, 1, GI * SUBLANES, LANES), lambda b, p, i: (b, p, i, 0))],
        out_specs=pl.BlockSpec((SC_IDX_ROWS, GI * BLK),
                               lambda b, p, i: (0, p * (T // (GI * BLK)) + b * (NB // GI) + i)),
        out_shape=jax.ShapeDtypeStruct((SC_IDX_ROWS, NP * T), jnp.int32),
        compiler_params=_params("parallel", "parallel", "parallel"),
        name="moba_place",
    )(selrank, table)

    qs = _sc_scatter_rows(q6.reshape(NP * T, LANES), idx, n_rep, n_rows)

    live = lambda t, tb, tp, th, tn, nu, sl: (jnp.where(t * G < nu[0], t, max_tiles // G), 0)
    kv_blk = lambda g: pl.BlockSpec(
        (1, BLK, 2 * LANES),
        lambda t, tb, tp, th, tn, nu, sl: (tb[t * G + g], tn[t * G + g], tp[t * G + g]))
    part = pl.pallas_call(
        functools.partial(_moba_tiles_kernel, group=G),
        grid_spec=pltpu.PrefetchScalarGridSpec(
            num_scalar_prefetch=6,
            grid=(max_tiles // G,),
            in_specs=[pl.BlockSpec((G * TQ, LANES), live)] + [kv_blk(g) for g in range(G)],
            out_specs=pl.BlockSpec((G * TQ, LANES), live)),
        out_shape=jax.ShapeDtypeStruct((n_rows, LANES), F32),
        compiler_params=_params("arbitrary"),
        name="moba_tiles",
    )(tile_b, tile_p, tile_h, tile_n, n_used, slopes, qs, *([kv3] * G))

    pg = _sc_gather_rows(part, idx[:n_rep].reshape(-1)).reshape(n_rep, NP, T, LANES)

    MB = math.gcd(NB, MOBA_MERGE_GROUP)
    blk = pl.BlockSpec((1, MB * BLK, LANES), lambda b, p, i: (b, i, p))
    return pl.pallas_call(
        _moba_merge_kernel,
        grid=(B, NP, NB // MB),
        in_specs=[blk, blk,
                  pl.BlockSpec((n_rep, 1, MB * BLK, LANES),
                               lambda b, p, i: (0, p, b * (NB // MB) + i, 0))],
        out_specs=blk,
        out_shape=jax.ShapeDtypeStruct((B, S, MIX_WIDTH), F32),
        compiler_params=_params("parallel", "parallel", "parallel"),
        name="moba_merge",
    )(own_o, own_lse, pg)


def _memkv_kernel(mem_ref, w_ref, kv_ref):
    kv_ref[0] = jnp.dot(mem_ref[0].astype(BF16), w_ref[...].astype(BF16),
                        preferred_element_type=F32).astype(BF16)


def _memkv(mem, w_kv):
    B, M, D = mem.shape
    N = w_kv.shape[1]
    return pl.pallas_call(
        _memkv_kernel,
        grid=(B,),
        in_specs=[pl.BlockSpec((1, M, D), lambda b: (b, 0, 0)),
                  pl.BlockSpec((D, N), lambda b: (0, 0))],
        out_specs=pl.BlockSpec((1, M, N), lambda b: (b, 0, 0)),
        out_shape=jax.ShapeDtypeStruct((B, M, N), BF16),
        compiler_params=_params("parallel"),
        name="memkv",
    )(mem, w_kv)


def _layer_norm(z, g, b):
    mu = jnp.mean(z, axis=-1, keepdims=True)
    zc = z - mu
    var = jnp.mean(zc * zc, axis=-1, keepdims=True)
    return zc * lax.rsqrt(var + LN_EPS) * g + b


def _post_kernel(x_ref, mix_ref, mq_ref, kv_ref, wo_ref, g_ref, b_ref, wr_ref, br_ref,
                 x1_ref, x1s_ref, idx_ref, gate_ref, rank_ref, cnt_ref, run_ref, *, alpha):
    tm = x_ref.shape[0]
    mq = mq_ref[...]
    kv = kv_ref[0]
    km = kv[:, :MEM_WIDTH]
    vm = kv[:, MEM_WIDTH:]
    lane = lax.broadcasted_iota(jnp.int32, (tm, MEM_WIDTH), 1)
    scale = MEM_HEAD_DIM ** -0.5
    mo = jnp.zeros((tm, MEM_WIDTH), F32)
    for hd in range(MEM_HEADS):
        head = (lane >> 6) == hd
        qh = jnp.where(head, mq * scale, 0.0).astype(BF16)
        s = lax.dot_general(qh, km, _NT, preferred_element_type=F32)
        m = jnp.max(s, axis=1, keepdims=True)
        p = jnp.exp(s - m)
        l = jnp.sum(p, axis=1, keepdims=True)
        oh = jnp.dot(p.astype(BF16), vm, preferred_element_type=F32) / l
        mo = jnp.where(head, oh, mo)

    y = jnp.dot(mix_ref[...].astype(BF16), wo_ref[:MIX_WIDTH, :], preferred_element_type=F32)
    y = y + jnp.dot(mo.astype(BF16), wo_ref[MIX_WIDTH:, :], preferred_element_type=F32)
    x1 = _layer_norm(alpha * x_ref[...] + y, g_ref[...], b_ref[...])
    x1_ref[...] = x1
    _store_subrows(x1s_ref, x1)

    x_hi = x1.astype(BF16)
    x_lo = (x1 - x_hi.astype(F32)).astype(BF16)
    hi = jnp.dot(x_hi, wr_ref[...], preferred_element_type=F32)
    lo = jnp.dot(x_lo, wr_ref[:, :LANES], preferred_element_type=F32)
    logits = hi[:, :LANES] + hi[:, LANES:] + lo
    g = logits.T[:N_EXPERTS] + br_ref[...]
    erow = lax.broadcasted_iota(jnp.int32, (N_EXPERTS, tm), 0)
    orow = lax.broadcasted_iota(jnp.int32, (SUBLANES, tm), 0)
    idx_out = jnp.zeros((SUBLANES, tm), jnp.int32)
    vals, picks = [], []
    chosen = jnp.zeros((N_EXPERTS, tm), F32)
    for kk in range(TOP_K):
        mx = jnp.max(g, axis=0, keepdims=True)
        idx = jnp.min(jnp.where(g == mx, erow, N_EXPERTS), axis=0, keepdims=True)
        idx_out = jnp.where(orow == kk, idx, idx_out)
        vals.append(mx)
        picks.append(idx)
        chosen = chosen + jnp.where(erow == idx, 1.0, 0.0)
        g = jnp.where(erow == idx, -jnp.inf, g)
    evs = [jnp.exp(v - vals[0]) for v in vals]
    den = sum(evs)
    gate_out = jnp.zeros((SUBLANES, tm), F32)
    for kk in range(TOP_K):
        gate_out = jnp.where(orow == kk, evs[kk] / den, gate_out)
    idx_ref[...] = idx_out
    gate_ref[...] = gate_out

    @pl.when(pl.program_id(0) == 0)
    def _():
        run_ref[...] = jnp.zeros_like(run_ref)

    ta = lax.broadcasted_iota(jnp.int32, (tm, tm), 0)
    tb = lax.broadcasted_iota(jnp.int32, (tm, tm), 1)
    before = jnp.where(ta < tb, 1.0, 0.0).astype(BF16)
    earlier = jnp.dot(chosen.astype(BF16), before, preferred_element_type=F32) + run_ref[:, :1]
    rank_out = jnp.zeros((SUBLANES, tm), jnp.int32)
    for kk in range(TOP_K):
        rank = jnp.sum(jnp.where(erow == picks[kk], earlier, 0.0), axis=0, keepdims=True)
        rank_out = jnp.where(orow == kk, rank.astype(jnp.int32), rank_out)
    rank_ref[...] = rank_out
    run_ref[...] = run_ref[...] + jnp.sum(chosen, axis=1, keepdims=True)
    cnt_ref[...] = run_ref[...]


def _post_mixer(x2, mix2, h2, kv, w_o_bf16, ln_g, ln_b, w_router, b_router, seq_len, alpha,
                part, n_parts, local_mix):
    D = x2.shape[1]
    T = x2.shape[0] // n_parts
    tm = POST_ROWS
    N = h2.shape[1]
    M = kv.shape[1]
    mq_col = (N - MEM_WIDTH) // MEM_WIDTH
    tiles_per_seq = seq_len // tm
    off = part * (T // tm)
    mix_off = 0 if local_mix else off
    wr = jnp.zeros((D, LANES), F32).at[:, :N_EXPERTS].set(w_router)
    wr_hi = wr.astype(BF16)
    wr = jnp.concatenate([wr_hi, (wr - wr_hi.astype(F32)).astype(BF16)], axis=1)
    br = b_router.reshape(N_EXPERTS, 1)
    row_in = lambda n: pl.BlockSpec((tm, n), lambda i: (i + off, 0))
    row = lambda n: pl.BlockSpec((tm, n), lambda i: (i, 0))
    full = lambda a, b: pl.BlockSpec((a, b), lambda i: (0, 0))
    per_token = pl.BlockSpec((SUBLANES, tm), lambda i: (0, i))
    return pl.pallas_call(
        functools.partial(_post_kernel, alpha=alpha),
        grid=(T // tm,),
        in_specs=[row_in(D), pl.BlockSpec((tm, MIX_WIDTH), lambda i: (i + mix_off, 0)),
                  pl.BlockSpec((tm, MEM_WIDTH), lambda i: (i + mix_off, mq_col)),
                  pl.BlockSpec((1, M, 2 * MEM_WIDTH), lambda i: ((i + off) // tiles_per_seq, 0, 0)),
                  full(D, D), full(1, D), full(1, D),
                  full(D, 2 * LANES), full(N_EXPERTS, 1)],
        out_specs=[row(D), pl.BlockSpec((tm * (D // LANES), LANES), lambda i: (i, 0)),
                   per_token, per_token, per_token, full(N_EXPERTS, LANES)],
        out_shape=[jax.ShapeDtypeStruct((T, D), F32),
                   jax.ShapeDtypeStruct((T * (D // LANES), LANES), F32),
                   jax.ShapeDtypeStruct((SUBLANES, T), jnp.int32),
                   jax.ShapeDtypeStruct((SUBLANES, T), F32),
                   jax.ShapeDtypeStruct((SUBLANES, T), jnp.int32),
                   jax.ShapeDtypeStruct((N_EXPERTS, LANES), F32)],
        scratch_shapes=[pltpu.VMEM((N_EXPERTS, LANES), F32)],
        compiler_params=_params("arbitrary"),
        name="post_mixer",
    )(x2, mix2, h2, kv, w_o_bf16, ln_g.reshape(1, D), ln_b.reshape(1, D), wr, br)


def _sc_mesh():
    return plsc.VectorSubcoreMesh(core_axis_name="core", subcore_axis_name="subcore")


def _sc_scatter_rows(rows, idx, n_rep, n_out):
    R, W = rows.shape

    @functools.partial(pl.kernel, out_type=jax.ShapeDtypeStruct((n_out, W), rows.dtype),
                       mesh=_sc_mesh(), scratch_types=[])
    def scatter(x_hbm, i_hbm, o_hbm):
        def body(x_vmem, i_vmem):
            for r in range(n_rep):
                pltpu.sync_copy(x_vmem, o_hbm.at[i_vmem.at[r]])

        pltpu.emit_pipeline(
            body, grid=(R // SC_WINDOW,),
            in_specs=[pl.BlockSpec((SC_WINDOW, W), lambda i: (i, 0)),
                      pl.BlockSpec((SC_IDX_ROWS, SC_WINDOW), lambda i: (0, i))],
            out_specs=[], core_axis_name=("core", "subcore"),
            dimension_semantics=(pltpu.PARALLEL,), trace_scopes=False)(x_hbm, i_hbm)

    return scatter(rows, idx)


def _sc_gather_rows(table, idx):
    n = idx.shape[0]
    W = table.shape[1]

    @functools.partial(pl.kernel, out_type=jax.ShapeDtypeStruct((n, W), table.dtype),
                       mesh=_sc_mesh(), scratch_types=[])
    def gather(t_hbm, i_hbm, o_hbm):
        def body(i_vmem, o_vmem):
            pltpu.sync_copy(t_hbm.at[i_vmem.at[0]], o_vmem)

        pltpu.emit_pipeline(
            body, grid=(n // SC_WINDOW,),
            in_specs=[pl.BlockSpec((1, SC_WINDOW), lambda i: (0, i))],
            out_specs=[pl.BlockSpec((SC_WINDOW, W), lambda i: (i, 0))],
            core_axis_name=("core", "subcore"),
            dimension_semantics=(pltpu.PARALLEL,), trace_scopes=False)(i_hbm, o_hbm)

    return gather(table, idx.reshape(1, n))


def _sc_workers():
    info = pltpu.get_tpu_info().sparse_core
    return info.num_cores, info.num_cores * info.num_subcores


def _sc_scatter_slabs(rows, idx, n_rep, n_out):
    R, S, W = rows.shape
    n_cores, n_workers = _sc_workers()
    per_worker = (R // SC_WINDOW) // n_workers
    assert per_worker * n_workers * SC_WINDOW == R

    @functools.partial(pl.kernel, out_type=jax.ShapeDtypeStruct((n_out, S, W), rows.dtype),
                       mesh=_sc_mesh(),
                       scratch_types=[pltpu.VMEM((SC_IDX_ROWS, SC_WINDOW), jnp.int32),
                                      pltpu.VMEM((SC_CHUNK, S, W), rows.dtype)])
    def scatter(x_hbm, i_hbm, o_hbm, ibuf, buf):
        wid = lax.axis_index("subcore") * n_cores + lax.axis_index("core")

        @pl.loop(0, per_worker)
        def _(s):
            first = (wid * per_worker + s) * SC_WINDOW
            pltpu.sync_copy(i_hbm.at[:, pl.ds(first, SC_WINDOW)], ibuf)
            for c in range(SC_WINDOW // SC_CHUNK):
                pltpu.sync_copy(x_hbm.at[pl.ds(first + c * SC_CHUNK, SC_CHUNK)], buf)
                for r in range(n_rep):
                    pltpu.sync_copy(buf, o_hbm.at[ibuf.at[r, pl.ds(c * SC_CHUNK, SC_CHUNK)]])

    return scatter(rows, idx)


def _sc_gather_slabs(table, idx):
    n = idx.shape[0]
    S, W = table.shape[1:]
    n_cores, n_workers = _sc_workers()
    per_worker = (n // SC_WINDOW) // n_workers
    assert per_worker * n_workers * SC_WINDOW == n
    n_chunks = SC_WINDOW // SC_CHUNK

    @functools.partial(pl.kernel, out_type=jax.ShapeDtypeStruct((n, S, W), table.dtype),
                       mesh=_sc_mesh(),
                       scratch_types=[pltpu.VMEM((1, SC_WINDOW), jnp.int32),
                                      pltpu.VMEM((2, SC_CHUNK, S, W), table.dtype),
                                      pltpu.SemaphoreType.DMA((2,)), pltpu.SemaphoreType.DMA((2,))])
    def gather(t_hbm, i_hbm, o_hbm, ibuf, buf, fetch_sem, store_sem):
        wid = lax.axis_index("subcore") * n_cores + lax.axis_index("core")

        @pl.loop(0, per_worker)
        def _(s):
            blk = wid * per_worker + s
            pltpu.sync_copy(i_hbm.at[pl.ds(blk, 1)], ibuf)

            def fetch(c):
                return pltpu.make_async_copy(
                    t_hbm.at[ibuf.at[0, pl.ds(c * SC_CHUNK, SC_CHUNK)]], buf.at[c % 2],
                    fetch_sem.at[c % 2])

            def store(c):
                return pltpu.make_async_copy(
                    buf.at[c % 2], o_hbm.at[pl.ds(blk * SC_WINDOW + c * SC_CHUNK, SC_CHUNK)],
                    store_sem.at[c % 2])

            fetch(0).start()
            for c in range(n_chunks):
                if c + 1 < n_chunks:
                    if c >= 1:
                        store(c - 1).wait()
                    fetch(c + 1).start()
                fetch(c).wait()
                store(c).start()
            store(n_chunks - 2).wait()
            store(n_chunks - 1).wait()

    return gather(table, idx.reshape(n // SC_WINDOW, SC_WINDOW))


def _route(top_idx, rank, counts):
    rb = MOE_ROWS
    n_tokens = top_idx.shape[1]
    tk = n_tokens * TOP_K
    padded = (counts + rb - 1) // rb * rb
    pend = jnp.cumsum(padded)
    pstart = pend - padded
    experts = jnp.arange(N_EXPERTS, dtype=jnp.int32)
    start = jnp.sum(jnp.where(top_idx[:TOP_K, :, None] == experts, pstart, 0), axis=2)
    dest = (start + rank[:TOP_K]).astype(jnp.int32)
    n_blocks = tk // rb + N_EXPERTS
    first_row = jnp.arange(n_blocks, dtype=jnp.int32) * rb
    block_e = jnp.minimum(jnp.sum(pend[None, :] <= first_row[:, None], axis=1),
                          N_EXPERTS - 1).astype(jnp.int32)
    n_used = (pend[-1] // rb).astype(jnp.int32).reshape(1)
    return dest, block_e, n_used


def _dispatch(x1s, dest, n_rows, sub):
    T = dest.shape[1]
    idx = jnp.concatenate([dest, jnp.zeros((SC_IDX_ROWS - TOP_K, T), jnp.int32)], axis=0)
    xs = _sc_scatter_slabs(x1s.reshape(T, sub, LANES), idx, TOP_K, n_rows)
    return xs.reshape(n_rows * sub, LANES)


def _expert_kernel(be_ref, nu_ref, x_ref, wg_ref, bg_ref, wu_ref, bu_ref, wd_ref, bd_ref,
                   y_ref, wgb, wub, wdb):
    i = pl.program_id(0)
    prev = be_ref[jnp.maximum(i - 1, 0)]

    @pl.when((i == 0) | (be_ref[i] != prev))
    def _():
        wgb[...] = wg_ref[0, 0].astype(BF16)
        wub[...] = wu_ref[0, 0].astype(BF16)
        wdb[...] = wd_ref[0, 0].astype(BF16)

    @pl.when(i < nu_ref[0])
    def _():
        sub = wgb.shape[0] // LANES
        xb = _load_subrows(x_ref, x_ref.shape[0] // sub, sub).astype(BF16)
        gate = jnp.dot(xb, wgb[...], preferred_element_type=F32) + bg_ref[0, 0]
        gate = jnp.minimum(gate, SWIGLU_LIMIT)
        up = jnp.dot(xb, wub[...], preferred_element_type=F32) + bu_ref[0, 0]
        up = jnp.clip(up, -SWIGLU_LIMIT, SWIGLU_LIMIT)
        hid = gate * _sigmoid(SWIGLU_ALPHA * gate) * (up + 1.0)
        y = jnp.dot(hid.astype(BF16), wdb[...], preferred_element_type=F32) + bd_ref[0, 0]
        _store_subrows(y_ref, y)


def _experts(xs, block_e, n_used, layer, w_gate, b_gate, w_up, b_up, w_down, b_down):
    rb = MOE_ROWS
    n_blocks = block_e.shape[0]
    E, D, F = w_gate.shape[1:]
    sub = D // LANES
    wspec = lambda a, b: pl.BlockSpec((1, 1, a, b), lambda i, be, nu: (layer, be[i], 0, 0))
    live = lambda i, be, nu: (jnp.where(i < nu[0], i, n_blocks), 0)
    grid_spec = pltpu.PrefetchScalarGridSpec(
        num_scalar_prefetch=2,
        grid=(n_blocks,),
        in_specs=[pl.BlockSpec((rb * sub, LANES), live),
                  wspec(D, F), wspec(1, F), wspec(D, F), wspec(1, F), wspec(F, D), wspec(1, D)],
        out_specs=pl.BlockSpec((rb * sub, LANES), live),
        scratch_shapes=[pltpu.VMEM((D, F), BF16), pltpu.VMEM((D, F), BF16),
                        pltpu.VMEM((F, D), BF16)],
    )
    depth = w_gate.shape[0]
    return pl.pallas_call(
        _expert_kernel,
        grid_spec=grid_spec,
        out_shape=jax.ShapeDtypeStruct(xs.shape, F32),
        compiler_params=_params("arbitrary"),
        name="experts",
    )(block_e, n_used, xs, w_gate, b_gate.reshape(depth, E, 1, F),
      w_up, b_up.reshape(depth, E, 1, F), w_down, b_down.reshape(depth, E, 1, D))


def _combine_kernel(x1_ref, gate_ref, y_ref, g_ref, b_ref, *rest, alpha):
    o_ref = rest[-1]
    tm, D = x1_ref.shape
    sub = D // LANES
    gates = gate_ref[...].T
    f = jnp.zeros(x1_ref.shape, F32)
    for kk in range(TOP_K):
        f = f + gates[:, kk:kk + 1] * _load_subrows(y_ref, tm, sub, kk * sub, TOP_K * sub)
    o_ref[...] = _layer_norm(alpha * x1_ref[...] + f, g_ref[...], b_ref[...])


def _combine(x1, gates, dest, y_rows, ln_g, ln_b, alpha, part, n_parts, earlier):
    T, D = x1.shape
    sub = D // LANES
    tm = COMBINE_ROWS
    off = part * (T // tm)
    yg = _sc_gather_slabs(y_rows.reshape(-1, sub, LANES), dest.T.reshape(-1)).reshape(-1, LANES)
    row = lambda n: pl.BlockSpec((tm, n), lambda i: (i, 0))
    full = lambda a, b: pl.BlockSpec((a, b), lambda i: (0, 0))
    in_specs = [row(D), pl.BlockSpec((SUBLANES, tm), lambda i: (0, i)),
                pl.BlockSpec((tm * TOP_K * sub, LANES), lambda i: (i, 0)),
                full(1, D), full(1, D)]
    operands = [x1, gates, yg, ln_g.reshape(1, D), ln_b.reshape(1, D)]
    aliases = {}
    if earlier is not None:
        in_specs.append(pl.BlockSpec(memory_space=pl.ANY))
        operands.append(earlier)
        aliases = {len(operands) - 1: 0}
    return pl.pallas_call(
        functools.partial(_combine_kernel, alpha=alpha),
        grid=(T // tm,),
        in_specs=in_specs,
        out_specs=pl.BlockSpec((tm, D), lambda i: (i + off, 0)),
        out_shape=jax.ShapeDtypeStruct((n_parts * T, D), F32),
        input_output_aliases=aliases,
        compiler_params=_params("parallel"),
        name="combine",
    )(*operands)


def kernel(x, mem, w_in_hgrn, hgrn_lb_logits, hgrn_norm_g, w_in_moba, w_mem_kv, w_o,
           ln_mix_g, ln_mix_b, w_router, b_router, w_gate, b_gate, w_up, b_up,
           w_down, b_down, ln_ffn_g, ln_ffn_b):
    B, S, D = x.shape
    T = B * S
    depth = w_o.shape[0]
    alpha = (2 * depth) ** 0.25

    p_lb = jax.nn.softmax(hgrn_lb_logits.astype(F32), axis=0)
    lower_bounds = jnp.cumsum(p_lb, axis=0) - p_lb[0]

    x2 = x.reshape(T, D)
    for layer in range(depth):
        j = layer // 2
        if layer % 2 == 0:
            (h2,) = _inproj(x2, w_in_hgrn[j].astype(BF16), for_moba=False)
            mix2 = _hgrn_mixer(h2.reshape(B, S, -1), lower_bounds[j], hgrn_norm_g[j])
            mixed = [(mix2.reshape(T, MIX_WIDTH), h2)] * B
        else:
            w_in = w_in_moba[j].astype(BF16)
            mixed = []
            for b in range(B):
                h2, kv2, q6 = _inproj(x2, w_in, for_moba=True, part=b, n_parts=B)
                mix2 = _moba_mixer(h2.reshape(1, S, -1), kv2.reshape(1, S, -1), q6)
                mixed.append((mix2.reshape(S, MIX_WIDTH), h2))
        kv = _memkv(mem, w_mem_kv[layer])
        w_out = w_o[layer].astype(BF16)
        x_next = None
        for part in range(B):
            mix2, h2 = mixed[part]
            x1, x1s, top_idx, gates, rank, counts = _post_mixer(
                x2, mix2, h2, kv, w_out, ln_mix_g[layer], ln_mix_b[layer],
                w_router[layer], b_router[layer], S, alpha, part, B, layer % 2 == 1)
            dest, block_e, n_used = _route(top_idx, rank, counts[:, 0].astype(jnp.int32))
            xs = _dispatch(x1s, dest, (block_e.shape[0] + 1) * MOE_ROWS, D // LANES)
            y_rows = _experts(xs, block_e, n_used, layer,
                              w_gate, b_gate, w_up, b_up, w_down, b_down)
            x_next = _combine(x1, gates, dest, y_rows, ln_ffn_g[layer], ln_ffn_b[layer], alpha,
                              part, B, x_next)
        x2 = x_next
    return x2.reshape(B, S, D)
```

```python
import functools
import math

import jax
import jax.numpy as jnp
from jax import lax
from jax.experimental import pallas as pl
from jax.experimental.pallas import tpu as pltpu
from jax.experimental.pallas import tpu_sc as plsc

MIX_WIDTH = 768
MEM_HEADS = 4
MEM_HEAD_DIM = 64
MEM_WIDTH = MEM_HEADS * MEM_HEAD_DIM
HGRN_HEADS = 6
HGRN_DK = 128
MOBA_HEADS = 12
MOBA_HEAD_DIM = 64
MOBA_BLOCK = 256
MOBA_TOPK = 3
N_EXPERTS = 32
TOP_K = 4
SWIGLU_ALPHA = 1.702
SWIGLU_LIMIT = 7.0
LN_EPS = 1e-5
RMS_EPS = 1e-6

LANES = 128
SUBLANES = 8
VMEM_LIMIT_BYTES = 56 * 1024 * 1024

INPROJ_ROWS = 512
HGRN_CHUNK = 64
HGRN_ROWS = 2048
POST_ROWS = 512
MOBA_TILE = 256
MOBA_TILE_GROUP = 32
MOBA_PLACE_GROUP = 8
MOBA_PICK_GROUP = 16
MOBA_MERGE_GROUP = 16
MOBA_OWN_GROUP = 16
MOE_ROWS = 512
COMBINE_ROWS = 512
SC_WINDOW = 128
SC_IDX_ROWS = 8
SC_CHUNK = 32

BF16 = jnp.bfloat16
F32 = jnp.float32

_NT = (((1,), (1,)), ((), ()))
_TN = (((0,), (0,)), ((), ()))


def _alibi_slope_list(n):
    def pow2(m):
        start = 2.0 ** (-(2.0 ** -(math.log2(m) - 3)))
        return [start ** (i + 1) for i in range(m)]
    if math.log2(n).is_integer():
        return pow2(n)
    c = 2 ** math.floor(math.log2(n))
    return pow2(c) + _alibi_slope_list(2 * c)[0::2][:n - c]


def _sigmoid(x):
    return 1.0 / (1.0 + jnp.exp(-x))


def _params(*sem):
    return pltpu.CompilerParams(dimension_semantics=sem, vmem_limit_bytes=VMEM_LIMIT_BYTES)


def _store_subrows(ref, value):
    sub = value.shape[1] // LANES
    for c in range(sub):
        ref[pl.ds(c, value.shape[0], stride=sub), :] = value[:, c * LANES:(c + 1) * LANES]


def _load_subrows(ref, rows, sub, first=0, stride=None):
    stride = stride or sub
    return jnp.concatenate(
        [ref[pl.ds(first + c, rows, stride=stride), :] for c in range(sub)], axis=1)


def _inproj_kernel(x_ref, w_ref, h_ref, *moba_refs):
    h = jnp.dot(x_ref[...].astype(BF16), w_ref[...], preferred_element_type=F32)
    if not moba_refs:
        h_ref[...] = h
    else:
        h_ref[:, :2 * MIX_WIDTH] = h[:, :2 * MIX_WIDTH]
        h_ref[:, 2 * MIX_WIDTH:] = h[:, 3 * MIX_WIDTH:]
        kv_ref, q6_ref = moba_refs
        n_pairs = q6_ref.shape[0]
        for p in range(n_pairs):
            q6_ref[p] = h[:, p * LANES:(p + 1) * LANES] * (MOBA_HEAD_DIM ** -0.5)
            for part in range(2):
                col = (1 + part) * MIX_WIDTH + p * LANES
                kv_ref[:, (2 * p + part) * LANES:(2 * p + part + 1) * LANES] = (
                    h[:, col:col + LANES].astype(BF16))


def _inproj(x2, w_bf16, for_moba, part=0, n_parts=1):
    D = x2.shape[1]
    T = x2.shape[0] // n_parts
    N = w_bf16.shape[1]
    tm = INPROJ_ROWS
    off = part * (T // tm)
    n_f32 = N - MIX_WIDTH if for_moba else N
    out_shape = [jax.ShapeDtypeStruct((T, n_f32), F32)]
    out_specs = [pl.BlockSpec((tm, n_f32), lambda i: (i, 0))]
    if for_moba:
        NP = MOBA_HEADS // 2
        out_shape += [jax.ShapeDtypeStruct((T, 2 * MIX_WIDTH), BF16),
                      jax.ShapeDtypeStruct((NP, T, LANES), F32)]
        out_specs += [pl.BlockSpec((tm, 2 * MIX_WIDTH), lambda i: (i, 0)),
                      pl.BlockSpec((NP, tm, LANES), lambda i: (0, i, 0))]
    return pl.pallas_call(
        _inproj_kernel,
        grid=(T // tm,),
        in_specs=[pl.BlockSpec((tm, D), lambda i: (i + off, 0)),
                  pl.BlockSpec((D, N), lambda i: (0, 0))],
        out_specs=out_specs,
        out_shape=out_shape,
        compiler_params=_params("parallel"),
        name="inproj",
    )(x2, w_bf16)


def _cumsum_rows(x, row):
    n = x.shape[0]
    sh = 1
    while sh < n:
        x = x + jnp.where(row >= sh, pltpu.roll(x, sh, 0), 0.0)
        sh *= 2
    return x


def _bcast_row(a, group, r):
    n = a.shape[0]
    a3 = a.reshape(n // group, group, LANES)
    return jnp.broadcast_to(a3[:, r:r + 1, :], a3.shape).reshape(n, LANES)


def _hgrn_chunk(qr, fr, v, gr, lb, ng, e_sum, st_t):
    C = qr.shape[0]
    row = lax.broadcasted_iota(jnp.int32, (C, LANES), 0)
    rr = lax.broadcasted_iota(jnp.int32, (C, C), 0)
    cc = lax.broadcasted_iota(jnp.int32, (C, C), 1)

    q = qr * _sigmoid(qr)
    forget = lb + (1.0 - lb) * _sigmoid(fr)
    k = 1.0 - forget
    G = _cumsum_rows(jnp.log(forget), row)

    z = jnp.log(k) - G
    parts = []
    for s in range(SUBLANES):
        parts.append((q * jnp.exp(jnp.minimum(G + _bcast_row(z, SUBLANES, s), 0.0))).astype(BF16))
    a_diag = jnp.dot(jnp.concatenate(parts, axis=1), e_sum, preferred_element_type=F32)
    A = jnp.where(((rr >> 3) == (cc >> 3)) & (cc <= rr), a_diag, 0.0)

    m = SUBLANES
    while m < C:
        lg = int(math.log2(m))
        Gr = _bcast_row(G, 2 * m, m - 1)
        second = ((row >> lg) & 1) == 1
        qm = q * jnp.exp(jnp.where(second, G - Gr, -jnp.inf))
        km = k * jnp.exp(jnp.where(second, -jnp.inf, Gr - G))
        am = lax.dot_general(qm.astype(BF16), km.astype(BF16), _NT, preferred_element_type=F32)
        A = A + jnp.where((rr >> (lg + 1)) == (cc >> (lg + 1)), am, 0.0)
        m *= 2

    vb = v.astype(BF16)
    o = jnp.dot(A.astype(BF16), vb, preferred_element_type=F32)
    o = o + lax.dot_general((q * jnp.exp(G)).astype(BF16), st_t.astype(BF16), _NT,
                            preferred_element_type=F32)
    g_end = G[C - 1:C, :]
    kd = (k * jnp.exp(g_end - G)).astype(BF16)
    st_new = st_t * jnp.exp(g_end) + lax.dot_general(vb, kd, _TN, preferred_element_type=F32)

    ms = jnp.mean(o * o, axis=-1, keepdims=True)
    out = o * lax.rsqrt(ms + RMS_EPS) * ng * _sigmoid(gr)
    return out, st_new


def _hgrn_kernel(q_ref, f_ref, i_ref, g_ref, lb_ref, ng_ref, e_ref, o_ref, st_ref, *, chunk):
    @pl.when(pl.program_id(2) == 0)
    def _():
        st_ref[...] = jnp.zeros_like(st_ref)

    lb = lb_ref[0]
    ng = ng_ref[...]
    e_sum = e_ref[...]
    n_chunks = q_ref.shape[1] // chunk
    for c in range(n_chunks):
        sl = pl.ds(c * chunk, chunk)
        out, st_new = _hgrn_chunk(q_ref[0, sl, :], f_ref[0, sl, :], i_ref[0, sl, :],
                                  g_ref[0, sl, :], lb, ng, e_sum, st_ref[...])
        st_ref[...] = st_new
        o_ref[0, sl, :] = out


def _hgrn_mixer(h3, lb, norm_g):
    B, S, _ = h3.shape
    ts = min(HGRN_ROWS, S)
    C = HGRN_CHUNK
    H = HGRN_HEADS
    e_sum = (jnp.arange(SUBLANES * LANES)[:, None] // LANES == jnp.arange(C)[None, :] % SUBLANES
             ).astype(BF16)
    col = lambda off: pl.BlockSpec((1, ts, LANES), lambda b, h, s, off=off: (b, s, off + h))
    return pl.pallas_call(
        functools.partial(_hgrn_kernel, chunk=C),
        grid=(B, H, S // ts),
        in_specs=[col(0), col(H), col(2 * H), col(3 * H),
                  pl.BlockSpec((1, 1, LANES), lambda b, h, s: (h, 0, 0)),
                  pl.BlockSpec((1, LANES), lambda b, h, s: (0, 0)),
                  pl.BlockSpec((SUBLANES * LANES, C), lambda b, h, s: (0, 0))],
        out_specs=pl.BlockSpec((1, ts, LANES), lambda b, h, s: (b, s, h)),
        out_shape=jax.ShapeDtypeStruct((B, S, MIX_WIDTH), F32),
        scratch_shapes=[pltpu.VMEM((HGRN_DK, HGRN_DK), F32)],
        compiler_params=_params("parallel", "parallel", "arbitrary"),
        name="hgrn",
    )(h3, h3, h3, h3, lb.reshape(H, 1, LANES), norm_g.reshape(1, LANES), e_sum)


def _moba_pick_kernel(q_ref, kf_ref, sr_ref, cnt_ref, kmean_ref, *, qblocks):
    BLK = MOBA_BLOCK
    W = qblocks * BLK
    nbp = kmean_ref.shape[0]
    i0 = pl.program_id(2) * qblocks

    @pl.when(pl.program_id(2) == 0)
    def _():
        kmean_ref[...] = jnp.zeros_like(kmean_ref)

    for j in range(qblocks):
        kmean_ref[pl.ds(i0 + j, 1), :] = jnp.mean(kf_ref[0, j * BLK:(j + 1) * BLK, :], axis=0,
                                                  keepdims=True)
    km = kmean_ref[...]
    lane_k = lax.broadcasted_iota(jnp.int32, (nbp, LANES), 1)
    km2 = jnp.concatenate([jnp.where(lane_k < MOBA_HEAD_DIM, km, 0.0),
                           jnp.where(lane_k >= MOBA_HEAD_DIM, km, 0.0)], axis=0)
    gate = lax.dot_general(km2, q_ref[0], _NT, precision=lax.Precision.HIGHEST,
                           preferred_element_type=F32).reshape(2, nbp, W)
    nblk = lax.broadcasted_iota(jnp.int32, (2, nbp, W), 1)
    qi = i0 + (lax.broadcasted_iota(jnp.int32, (2, 1, W), 2) >> int(math.log2(BLK)))
    g = jnp.where(nblk < qi, gate, -jnp.inf)
    picks = []
    for _ in range(MOBA_TOPK):
        mx = jnp.max(g, axis=1, keepdims=True)
        idx = jnp.min(jnp.where(g == mx, nblk, nbp), axis=1, keepdims=True)
        picks.append((idx, (mx > -jnp.inf) & (idx < qi)))
        g = jnp.where(nblk == idx, -jnp.inf, g)
    chosen = jnp.zeros((2, nbp, W), F32)
    for idx, valid in picks:
        chosen = chosen + jnp.where((nblk == idx) & valid, 1.0, 0.0)
    qa = lax.broadcasted_iota(jnp.int32, (BLK, BLK), 0)
    qc = lax.broadcasted_iota(jnp.int32, (BLK, BLK), 1)
    before = jnp.where(qa < qc, 1.0, 0.0).astype(BF16)
    chosen2 = chosen.reshape(2 * nbp, W).astype(BF16)
    earlier = jnp.concatenate(
        [jnp.dot(chosen2[:, j * BLK:(j + 1) * BLK], before, preferred_element_type=F32)
         for j in range(qblocks)], axis=1).reshape(2, nbp, W)
    orow = lax.broadcasted_iota(jnp.int32, (2 * SUBLANES, W), 0)
    out = jnp.zeros((2 * SUBLANES, W), jnp.int32)
    for j, (idx, valid) in enumerate(picks):
        rank = jnp.sum(jnp.where(nblk == idx, earlier, 0.0), axis=1, keepdims=True).astype(jnp.int32)
        sel = jnp.where(valid, idx, -1)
        for hh in range(2):
            rep = hh * MOBA_TOPK + j
            out = jnp.where(orow == rep, sel[hh], out)
            out = jnp.where(orow == SUBLANES + rep, rank[hh], out)
    sr_ref[0, 0] = out
    lane_c = lax.broadcasted_iota(jnp.int32, (nbp, LANES), 1)
    for j in range(qblocks):
        per_block = jnp.sum(chosen[:, :, j * BLK:(j + 1) * BLK], axis=2, keepdims=True)
        cnt_ref[0, 0, j * nbp:(j + 1) * nbp, :] = jnp.where(
            lane_c == 0, per_block[0], jnp.where(lane_c == 1, per_block[1], 0.0))


def _moba_place_kernel(sr_ref, tab_ref, idx_ref, *, group, spare):
    BLK = MOBA_BLOCK
    n_rep = 2 * MOBA_TOPK
    b = pl.program_id(0)
    p = pl.program_id(1)
    ig = pl.program_id(2)
    nrow = lax.broadcasted_iota(jnp.int32, (LANES, BLK), 0)
    orow = lax.broadcasted_iota(jnp.int32, (SC_IDX_ROWS, BLK), 0)
    qpos = lax.broadcasted_iota(jnp.int32, (1, BLK), 1)
    for g in range(group):
        blk = sr_ref[0, 0, :, g * BLK:(g + 1) * BLK]
        tab_t = tab_ref[0, 0, g * SUBLANES:(g + 1) * SUBLANES, :].T
        early = jnp.minimum(ig * group + g, MOBA_TOPK - 1)
        out = jnp.zeros((SC_IDX_ROWS, BLK), jnp.int32)
        for rep in range(n_rep):
            hh = rep // MOBA_TOPK
            sel = blk[rep:rep + 1, :]
            rank = blk[SUBLANES + rep:SUBLANES + rep + 1, :]
            start = jnp.sum(jnp.where(nrow == sel, tab_t[:, hh:hh + 1], 0.0), axis=0, keepdims=True)
            unused = spare + (((b * pl.num_programs(1) + p) * MOBA_TOPK + early) * n_rep + rep) * BLK
            dest = jnp.where(sel >= 0, start.astype(jnp.int32) + rank, unused + qpos)
            out = jnp.where(orow == rep, dest, out)
        idx_ref[:, g * BLK:(g + 1) * BLK] = out


def _moba_tiles_kernel(tb_ref, tp_ref, th_ref, tn_ref, nu_ref, sl_ref, q_ref, *refs, group):
    kv_refs, o_ref = refs[:group], refs[group]
    t = pl.program_id(0)
    tq = q_ref.shape[0] // group

    @pl.when(t * group < nu_ref[0])
    def _():
        lane = lax.broadcasted_iota(jnp.int32, (tq, LANES), 1)
        kpos = lax.broadcasted_iota(jnp.int32, (1, MOBA_BLOCK), 1)
        for g in range(group):
            tt = t * group + g
            hh = th_ref[tt]
            slope = sl_ref[2 * tp_ref[tt] + hh]
            head = (lane >> 6) == hh
            rows = pl.ds(g * tq, tq)
            q = jnp.where(head, q_ref[rows, :], 0.0).astype(BF16)
            s = lax.dot_general(q, kv_refs[g][0, :, :LANES], _NT, preferred_element_type=F32)
            s = s + slope * (kpos + tn_ref[tt] * MOBA_BLOCK).astype(F32)
            m = jnp.max(s, axis=1, keepdims=True)
            pr = jnp.exp(s - m)
            l = jnp.sum(pr, axis=1, keepdims=True)
            o = jnp.dot(pr.astype(BF16), kv_refs[g][0, :, LANES:], preferred_element_type=F32) / l
            o_ref[rows, :] = jnp.where(head, o, m + jnp.log(l))


def _moba_own_kernel(sl_ref, q_ref, kv_ref, o_ref, lse_ref):
    BLK = MOBA_BLOCK
    p = pl.program_id(1)
    n_blocks = q_ref.shape[1] // BLK
    lane = lax.broadcasted_iota(jnp.int32, (BLK, LANES), 1)
    rr = lax.broadcasted_iota(jnp.int32, (BLK, BLK), 0)
    cc = lax.broadcasted_iota(jnp.int32, (BLK, BLK), 1)
    first = lane < MOBA_HEAD_DIM
    for h in range(n_blocks):
        i = pl.program_id(2) * n_blocks + h
        rows = pl.ds(h * BLK, BLK)
        qf = q_ref[0, rows, :]
        k_own = kv_ref[0, rows, :LANES]
        v_own = kv_ref[0, rows, LANES:]
        key_pos = (lax.broadcasted_iota(jnp.int32, (1, BLK), 1) + i * BLK).astype(F32)
        outs, lses = [], []
        for hh in range(2):
            head = (lane >> 6) == hh
            slope = sl_ref[2 * p + hh]
            qh = jnp.where(head, qf * (MOBA_HEAD_DIM ** -0.5), 0.0).astype(BF16)
            s = lax.dot_general(qh, k_own, _NT, preferred_element_type=F32)
            s = jnp.where(cc <= rr, s + slope * key_pos, -jnp.inf)
            m = jnp.max(s, axis=1, keepdims=True)
            pr = jnp.exp(s - m)
            l = jnp.sum(pr, axis=1, keepdims=True)
            lses.append(m + jnp.log(l))
            outs.append(jnp.dot(pr.astype(BF16), v_own, preferred_element_type=F32) / l)
        o_ref[0, rows, :] = jnp.where(first, outs[0], outs[1])
        lse_ref[0, rows, :] = jnp.where(first, lses[0], lses[1])


def _moba_merge_kernel(o_ref, lse_ref, pg_ref, out_ref):
    BLK = MOBA_BLOCK
    first = lax.broadcasted_iota(jnp.int32, (BLK, LANES), 1) < MOBA_HEAD_DIM
    for h in range(o_ref.shape[1] // BLK):
        i = pl.program_id(2) * (o_ref.shape[1] // BLK) + h
        rows = pl.ds(h * BLK, BLK)
        lses = [lse_ref[0, rows, :]]
        vals = [o_ref[0, rows, :]]
        for j in range(MOBA_TOPK):
            p0 = pg_ref[j, 0, rows, :]
            p1 = pg_ref[MOBA_TOPK + j, 0, rows, :]
            has_block = j < i
            stats = pltpu.roll(jnp.where(first, p1, p0), MOBA_HEAD_DIM, 1)
            lses.append(jnp.where(has_block, stats, -jnp.inf))
            vals.append(jnp.where(has_block, jnp.where(first, p0, p1), 0.0))
        top = functools.reduce(jnp.maximum, lses)
        ws = [jnp.exp(x - top) for x in lses]
        out_ref[0, rows, :] = sum(w * v for w, v in zip(ws, vals)) / sum(ws)


def _moba_mixer(h3, kv3, q6):
    B, S, _ = h3.shape
    T = B * S
    BLK = MOBA_BLOCK
    TQ = MOBA_TILE
    G = MOBA_TILE_GROUP
    NB = S // BLK
    GI = math.gcd(NB, MOBA_PLACE_GROUP)
    nbp = -(-NB // SUBLANES) * SUBLANES
    NP = MOBA_HEADS // 2
    n_rep = 2 * MOBA_TOPK
    slopes = jnp.asarray(_alibi_slope_list(MOBA_HEADS), F32)

    QB = math.gcd(NB, MOBA_PICK_GROUP)
    selrank, counts = pl.pallas_call(
        functools.partial(_moba_pick_kernel, qblocks=QB),
        grid=(B, NP, NB // QB),
        in_specs=[pl.BlockSpec((1, QB * BLK, LANES), lambda b, p, i: (b, i, p)),
                  pl.BlockSpec((1, QB * BLK, LANES), lambda b, p, i: (b, i, NP + p))],
        out_specs=[pl.BlockSpec((1, 1, 2 * SUBLANES, QB * BLK), lambda b, p, i: (b, p, 0, i)),
                   pl.BlockSpec((1, 1, QB * nbp, LANES), lambda b, p, i: (b, p, i, 0))],
        out_shape=[jax.ShapeDtypeStruct((B, NP, 2 * SUBLANES, S), jnp.int32),
                   jax.ShapeDtypeStruct((B, NP, NB * nbp, LANES), F32)],
        scratch_shapes=[pltpu.VMEM((nbp, LANES), F32)],
        compiler_params=_params("parallel", "parallel", "arbitrary"),
        name="moba_pick",
    )(h3, h3)

    OB = math.gcd(NB, MOBA_OWN_GROUP)
    blk_spec = lambda w: pl.BlockSpec((1, OB * BLK, w), lambda b, p, i, sl: (b, i, p))
    own_o, own_lse = pl.pallas_call(
        _moba_own_kernel,
        grid_spec=pltpu.PrefetchScalarGridSpec(
            num_scalar_prefetch=1,
            grid=(B, NP, NB // OB),
            in_specs=[blk_spec(LANES), blk_spec(2 * LANES)],
            out_specs=[blk_spec(LANES), blk_spec(LANES)]),
        out_shape=[jax.ShapeDtypeStruct((B, S, MIX_WIDTH), F32)] * 2,
        compiler_params=_params("parallel", "parallel", "parallel"),
        name="moba_own",
    )(slopes, h3, kv3)

    cnt = counts.reshape(B, NP, NB, nbp, LANES)[:, :, :, :NB, :2].astype(jnp.int32)
    cnt = cnt.transpose(0, 1, 2, 4, 3)
    base = jnp.cumsum(cnt, axis=2) - cnt
    total = jnp.sum(cnt, axis=2)
    padded = (total + TQ - 1) // TQ * TQ
    pend = jnp.cumsum(padded.reshape(-1))
    seg_start = (pend - padded.reshape(-1)).reshape(B, NP, 1, 2, NB)
    table = jnp.zeros((B, NP, NB, SUBLANES, LANES), F32).at[:, :, :, :2, :NB].set(
        (seg_start + base).astype(F32)).reshape(B, NP, NB * SUBLANES, LANES)
    n_seg = B * NP * 2 * NB
    max_tiles = -(-((T * NP * n_rep) // TQ + n_seg) // G) * G
    first_row = jnp.arange(max_tiles, dtype=jnp.int32) * TQ
    tile_seg = jnp.minimum(jnp.sum(pend[None, :] <= first_row[:, None], axis=1), n_seg - 1)
    tile_n = (tile_seg % NB).astype(jnp.int32)
    tile_h = ((tile_seg // NB) % 2).astype(jnp.int32)
    tile_p = ((tile_seg // (2 * NB)) % NP).astype(jnp.int32)
    tile_b = (tile_seg // (2 * NB * NP)).astype(jnp.int32)
    n_used = (pend[-1] // TQ).astype(jnp.int32).reshape(1)
    spare = max_tiles * TQ
    n_rows = spare + max(B * NP * MOBA_TOPK * n_rep * BLK, G * TQ)

    idx = pl.pallas_call(
        functools.partial(_moba_place_kernel, group=GI, spare=spare),
        grid=(B, NP, NB // GI),
        in_specs=[pl.BlockSpec((1, 1, 2 * SUBLANES, GI * BLK), lambda b, p, i: (b, p, 0, i)),
                  pl.BlockSpec((1, 1, GI * SUBLANES, LANES), lambda b, p, i: (b, p, i, 0))],
        out_specs=pl.BlockSpec((SC_IDX_ROWS, GI * BLK),
                               lambda b, p, i: (0, p * (T // (GI * BLK)) + b * (NB // GI) + i)),
        out_shape=jax.ShapeDtypeStruct((SC_IDX_ROWS, NP * T), jnp.int32),
        compiler_params=_params("parallel", "parallel", "parallel"),
        name="moba_place",
    )(selrank, table)

    qs = _sc_scatter_rows(q6.reshape(NP * T, LANES), idx, n_rep, n_rows)

    live = lambda t, tb, tp, th, tn, nu, sl: (jnp.where(t * G < nu[0], t, max_tiles // G), 0)
    kv_blk = lambda g: pl.BlockSpec(
        (1, BLK, 2 * LANES),
        lambda t, tb, tp, th, tn, nu, sl: (tb[t * G + g], tn[t * G + g], tp[t * G + g]))
    part = pl.pallas_call(
        functools.partial(_moba_tiles_kernel, group=G),
        grid_spec=pltpu.PrefetchScalarGridSpec(
            num_scalar_prefetch=6,
            grid=(max_tiles // G,),
            in_specs=[pl.BlockSpec((G * TQ, LANES), live)] + [kv_blk(g) for g in range(G)],
            out_specs=pl.BlockSpec((G * TQ, LANES), live)),
        out_shape=jax.ShapeDtypeStruct((n_rows, LANES), F32),
        compiler_params=_params("arbitrary"),
        name="moba_tiles",
    )(tile_b, tile_p, tile_h, tile_n, n_used, slopes, qs, *([kv3] * G))

    pg = _sc_gather_rows(part, idx[:n_rep].reshape(-1)).reshape(n_rep, NP, T, LANES)

    MB = math.gcd(NB, MOBA_MERGE_GROUP)
    blk = pl.BlockSpec((1, MB * BLK, LANES), lambda b, p, i: (b, i, p))
    return pl.pallas_call(
        _moba_merge_kernel,
        grid=(B, NP, NB // MB),
        in_specs=[blk, blk,
                  pl.BlockSpec((n_rep, 1, MB * BLK, LANES),
                               lambda b, p, i: (0, p, b * (NB // MB) + i, 0))],
        out_specs=blk,
        out_shape=jax.ShapeDtypeStruct((B, S, MIX_WIDTH), F32),
        compiler_params=_params("parallel", "parallel", "parallel"),
        name="moba_merge",
    )(own_o, own_lse, pg)


def _memkv_kernel(mem_ref, w_ref, kv_ref):
    kv_ref[0] = jnp.dot(mem_ref[0].astype(BF16), w_ref[...].astype(BF16),
                        preferred_element_type=F32).astype(BF16)


def _memkv(mem, w_kv):
    B, M, D = mem.shape
    N = w_kv.shape[1]
    return pl.pallas_call(
        _memkv_kernel,
        grid=(B,),
        in_specs=[pl.BlockSpec((1, M, D), lambda b: (b, 0, 0)),
                  pl.BlockSpec((D, N), lambda b: (0, 0))],
        out_specs=pl.BlockSpec((1, M, N), lambda b: (b, 0, 0)),
        out_shape=jax.ShapeDtypeStruct((B, M, N), BF16),
        compiler_params=_params("parallel"),
        name="memkv",
    )(mem, w_kv)


def _layer_norm(z, g, b):
    mu = jnp.mean(z, axis=-1, keepdims=True)
    zc = z - mu
    var = jnp.mean(zc * zc, axis=-1, keepdims=True)
    return zc * lax.rsqrt(var + LN_EPS) * g + b


def _post_kernel(x_ref, mix_ref, mq_ref, kv_ref, wo_ref, g_ref, b_ref, wr_ref, br_ref,
                 x1_ref, x1s_ref, idx_ref, gate_ref, rank_ref, cnt_ref, run_ref, *, alpha):
    tm = x_ref.shape[0]
    mq = mq_ref[...]
    kv = kv_ref[0]
    km = kv[:, :MEM_WIDTH]
    vm = kv[:, MEM_WIDTH:]
    lane = lax.broadcasted_iota(jnp.int32, (tm, MEM_WIDTH), 1)
    scale = MEM_HEAD_DIM ** -0.5
    mo = jnp.zeros((tm, MEM_WIDTH), F32)
    for hd in range(MEM_HEADS):
        head = (lane >> 6) == hd
        qh = jnp.where(head, mq * scale, 0.0).astype(BF16)
        s = lax.dot_general(qh, km, _NT, preferred_element_type=F32)
        m = jnp.max(s, axis=1, keepdims=True)
        p = jnp.exp(s - m)
        l = jnp.sum(p, axis=1, keepdims=True)
        oh = jnp.dot(p.astype(BF16), vm, preferred_element_type=F32) / l
        mo = jnp.where(head, oh, mo)

    y = jnp.dot(mix_ref[...].astype(BF16), wo_ref[:MIX_WIDTH, :], preferred_element_type=F32)
    y = y + jnp.dot(mo.astype(BF16), wo_ref[MIX_WIDTH:, :], preferred_element_type=F32)
    x1 = _layer_norm(alpha * x_ref[...] + y, g_ref[...], b_ref[...])
    x1_ref[...] = x1
    _store_subrows(x1s_ref, x1)

    x_hi = x1.astype(BF16)
    x_lo = (x1 - x_hi.astype(F32)).astype(BF16)
    hi = jnp.dot(x_hi, wr_ref[...], preferred_element_type=F32)
    lo = jnp.dot(x_lo, wr_ref[:, :LANES], preferred_element_type=F32)
    logits = hi[:, :LANES] + hi[:, LANES:] + lo
    g = logits.T[:N_EXPERTS] + br_ref[...]
    erow = lax.broadcasted_iota(jnp.int32, (N_EXPERTS, tm), 0)
    orow = lax.broadcasted_iota(jnp.int32, (SUBLANES, tm), 0)
    idx_out = jnp.zeros((SUBLANES, tm), jnp.int32)
    vals, picks = [], []
    chosen = jnp.zeros((N_EXPERTS, tm), F32)
    for kk in range(TOP_K):
        mx = jnp.max(g, axis=0, keepdims=True)
        idx = jnp.min(jnp.where(g == mx, erow, N_EXPERTS), axis=0, keepdims=True)
        idx_out = jnp.where(orow == kk, idx, idx_out)
        vals.append(mx)
        picks.append(idx)
        chosen = chosen + jnp.where(erow == idx, 1.0, 0.0)
        g = jnp.where(erow == idx, -jnp.inf, g)
    evs = [jnp.exp(v - vals[0]) for v in vals]
    den = sum(evs)
    gate_out = jnp.zeros((SUBLANES, tm), F32)
    for kk in range(TOP_K):
        gate_out = jnp.where(orow == kk, evs[kk] / den, gate_out)
    idx_ref[...] = idx_out
    gate_ref[...] = gate_out

    @pl.when(pl.program_id(0) == 0)
    def _():
        run_ref[...] = jnp.zeros_like(run_ref)

    ta = lax.broadcasted_iota(jnp.int32, (tm, tm), 0)
    tb = lax.broadcasted_iota(jnp.int32, (tm, tm), 1)
    before = jnp.where(ta < tb, 1.0, 0.0).astype(BF16)
    earlier = jnp.dot(chosen.astype(BF16), before, preferred_element_type=F32) + run_ref[:, :1]
    rank_out = jnp.zeros((SUBLANES, tm), jnp.int32)
    for kk in range(TOP_K):
        rank = jnp.sum(jnp.where(erow == picks[kk], earlier, 0.0), axis=0, keepdims=True)
        rank_out = jnp.where(orow == kk, rank.astype(jnp.int32), rank_out)
    rank_ref[...] = rank_out
    run_ref[...] = run_ref[...] + jnp.sum(chosen, axis=1, keepdims=True)
    cnt_ref[...] = run_ref[...]


def _post_mixer(x2, mix2, h2, kv, w_o_bf16, ln_g, ln_b, w_router, b_router, seq_len, alpha,
                part, n_parts, local_mix):
    D = x2.shape[1]
    T = x2.shape[0] // n_parts
    tm = POST_ROWS
    N = h2.shape[1]
    M = kv.shape[1]
    mq_col = (N - MEM_WIDTH) // MEM_WIDTH
    tiles_per_seq = seq_len // tm
    off = part * (T // tm)
    mix_off = 0 if local_mix else off
    wr = jnp.zeros((D, LANES), F32).at[:, :N_EXPERTS].set(w_router)
    wr_hi = wr.astype(BF16)
    wr = jnp.concatenate([wr_hi, (wr - wr_hi.astype(F32)).astype(BF16)], axis=1)
    br = b_router.reshape(N_EXPERTS, 1)
    row_in = lambda n: pl.BlockSpec((tm, n), lambda i: (i + off, 0))
    row = lambda n: pl.BlockSpec((tm, n), lambda i: (i, 0))
    full = lambda a, b: pl.BlockSpec((a, b), lambda i: (0, 0))
    per_token = pl.BlockSpec((SUBLANES, tm), lambda i: (0, i))
    return pl.pallas_call(
        functools.partial(_post_kernel, alpha=alpha),
        grid=(T // tm,),
        in_specs=[row_in(D), pl.BlockSpec((tm, MIX_WIDTH), lambda i: (i + mix_off, 0)),
                  pl.BlockSpec((tm, MEM_WIDTH), lambda i: (i + mix_off, mq_col)),
                  pl.BlockSpec((1, M, 2 * MEM_WIDTH), lambda i: ((i + off) // tiles_per_seq, 0, 0)),
                  full(D, D), full(1, D), full(1, D),
                  full(D, 2 * LANES), full(N_EXPERTS, 1)],
        out_specs=[row(D), pl.BlockSpec((tm * (D // LANES), LANES), lambda i: (i, 0)),
                   per_token, per_token, per_token, full(N_EXPERTS, LANES)],
        out_shape=[jax.ShapeDtypeStruct((T, D), F32),
                   jax.ShapeDtypeStruct((T * (D // LANES), LANES), F32),
                   jax.ShapeDtypeStruct((SUBLANES, T), jnp.int32),
                   jax.ShapeDtypeStruct((SUBLANES, T), F32),
                   jax.ShapeDtypeStruct((SUBLANES, T), jnp.int32),
                   jax.ShapeDtypeStruct((N_EXPERTS, LANES), F32)],
        scratch_shapes=[pltpu.VMEM((N_EXPERTS, LANES), F32)],
        compiler_params=_params("arbitrary"),
        name="post_mixer",
    )(x2, mix2, h2, kv, w_o_bf16, ln_g.reshape(1, D), ln_b.reshape(1, D), wr, br)


def _sc_mesh():
    return plsc.VectorSubcoreMesh(core_axis_name="core", subcore_axis_name="subcore")


def _sc_scatter_rows(rows, idx, n_rep, n_out):
    R, W = rows.shape

    @functools.partial(pl.kernel, out_type=jax.ShapeDtypeStruct((n_out, W), rows.dtype),
                       mesh=_sc_mesh(), scratch_types=[])
    def scatter(x_hbm, i_hbm, o_hbm):
        def body(x_vmem, i_vmem):
            for r in range(n_rep):
                pltpu.sync_copy(x_vmem, o_hbm.at[i_vmem.at[r]])

        pltpu.emit_pipeline(
            body, grid=(R // SC_WINDOW,),
            in_specs=[pl.BlockSpec((SC_WINDOW, W), lambda i: (i, 0)),
                      pl.BlockSpec((SC_IDX_ROWS, SC_WINDOW), lambda i: (0, i))],
            out_specs=[], core_axis_name=("core", "subcore"),
            dimension_semantics=(pltpu.PARALLEL,), trace_scopes=False)(x_hbm, i_hbm)

    return scatter(rows, idx)


def _sc_gather_rows(table, idx):
    n = idx.shape[0]
    W = table.shape[1]

    @functools.partial(pl.kernel, out_type=jax.ShapeDtypeStruct((n, W), table.dtype),
                       mesh=_sc_mesh(), scratch_types=[])
    def gather(t_hbm, i_hbm, o_hbm):
        def body(i_vmem, o_vmem):
            pltpu.sync_copy(t_hbm.at[i_vmem.at[0]], o_vmem)

        pltpu.emit_pipeline(
            body, grid=(n // SC_WINDOW,),
            in_specs=[pl.BlockSpec((1, SC_WINDOW), lambda i: (0, i))],
            out_specs=[pl.BlockSpec((SC_WINDOW, W), lambda i: (i, 0))],
            core_axis_name=("core", "subcore"),
            dimension_semantics=(pltpu.PARALLEL,), trace_scopes=False)(i_hbm, o_hbm)

    return gather(table, idx.reshape(1, n))


def _sc_workers():
    info = pltpu.get_tpu_info().sparse_core
    return info.num_cores, info.num_cores * info.num_subcores


def _sc_scatter_slabs(rows, idx, n_rep, n_out):
    R, S, W = rows.shape
    n_cores, n_workers = _sc_workers()
    per_worker = (R // SC_WINDOW) // n_workers
    assert per_worker * n_workers * SC_WINDOW == R

    @functools.partial(pl.kernel, out_type=jax.ShapeDtypeStruct((n_out, S, W), rows.dtype),
                       mesh=_sc_mesh(),
                       scratch_types=[pltpu.VMEM((SC_IDX_ROWS, SC_WINDOW), jnp.int32),
                                      pltpu.VMEM((SC_CHUNK, S, W), rows.dtype)])
    def scatter(x_hbm, i_hbm, o_hbm, ibuf, buf):
        wid = lax.axis_index("subcore") * n_cores + lax.axis_index("core")

        @pl.loop(0, per_worker)
        def _(s):
            first = (wid * per_worker + s) * SC_WINDOW
            pltpu.sync_copy(i_hbm.at[:, pl.ds(first, SC_WINDOW)], ibuf)
            for c in range(SC_WINDOW // SC_CHUNK):
                pltpu.sync_copy(x_hbm.at[pl.ds(first + c * SC_CHUNK, SC_CHUNK)], buf)
                for r in range(n_rep):
                    pltpu.sync_copy(buf, o_hbm.at[ibuf.at[r, pl.ds(c * SC_CHUNK, SC_CHUNK)]])

    return scatter(rows, idx)


def _sc_gather_slabs(table, idx):
    n = idx.shape[0]
    S, W = table.shape[1:]
    n_cores, n_workers = _sc_workers()
    per_worker = (n // SC_WINDOW) // n_workers
    assert per_worker * n_workers * SC_WINDOW == n
    n_chunks = SC_WINDOW // SC_CHUNK

    @functools.partial(pl.kernel, out_type=jax.ShapeDtypeStruct((n, S, W), table.dtype),
                       mesh=_sc_mesh(),
                       scratch_types=[pltpu.VMEM((1, SC_WINDOW), jnp.int32),
                                      pltpu.VMEM((2, SC_CHUNK, S, W), table.dtype),
                                      pltpu.SemaphoreType.DMA((2,)), pltpu.SemaphoreType.DMA((2,))])
    def gather(t_hbm, i_hbm, o_hbm, ibuf, buf, fetch_sem, store_sem):
        wid = lax.axis_index("subcore") * n_cores + lax.axis_index("core")

        @pl.loop(0, per_worker)
        def _(s):
            blk = wid * per_worker + s
            pltpu.sync_copy(i_hbm.at[pl.ds(blk, 1)], ibuf)

            def fetch(c):
                return pltpu.make_async_copy(
                    t_hbm.at[ibuf.at[0, pl.ds(c * SC_CHUNK, SC_CHUNK)]], buf.at[c % 2],
                    fetch_sem.at[c % 2])

            def store(c):
                return pltpu.make_async_copy(
                    buf.at[c % 2], o_hbm.at[pl.ds(blk * SC_WINDOW + c * SC_CHUNK, SC_CHUNK)],
                    store_sem.at[c % 2])

            fetch(0).start()
            for c in range(n_chunks):
                if c + 1 < n_chunks:
                    if c >= 1:
                        store(c - 1).wait()
                    fetch(c + 1).start()
                fetch(c).wait()
                store(c).start()
            store(n_chunks - 2).wait()
            store(n_chunks - 1).wait()

    return gather(table, idx.reshape(n // SC_WINDOW, SC_WINDOW))


def _route(top_idx, rank, counts):
    rb = MOE_ROWS
    n_tokens = top_idx.shape[1]
    tk = n_tokens * TOP_K
    padded = (counts + rb - 1) // rb * rb
    pend = jnp.cumsum(padded)
    pstart = pend - padded
    experts = jnp.arange(N_EXPERTS, dtype=jnp.int32)
    start = jnp.sum(jnp.where(top_idx[:TOP_K, :, None] == experts, pstart, 0), axis=2)
    dest = (start + rank[:TOP_K]).astype(jnp.int32)
    n_blocks = tk // rb + N_EXPERTS
    first_row = jnp.arange(n_blocks, dtype=jnp.int32) * rb
    block_e = jnp.minimum(jnp.sum(pend[None, :] <= first_row[:, None], axis=1),
                          N_EXPERTS - 1).astype(jnp.int32)
    n_used = (pend[-1] // rb).astype(jnp.int32).reshape(1)
    return dest, block_e, n_used


def _dispatch(x1s, dest, n_rows, sub):
    T = dest.shape[1]
    idx = jnp.concatenate([dest, jnp.zeros((SC_IDX_ROWS - TOP_K, T), jnp.int32)], axis=0)
    xs = _sc_scatter_slabs(x1s.reshape(T, sub, LANES), idx, TOP_K, n_rows)
    return xs.reshape(n_rows * sub, LANES)


def _expert_kernel(be_ref, nu_ref, x_ref, wg_ref, bg_ref, wu_ref, bu_ref, wd_ref, bd_ref,
                   y_ref, wgb, wub, wdb):
    i = pl.program_id(0)
    prev = be_ref[jnp.maximum(i - 1, 0)]

    @pl.when((i == 0) | (be_ref[i] != prev))
    def _():
        wgb[...] = wg_ref[0, 0].astype(BF16)
        wub[...] = wu_ref[0, 0].astype(BF16)
        wdb[...] = wd_ref[0, 0].astype(BF16)

    @pl.when(i < nu_ref[0])
    def _():
        sub = wgb.shape[0] // LANES
        xb = _load_subrows(x_ref, x_ref.shape[0] // sub, sub).astype(BF16)
        gate = jnp.dot(xb, wgb[...], preferred_element_type=F32) + bg_ref[0, 0]
        gate = jnp.minimum(gate, SWIGLU_LIMIT)
        up = jnp.dot(xb, wub[...], preferred_element_type=F32) + bu_ref[0, 0]
        up = jnp.clip(up, -SWIGLU_LIMIT, SWIGLU_LIMIT)
        hid = gate * _sigmoid(SWIGLU_ALPHA * gate) * (up + 1.0)
        y = jnp.dot(hid.astype(BF16), wdb[...], preferred_element_type=F32) + bd_ref[0, 0]
        _store_subrows(y_ref, y)


def _experts(xs, block_e, n_used, layer, w_gate, b_gate, w_up, b_up, w_down, b_down):
    rb = MOE_ROWS
    n_blocks = block_e.shape[0]
    E, D, F = w_gate.shape[1:]
    sub = D // LANES
    wspec = lambda a, b: pl.BlockSpec((1, 1, a, b), lambda i, be, nu: (layer, be[i], 0, 0))
    live = lambda i, be, nu: (jnp.where(i < nu[0], i, n_blocks), 0)
    grid_spec = pltpu.PrefetchScalarGridSpec(
        num_scalar_prefetch=2,
        grid=(n_blocks,),
        in_specs=[pl.BlockSpec((rb * sub, LANES), live),
                  wspec(D, F), wspec(1, F), wspec(D, F), wspec(1, F), wspec(F, D), wspec(1, D)],
        out_specs=pl.BlockSpec((rb * sub, LANES), live),
        scratch_shapes=[pltpu.VMEM((D, F), BF16), pltpu.VMEM((D, F), BF16),
                        pltpu.VMEM((F, D), BF16)],
    )
    depth = w_gate.shape[0]
    return pl.pallas_call(
        _expert_kernel,
        grid_spec=grid_spec,
        out_shape=jax.ShapeDtypeStruct(xs.shape, F32),
        compiler_params=_params("arbitrary"),
        name="experts",
    )(block_e, n_used, xs, w_gate, b_gate.reshape(depth, E, 1, F),
      w_up, b_up.reshape(depth, E, 1, F), w_down, b_down.reshape(depth, E, 1, D))


def _combine_kernel(x1_ref, gate_ref, y_ref, g_ref, b_ref, *rest, alpha):
    o_ref = rest[-1]
    tm, D = x1_ref.shape
    sub = D // LANES
    gates = gate_ref[...].T
    f = jnp.zeros(x1_ref.shape, F32)
    for kk in range(TOP_K):
        f = f + gates[:, kk:kk + 1] * _load_subrows(y_ref, tm, sub, kk * sub, TOP_K * sub)
    o_ref[...] = _layer_norm(alpha * x1_ref[...] + f, g_ref[...], b_ref[...])


def _combine(x1, gates, dest, y_rows, ln_g, ln_b, alpha, part, n_parts, earlier):
    T, D = x1.shape
    sub = D // LANES
    tm = COMBINE_ROWS
    off = part * (T // tm)
    yg = _sc_gather_slabs(y_rows.reshape(-1, sub, LANES), dest.T.reshape(-1)).reshape(-1, LANES)
    row = lambda n: pl.BlockSpec((tm, n), lambda i: (i, 0))
    full = lambda a, b: pl.BlockSpec((a, b), lambda i: (0, 0))
    in_specs = [row(D), pl.BlockSpec((SUBLANES, tm), lambda i: (0, i)),
                pl.BlockSpec((tm * TOP_K * sub, LANES), lambda i: (i, 0)),
                full(1, D), full(1, D)]
    operands = [x1, gates, yg, ln_g.reshape(1, D), ln_b.reshape(1, D)]
    aliases = {}
    if earlier is not None:
        in_specs.append(pl.BlockSpec(memory_space=pl.ANY))
        operands.append(earlier)
        aliases = {len(operands) - 1: 0}
    return pl.pallas_call(
        functools.partial(_combine_kernel, alpha=alpha),
        grid=(T // tm,),
        in_specs=in_specs,
        out_specs=pl.BlockSpec((tm, D), lambda i: (i + off, 0)),
        out_shape=jax.ShapeDtypeStruct((n_parts * T, D), F32),
        input_output_aliases=aliases,
        compiler_params=_params("parallel"),
        name="combine",
    )(*operands)


def kernel(x, mem, w_in_hgrn, hgrn_lb_logits, hgrn_norm_g, w_in_moba, w_mem_kv, w_o,
           ln_mix_g, ln_mix_b, w_router, b_router, w_gate, b_gate, w_up, b_up,
           w_down, b_down, ln_ffn_g, ln_ffn_b):
    B, S, D = x.shape
    T = B * S
    depth = w_o.shape[0]
    alpha = (2 * depth) ** 0.25

    p_lb = jax.nn.softmax(hgrn_lb_logits.astype(F32), axis=0)
    lower_bounds = jnp.cumsum(p_lb, axis=0) - p_lb[0]

    x2 = x.reshape(T, D)
    for layer in range(depth):
        j = layer // 2
        if layer % 2 == 0:
            (h2,) = _inproj(x2, w_in_hgrn[j].astype(BF16), for_moba=False)
            mix2 = _hgrn_mixer(h2.reshape(B, S, -1), lower_bounds[j], hgrn_norm_g[j])
            mixed = [(mix2.reshape(T, MIX_WIDTH), h2)] * B
        else:
            w_in = w_in_moba[j].astype(BF16)
            mixed = []
            for b in range(B):
                h2, kv2, q6 = _inproj(x2, w_in, for_moba=True, part=b, n_parts=B)
                mix2 = _moba_mixer(h2.reshape(1, S, -1), kv2.reshape(1, S, -1), q6)
                mixed.append((mix2.reshape(S, MIX_WIDTH), h2))
        kv = _memkv(mem, w_mem_kv[layer])
        w_out = w_o[layer].astype(BF16)
        x_next = None
        for part in range(B):
            mix2, h2 = mixed[part]
            x1, x1s, top_idx, gates, rank, counts = _post_mixer(
                x2, mix2, h2, kv, w_out, ln_mix_g[layer], ln_mix_b[layer],
                w_router[layer], b_router[layer], S, alpha, part, B, layer % 2 == 1)
            dest, block_e, n_used = _route(top_idx, rank, counts[:, 0].astype(jnp.int32))
            xs = _dispatch(x1s, dest, (block_e.shape[0] + 1) * MOE_ROWS, D // LANES)
            y_rows = _experts(xs, block_e, n_used, layer,
                              w_gate, b_gate, w_up, b_up, w_down, b_down)
            x_next = _combine(x1, gates, dest, y_rows, ln_ffn_g[layer], ln_ffn_b[layer], alpha,
                              part, B, x_next)
        x2 = x_next
    return x2.reshape(B, S, D)
```

```python
import functools
import math

import jax
import jax.numpy as jnp
from jax import lax
from jax.experimental import pallas as pl
from jax.experimental.pallas import tpu as pltpu
from jax.experimental.pallas import tpu_sc as plsc

MIX_WIDTH = 768
MEM_HEADS = 4
MEM_HEAD_DIM = 64
MEM_WIDTH = MEM_HEADS * MEM_HEAD_DIM
HGRN_HEADS = 6
HGRN_DK = 128
MOBA_HEADS = 12
MOBA_HEAD_DIM = 64
MOBA_BLOCK = 256
MOBA_TOPK = 3
N_EXPERTS = 32
TOP_K = 4
SWIGLU_ALPHA = 1.702
SWIGLU_LIMIT = 7.0
LN_EPS = 1e-5
RMS_EPS = 1e-6

LANES = 128
SUBLANES = 8
VMEM_LIMIT_BYTES = 56 * 1024 * 1024

INPROJ_ROWS = 512
HGRN_CHUNK = 64
HGRN_ROWS = 2048
POST_ROWS = 512
MOBA_TILE = 256
MOBA_TILE_GROUP = 32
MOBA_PLACE_GROUP = 8
MOBA_PICK_GROUP = 16
MOBA_MERGE_GROUP = 16
MOBA_OWN_GROUP = 16
MOE_ROWS = 512
COMBINE_ROWS = 512
SC_WINDOW = 128
SC_IDX_ROWS = 8
SC_CHUNK = 32

BF16 = jnp.bfloat16
F32 = jnp.float32

_NT = (((1,), (1,)), ((), ()))
_TN = (((0,), (0,)), ((), ()))


def _alibi_slope_list(n):
    def pow2(m):
        start = 2.0 ** (-(2.0 ** -(math.log2(m) - 3)))
        return [start ** (i + 1) for i in range(m)]
    if math.log2(n).is_integer():
        return pow2(n)
    c = 2 ** math.floor(math.log2(n))
    return pow2(c) + _alibi_slope_list(2 * c)[0::2][:n - c]


def _sigmoid(x):
    return 1.0 / (1.0 + jnp.exp(-x))


def _params(*sem):
    return pltpu.CompilerParams(dimension_semantics=sem, vmem_limit_bytes=VMEM_LIMIT_BYTES)


def _store_subrows(ref, value):
    sub = value.shape[1] // LANES
    for c in range(sub):
        ref[pl.ds(c, value.shape[0], stride=sub), :] = value[:, c * LANES:(c + 1) * LANES]


def _load_subrows(ref, rows, sub, first=0, stride=None):
    stride = stride or sub
    return jnp.concatenate(
        [ref[pl.ds(first + c, rows, stride=stride), :] for c in range(sub)], axis=1)


def _inproj_kernel(x_ref, w_ref, h_ref, *moba_refs):
    h = jnp.dot(x_ref[...].astype(BF16), w_ref[...], preferred_element_type=F32)
    if not moba_refs:
        h_ref[...] = h
    else:
        h_ref[:, :MIX_WIDTH] = h[:, MIX_WIDTH:2 * MIX_WIDTH]
        h_ref[:, MIX_WIDTH:] = h[:, 3 * MIX_WIDTH:]
        kv_ref, q6_ref = moba_refs
        n_pairs = q6_ref.shape[0]
        for p in range(n_pairs):
            q6_ref[p] = h[:, p * LANES:(p + 1) * LANES] * (MOBA_HEAD_DIM ** -0.5)
            for part in range(2):
                col = (1 + part) * MIX_WIDTH + p * LANES
                kv_ref[:, (2 * p + part) * LANES:(2 * p + part + 1) * LANES] = (
                    h[:, col:col + LANES].astype(BF16))


def _inproj(x2, w_bf16, for_moba, part=0, n_parts=1):
    D = x2.shape[1]
    T = x2.shape[0] // n_parts
    N = w_bf16.shape[1]
    tm = INPROJ_ROWS
    off = part * (T // tm)
    n_f32 = N - 2 * MIX_WIDTH if for_moba else N
    out_shape = [jax.ShapeDtypeStruct((T, n_f32), F32)]
    out_specs = [pl.BlockSpec((tm, n_f32), lambda i: (i, 0))]
    if for_moba:
        NP = MOBA_HEADS // 2
        out_shape += [jax.ShapeDtypeStruct((T, 2 * MIX_WIDTH), BF16),
                      jax.ShapeDtypeStruct((NP, T, LANES), F32)]
        out_specs += [pl.BlockSpec((tm, 2 * MIX_WIDTH), lambda i: (i, 0)),
                      pl.BlockSpec((NP, tm, LANES), lambda i: (0, i, 0))]
    return pl.pallas_call(
        _inproj_kernel,
        grid=(T // tm,),
        in_specs=[pl.BlockSpec((tm, D), lambda i: (i + off, 0)),
                  pl.BlockSpec((D, N), lambda i: (0, 0))],
        out_specs=out_specs,
        out_shape=out_shape,
        compiler_params=_params("parallel"),
        name="inproj",
    )(x2, w_bf16)


def _cumsum_rows(x, row):
    n = x.shape[0]
    sh = 1
    while sh < n:
        x = x + jnp.where(row >= sh, pltpu.roll(x, sh, 0), 0.0)
        sh *= 2
    return x


def _bcast_row(a, group, r):
    n = a.shape[0]
    a3 = a.reshape(n // group, group, LANES)
    return jnp.broadcast_to(a3[:, r:r + 1, :], a3.shape).reshape(n, LANES)


def _hgrn_chunk(qr, fr, v, gr, lb, ng, e_sum, st_t):
    C = qr.shape[0]
    row = lax.broadcasted_iota(jnp.int32, (C, LANES), 0)
    rr = lax.broadcasted_iota(jnp.int32, (C, C), 0)
    cc = lax.broadcasted_iota(jnp.int32, (C, C), 1)

    q = qr * _sigmoid(qr)
    forget = lb + (1.0 - lb) * _sigmoid(fr)
    k = 1.0 - forget
    G = _cumsum_rows(jnp.log(forget), row)

    z = jnp.log(k) - G
    parts = []
    for s in range(SUBLANES):
        parts.append((q * jnp.exp(jnp.minimum(G + _bcast_row(z, SUBLANES, s), 0.0))).astype(BF16))
    a_diag = jnp.dot(jnp.concatenate(parts, axis=1), e_sum, preferred_element_type=F32)
    A = jnp.where(((rr >> 3) == (cc >> 3)) & (cc <= rr), a_diag, 0.0)

    m = SUBLANES
    while m < C:
        lg = int(math.log2(m))
        Gr = _bcast_row(G, 2 * m, m - 1)
        second = ((row >> lg) & 1) == 1
        qm = q * jnp.exp(jnp.where(second, G - Gr, -jnp.inf))
        km = k * jnp.exp(jnp.where(second, -jnp.inf, Gr - G))
        am = lax.dot_general(qm.astype(BF16), km.astype(BF16), _NT, preferred_element_type=F32)
        A = A + jnp.where((rr >> (lg + 1)) == (cc >> (lg + 1)), am, 0.0)
        m *= 2

    vb = v.astype(BF16)
    o = jnp.dot(A.astype(BF16), vb, preferred_element_type=F32)
    o = o + lax.dot_general((q * jnp.exp(G)).astype(BF16), st_t.astype(BF16), _NT,
                            preferred_element_type=F32)
    g_end = G[C - 1:C, :]
    kd = (k * jnp.exp(g_end - G)).astype(BF16)
    st_new = st_t * jnp.exp(g_end) + lax.dot_general(vb, kd, _TN, preferred_element_type=F32)

    ms = jnp.mean(o * o, axis=-1, keepdims=True)
    out = o * lax.rsqrt(ms + RMS_EPS) * ng * _sigmoid(gr)
    return out, st_new


def _hgrn_kernel(q_ref, f_ref, i_ref, g_ref, lb_ref, ng_ref, e_ref, o_ref, st_ref, *, chunk):
    @pl.when(pl.program_id(2) == 0)
    def _():
        st_ref[...] = jnp.zeros_like(st_ref)

    lb = lb_ref[0]
    ng = ng_ref[...]
    e_sum = e_ref[...]
    n_chunks = q_ref.shape[1] // chunk
    for c in range(n_chunks):
        sl = pl.ds(c * chunk, chunk)
        out, st_new = _hgrn_chunk(q_ref[0, sl, :], f_ref[0, sl, :], i_ref[0, sl, :],
                                  g_ref[0, sl, :], lb, ng, e_sum, st_ref[...])
        st_ref[...] = st_new
        o_ref[0, sl, :] = out


def _hgrn_mixer(h3, lb, norm_g):
    B, S, _ = h3.shape
    ts = min(HGRN_ROWS, S)
    C = HGRN_CHUNK
    H = HGRN_HEADS
    e_sum = (jnp.arange(SUBLANES * LANES)[:, None] // LANES == jnp.arange(C)[None, :] % SUBLANES
             ).astype(BF16)
    col = lambda off: pl.BlockSpec((1, ts, LANES), lambda b, h, s, off=off: (b, s, off + h))
    return pl.pallas_call(
        functools.partial(_hgrn_kernel, chunk=C),
        grid=(B, H, S // ts),
        in_specs=[col(0), col(H), col(2 * H), col(3 * H),
                  pl.BlockSpec((1, 1, LANES), lambda b, h, s: (h, 0, 0)),
                  pl.BlockSpec((1, LANES), lambda b, h, s: (0, 0)),
                  pl.BlockSpec((SUBLANES * LANES, C), lambda b, h, s: (0, 0))],
        out_specs=pl.BlockSpec((1, ts, LANES), lambda b, h, s: (b, s, h)),
        out_shape=jax.ShapeDtypeStruct((B, S, MIX_WIDTH), F32),
        scratch_shapes=[pltpu.VMEM((HGRN_DK, HGRN_DK), F32)],
        compiler_params=_params("parallel", "parallel", "arbitrary"),
        name="hgrn",
    )(h3, h3, h3, h3, lb.reshape(H, 1, LANES), norm_g.reshape(1, LANES), e_sum)


def _moba_pick_kernel(q_ref, kf_ref, sr_ref, cnt_ref, kmean_ref, *, qblocks):
    BLK = MOBA_BLOCK
    W = qblocks * BLK
    nbp = kmean_ref.shape[0]
    i0 = pl.program_id(2) * qblocks

    @pl.when(pl.program_id(2) == 0)
    def _():
        kmean_ref[...] = jnp.zeros_like(kmean_ref)

    for j in range(qblocks):
        kmean_ref[pl.ds(i0 + j, 1), :] = jnp.mean(kf_ref[0, j * BLK:(j + 1) * BLK, :], axis=0,
                                                  keepdims=True)
    km = kmean_ref[...]
    lane_k = lax.broadcasted_iota(jnp.int32, (nbp, LANES), 1)
    km2 = jnp.concatenate([jnp.where(lane_k < MOBA_HEAD_DIM, km, 0.0),
                           jnp.where(lane_k >= MOBA_HEAD_DIM, km, 0.0)], axis=0)
    gate = lax.dot_general(km2, q_ref[0], _NT, precision=lax.Precision.HIGHEST,
                           preferred_element_type=F32).reshape(2, nbp, W)
    nblk = lax.broadcasted_iota(jnp.int32, (2, nbp, W), 1)
    qi = i0 + (lax.broadcasted_iota(jnp.int32, (2, 1, W), 2) >> int(math.log2(BLK)))
    g = jnp.where(nblk < qi, gate, -jnp.inf)
    picks = []
    for _ in range(MOBA_TOPK):
        mx = jnp.max(g, axis=1, keepdims=True)
        idx = jnp.min(jnp.where(g == mx, nblk, nbp), axis=1, keepdims=True)
        picks.append((idx, (mx > -jnp.inf) & (idx < qi)))
        g = jnp.where(nblk == idx, -jnp.inf, g)
    chosen = jnp.zeros((2, nbp, W), F32)
    for idx, valid in picks:
        chosen = chosen + jnp.where((nblk == idx) & valid, 1.0, 0.0)
    qa = lax.broadcasted_iota(jnp.int32, (BLK, BLK), 0)
    qc = lax.broadcasted_iota(jnp.int32, (BLK, BLK), 1)
    before = jnp.where(qa < qc, 1.0, 0.0).astype(BF16)
    chosen2 = chosen.reshape(2 * nbp, W).astype(BF16)
    earlier = jnp.concatenate(
        [jnp.dot(chosen2[:, j * BLK:(j + 1) * BLK], before, preferred_element_type=F32)
         for j in range(qblocks)], axis=1).reshape(2, nbp, W)
    orow = lax.broadcasted_iota(jnp.int32, (2 * SUBLANES, W), 0)
    out = jnp.zeros((2 * SUBLANES, W), jnp.int32)
    for j, (idx, valid) in enumerate(picks):
        rank = jnp.sum(jnp.where(nblk == idx, earlier, 0.0), axis=1, keepdims=True).astype(jnp.int32)
        sel = jnp.where(valid, idx, -1)
        for hh in range(2):
            rep = hh * MOBA_TOPK + j
            out = jnp.where(orow == rep, sel[hh], out)
            out = jnp.where(orow == SUBLANES + rep, rank[hh], out)
    sr_ref[0, 0] = out
    lane_c = lax.broadcasted_iota(jnp.int32, (nbp, LANES), 1)
    for j in range(qblocks):
        per_block = jnp.sum(chosen[:, :, j * BLK:(j + 1) * BLK], axis=2, keepdims=True)
        cnt_ref[0, 0, j * nbp:(j + 1) * nbp, :] = jnp.where(
            lane_c == 0, per_block[0], jnp.where(lane_c == 1, per_block[1], 0.0))


def _moba_place_kernel(sr_ref, tab_ref, idx_ref, *, group, spare):
    BLK = MOBA_BLOCK
    n_rep = 2 * MOBA_TOPK
    b = pl.program_id(0)
    p = pl.program_id(1)
    ig = pl.program_id(2)
    nrow = lax.broadcasted_iota(jnp.int32, (LANES, BLK), 0)
    orow = lax.broadcasted_iota(jnp.int32, (SC_IDX_ROWS, BLK), 0)
    qpos = lax.broadcasted_iota(jnp.int32, (1, BLK), 1)
    for g in range(group):
        blk = sr_ref[0, 0, :, g * BLK:(g + 1) * BLK]
        tab_t = tab_ref[0, 0, g * SUBLANES:(g + 1) * SUBLANES, :].T
        early = jnp.minimum(ig * group + g, MOBA_TOPK - 1)
        out = jnp.zeros((SC_IDX_ROWS, BLK), jnp.int32)
        for rep in range(n_rep):
            hh = rep // MOBA_TOPK
            sel = blk[rep:rep + 1, :]
            rank = blk[SUBLANES + rep:SUBLANES + rep + 1, :]
            start = jnp.sum(jnp.where(nrow == sel, tab_t[:, hh:hh + 1], 0.0), axis=0, keepdims=True)
            unused = spare + (((b * pl.num_programs(1) + p) * MOBA_TOPK + early) * n_rep + rep) * BLK
            dest = jnp.where(sel >= 0, start.astype(jnp.int32) + rank, unused + qpos)
            out = jnp.where(orow == rep, dest, out)
        idx_ref[:, g * BLK:(g + 1) * BLK] = out


def _moba_tiles_kernel(tb_ref, tp_ref, th_ref, tn_ref, nu_ref, sl_ref, q_ref, *refs, group):
    kv_refs, o_ref = refs[:group], refs[group]
    t = pl.program_id(0)
    tq = q_ref.shape[0] // group

    @pl.when(t * group < nu_ref[0])
    def _():
        lane = lax.broadcasted_iota(jnp.int32, (tq, LANES), 1)
        kpos = lax.broadcasted_iota(jnp.int32, (1, MOBA_BLOCK), 1)
        for g in range(group):
            tt = t * group + g
            hh = th_ref[tt]
            slope = sl_ref[2 * tp_ref[tt] + hh]
            head = (lane >> 6) == hh
            rows = pl.ds(g * tq, tq)
            q = jnp.where(head, q_ref[rows, :], 0.0).astype(BF16)
            s = lax.dot_general(q, kv_refs[g][0, :, :LANES], _NT, preferred_element_type=F32)
            s = s + slope * (kpos + tn_ref[tt] * MOBA_BLOCK).astype(F32)
            m = jnp.max(s, axis=1, keepdims=True)
            pr = jnp.exp(s - m)
            l = jnp.sum(pr, axis=1, keepdims=True)
            o = jnp.dot(pr.astype(BF16), kv_refs[g][0, :, LANES:], preferred_element_type=F32) / l
            o_ref[rows, :] = jnp.where(head, o, m + jnp.log(l))


def _moba_own_kernel(sl_ref, q_ref, kv_ref, o_ref, lse_ref):
    BLK = MOBA_BLOCK
    p = pl.program_id(1)
    n_blocks = q_ref.shape[1] // BLK
    lane = lax.broadcasted_iota(jnp.int32, (BLK, LANES), 1)
    rr = lax.broadcasted_iota(jnp.int32, (BLK, BLK), 0)
    cc = lax.broadcasted_iota(jnp.int32, (BLK, BLK), 1)
    first = lane < MOBA_HEAD_DIM
    for h in range(n_blocks):
        i = pl.program_id(2) * n_blocks + h
        rows = pl.ds(h * BLK, BLK)
        qf = q_ref[0, rows, :]
        k_own = kv_ref[0, rows, :LANES]
        v_own = kv_ref[0, rows, LANES:]
        key_pos = (lax.broadcasted_iota(jnp.int32, (1, BLK), 1) + i * BLK).astype(F32)
        outs, lses = [], []
        for hh in range(2):
            head = (lane >> 6) == hh
            slope = sl_ref[2 * p + hh]
            qh = jnp.where(head, qf, 0.0).astype(BF16)
            s = lax.dot_general(qh, k_own, _NT, preferred_element_type=F32)
            s = jnp.where(cc <= rr, s + slope * key_pos, -jnp.inf)
            m = jnp.max(s, axis=1, keepdims=True)
            pr = jnp.exp(s - m)
            l = jnp.sum(pr, axis=1, keepdims=True)
            lses.append(m + jnp.log(l))
            outs.append(jnp.dot(pr.astype(BF16), v_own, preferred_element_type=F32) / l)
        o_ref[0, rows, :] = jnp.where(first, outs[0], outs[1])
        lse_ref[0, rows, :] = jnp.where(first, lses[0], lses[1])


def _moba_merge_kernel(o_ref, lse_ref, pg_ref, out_ref):
    BLK = MOBA_BLOCK
    first = lax.broadcasted_iota(jnp.int32, (BLK, LANES), 1) < MOBA_HEAD_DIM
    for h in range(o_ref.shape[1] // BLK):
        i = pl.program_id(2) * (o_ref.shape[1] // BLK) + h
        rows = pl.ds(h * BLK, BLK)
        lses = [lse_ref[0, rows, :]]
        vals = [o_ref[0, rows, :]]
        for j in range(MOBA_TOPK):
            p0 = pg_ref[j, 0, rows, :]
            p1 = pg_ref[MOBA_TOPK + j, 0, rows, :]
            has_block = j < i
            stats = pltpu.roll(jnp.where(first, p1, p0), MOBA_HEAD_DIM, 1)
            lses.append(jnp.where(has_block, stats, -jnp.inf))
            vals.append(jnp.where(has_block, jnp.where(first, p0, p1), 0.0))
        top = functools.reduce(jnp.maximum, lses)
        ws = [jnp.exp(x - top) for x in lses]
        out_ref[0, rows, :] = sum(w * v for w, v in zip(ws, vals)) / sum(ws)


def _moba_mixer(h3, kv3, q6):
    B, S, _ = h3.shape
    T = B * S
    BLK = MOBA_BLOCK
    TQ = MOBA_TILE
    G = MOBA_TILE_GROUP
    NB = S // BLK
    GI = math.gcd(NB, MOBA_PLACE_GROUP)
    nbp = -(-NB // SUBLANES) * SUBLANES
    NP = MOBA_HEADS // 2
    n_rep = 2 * MOBA_TOPK
    slopes = jnp.asarray(_alibi_slope_list(MOBA_HEADS), F32)

    QB = math.gcd(NB, MOBA_PICK_GROUP)
    selrank, counts = pl.pallas_call(
        functools.partial(_moba_pick_kernel, qblocks=QB),
        grid=(B, NP, NB // QB),
        in_specs=[pl.BlockSpec((1, QB * BLK, LANES), lambda b, p, i: (p, b * (NB // QB) + i, 0)),
                  pl.BlockSpec((1, QB * BLK, LANES), lambda b, p, i: (b, i, p))],
        out_specs=[pl.BlockSpec((1, 1, 2 * SUBLANES, QB * BLK), lambda b, p, i: (b, p, 0, i)),
                   pl.BlockSpec((1, 1, QB * nbp, LANES), lambda b, p, i: (b, p, i, 0))],
        out_shape=[jax.ShapeDtypeStruct((B, NP, 2 * SUBLANES, S), jnp.int32),
                   jax.ShapeDtypeStruct((B, NP, NB * nbp, LANES), F32)],
        scratch_shapes=[pltpu.VMEM((nbp, LANES), F32)],
        compiler_params=_params("parallel", "parallel", "arbitrary"),
        name="moba_pick",
    )(q6, h3)

    OB = math.gcd(NB, MOBA_OWN_GROUP)
    blk_spec = lambda w: pl.BlockSpec((1, OB * BLK, w), lambda b, p, i, sl: (b, i, p))
    own_o, own_lse = pl.pallas_call(
        _moba_own_kernel,
        grid_spec=pltpu.PrefetchScalarGridSpec(
            num_scalar_prefetch=1,
            grid=(B, NP, NB // OB),
            in_specs=[pl.BlockSpec((1, OB * BLK, LANES),
                                   lambda b, p, i, sl: (p, b * (NB // OB) + i, 0)),
                      blk_spec(2 * LANES)],
            out_specs=[blk_spec(LANES), blk_spec(LANES)]),
        out_shape=[jax.ShapeDtypeStruct((B, S, MIX_WIDTH), F32)] * 2,
        compiler_params=_params("parallel", "parallel", "parallel"),
        name="moba_own",
    )(slopes, q6, kv3)

    cnt = counts.reshape(B, NP, NB, nbp, LANES)[:, :, :, :NB, :2].astype(jnp.int32)
    cnt = cnt.transpose(0, 1, 2, 4, 3)
    base = jnp.cumsum(cnt, axis=2) - cnt
    total = jnp.sum(cnt, axis=2)
    padded = (total + TQ - 1) // TQ * TQ
    pend = jnp.cumsum(padded.reshape(-1))
    seg_start = (pend - padded.reshape(-1)).reshape(B, NP, 1, 2, NB)
    table = jnp.zeros((B, NP, NB, SUBLANES, LANES), F32).at[:, :, :, :2, :NB].set(
        (seg_start + base).astype(F32)).reshape(B, NP, NB * SUBLANES, LANES)
    n_seg = B * NP * 2 * NB
    max_tiles = -(-((T * NP * n_rep) // TQ + n_seg) // G) * G
    first_row = jnp.arange(max_tiles, dtype=jnp.int32) * TQ
    tile_seg = jnp.minimum(jnp.sum(pend[None, :] <= first_row[:, None], axis=1), n_seg - 1)
    tile_n = (tile_seg % NB).astype(jnp.int32)
    tile_h = ((tile_seg // NB) % 2).astype(jnp.int32)
    tile_p = ((tile_seg // (2 * NB)) % NP).astype(jnp.int32)
    tile_b = (tile_seg // (2 * NB * NP)).astype(jnp.int32)
    n_used = (pend[-1] // TQ).astype(jnp.int32).reshape(1)
    spare = max_tiles * TQ
    n_rows = spare + max(B * NP * MOBA_TOPK * n_rep * BLK, G * TQ)

    idx = pl.pallas_call(
        functools.partial(_moba_place_kernel, group=GI, spare=spare),
        grid=(B, NP, NB // GI),
        in_specs=[pl.BlockSpec((1, 1, 2 * SUBLANES, GI * BLK), lambda b, p, i: (b, p, 0, i)),
                  pl.BlockSpec((1, 1, GI * SUBLANES, LANES), lambda b, p, i: (b, p, i, 0))],
        out_specs=pl.BlockSpec((SC_IDX_ROWS, GI * BLK),
                               lambda b, p, i: (0, p * (T // (GI * BLK)) + b * (NB // GI) + i)),
        out_shape=jax.ShapeDtypeStruct((SC_IDX_ROWS, NP * T), jnp.int32),
        compiler_params=_params("parallel", "parallel", "parallel"),
        name="moba_place",
    )(selrank, table)

    qs = _sc_scatter_rows(q6.reshape(NP * T, LANES), idx, n_rep, n_rows)

    live = lambda t, tb, tp, th, tn, nu, sl: (jnp.where(t * G < nu[0], t, max_tiles // G), 0)
    kv_blk = lambda g: pl.BlockSpec(
        (1, BLK, 2 * LANES),
        lambda t, tb, tp, th, tn, nu, sl: (tb[t * G + g], tn[t * G + g], tp[t * G + g]))
    part = pl.pallas_call(
        functools.partial(_moba_tiles_kernel, group=G),
        grid_spec=pltpu.PrefetchScalarGridSpec(
            num_scalar_prefetch=6,
            grid=(max_tiles // G,),
            in_specs=[pl.BlockSpec((G * TQ, LANES), live)] + [kv_blk(g) for g in range(G)],
            out_specs=pl.BlockSpec((G * TQ, LANES), live)),
        out_shape=jax.ShapeDtypeStruct((n_rows, LANES), F32),
        compiler_params=_params("arbitrary"),
        name="moba_tiles",
    )(tile_b, tile_p, tile_h, tile_n, n_used, slopes, qs, *([kv3] * G))

    pg = _sc_gather_rows(part, idx[:n_rep].reshape(-1)).reshape(n_rep, NP, T, LANES)

    MB = math.gcd(NB, MOBA_MERGE_GROUP)
    blk = pl.BlockSpec((1, MB * BLK, LANES), lambda b, p, i: (b, i, p))
    return pl.pallas_call(
        _moba_merge_kernel,
        grid=(B, NP, NB // MB),
        in_specs=[blk, blk,
                  pl.BlockSpec((n_rep, 1, MB * BLK, LANES),
                               lambda b, p, i: (0, p, b * (NB // MB) + i, 0))],
        out_specs=blk,
        out_shape=jax.ShapeDtypeStruct((B, S, MIX_WIDTH), F32),
        compiler_params=_params("parallel", "parallel", "parallel"),
        name="moba_merge",
    )(own_o, own_lse, pg)


def _memkv_kernel(mem_ref, w_ref, kv_ref):
    kv_ref[0] = jnp.dot(mem_ref[0].astype(BF16), w_ref[...].astype(BF16),
                        preferred_element_type=F32).astype(BF16)


def _memkv(mem, w_kv):
    B, M, D = mem.shape
    N = w_kv.shape[1]
    return pl.pallas_call(
        _memkv_kernel,
        grid=(B,),
        in_specs=[pl.BlockSpec((1, M, D), lambda b: (b, 0, 0)),
                  pl.BlockSpec((D, N), lambda b: (0, 0))],
        out_specs=pl.BlockSpec((1, M, N), lambda b: (b, 0, 0)),
        out_shape=jax.ShapeDtypeStruct((B, M, N), BF16),
        compiler_params=_params("parallel"),
        name="memkv",
    )(mem, w_kv)


def _layer_norm(z, g, b):
    mu = jnp.mean(z, axis=-1, keepdims=True)
    zc = z - mu
    var = jnp.mean(zc * zc, axis=-1, keepdims=True)
    return zc * lax.rsqrt(var + LN_EPS) * g + b


def _post_kernel(x_ref, mix_ref, mq_ref, kv_ref, wo_ref, g_ref, b_ref, wr_ref, br_ref,
                 x1_ref, x1s_ref, idx_ref, gate_ref, rank_ref, cnt_ref, run_ref, *, alpha):
    tm = x_ref.shape[0]
    mq = mq_ref[...]
    kv = kv_ref[0]
    km = kv[:, :MEM_WIDTH]
    vm = kv[:, MEM_WIDTH:]
    lane = lax.broadcasted_iota(jnp.int32, (tm, MEM_WIDTH), 1)
    scale = MEM_HEAD_DIM ** -0.5
    mo = jnp.zeros((tm, MEM_WIDTH), F32)
    for hd in range(MEM_HEADS):
        head = (lane >> 6) == hd
        qh = jnp.where(head, mq * scale, 0.0).astype(BF16)
        s = lax.dot_general(qh, km, _NT, preferred_element_type=F32)
        m = jnp.max(s, axis=1, keepdims=True)
        p = jnp.exp(s - m)
        l = jnp.sum(p, axis=1, keepdims=True)
        oh = jnp.dot(p.astype(BF16), vm, preferred_element_type=F32) / l
        mo = jnp.where(head, oh, mo)

    y = jnp.dot(mix_ref[...].astype(BF16), wo_ref[:MIX_WIDTH, :], preferred_element_type=F32)
    y = y + jnp.dot(mo.astype(BF16), wo_ref[MIX_WIDTH:, :], preferred_element_type=F32)
    x1 = _layer_norm(alpha * x_ref[...] + y, g_ref[...], b_ref[...])
    x1_ref[...] = x1
    _store_subrows(x1s_ref, x1)

    x_hi = x1.astype(BF16)
    x_lo = (x1 - x_hi.astype(F32)).astype(BF16)
    hi = jnp.dot(x_hi, wr_ref[...], preferred_element_type=F32)
    lo = jnp.dot(x_lo, wr_ref[:, :LANES], preferred_element_type=F32)
    logits = hi[:, :LANES] + hi[:, LANES:] + lo
    g = logits.T[:N_EXPERTS] + br_ref[...]
    erow = lax.broadcasted_iota(jnp.int32, (N_EXPERTS, tm), 0)
    orow = lax.broadcasted_iota(jnp.int32, (SUBLANES, tm), 0)
    idx_out = jnp.zeros((SUBLANES, tm), jnp.int32)
    vals, picks = [], []
    chosen = jnp.zeros((N_EXPERTS, tm), F32)
    for kk in range(TOP_K):
        mx = jnp.max(g, axis=0, keepdims=True)
        idx = jnp.min(jnp.where(g == mx, erow, N_EXPERTS), axis=0, keepdims=True)
        idx_out = jnp.where(orow == kk, idx, idx_out)
        vals.append(mx)
        picks.append(idx)
        chosen = chosen + jnp.where(erow == idx, 1.0, 0.0)
        g = jnp.where(erow == idx, -jnp.inf, g)
    evs = [jnp.exp(v - vals[0]) for v in vals]
    den = sum(evs)
    gate_out = jnp.zeros((SUBLANES, tm), F32)
    for kk in range(TOP_K):
        gate_out = jnp.where(orow == kk, evs[kk] / den, gate_out)
    idx_ref[...] = idx_out
    gate_ref[...] = gate_out

    @pl.when(pl.program_id(0) == 0)
    def _():
        run_ref[...] = jnp.zeros_like(run_ref)

    ta = lax.broadcasted_iota(jnp.int32, (tm, tm), 0)
    tb = lax.broadcasted_iota(jnp.int32, (tm, tm), 1)
    before = jnp.where(ta < tb, 1.0, 0.0).astype(BF16)
    earlier = jnp.dot(chosen.astype(BF16), before, preferred_element_type=F32) + run_ref[:, :1]
    rank_out = jnp.zeros((SUBLANES, tm), jnp.int32)
    for kk in range(TOP_K):
        rank = jnp.sum(jnp.where(erow == picks[kk], earlier, 0.0), axis=0, keepdims=True)
        rank_out = jnp.where(orow == kk, rank.astype(jnp.int32), rank_out)
    rank_ref[...] = rank_out
    run_ref[...] = run_ref[...] + jnp.sum(chosen, axis=1, keepdims=True)
    cnt_ref[...] = run_ref[...]


def _post_mixer(x2, mix2, h2, kv, w_o_bf16, ln_g, ln_b, w_router, b_router, seq_len, alpha,
                part, n_parts, local_mix):
    D = x2.shape[1]
    T = x2.shape[0] // n_parts
    tm = POST_ROWS
    N = h2.shape[1]
    M = kv.shape[1]
    mq_col = (N - MEM_WIDTH) // MEM_WIDTH
    tiles_per_seq = seq_len // tm
    off = part * (T // tm)
    mix_off = 0 if local_mix else off
    wr = jnp.zeros((D, LANES), F32).at[:, :N_EXPERTS].set(w_router)
    wr_hi = wr.astype(BF16)
    wr = jnp.concatenate([wr_hi, (wr - wr_hi.astype(F32)).astype(BF16)], axis=1)
    br = b_router.reshape(N_EXPERTS, 1)
    row_in = lambda n: pl.BlockSpec((tm, n), lambda i: (i + off, 0))
    row = lambda n: pl.BlockSpec((tm, n), lambda i: (i, 0))
    full = lambda a, b: pl.BlockSpec((a, b), lambda i: (0, 0))
    per_token = pl.BlockSpec((SUBLANES, tm), lambda i: (0, i))
    return pl.pallas_call(
        functools.partial(_post_kernel, alpha=alpha),
        grid=(T // tm,),
        in_specs=[row_in(D), pl.BlockSpec((tm, MIX_WIDTH), lambda i: (i + mix_off, 0)),
                  pl.BlockSpec((tm, MEM_WIDTH), lambda i: (i + mix_off, mq_col)),
                  pl.BlockSpec((1, M, 2 * MEM_WIDTH), lambda i: ((i + off) // tiles_per_seq, 0, 0)),
                  full(D, D), full(1, D), full(1, D),
                  full(D, 2 * LANES), full(N_EXPERTS, 1)],
        out_specs=[row(D), pl.BlockSpec((tm * (D // LANES), LANES), lambda i: (i, 0)),
                   per_token, per_token, per_token, full(N_EXPERTS, LANES)],
        out_shape=[jax.ShapeDtypeStruct((T, D), F32),
                   jax.ShapeDtypeStruct((T * (D // LANES), LANES), F32),
                   jax.ShapeDtypeStruct((SUBLANES, T), jnp.int32),
                   jax.ShapeDtypeStruct((SUBLANES, T), F32),
                   jax.ShapeDtypeStruct((SUBLANES, T), jnp.int32),
                   jax.ShapeDtypeStruct((N_EXPERTS, LANES), F32)],
        scratch_shapes=[pltpu.VMEM((N_EXPERTS, LANES), F32)],
        compiler_params=_params("arbitrary"),
        name="post_mixer",
    )(x2, mix2, h2, kv, w_o_bf16, ln_g.reshape(1, D), ln_b.reshape(1, D), wr, br)


def _sc_mesh():
    return plsc.VectorSubcoreMesh(core_axis_name="core", subcore_axis_name="subcore")


def _sc_scatter_rows(rows, idx, n_rep, n_out):
    R, W = rows.shape

    @functools.partial(pl.kernel, out_type=jax.ShapeDtypeStruct((n_out, W), rows.dtype),
                       mesh=_sc_mesh(), scratch_types=[])
    def scatter(x_hbm, i_hbm, o_hbm):
        def body(x_vmem, i_vmem):
            for r in range(n_rep):
                pltpu.sync_copy(x_vmem, o_hbm.at[i_vmem.at[r]])

        pltpu.emit_pipeline(
            body, grid=(R // SC_WINDOW,),
            in_specs=[pl.BlockSpec((SC_WINDOW, W), lambda i: (i, 0)),
                      pl.BlockSpec((SC_IDX_ROWS, SC_WINDOW), lambda i: (0, i))],
            out_specs=[], core_axis_name=("core", "subcore"),
            dimension_semantics=(pltpu.PARALLEL,), trace_scopes=False)(x_hbm, i_hbm)

    return scatter(rows, idx)


def _sc_gather_rows(table, idx):
    n = idx.shape[0]
    W = table.shape[1]

    @functools.partial(pl.kernel, out_type=jax.ShapeDtypeStruct((n, W), table.dtype),
                       mesh=_sc_mesh(), scratch_types=[])
    def gather(t_hbm, i_hbm, o_hbm):
        def body(i_vmem, o_vmem):
            pltpu.sync_copy(t_hbm.at[i_vmem.at[0]], o_vmem)

        pltpu.emit_pipeline(
            body, grid=(n // SC_WINDOW,),
            in_specs=[pl.BlockSpec((1, SC_WINDOW), lambda i: (0, i))],
            out_specs=[pl.BlockSpec((SC_WINDOW, W), lambda i: (i, 0))],
            core_axis_name=("core", "subcore"),
            dimension_semantics=(pltpu.PARALLEL,), trace_scopes=False)(i_hbm, o_hbm)

    return gather(table, idx.reshape(1, n))


def _sc_workers():
    info = pltpu.get_tpu_info().sparse_core
    return info.num_cores, info.num_cores * info.num_subcores


def _sc_scatter_slabs(rows, idx, n_rep, n_out):
    R, S, W = rows.shape
    n_cores, n_workers = _sc_workers()
    per_worker = (R // SC_WINDOW) // n_workers
    assert per_worker * n_workers * SC_WINDOW == R

    @functools.partial(pl.kernel, out_type=jax.ShapeDtypeStruct((n_out, S, W), rows.dtype),
                       mesh=_sc_mesh(),
                       scratch_types=[pltpu.VMEM((SC_IDX_ROWS, SC_WINDOW), jnp.int32),
                                      pltpu.VMEM((SC_CHUNK, S, W), rows.dtype)])
    def scatter(x_hbm, i_hbm, o_hbm, ibuf, buf):
        wid = lax.axis_index("subcore") * n_cores + lax.axis_index("core")

        @pl.loop(0, per_worker)
        def _(s):
            first = (wid * per_worker + s) * SC_WINDOW
            pltpu.sync_copy(i_hbm.at[:, pl.ds(first, SC_WINDOW)], ibuf)
            for c in range(SC_WINDOW // SC_CHUNK):
                pltpu.sync_copy(x_hbm.at[pl.ds(first + c * SC_CHUNK, SC_CHUNK)], buf)
                for r in range(n_rep):
                    pltpu.sync_copy(buf, o_hbm.at[ibuf.at[r, pl.ds(c * SC_CHUNK, SC_CHUNK)]])

    return scatter(rows, idx)


def _sc_gather_slabs(table, idx):
    n = idx.shape[0]
    S, W = table.shape[1:]
    n_cores, n_workers = _sc_workers()
    per_worker = (n // SC_WINDOW) // n_workers
    assert per_worker * n_workers * SC_WINDOW == n
    n_chunks = SC_WINDOW // SC_CHUNK

    @functools.partial(pl.kernel, out_type=jax.ShapeDtypeStruct((n, S, W), table.dtype),
                       mesh=_sc_mesh(),
                       scratch_types=[pltpu.VMEM((1, SC_WINDOW), jnp.int32),
                                      pltpu.VMEM((2, SC_CHUNK, S, W), table.dtype),
                                      pltpu.SemaphoreType.DMA((2,)), pltpu.SemaphoreType.DMA((2,))])
    def gather(t_hbm, i_hbm, o_hbm, ibuf, buf, fetch_sem, store_sem):
        wid = lax.axis_index("subcore") * n_cores + lax.axis_index("core")

        @pl.loop(0, per_worker)
        def _(s):
            blk = wid * per_worker + s
            pltpu.sync_copy(i_hbm.at[pl.ds(blk, 1)], ibuf)

            def fetch(c):
                return pltpu.make_async_copy(
                    t_hbm.at[ibuf.at[0, pl.ds(c * SC_CHUNK, SC_CHUNK)]], buf.at[c % 2],
                    fetch_sem.at[c % 2])

            def store(c):
                return pltpu.make_async_copy(
                    buf.at[c % 2], o_hbm.at[pl.ds(blk * SC_WINDOW + c * SC_CHUNK, SC_CHUNK)],
                    store_sem.at[c % 2])

            fetch(0).start()
            for c in range(n_chunks):
                if c + 1 < n_chunks:
                    if c >= 1:
                        store(c - 1).wait()
                    fetch(c + 1).start()
                fetch(c).wait()
                store(c).start()
            store(n_chunks - 2).wait()
            store(n_chunks - 1).wait()

    return gather(table, idx.reshape(n // SC_WINDOW, SC_WINDOW))


def _route(top_idx, rank, counts):
    rb = MOE_ROWS
    n_tokens = top_idx.shape[1]
    tk = n_tokens * TOP_K
    padded = (counts + rb - 1) // rb * rb
    pend = jnp.cumsum(padded)
    pstart = pend - padded
    experts = jnp.arange(N_EXPERTS, dtype=jnp.int32)
    start = jnp.sum(jnp.where(top_idx[:TOP_K, :, None] == experts, pstart, 0), axis=2)
    dest = (start + rank[:TOP_K]).astype(jnp.int32)
    n_blocks = tk // rb + N_EXPERTS
    first_row = jnp.arange(n_blocks, dtype=jnp.int32) * rb
    block_e = jnp.minimum(jnp.sum(pend[None, :] <= first_row[:, None], axis=1),
                          N_EXPERTS - 1).astype(jnp.int32)
    n_used = (pend[-1] // rb).astype(jnp.int32).reshape(1)
    return dest, block_e, n_used


def _dispatch(x1s, dest, n_rows, sub):
    T = dest.shape[1]
    idx = jnp.concatenate([dest, jnp.zeros((SC_IDX_ROWS - TOP_K, T), jnp.int32)], axis=0)
    xs = _sc_scatter_slabs(x1s.reshape(T, sub, LANES), idx, TOP_K, n_rows)
    return xs.reshape(n_rows * sub, LANES)


def _expert_kernel(be_ref, nu_ref, x_ref, wg_ref, bg_ref, wu_ref, bu_ref, wd_ref, bd_ref,
                   y_ref, wgb, wub, wdb):
    i = pl.program_id(0)
    prev = be_ref[jnp.maximum(i - 1, 0)]

    @pl.when((i == 0) | (be_ref[i] != prev))
    def _():
        wgb[...] = wg_ref[0, 0].astype(BF16)
        wub[...] = wu_ref[0, 0].astype(BF16)
        wdb[...] = wd_ref[0, 0].astype(BF16)

    @pl.when(i < nu_ref[0])
    def _():
        sub = wgb.shape[0] // LANES
        xb = _load_subrows(x_ref, x_ref.shape[0] // sub, sub).astype(BF16)
        gate = jnp.dot(xb, wgb[...], preferred_element_type=F32) + bg_ref[0, 0]
        gate = jnp.minimum(gate, SWIGLU_LIMIT)
        up = jnp.dot(xb, wub[...], preferred_element_type=F32) + bu_ref[0, 0]
        up = jnp.clip(up, -SWIGLU_LIMIT, SWIGLU_LIMIT)
        hid = gate * _sigmoid(SWIGLU_ALPHA * gate) * (up + 1.0)
        y = jnp.dot(hid.astype(BF16), wdb[...], preferred_element_type=F32) + bd_ref[0, 0]
        _store_subrows(y_ref, y)


def _experts(xs, block_e, n_used, layer, w_gate, b_gate, w_up, b_up, w_down, b_down):
    rb = MOE_ROWS
    n_blocks = block_e.shape[0]
    E, D, F = w_gate.shape[1:]
    sub = D // LANES
    wspec = lambda a, b: pl.BlockSpec((1, 1, a, b), lambda i, be, nu: (layer, be[i], 0, 0))
    live = lambda i, be, nu: (jnp.where(i < nu[0], i, n_blocks), 0)
    grid_spec = pltpu.PrefetchScalarGridSpec(
        num_scalar_prefetch=2,
        grid=(n_blocks,),
        in_specs=[pl.BlockSpec((rb * sub, LANES), live),
                  wspec(D, F), wspec(1, F), wspec(D, F), wspec(1, F), wspec(F, D), wspec(1, D)],
        out_specs=pl.BlockSpec((rb * sub, LANES), live),
        scratch_shapes=[pltpu.VMEM((D, F), BF16), pltpu.VMEM((D, F), BF16),
                        pltpu.VMEM((F, D), BF16)],
    )
    depth = w_gate.shape[0]
    return pl.pallas_call(
        _expert_kernel,
        grid_spec=grid_spec,
        out_shape=jax.ShapeDtypeStruct(xs.shape, F32),
        compiler_params=_params("arbitrary"),
        name="experts",
    )(block_e, n_used, xs, w_gate, b_gate.reshape(depth, E, 1, F),
      w_up, b_up.reshape(depth, E, 1, F), w_down, b_down.reshape(depth, E, 1, D))


def _combine_kernel(x1_ref, gate_ref, y_ref, g_ref, b_ref, *rest, alpha):
    o_ref = rest[-1]
    tm, D = x1_ref.shape
    sub = D // LANES
    gates = gate_ref[...].T
    f = jnp.zeros(x1_ref.shape, F32)
    for kk in range(TOP_K):
        f = f + gates[:, kk:kk + 1] * _load_subrows(y_ref, tm, sub, kk * sub, TOP_K * sub)
    o_ref[...] = _layer_norm(alpha * x1_ref[...] + f, g_ref[...], b_ref[...])


def _combine(x1, gates, dest, y_rows, ln_g, ln_b, alpha, part, n_parts, earlier):
    T, D = x1.shape
    sub = D // LANES
    tm = COMBINE_ROWS
    off = part * (T // tm)
    yg = _sc_gather_slabs(y_rows.reshape(-1, sub, LANES), dest.T.reshape(-1)).reshape(-1, LANES)
    row = lambda n: pl.BlockSpec((tm, n), lambda i: (i, 0))
    full = lambda a, b: pl.BlockSpec((a, b), lambda i: (0, 0))
    in_specs = [row(D), pl.BlockSpec((SUBLANES, tm), lambda i: (0, i)),
                pl.BlockSpec((tm * TOP_K * sub, LANES), lambda i: (i, 0)),
                full(1, D), full(1, D)]
    operands = [x1, gates, yg, ln_g.reshape(1, D), ln_b.reshape(1, D)]
    aliases = {}
    if earlier is not None:
        in_specs.append(pl.BlockSpec(memory_space=pl.ANY))
        operands.append(earlier)
        aliases = {len(operands) - 1: 0}
    return pl.pallas_call(
        functools.partial(_combine_kernel, alpha=alpha),
        grid=(T // tm,),
        in_specs=in_specs,
        out_specs=pl.BlockSpec((tm, D), lambda i: (i + off, 0)),
        out_shape=jax.ShapeDtypeStruct((n_parts * T, D), F32),
        input_output_aliases=aliases,
        compiler_params=_params("parallel"),
        name="combine",
    )(*operands)


def kernel(x, mem, w_in_hgrn, hgrn_lb_logits, hgrn_norm_g, w_in_moba, w_mem_kv, w_o,
           ln_mix_g, ln_mix_b, w_router, b_router, w_gate, b_gate, w_up, b_up,
           w_down, b_down, ln_ffn_g, ln_ffn_b):
    B, S, D = x.shape
    T = B * S
    depth = w_o.shape[0]
    alpha = (2 * depth) ** 0.25

    p_lb = jax.nn.softmax(hgrn_lb_logits.astype(F32), axis=0)
    lower_bounds = jnp.cumsum(p_lb, axis=0) - p_lb[0]

    x2 = x.reshape(T, D)
    for layer in range(depth):
        j = layer // 2
        if layer % 2 == 0:
            (h2,) = _inproj(x2, w_in_hgrn[j].astype(BF16), for_moba=False)
            mix2 = _hgrn_mixer(h2.reshape(B, S, -1), lower_bounds[j], hgrn_norm_g[j])
            mixed = [(mix2.reshape(T, MIX_WIDTH), h2)] * B
        else:
            w_in = w_in_moba[j].astype(BF16)
            mixed = []
            for b in range(B):
                h2, kv2, q6 = _inproj(x2, w_in, for_moba=True, part=b, n_parts=B)
                mix2 = _moba_mixer(h2.reshape(1, S, -1), kv2.reshape(1, S, -1), q6)
                mixed.append((mix2.reshape(S, MIX_WIDTH), h2))
        kv = _memkv(mem, w_mem_kv[layer])
        w_out = w_o[layer].astype(BF16)
        x_next = None
        for part in range(B):
            mix2, h2 = mixed[part]
            x1, x1s, top_idx, gates, rank, counts = _post_mixer(
                x2, mix2, h2, kv, w_out, ln_mix_g[layer], ln_mix_b[layer],
                w_router[layer], b_router[layer], S, alpha, part, B, layer % 2 == 1)
            dest, block_e, n_used = _route(top_idx, rank, counts[:, 0].astype(jnp.int32))
            xs = _dispatch(x1s, dest, (block_e.shape[0] + 1) * MOE_ROWS, D // LANES)
            y_rows = _experts(xs, block_e, n_used, layer,
                              w_gate, b_gate, w_up, b_up, w_down, b_down)
            x_next = _combine(x1, gates, dest, y_rows, ln_ffn_g[layer], ln_ffn_b[layer], alpha,
                              part, B, x_next)
        x2 = x_next
    return x2.reshape(B, S, D)
```
